```python
import math
import numpy as np
import jax, jax.numpy as jnp
from jax import lax

D_MODEL = 1024
BATCH = 8
SEQ = 2048
DEPTH = 2

GRID_W = 64
CTX_LEN = 256
CHUNK = 64
CONV_K = 5
EPS = 1e-6

SSD_HEADS = 16
SSD_HEAD_DIM = 64
SSD_WIDTH = SSD_HEADS * SSD_HEAD_DIM
SSD_GROUPS = 2
SSD_STATE = 64
SSD_BC = SSD_GROUPS * SSD_STATE
SSD_XBC = SSD_WIDTH + 2 * SSD_BC
DT_MIN = 1e-3
DT_MAX = 1e-1
ML_HEADS = 4
ML_QK_DIM = 128
ML_V_DIM = 256
ML_QK_WIDTH = ML_HEADS * ML_QK_DIM
ML_V_WIDTH = ML_HEADS * ML_V_DIM
GLA_HEADS = 4
GLA_K_DIM = 128
GLA_V_DIM = 256
GLA_K_WIDTH = GLA_HEADS * GLA_K_DIM
GLA_V_WIDTH = GLA_HEADS * GLA_V_DIM
GLA_RANK = 16
GLA_TAU = 16.0
D_FF = ((-(-8 * D_MODEL // 3) + 255) // 256) * 256

IN_SPLITS = (
    SSD_WIDTH, SSD_WIDTH, SSD_BC, SSD_BC, SSD_HEADS, SSD_HEADS,
    ML_QK_WIDTH, ML_QK_WIDTH, ML_V_WIDTH, ML_V_WIDTH,
    ML_HEADS, ML_HEADS, ML_HEADS, ML_HEADS,
    GLA_K_WIDTH, GLA_K_WIDTH, GLA_V_WIDTH, GLA_V_WIDTH, GLA_RANK, GLA_RANK,
    D_MODEL, D_MODEL, D_MODEL,
)
D_IN = sum(IN_SPLITS)

kernel_name = 'hybrid_ssd_mlstm_gla_prefix_block'


def _rmsnorm(x, w, groups=1):
    shape = x.shape
    xf = x.astype(jnp.float32).reshape(*shape[:-1], groups, shape[-1] // groups)
    xf = xf * lax.rsqrt(jnp.mean(xf * xf, axis=-1, keepdims=True) + EPS)
    return (xf.reshape(shape) * w.astype(jnp.float32)).astype(x.dtype)


def _dwconv(u, w, b):
    y = lax.conv_general_dilated(u, w[:, None, :], window_strides=(1,), padding='SAME',
                                 dimension_numbers=('NWC', 'WIO', 'NWC'),
                                 feature_group_count=u.shape[-1])
    return y + b


def _conv_two(u, n_ctx, w, b):
    return jnp.concatenate([_dwconv(u[:, :n_ctx], w, b), _dwconv(u[:, n_ctx:], w, b)], axis=1)


def _take(t, idx):
    return jnp.take(t, idx, axis=1)


def _to_chunks(t):
    bsz, T = t.shape[:2]
    return jnp.moveaxis(t.reshape(bsz, T // CHUNK, CHUNK, *t.shape[2:]), 1, 0)


def _from_chunks(y):
    nc, bsz, L = y.shape[:3]
    return jnp.moveaxis(y, 0, 1).reshape(bsz, nc * L, *y.shape[3:])


def _lower_tri():
    return jnp.tril(jnp.ones((CHUNK, CHUNK), dtype=bool))


def _ssd_scan(xh, dt, bm, cm, a_log):
    dtype = xh.dtype
    f32 = jnp.float32
    bsz, T, H, P = xh.shape
    la = dt.astype(f32) * -jnp.exp(a_log.astype(f32))
    xs = tuple(_to_chunks(t) for t in ((xh * dt[..., None]).astype(f32), la, bm.astype(f32), cm.astype(f32)))
    mask = _lower_tri()

    def step(S, inp):
        xc, lac, bc, cc = inp
        cum = jnp.cumsum(lac, axis=1)
        seg = jnp.where(mask[None, :, :, None], cum[:, :, None] - cum[:, None], -jnp.inf)
        scores = jnp.einsum('bthn,bshn->btsh', cc, bc) * jnp.exp(seg)
        y = (jnp.einsum('btsh,bshp->bthp', scores, xc)
             + jnp.exp(cum)[..., None] * jnp.einsum('bthn,bhpn->bthp', cc, S))
        last = cum[:, -1]
        S = (jnp.exp(last)[..., None, None] * S
             + jnp.einsum('bsh,bshp,bshn->bhpn', jnp.exp(last[:, None] - cum), xc, bc))
        return S, y

    S0 = jnp.zeros((bsz, H, P, bm.shape[-1]), f32)
    _, ys = lax.scan(step, S0, xs)
    return _from_chunks(ys).astype(dtype)


def _mlstm_scan(q, k, v, li, lf):
    dtype = v.dtype
    f32 = jnp.float32
    bsz, T, H, dk = q.shape
    dv = v.shape[-1]
    xs = tuple(_to_chunks(t.astype(f32)) for t in (q, k, v, li, lf))
    mask = _lower_tri()

    def step(carry, inp):
        Cs, ns, m = carry
        qc, kc, vc, lic, lfc = inp
        b = jnp.cumsum(lfc, axis=1)
        seg = jnp.where(mask[None, :, :, None], b[:, :, None] - b[:, None] + lic[:, None], -jnp.inf)
        inter = m[:, None, :] + b
        m_t = jnp.maximum(inter, jnp.max(seg, axis=2))
        w = jnp.exp(seg - m_t[:, :, None, :])
        w_inter = jnp.exp(inter - m_t)
        qk = jnp.einsum('bthd,bshd->btsh', qc, kc) * w
        num = (jnp.einsum('btsh,bshv->bthv', qk, vc)
               + w_inter[..., None] * jnp.einsum('bhvd,bthd->bthv', Cs, qc))
        den = jnp.sum(qk, axis=2) + w_inter * jnp.einsum('bhd,bthd->bth', ns, qc)
        h = num / jnp.maximum(jnp.abs(den), jnp.exp(-m_t))[..., None]
        m_new = m_t[:, -1]
        ws = jnp.exp(b[:, -1:] - b + lic - m_new[:, None])
        keep = jnp.exp(m + b[:, -1] - m_new)
        Cs = keep[..., None, None] * Cs + jnp.einsum('bsh,bshv,bshd->bhvd', ws, vc, kc)
        ns = keep[..., None] * ns + jnp.einsum('bsh,bshd->bhd', ws, kc)
        return (Cs, ns, m_new), h

    init = (jnp.zeros((bsz, H, dv, dk), f32), jnp.zeros((bsz, H, dk), f32), jnp.zeros((bsz, H), f32))
    _, hs = lax.scan(step, init, xs)
    return _from_chunks(hs).astype(dtype)


def _gla_scan(q, k, v, lg):
    dtype = v.dtype
    f32 = jnp.float32
    bsz, T, H, dk = q.shape
    dv = v.shape[-1]
    xs = tuple(_to_chunks(t.astype(f32)) for t in (q, k, v, lg))
    mask = _lower_tri()

    def step(S, inp):
        qc, kc, vc, gc = inp
        b = jnp.cumsum(gc, axis=1)
        seg = jnp.where(mask[None, :, :, None, None], b[:, :, None] - b[:, None], -jnp.inf)
        att = jnp.einsum('bthk,bshk,btshk->btsh', qc, kc, jnp.exp(seg))
        o = (jnp.einsum('btsh,bshv->bthv', att, vc)
             + jnp.einsum('bthk,bhkv->bthv', qc * jnp.exp(b), S))
        last = b[:, -1]
        S = (jnp.exp(last)[..., None] * S
             + jnp.einsum('bshk,bshv->bhkv', kc * jnp.exp(last[:, None] - b), vc))
        return S, o

    _, os_ = lax.scan(step, jnp.zeros((bsz, H, dk, dv), f32), xs)
    return _from_chunks(os_).astype(dtype)


def _mixer(h, n_ctx, keep_ctx, orders, p):
    rev, col_f, col_f_inv, col_b, col_b_inv = orders
    bsz, T, _ = h.shape
    (s_x, s_z, s_b, s_c, s_dtf, s_dtb,
     m_q, m_k, m_v, m_o, m_if, m_ib, m_ff, m_fb,
     g_q, g_k, g_v, g_g, g_af, g_ab,
     gate_ssd, gate_ml, gate_gla) = jnp.split(h @ p['w_in'], np.cumsum(IN_SPLITS)[:-1], axis=-1)

    xbc = jax.nn.silu(_conv_two(jnp.concatenate([s_x, s_b, s_c], axis=-1), n_ctx, p['ssd_conv_w'], p['ssd_conv_b']))
    s_x, s_b, s_c = jnp.split(xbc, [SSD_WIDTH, SSD_WIDTH + SSD_BC], axis=-1)
    xh = s_x.reshape(bsz, T, SSD_HEADS, SSD_HEAD_DIM)
    rep = SSD_HEADS // SSD_GROUPS
    bm = jnp.repeat(s_b.reshape(bsz, T, SSD_GROUPS, SSD_STATE), rep, axis=2)
    cm = jnp.repeat(s_c.reshape(bsz, T, SSD_GROUPS, SSD_STATE), rep, axis=2)
    dt_f = jax.nn.softplus(s_dtf + p['ssd_dt_bias'][0])
    dt_b = jax.nn.softplus(s_dtb + p['ssd_dt_bias'][1])
    y = (_ssd_scan(xh, dt_f, bm, cm, p['ssd_a_log'][0])
         + _take(_ssd_scan(_take(xh, rev), _take(dt_b, rev), _take(bm, rev), _take(cm, rev), p['ssd_a_log'][1]), rev)
         + p['ssd_d'][:, None] * xh)
    y_ssd = _rmsnorm(y.reshape(bsz, T, SSD_WIDTH) * jax.nn.silu(s_z), p['ssd_norm_w'], SSD_GROUPS)

    qk = jax.nn.silu(_conv_two(jnp.concatenate([m_q, m_k], axis=-1), n_ctx, p['ml_conv_w'], p['ml_conv_b']))
    q = qk[..., :ML_QK_WIDTH].reshape(bsz, T, ML_HEADS, ML_QK_DIM)
    k = qk[..., ML_QK_WIDTH:].reshape(bsz, T, ML_HEADS, ML_QK_DIM) * ML_QK_DIM ** -0.5
    v = m_v.reshape(bsz, T, ML_HEADS, ML_V_DIM)
    li_f = m_if + p['ml_i_bias'][0]
    li_b = m_ib + p['ml_i_bias'][1]
    lf_f = jax.nn.log_sigmoid(m_ff + p['ml_f_bias'][0])
    lf_b = jax.nn.log_sigmoid(m_fb + p['ml_f_bias'][1])
    hm = (_mlstm_scan(q, k, v, li_f, lf_f)
          + _take(_mlstm_scan(_take(q, rev), _take(k, rev), _take(v, rev), _take(li_b, rev), _take(lf_b, rev)), rev))
    y_ml = _rmsnorm(hm.reshape(bsz, T, ML_V_WIDTH), p['ml_norm_w'], ML_HEADS) * jax.nn.sigmoid(m_o)

    gq = g_q.reshape(bsz, T, GLA_HEADS, GLA_K_DIM) * GLA_K_DIM ** -0.5
    gk = g_k.reshape(bsz, T, GLA_HEADS, GLA_K_DIM)
    gv = g_v.reshape(bsz, T, GLA_HEADS, GLA_V_DIM)
    lg_f = (jax.nn.log_sigmoid(g_af @ p['gla_a_up'][0] + p['gla_a_bias'][0]) / GLA_TAU).reshape(bsz, T, GLA_HEADS, GLA_K_DIM)
    lg_b = (jax.nn.log_sigmoid(g_ab @ p['gla_a_up'][1] + p['gla_a_bias'][1]) / GLA_TAU).reshape(bsz, T, GLA_HEADS, GLA_K_DIM)
    o = (_take(_gla_scan(_take(gq, col_f), _take(gk, col_f), _take(gv, col_f), _take(lg_f, col_f)), col_f_inv)
         + _take(_gla_scan(_take(gq, col_b), _take(gk, col_b), _take(gv, col_b), _take(lg_b, col_b)), col_b_inv))
    y_gla = _rmsnorm(o.reshape(bsz, T, GLA_V_WIDTH), p['gla_norm_w'], GLA_HEADS) * jax.nn.silu(g_g)

    st = 0 if keep_ctx else n_ctx
    merged = (jax.nn.sigmoid(gate_ssd[:, st:]) * (y_ssd[:, st:] @ p['w_b_ssd'])
              + jax.nn.sigmoid(gate_ml[:, st:]) * (y_ml[:, st:] @ p['w_b_ml'])
              + jax.nn.sigmoid(gate_gla[:, st:]) * (y_gla[:, st:] @ p['w_b_gla']))
    return merged @ p['w_out']


def _swiglu(h, w_in, w_out):
    g, u = jnp.split(h @ w_in, 2, axis=-1)
    return (jax.nn.silu(g) * u) @ w_out


def setup_inputs(seed: int = 0) -> dict:
    key = jax.random.key(seed)
    ks = jax.random.split(key, 32)
    f32 = jnp.float32
    L = DEPTH

    def nrm(k, shape, scale):
        return jax.random.normal(k, shape, f32) * scale

    def gain(k, shape):
        return 1.0 + 0.02 * jax.random.normal(k, shape, f32)

    dt = jnp.exp(jax.random.uniform(ks[10], (L, 2, SSD_HEADS), f32, math.log(DT_MIN), math.log(DT_MAX)))
    return {
        'x': nrm(ks[0], (BATCH, SEQ, D_MODEL), 1.0),
        'c': nrm(ks[1], (BATCH, D_MODEL), 1.0),
        'ctx': nrm(ks[2], (BATCH, CTX_LEN, D_MODEL), 1.0),
        'c_ctx': nrm(ks[3], (D_MODEL,), 1.0),
        'w_mod': nrm(ks[4], (L, D_MODEL, 6 * D_MODEL), D_MODEL ** -0.5),
        'b_mod': nrm(ks[5], (L, 6 * D_MODEL), 0.02),
        'norm_mix_w': gain(ks[6], (L, D_MODEL)),
        'norm_ffn_w': gain(ks[7], (L, D_MODEL)),
        'w_in': nrm(ks[8], (L, D_MODEL, D_IN), D_MODEL ** -0.5),
        'ssd_conv_w': nrm(ks[9], (L, CONV_K, SSD_XBC), CONV_K ** -0.5),
        'ssd_conv_b': nrm(ks[11], (L, SSD_XBC), 0.02),
        'ssd_dt_bias': dt + jnp.log(-jnp.expm1(-dt)),
        'ssd_a_log': jnp.log(jax.random.uniform(ks[12], (L, 2, SSD_HEADS), f32, 1.0, 16.0)),
        'ssd_d': gain(ks[13], (L, SSD_HEADS)),
        'ssd_norm_w': gain(ks[14], (L, SSD_WIDTH)),
        'ml_conv_w': nrm(ks[15], (L, CONV_K, 2 * ML_QK_WIDTH), CONV_K ** -0.5),
        'ml_conv_b': nrm(ks[16], (L, 2 * ML_QK_WIDTH), 0.02),
        'ml_i_bias': nrm(ks[17], (L, 2, ML_HEADS), 0.1),
        'ml_f_bias': jax.random.uniform(ks[18], (L, 2, ML_HEADS), f32, 3.0, 6.0),
        'ml_norm_w': gain(ks[19], (L, ML_V_WIDTH)),
        'gla_a_up': nrm(ks[20], (L, 2, GLA_RANK, GLA_K_WIDTH), GLA_RANK ** -0.5),
        'gla_a_bias': nrm(ks[21], (L, 2, GLA_K_WIDTH), 0.02),
        'gla_norm_w': gain(ks[22], (L, GLA_V_WIDTH)),
        'w_b_ssd': nrm(ks[23], (L, SSD_WIDTH, D_MODEL), SSD_WIDTH ** -0.5),
        'w_b_ml': nrm(ks[24], (L, ML_V_WIDTH, D_MODEL), ML_V_WIDTH ** -0.5),
        'w_b_gla': nrm(ks[25], (L, GLA_V_WIDTH, D_MODEL), GLA_V_WIDTH ** -0.5),
        'w_out': nrm(ks[26], (L, D_MODEL, D_MODEL), D_MODEL ** -0.5),
        'w_ffn_in': nrm(ks[27], (L, D_MODEL, 2 * D_FF), D_MODEL ** -0.5),
        'w_ffn_out': nrm(ks[28], (L, D_FF, D_MODEL), D_FF ** -0.5),
        'final_norm_w': gain(ks[29], (D_MODEL,)),
    }


def reference(x, c, ctx, c_ctx, w_mod, b_mod, norm_mix_w, norm_ffn_w, w_in,
              ssd_conv_w, ssd_conv_b, ssd_dt_bias, ssd_a_log, ssd_d, ssd_norm_w,
              ml_conv_w, ml_conv_b, ml_i_bias, ml_f_bias, ml_norm_w,
              gla_a_up, gla_a_bias, gla_norm_w,
              w_b_ssd, w_b_ml, w_b_gla, w_out, w_ffn_in, w_ffn_out, final_norm_w):
    n_ctx, n_lat = ctx.shape[1], x.shape[1]
    rows = n_lat // GRID_W
    ctx_ids = np.arange(n_ctx)
    lat_ids = n_ctx + np.arange(n_lat)
    col_ids = n_ctx + np.arange(n_lat).reshape(rows, GRID_W).T.reshape(-1)
    rev = np.concatenate([ctx_ids[::-1], lat_ids[::-1]]).astype(np.int32)
    col_f = np.concatenate([ctx_ids, col_ids]).astype(np.int32)
    col_b = np.concatenate([ctx_ids[::-1], col_ids[::-1]]).astype(np.int32)
    orders = (rev, col_f, np.argsort(col_f).astype(np.int32), col_b, np.argsort(col_b).astype(np.int32))

    silu_c = jax.nn.silu(c)
    silu_cc = jax.nn.silu(c_ctx)
    x_ctx, x_lat = ctx, x
    for l in range(DEPTH):
        last = l == DEPTH - 1
        sh1, sc1, g1, sh2, sc2, g2 = jnp.split((silu_c @ w_mod[l] + b_mod[l])[:, None, :], 6, axis=-1)
        csh1, csc1, cg1, csh2, csc2, cg2 = jnp.split(silu_cc @ w_mod[l] + b_mod[l], 6)
        h = jnp.concatenate([_rmsnorm(x_ctx, norm_mix_w[l]) * (1.0 + csc1) + csh1,
                             _rmsnorm(x_lat, norm_mix_w[l]) * (1.0 + sc1) + sh1], axis=1)
        p = {
            'w_in': w_in[l], 'ssd_conv_w': ssd_conv_w[l], 'ssd_conv_b': ssd_conv_b[l],
            'ssd_dt_bias': ssd_dt_bias[l], 'ssd_a_log': ssd_a_log[l], 'ssd_d': ssd_d[l],
            'ssd_norm_w': ssd_norm_w[l], 'ml_conv_w': ml_conv_w[l], 'ml_conv_b': ml_conv_b[l],
            'ml_i_bias': ml_i_bias[l], 'ml_f_bias': ml_f_bias[l], 'ml_norm_w': ml_norm_w[l],
            'gla_a_up': gla_a_up[l], 'gla_a_bias': gla_a_bias[l], 'gla_norm_w': gla_norm_w[l],
            'w_b_ssd': w_b_ssd[l], 'w_b_ml': w_b_ml[l], 'w_b_gla': w_b_gla[l], 'w_out': w_out[l],
        }
        out = _mixer(h, n_ctx, not last, orders, p)
        x_lat = x_lat + g1 * out[:, -n_lat:]
        h_lat = _rmsnorm(x_lat, norm_ffn_w[l]) * (1.0 + sc2) + sh2
        x_lat = x_lat + g2 * _swiglu(h_lat, w_ffn_in[l], w_ffn_out[l])
        if not last:
            x_ctx = x_ctx + cg1 * out[:, :n_ctx]
            h_ctx = _rmsnorm(x_ctx, norm_ffn_w[l]) * (1.0 + csc2) + csh2
            x_ctx = x_ctx + cg2 * _swiglu(h_ctx, w_ffn_in[l], w_ffn_out[l])
    return _rmsnorm(x_lat, final_norm_w)
```

```python
import functools

import numpy as np
import jax
import jax.numpy as jnp
from jax import lax
from jax.experimental import pallas as pl
from jax.experimental.pallas import tpu as pltpu

F32 = jnp.float32
BF16 = jnp.bfloat16

EPS = 1e-6
GRID_W = 64
SSD_HEADS = 16
SSD_HEAD_DIM = 64
SSD_WIDTH = 1024
SSD_STATE = 64
SSD_BC = 128
SSD_XBC = SSD_WIDTH + 2 * SSD_BC
ML_HEADS = 4
ML_QK_DIM = 128
ML_V_DIM = 256
ML_QK_WIDTH = 512
ML_V_WIDTH = 1024
GLA_HEADS = 4
GLA_K_DIM = 128
GLA_V_DIM = 256
GLA_K_WIDTH = 512
GLA_V_WIDTH = 1024
GLA_RANK = 16
GLA_TAU = 16.0
GLA_CHUNK = 64

LANES = 128
SUBLANES = 8
VMEM_LIMIT = 56 * 1024 * 1024

TILE = 256

_IN_NAMES = ("s_x", "s_z", "s_b", "s_c", "dt_f", "dt_b",
             "m_q", "m_k", "m_v", "m_o", "i_f", "i_b", "f_f", "f_b",
             "g_q", "g_k", "g_v", "g_g", "a_f", "a_b",
             "gate_ssd", "gate_ml", "gate_gla")
_IN_WIDTHS = (1024, 1024, 128, 128, 16, 16,
              512, 512, 1024, 1024, 4, 4, 4, 4,
              512, 512, 1024, 1024, 16, 16,
              1024, 1024, 1024)
_IN_OFF = dict(zip(_IN_NAMES, np.concatenate([[0], np.cumsum(_IN_WIDTHS)[:-1]]).tolist()))
_IN_W = dict(zip(_IN_NAMES, _IN_WIDTHS))

_P_ORDER = ("s_z", "m_o", "g_g", "gate_ssd", "gate_ml", "gate_gla", "m_v", "g_v",
            "g_q", "g_k", "s_x", "s_b", "s_c", "m_q", "m_k")
_SMALL = ("dt_f", "dt_b", "i_f", "i_b", "f_f", "f_b", "a_f", "a_b")
_P_OFF = {}
_o = 0
for _n in _P_ORDER:
    _P_OFF[_n] = _o
    _o += _IN_W[_n]
P_SMALL = _o
_SM_OFF = {}
_s = 0
for _n in _SMALL:
    _SM_OFF[_n] = _s
    _s += _IN_W[_n]
N_PROJ = P_SMALL + LANES


def _cparams(sem):
    return pltpu.CompilerParams(dimension_semantics=sem, vmem_limit_bytes=VMEM_LIMIT)


def _sigmoid(x):
    return 1.0 / (1.0 + jnp.exp(-x))


def _silu(x):
    return x * _sigmoid(x)


def _softplus(x):
    return jnp.maximum(x, 0.0) + jnp.log1p(jnp.exp(-jnp.abs(x)))


def _log_sigmoid(x):
    return -_softplus(-x)


def _split(x, n):
    out = []
    r = x
    for _ in range(n):
        p = r.astype(BF16)
        out.append(p)
        r = r - p.astype(F32)
    return out


def _dot(a, b):
    return jnp.dot(a, b, preferred_element_type=F32)


def _dot_nt(a, b):
    return lax.dot_general(a, b, (((1,), (1,)), ((), ())), preferred_element_type=F32)


def _dot_tn(a, b):
    return lax.dot_general(a, b, (((0,), (0,)), ((), ())), preferred_element_type=F32)


def _dot_x3(x, e):
    return sum(_dot(p, e) for p in _split(x, 3))


def _dot_3x(t, x):
    return sum(_dot(t, p) for p in _split(x, 3))


def _dot_hp(a, b):
    ah, am = _split(a, 2)
    bh, bm = _split(b, 2)
    return _dot(ah, bh) + _dot(ah, bm) + _dot(am, bh)


def _causal(n, rev):
    t = lax.broadcasted_iota(jnp.int32, (n, n), 0)
    s = lax.broadcasted_iota(jnp.int32, (n, n), 1)
    return (s >= t) if rev else (s <= t)


def _mod_kernel(c_ref, w_ref, b_ref, o_ref):
    o_ref[...] = _dot_hp(_silu(c_ref[...]), w_ref[...]) + b_ref[...]


def _modulation(c16, w_mod, b_mod):
    rows, d = c16.shape
    n = w_mod.shape[1]
    tn = 1536
    return pl.pallas_call(
        _mod_kernel,
        out_shape=jax.ShapeDtypeStruct((rows, n), F32),
        grid=(n // tn,),
        in_specs=[pl.BlockSpec((rows, d), lambda j: (0, 0)),
                  pl.BlockSpec((d, tn), lambda j: (0, j)),
                  pl.BlockSpec((1, tn), lambda j: (0, j))],
        out_specs=pl.BlockSpec((rows, tn), lambda j: (0, j)),
        compiler_params=_cparams(("arbitrary",)),
        name="modulation",
    )(c16, w_mod, b_mod.reshape(1, n))


def _mod_row(nb, k):
    return lambda b, i: (jnp.where(i == 0, nb, b) * 6 + k, 0, 0)


def _norm_mod_kernel(x_ref, w_ref, sc_ref, sh_ref, o_ref):
    x = x_ref[0]
    xn = x * lax.rsqrt(jnp.mean(x * x, axis=-1, keepdims=True) + EPS) * w_ref[...]
    o_ref[0] = (xn * (1.0 + sc_ref[0]) + sh_ref[0]).astype(o_ref.dtype)


def _norm_mod(x, w, mods, k_shift, k_scale):
    nb, t, d = x.shape
    return pl.pallas_call(
        _norm_mod_kernel,
        out_shape=jax.ShapeDtypeStruct((nb, t, d), BF16),
        grid=(nb, t // TILE),
        in_specs=[pl.BlockSpec((1, TILE, d), lambda b, i: (b, i, 0)),
                  pl.BlockSpec((1, d), lambda b, i: (0, 0)),
                  pl.BlockSpec((1, 1, d), _mod_row(nb, k_scale)),
                  pl.BlockSpec((1, 1, d), _mod_row(nb, k_shift))],
        out_specs=pl.BlockSpec((1, TILE, d), lambda b, i: (b, i, 0)),
        compiler_params=_cparams(("parallel", "parallel")),
        name="norm_mod",
    )(x, w.reshape(1, d), mods, mods)


def _mm_kernel(a_ref, w_ref, o_ref):
    o_ref[...] = _dot(a_ref[...], w_ref[...]).astype(o_ref.dtype)


def _row_tile(m):
    return 512 if m % 512 == 0 else TILE


def _matmul(a, w, tn, out_dtype):
    m, k = a.shape
    n = w.shape[1]
    tm = _row_tile(m)
    return pl.pallas_call(
        _mm_kernel,
        out_shape=jax.ShapeDtypeStruct((m, n), out_dtype),
        grid=(n // tn, m // tm),
        in_specs=[pl.BlockSpec((tm, k), lambda j, i: (i, 0)),
                  pl.BlockSpec((k, tn), lambda j, i: (0, j))],
        out_specs=pl.BlockSpec((tm, tn), lambda j, i: (i, j)),
        compiler_params=_cparams(("parallel", "parallel")),
        name="matmul",
    )(a, w)


CONV_K = 5
CONV_ROWS = 256


def _conv_kernel(u_ref, w_ref, b_ref, o_ref, pad_ref, *, n_ctx):
    t, c = u_ref.shape[1], u_ref.shape[2]
    half = CONV_K // 2
    zeros = jnp.zeros((SUBLANES, c), F32)
    w = w_ref[...]
    bias = b_ref[...]
    for s0, n in ((0, n_ctx), (n_ctx, t - n_ctx)):
        pad_ref[0:SUBLANES, :] = zeros
        pad_ref[SUBLANES:SUBLANES + n, :] = u_ref[0, s0:s0 + n, :]
        pad_ref[SUBLANES + n:2 * SUBLANES + n, :] = zeros
        for r0 in range(0, n, CONV_ROWS):
            acc = bias
            for j in range(CONV_K):
                lo = SUBLANES - half + j + r0
                acc = acc + w[j:j + 1, :] * pad_ref[lo:lo + CONV_ROWS, :]
            o_ref[0, s0 + r0:s0 + r0 + CONV_ROWS, :] = _silu(acc)


def _conv(p3, col0, w, b, n_ctx):
    nb, t, _ = p3.shape
    width = w.shape[1]
    cb = 256
    return pl.pallas_call(
        functools.partial(_conv_kernel, n_ctx=n_ctx),
        out_shape=jax.ShapeDtypeStruct((nb, t, width), F32),
        grid=(nb, width // cb),
        in_specs=[pl.BlockSpec((1, t, cb), lambda b_, j: (b_, 0, col0 // cb + j)),
                  pl.BlockSpec((CONV_K, cb), lambda b_, j: (0, j)),
                  pl.BlockSpec((1, cb), lambda b_, j: (0, j))],
        out_specs=pl.BlockSpec((1, t, cb), lambda b_, j: (b_, 0, j)),
        scratch_shapes=[pltpu.VMEM((t + 2 * SUBLANES, cb), F32)],
        compiler_params=_cparams(("parallel", "parallel")),
        name="conv",
    )(p3, w, b.reshape(1, width))


def _tile_order(n_tiles, rev):
    if rev:
        return lambda i: jnp.where(i == 0, 0, n_tiles - i)
    return lambda i: i


def _ssd_kernel(x_ref, bc_ref, sm_ref, dtb_ref, alog_ref, e_ref, d_ref, o_ref, st_ref,
                *, rev, lane_off, add_skip):
    n = x_ref.shape[1]

    @pl.when(pl.program_id(1) == 0)
    def _init():
        st_ref[...] = jnp.zeros_like(st_ref)

    lane = lax.broadcasted_iota(jnp.int32, (1, LANES), 1)
    in_rng = (lane >= lane_off) & (lane < lane_off + SSD_HEADS)
    dt = _softplus(sm_ref[0] + dtb_ref[...])
    la = dt * jnp.where(in_rng, -jnp.exp(alog_ref[...]), 0.0)
    causal = _causal(n, rev)
    tri = jnp.where(causal, 1.0, 0.0).astype(BF16)
    cum = _dot_3x(tri, la)
    cum_t = cum.T
    e = e_ref[...]
    cum_e = _dot_x3(cum, e)
    dt_e = _dot_x3(dt, e)
    x = x_ref[0]
    xdt = x * dt_e
    last = 0 if rev else n - 1
    last_e = cum_e[last:last + 1, :]
    bc = bc_ref[0]
    b128 = bc[:, :SSD_BC].astype(BF16)
    c128 = bc[:, SSD_BC:].astype(BF16)

    st = st_ref[...]
    y_inter = jnp.exp(cum_e) * _dot(c128, st.astype(BF16))
    xw = (xdt * jnp.exp(last_e - cum_e)).astype(BF16)
    rows = lax.broadcasted_iota(jnp.int32, st.shape, 0) // SSD_STATE
    cols = lax.broadcasted_iota(jnp.int32, st.shape, 1) // (SSD_WIDTH // 2)
    st_ref[...] = jnp.where(rows == cols, jnp.exp(last_e) * st + _dot_tn(b128, xw), 0.0)

    xdt_b = xdt.astype(BF16)
    lo = lane < SSD_HEAD_DIM
    zero_b = jnp.zeros((), BF16)
    pairs_per_group = SSD_HEADS // 4
    for g in range(2):
        cg = jnp.where(lo if g == 0 else jnp.logical_not(lo), c128, zero_b)
        cbm = jnp.where(causal, _dot_nt(cg, b128), 0.0)
        for jj in range(pairs_per_group):
            j = g * pairs_per_group + jj
            ms = []
            for h in (2 * j, 2 * j + 1):
                col = cum[:, lane_off + h:lane_off + h + 1]
                row = cum_t[lane_off + h:lane_off + h + 1, :]
                ms.append((cbm * jnp.exp(jnp.minimum(col - row, 0.0))).astype(BF16))
            sl = slice(LANES * j, LANES * (j + 1))
            xp = xdt_b[:, sl]
            x2 = jnp.concatenate([jnp.where(lo, xp, zero_b), jnp.where(lo, zero_b, xp)], axis=0)
            y = _dot(jnp.concatenate(ms, axis=1), x2) + y_inter[:, sl]
            if add_skip:
                y = y + d_ref[:, sl] * x[:, sl]
            o_ref[0, :, sl] = y


def _ssd_scan(cvs, p3, dt_bias, a_log, d_e, rev, add_skip):
    nb, t, _ = cvs.shape
    nt = t // TILE
    lane_off = _SM_OFF["dt_b"] if rev else _SM_OFF["dt_f"]
    e = np.zeros((LANES, SSD_WIDTH), np.float32)
    for h in range(SSD_HEADS):
        e[lane_off + h, h * SSD_HEAD_DIM:(h + 1) * SSD_HEAD_DIM] = 1.0
    row = lambda v: jnp.zeros((1, LANES), F32).at[0, lane_off:lane_off + SSD_HEADS].set(v)
    order = _tile_order(nt, rev)
    tok = lambda blk: (lambda b, i: (b, order(i), blk))
    const = lambda b, i: (0, 0)
    return pl.pallas_call(
        functools.partial(_ssd_kernel, rev=rev, lane_off=lane_off, add_skip=add_skip),
        out_shape=jax.ShapeDtypeStruct((nb, t, SSD_WIDTH), F32),
        grid=(nb, nt),
        in_specs=[pl.BlockSpec((1, TILE, SSD_WIDTH), tok(0)),
                  pl.BlockSpec((1, TILE, 2 * SSD_BC), tok(SSD_WIDTH // (2 * SSD_BC))),
                  pl.BlockSpec((1, TILE, LANES), tok(P_SMALL // LANES)),
                  pl.BlockSpec((1, LANES), const),
                  pl.BlockSpec((1, LANES), const),
                  pl.BlockSpec((LANES, SSD_WIDTH), const),
                  pl.BlockSpec((1, SSD_WIDTH), const)],
        out_specs=pl.BlockSpec((1, TILE, SSD_WIDTH), tok(0)),
        scratch_shapes=[pltpu.VMEM((2 * SSD_STATE, SSD_WIDTH), F32)],
        compiler_params=_cparams(("parallel", "arbitrary")),
        name="ssd_scan",
    )(cvs, cvs, p3, row(dt_bias), row(a_log), jnp.asarray(e, BF16), d_e)


def _mlstm_kernel(qk_ref, v_ref, sm_ref, ib_ref, fb_ref, o_ref, ct_ref, n_ref, m_ref,
                  *, rev, i_off, f_off):
    n = qk_ref.shape[1]

    @pl.when(pl.program_id(1) == 0)
    def _init():
        ct_ref[...] = jnp.zeros_like(ct_ref)
        n_ref[...] = jnp.zeros_like(n_ref)
        m_ref[...] = jnp.zeros_like(m_ref)

    lane = lax.broadcasted_iota(jnp.int32, (1, LANES), 1)
    in_f = (lane >= f_off) & (lane < f_off + ML_HEADS)
    sm = sm_ref[0]
    li = sm + ib_ref[...]
    lf = jnp.where(in_f, _log_sigmoid(sm + fb_ref[...]), 0.0)
    causal = _causal(n, rev)
    tri = jnp.where(causal, 1.0, 0.0).astype(BF16)
    bcum = _dot_3x(tri, lf)
    bcum_t = bcum.T
    li_t = li.T
    last = 0 if rev else n - 1
    scale = ML_QK_DIM ** -0.5
    for h in range(ML_HEADS):
        q = qk_ref[0, :, ML_QK_DIM * h:ML_QK_DIM * (h + 1)]
        k = qk_ref[0, :, ML_QK_WIDTH + ML_QK_DIM * h:ML_QK_WIDTH + ML_QK_DIM * (h + 1)] * scale
        v = v_ref[0, :, ML_V_DIM * h:ML_V_DIM * (h + 1)]
        bcol = bcum[:, f_off + h:f_off + h + 1]
        brow = bcum_t[f_off + h:f_off + h + 1, :]
        licol = li[:, i_off + h:i_off + h + 1]
        lirow = li_t[i_off + h:i_off + h + 1, :]
        m_prev = m_ref[h:h + 1, 0:1]
        seg = jnp.where(causal, bcol - brow + lirow, -jnp.inf)
        inter = m_prev + bcol
        m_t = jnp.maximum(inter, jnp.max(seg, axis=1, keepdims=True))
        w = jnp.exp(seg - m_t)
        w_inter = jnp.exp(inter - m_t)
        qb = q.astype(BF16)
        kb = k.astype(BF16)
        s = _dot_nt(qb, kb) * w
        ct = ct_ref[h]
        nrow = n_ref[h:h + 1, :]
        num = _dot(s.astype(BF16), v.astype(BF16)) + w_inter * _dot(qb, ct.astype(BF16))
        den = (jnp.sum(s, axis=1, keepdims=True)
               + w_inter * jnp.sum(q * nrow, axis=1, keepdims=True))
        o_ref[0, :, ML_V_DIM * h:ML_V_DIM * (h + 1)] = num / jnp.maximum(jnp.abs(den), jnp.exp(-m_t))
        m_new = m_t[last:last + 1, :]
        b_last = bcol[last:last + 1, :]
        ws = jnp.exp(b_last - bcol + licol - m_new)
        keep = jnp.exp(m_prev + b_last - m_new)
        ct_ref[h] = keep * ct + _dot_tn(kb, (ws * v).astype(BF16))
        n_ref[h:h + 1, :] = keep * nrow + jnp.sum(ws * k, axis=0, keepdims=True)
        m_ref[h:h + 1, :] = jnp.broadcast_to(m_new, (1, LANES))


def _mlstm_scan(cvm, p3, i_bias, f_bias, rev):
    nb, t, _ = cvm.shape
    nt = t // TILE
    i_off = _SM_OFF["i_b"] if rev else _SM_OFF["i_f"]
    f_off = _SM_OFF["f_b"] if rev else _SM_OFF["f_f"]
    row = lambda v, off: jnp.zeros((1, LANES), F32).at[0, off:off + ML_HEADS].set(v)
    order = _tile_order(nt, rev)
    tok = lambda blk: (lambda b, i: (b, order(i), blk))
    const = lambda b, i: (0, 0)
    return pl.pallas_call(
        functools.partial(_mlstm_kernel, rev=rev, i_off=i_off, f_off=f_off),
        out_shape=jax.ShapeDtypeStruct((nb, t, ML_V_WIDTH), F32),
        grid=(nb, nt),
        in_specs=[pl.BlockSpec((1, TILE, 2 * ML_QK_WIDTH), tok(0)),
                  pl.BlockSpec((1, TILE, ML_V_WIDTH), tok(_P_OFF["m_v"] // ML_V_WIDTH)),
                  pl.BlockSpec((1, TILE, LANES), tok(P_SMALL // LANES)),
                  pl.BlockSpec((1, LANES), const),
                  pl.BlockSpec((1, LANES), const)],
        out_specs=pl.BlockSpec((1, TILE, ML_V_WIDTH), tok(0)),
        scratch_shapes=[pltpu.VMEM((ML_HEADS, ML_QK_DIM, ML_V_DIM), F32),
                        pltpu.VMEM((SUBLANES, ML_QK_DIM), F32),
                        pltpu.VMEM((SUBLANES, LANES), F32)],
        compiler_params=_cparams(("parallel", "arbitrary")),
        name="mlstm_scan",
    )(cvm, p3, p3, row(i_bias, i_off), row(f_bias, f_off))


def _gla_chunk(q, k, v, araw, aup, abias, st_ref, causal, tri, rev):
    n = q.shape[0]
    g = _log_sigmoid(_dot_hp(araw, aup) + abias) * (1.0 / GLA_TAU)
    b = _dot_3x(tri, g)
    ref = b[n // 2:n // 2 + 1, :]
    last = b[0:1, :] if rev else b[n - 1:n, :]
    qs = q * (GLA_K_DIM ** -0.5)
    qe = (qs * jnp.exp(b - ref)).astype(BF16)
    ke = (k * jnp.exp(ref - b)).astype(BF16)
    qb = (qs * jnp.exp(b)).astype(BF16)
    kl = (k * jnp.exp(last - b)).astype(BF16)
    vb = v.astype(BF16)
    dec = jnp.exp(last)
    outs = []
    for h in range(GLA_HEADS):
        ks = slice(GLA_K_DIM * h, GLA_K_DIM * (h + 1))
        vs = slice(GLA_V_DIM * h, GLA_V_DIM * (h + 1))
        att = jnp.where(causal, _dot_nt(qe[:, ks], ke[:, ks]), 0.0).astype(BF16)
        st = st_ref[h]
        outs.append(_dot(att, vb[:, vs]) + _dot_nt(qb[:, ks], st.astype(BF16)))
        st_ref[h] = st * dec[:, ks] + _dot_tn(vb[:, vs], kl[:, ks])
    return jnp.concatenate(outs, axis=1)


def _gla_kernel(qc_ref, kc_ref, vc_ref, sc_ref, ql_ref, kl_ref, vl_ref, sl_ref,
                aup_ref, ab_ref, o_ref, st_ref, ctxo_ref, *, rev, ctx_rows, lat_rows, n_cblk):
    i = pl.program_id(1)
    nc = GLA_CHUNK
    causal = _causal(nc, rev)
    tri = jnp.where(causal, 1.0, 0.0).astype(BF16)
    aup = aup_ref[...]
    abias = ab_ref[...]
    n_ctx = ctx_rows * GRID_W
    cols_per_chunk = nc // lat_rows
    chunks_per_tile = SUBLANES // cols_per_chunk
    r0, r1 = ctx_rows, ctx_rows + lat_rows

    @pl.when(i == 0)
    def _ctx():
        st_ref[...] = jnp.zeros_like(st_ref)
        js = range(n_ctx // nc)
        for j in (reversed(js) if rev else js):
            rs = slice(nc * j, nc * (j + 1))
            ctxo_ref[rs, :] = _gla_chunk(qc_ref[0, rs, :], kc_ref[0, rs, :], vc_ref[0, rs, :],
                                         sc_ref[0, rs, :], aup, abias, st_ref, causal, tri, rev)

    @pl.when(i > 0)
    def _lat():
        cblk = (n_cblk - i) if rev else (i - 1)
        js = range(chunks_per_tile)
        for j in (reversed(js) if rev else js):
            cs = range(cols_per_chunk * j, cols_per_chunk * (j + 1))
            gather = lambda r: jnp.concatenate([r[0, r0:r1, c, :] for c in cs], axis=0)
            o = _gla_chunk(gather(ql_ref), gather(kl_ref), gather(vl_ref), gather(sl_ref),
                           aup, abias, st_ref, causal, tri, rev)
            for ci, c in enumerate(cs):
                o_ref[0, r0:r1, c, :] = o[lat_rows * ci:lat_rows * (ci + 1), :]
        for r in range(ctx_rows):
            start = pl.multiple_of(r * GRID_W + cblk * SUBLANES, SUBLANES)
            o_ref[0, r, :, :] = ctxo_ref[pl.ds(start, SUBLANES), :]


def _gla_scan(p3, a_up, a_bias, n_ctx, rev):
    nb, t, ncol = p3.shape
    rows = t // GRID_W
    ctx_rows = n_ctx // GRID_W
    lat_rows = rows - ctx_rows
    n_cblk = GRID_W // SUBLANES
    p4 = p3.reshape(nb, rows, GRID_W, ncol)
    a_off = _SM_OFF["a_b"] if rev else _SM_OFF["a_f"]
    aup = jnp.zeros((LANES, GLA_K_WIDTH), F32).at[a_off:a_off + GLA_RANK, :].set(a_up)
    cblk = lambda i: jnp.where(i == 0, n_cblk - 1 if rev else 0, (n_cblk - i) if rev else (i - 1))
    ctx = lambda blk: (lambda b, i: (b, 0, blk))
    lat = lambda blk: (lambda b, i: (b, 0, cblk(i), blk))
    const = lambda b, i: (0, 0)
    widths = (GLA_K_WIDTH, GLA_K_WIDTH, GLA_V_WIDTH, LANES)
    offs = (_P_OFF["g_q"], _P_OFF["g_k"], _P_OFF["g_v"], P_SMALL)
    in_specs = ([pl.BlockSpec((1, n_ctx, w), ctx(o // w)) for w, o in zip(widths, offs)]
                + [pl.BlockSpec((1, rows, SUBLANES, w), lat(o // w)) for w, o in zip(widths, offs)]
                + [pl.BlockSpec((LANES, GLA_K_WIDTH), const),
                   pl.BlockSpec((1, GLA_K_WIDTH), const)])
    out = pl.pallas_call(
        functools.partial(_gla_kernel, rev=rev, ctx_rows=ctx_rows, lat_rows=lat_rows, n_cblk=n_cblk),
        out_shape=jax.ShapeDtypeStruct((nb, rows, GRID_W, GLA_V_WIDTH), F32),
        grid=(nb, n_cblk + 1),
        in_specs=in_specs,
        out_specs=pl.BlockSpec((1, rows, SUBLANES, GLA_V_WIDTH), lat(0)),
        scratch_shapes=[pltpu.VMEM((GLA_HEADS, GLA_V_DIM, GLA_K_DIM), F32),
                        pltpu.VMEM((n_ctx, GLA_V_WIDTH), F32)],
        compiler_params=_cparams(("parallel", "arbitrary")),
        name="gla_scan",
    )(p3, p3, p3, p3, p4, p4, p4, p4, aup, a_bias.reshape(1, GLA_K_WIDTH))
    return out.reshape(nb, t, GLA_V_WIDTH)


def _group_rmsnorm(y, groups):
    width = y.shape[-1] // groups
    parts = []
    for g in range(groups):
        yg = y[:, width * g:width * (g + 1)]
        parts.append(yg * lax.rsqrt(jnp.mean(yg * yg, axis=-1, keepdims=True) + EPS))
    return jnp.concatenate(parts, axis=1)


def _post_kernel(x_ref, yf_ref, yb_ref, hf_ref, hb_ref, of_ref, ob_ref,
                 z_ref, mo_ref, gg_ref, gs_ref, gm_ref, gl_ref,
                 nws_ref, nwm_ref, nwg_ref, wbs_ref, wbm_ref, wbg_ref, wout_ref, g1_ref, o_ref):
    y = (yf_ref[0] + yb_ref[0]) * _silu(z_ref[0])
    y_ssd = (_group_rmsnorm(y, 2) * nws_ref[...]).astype(BF16)
    hm = hf_ref[0] + hb_ref[0]
    y_ml = (_group_rmsnorm(hm, ML_HEADS) * nwm_ref[...] * _sigmoid(mo_ref[0])).astype(BF16)
    og = of_ref[0] + ob_ref[0]
    y_gla = (_group_rmsnorm(og, GLA_HEADS) * nwg_ref[...] * _silu(gg_ref[0])).astype(BF16)
    merged = (_sigmoid(gs_ref[0]) * _dot(y_ssd, wbs_ref[...])
              + _sigmoid(gm_ref[0]) * _dot(y_ml, wbm_ref[...])
              + _sigmoid(gl_ref[0]) * _dot(y_gla, wbg_ref[...]))
    out = _dot(merged.astype(BF16), wout_ref[...])
    o_ref[0] = x_ref[0] + g1_ref[0] * out


def _post(x, scans, p3, norm_ws, w_bs, w_out, mods):
    nb, t, d = x.shape
    tile = TILE // 2
    tok = lambda blk: (lambda b, i: (b, i, blk))
    const = lambda b, i: (0, 0)
    mod = lambda b, i: (jnp.where(i < 2, nb, b) * 6 + 2, 0, 0)
    tok_spec = lambda blk: pl.BlockSpec((1, tile, d), tok(blk))
    w_spec = pl.BlockSpec((d, d), const, pipeline_mode=pl.Buffered(1))
    names = ("s_z", "m_o", "g_g", "gate_ssd", "gate_ml", "gate_gla")
    in_specs = ([tok_spec(0)] * 7
                + [tok_spec(_P_OFF[nm] // d) for nm in names]
                + [pl.BlockSpec((1, d), const)] * 3
                + [w_spec] * 4
                + [pl.BlockSpec((1, 1, d), mod)])
    return pl.pallas_call(
        _post_kernel,
        out_shape=jax.ShapeDtypeStruct((nb, t, d), F32),
        grid=(nb, t // tile),
        in_specs=in_specs,
        out_specs=tok_spec(0),
        compiler_params=_cparams(("parallel", "parallel")),
        name="post",
    )(x, *scans, *([p3] * 6), *[w.reshape(1, d) for w in norm_ws], *w_bs, w_out, mods)


def _ffn_in_kernel(a_ref, w_ref, o_ref):
    acc = _dot(a_ref[...], w_ref[...])
    half = acc.shape[1] // 2
    o_ref[...] = (_silu(acc[:, :half]) * acc[:, half:]).astype(o_ref.dtype)


def _ffn_in(h, w_gu, half):
    m, k = h.shape
    n_half = w_gu.shape[1] // 2
    tm = _row_tile(m)
    return pl.pallas_call(
        _ffn_in_kernel,
        out_shape=jax.ShapeDtypeStruct((m, n_half), BF16),
        grid=(n_half // half, m // tm),
        in_specs=[pl.BlockSpec((tm, k), lambda j, i: (i, 0)),
                  pl.BlockSpec((k, 2 * half), lambda j, i: (0, j))],
        out_specs=pl.BlockSpec((tm, half), lambda j, i: (i, j)),
        compiler_params=_cparams(("parallel", "parallel")),
        name="ffn_in",
    )(h, w_gu)


def _ffn_out_kernel(a_ref, w_ref, x_ref, g_ref, o_ref):
    o_ref[0] = x_ref[0] + g_ref[0] * _dot(a_ref[0], w_ref[...])


def _ffn_out(a, w, x, mods):
    nb, t, d = x.shape
    k = a.shape[-1]
    return pl.pallas_call(
        _ffn_out_kernel,
        out_shape=jax.ShapeDtypeStruct((nb, t, d), F32),
        grid=(nb, t // TILE),
        in_specs=[pl.BlockSpec((1, TILE, k), lambda b, i: (b, i, 0)),
                  pl.BlockSpec((k, d), lambda b, i: (0, 0)),
                  pl.BlockSpec((1, TILE, d), lambda b, i: (b, i, 0)),
                  pl.BlockSpec((1, 1, d), _mod_row(nb, 5))],
        out_specs=pl.BlockSpec((1, TILE, d), lambda b, i: (b, i, 0)),
        compiler_params=_cparams(("parallel", "parallel")),
        name="ffn_out",
    )(a, w, x, mods)


def _final_norm_kernel(x_ref, w_ref, o_ref):
    x = x_ref[0]
    o_ref[0] = x * lax.rsqrt(jnp.mean(x * x, axis=-1, keepdims=True) + EPS) * w_ref[...]


def _final_norm(x, w, n_ctx):
    nb, t, d = x.shape
    skip = n_ctx // TILE
    return pl.pallas_call(
        _final_norm_kernel,
        out_shape=jax.ShapeDtypeStruct((nb, t - n_ctx, d), F32),
        grid=(nb, (t - n_ctx) // TILE),
        in_specs=[pl.BlockSpec((1, TILE, d), lambda b, i: (b, i + skip, 0)),
                  pl.BlockSpec((1, d), lambda b, i: (0, 0))],
        out_specs=pl.BlockSpec((1, TILE, d), lambda b, i: (b, i, 0)),
        compiler_params=_cparams(("parallel", "parallel")),
        name="final_norm",
    )(x, w.reshape(1, d))


def _proj_weight(w_in):
    cols = [w_in[:, _IN_OFF[nm]:_IN_OFF[nm] + _IN_W[nm]] for nm in _P_ORDER + _SMALL]
    used = sum(_IN_W[nm] for nm in _P_ORDER + _SMALL)
    cols.append(jnp.zeros((w_in.shape[0], N_PROJ - used), w_in.dtype))
    return jnp.concatenate(cols, axis=1).astype(BF16)


def _ffn_weight(w_ffn_in, half):
    d_ff = w_ffn_in.shape[1] // 2
    cols = []
    for j in range(d_ff // half):
        cols.append(w_ffn_in[:, j * half:(j + 1) * half])
        cols.append(w_ffn_in[:, d_ff + j * half:d_ff + (j + 1) * half])
    return jnp.concatenate(cols, axis=1).astype(BF16)


def kernel(x, c, ctx, c_ctx, w_mod, b_mod, norm_mix_w, norm_ffn_w, w_in, ssd_conv_w, ssd_conv_b, ssd_dt_bias, ssd_a_log, ssd_d, ssd_norm_w, ml_conv_w, ml_conv_b, ml_i_bias, ml_f_bias, ml_norm_w, gla_a_up, gla_a_bias, gla_norm_w, w_b_ssd, w_b_ml, w_b_gla, w_out, w_ffn_in, w_ffn_out, final_norm_w):
    nb, n_lat, d = x.shape
    n_ctx = ctx.shape[1]
    t = n_ctx + n_lat
    depth = w_in.shape[0]
    d_ff = w_ffn_out.shape[1]
    assert n_ctx == TILE and n_lat % TILE == 0 and n_lat // GRID_W == 32 and GLA_CHUNK % 32 == 0
    ffn_half = d_ff // 2

    xs = jnp.concatenate([ctx, x], axis=1)
    c16 = jnp.zeros((2 * SUBLANES, d), F32).at[:nb].set(c).at[nb].set(c_ctx)
    for l in range(depth):
        mods = _modulation(c16, w_mod[l], b_mod[l]).reshape(2 * SUBLANES * 6, 1, d)
        h = _norm_mod(xs, norm_mix_w[l], mods, 0, 1)
        p = _matmul(h.reshape(nb * t, d), _proj_weight(w_in[l]), N_PROJ // 7, F32)
        p3 = p.reshape(nb, t, N_PROJ)
        cvs = _conv(p3, _P_OFF["s_x"], ssd_conv_w[l], ssd_conv_b[l], n_ctx)
        cvm = _conv(p3, _P_OFF["m_q"], ml_conv_w[l], ml_conv_b[l], n_ctx)
        d_e = jnp.repeat(ssd_d[l], SSD_HEAD_DIM).reshape(1, SSD_WIDTH)
        scans = []
        for rev in (False, True):
            k = int(rev)
            scans.append(_ssd_scan(cvs, p3, ssd_dt_bias[l, k], ssd_a_log[l, k], d_e, rev, not rev))
        for rev in (False, True):
            k = int(rev)
            scans.append(_mlstm_scan(cvm, p3, ml_i_bias[l, k], ml_f_bias[l, k], rev))
        for rev in (False, True):
            k = int(rev)
            scans.append(_gla_scan(p3, gla_a_up[l, k], gla_a_bias[l, k], n_ctx, rev))
        xs = _post(xs, scans, p3, (ssd_norm_w[l], ml_norm_w[l], gla_norm_w[l]),
                   (w_b_ssd[l].astype(BF16), w_b_ml[l].astype(BF16), w_b_gla[l].astype(BF16)),
                   w_out[l].astype(BF16), mods)
        h2 = _norm_mod(xs, norm_ffn_w[l], mods, 3, 4)
        a = _ffn_in(h2.reshape(nb * t, d), _ffn_weight(w_ffn_in[l], ffn_half), ffn_half)
        xs = _ffn_out(a.reshape(nb, t, d_ff), w_ffn_out[l].astype(BF16), xs, mods)
    return _final_norm(xs, final_norm_w, n_ctx)
```

```python
import functools

import numpy as np
import jax
import jax.numpy as jnp
from jax import lax
from jax.experimental import pallas as pl
from jax.experimental.pallas import tpu as pltpu

F32 = jnp.float32
BF16 = jnp.bfloat16

EPS = 1e-6
GRID_W = 64
SSD_HEADS = 16
SSD_HEAD_DIM = 64
SSD_WIDTH = 1024
SSD_STATE = 64
SSD_BC = 128
ML_HEADS = 4
ML_QK_DIM = 128
ML_V_DIM = 256
ML_QK_WIDTH = 512
ML_V_WIDTH = 1024
GLA_HEADS = 4
GLA_K_DIM = 128
GLA_V_DIM = 256
GLA_K_WIDTH = 512
GLA_V_WIDTH = 1024
GLA_RANK = 16
GLA_TAU = 16.0
GLA_CHUNK = 64

LANES = 128
SUBLANES = 8
BF16_SUBLANES = 16
VMEM_LIMIT = 56 * 1024 * 1024

TILE = 256

_IN_NAMES = ("s_x", "s_z", "s_b", "s_c", "dt_f", "dt_b",
             "m_q", "m_k", "m_v", "m_o", "i_f", "i_b", "f_f", "f_b",
             "g_q", "g_k", "g_v", "g_g", "a_f", "a_b",
             "gate_ssd", "gate_ml", "gate_gla")
_IN_WIDTHS = (1024, 1024, 128, 128, 16, 16,
              512, 512, 1024, 1024, 4, 4, 4, 4,
              512, 512, 1024, 1024, 16, 16,
              1024, 1024, 1024)
_IN_OFF = dict(zip(_IN_NAMES, np.concatenate([[0], np.cumsum(_IN_WIDTHS)[:-1]]).tolist()))
_IN_W = dict(zip(_IN_NAMES, _IN_WIDTHS))

_P_ORDER = ("s_z", "m_o", "g_g", "gate_ssd", "gate_ml", "gate_gla", "m_v", "g_v",
            "g_q", "g_k", "s_x", "s_b", "s_c", "m_q", "m_k")
_P_OFF = {}
_o = 0
for _n in _P_ORDER:
    _P_OFF[_n] = _o
    _o += _IN_W[_n]
N_PROJ = _o
_SMALL = ("dt_f", "dt_b", "i_f", "i_b", "f_f", "f_b", "a_f", "a_b")
_SM_OFF = {}
_s = 0
for _n in _SMALL:
    _SM_OFF[_n] = _s
    _s += _IN_W[_n]
N_SMALL_USED = _s


def _cparams(sem):
    return pltpu.CompilerParams(dimension_semantics=sem, vmem_limit_bytes=VMEM_LIMIT)


def _sigmoid(x):
    return 1.0 / (1.0 + jnp.exp(-x))


def _silu(x):
    return x * _sigmoid(x)


def _softplus(x):
    return jnp.maximum(x, 0.0) + jnp.log1p(jnp.exp(-jnp.abs(x)))


def _log_sigmoid(x):
    return -_softplus(-x)


def _split(x, n):
    out = []
    r = x
    for _ in range(n):
        p = r.astype(BF16)
        out.append(p)
        r = r - p.astype(F32)
    return out


def _dot(a, b):
    return jnp.dot(a, b, preferred_element_type=F32)


def _dot_nt(a, b):
    return lax.dot_general(a, b, (((1,), (1,)), ((), ())), preferred_element_type=F32)


def _dot_tn(a, b):
    return lax.dot_general(a, b, (((0,), (0,)), ((), ())), preferred_element_type=F32)


def _dot_x3(x, e):
    return sum(_dot(p, e) for p in _split(x, 3))


def _dot_3x(t, x):
    return sum(_dot(t, p) for p in _split(x, 3))


def _dot_hp(a, b):
    ah, am = _split(a, 2)
    bh, bm = _split(b, 2)
    return _dot(ah, bh) + _dot(ah, bm) + _dot(am, bh)


def _causal(n, rev):
    t = lax.broadcasted_iota(jnp.int32, (n, n), 0)
    s = lax.broadcasted_iota(jnp.int32, (n, n), 1)
    return (s >= t) if rev else (s <= t)


def _rms(x):
    return x * lax.rsqrt(jnp.mean(x * x, axis=-1, keepdims=True) + EPS)


def _mod_kernel(c_ref, w_ref, b_ref, o_ref):
    o_ref[...] = _dot_hp(_silu(c_ref[...]), w_ref[...]) + b_ref[...]


def _modulation(c16, w_mod, b_mod):
    rows, d = c16.shape
    n = w_mod.shape[1]
    tn = 1536
    return pl.pallas_call(
        _mod_kernel,
        out_shape=jax.ShapeDtypeStruct((rows, n), F32),
        grid=(n // tn,),
        in_specs=[pl.BlockSpec((rows, d), lambda j: (0, 0)),
                  pl.BlockSpec((d, tn), lambda j: (0, j)),
                  pl.BlockSpec((1, tn), lambda j: (0, j))],
        out_specs=pl.BlockSpec((rows, tn), lambda j: (0, j)),
        compiler_params=_cparams(("arbitrary",)),
        name="modulation",
    )(c16, w_mod, b_mod.reshape(1, n))


def _mod_row(nb, k, ctx_tiles, tile0=0):
    return lambda b, i: (jnp.where(i + tile0 < ctx_tiles, nb, b) * 6 + k, 0, 0)


def _norm_mod_kernel(x_ref, w_ref, sc_ref, sh_ref, o_ref):
    o_ref[0] = (_rms(x_ref[0]) * w_ref[...] * (1.0 + sc_ref[0]) + sh_ref[0]).astype(o_ref.dtype)


def _norm_mod(x, w, mods, k_shift, k_scale, n_ctx):
    nb, t, d = x.shape
    return pl.pallas_call(
        _norm_mod_kernel,
        out_shape=jax.ShapeDtypeStruct((nb, t, d), BF16),
        grid=(nb, t // TILE),
        in_specs=[pl.BlockSpec((1, TILE, d), lambda b, i: (b, i, 0)),
                  pl.BlockSpec((1, d), lambda b, i: (0, 0)),
                  pl.BlockSpec((1, 1, d), _mod_row(nb, k_scale, n_ctx // TILE)),
                  pl.BlockSpec((1, 1, d), _mod_row(nb, k_shift, n_ctx // TILE))],
        out_specs=pl.BlockSpec((1, TILE, d), lambda b, i: (b, i, 0)),
        compiler_params=_cparams(("parallel", "parallel")),
        name="norm_mod",
    )(x, w.reshape(1, d), mods, mods)


def _mm_kernel(a_ref, w_ref, o_ref):
    o_ref[...] = _dot(a_ref[...], w_ref[...]).astype(o_ref.dtype)


def _row_tile(m):
    return 512 if m % 512 == 0 else TILE


def _matmul(a, w, tn, out_dtype):
    m, k = a.shape
    n = w.shape[1]
    tm = _row_tile(m)
    return pl.pallas_call(
        _mm_kernel,
        out_shape=jax.ShapeDtypeStruct((m, n), out_dtype),
        grid=(n // tn, m // tm),
        in_specs=[pl.BlockSpec((tm, k), lambda j, i: (i, 0)),
                  pl.BlockSpec((k, tn), lambda j, i: (0, j))],
        out_specs=pl.BlockSpec((tm, tn), lambda j, i: (i, j)),
        compiler_params=_cparams(("parallel", "parallel")),
        name="matmul",
    )(a, w)


CONV_K = 5
CONV_ROWS = 256


def _conv_kernel(u_ref, w_ref, b_ref, s_ref, o_ref, pad_ref, *, n_ctx):
    t, c = u_ref.shape[1], u_ref.shape[2]
    half = CONV_K // 2
    zeros = jnp.zeros((SUBLANES, c), F32)
    w = w_ref[...]
    bias = b_ref[...]
    post = s_ref[...]
    for s0, n in ((0, n_ctx), (n_ctx, t - n_ctx)):
        pad_ref[0:SUBLANES, :] = zeros
        pad_ref[SUBLANES:SUBLANES + n, :] = u_ref[0, s0:s0 + n, :].astype(F32)
        pad_ref[SUBLANES + n:2 * SUBLANES + n, :] = zeros
        for r0 in range(0, n, CONV_ROWS):
            acc = bias
            for j in range(CONV_K):
                lo = SUBLANES - half + j + r0
                acc = acc + w[j:j + 1, :] * pad_ref[lo:lo + CONV_ROWS, :]
            o_ref[0, s0 + r0:s0 + r0 + CONV_ROWS, :] = (_silu(acc) * post).astype(o_ref.dtype)


def _conv(p3, col0, w, b, post_scale, n_ctx):
    nb, t, _ = p3.shape
    width = w.shape[1]
    cb = 256
    return pl.pallas_call(
        functools.partial(_conv_kernel, n_ctx=n_ctx),
        out_shape=jax.ShapeDtypeStruct((nb, t, width), BF16),
        grid=(nb, width // cb),
        in_specs=[pl.BlockSpec((1, t, cb), lambda b_, j: (b_, 0, col0 // cb + j)),
                  pl.BlockSpec((CONV_K, cb), lambda b_, j: (0, j)),
                  pl.BlockSpec((1, cb), lambda b_, j: (0, j)),
                  pl.BlockSpec((1, cb), lambda b_, j: (0, j))],
        out_specs=pl.BlockSpec((1, t, cb), lambda b_, j: (b_, 0, j)),
        scratch_shapes=[pltpu.VMEM((t + 2 * SUBLANES, cb), F32)],
        compiler_params=_cparams(("parallel", "parallel")),
        name="conv",
    )(p3, w, b.reshape(1, width), post_scale.reshape(1, width))


def _tile_order(n_tiles, rev):
    if rev:
        return lambda i: jnp.where(i == 0, 0, n_tiles - i)
    return lambda i: i


def _write(o_ref, idx, val, acc_ref):
    o_ref[idx] = val if acc_ref is None else acc_ref[idx] + val


def _ssd_kernel(x_ref, bc_ref, sm_ref, dtb_ref, alog_ref, e_ref, d_ref, *rest, rev, lane_off):
    acc_ref, o_ref, st_ref = rest if len(rest) == 3 else (None,) + rest
    n = x_ref.shape[1]

    @pl.when(pl.program_id(1) == 0)
    def _init():
        st_ref[...] = jnp.zeros_like(st_ref)

    lane = lax.broadcasted_iota(jnp.int32, (1, LANES), 1)
    in_rng = (lane >= lane_off) & (lane < lane_off + SSD_HEADS)
    dt = _softplus(sm_ref[0] + dtb_ref[...])
    la = dt * jnp.where(in_rng, -jnp.exp(alog_ref[...]), 0.0)
    causal = _causal(n, rev)
    tri = jnp.where(causal, 1.0, 0.0).astype(BF16)
    cum = _dot_3x(tri, la)
    cum_t = cum.T
    e = e_ref[...]
    cum_e = _dot_x3(cum, e)
    dt_e = _dot_x3(dt, e)
    x = x_ref[0].astype(F32)
    xdt = x * dt_e
    last = 0 if rev else n - 1
    last_e = cum_e[last:last + 1, :]
    b128 = bc_ref[0, :, :SSD_BC]
    c128 = bc_ref[0, :, SSD_BC:]

    st = st_ref[...]
    y_inter = jnp.exp(cum_e) * _dot(c128, st.astype(BF16))
    xw = (xdt * jnp.exp(last_e - cum_e)).astype(BF16)
    rows = lax.broadcasted_iota(jnp.int32, st.shape, 0) // SSD_STATE
    cols = lax.broadcasted_iota(jnp.int32, st.shape, 1) // (SSD_WIDTH // 2)
    st_ref[...] = jnp.where(rows == cols, jnp.exp(last_e) * st + _dot_tn(b128, xw), 0.0)

    xdt_b = xdt.astype(BF16)
    lo = lane < SSD_HEAD_DIM
    zero_b = jnp.zeros((), BF16)
    pairs_per_group = SSD_HEADS // 4
    for g in range(2):
        cg = jnp.where(lo if g == 0 else jnp.logical_not(lo), c128, zero_b)
        cbm = jnp.where(causal, _dot_nt(cg, b128), 0.0)
        for jj in range(pairs_per_group):
            j = g * pairs_per_group + jj
            ms = []
            for h in (2 * j, 2 * j + 1):
                col = cum[:, lane_off + h:lane_off + h + 1]
                row = cum_t[lane_off + h:lane_off + h + 1, :]
                ms.append((cbm * jnp.exp(jnp.minimum(col - row, 0.0))).astype(BF16))
            sl = slice(LANES * j, LANES * (j + 1))
            xp = xdt_b[:, sl]
            x2 = jnp.concatenate([jnp.where(lo, xp, zero_b), jnp.where(lo, zero_b, xp)], axis=0)
            y = _dot(jnp.concatenate(ms, axis=1), x2) + y_inter[:, sl]
            if acc_ref is None:
                y = y + d_ref[:, sl] * x[:, sl]
            _write(o_ref, (0, slice(None), sl), y, acc_ref)


def _ssd_scan(cvs, sm3, dt_bias, a_log, d_e, rev, acc):
    nb, t, _ = cvs.shape
    nt = t // TILE
    lane_off = _SM_OFF["dt_b"] if rev else _SM_OFF["dt_f"]
    e = np.zeros((LANES, SSD_WIDTH), np.float32)
    for h in range(SSD_HEADS):
        e[lane_off + h, h * SSD_HEAD_DIM:(h + 1) * SSD_HEAD_DIM] = 1.0
    row = lambda v: jnp.zeros((1, LANES), F32).at[0, lane_off:lane_off + SSD_HEADS].set(v)
    order = _tile_order(nt, rev)
    tok = lambda blk: (lambda b, i: (b, order(i), blk))
    const = lambda b, i: (0, 0)
    out_spec = pl.BlockSpec((1, TILE, SSD_WIDTH), tok(0))
    in_specs = [pl.BlockSpec((1, TILE, SSD_WIDTH), tok(0)),
                pl.BlockSpec((1, TILE, 2 * SSD_BC), tok(SSD_WIDTH // (2 * SSD_BC))),
                pl.BlockSpec((1, TILE, LANES), tok(0)),
                pl.BlockSpec((1, LANES), const),
                pl.BlockSpec((1, LANES), const),
                pl.BlockSpec((LANES, SSD_WIDTH), const),
                pl.BlockSpec((1, SSD_WIDTH), const)]
    args = [cvs, cvs, sm3, row(dt_bias), row(a_log), jnp.asarray(e, BF16), d_e]
    if acc is not None:
        in_specs.append(out_spec)
        args.append(acc)
    return pl.pallas_call(
        functools.partial(_ssd_kernel, rev=rev, lane_off=lane_off),
        out_shape=jax.ShapeDtypeStruct((nb, t, SSD_WIDTH), F32),
        grid=(nb, nt),
        in_specs=in_specs,
        out_specs=out_spec,
        scratch_shapes=[pltpu.VMEM((2 * SSD_STATE, SSD_WIDTH), F32)],
        input_output_aliases={} if acc is None else {len(args) - 1: 0},
        compiler_params=_cparams(("parallel", "arbitrary")),
        name="ssd_scan",
    )(*args)


def _mlstm_kernel(qk_ref, v_ref, sm_ref, ib_ref, fb_ref, *rest, rev, i_off, f_off):
    acc_ref, o_ref, ct_ref, n_ref, m_ref = rest if len(rest) == 5 else (None,) + rest
    n = qk_ref.shape[1]

    @pl.when(pl.program_id(1) == 0)
    def _init():
        ct_ref[...] = jnp.zeros_like(ct_ref)
        n_ref[...] = jnp.zeros_like(n_ref)
        m_ref[...] = jnp.zeros_like(m_ref)

    lane = lax.broadcasted_iota(jnp.int32, (1, LANES), 1)
    in_f = (lane >= f_off) & (lane < f_off + ML_HEADS)
    sm = sm_ref[0]
    li = sm + ib_ref[...]
    lf = jnp.where(in_f, _log_sigmoid(sm + fb_ref[...]), 0.0)
    causal = _causal(n, rev)
    tri = jnp.where(causal, 1.0, 0.0).astype(BF16)
    bcum = _dot_3x(tri, lf)
    bcum_t = bcum.T
    li_t = li.T
    last = 0 if rev else n - 1
    for h in range(ML_HEADS):
        qb = qk_ref[0, :, ML_QK_DIM * h:ML_QK_DIM * (h + 1)]
        kb = qk_ref[0, :, ML_QK_WIDTH + ML_QK_DIM * h:ML_QK_WIDTH + ML_QK_DIM * (h + 1)]
        vb = v_ref[0, :, ML_V_DIM * h:ML_V_DIM * (h + 1)]
        bcol = bcum[:, f_off + h:f_off + h + 1]
        brow = bcum_t[f_off + h:f_off + h + 1, :]
        licol = li[:, i_off + h:i_off + h + 1]
        lirow = li_t[i_off + h:i_off + h + 1, :]
        m_prev = m_ref[h:h + 1, 0:1]
        seg = jnp.where(causal, bcol - brow + lirow, -jnp.inf)
        inter = m_prev + bcol
        m_t = jnp.maximum(inter, jnp.max(seg, axis=1, keepdims=True))
        w = jnp.exp(seg - m_t)
        w_inter = jnp.exp(inter - m_t)
        s = _dot_nt(qb, kb) * w
        ct = ct_ref[h]
        nrow = n_ref[h:h + 1, :]
        num = _dot(s.astype(BF16), vb) + w_inter * _dot(qb, ct.astype(BF16))
        den = (jnp.sum(s, axis=1, keepdims=True)
               + w_inter * jnp.sum(qb.astype(F32) * nrow, axis=1, keepdims=True))
        hid = num / jnp.maximum(jnp.abs(den), jnp.exp(-m_t))
        _write(o_ref, (0, slice(None), slice(ML_V_DIM * h, ML_V_DIM * (h + 1))), hid, acc_ref)
        m_new = m_t[last:last + 1, :]
        b_last = bcol[last:last + 1, :]
        ws = jnp.exp(b_last - bcol + licol - m_new)
        keep = jnp.exp(m_prev + b_last - m_new)
        ct_ref[h] = keep * ct + _dot_tn(kb, (ws * vb.astype(F32)).astype(BF16))
        n_ref[h:h + 1, :] = keep * nrow + jnp.sum(ws * kb.astype(F32), axis=0, keepdims=True)
        m_ref[h:h + 1, :] = jnp.broadcast_to(m_new, (1, LANES))


def _mlstm_scan(cvm, p3, sm3, i_bias, f_bias, rev, acc):
    nb, t, _ = cvm.shape
    nt = t // TILE
    i_off = _SM_OFF["i_b"] if rev else _SM_OFF["i_f"]
    f_off = _SM_OFF["f_b"] if rev else _SM_OFF["f_f"]
    row = lambda v, off: jnp.zeros((1, LANES), F32).at[0, off:off + ML_HEADS].set(v)
    order = _tile_order(nt, rev)
    tok = lambda blk: (lambda b, i: (b, order(i), blk))
    const = lambda b, i: (0, 0)
    out_spec = pl.BlockSpec((1, TILE, ML_V_WIDTH), tok(0))
    in_specs = [pl.BlockSpec((1, TILE, 2 * ML_QK_WIDTH), tok(0)),
                pl.BlockSpec((1, TILE, ML_V_WIDTH), tok(_P_OFF["m_v"] // ML_V_WIDTH)),
                pl.BlockSpec((1, TILE, LANES), tok(0)),
                pl.BlockSpec((1, LANES), const),
                pl.BlockSpec((1, LANES), const)]
    args = [cvm, p3, sm3, row(i_bias, i_off), row(f_bias, f_off)]
    if acc is not None:
        in_specs.append(out_spec)
        args.append(acc)
    return pl.pallas_call(
        functools.partial(_mlstm_kernel, rev=rev, i_off=i_off, f_off=f_off),
        out_shape=jax.ShapeDtypeStruct((nb, t, ML_V_WIDTH), F32),
        grid=(nb, nt),
        in_specs=in_specs,
        out_specs=out_spec,
        scratch_shapes=[pltpu.VMEM((ML_HEADS, ML_QK_DIM, ML_V_DIM), F32),
                        pltpu.VMEM((SUBLANES, ML_QK_DIM), F32),
                        pltpu.VMEM((SUBLANES, LANES), F32)],
        input_output_aliases={} if acc is None else {len(args) - 1: 0},
        compiler_params=_cparams(("parallel", "arbitrary")),
        name="mlstm_scan",
    )(*args)


GLA_SUB = 256
GLA_NCH = GLA_SUB // GLA_CHUNK
GLA_COLS = BF16_SUBLANES


def _gla_layout(is_ctx, lat_rows):
    r = np.arange(GLA_SUB)
    if is_ctx:
        return r // GLA_CHUNK, r % GLA_CHUNK
    col = r % SUBLANES
    cpc = GLA_CHUNK // lat_rows
    return col // cpc, (col % cpc) * lat_rows + r // SUBLANES


def _gla_consts(is_ctx, lat_rows, rev):
    ch, pos = _gla_layout(is_ctx, lat_rows)
    same = ch[:, None] == ch[None, :]
    before = (pos[None, :] >= pos[:, None]) if rev else (pos[None, :] <= pos[:, None])
    tri = (same & before).astype(np.float32)
    cmask = np.stack([np.repeat((ch == j)[:, None], LANES, axis=1) for j in range(GLA_NCH)])
    return tri, cmask.astype(np.float32)


def _gla_row(is_ctx, lat_rows, j, p):
    ch, pos = _gla_layout(is_ctx, lat_rows)
    return int(np.nonzero((ch == j) & (pos == p))[0][0])


def _per_chunk_rows(b, is_ctx, lat_rows, p):
    rows = [b[_gla_row(is_ctx, lat_rows, j, p):_gla_row(is_ctx, lat_rows, j, p) + 1, :]
            for j in range(GLA_NCH)]
    w = b.shape[1]
    if is_ctx:
        full = jnp.concatenate([jnp.broadcast_to(r, (GLA_CHUNK, w)) for r in rows], axis=0)
    else:
        rep = SUBLANES // GLA_NCH
        pat = jnp.concatenate([jnp.broadcast_to(r, (rep, w)) for r in rows], axis=0)
        full = jnp.broadcast_to(pat[None], (GLA_SUB // SUBLANES, SUBLANES, w)).reshape(GLA_SUB, w)
    return rows, full


def _gla_sub(q, k, v, araw, aup, abias, tri_b, tri_f, cmask_ref, st_ref, *, rev, is_ctx, lat_rows):
    g = _log_sigmoid(_dot_hp(araw, aup) + abias) * (1.0 / GLA_TAU)
    b = _dot_3x(tri_b, g)
    _, ref = _per_chunk_rows(b, is_ctx, lat_rows, GLA_CHUNK // 2)
    lasts, last = _per_chunk_rows(b, is_ctx, lat_rows, 0 if rev else GLA_CHUNK - 1)
    qs = q * (GLA_K_DIM ** -0.5)
    qe = (qs * jnp.exp(b - ref)).astype(BF16)
    ke = (k * jnp.exp(ref - b)).astype(BF16)
    qb = (qs * jnp.exp(b)).astype(BF16)
    kl = (k * jnp.exp(last - b)).astype(BF16)
    visible = tri_f > 0.0
    order = range(GLA_NCH - 1, -1, -1) if rev else range(GLA_NCH)
    outs = []
    for h in range(GLA_HEADS):
        ks = slice(GLA_K_DIM * h, GLA_K_DIM * (h + 1))
        vh = v[:, GLA_V_DIM * h:GLA_V_DIM * (h + 1)]
        att = jnp.where(visible, _dot_nt(qe[:, ks], ke[:, ks]), 0.0).astype(BF16)
        o = _dot(att, vh)
        klm = jnp.concatenate([kl[:, ks] * cmask_ref[j].astype(BF16) for j in range(GLA_NCH)], axis=1)
        upd = _dot_tn(vh, klm)
        s = st_ref[h]
        s_in = [None] * GLA_NCH
        for j in order:
            s_in[j] = s.astype(BF16)
            s = s * jnp.exp(lasts[j][:, ks]) + upd[:, GLA_K_DIM * j:GLA_K_DIM * (j + 1)]
        st_ref[h] = s
        qbm = jnp.concatenate([qb[:, ks] * cmask_ref[j].astype(BF16) for j in range(GLA_NCH)], axis=1)
        outs.append(o + _dot_nt(qbm, jnp.concatenate(s_in, axis=1)))
    return jnp.concatenate(outs, axis=1)


def _gla_kernel(qc_ref, kc_ref, vc_ref, sc_ref, ql_ref, kl_ref, vl_ref, sl_ref,
                aup_ref, ab_ref, tcb_ref, tcf_ref, cmc_ref, tlb_ref, tlf_ref, cml_ref,
                *rest, rev, ctx_rows, lat_rows, n_cblk):
    acc_ref, o_ref, st_ref, ctxo_ref = rest if len(rest) == 4 else (None,) + rest
    i = pl.program_id(1)
    aup = aup_ref[...]
    abias = ab_ref[...]
    r0, r1 = ctx_rows, ctx_rows + lat_rows
    sub = functools.partial(_gla_sub, aup=aup, abias=abias, st_ref=st_ref, rev=rev, lat_rows=lat_rows)

    @pl.when(i == 0)
    def _ctx():
        st_ref[...] = jnp.zeros_like(st_ref)
        ctxo_ref[...] = sub(qc_ref[0].astype(F32), kc_ref[0].astype(F32), vc_ref[0], sc_ref[0],
                            tri_b=tcb_ref[...], tri_f=tcf_ref[...], cmask_ref=cmc_ref, is_ctx=True)

    @pl.when(i > 0)
    def _lat():
        cblk = (n_cblk - i) if rev else (i - 1)
        halves = range(GLA_COLS // SUBLANES)
        for half in (reversed(halves) if rev else halves):
            cs = slice(SUBLANES * half, SUBLANES * (half + 1))
            take = lambda r: r[0, r0:r1].astype(F32)[:, cs, :].reshape(GLA_SUB, r.shape[-1])
            o = sub(take(ql_ref), take(kl_ref), take(vl_ref).astype(BF16), take(sl_ref),
                    tri_b=tlb_ref[...], tri_f=tlf_ref[...], cmask_ref=cml_ref, is_ctx=False)
            _write(o_ref, (0, slice(r0, r1), cs, slice(None)),
                   o.reshape(lat_rows, SUBLANES, GLA_V_WIDTH), acc_ref)
        for r in range(ctx_rows):
            start = pl.multiple_of(r * GRID_W + cblk * GLA_COLS, GLA_COLS)
            _write(o_ref, (0, r, slice(None), slice(None)), ctxo_ref[pl.ds(start, GLA_COLS), :], acc_ref)


def _gla_scan(p3, sm3, a_up, a_bias, n_ctx, rev, acc):
    nb, t, ncol = p3.shape
    rows = t // GRID_W
    ctx_rows = n_ctx // GRID_W
    lat_rows = rows - ctx_rows
    n_cblk = GRID_W // GLA_COLS
    p4 = p3.reshape(nb, rows, GRID_W, ncol)
    sm4 = sm3.reshape(nb, rows, GRID_W, LANES)
    a_off = _SM_OFF["a_b"] if rev else _SM_OFF["a_f"]
    aup = jnp.zeros((LANES, GLA_K_WIDTH), F32).at[a_off:a_off + GLA_RANK, :].set(a_up)
    cblk = lambda i: jnp.where(i == 0, n_cblk - 1 if rev else 0, (n_cblk - i) if rev else (i - 1))
    ctx = lambda blk: (lambda b, i: (b, 0, blk))
    lat = lambda blk: (lambda b, i: (b, 0, cblk(i), blk))
    const2 = lambda b, i: (0, 0)
    const3 = lambda b, i: (0, 0, 0)
    widths = (GLA_K_WIDTH, GLA_K_WIDTH, GLA_V_WIDTH)
    offs = (_P_OFF["g_q"], _P_OFF["g_k"], _P_OFF["g_v"])
    consts = []
    const_specs = []
    for is_ctx in (True, False):
        tri, cmask = _gla_consts(is_ctx, lat_rows, rev)
        consts += [jnp.asarray(tri, BF16), jnp.asarray(tri, F32), jnp.asarray(cmask, F32)]
        const_specs += [pl.BlockSpec((GLA_SUB, GLA_SUB), const2), pl.BlockSpec((GLA_SUB, GLA_SUB), const2),
                        pl.BlockSpec((GLA_NCH, GLA_SUB, LANES), const3)]
    out_spec = pl.BlockSpec((1, rows, GLA_COLS, GLA_V_WIDTH), lat(0))
    in_specs = ([pl.BlockSpec((1, n_ctx, w), ctx(o // w)) for w, o in zip(widths, offs)]
                + [pl.BlockSpec((1, n_ctx, LANES), ctx(0))]
                + [pl.BlockSpec((1, rows, GLA_COLS, w), lat(o // w)) for w, o in zip(widths, offs)]
                + [pl.BlockSpec((1, rows, GLA_COLS, LANES), lat(0))]
                + [pl.BlockSpec((LANES, GLA_K_WIDTH), const2), pl.BlockSpec((1, GLA_K_WIDTH), const2)]
                + const_specs)
    args = [p3, p3, p3, sm3, p4, p4, p4, sm4, aup, a_bias.reshape(1, GLA_K_WIDTH)] + consts
    if acc is not None:
        in_specs.append(out_spec)
        args.append(acc.reshape(nb, rows, GRID_W, GLA_V_WIDTH))
    out = pl.pallas_call(
        functools.partial(_gla_kernel, rev=rev, ctx_rows=ctx_rows, lat_rows=lat_rows, n_cblk=n_cblk),
        out_shape=jax.ShapeDtypeStruct((nb, rows, GRID_W, GLA_V_WIDTH), F32),
        grid=(nb, n_cblk + 1),
        in_specs=in_specs,
        out_specs=out_spec,
        scratch_shapes=[pltpu.VMEM((GLA_HEADS, GLA_V_DIM, GLA_K_DIM), F32),
                        pltpu.VMEM((n_ctx, GLA_V_WIDTH), F32)],
        input_output_aliases={} if acc is None else {len(args) - 1: 0},
        compiler_params=_cparams(("parallel", "arbitrary")),
        name="gla_scan",
    )(*args)
    return out.reshape(nb, t, GLA_V_WIDTH)


def _group_rmsnorm(y, groups):
    width = y.shape[-1] // groups
    return jnp.concatenate([_rms(y[:, width * g:width * (g + 1)]) for g in range(groups)], axis=1)


def _post_kernel(x_ref, y_ref, h_ref, o_ref_in, z_ref, mo_ref, gg_ref, gs_ref, gm_ref, gl_ref,
                 nws_ref, nwm_ref, nwg_ref, wbs_ref, wbm_ref, wbg_ref, wout_ref, g1_ref,
                 nwf_ref, sc2_ref, sh2_ref, xo_ref, ho_ref):
    f = lambda r: r[0].astype(F32)
    y_ssd = (_group_rmsnorm(y_ref[0] * _silu(f(z_ref)), 2) * nws_ref[...]).astype(BF16)
    y_ml = (_group_rmsnorm(h_ref[0], ML_HEADS) * nwm_ref[...] * _sigmoid(f(mo_ref))).astype(BF16)
    y_gla = (_group_rmsnorm(o_ref_in[0], GLA_HEADS) * nwg_ref[...] * _silu(f(gg_ref))).astype(BF16)
    merged = (_sigmoid(f(gs_ref)) * _dot(y_ssd, wbs_ref[...])
              + _sigmoid(f(gm_ref)) * _dot(y_ml, wbm_ref[...])
              + _sigmoid(f(gl_ref)) * _dot(y_gla, wbg_ref[...]))
    x_new = x_ref[0] + g1_ref[0] * _dot(merged.astype(BF16), wout_ref[...])
    xo_ref[0] = x_new
    ho_ref[0] = (_rms(x_new) * nwf_ref[...] * (1.0 + sc2_ref[0]) + sh2_ref[0]).astype(ho_ref.dtype)


def _post(x, scans, p3, norm_ws, w_bs, w_out, norm_ffn_w, mods, n_ctx, tile0):
    nb, t, d = x.shape
    nt = t // TILE - tile0
    ctx_tiles = n_ctx // TILE
    tok = lambda blk: (lambda b, i: (b, i + tile0, blk))
    out = lambda b, i: (b, i, 0)
    const = lambda b, i: (0, 0)
    tok_spec = lambda blk: pl.BlockSpec((1, TILE, d), tok(blk))
    w_spec = pl.BlockSpec((d, d), const, pipeline_mode=pl.Buffered(1))
    vec = pl.BlockSpec((1, d), const)
    mod = lambda k: pl.BlockSpec((1, 1, d), _mod_row(nb, k, ctx_tiles, tile0))
    names = ("s_z", "m_o", "g_g", "gate_ssd", "gate_ml", "gate_gla")
    in_specs = ([tok_spec(0)] * 4 + [tok_spec(_P_OFF[nm] // d) for nm in names]
                + [vec] * 3 + [w_spec] * 4 + [mod(2), vec, mod(4), mod(3)])
    return pl.pallas_call(
        _post_kernel,
        out_shape=(jax.ShapeDtypeStruct((nb, nt * TILE, d), F32),
                   jax.ShapeDtypeStruct((nb, nt * TILE, d), BF16)),
        grid=(nb, nt),
        in_specs=in_specs,
        out_specs=(pl.BlockSpec((1, TILE, d), out), pl.BlockSpec((1, TILE, d), out)),
        compiler_params=_cparams(("parallel", "parallel")),
        name="post",
    )(x, *scans, *([p3] * 6), *[w.reshape(1, d) for w in norm_ws], *w_bs, w_out, mods,
      norm_ffn_w.reshape(1, d), mods, mods)


def _ffn_in_kernel(a_ref, w_ref, o_ref):
    acc = _dot(a_ref[...], w_ref[...])
    half = acc.shape[1] // 2
    o_ref[...] = (_silu(acc[:, :half]) * acc[:, half:]).astype(o_ref.dtype)


def _ffn_in(h, w_gu, half):
    m, k = h.shape
    n_half = w_gu.shape[1] // 2
    tm = _row_tile(m)
    return pl.pallas_call(
        _ffn_in_kernel,
        out_shape=jax.ShapeDtypeStruct((m, n_half), BF16),
        grid=(n_half // half, m // tm),
        in_specs=[pl.BlockSpec((tm, k), lambda j, i: (i, 0)),
                  pl.BlockSpec((k, 2 * half), lambda j, i: (0, j))],
        out_specs=pl.BlockSpec((tm, half), lambda j, i: (i, j)),
        compiler_params=_cparams(("parallel", "parallel")),
        name="ffn_in",
    )(h, w_gu)


def _ffn_out_kernel(a_ref, w_ref, x_ref, g_ref, nw_ref, sc_ref, sh_ref, xo_ref, ho_ref):
    x_new = x_ref[0] + g_ref[0] * _dot(a_ref[0], w_ref[...])
    xo_ref[0] = x_new
    ho_ref[0] = (_rms(x_new) * nw_ref[...] * (1.0 + sc_ref[0]) + sh_ref[0]).astype(ho_ref.dtype)


def _ffn_out_last_kernel(a_ref, w_ref, x_ref, g_ref, nw_ref, o_ref):
    x_new = x_ref[0] + g_ref[0] * _dot(a_ref[0], w_ref[...])
    o_ref[0] = _rms(x_new) * nw_ref[...]


def _ffn_out(a, w, x, mods, n_ctx, tile0, next_norm_w, next_mods):
    nb, t, d = x.shape
    k = a.shape[-1]
    ctx_tiles = n_ctx // TILE
    tok = lambda b, i: (b, i, 0)
    mod = lambda k_, m=None: pl.BlockSpec((1, 1, d), _mod_row(nb, k_, ctx_tiles, tile0))
    in_specs = [pl.BlockSpec((1, TILE, k), tok),
                pl.BlockSpec((k, d), lambda b, i: (0, 0), pipeline_mode=pl.Buffered(1)),
                pl.BlockSpec((1, TILE, d), tok),
                mod(5),
                pl.BlockSpec((1, d), lambda b, i: (0, 0))]
    args = [a, w, x, mods, next_norm_w.reshape(1, d)]
    if next_mods is None:
        body = _ffn_out_last_kernel
        out_shape = jax.ShapeDtypeStruct((nb, t, d), F32)
        out_specs = pl.BlockSpec((1, TILE, d), tok)
    else:
        body = _ffn_out_kernel
        in_specs += [mod(1), mod(0)]
        args += [next_mods, next_mods]
        out_shape = (jax.ShapeDtypeStruct((nb, t, d), F32), jax.ShapeDtypeStruct((nb, t, d), BF16))
        out_specs = (pl.BlockSpec((1, TILE, d), tok), pl.BlockSpec((1, TILE, d), tok))
    return pl.pallas_call(
        body,
        out_shape=out_shape,
        grid=(nb, t // TILE),
        in_specs=in_specs,
        out_specs=out_specs,
        compiler_params=_cparams(("parallel", "parallel")),
        name="ffn_out",
    )(*args)


def _proj_weights(w_in):
    main = jnp.concatenate([w_in[:, _IN_OFF[nm]:_IN_OFF[nm] + _IN_W[nm]] for nm in _P_ORDER], axis=1)
    small = [w_in[:, _IN_OFF[nm]:_IN_OFF[nm] + _IN_W[nm]] for nm in _SMALL]
    small.append(jnp.zeros((w_in.shape[0], LANES - N_SMALL_USED), w_in.dtype))
    return main.astype(BF16), jnp.concatenate(small, axis=1).astype(BF16)


def _ffn_weight(w_ffn_in, half):
    d_ff = w_ffn_in.shape[1] // 2
    cols = []
    for j in range(d_ff // half):
        cols.append(w_ffn_in[:, j * half:(j + 1) * half])
        cols.append(w_ffn_in[:, d_ff + j * half:d_ff + (j + 1) * half])
    return jnp.concatenate(cols, axis=1).astype(BF16)


def kernel(x, c, ctx, c_ctx, w_mod, b_mod, norm_mix_w, norm_ffn_w, w_in, ssd_conv_w, ssd_conv_b, ssd_dt_bias, ssd_a_log, ssd_d, ssd_norm_w, ml_conv_w, ml_conv_b, ml_i_bias, ml_f_bias, ml_norm_w, gla_a_up, gla_a_bias, gla_norm_w, w_b_ssd, w_b_ml, w_b_gla, w_out, w_ffn_in, w_ffn_out, final_norm_w):
    nb, n_lat, d = x.shape
    n_ctx = ctx.shape[1]
    t = n_ctx + n_lat
    depth = w_in.shape[0]
    d_ff = w_ffn_out.shape[1]
    assert n_ctx == TILE == GLA_SUB and n_lat % TILE == 0 and n_lat // GRID_W == 32
    ffn_half = d_ff // 2

    xs = jnp.concatenate([ctx, x], axis=1)
    c16 = jnp.zeros((2 * SUBLANES, d), F32).at[:nb].set(c).at[nb].set(c_ctx)
    mods = [_modulation(c16, w_mod[l], b_mod[l]).reshape(2 * SUBLANES * 6, 1, d) for l in range(depth)]
    h = _norm_mod(xs, norm_mix_w[0], mods[0], 0, 1, n_ctx)
    ml_post = jnp.concatenate([jnp.ones((ML_QK_WIDTH,), F32),
                               jnp.full((ML_QK_WIDTH,), ML_QK_DIM ** -0.5, F32)])
    for l in range(depth):
        last = l == depth - 1
        w_main, w_small = _proj_weights(w_in[l])
        h2d = h.reshape(nb * t, d)
        p3 = _matmul(h2d, w_main, N_PROJ // 5, BF16).reshape(nb, t, N_PROJ)
        sm3 = _matmul(h2d, w_small, LANES, F32).reshape(nb, t, LANES)
        cvs = _conv(p3, _P_OFF["s_x"], ssd_conv_w[l], ssd_conv_b[l],
                    jnp.ones((ssd_conv_w.shape[-1],), F32), n_ctx)
        cvm = _conv(p3, _P_OFF["m_q"], ml_conv_w[l], ml_conv_b[l], ml_post, n_ctx)
        d_e = jnp.repeat(ssd_d[l], SSD_HEAD_DIM).reshape(1, SSD_WIDTH)
        y = hm = og = None
        for rev in (False, True):
            k = int(rev)
            y = _ssd_scan(cvs, sm3, ssd_dt_bias[l, k], ssd_a_log[l, k], d_e, rev, y)
            hm = _mlstm_scan(cvm, p3, sm3, ml_i_bias[l, k], ml_f_bias[l, k], rev, hm)
            og = _gla_scan(p3, sm3, gla_a_up[l, k], gla_a_bias[l, k], n_ctx, rev, og)
        tile0 = n_ctx // TILE if last else 0
        xs, h2 = _post(xs, (y, hm, og), p3, (ssd_norm_w[l], ml_norm_w[l], gla_norm_w[l]),
                       (w_b_ssd[l].astype(BF16), w_b_ml[l].astype(BF16), w_b_gla[l].astype(BF16)),
                       w_out[l].astype(BF16), norm_ffn_w[l], mods[l], n_ctx, tile0)
        nt = xs.shape[1]
        a = _ffn_in(h2.reshape(nb * nt, d), _ffn_weight(w_ffn_in[l], ffn_half), ffn_half)
        a = a.reshape(nb, nt, d_ff)
        if last:
            return _ffn_out(a, w_ffn_out[l].astype(BF16), xs, mods[l], n_ctx, tile0, final_norm_w, None)
        xs, h = _ffn_out(a, w_ffn_out[l].astype(BF16), xs, mods[l], n_ctx, tile0,
                         norm_mix_w[l + 1], mods[l + 1])
```

```python
import functools

import numpy as np
import jax
import jax.numpy as jnp
from jax import lax
from jax.experimental import pallas as pl
from jax.experimental.pallas import tpu as pltpu

F32 = jnp.float32
BF16 = jnp.bfloat16

EPS = 1e-6
LOG2E = 1.4426950408889634
GRID_W = 64
SSD_HEADS = 16
SSD_HEAD_DIM = 64
SSD_WIDTH = 1024
SSD_STATE = 64
SSD_BC = 128
ML_HEADS = 4
ML_QK_DIM = 128
ML_V_DIM = 256
ML_QK_WIDTH = 512
ML_V_WIDTH = 1024
GLA_HEADS = 4
GLA_K_DIM = 128
GLA_V_DIM = 256
GLA_K_WIDTH = 512
GLA_V_WIDTH = 1024
GLA_RANK = 16
GLA_TAU = 16.0
GLA_CHUNK = 64

LANES = 128
SUBLANES = 8
BF16_SUBLANES = 16
VMEM_LIMIT = 56 * 1024 * 1024

TILE = 256

_IN_NAMES = ("s_x", "s_z", "s_b", "s_c", "dt_f", "dt_b",
             "m_q", "m_k", "m_v", "m_o", "i_f", "i_b", "f_f", "f_b",
             "g_q", "g_k", "g_v", "g_g", "a_f", "a_b",
             "gate_ssd", "gate_ml", "gate_gla")
_IN_WIDTHS = (1024, 1024, 128, 128, 16, 16,
              512, 512, 1024, 1024, 4, 4, 4, 4,
              512, 512, 1024, 1024, 16, 16,
              1024, 1024, 1024)
_IN_OFF = dict(zip(_IN_NAMES, np.concatenate([[0], np.cumsum(_IN_WIDTHS)[:-1]]).tolist()))
_IN_W = dict(zip(_IN_NAMES, _IN_WIDTHS))

_P_ORDER = ("s_z", "m_o", "g_g", "gate_ssd", "gate_ml", "gate_gla", "m_v", "g_v",
            "g_q", "g_k", "s_x", "s_b", "s_c", "m_q", "m_k")
_P_OFF = {}
_o = 0
for _n in _P_ORDER:
    _P_OFF[_n] = _o
    _o += _IN_W[_n]
N_PROJ = _o
_SMALL = ("dt_f", "dt_b", "i_f", "i_b", "f_f", "f_b", "a_f", "a_b")
_SM_OFF = {}
_s = 0
for _n in _SMALL:
    _SM_OFF[_n] = _s
    _s += _IN_W[_n]
N_SMALL_USED = _s


def _cparams(sem):
    return pltpu.CompilerParams(dimension_semantics=sem, vmem_limit_bytes=VMEM_LIMIT)


def _sigmoid(x):
    return 1.0 / (1.0 + jnp.exp(-x))


def _silu(x):
    return x * _sigmoid(x)


def _softplus(x):
    return jnp.maximum(x, 0.0) + jnp.log1p(jnp.exp(-jnp.abs(x)))


def _log_sigmoid(x):
    return -_softplus(-x)


def _split(x, n):
    out = []
    r = x
    for _ in range(n):
        p = r.astype(BF16)
        out.append(p)
        r = r - p.astype(F32)
    return out


def _dot(a, b):
    return jnp.dot(a, b, preferred_element_type=F32)


def _dot_nt(a, b):
    return lax.dot_general(a, b, (((1,), (1,)), ((), ())), preferred_element_type=F32)


def _dot_tn(a, b):
    return lax.dot_general(a, b, (((0,), (0,)), ((), ())), preferred_element_type=F32)


def _dot_x3(x, e):
    return sum(_dot(p, e) for p in _split(x, 3))


def _dot_3x(t, x):
    return sum(_dot(t, p) for p in _split(x, 3))


def _dot_hp(a, b):
    ah, am = _split(a, 2)
    bh, bm = _split(b, 2)
    return _dot(ah, bh) + _dot(ah, bm) + _dot(am, bh)


def _causal(n, rev):
    t = lax.broadcasted_iota(jnp.int32, (n, n), 0)
    s = lax.broadcasted_iota(jnp.int32, (n, n), 1)
    return (s >= t) if rev else (s <= t)


def _rms(x):
    return x * lax.rsqrt(jnp.mean(x * x, axis=-1, keepdims=True) + EPS)


def _mod_kernel(c_ref, w_ref, b_ref, o_ref):
    o_ref[...] = _dot_hp(_silu(c_ref[...]), w_ref[...]) + b_ref[...]


def _modulation(c16, w_mod, b_mod):
    rows, d = c16.shape
    n = w_mod.shape[1]
    tn = 1536
    return pl.pallas_call(
        _mod_kernel,
        out_shape=jax.ShapeDtypeStruct((rows, n), F32),
        grid=(n // tn,),
        in_specs=[pl.BlockSpec((rows, d), lambda j: (0, 0)),
                  pl.BlockSpec((d, tn), lambda j: (0, j)),
                  pl.BlockSpec((1, tn), lambda j: (0, j))],
        out_specs=pl.BlockSpec((rows, tn), lambda j: (0, j)),
        compiler_params=_cparams(("arbitrary",)),
        name="modulation",
    )(c16, w_mod, b_mod.reshape(1, n))


def _mod_row(nb, k, ctx_tiles, tile0=0):
    return lambda b, i: (jnp.where(i + tile0 < ctx_tiles, nb, b) * 6 + k, 0, 0)


def _norm_mod_kernel(x_ref, w_ref, sc_ref, sh_ref, o_ref):
    o_ref[0] = (_rms(x_ref[0]) * w_ref[...] * (1.0 + sc_ref[0]) + sh_ref[0]).astype(o_ref.dtype)


def _norm_mod(x, w, mods, k_shift, k_scale, n_ctx):
    nb, t, d = x.shape
    return pl.pallas_call(
        _norm_mod_kernel,
        out_shape=jax.ShapeDtypeStruct((nb, t, d), BF16),
        grid=(nb, t // TILE),
        in_specs=[pl.BlockSpec((1, TILE, d), lambda b, i: (b, i, 0)),
                  pl.BlockSpec((1, d), lambda b, i: (0, 0)),
                  pl.BlockSpec((1, 1, d), _mod_row(nb, k_scale, n_ctx // TILE)),
                  pl.BlockSpec((1, 1, d), _mod_row(nb, k_shift, n_ctx // TILE))],
        out_specs=pl.BlockSpec((1, TILE, d), lambda b, i: (b, i, 0)),
        compiler_params=_cparams(("parallel", "parallel")),
        name="norm_mod",
    )(x, w.reshape(1, d), mods, mods)


def _mm_kernel(a_ref, w_ref, o_ref):
    o_ref[...] = _dot(a_ref[...], w_ref[...]).astype(o_ref.dtype)


def _row_tile(m):
    return 512 if m % 512 == 0 else TILE


def _matmul(a, w, tn, out_dtype):
    m, k = a.shape
    n = w.shape[1]
    tm = _row_tile(m)
    return pl.pallas_call(
        _mm_kernel,
        out_shape=jax.ShapeDtypeStruct((m, n), out_dtype),
        grid=(n // tn, m // tm),
        in_specs=[pl.BlockSpec((tm, k), lambda j, i: (i, 0)),
                  pl.BlockSpec((k, tn), lambda j, i: (0, j))],
        out_specs=pl.BlockSpec((tm, tn), lambda j, i: (i, j)),
        compiler_params=_cparams(("parallel", "parallel")),
        name="matmul",
    )(a, w)


CONV_K = 5
CONV_ROWS = 256


def _conv_kernel(u_ref, w_ref, b_ref, s_ref, o_ref, pad_ref, *, n_ctx):
    t, c = u_ref.shape[1], u_ref.shape[2]
    half = CONV_K // 2
    zeros = jnp.zeros((SUBLANES, c), F32)
    w = w_ref[...]
    bias = b_ref[...]
    post = s_ref[...]
    for s0, n in ((0, n_ctx), (n_ctx, t - n_ctx)):
        pad_ref[0:SUBLANES, :] = zeros
        pad_ref[SUBLANES:SUBLANES + n, :] = u_ref[0, s0:s0 + n, :].astype(F32)
        pad_ref[SUBLANES + n:2 * SUBLANES + n, :] = zeros
        for r0 in range(0, n, CONV_ROWS):
            acc = bias
            for j in range(CONV_K):
                lo = SUBLANES - half + j + r0
                acc = acc + w[j:j + 1, :] * pad_ref[lo:lo + CONV_ROWS, :]
            o_ref[0, s0 + r0:s0 + r0 + CONV_ROWS, :] = (_silu(acc) * post).astype(o_ref.dtype)


def _conv(p3, col0, w, b, post_scale, n_ctx):
    nb, t, _ = p3.shape
    width = w.shape[1]
    cb = 256
    return pl.pallas_call(
        functools.partial(_conv_kernel, n_ctx=n_ctx),
        out_shape=jax.ShapeDtypeStruct((nb, t, width), BF16),
        grid=(nb, width // cb),
        in_specs=[pl.BlockSpec((1, t, cb), lambda b_, j: (b_, 0, col0 // cb + j)),
                  pl.BlockSpec((CONV_K, cb), lambda b_, j: (0, j)),
                  pl.BlockSpec((1, cb), lambda b_, j: (0, j)),
                  pl.BlockSpec((1, cb), lambda b_, j: (0, j))],
        out_specs=pl.BlockSpec((1, t, cb), lambda b_, j: (b_, 0, j)),
        scratch_shapes=[pltpu.VMEM((t + 2 * SUBLANES, cb), F32)],
        compiler_params=_cparams(("parallel", "parallel")),
        name="conv",
    )(p3, w, b.reshape(1, width), post_scale.reshape(1, width))


def _tile_order(n_tiles, rev, ctx_tiles=1):
    if rev:
        return lambda i: jnp.where(i < ctx_tiles, ctx_tiles - 1 - i, n_tiles - 1 + ctx_tiles - i)
    return lambda i: i


def _write(o_ref, idx, val, acc_ref):
    o_ref[idx] = val if acc_ref is None else acc_ref[idx] + val


SSD_PAIRS = SSD_HEADS // 2
SSD_B_OFF = SSD_HEADS
SSD_TILE = 256


def _ssd_gates(sm_ref, dtb_ref, alog_ref, n, both):
    lane = lax.broadcasted_iota(jnp.int32, (1, LANES), 1)
    dt = _softplus(sm_ref[0] + dtb_ref[...])
    la = dt * jnp.where(lane < 2 * SSD_HEADS, -jnp.exp(alog_ref[...]), 0.0)
    parts = _split(la, 3)
    upp = jnp.where(_causal(n, True), 1.0, 0.0).astype(BF16)
    cum = sum(_dot(upp, p) for p in parts)
    if both:
        low = jnp.where(_causal(n, False), 1.0, 0.0).astype(BF16)
        cum = jnp.where(lane < SSD_HEADS, sum(_dot(low, p) for p in parts), cum)
    return lane, dt, cum


def _expand(a, e):
    return sum(_dot(p, e) for p in _split(a, 2))


def _group_dup(v, g, lo):
    other = pltpu.roll(v, SSD_STATE, axis=1)
    return jnp.where(lo, v, other) if g == 0 else jnp.where(lo, other, v)


def _ssd_state_step(st_ref, j, bw, xp, elast, off):
    r = lax.broadcasted_iota(jnp.int32, (LANES, LANES), 0) < SSD_STATE
    c = lax.broadcasted_iota(jnp.int32, (LANES, LANES), 1) < SSD_HEAD_DIM
    dec = jnp.where(r, elast[:, off + 2 * j:off + 2 * j + 1], elast[:, off + 2 * j + 1:off + 2 * j + 2])
    st_ref[j] = jnp.where(r == c, dec * st_ref[j] + _dot_tn(bw.astype(BF16), xp), 0.0)


def _ssd_states_kernel(x_ref, bc_ref, sm_ref, dtb_ref, alog_ref, eb_ref, o_ref, st_ref):
    n = x_ref.shape[1]

    @pl.when(pl.program_id(1) == 0)
    def _init():
        st_ref[...] = jnp.zeros_like(st_ref)

    lane, dt, cum = _ssd_gates(sm_ref, dtb_ref, alog_ref, n, False)
    lo = lane < SSD_HEAD_DIM
    last = cum[0:1, :]
    wst = _expand(jnp.exp(last - cum) * dt, eb_ref[...])
    elast = jnp.exp(last)
    b128 = bc_ref[0, :, :SSD_BC].astype(F32)
    o_ref[0, 0] = st_ref[...].astype(o_ref.dtype)
    for j in range(SSD_PAIRS):
        sl = slice(LANES * j, LANES * (j + 1))
        bw = _group_dup(b128, j // (SSD_PAIRS // 2), lo) * wst[:, sl]
        _ssd_state_step(st_ref, j, bw, x_ref[0, :, sl], elast, SSD_B_OFF)


def _ssd_out_kernel(x_ref, bc_ref, sm_ref, dtb_ref, alog_ref, ef_ref, eb_ref, d_ref, sb_ref, o_ref, st_ref):
    n = x_ref.shape[1]

    @pl.when(pl.program_id(1) == 0)
    def _init():
        st_ref[...] = jnp.zeros_like(st_ref)

    lane, dt, cum = _ssd_gates(sm_ref, dtb_ref, alog_ref, n, True)
    lo = lane < SSD_HEAD_DIM
    is_f = lane < SSD_HEADS
    ldt = jnp.log(dt)
    dsum = jnp.log(dt + pltpu.roll(dt, LANES - SSD_B_OFF, axis=1))
    rt = (jnp.where(lane < 2 * SSD_HEADS, cum - ldt, pltpu.roll(dsum, 2 * SSD_HEADS, axis=1)) * LOG2E).T
    cum2 = cum * LOG2E
    last = jnp.where(is_f, cum[n - 1:n, :], cum[0:1, :])
    elast = jnp.exp(last)
    ecum = _split(jnp.exp(cum), 2)
    ecum_f = sum(_dot(p, ef_ref[...]) for p in ecum)
    ecum_b = sum(_dot(p, eb_ref[...]) for p in ecum)
    wst = _expand(jnp.exp(last - cum) * dt, ef_ref[...])
    bc = bc_ref[0].astype(F32)
    b128, c128 = bc[:, :SSD_BC], bc[:, SSD_BC:]
    b128_b = bc_ref[0, :, :SSD_BC]
    c128_b = bc_ref[0, :, SSD_BC:]
    ti = lax.broadcasted_iota(jnp.int32, (n, n), 0)
    si = lax.broadcasted_iota(jnp.int32, (n, n), 1)
    below, above = si < ti, si > ti
    zero_b = jnp.zeros((), BF16)
    for g in range(2):
        cg = jnp.where(lo if g == 0 else jnp.logical_not(lo), c128_b, zero_b)
        cb = _dot_nt(cg, b128_b)
        cdup = _group_dup(c128, g, lo)
        bdup = _group_dup(b128, g, lo)
        for j in range(g * (SSD_PAIRS // 2), (g + 1) * (SSD_PAIRS // 2)):
            ms = []
            for h in (2 * j, 2 * j + 1):
                e_f = cum2[:, h:h + 1] - rt[h:h + 1, :]
                e_b = cum2[:, SSD_B_OFF + h:SSD_B_OFF + h + 1] - rt[SSD_B_OFF + h:SSD_B_OFF + h + 1, :]
                e = jnp.where(below, e_f, jnp.where(above, e_b, rt[2 * SSD_HEADS + h:2 * SSD_HEADS + h + 1, :]))
                ms.append((cb * jnp.exp2(e)).astype(BF16))
            sl = slice(LANES * j, LANES * (j + 1))
            xp = x_ref[0, :, sl]
            cs_f = (cdup * ecum_f[:, sl]).astype(BF16)
            cs_b = (cdup * ecum_b[:, sl]).astype(BF16)
            lhs = jnp.concatenate(ms + [cs_f, cs_b], axis=1)
            rhs = jnp.concatenate([jnp.where(lo, xp, zero_b), jnp.where(lo, zero_b, xp),
                                   st_ref[j].astype(BF16), sb_ref[0, 0, j]], axis=0)
            o_ref[0, :, sl] = _dot(lhs, rhs) + d_ref[:, sl] * xp.astype(F32)
            _ssd_state_step(st_ref, j, bdup * wst[:, sl], xp, elast, 0)


def _ssd(cvs, sm3, dt_bias, a_log, d_e, n_ctx):
    nb, t, _ = cvs.shape
    tile = SSD_TILE
    nt = t // tile
    row = lambda v: jnp.zeros((1, LANES), F32).at[0, :2 * SSD_HEADS].set(v.reshape(-1))
    const = lambda b, i: (0, 0)
    specs = lambda order: [
        pl.BlockSpec((1, tile, SSD_WIDTH), lambda b, i: (b, order(i), 0)),
        pl.BlockSpec((1, tile, 2 * SSD_BC), lambda b, i: (b, order(i), SSD_WIDTH // (2 * SSD_BC))),
        pl.BlockSpec((1, tile, LANES), lambda b, i: (b, order(i), 0)),
        pl.BlockSpec((1, LANES), const),
        pl.BlockSpec((1, LANES), const)]
    st_block = (1, 1, SSD_PAIRS, LANES, LANES)
    args = (cvs, cvs, sm3, row(dt_bias), row(a_log))
    sel = np.zeros((2, LANES, SSD_WIDTH), np.float32)
    for h in range(SSD_HEADS):
        sel[0, h, h * SSD_HEAD_DIM:(h + 1) * SSD_HEAD_DIM] = 1.0
        sel[1, SSD_B_OFF + h, h * SSD_HEAD_DIM:(h + 1) * SSD_HEAD_DIM] = 1.0
    e_f, e_b = jnp.asarray(sel[0], BF16), jnp.asarray(sel[1], BF16)
    e_spec = pl.BlockSpec((LANES, SSD_WIDTH), const)
    bwd = _tile_order(nt, True, n_ctx // tile)
    states_b = pl.pallas_call(
        _ssd_states_kernel,
        out_shape=jax.ShapeDtypeStruct((nb, nt) + st_block[2:], BF16),
        grid=(nb, nt),
        in_specs=specs(bwd) + [e_spec],
        out_specs=pl.BlockSpec(st_block, lambda b, i: (b, bwd(i), 0, 0, 0)),
        scratch_shapes=[pltpu.VMEM(st_block[2:], F32)],
        compiler_params=_cparams(("parallel", "arbitrary")),
        name="ssd_states",
    )(*args, e_b)
    fwd = _tile_order(nt, False)
    return pl.pallas_call(
        _ssd_out_kernel,
        out_shape=jax.ShapeDtypeStruct((nb, t, SSD_WIDTH), F32),
        grid=(nb, nt),
        in_specs=specs(fwd) + [e_spec, e_spec, pl.BlockSpec((1, SSD_WIDTH), const),
                               pl.BlockSpec(st_block, lambda b, i: (b, i, 0, 0, 0))],
        out_specs=pl.BlockSpec((1, tile, SSD_WIDTH), lambda b, i: (b, i, 0)),
        scratch_shapes=[pltpu.VMEM(st_block[2:], F32)],
        compiler_params=_cparams(("parallel", "arbitrary")),
        name="ssd_out",
    )(*args, e_f, e_b, d_e, states_b)


def _mlstm_kernel(qk_ref, v_ref, sm_ref, ib_ref, fb_ref, *rest, rev, i_off, f_off):
    acc_ref, o_ref, ct_ref, n_ref, m_ref = rest if len(rest) == 5 else (None,) + rest
    n = qk_ref.shape[1]

    @pl.when(pl.program_id(1) == 0)
    def _init():
        ct_ref[...] = jnp.zeros_like(ct_ref)
        n_ref[...] = jnp.zeros_like(n_ref)
        m_ref[...] = jnp.zeros_like(m_ref)

    lane = lax.broadcasted_iota(jnp.int32, (1, LANES), 1)
    in_f = (lane >= f_off) & (lane < f_off + ML_HEADS)
    sm = sm_ref[0]
    li = sm + ib_ref[...]
    lf = jnp.where(in_f, _log_sigmoid(sm + fb_ref[...]), 0.0)
    causal = _causal(n, rev)
    tri = jnp.where(causal, 1.0, 0.0).astype(BF16)
    bcum = _dot_3x(tri, lf)
    bcum_t = bcum.T
    li_t = li.T
    last = 0 if rev else n - 1
    for h in range(ML_HEADS):
        qb = qk_ref[0, :, ML_QK_DIM * h:ML_QK_DIM * (h + 1)]
        kb = qk_ref[0, :, ML_QK_WIDTH + ML_QK_DIM * h:ML_QK_WIDTH + ML_QK_DIM * (h + 1)]
        vb = v_ref[0, :, ML_V_DIM * h:ML_V_DIM * (h + 1)]
        bcol = bcum[:, f_off + h:f_off + h + 1]
        brow = bcum_t[f_off + h:f_off + h + 1, :]
        licol = li[:, i_off + h:i_off + h + 1]
        lirow = li_t[i_off + h:i_off + h + 1, :]
        m_prev = m_ref[h:h + 1, 0:1]
        seg = jnp.where(causal, bcol - brow + lirow, -jnp.inf)
        inter = m_prev + bcol
        m_t = jnp.maximum(inter, jnp.max(seg, axis=1, keepdims=True))
        w = jnp.exp(seg - m_t)
        w_inter = jnp.exp(inter - m_t)
        s = _dot_nt(qb, kb) * w
        ct = ct_ref[h]
        nrow = n_ref[h:h + 1, :]
        num = _dot(s.astype(BF16), vb) + w_inter * _dot(qb, ct.astype(BF16))
        den = (jnp.sum(s, axis=1, keepdims=True)
               + w_inter * jnp.sum(qb.astype(F32) * nrow, axis=1, keepdims=True))
        hid = num / jnp.maximum(jnp.abs(den), jnp.exp(-m_t))
        _write(o_ref, (0, slice(None), slice(ML_V_DIM * h, ML_V_DIM * (h + 1))), hid, acc_ref)
        m_new = m_t[last:last + 1, :]
        b_last = bcol[last:last + 1, :]
        ws = jnp.exp(b_last - bcol + licol - m_new)
        keep = jnp.exp(m_prev + b_last - m_new)
        ct_ref[h] = keep * ct + _dot_tn(kb, (ws * vb.astype(F32)).astype(BF16))
        n_ref[h:h + 1, :] = keep * nrow + jnp.sum(ws * kb.astype(F32), axis=0, keepdims=True)
        m_ref[h:h + 1, :] = jnp.broadcast_to(m_new, (1, LANES))


def _mlstm_scan(cvm, p3, sm3, i_bias, f_bias, rev, acc):
    nb, t, _ = cvm.shape
    nt = t // TILE
    i_off = _SM_OFF["i_b"] if rev else _SM_OFF["i_f"]
    f_off = _SM_OFF["f_b"] if rev else _SM_OFF["f_f"]
    row = lambda v, off: jnp.zeros((1, LANES), F32).at[0, off:off + ML_HEADS].set(v)
    order = _tile_order(nt, rev)
    tok = lambda blk: (lambda b, i: (b, order(i), blk))
    const = lambda b, i: (0, 0)
    out_spec = pl.BlockSpec((1, TILE, ML_V_WIDTH), tok(0))
    in_specs = [pl.BlockSpec((1, TILE, 2 * ML_QK_WIDTH), tok(0)),
                pl.BlockSpec((1, TILE, ML_V_WIDTH), tok(_P_OFF["m_v"] // ML_V_WIDTH)),
                pl.BlockSpec((1, TILE, LANES), tok(0)),
                pl.BlockSpec((1, LANES), const),
                pl.BlockSpec((1, LANES), const)]
    args = [cvm, p3, sm3, row(i_bias, i_off), row(f_bias, f_off)]
    if acc is not None:
        in_specs.append(out_spec)
        args.append(acc)
    return pl.pallas_call(
        functools.partial(_mlstm_kernel, rev=rev, i_off=i_off, f_off=f_off),
        out_shape=jax.ShapeDtypeStruct((nb, t, ML_V_WIDTH), F32),
        grid=(nb, nt),
        in_specs=in_specs,
        out_specs=out_spec,
        scratch_shapes=[pltpu.VMEM((ML_HEADS, ML_QK_DIM, ML_V_DIM), F32),
                        pltpu.VMEM((SUBLANES, ML_QK_DIM), F32),
                        pltpu.VMEM((SUBLANES, LANES), F32)],
        input_output_aliases={} if acc is None else {len(args) - 1: 0},
        compiler_params=_cparams(("parallel", "arbitrary")),
        name="mlstm_scan",
    )(*args)


GLA_SUB = 256
GLA_NCH = GLA_SUB // GLA_CHUNK
GLA_COLS = BF16_SUBLANES


def _gla_layout(is_ctx, lat_rows):
    r = np.arange(GLA_SUB)
    if is_ctx:
        return r // GLA_CHUNK, r % GLA_CHUNK
    col = r % SUBLANES
    cpc = GLA_CHUNK // lat_rows
    return col // cpc, (col % cpc) * lat_rows + r // SUBLANES


def _gla_consts(is_ctx, lat_rows, rev):
    ch, pos = _gla_layout(is_ctx, lat_rows)
    same = ch[:, None] == ch[None, :]
    before = (pos[None, :] >= pos[:, None]) if rev else (pos[None, :] <= pos[:, None])
    tri = (same & before).astype(np.float32)
    cmask = np.stack([np.repeat((ch == j)[:, None], LANES, axis=1) for j in range(GLA_NCH)])
    return tri, cmask.astype(np.float32)


def _gla_row(is_ctx, lat_rows, j, p):
    ch, pos = _gla_layout(is_ctx, lat_rows)
    return int(np.nonzero((ch == j) & (pos == p))[0][0])


def _per_chunk_rows(b, is_ctx, lat_rows, p):
    rows = [b[_gla_row(is_ctx, lat_rows, j, p):_gla_row(is_ctx, lat_rows, j, p) + 1, :]
            for j in range(GLA_NCH)]
    w = b.shape[1]
    if is_ctx:
        full = jnp.concatenate([jnp.broadcast_to(r, (GLA_CHUNK, w)) for r in rows], axis=0)
    else:
        rep = SUBLANES // GLA_NCH
        pat = jnp.concatenate([jnp.broadcast_to(r, (rep, w)) for r in rows], axis=0)
        full = jnp.broadcast_to(pat[None], (GLA_SUB // SUBLANES, SUBLANES, w)).reshape(GLA_SUB, w)
    return rows, full


def _gla_sub(q, k, v, araw, aup, abias, tri_b, tri_f, cmask_ref, st_ref, *, rev, is_ctx, lat_rows):
    g = _log_sigmoid(_dot_hp(araw, aup) + abias) * (1.0 / GLA_TAU)
    b = _dot_3x(tri_b, g)
    _, ref = _per_chunk_rows(b, is_ctx, lat_rows, GLA_CHUNK // 2)
    lasts, last = _per_chunk_rows(b, is_ctx, lat_rows, 0 if rev else GLA_CHUNK - 1)
    qs = q * (GLA_K_DIM ** -0.5)
    qe = (qs * jnp.exp(b - ref)).astype(BF16)
    ke = (k * jnp.exp(ref - b)).astype(BF16)
    qb = (qs * jnp.exp(b)).astype(BF16)
    kl = (k * jnp.exp(last - b)).astype(BF16)
    visible = tri_f > 0.0
    order = range(GLA_NCH - 1, -1, -1) if rev else range(GLA_NCH)
    outs = []
    for h in range(GLA_HEADS):
        ks = slice(GLA_K_DIM * h, GLA_K_DIM * (h + 1))
        vh = v[:, GLA_V_DIM * h:GLA_V_DIM * (h + 1)]
        att = jnp.where(visible, _dot_nt(qe[:, ks], ke[:, ks]), 0.0).astype(BF16)
        o = _dot(att, vh)
        klm = jnp.concatenate([kl[:, ks] * cmask_ref[j].astype(BF16) for j in range(GLA_NCH)], axis=1)
        upd = _dot_tn(vh, klm)
        s = st_ref[h]
        s_in = [None] * GLA_NCH
        for j in order:
            s_in[j] = s.astype(BF16)
            s = s * jnp.exp(lasts[j][:, ks]) + upd[:, GLA_K_DIM * j:GLA_K_DIM * (j + 1)]
        st_ref[h] = s
        qbm = jnp.concatenate([qb[:, ks] * cmask_ref[j].astype(BF16) for j in range(GLA_NCH)], axis=1)
        outs.append(o + _dot_nt(qbm, jnp.concatenate(s_in, axis=1)))
    return jnp.concatenate(outs, axis=1)


def _gla_kernel(qc_ref, kc_ref, vc_ref, sc_ref, ql_ref, kl_ref, vl_ref, sl_ref,
                aup_ref, ab_ref, tcb_ref, tcf_ref, cmc_ref, tlb_ref, tlf_ref, cml_ref,
                *rest, rev, ctx_rows, lat_rows, n_cblk):
    acc_ref, o_ref, st_ref, ctxo_ref = rest if len(rest) == 4 else (None,) + rest
    i = pl.program_id(1)
    aup = aup_ref[...]
    abias = ab_ref[...]
    r0, r1 = ctx_rows, ctx_rows + lat_rows
    sub = functools.partial(_gla_sub, aup=aup, abias=abias, st_ref=st_ref, rev=rev, lat_rows=lat_rows)

    @pl.when(i == 0)
    def _ctx():
        st_ref[...] = jnp.zeros_like(st_ref)
        ctxo_ref[...] = sub(qc_ref[0].astype(F32), kc_ref[0].astype(F32), vc_ref[0], sc_ref[0],
                            tri_b=tcb_ref[...], tri_f=tcf_ref[...], cmask_ref=cmc_ref, is_ctx=True)

    @pl.when(i > 0)
    def _lat():
        cblk = (n_cblk - i) if rev else (i - 1)
        halves = range(GLA_COLS // SUBLANES)
        for half in (reversed(halves) if rev else halves):
            cs = slice(SUBLANES * half, SUBLANES * (half + 1))
            take = lambda r: r[0, r0:r1].astype(F32)[:, cs, :].reshape(GLA_SUB, r.shape[-1])
            o = sub(take(ql_ref), take(kl_ref), take(vl_ref).astype(BF16), take(sl_ref),
                    tri_b=tlb_ref[...], tri_f=tlf_ref[...], cmask_ref=cml_ref, is_ctx=False)
            _write(o_ref, (0, slice(r0, r1), cs, slice(None)),
                   o.reshape(lat_rows, SUBLANES, GLA_V_WIDTH), acc_ref)
        for r in range(ctx_rows):
            start = pl.multiple_of(r * GRID_W + cblk * GLA_COLS, GLA_COLS)
            _write(o_ref, (0, r, slice(None), slice(None)), ctxo_ref[pl.ds(start, GLA_COLS), :], acc_ref)


def _gla_scan(p3, sm3, a_up, a_bias, n_ctx, rev, acc):
    nb, t, ncol = p3.shape
    rows = t // GRID_W
    ctx_rows = n_ctx // GRID_W
    lat_rows = rows - ctx_rows
    n_cblk = GRID_W // GLA_COLS
    p4 = p3.reshape(nb, rows, GRID_W, ncol)
    sm4 = sm3.reshape(nb, rows, GRID_W, LANES)
    a_off = _SM_OFF["a_b"] if rev else _SM_OFF["a_f"]
    aup = jnp.zeros((LANES, GLA_K_WIDTH), F32).at[a_off:a_off + GLA_RANK, :].set(a_up)
    cblk = lambda i: jnp.where(i == 0, n_cblk - 1 if rev else 0, (n_cblk - i) if rev else (i - 1))
    ctx = lambda blk: (lambda b, i: (b, 0, blk))
    lat = lambda blk: (lambda b, i: (b, 0, cblk(i), blk))
    const2 = lambda b, i: (0, 0)
    const3 = lambda b, i: (0, 0, 0)
    widths = (GLA_K_WIDTH, GLA_K_WIDTH, GLA_V_WIDTH)
    offs = (_P_OFF["g_q"], _P_OFF["g_k"], _P_OFF["g_v"])
    consts = []
    const_specs = []
    for is_ctx in (True, False):
        tri, cmask = _gla_consts(is_ctx, lat_rows, rev)
        consts += [jnp.asarray(tri, BF16), jnp.asarray(tri, F32), jnp.asarray(cmask, F32)]
        const_specs += [pl.BlockSpec((GLA_SUB, GLA_SUB), const2), pl.BlockSpec((GLA_SUB, GLA_SUB), const2),
                        pl.BlockSpec((GLA_NCH, GLA_SUB, LANES), const3)]
    out_spec = pl.BlockSpec((1, rows, GLA_COLS, GLA_V_WIDTH), lat(0))
    in_specs = ([pl.BlockSpec((1, n_ctx, w), ctx(o // w)) for w, o in zip(widths, offs)]
                + [pl.BlockSpec((1, n_ctx, LANES), ctx(0))]
                + [pl.BlockSpec((1, rows, GLA_COLS, w), lat(o // w)) for w, o in zip(widths, offs)]
                + [pl.BlockSpec((1, rows, GLA_COLS, LANES), lat(0))]
                + [pl.BlockSpec((LANES, GLA_K_WIDTH), const2), pl.BlockSpec((1, GLA_K_WIDTH), const2)]
                + const_specs)
    args = [p3, p3, p3, sm3, p4, p4, p4, sm4, aup, a_bias.reshape(1, GLA_K_WIDTH)] + consts
    if acc is not None:
        in_specs.append(out_spec)
        args.append(acc.reshape(nb, rows, GRID_W, GLA_V_WIDTH))
    out = pl.pallas_call(
        functools.partial(_gla_kernel, rev=rev, ctx_rows=ctx_rows, lat_rows=lat_rows, n_cblk=n_cblk),
        out_shape=jax.ShapeDtypeStruct((nb, rows, GRID_W, GLA_V_WIDTH), F32),
        grid=(nb, n_cblk + 1),
        in_specs=in_specs,
        out_specs=out_spec,
        scratch_shapes=[pltpu.VMEM((GLA_HEADS, GLA_V_DIM, GLA_K_DIM), F32),
                        pltpu.VMEM((n_ctx, GLA_V_WIDTH), F32)],
        input_output_aliases={} if acc is None else {len(args) - 1: 0},
        compiler_params=_cparams(("parallel", "arbitrary")),
        name="gla_scan",
    )(*args)
    return out.reshape(nb, t, GLA_V_WIDTH)


def _group_rmsnorm(y, groups):
    width = y.shape[-1] // groups
    return jnp.concatenate([_rms(y[:, width * g:width * (g + 1)]) for g in range(groups)], axis=1)


def _post_kernel(x_ref, y_ref, h_ref, o_ref_in, z_ref, mo_ref, gg_ref, gs_ref, gm_ref, gl_ref,
                 nws_ref, nwm_ref, nwg_ref, wbs_ref, wbm_ref, wbg_ref, wout_ref, g1_ref,
                 nwf_ref, sc2_ref, sh2_ref, xo_ref, ho_ref):
    f = lambda r: r[0].astype(F32)
    y_ssd = (_group_rmsnorm(y_ref[0] * _silu(f(z_ref)), 2) * nws_ref[...]).astype(BF16)
    y_ml = (_group_rmsnorm(h_ref[0], ML_HEADS) * nwm_ref[...] * _sigmoid(f(mo_ref))).astype(BF16)
    y_gla = (_group_rmsnorm(o_ref_in[0], GLA_HEADS) * nwg_ref[...] * _silu(f(gg_ref))).astype(BF16)
    merged = (_sigmoid(f(gs_ref)) * _dot(y_ssd, wbs_ref[...])
              + _sigmoid(f(gm_ref)) * _dot(y_ml, wbm_ref[...])
              + _sigmoid(f(gl_ref)) * _dot(y_gla, wbg_ref[...]))
    x_new = x_ref[0] + g1_ref[0] * _dot(merged.astype(BF16), wout_ref[...])
    xo_ref[0] = x_new
    ho_ref[0] = (_rms(x_new) * nwf_ref[...] * (1.0 + sc2_ref[0]) + sh2_ref[0]).astype(ho_ref.dtype)


def _post(x, scans, p3, norm_ws, w_bs, w_out, norm_ffn_w, mods, n_ctx, tile0):
    nb, t, d = x.shape
    nt = t // TILE - tile0
    ctx_tiles = n_ctx // TILE
    tok = lambda blk: (lambda b, i: (b, i + tile0, blk))
    out = lambda b, i: (b, i, 0)
    const = lambda b, i: (0, 0)
    tok_spec = lambda blk: pl.BlockSpec((1, TILE, d), tok(blk))
    w_spec = pl.BlockSpec((d, d), const, pipeline_mode=pl.Buffered(1))
    vec = pl.BlockSpec((1, d), const)
    mod = lambda k: pl.BlockSpec((1, 1, d), _mod_row(nb, k, ctx_tiles, tile0))
    names = ("s_z", "m_o", "g_g", "gate_ssd", "gate_ml", "gate_gla")
    in_specs = ([tok_spec(0)] * 4 + [tok_spec(_P_OFF[nm] // d) for nm in names]
                + [vec] * 3 + [w_spec] * 4 + [mod(2), vec, mod(4), mod(3)])
    return pl.pallas_call(
        _post_kernel,
        out_shape=(jax.ShapeDtypeStruct((nb, nt * TILE, d), F32),
                   jax.ShapeDtypeStruct((nb, nt * TILE, d), BF16)),
        grid=(nb, nt),
        in_specs=in_specs,
        out_specs=(pl.BlockSpec((1, TILE, d), out), pl.BlockSpec((1, TILE, d), out)),
        compiler_params=_cparams(("parallel", "parallel")),
        name="post",
    )(x, *scans, *([p3] * 6), *[w.reshape(1, d) for w in norm_ws], *w_bs, w_out, mods,
      norm_ffn_w.reshape(1, d), mods, mods)


def _ffn_in_kernel(a_ref, w_ref, o_ref):
    acc = _dot(a_ref[...], w_ref[...])
    half = acc.shape[1] // 2
    o_ref[...] = (_silu(acc[:, :half]) * acc[:, half:]).astype(o_ref.dtype)


def _ffn_in(h, w_gu, half):
    m, k = h.shape
    n_half = w_gu.shape[1] // 2
    tm = _row_tile(m)
    return pl.pallas_call(
        _ffn_in_kernel,
        out_shape=jax.ShapeDtypeStruct((m, n_half), BF16),
        grid=(n_half // half, m // tm),
        in_specs=[pl.BlockSpec((tm, k), lambda j, i: (i, 0)),
                  pl.BlockSpec((k, 2 * half), lambda j, i: (0, j))],
        out_specs=pl.BlockSpec((tm, half), lambda j, i: (i, j)),
        compiler_params=_cparams(("parallel", "parallel")),
        name="ffn_in",
    )(h, w_gu)


def _ffn_out_kernel(a_ref, w_ref, x_ref, g_ref, nw_ref, sc_ref, sh_ref, xo_ref, ho_ref):
    x_new = x_ref[0] + g_ref[0] * _dot(a_ref[0], w_ref[...])
    xo_ref[0] = x_new
    ho_ref[0] = (_rms(x_new) * nw_ref[...] * (1.0 + sc_ref[0]) + sh_ref[0]).astype(ho_ref.dtype)


def _ffn_out_last_kernel(a_ref, w_ref, x_ref, g_ref, nw_ref, o_ref):
    x_new = x_ref[0] + g_ref[0] * _dot(a_ref[0], w_ref[...])
    o_ref[0] = _rms(x_new) * nw_ref[...]


def _ffn_out(a, w, x, mods, n_ctx, tile0, next_norm_w, next_mods):
    nb, t, d = x.shape
    k = a.shape[-1]
    ctx_tiles = n_ctx // TILE
    tok = lambda b, i: (b, i, 0)
    mod = lambda k_, m=None: pl.BlockSpec((1, 1, d), _mod_row(nb, k_, ctx_tiles, tile0))
    in_specs = [pl.BlockSpec((1, TILE, k), tok),
                pl.BlockSpec((k, d), lambda b, i: (0, 0), pipeline_mode=pl.Buffered(1)),
                pl.BlockSpec((1, TILE, d), tok),
                mod(5),
                pl.BlockSpec((1, d), lambda b, i: (0, 0))]
    args = [a, w, x, mods, next_norm_w.reshape(1, d)]
    if next_mods is None:
        body = _ffn_out_last_kernel
        out_shape = jax.ShapeDtypeStruct((nb, t, d), F32)
        out_specs = pl.BlockSpec((1, TILE, d), tok)
    else:
        body = _ffn_out_kernel
        in_specs += [mod(1), mod(0)]
        args += [next_mods, next_mods]
        out_shape = (jax.ShapeDtypeStruct((nb, t, d), F32), jax.ShapeDtypeStruct((nb, t, d), BF16))
        out_specs = (pl.BlockSpec((1, TILE, d), tok), pl.BlockSpec((1, TILE, d), tok))
    return pl.pallas_call(
        body,
        out_shape=out_shape,
        grid=(nb, t // TILE),
        in_specs=in_specs,
        out_specs=out_specs,
        compiler_params=_cparams(("parallel", "parallel")),
        name="ffn_out",
    )(*args)


def _proj_weights(w_in):
    main = jnp.concatenate([w_in[:, _IN_OFF[nm]:_IN_OFF[nm] + _IN_W[nm]] for nm in _P_ORDER], axis=1)
    small = [w_in[:, _IN_OFF[nm]:_IN_OFF[nm] + _IN_W[nm]] for nm in _SMALL]
    small.append(jnp.zeros((w_in.shape[0], LANES - N_SMALL_USED), w_in.dtype))
    return main.astype(BF16), jnp.concatenate(small, axis=1).astype(BF16)


def _ffn_weight(w_ffn_in, half):
    d_ff = w_ffn_in.shape[1] // 2
    cols = []
    for j in range(d_ff // half):
        cols.append(w_ffn_in[:, j * half:(j + 1) * half])
        cols.append(w_ffn_in[:, d_ff + j * half:d_ff + (j + 1) * half])
    return jnp.concatenate(cols, axis=1).astype(BF16)


def kernel(x, c, ctx, c_ctx, w_mod, b_mod, norm_mix_w, norm_ffn_w, w_in, ssd_conv_w, ssd_conv_b, ssd_dt_bias, ssd_a_log, ssd_d, ssd_norm_w, ml_conv_w, ml_conv_b, ml_i_bias, ml_f_bias, ml_norm_w, gla_a_up, gla_a_bias, gla_norm_w, w_b_ssd, w_b_ml, w_b_gla, w_out, w_ffn_in, w_ffn_out, final_norm_w):
    nb, n_lat, d = x.shape
    n_ctx = ctx.shape[1]
    t = n_ctx + n_lat
    depth = w_in.shape[0]
    d_ff = w_ffn_out.shape[1]
    assert n_ctx == TILE == GLA_SUB and n_lat % TILE == 0 and n_lat // GRID_W == 32
    ffn_half = d_ff // 2

    xs = jnp.concatenate([ctx, x], axis=1)
    c16 = jnp.zeros((2 * SUBLANES, d), F32).at[:nb].set(c).at[nb].set(c_ctx)
    mods = [_modulation(c16, w_mod[l], b_mod[l]).reshape(2 * SUBLANES * 6, 1, d) for l in range(depth)]
    h = _norm_mod(xs, norm_mix_w[0], mods[0], 0, 1, n_ctx)
    ml_post = jnp.concatenate([jnp.ones((ML_QK_WIDTH,), F32),
                               jnp.full((ML_QK_WIDTH,), ML_QK_DIM ** -0.5, F32)])
    for l in range(depth):
        last = l == depth - 1
        w_main, w_small = _proj_weights(w_in[l])
        h2d = h.reshape(nb * t, d)
        p3 = _matmul(h2d, w_main, N_PROJ // 5, BF16).reshape(nb, t, N_PROJ)
        sm3 = _matmul(h2d, w_small, LANES, F32).reshape(nb, t, LANES)
        cvs = _conv(p3, _P_OFF["s_x"], ssd_conv_w[l], ssd_conv_b[l],
                    jnp.ones((ssd_conv_w.shape[-1],), F32), n_ctx)
        cvm = _conv(p3, _P_OFF["m_q"], ml_conv_w[l], ml_conv_b[l], ml_post, n_ctx)
        d_e = jnp.repeat(ssd_d[l], SSD_HEAD_DIM).reshape(1, SSD_WIDTH)
        y = _ssd(cvs, sm3, ssd_dt_bias[l], ssd_a_log[l], d_e, n_ctx)
        hm = og = None
        for rev in (False, True):
            k = int(rev)
            hm = _mlstm_scan(cvm, p3, sm3, ml_i_bias[l, k], ml_f_bias[l, k], rev, hm)
            og = _gla_scan(p3, sm3, gla_a_up[l, k], gla_a_bias[l, k], n_ctx, rev, og)
        tile0 = n_ctx // TILE if last else 0
        xs, h2 = _post(xs, (y, hm, og), p3, (ssd_norm_w[l], ml_norm_w[l], gla_norm_w[l]),
                       (w_b_ssd[l].astype(BF16), w_b_ml[l].astype(BF16), w_b_gla[l].astype(BF16)),
                       w_out[l].astype(BF16), norm_ffn_w[l], mods[l], n_ctx, tile0)
        nt = xs.shape[1]
        a = _ffn_in(h2.reshape(nb * nt, d), _ffn_weight(w_ffn_in[l], ffn_half), ffn_half)
        a = a.reshape(nb, nt, d_ff)
        if last:
            return _ffn_out(a, w_ffn_out[l].astype(BF16), xs, mods[l], n_ctx, tile0, final_norm_w, None)
        xs, h = _ffn_out(a, w_ffn_out[l].astype(BF16), xs, mods[l], n_ctx, tile0,
                         norm_mix_w[l + 1], mods[l + 1])
```

```python
import functools

import numpy as np
import jax
import jax.numpy as jnp
from jax import lax
from jax.experimental import pallas as pl
from jax.experimental.pallas import tpu as pltpu

F32 = jnp.float32
BF16 = jnp.bfloat16

EPS = 1e-6
LOG2E = 1.4426950408889634
GRID_W = 64
SSD_HEADS = 16
SSD_HEAD_DIM = 64
SSD_WIDTH = 1024
SSD_STATE = 64
SSD_BC = 128
ML_HEADS = 4
ML_QK_DIM = 128
ML_V_DIM = 256
ML_QK_WIDTH = 512
ML_V_WIDTH = 1024
GLA_HEADS = 4
GLA_K_DIM = 128
GLA_V_DIM = 256
GLA_K_WIDTH = 512
GLA_V_WIDTH = 1024
GLA_RANK = 16
GLA_TAU = 16.0
GLA_CHUNK = 64

LANES = 128
SUBLANES = 8
BF16_SUBLANES = 16
VMEM_LIMIT = 56 * 1024 * 1024

TILE = 256

_IN_NAMES = ("s_x", "s_z", "s_b", "s_c", "dt_f", "dt_b",
             "m_q", "m_k", "m_v", "m_o", "i_f", "i_b", "f_f", "f_b",
             "g_q", "g_k", "g_v", "g_g", "a_f", "a_b",
             "gate_ssd", "gate_ml", "gate_gla")
_IN_WIDTHS = (1024, 1024, 128, 128, 16, 16,
              512, 512, 1024, 1024, 4, 4, 4, 4,
              512, 512, 1024, 1024, 16, 16,
              1024, 1024, 1024)
_IN_OFF = dict(zip(_IN_NAMES, np.concatenate([[0], np.cumsum(_IN_WIDTHS)[:-1]]).tolist()))
_IN_W = dict(zip(_IN_NAMES, _IN_WIDTHS))

_P_ORDER = ("s_z", "m_o", "g_g", "gate_ssd", "gate_ml", "gate_gla", "m_v", "g_v",
            "g_q", "g_k", "s_x", "s_b", "s_c", "m_q", "m_k")
_P_OFF = {}
_o = 0
for _n in _P_ORDER:
    _P_OFF[_n] = _o
    _o += _IN_W[_n]
N_PROJ = _o
_SMALL = ("dt_f", "dt_b", "i_f", "i_b", "f_f", "f_b", "a_f", "a_b")
_SM_OFF = {}
_s = 0
for _n in _SMALL:
    _SM_OFF[_n] = _s
    _s += _IN_W[_n]
N_SMALL_USED = _s


def _cparams(sem):
    return pltpu.CompilerParams(dimension_semantics=sem, vmem_limit_bytes=VMEM_LIMIT)


def _sigmoid(x):
    return 1.0 / (1.0 + jnp.exp(-x))


def _silu(x):
    return x * _sigmoid(x)


def _softplus(x):
    return jnp.maximum(x, 0.0) + jnp.log1p(jnp.exp(-jnp.abs(x)))


def _log_sigmoid(x):
    return jnp.minimum(x, 0.0) - jnp.log(1.0 + jnp.exp(-jnp.abs(x)))


def _split(x, n):
    out = []
    r = x
    for _ in range(n):
        p = r.astype(BF16)
        out.append(p)
        r = r - p.astype(F32)
    return out


def _dot(a, b):
    return jnp.dot(a, b, preferred_element_type=F32)


def _dot_nt(a, b):
    return lax.dot_general(a, b, (((1,), (1,)), ((), ())), preferred_element_type=F32)


def _dot_tn(a, b):
    return lax.dot_general(a, b, (((0,), (0,)), ((), ())), preferred_element_type=F32)


def _dot_x3(x, e):
    return sum(_dot(p, e) for p in _split(x, 3))


def _dot_3x(t, x):
    return sum(_dot(t, p) for p in _split(x, 3))


def _dot_hp(a, b):
    ah, am = _split(a, 2)
    bh, bm = _split(b, 2)
    return _dot(ah, bh) + _dot(ah, bm) + _dot(am, bh)


def _causal(n, rev):
    t = lax.broadcasted_iota(jnp.int32, (n, n), 0)
    s = lax.broadcasted_iota(jnp.int32, (n, n), 1)
    return (s >= t) if rev else (s <= t)


def _rms(x):
    return x * lax.rsqrt(jnp.mean(x * x, axis=-1, keepdims=True) + EPS)


def _mod_kernel(c_ref, w_ref, b_ref, o_ref):
    o_ref[...] = _dot_hp(_silu(c_ref[...]), w_ref[...]) + b_ref[...]


def _modulation(c16, w_mod, b_mod):
    rows, d = c16.shape
    n = w_mod.shape[1]
    tn = 1536
    return pl.pallas_call(
        _mod_kernel,
        out_shape=jax.ShapeDtypeStruct((rows, n), F32),
        grid=(n // tn,),
        in_specs=[pl.BlockSpec((rows, d), lambda j: (0, 0)),
                  pl.BlockSpec((d, tn), lambda j: (0, j)),
                  pl.BlockSpec((1, tn), lambda j: (0, j))],
        out_specs=pl.BlockSpec((rows, tn), lambda j: (0, j)),
        compiler_params=_cparams(("arbitrary",)),
        name="modulation",
    )(c16, w_mod, b_mod.reshape(1, n))


def _mod_row(nb, k, ctx_tiles, tile0=0):
    return lambda b, i: (jnp.where(i + tile0 < ctx_tiles, nb, b) * 6 + k, 0, 0)


def _norm_mod_kernel(x_ref, w_ref, sc_ref, sh_ref, o_ref):
    o_ref[0] = (_rms(x_ref[0]) * w_ref[...] * (1.0 + sc_ref[0]) + sh_ref[0]).astype(o_ref.dtype)


def _norm_mod(x, w, mods, k_shift, k_scale, n_ctx):
    nb, t, d = x.shape
    return pl.pallas_call(
        _norm_mod_kernel,
        out_shape=jax.ShapeDtypeStruct((nb, t, d), BF16),
        grid=(nb, t // TILE),
        in_specs=[pl.BlockSpec((1, TILE, d), lambda b, i: (b, i, 0)),
                  pl.BlockSpec((1, d), lambda b, i: (0, 0)),
                  pl.BlockSpec((1, 1, d), _mod_row(nb, k_scale, n_ctx // TILE)),
                  pl.BlockSpec((1, 1, d), _mod_row(nb, k_shift, n_ctx // TILE))],
        out_specs=pl.BlockSpec((1, TILE, d), lambda b, i: (b, i, 0)),
        compiler_params=_cparams(("parallel", "parallel")),
        name="norm_mod",
    )(x, w.reshape(1, d), mods, mods)


def _mm_kernel(a_ref, w_ref, o_ref):
    o_ref[...] = _dot(a_ref[...], w_ref[...]).astype(o_ref.dtype)


def _row_tile(m):
    return 512 if m % 512 == 0 else TILE


def _matmul(a, w, tn, out_dtype):
    m, k = a.shape
    n = w.shape[1]
    tm = _row_tile(m)
    return pl.pallas_call(
        _mm_kernel,
        out_shape=jax.ShapeDtypeStruct((m, n), out_dtype),
        grid=(n // tn, m // tm),
        in_specs=[pl.BlockSpec((tm, k), lambda j, i: (i, 0)),
                  pl.BlockSpec((k, tn), lambda j, i: (0, j))],
        out_specs=pl.BlockSpec((tm, tn), lambda j, i: (i, j)),
        compiler_params=_cparams(("parallel", "parallel")),
        name="matmul",
    )(a, w)


CONV_K = 5
CONV_ROWS = 256


def _conv_kernel(u_ref, w_ref, b_ref, s_ref, o_ref, pad_ref, *, n_ctx):
    t, c = u_ref.shape[1], u_ref.shape[2]
    half = CONV_K // 2
    zeros = jnp.zeros((SUBLANES, c), F32)
    w = w_ref[...]
    bias = b_ref[...]
    post = s_ref[...]
    for s0, n in ((0, n_ctx), (n_ctx, t - n_ctx)):
        pad_ref[0:SUBLANES, :] = zeros
        pad_ref[SUBLANES:SUBLANES + n, :] = u_ref[0, s0:s0 + n, :].astype(F32)
        pad_ref[SUBLANES + n:2 * SUBLANES + n, :] = zeros
        for r0 in range(0, n, CONV_ROWS):
            acc = bias
            for j in range(CONV_K):
                lo = SUBLANES - half + j + r0
                acc = acc + w[j:j + 1, :] * pad_ref[lo:lo + CONV_ROWS, :]
            o_ref[0, s0 + r0:s0 + r0 + CONV_ROWS, :] = (_silu(acc) * post).astype(o_ref.dtype)


def _conv(p3, col0, w, b, post_scale, n_ctx):
    nb, t, _ = p3.shape
    width = w.shape[1]
    cb = 256
    return pl.pallas_call(
        functools.partial(_conv_kernel, n_ctx=n_ctx),
        out_shape=jax.ShapeDtypeStruct((nb, t, width), BF16),
        grid=(nb, width // cb),
        in_specs=[pl.BlockSpec((1, t, cb), lambda b_, j: (b_, 0, col0 // cb + j)),
                  pl.BlockSpec((CONV_K, cb), lambda b_, j: (0, j)),
                  pl.BlockSpec((1, cb), lambda b_, j: (0, j)),
                  pl.BlockSpec((1, cb), lambda b_, j: (0, j))],
        out_specs=pl.BlockSpec((1, t, cb), lambda b_, j: (b_, 0, j)),
        scratch_shapes=[pltpu.VMEM((t + 2 * SUBLANES, cb), F32)],
        compiler_params=_cparams(("parallel", "parallel")),
        name="conv",
    )(p3, w, b.reshape(1, width), post_scale.reshape(1, width))


def _tile_order(n_tiles, rev, ctx_tiles=1):
    if rev:
        return lambda i: jnp.where(i < ctx_tiles, ctx_tiles - 1 - i, n_tiles - 1 + ctx_tiles - i)
    return lambda i: i


def _write(o_ref, idx, val, acc_ref):
    o_ref[idx] = val if acc_ref is None else acc_ref[idx] + val


SSD_PAIRS = SSD_HEADS // 2
SSD_B_OFF = SSD_HEADS
SSD_TILE = 256


def _ssd_gates(sm_ref, dtb_ref, alog_ref, n, both):
    lane = lax.broadcasted_iota(jnp.int32, (1, LANES), 1)
    dt = _softplus(sm_ref[0] + dtb_ref[...])
    la = dt * jnp.where(lane < 2 * SSD_HEADS, -jnp.exp(alog_ref[...]), 0.0)
    parts = _split(la, 3)
    upp = jnp.where(_causal(n, True), 1.0, 0.0).astype(BF16)
    cum = sum(_dot(upp, p) for p in parts)
    if both:
        low = jnp.where(_causal(n, False), 1.0, 0.0).astype(BF16)
        cum = jnp.where(lane < SSD_HEADS, sum(_dot(low, p) for p in parts), cum)
    return lane, dt, cum


def _expand(a, e):
    return sum(_dot(p, e) for p in _split(a, 2))


def _group_dup(v, g, lo):
    other = pltpu.roll(v, SSD_STATE, axis=1)
    return jnp.where(lo, v, other) if g == 0 else jnp.where(lo, other, v)


def _ssd_state_step(st_ref, j, bw, xp, elast, off):
    r = lax.broadcasted_iota(jnp.int32, (LANES, LANES), 0) < SSD_STATE
    c = lax.broadcasted_iota(jnp.int32, (LANES, LANES), 1) < SSD_HEAD_DIM
    dec = jnp.where(r, elast[:, off + 2 * j:off + 2 * j + 1], elast[:, off + 2 * j + 1:off + 2 * j + 2])
    st_ref[j] = jnp.where(r == c, dec * st_ref[j] + _dot_tn(bw.astype(BF16), xp), 0.0)


def _ssd_states_kernel(x_ref, bc_ref, sm_ref, dtb_ref, alog_ref, eb_ref, o_ref, st_ref):
    n = x_ref.shape[1]

    @pl.when(pl.program_id(1) == 0)
    def _init():
        st_ref[...] = jnp.zeros_like(st_ref)

    lane, dt, cum = _ssd_gates(sm_ref, dtb_ref, alog_ref, n, False)
    lo = lane < SSD_HEAD_DIM
    last = cum[0:1, :]
    wst = _expand(jnp.exp(last - cum) * dt, eb_ref[...])
    elast = jnp.exp(last)
    b128 = bc_ref[0, :, :SSD_BC].astype(F32)
    o_ref[0, 0] = st_ref[...].astype(o_ref.dtype)
    for j in range(SSD_PAIRS):
        sl = slice(LANES * j, LANES * (j + 1))
        bw = _group_dup(b128, j // (SSD_PAIRS // 2), lo) * wst[:, sl]
        _ssd_state_step(st_ref, j, bw, x_ref[0, :, sl], elast, SSD_B_OFF)


def _ssd_out_kernel(x_ref, bc_ref, sm_ref, dtb_ref, alog_ref, ef_ref, eb_ref, d_ref, sb_ref, o_ref, st_ref):
    n = x_ref.shape[1]

    @pl.when(pl.program_id(1) == 0)
    def _init():
        st_ref[...] = jnp.zeros_like(st_ref)

    lane, dt, cum = _ssd_gates(sm_ref, dtb_ref, alog_ref, n, True)
    lo = lane < SSD_HEAD_DIM
    is_f = lane < SSD_HEADS
    ldt = jnp.log(dt)
    dsum = jnp.log(dt + pltpu.roll(dt, LANES - SSD_B_OFF, axis=1))
    rt = (jnp.where(lane < 2 * SSD_HEADS, cum - ldt, pltpu.roll(dsum, 2 * SSD_HEADS, axis=1)) * LOG2E).T
    cum2 = cum * LOG2E
    last = jnp.where(is_f, cum[n - 1:n, :], cum[0:1, :])
    elast = jnp.exp(last)
    ecum = _split(jnp.exp(cum), 2)
    ecum_f = sum(_dot(p, ef_ref[...]) for p in ecum)
    ecum_b = sum(_dot(p, eb_ref[...]) for p in ecum)
    wst = _expand(jnp.exp(last - cum) * dt, ef_ref[...])
    bc = bc_ref[0].astype(F32)
    b128, c128 = bc[:, :SSD_BC], bc[:, SSD_BC:]
    b128_b = bc_ref[0, :, :SSD_BC]
    c128_b = bc_ref[0, :, SSD_BC:]
    ti = lax.broadcasted_iota(jnp.int32, (n, n), 0)
    si = lax.broadcasted_iota(jnp.int32, (n, n), 1)
    below, above = si < ti, si > ti
    zero_b = jnp.zeros((), BF16)
    for g in range(2):
        cg = jnp.where(lo if g == 0 else jnp.logical_not(lo), c128_b, zero_b)
        cb = _dot_nt(cg, b128_b)
        cdup = _group_dup(c128, g, lo)
        bdup = _group_dup(b128, g, lo)
        for j in range(g * (SSD_PAIRS // 2), (g + 1) * (SSD_PAIRS // 2)):
            ms = []
            for h in (2 * j, 2 * j + 1):
                e_f = cum2[:, h:h + 1] - rt[h:h + 1, :]
                e_b = cum2[:, SSD_B_OFF + h:SSD_B_OFF + h + 1] - rt[SSD_B_OFF + h:SSD_B_OFF + h + 1, :]
                e = jnp.where(below, e_f, jnp.where(above, e_b, rt[2 * SSD_HEADS + h:2 * SSD_HEADS + h + 1, :]))
                ms.append((cb * jnp.exp2(e)).astype(BF16))
            sl = slice(LANES * j, LANES * (j + 1))
            xp = x_ref[0, :, sl]
            cs_f = (cdup * ecum_f[:, sl]).astype(BF16)
            cs_b = (cdup * ecum_b[:, sl]).astype(BF16)
            lhs = jnp.concatenate(ms + [cs_f, cs_b], axis=1)
            rhs = jnp.concatenate([jnp.where(lo, xp, zero_b), jnp.where(lo, zero_b, xp),
                                   st_ref[j].astype(BF16), sb_ref[0, 0, j]], axis=0)
            o_ref[0, :, sl] = _dot(lhs, rhs) + d_ref[:, sl] * xp.astype(F32)
            _ssd_state_step(st_ref, j, bdup * wst[:, sl], xp, elast, 0)


def _ssd(cvs, sm3, dt_bias, a_log, d_e, n_ctx):
    nb, t, _ = cvs.shape
    tile = SSD_TILE
    nt = t // tile
    row = lambda v: jnp.zeros((1, LANES), F32).at[0, :2 * SSD_HEADS].set(v.reshape(-1))
    const = lambda b, i: (0, 0)
    specs = lambda order: [
        pl.BlockSpec((1, tile, SSD_WIDTH), lambda b, i: (b, order(i), 0)),
        pl.BlockSpec((1, tile, 2 * SSD_BC), lambda b, i: (b, order(i), SSD_WIDTH // (2 * SSD_BC))),
        pl.BlockSpec((1, tile, LANES), lambda b, i: (b, order(i), 0)),
        pl.BlockSpec((1, LANES), const),
        pl.BlockSpec((1, LANES), const)]
    st_block = (1, 1, SSD_PAIRS, LANES, LANES)
    args = (cvs, cvs, sm3, row(dt_bias), row(a_log))
    sel = np.zeros((2, LANES, SSD_WIDTH), np.float32)
    for h in range(SSD_HEADS):
        sel[0, h, h * SSD_HEAD_DIM:(h + 1) * SSD_HEAD_DIM] = 1.0
        sel[1, SSD_B_OFF + h, h * SSD_HEAD_DIM:(h + 1) * SSD_HEAD_DIM] = 1.0
    e_f, e_b = jnp.asarray(sel[0], BF16), jnp.asarray(sel[1], BF16)
    e_spec = pl.BlockSpec((LANES, SSD_WIDTH), const)
    bwd = _tile_order(nt, True, n_ctx // tile)
    states_b = pl.pallas_call(
        _ssd_states_kernel,
        out_shape=jax.ShapeDtypeStruct((nb, nt) + st_block[2:], BF16),
        grid=(nb, nt),
        in_specs=specs(bwd) + [e_spec],
        out_specs=pl.BlockSpec(st_block, lambda b, i: (b, bwd(i), 0, 0, 0)),
        scratch_shapes=[pltpu.VMEM(st_block[2:], F32)],
        compiler_params=_cparams(("parallel", "arbitrary")),
        name="ssd_states",
    )(*args, e_b)
    fwd = _tile_order(nt, False)
    return pl.pallas_call(
        _ssd_out_kernel,
        out_shape=jax.ShapeDtypeStruct((nb, t, SSD_WIDTH), F32),
        grid=(nb, nt),
        in_specs=specs(fwd) + [e_spec, e_spec, pl.BlockSpec((1, SSD_WIDTH), const),
                               pl.BlockSpec(st_block, lambda b, i: (b, i, 0, 0, 0))],
        out_specs=pl.BlockSpec((1, tile, SSD_WIDTH), lambda b, i: (b, i, 0)),
        scratch_shapes=[pltpu.VMEM(st_block[2:], F32)],
        compiler_params=_cparams(("parallel", "arbitrary")),
        name="ssd_out",
    )(*args, e_f, e_b, d_e, states_b)


ML_GATE = 32
ML_ND = 2 * ML_HEADS
ML_AUG = ML_V_DIM + LANES


def _ml_gates(sm_ref, ib_ref, fb_ref, n, both):
    lane = lax.broadcasted_iota(jnp.int32, (1, LANES), 1)
    valid = (lane >= ML_GATE) & (lane < ML_GATE + ML_ND)
    is_f = lane < ML_GATE + ML_HEADS
    sm = sm_ref[0]
    li = sm + ib_ref[...]
    lf = pltpu.roll(_log_sigmoid(sm + fb_ref[...]), LANES - ML_ND, axis=1)
    parts = _split(jnp.where(valid, lf, 0.0), 3)
    upp = jnp.where(_causal(n, True), 1.0, 0.0).astype(BF16)
    bcum = sum(_dot(upp, p) for p in parts)
    if both:
        low = jnp.where(_causal(n, False), 1.0, 0.0).astype(BF16)
        bcum = jnp.where(is_f, sum(_dot(low, p) for p in parts), bcum)
    return valid, is_f, bcum, jnp.where(valid, li - bcum, 0.0)


def _ml_state_step(cn_ref, h, cn, keep, k, ws_dense, v_aug):
    w3 = jnp.concatenate([ws_dense.astype(BF16)] * (ML_AUG // LANES), axis=1)
    cn_ref[h] = keep * cn + _dot_tn(k, w3 * v_aug)


def _ml_v_aug(v_ref, h, n):
    return jnp.concatenate([v_ref[0, :, ML_V_DIM * h:ML_V_DIM * (h + 1)], jnp.ones((n, LANES), BF16)], axis=1)


def _ml_states_kernel(qk_ref, v_ref, sm_ref, ib_ref, fb_ref, sel_ref, cn_out, m_out, cn_ref, m_ref):
    n = qk_ref.shape[1]

    @pl.when(pl.program_id(1) == 0)
    def _init():
        cn_ref[...] = jnp.zeros_like(cn_ref)
        m_ref[...] = jnp.zeros_like(m_ref)

    valid, is_f, bcum, a = _ml_gates(sm_ref, ib_ref, fb_ref, n, False)
    cn_out[0, 0] = cn_ref[...].astype(cn_out.dtype)
    m_out[0, 0] = m_ref[...]
    m_prev = m_ref[0:1, :]
    g_last = jnp.maximum(m_prev, jnp.max(a, axis=0, keepdims=True))
    ws = _expand(jnp.exp(a - g_last), sel_ref[:, ML_HEADS * LANES:])
    keep = jnp.exp(m_prev - g_last)
    for h in range(ML_HEADS):
        lane_b = ML_GATE + ML_HEADS + h
        k = qk_ref[0, :, ML_QK_WIDTH + ML_QK_DIM * h:ML_QK_WIDTH + ML_QK_DIM * (h + 1)]
        _ml_state_step(cn_ref, h, cn_ref[h], keep[:, lane_b:lane_b + 1], k,
                       ws[:, LANES * h:LANES * (h + 1)], _ml_v_aug(v_ref, h, n))
    m_ref[...] = jnp.broadcast_to(bcum[0:1, :] + g_last, m_ref.shape)


def _ml_out_kernel(qk_ref, v_ref, sm_ref, ib_ref, fb_ref, sel_ref, cnb_ref, mb_ref, o_ref, cn_ref, m_ref):
    n = qk_ref.shape[1]

    @pl.when(pl.program_id(1) == 0)
    def _init():
        cn_ref[...] = jnp.zeros_like(cn_ref)
        m_ref[...] = jnp.zeros_like(m_ref)

    valid, is_f, bcum, a = _ml_gates(sm_ref, ib_ref, fb_ref, n, True)
    m_prev = jnp.where(is_f, m_ref[0:1, :], mb_ref[0, 0, 0:1, :])
    a_t = a.T
    pre = suf = a_t[ML_GATE:ML_GATE + ML_ND, :]
    pos = lax.broadcasted_iota(jnp.int32, (ML_ND, n), 1)
    k = 1
    while k < n:
        pre = jnp.maximum(pre, jnp.where(pos >= k, pltpu.roll(pre, k, axis=1), -jnp.inf))
        suf = jnp.maximum(suf, jnp.where(pos < n - k, pltpu.roll(suf, n - k, axis=1), -jnp.inf))
        k *= 2
    run = jnp.where(lax.broadcasted_iota(jnp.int32, (ML_ND, n), 0) < ML_HEADS, pre, suf)
    run = jnp.concatenate([jnp.zeros((ML_GATE, n), F32), run,
                           jnp.zeros((LANES - ML_GATE - ML_ND, n), F32)], axis=0).T
    g = jnp.maximum(m_prev, run)
    m_t = bcum + g
    floor = jnp.exp(-m_t)
    dense = lambda x, lane_: jnp.broadcast_to(x[:, lane_:lane_ + 1], (n, LANES))
    diag_t = jnp.exp(a - g).T
    g_last = g[n - 1:n, :]
    ws = _expand(jnp.exp(a - g_last), sel_ref[:, :ML_HEADS * LANES])
    keep = jnp.exp(m_prev - g_last)
    ti = lax.broadcasted_iota(jnp.int32, (n, n), 0)
    si = lax.broadcasted_iota(jnp.int32, (n, n), 1)
    not_above, above, on_diag = si <= ti, si > ti, si == ti
    wide = lambda x: jnp.concatenate([x] * (n // LANES), axis=1)
    for h in range(ML_HEADS):
        q = qk_ref[0, :, ML_QK_DIM * h:ML_QK_DIM * (h + 1)]
        k_h = qk_ref[0, :, ML_QK_WIDTH + ML_QK_DIM * h:ML_QK_WIDTH + ML_QK_DIM * (h + 1)]
        v_aug = _ml_v_aug(v_ref, h, n)
        lanes = (ML_GATE + h, ML_GATE + ML_HEADS + h)
        g_d = [dense(g, ln) for ln in lanes]
        qk = _dot_nt(q, k_h)
        e = jnp.where(not_above, a_t[lanes[0]:lanes[0] + 1, :] - wide(g_d[0]),
                      a_t[lanes[1]:lanes[1] + 1, :] - wide(g_d[1]))
        p = qk * jnp.exp(e)
        p_f = jnp.where(not_above, p, 0.0).astype(BF16)
        p_b = jnp.where(above, p, jnp.where(on_diag, qk * diag_t[lanes[1]:lanes[1] + 1, :], 0.0)).astype(BF16)
        intra = _dot(jnp.concatenate([p_f, p_b], axis=0), v_aug)
        cn_f = cn_ref[h]
        inter = _dot(q, jnp.concatenate([cn_f.astype(BF16), cnb_ref[0, 0, h]], axis=1))
        out = None
        for d in range(2):
            w_inter = jnp.exp(m_prev[:, lanes[d]:lanes[d] + 1] - g_d[d])
            s = (intra[n * d:n * (d + 1)]
                 + jnp.concatenate([w_inter] * (ML_AUG // LANES), axis=1) * inter[:, ML_AUG * d:ML_AUG * (d + 1)])
            rn = 1.0 / jnp.maximum(jnp.abs(s[:, ML_V_DIM:]), dense(floor, lanes[d]))
            hid = s[:, :ML_V_DIM] * jnp.concatenate([rn] * (ML_V_DIM // LANES), axis=1)
            out = hid if out is None else out + hid
        o_ref[0, :, ML_V_DIM * h:ML_V_DIM * (h + 1)] = out
        _ml_state_step(cn_ref, h, cn_f, keep[:, lanes[0]:lanes[0] + 1], k_h,
                       ws[:, LANES * h:LANES * (h + 1)], v_aug)
    m_ref[...] = jnp.broadcast_to(m_t[n - 1:n, :], m_ref.shape)


def _mlstm(cvm, p3, sm3, i_bias, f_bias, n_ctx):
    nb, t, _ = cvm.shape
    nt = t // TILE
    assert _SM_OFF["i_f"] == ML_GATE and _SM_OFF["f_f"] == ML_GATE + ML_ND
    row = lambda v, off: jnp.zeros((1, LANES), F32).at[0, off:off + ML_ND].set(v.reshape(-1))
    sel = np.zeros((LANES, ML_ND * LANES), np.float32)
    for r in range(ML_ND):
        sel[ML_GATE + r, r * LANES:(r + 1) * LANES] = 1.0
    const = lambda b, i: (0, 0)
    specs = lambda order: [
        pl.BlockSpec((1, TILE, 2 * ML_QK_WIDTH), lambda b, i: (b, order(i), 0)),
        pl.BlockSpec((1, TILE, ML_V_WIDTH), lambda b, i: (b, order(i), _P_OFF["m_v"] // ML_V_WIDTH)),
        pl.BlockSpec((1, TILE, LANES), lambda b, i: (b, order(i), 0)),
        pl.BlockSpec((1, LANES), const),
        pl.BlockSpec((1, LANES), const),
        pl.BlockSpec((LANES, ML_ND * LANES), const)]
    args = (cvm, p3, sm3, row(i_bias, ML_GATE), row(f_bias, ML_GATE + ML_ND), jnp.asarray(sel, BF16))
    cn_block = (1, 1, ML_HEADS, ML_QK_DIM, ML_AUG)
    m_block = (1, 1, SUBLANES, LANES)
    scratch = [pltpu.VMEM(cn_block[2:], F32), pltpu.VMEM(m_block[2:], F32)]
    bwd = _tile_order(nt, True, n_ctx // TILE)
    st_idx = lambda b, i: (b, bwd(i)) + (0,) * 3
    cn_b, m_b = pl.pallas_call(
        _ml_states_kernel,
        out_shape=(jax.ShapeDtypeStruct((nb, nt) + cn_block[2:], BF16),
                   jax.ShapeDtypeStruct((nb, nt) + m_block[2:], F32)),
        grid=(nb, nt),
        in_specs=specs(bwd),
        out_specs=(pl.BlockSpec(cn_block, st_idx), pl.BlockSpec(m_block, lambda b, i: (b, bwd(i), 0, 0))),
        scratch_shapes=scratch,
        compiler_params=_cparams(("parallel", "arbitrary")),
        name="mlstm_states",
    )(*args)
    fwd = _tile_order(nt, False)
    return pl.pallas_call(
        _ml_out_kernel,
        out_shape=jax.ShapeDtypeStruct((nb, t, ML_V_WIDTH), F32),
        grid=(nb, nt),
        in_specs=specs(fwd) + [pl.BlockSpec(cn_block, lambda b, i: (b, i, 0, 0, 0)),
                               pl.BlockSpec(m_block, lambda b, i: (b, i, 0, 0))],
        out_specs=pl.BlockSpec((1, TILE, ML_V_WIDTH), lambda b, i: (b, i, 0)),
        scratch_shapes=scratch,
        compiler_params=_cparams(("parallel", "arbitrary")),
        name="mlstm_out",
    )(*args, cn_b, m_b)


GLA_SUB = 256
GLA_NCH = GLA_SUB // GLA_CHUNK
GLA_COLS = BF16_SUBLANES


def _gla_layout(is_ctx, lat_rows):
    r = np.arange(GLA_SUB)
    if is_ctx:
        return r // GLA_CHUNK, r % GLA_CHUNK
    col = r % SUBLANES
    cpc = GLA_CHUNK // lat_rows
    return col // cpc, (col % cpc) * lat_rows + r // SUBLANES


def _gla_consts(is_ctx, lat_rows, rev):
    ch, pos = _gla_layout(is_ctx, lat_rows)
    same = ch[:, None] == ch[None, :]
    before = (pos[None, :] >= pos[:, None]) if rev else (pos[None, :] <= pos[:, None])
    tri = (same & before).astype(np.float32)
    cmask = np.stack([np.repeat((ch == j)[:, None], LANES, axis=1) for j in range(GLA_NCH)])
    return tri, cmask.astype(np.float32)


def _gla_row(is_ctx, lat_rows, j, p):
    ch, pos = _gla_layout(is_ctx, lat_rows)
    return int(np.nonzero((ch == j) & (pos == p))[0][0])


def _per_chunk_rows(b, is_ctx, lat_rows, p):
    rows = [b[_gla_row(is_ctx, lat_rows, j, p):_gla_row(is_ctx, lat_rows, j, p) + 1, :]
            for j in range(GLA_NCH)]
    w = b.shape[1]
    if is_ctx:
        full = jnp.concatenate([jnp.broadcast_to(r, (GLA_CHUNK, w)) for r in rows], axis=0)
    else:
        rep = SUBLANES // GLA_NCH
        pat = jnp.concatenate([jnp.broadcast_to(r, (rep, w)) for r in rows], axis=0)
        full = jnp.broadcast_to(pat[None], (GLA_SUB // SUBLANES, SUBLANES, w)).reshape(GLA_SUB, w)
    return rows, full


def _gla_sub(q, k, v, araw, aup, abias, tri_b, tri_f, cmask_ref, st_ref, *, rev, is_ctx, lat_rows):
    g = _log_sigmoid(_dot_hp(araw, aup) + abias) * (1.0 / GLA_TAU)
    b = _dot_3x(tri_b, g)
    _, ref = _per_chunk_rows(b, is_ctx, lat_rows, GLA_CHUNK // 2)
    lasts, last = _per_chunk_rows(b, is_ctx, lat_rows, 0 if rev else GLA_CHUNK - 1)
    qs = q * (GLA_K_DIM ** -0.5)
    qe = (qs * jnp.exp(b - ref)).astype(BF16)
    ke = (k * jnp.exp(ref - b)).astype(BF16)
    qb = (qs * jnp.exp(b)).astype(BF16)
    kl = (k * jnp.exp(last - b)).astype(BF16)
    visible = tri_f > 0.0
    order = range(GLA_NCH - 1, -1, -1) if rev else range(GLA_NCH)
    outs = []
    for h in range(GLA_HEADS):
        ks = slice(GLA_K_DIM * h, GLA_K_DIM * (h + 1))
        vh = v[:, GLA_V_DIM * h:GLA_V_DIM * (h + 1)]
        att = jnp.where(visible, _dot_nt(qe[:, ks], ke[:, ks]), 0.0).astype(BF16)
        o = _dot(att, vh)
        klm = jnp.concatenate([kl[:, ks] * cmask_ref[j].astype(BF16) for j in range(GLA_NCH)], axis=1)
        upd = _dot_tn(vh, klm)
        s = st_ref[h]
        s_in = [None] * GLA_NCH
        for j in order:
            s_in[j] = s.astype(BF16)
            s = s * jnp.exp(lasts[j][:, ks]) + upd[:, GLA_K_DIM * j:GLA_K_DIM * (j + 1)]
        st_ref[h] = s
        qbm = jnp.concatenate([qb[:, ks] * cmask_ref[j].astype(BF16) for j in range(GLA_NCH)], axis=1)
        outs.append(o + _dot_nt(qbm, jnp.concatenate(s_in, axis=1)))
    return jnp.concatenate(outs, axis=1)


def _gla_kernel(qc_ref, kc_ref, vc_ref, sc_ref, ql_ref, kl_ref, vl_ref, sl_ref,
                aup_ref, ab_ref, tcb_ref, tcf_ref, cmc_ref, tlb_ref, tlf_ref, cml_ref,
                *rest, rev, ctx_rows, lat_rows, n_cblk):
    acc_ref, o_ref, st_ref, ctxo_ref = rest if len(rest) == 4 else (None,) + rest
    i = pl.program_id(1)
    aup = aup_ref[...]
    abias = ab_ref[...]
    r0, r1 = ctx_rows, ctx_rows + lat_rows
    sub = functools.partial(_gla_sub, aup=aup, abias=abias, st_ref=st_ref, rev=rev, lat_rows=lat_rows)

    @pl.when(i == 0)
    def _ctx():
        st_ref[...] = jnp.zeros_like(st_ref)
        ctxo_ref[...] = sub(qc_ref[0].astype(F32), kc_ref[0].astype(F32), vc_ref[0], sc_ref[0],
                            tri_b=tcb_ref[...], tri_f=tcf_ref[...], cmask_ref=cmc_ref, is_ctx=True)

    @pl.when(i > 0)
    def _lat():
        cblk = (n_cblk - i) if rev else (i - 1)
        halves = range(GLA_COLS // SUBLANES)
        for half in (reversed(halves) if rev else halves):
            cs = slice(SUBLANES * half, SUBLANES * (half + 1))
            take = lambda r: r[0, r0:r1].astype(F32)[:, cs, :].reshape(GLA_SUB, r.shape[-1])
            o = sub(take(ql_ref), take(kl_ref), take(vl_ref).astype(BF16), take(sl_ref),
                    tri_b=tlb_ref[...], tri_f=tlf_ref[...], cmask_ref=cml_ref, is_ctx=False)
            _write(o_ref, (0, slice(r0, r1), cs, slice(None)),
                   o.reshape(lat_rows, SUBLANES, GLA_V_WIDTH), acc_ref)
        for r in range(ctx_rows):
            start = pl.multiple_of(r * GRID_W + cblk * GLA_COLS, GLA_COLS)
            _write(o_ref, (0, r, slice(None), slice(None)), ctxo_ref[pl.ds(start, GLA_COLS), :], acc_ref)


def _gla_scan(p3, sm3, a_up, a_bias, n_ctx, rev, acc):
    nb, t, ncol = p3.shape
    rows = t // GRID_W
    ctx_rows = n_ctx // GRID_W
    lat_rows = rows - ctx_rows
    n_cblk = GRID_W // GLA_COLS
    p4 = p3.reshape(nb, rows, GRID_W, ncol)
    sm4 = sm3.reshape(nb, rows, GRID_W, LANES)
    a_off = _SM_OFF["a_b"] if rev else _SM_OFF["a_f"]
    aup = jnp.zeros((LANES, GLA_K_WIDTH), F32).at[a_off:a_off + GLA_RANK, :].set(a_up)
    cblk = lambda i: jnp.where(i == 0, n_cblk - 1 if rev else 0, (n_cblk - i) if rev else (i - 1))
    ctx = lambda blk: (lambda b, i: (b, 0, blk))
    lat = lambda blk: (lambda b, i: (b, 0, cblk(i), blk))
    const2 = lambda b, i: (0, 0)
    const3 = lambda b, i: (0, 0, 0)
    widths = (GLA_K_WIDTH, GLA_K_WIDTH, GLA_V_WIDTH)
    offs = (_P_OFF["g_q"], _P_OFF["g_k"], _P_OFF["g_v"])
    consts = []
    const_specs = []
    for is_ctx in (True, False):
        tri, cmask = _gla_consts(is_ctx, lat_rows, rev)
        consts += [jnp.asarray(tri, BF16), jnp.asarray(tri, F32), jnp.asarray(cmask, F32)]
        const_specs += [pl.BlockSpec((GLA_SUB, GLA_SUB), const2), pl.BlockSpec((GLA_SUB, GLA_SUB), const2),
                        pl.BlockSpec((GLA_NCH, GLA_SUB, LANES), const3)]
    out_spec = pl.BlockSpec((1, rows, GLA_COLS, GLA_V_WIDTH), lat(0))
    in_specs = ([pl.BlockSpec((1, n_ctx, w), ctx(o // w)) for w, o in zip(widths, offs)]
                + [pl.BlockSpec((1, n_ctx, LANES), ctx(0))]
                + [pl.BlockSpec((1, rows, GLA_COLS, w), lat(o // w)) for w, o in zip(widths, offs)]
                + [pl.BlockSpec((1, rows, GLA_COLS, LANES), lat(0))]
                + [pl.BlockSpec((LANES, GLA_K_WIDTH), const2), pl.BlockSpec((1, GLA_K_WIDTH), const2)]
                + const_specs)
    args = [p3, p3, p3, sm3, p4, p4, p4, sm4, aup, a_bias.reshape(1, GLA_K_WIDTH)] + consts
    if acc is not None:
        in_specs.append(out_spec)
        args.append(acc.reshape(nb, rows, GRID_W, GLA_V_WIDTH))
    out = pl.pallas_call(
        functools.partial(_gla_kernel, rev=rev, ctx_rows=ctx_rows, lat_rows=lat_rows, n_cblk=n_cblk),
        out_shape=jax.ShapeDtypeStruct((nb, rows, GRID_W, GLA_V_WIDTH), F32),
        grid=(nb, n_cblk + 1),
        in_specs=in_specs,
        out_specs=out_spec,
        scratch_shapes=[pltpu.VMEM((GLA_HEADS, GLA_V_DIM, GLA_K_DIM), F32),
                        pltpu.VMEM((n_ctx, GLA_V_WIDTH), F32)],
        input_output_aliases={} if acc is None else {len(args) - 1: 0},
        compiler_params=_cparams(("parallel", "arbitrary")),
        name="gla_scan",
    )(*args)
    return out.reshape(nb, t, GLA_V_WIDTH)


def _group_rmsnorm(y, groups):
    width = y.shape[-1] // groups
    return jnp.concatenate([_rms(y[:, width * g:width * (g + 1)]) for g in range(groups)], axis=1)


def _post_kernel(x_ref, y_ref, h_ref, o_ref_in, z_ref, mo_ref, gg_ref, gs_ref, gm_ref, gl_ref,
                 nws_ref, nwm_ref, nwg_ref, wbs_ref, wbm_ref, wbg_ref, wout_ref, g1_ref,
                 nwf_ref, sc2_ref, sh2_ref, xo_ref, ho_ref):
    f = lambda r: r[0].astype(F32)
    y_ssd = (_group_rmsnorm(y_ref[0] * _silu(f(z_ref)), 2) * nws_ref[...]).astype(BF16)
    y_ml = (_group_rmsnorm(h_ref[0], ML_HEADS) * nwm_ref[...] * _sigmoid(f(mo_ref))).astype(BF16)
    y_gla = (_group_rmsnorm(o_ref_in[0], GLA_HEADS) * nwg_ref[...] * _silu(f(gg_ref))).astype(BF16)
    merged = (_sigmoid(f(gs_ref)) * _dot(y_ssd, wbs_ref[...])
              + _sigmoid(f(gm_ref)) * _dot(y_ml, wbm_ref[...])
              + _sigmoid(f(gl_ref)) * _dot(y_gla, wbg_ref[...]))
    x_new = x_ref[0] + g1_ref[0] * _dot(merged.astype(BF16), wout_ref[...])
    xo_ref[0] = x_new
    ho_ref[0] = (_rms(x_new) * nwf_ref[...] * (1.0 + sc2_ref[0]) + sh2_ref[0]).astype(ho_ref.dtype)


def _post(x, scans, p3, norm_ws, w_bs, w_out, norm_ffn_w, mods, n_ctx, tile0):
    nb, t, d = x.shape
    nt = t // TILE - tile0
    ctx_tiles = n_ctx // TILE
    tok = lambda blk: (lambda b, i: (b, i + tile0, blk))
    out = lambda b, i: (b, i, 0)
    const = lambda b, i: (0, 0)
    tok_spec = lambda blk: pl.BlockSpec((1, TILE, d), tok(blk))
    w_spec = pl.BlockSpec((d, d), const, pipeline_mode=pl.Buffered(1))
    vec = pl.BlockSpec((1, d), const)
    mod = lambda k: pl.BlockSpec((1, 1, d), _mod_row(nb, k, ctx_tiles, tile0))
    names = ("s_z", "m_o", "g_g", "gate_ssd", "gate_ml", "gate_gla")
    in_specs = ([tok_spec(0)] * 4 + [tok_spec(_P_OFF[nm] // d) for nm in names]
                + [vec] * 3 + [w_spec] * 4 + [mod(2), vec, mod(4), mod(3)])
    return pl.pallas_call(
        _post_kernel,
        out_shape=(jax.ShapeDtypeStruct((nb, nt * TILE, d), F32),
                   jax.ShapeDtypeStruct((nb, nt * TILE, d), BF16)),
        grid=(nb, nt),
        in_specs=in_specs,
        out_specs=(pl.BlockSpec((1, TILE, d), out), pl.BlockSpec((1, TILE, d), out)),
        compiler_params=_cparams(("parallel", "parallel")),
        name="post",
    )(x, *scans, *([p3] * 6), *[w.reshape(1, d) for w in norm_ws], *w_bs, w_out, mods,
      norm_ffn_w.reshape(1, d), mods, mods)


def _ffn_in_kernel(a_ref, w_ref, o_ref):
    acc = _dot(a_ref[...], w_ref[...])
    half = acc.shape[1] // 2
    o_ref[...] = (_silu(acc[:, :half]) * acc[:, half:]).astype(o_ref.dtype)


def _ffn_in(h, w_gu, half):
    m, k = h.shape
    n_half = w_gu.shape[1] // 2
    tm = _row_tile(m)
    return pl.pallas_call(
        _ffn_in_kernel,
        out_shape=jax.ShapeDtypeStruct((m, n_half), BF16),
        grid=(n_half // half, m // tm),
        in_specs=[pl.BlockSpec((tm, k), lambda j, i: (i, 0)),
                  pl.BlockSpec((k, 2 * half), lambda j, i: (0, j))],
        out_specs=pl.BlockSpec((tm, half), lambda j, i: (i, j)),
        compiler_params=_cparams(("parallel", "parallel")),
        name="ffn_in",
    )(h, w_gu)


def _ffn_out_kernel(a_ref, w_ref, x_ref, g_ref, nw_ref, sc_ref, sh_ref, xo_ref, ho_ref):
    x_new = x_ref[0] + g_ref[0] * _dot(a_ref[0], w_ref[...])
    xo_ref[0] = x_new
    ho_ref[0] = (_rms(x_new) * nw_ref[...] * (1.0 + sc_ref[0]) + sh_ref[0]).astype(ho_ref.dtype)


def _ffn_out_last_kernel(a_ref, w_ref, x_ref, g_ref, nw_ref, o_ref):
    x_new = x_ref[0] + g_ref[0] * _dot(a_ref[0], w_ref[...])
    o_ref[0] = _rms(x_new) * nw_ref[...]


def _ffn_out(a, w, x, mods, n_ctx, tile0, next_norm_w, next_mods):
    nb, t, d = x.shape
    k = a.shape[-1]
    ctx_tiles = n_ctx // TILE
    tok = lambda b, i: (b, i, 0)
    mod = lambda k_, m=None: pl.BlockSpec((1, 1, d), _mod_row(nb, k_, ctx_tiles, tile0))
    in_specs = [pl.BlockSpec((1, TILE, k), tok),
                pl.BlockSpec((k, d), lambda b, i: (0, 0), pipeline_mode=pl.Buffered(1)),
                pl.BlockSpec((1, TILE, d), tok),
                mod(5),
                pl.BlockSpec((1, d), lambda b, i: (0, 0))]
    args = [a, w, x, mods, next_norm_w.reshape(1, d)]
    if next_mods is None:
        body = _ffn_out_last_kernel
        out_shape = jax.ShapeDtypeStruct((nb, t, d), F32)
        out_specs = pl.BlockSpec((1, TILE, d), tok)
    else:
        body = _ffn_out_kernel
        in_specs += [mod(1), mod(0)]
        args += [next_mods, next_mods]
        out_shape = (jax.ShapeDtypeStruct((nb, t, d), F32), jax.ShapeDtypeStruct((nb, t, d), BF16))
        out_specs = (pl.BlockSpec((1, TILE, d), tok), pl.BlockSpec((1, TILE, d), tok))
    return pl.pallas_call(
        body,
        out_shape=out_shape,
        grid=(nb, t // TILE),
        in_specs=in_specs,
        out_specs=out_specs,
        compiler_params=_cparams(("parallel", "parallel")),
        name="ffn_out",
    )(*args)


def _proj_weights(w_in):
    main = jnp.concatenate([w_in[:, _IN_OFF[nm]:_IN_OFF[nm] + _IN_W[nm]] for nm in _P_ORDER], axis=1)
    small = [w_in[:, _IN_OFF[nm]:_IN_OFF[nm] + _IN_W[nm]] for nm in _SMALL]
    small.append(jnp.zeros((w_in.shape[0], LANES - N_SMALL_USED), w_in.dtype))
    return main.astype(BF16), jnp.concatenate(small, axis=1).astype(BF16)


def _ffn_weight(w_ffn_in, half):
    d_ff = w_ffn_in.shape[1] // 2
    cols = []
    for j in range(d_ff // half):
        cols.append(w_ffn_in[:, j * half:(j + 1) * half])
        cols.append(w_ffn_in[:, d_ff + j * half:d_ff + (j + 1) * half])
    return jnp.concatenate(cols, axis=1).astype(BF16)


def kernel(x, c, ctx, c_ctx, w_mod, b_mod, norm_mix_w, norm_ffn_w, w_in, ssd_conv_w, ssd_conv_b, ssd_dt_bias, ssd_a_log, ssd_d, ssd_norm_w, ml_conv_w, ml_conv_b, ml_i_bias, ml_f_bias, ml_norm_w, gla_a_up, gla_a_bias, gla_norm_w, w_b_ssd, w_b_ml, w_b_gla, w_out, w_ffn_in, w_ffn_out, final_norm_w):
    nb, n_lat, d = x.shape
    n_ctx = ctx.shape[1]
    t = n_ctx + n_lat
    depth = w_in.shape[0]
    d_ff = w_ffn_out.shape[1]
    assert n_ctx == TILE == GLA_SUB and n_lat % TILE == 0 and n_lat // GRID_W == 32
    ffn_half = d_ff // 2

    xs = jnp.concatenate([ctx, x], axis=1)
    c16 = jnp.zeros((2 * SUBLANES, d), F32).at[:nb].set(c).at[nb].set(c_ctx)
    mods = [_modulation(c16, w_mod[l], b_mod[l]).reshape(2 * SUBLANES * 6, 1, d) for l in range(depth)]
    h = _norm_mod(xs, norm_mix_w[0], mods[0], 0, 1, n_ctx)
    ml_post = jnp.concatenate([jnp.ones((ML_QK_WIDTH,), F32),
                               jnp.full((ML_QK_WIDTH,), ML_QK_DIM ** -0.5, F32)])
    for l in range(depth):
        last = l == depth - 1
        w_main, w_small = _proj_weights(w_in[l])
        h2d = h.reshape(nb * t, d)
        p3 = _matmul(h2d, w_main, N_PROJ // 5, BF16).reshape(nb, t, N_PROJ)
        sm3 = _matmul(h2d, w_small, LANES, F32).reshape(nb, t, LANES)
        cvs = _conv(p3, _P_OFF["s_x"], ssd_conv_w[l], ssd_conv_b[l],
                    jnp.ones((ssd_conv_w.shape[-1],), F32), n_ctx)
        cvm = _conv(p3, _P_OFF["m_q"], ml_conv_w[l], ml_conv_b[l], ml_post, n_ctx)
        d_e = jnp.repeat(ssd_d[l], SSD_HEAD_DIM).reshape(1, SSD_WIDTH)
        y = _ssd(cvs, sm3, ssd_dt_bias[l], ssd_a_log[l], d_e, n_ctx)
        hm = _mlstm(cvm, p3, sm3, ml_i_bias[l], ml_f_bias[l], n_ctx)
        og = None
        for rev in (False, True):
            k = int(rev)
            og = _gla_scan(p3, sm3, gla_a_up[l, k], gla_a_bias[l, k], n_ctx, rev, og)
        tile0 = n_ctx // TILE if last else 0
        xs, h2 = _post(xs, (y, hm, og), p3, (ssd_norm_w[l], ml_norm_w[l], gla_norm_w[l]),
                       (w_b_ssd[l].astype(BF16), w_b_ml[l].astype(BF16), w_b_gla[l].astype(BF16)),
                       w_out[l].astype(BF16), norm_ffn_w[l], mods[l], n_ctx, tile0)
        nt = xs.shape[1]
        a = _ffn_in(h2.reshape(nb * nt, d), _ffn_weight(w_ffn_in[l], ffn_half), ffn_half)
        a = a.reshape(nb, nt, d_ff)
        if last:
            return _ffn_out(a, w_ffn_out[l].astype(BF16), xs, mods[l], n_ctx, tile0, final_norm_w, None)
        xs, h = _ffn_out(a, w_ffn_out[l].astype(BF16), xs, mods[l], n_ctx, tile0,
                         norm_mix_w[l + 1], mods[l + 1])
```

```python
import functools

import numpy as np
import jax
import jax.numpy as jnp
from jax import lax
from jax.experimental import pallas as pl
from jax.experimental.pallas import tpu as pltpu

F32 = jnp.float32
BF16 = jnp.bfloat16

EPS = 1e-6
LOG2E = 1.4426950408889634
GRID_W = 64
SSD_HEADS = 16
SSD_HEAD_DIM = 64
SSD_WIDTH = 1024
SSD_STATE = 64
SSD_BC = 128
ML_HEADS = 4
ML_QK_DIM = 128
ML_V_DIM = 256
ML_QK_WIDTH = 512
ML_V_WIDTH = 1024
GLA_HEADS = 4
GLA_K_DIM = 128
GLA_V_DIM = 256
GLA_K_WIDTH = 512
GLA_V_WIDTH = 1024
GLA_RANK = 16
GLA_TAU = 16.0
GLA_CHUNK = 64

LANES = 128
SUBLANES = 8
BF16_SUBLANES = 16
VMEM_LIMIT = 56 * 1024 * 1024

TILE = 256

_IN_NAMES = ("s_x", "s_z", "s_b", "s_c", "dt_f", "dt_b",
             "m_q", "m_k", "m_v", "m_o", "i_f", "i_b", "f_f", "f_b",
             "g_q", "g_k", "g_v", "g_g", "a_f", "a_b",
             "gate_ssd", "gate_ml", "gate_gla")
_IN_WIDTHS = (1024, 1024, 128, 128, 16, 16,
              512, 512, 1024, 1024, 4, 4, 4, 4,
              512, 512, 1024, 1024, 16, 16,
              1024, 1024, 1024)
_IN_OFF = dict(zip(_IN_NAMES, np.concatenate([[0], np.cumsum(_IN_WIDTHS)[:-1]]).tolist()))
_IN_W = dict(zip(_IN_NAMES, _IN_WIDTHS))

_P_ORDER = ("s_z", "m_o", "g_g", "gate_ssd", "gate_ml", "gate_gla", "m_v", "g_v",
            "g_q", "g_k", "s_x", "s_b", "s_c", "m_q", "m_k")
_P_OFF = {}
_o = 0
for _n in _P_ORDER:
    _P_OFF[_n] = _o
    _o += _IN_W[_n]
N_PROJ = _o
_SMALL = ("dt_f", "dt_b", "i_f", "i_b", "f_f", "f_b", "a_f", "a_b")
_SM_OFF = {}
_s = 0
for _n in _SMALL:
    _SM_OFF[_n] = _s
    _s += _IN_W[_n]
N_SMALL_USED = _s


def _cparams(sem):
    return pltpu.CompilerParams(dimension_semantics=sem, vmem_limit_bytes=VMEM_LIMIT)


def _sigmoid(x):
    return 0.5 * jnp.tanh(0.5 * x) + 0.5


def _silu(x):
    return x * _sigmoid(x)


def _softplus(x):
    return jnp.maximum(x, 0.0) + jnp.log1p(jnp.exp(-jnp.abs(x)))


def _log_sigmoid(x):
    return jnp.minimum(x, 0.0) - jnp.log(1.0 + jnp.exp(-jnp.abs(x)))


def _split(x, n):
    out = []
    r = x
    for _ in range(n):
        p = r.astype(BF16)
        out.append(p)
        r = r - p.astype(F32)
    return out


def _dot(a, b):
    return jnp.dot(a, b, preferred_element_type=F32)


def _dot_nt(a, b):
    return lax.dot_general(a, b, (((1,), (1,)), ((), ())), preferred_element_type=F32)


def _dot_tn(a, b):
    return lax.dot_general(a, b, (((0,), (0,)), ((), ())), preferred_element_type=F32)


def _dot_x3(x, e):
    return sum(_dot(p, e) for p in _split(x, 3))


def _dot_3x(t, x):
    return sum(_dot(t, p) for p in _split(x, 3))


def _dot_hp(a, b):
    ah, am = _split(a, 2)
    bh, bm = _split(b, 2)
    return _dot(ah, bh) + _dot(ah, bm) + _dot(am, bh)


def _causal(n, rev):
    t = lax.broadcasted_iota(jnp.int32, (n, n), 0)
    s = lax.broadcasted_iota(jnp.int32, (n, n), 1)
    return (s >= t) if rev else (s <= t)


def _rms(x):
    return x * lax.rsqrt(jnp.mean(x * x, axis=-1, keepdims=True) + EPS)


def _mod_kernel(c_ref, w_ref, b_ref, o_ref):
    o_ref[...] = _dot_hp(_silu(c_ref[...]), w_ref[...]) + b_ref[...]


def _modulation(c16, w_mod, b_mod):
    rows, d = c16.shape
    n = w_mod.shape[1]
    tn = 1536
    return pl.pallas_call(
        _mod_kernel,
        out_shape=jax.ShapeDtypeStruct((rows, n), F32),
        grid=(n // tn,),
        in_specs=[pl.BlockSpec((rows, d), lambda j: (0, 0)),
                  pl.BlockSpec((d, tn), lambda j: (0, j)),
                  pl.BlockSpec((1, tn), lambda j: (0, j))],
        out_specs=pl.BlockSpec((rows, tn), lambda j: (0, j)),
        compiler_params=_cparams(("arbitrary",)),
        name="modulation",
    )(c16, w_mod, b_mod.reshape(1, n))


def _mod_row(nb, k, ctx_tiles, tile0=0):
    return lambda b, i: (jnp.where(i + tile0 < ctx_tiles, nb, b) * 6 + k, 0, 0)


def _norm_mod_kernel(x_ref, w_ref, sc_ref, sh_ref, o_ref):
    o_ref[0] = (_rms(x_ref[0]) * w_ref[...] * (1.0 + sc_ref[0]) + sh_ref[0]).astype(o_ref.dtype)


def _norm_mod(x, w, mods, k_shift, k_scale, n_ctx):
    nb, t, d = x.shape
    return pl.pallas_call(
        _norm_mod_kernel,
        out_shape=jax.ShapeDtypeStruct((nb, t, d), BF16),
        grid=(nb, t // TILE),
        in_specs=[pl.BlockSpec((1, TILE, d), lambda b, i: (b, i, 0)),
                  pl.BlockSpec((1, d), lambda b, i: (0, 0)),
                  pl.BlockSpec((1, 1, d), _mod_row(nb, k_scale, n_ctx // TILE)),
                  pl.BlockSpec((1, 1, d), _mod_row(nb, k_shift, n_ctx // TILE))],
        out_specs=pl.BlockSpec((1, TILE, d), lambda b, i: (b, i, 0)),
        compiler_params=_cparams(("parallel", "parallel")),
        name="norm_mod",
    )(x, w.reshape(1, d), mods, mods)


def _mm_kernel(a_ref, w_ref, o_ref):
    o_ref[...] = _dot(a_ref[...], w_ref[...]).astype(o_ref.dtype)


def _row_tile(m):
    return 512 if m % 512 == 0 else TILE


def _matmul(a, w, tn, out_dtype):
    m, k = a.shape
    n = w.shape[1]
    tm = _row_tile(m)
    return pl.pallas_call(
        _mm_kernel,
        out_shape=jax.ShapeDtypeStruct((m, n), out_dtype),
        grid=(n // tn, m // tm),
        in_specs=[pl.BlockSpec((tm, k), lambda j, i: (i, 0)),
                  pl.BlockSpec((k, tn), lambda j, i: (0, j))],
        out_specs=pl.BlockSpec((tm, tn), lambda j, i: (i, j)),
        compiler_params=_cparams(("parallel", "parallel")),
        name="matmul",
    )(a, w)


CONV_K = 5
CONV_ROWS = 256


def _conv_kernel(u_ref, w_ref, b_ref, s_ref, o_ref, pad_ref, *, n_ctx):
    t, c = u_ref.shape[1], u_ref.shape[2]
    half = CONV_K // 2
    zeros = jnp.zeros((SUBLANES, c), F32)
    w = w_ref[...]
    bias = b_ref[...]
    post = s_ref[...]
    for s0, n in ((0, n_ctx), (n_ctx, t - n_ctx)):
        pad_ref[0:SUBLANES, :] = zeros
        pad_ref[SUBLANES:SUBLANES + n, :] = u_ref[0, s0:s0 + n, :].astype(F32)
        pad_ref[SUBLANES + n:2 * SUBLANES + n, :] = zeros
        for r0 in range(0, n, CONV_ROWS):
            acc = bias
            for j in range(CONV_K):
                lo = SUBLANES - half + j + r0
                acc = acc + w[j:j + 1, :] * pad_ref[lo:lo + CONV_ROWS, :]
            o_ref[0, s0 + r0:s0 + r0 + CONV_ROWS, :] = (_silu(acc) * post).astype(o_ref.dtype)


def _conv(p3, col0, w, b, post_scale, n_ctx):
    nb, t, _ = p3.shape
    width = w.shape[1]
    cb = 256
    return pl.pallas_call(
        functools.partial(_conv_kernel, n_ctx=n_ctx),
        out_shape=jax.ShapeDtypeStruct((nb, t, width), BF16),
        grid=(nb, width // cb),
        in_specs=[pl.BlockSpec((1, t, cb), lambda b_, j: (b_, 0, col0 // cb + j)),
                  pl.BlockSpec((CONV_K, cb), lambda b_, j: (0, j)),
                  pl.BlockSpec((1, cb), lambda b_, j: (0, j)),
                  pl.BlockSpec((1, cb), lambda b_, j: (0, j))],
        out_specs=pl.BlockSpec((1, t, cb), lambda b_, j: (b_, 0, j)),
        scratch_shapes=[pltpu.VMEM((t + 2 * SUBLANES, cb), F32)],
        compiler_params=_cparams(("parallel", "parallel")),
        name="conv",
    )(p3, w, b.reshape(1, width), post_scale.reshape(1, width))


def _tile_order(n_tiles, rev, ctx_tiles=1):
    if rev:
        return lambda i: jnp.where(i < ctx_tiles, ctx_tiles - 1 - i, n_tiles - 1 + ctx_tiles - i)
    return lambda i: i


def _write(o_ref, idx, val, acc_ref):
    o_ref[idx] = val if acc_ref is None else acc_ref[idx] + val


SSD_PAIRS = SSD_HEADS // 2
SSD_B_OFF = SSD_HEADS
SSD_TILE = 256


def _ssd_gates(sm_ref, dtb_ref, alog_ref, n, both):
    lane = lax.broadcasted_iota(jnp.int32, (1, LANES), 1)
    dt = _softplus(sm_ref[0] + dtb_ref[...])
    la = dt * jnp.where(lane < 2 * SSD_HEADS, -jnp.exp(alog_ref[...]), 0.0)
    parts = _split(la, 3)
    upp = jnp.where(_causal(n, True), 1.0, 0.0).astype(BF16)
    cum = sum(_dot(upp, p) for p in parts)
    if both:
        low = jnp.where(_causal(n, False), 1.0, 0.0).astype(BF16)
        cum = jnp.where(lane < SSD_HEADS, sum(_dot(low, p) for p in parts), cum)
    return lane, dt, cum


def _expand(a, e):
    return sum(_dot(p, e) for p in _split(a, 2))


def _group_dup(v, g, lo):
    other = pltpu.roll(v, SSD_STATE, axis=1)
    return jnp.where(lo, v, other) if g == 0 else jnp.where(lo, other, v)


def _ssd_state_step(st_ref, j, bw, xp, elast, off):
    r = lax.broadcasted_iota(jnp.int32, (LANES, LANES), 0) < SSD_STATE
    c = lax.broadcasted_iota(jnp.int32, (LANES, LANES), 1) < SSD_HEAD_DIM
    dec = jnp.where(r, elast[:, off + 2 * j:off + 2 * j + 1], elast[:, off + 2 * j + 1:off + 2 * j + 2])
    st_ref[j] = jnp.where(r == c, dec * st_ref[j] + _dot_tn(bw.astype(BF16), xp), 0.0)


def _ssd_states_kernel(x_ref, bc_ref, sm_ref, dtb_ref, alog_ref, eb_ref, o_ref, st_ref):
    n = x_ref.shape[1]

    @pl.when(pl.program_id(1) == 0)
    def _init():
        st_ref[...] = jnp.zeros_like(st_ref)

    lane, dt, cum = _ssd_gates(sm_ref, dtb_ref, alog_ref, n, False)
    lo = lane < SSD_HEAD_DIM
    last = cum[0:1, :]
    wst = _expand(jnp.exp(last - cum) * dt, eb_ref[...])
    elast = jnp.exp(last)
    b128 = bc_ref[0, :, :SSD_BC].astype(F32)
    o_ref[0, 0] = st_ref[...].astype(o_ref.dtype)
    for j in range(SSD_PAIRS):
        sl = slice(LANES * j, LANES * (j + 1))
        bw = _group_dup(b128, j // (SSD_PAIRS // 2), lo) * wst[:, sl]
        _ssd_state_step(st_ref, j, bw, x_ref[0, :, sl], elast, SSD_B_OFF)


def _ssd_out_kernel(x_ref, bc_ref, sm_ref, dtb_ref, alog_ref, ef_ref, eb_ref, d_ref, sb_ref, o_ref, st_ref):
    n = x_ref.shape[1]

    @pl.when(pl.program_id(1) == 0)
    def _init():
        st_ref[...] = jnp.zeros_like(st_ref)

    lane, dt, cum = _ssd_gates(sm_ref, dtb_ref, alog_ref, n, True)
    lo = lane < SSD_HEAD_DIM
    is_f = lane < SSD_HEADS
    ldt = jnp.log(dt)
    dsum = jnp.log(dt + pltpu.roll(dt, LANES - SSD_B_OFF, axis=1))
    rt = (jnp.where(lane < 2 * SSD_HEADS, cum - ldt, pltpu.roll(dsum, 2 * SSD_HEADS, axis=1)) * LOG2E).T
    cum2 = cum * LOG2E
    last = jnp.where(is_f, cum[n - 1:n, :], cum[0:1, :])
    elast = jnp.exp(last)
    ecum = _split(jnp.exp(cum), 2)
    ecum_f = sum(_dot(p, ef_ref[...]) for p in ecum)
    ecum_b = sum(_dot(p, eb_ref[...]) for p in ecum)
    wst = _expand(jnp.exp(last - cum) * dt, ef_ref[...])
    bc = bc_ref[0].astype(F32)
    b128, c128 = bc[:, :SSD_BC], bc[:, SSD_BC:]
    b128_b = bc_ref[0, :, :SSD_BC]
    c128_b = bc_ref[0, :, SSD_BC:]
    ti = lax.broadcasted_iota(jnp.int32, (n, n), 0)
    si = lax.broadcasted_iota(jnp.int32, (n, n), 1)
    below, above = si < ti, si > ti
    zero_b = jnp.zeros((), BF16)
    for g in range(2):
        cg = jnp.where(lo if g == 0 else jnp.logical_not(lo), c128_b, zero_b)
        cb = _dot_nt(cg, b128_b)
        cdup = _group_dup(c128, g, lo)
        bdup = _group_dup(b128, g, lo)
        for j in range(g * (SSD_PAIRS // 2), (g + 1) * (SSD_PAIRS // 2)):
            ms = []
            for h in (2 * j, 2 * j + 1):
                e_f = cum2[:, h:h + 1] - rt[h:h + 1, :]
                e_b = cum2[:, SSD_B_OFF + h:SSD_B_OFF + h + 1] - rt[SSD_B_OFF + h:SSD_B_OFF + h + 1, :]
                e = jnp.where(below, e_f, jnp.where(above, e_b, rt[2 * SSD_HEADS + h:2 * SSD_HEADS + h + 1, :]))
                ms.append((cb * jnp.exp2(e)).astype(BF16))
            sl = slice(LANES * j, LANES * (j + 1))
            xp = x_ref[0, :, sl]
            cs_f = (cdup * ecum_f[:, sl]).astype(BF16)
            cs_b = (cdup * ecum_b[:, sl]).astype(BF16)
            lhs = jnp.concatenate(ms + [cs_f, cs_b], axis=1)
            rhs = jnp.concatenate([jnp.where(lo, xp, zero_b), jnp.where(lo, zero_b, xp),
                                   st_ref[j].astype(BF16), sb_ref[0, 0, j]], axis=0)
            o_ref[0, :, sl] = _dot(lhs, rhs) + d_ref[:, sl] * xp.astype(F32)
            _ssd_state_step(st_ref, j, bdup * wst[:, sl], xp, elast, 0)


def _ssd(cvs, sm3, dt_bias, a_log, d_e, n_ctx):
    nb, t, _ = cvs.shape
    tile = SSD_TILE
    nt = t // tile
    row = lambda v: jnp.pad(v.reshape(1, -1), ((0, 0), (0, LANES - 2 * SSD_HEADS)))
    const = lambda b, i: (0, 0)
    specs = lambda order: [
        pl.BlockSpec((1, tile, SSD_WIDTH), lambda b, i: (b, order(i), 0)),
        pl.BlockSpec((1, tile, 2 * SSD_BC), lambda b, i: (b, order(i), SSD_WIDTH // (2 * SSD_BC))),
        pl.BlockSpec((1, tile, LANES), lambda b, i: (b, order(i), 0)),
        pl.BlockSpec((1, LANES), const),
        pl.BlockSpec((1, LANES), const)]
    st_block = (1, 1, SSD_PAIRS, LANES, LANES)
    args = (cvs, cvs, sm3, row(dt_bias), row(a_log))
    sel = np.zeros((2, LANES, SSD_WIDTH), np.float32)
    for h in range(SSD_HEADS):
        sel[0, h, h * SSD_HEAD_DIM:(h + 1) * SSD_HEAD_DIM] = 1.0
        sel[1, SSD_B_OFF + h, h * SSD_HEAD_DIM:(h + 1) * SSD_HEAD_DIM] = 1.0
    e_f, e_b = jnp.asarray(sel[0], BF16), jnp.asarray(sel[1], BF16)
    e_spec = pl.BlockSpec((LANES, SSD_WIDTH), const)
    bwd = _tile_order(nt, True, n_ctx // tile)
    states_b = pl.pallas_call(
        _ssd_states_kernel,
        out_shape=jax.ShapeDtypeStruct((nb, nt) + st_block[2:], BF16),
        grid=(nb, nt),
        in_specs=specs(bwd) + [e_spec],
        out_specs=pl.BlockSpec(st_block, lambda b, i: (b, bwd(i), 0, 0, 0)),
        scratch_shapes=[pltpu.VMEM(st_block[2:], F32)],
        compiler_params=_cparams(("parallel", "arbitrary")),
        name="ssd_states",
    )(*args, e_b)
    fwd = _tile_order(nt, False)
    return pl.pallas_call(
        _ssd_out_kernel,
        out_shape=jax.ShapeDtypeStruct((nb, t, SSD_WIDTH), F32),
        grid=(nb, nt),
        in_specs=specs(fwd) + [e_spec, e_spec, pl.BlockSpec((1, SSD_WIDTH), const),
                               pl.BlockSpec(st_block, lambda b, i: (b, i, 0, 0, 0))],
        out_specs=pl.BlockSpec((1, tile, SSD_WIDTH), lambda b, i: (b, i, 0)),
        scratch_shapes=[pltpu.VMEM(st_block[2:], F32)],
        compiler_params=_cparams(("parallel", "arbitrary")),
        name="ssd_out",
    )(*args, e_f, e_b, d_e, states_b)


ML_GATE = 32
ML_ND = 2 * ML_HEADS
ML_AUG = ML_V_DIM + LANES


def _ml_gates(sm_ref, ib_ref, fb_ref, n, both):
    lane = lax.broadcasted_iota(jnp.int32, (1, LANES), 1)
    valid = (lane >= ML_GATE) & (lane < ML_GATE + ML_ND)
    is_f = lane < ML_GATE + ML_HEADS
    sm = sm_ref[0]
    li = sm + ib_ref[...]
    lf = pltpu.roll(_log_sigmoid(sm + fb_ref[...]), LANES - ML_ND, axis=1)
    parts = _split(jnp.where(valid, lf, 0.0), 3)
    upp = jnp.where(_causal(n, True), 1.0, 0.0).astype(BF16)
    bcum = sum(_dot(upp, p) for p in parts)
    if both:
        low = jnp.where(_causal(n, False), 1.0, 0.0).astype(BF16)
        bcum = jnp.where(is_f, sum(_dot(low, p) for p in parts), bcum)
    return valid, is_f, bcum, jnp.where(valid, li - bcum, 0.0)


def _ml_state_step(cn_ref, h, cn, keep, k, ws_dense, v_aug):
    w3 = jnp.concatenate([ws_dense.astype(BF16)] * (ML_AUG // LANES), axis=1)
    cn_ref[h] = keep * cn + _dot_tn(k, w3 * v_aug)


def _ml_v_aug(v_ref, h, n):
    return jnp.concatenate([v_ref[0, :, ML_V_DIM * h:ML_V_DIM * (h + 1)], jnp.ones((n, LANES), BF16)], axis=1)


def _ml_states_kernel(qk_ref, v_ref, sm_ref, ib_ref, fb_ref, sel_ref, cn_out, m_out, cn_ref, m_ref):
    n = qk_ref.shape[1]

    @pl.when(pl.program_id(1) == 0)
    def _init():
        cn_ref[...] = jnp.zeros_like(cn_ref)
        m_ref[...] = jnp.zeros_like(m_ref)

    valid, is_f, bcum, a = _ml_gates(sm_ref, ib_ref, fb_ref, n, False)
    cn_out[0, 0] = cn_ref[...].astype(cn_out.dtype)
    m_out[0, 0] = m_ref[...]
    m_prev = m_ref[0:1, :]
    g_last = jnp.maximum(m_prev, jnp.max(a, axis=0, keepdims=True))
    ws = _expand(jnp.exp(a - g_last), sel_ref[:, ML_HEADS * LANES:])
    keep = jnp.exp(m_prev - g_last)
    for h in range(ML_HEADS):
        lane_b = ML_GATE + ML_HEADS + h
        k = qk_ref[0, :, ML_QK_WIDTH + ML_QK_DIM * h:ML_QK_WIDTH + ML_QK_DIM * (h + 1)]
        _ml_state_step(cn_ref, h, cn_ref[h], keep[:, lane_b:lane_b + 1], k,
                       ws[:, LANES * h:LANES * (h + 1)], _ml_v_aug(v_ref, h, n))
    m_ref[...] = jnp.broadcast_to(bcum[0:1, :] + g_last, m_ref.shape)


def _ml_out_kernel(qk_ref, v_ref, sm_ref, ib_ref, fb_ref, sel_ref, cnb_ref, mb_ref, o_ref, cn_ref, m_ref):
    n = qk_ref.shape[1]

    @pl.when(pl.program_id(1) == 0)
    def _init():
        cn_ref[...] = jnp.zeros_like(cn_ref)
        m_ref[...] = jnp.zeros_like(m_ref)

    valid, is_f, bcum, a = _ml_gates(sm_ref, ib_ref, fb_ref, n, True)
    m_prev = jnp.where(is_f, m_ref[0:1, :], mb_ref[0, 0, 0:1, :])
    a_t = a.T
    pre = suf = a_t[ML_GATE:ML_GATE + ML_ND, :]
    pos = lax.broadcasted_iota(jnp.int32, (ML_ND, n), 1)
    k = 1
    while k < n:
        pre = jnp.maximum(pre, jnp.where(pos >= k, pltpu.roll(pre, k, axis=1), -jnp.inf))
        suf = jnp.maximum(suf, jnp.where(pos < n - k, pltpu.roll(suf, n - k, axis=1), -jnp.inf))
        k *= 2
    run = jnp.where(lax.broadcasted_iota(jnp.int32, (ML_ND, n), 0) < ML_HEADS, pre, suf)
    run = jnp.concatenate([jnp.zeros((ML_GATE, n), F32), run,
                           jnp.zeros((LANES - ML_GATE - ML_ND, n), F32)], axis=0).T
    g = jnp.maximum(m_prev, run)
    m_t = bcum + g
    floor = jnp.exp(-m_t)
    dense = lambda x, lane_: jnp.broadcast_to(x[:, lane_:lane_ + 1], (n, LANES))
    diag_t = jnp.exp(a - g).T
    g_last = g[n - 1:n, :]
    ws = _expand(jnp.exp(a - g_last), sel_ref[:, :ML_HEADS * LANES])
    keep = jnp.exp(m_prev - g_last)
    ti = lax.broadcasted_iota(jnp.int32, (n, n), 0)
    si = lax.broadcasted_iota(jnp.int32, (n, n), 1)
    not_above, above, on_diag = si <= ti, si > ti, si == ti
    wide = lambda x: jnp.concatenate([x] * (n // LANES), axis=1)
    for h in range(ML_HEADS):
        q = qk_ref[0, :, ML_QK_DIM * h:ML_QK_DIM * (h + 1)]
        k_h = qk_ref[0, :, ML_QK_WIDTH + ML_QK_DIM * h:ML_QK_WIDTH + ML_QK_DIM * (h + 1)]
        v_aug = _ml_v_aug(v_ref, h, n)
        lanes = (ML_GATE + h, ML_GATE + ML_HEADS + h)
        g_d = [dense(g, ln) for ln in lanes]
        qk = _dot_nt(q, k_h)
        e = jnp.where(not_above, a_t[lanes[0]:lanes[0] + 1, :] - wide(g_d[0]),
                      a_t[lanes[1]:lanes[1] + 1, :] - wide(g_d[1]))
        p = qk * jnp.exp(e)
        p_f = jnp.where(not_above, p, 0.0).astype(BF16)
        p_b = jnp.where(above, p, jnp.where(on_diag, qk * diag_t[lanes[1]:lanes[1] + 1, :], 0.0)).astype(BF16)
        intra = _dot(jnp.concatenate([p_f, p_b], axis=0), v_aug)
        cn_f = cn_ref[h]
        inter = _dot(q, jnp.concatenate([cn_f.astype(BF16), cnb_ref[0, 0, h]], axis=1))
        out = None
        for d in range(2):
            w_inter = jnp.exp(m_prev[:, lanes[d]:lanes[d] + 1] - g_d[d])
            s = (intra[n * d:n * (d + 1)]
                 + jnp.concatenate([w_inter] * (ML_AUG // LANES), axis=1) * inter[:, ML_AUG * d:ML_AUG * (d + 1)])
            rn = 1.0 / jnp.maximum(jnp.abs(s[:, ML_V_DIM:]), dense(floor, lanes[d]))
            hid = s[:, :ML_V_DIM] * jnp.concatenate([rn] * (ML_V_DIM // LANES), axis=1)
            out = hid if out is None else out + hid
        o_ref[0, :, ML_V_DIM * h:ML_V_DIM * (h + 1)] = out
        _ml_state_step(cn_ref, h, cn_f, keep[:, lanes[0]:lanes[0] + 1], k_h,
                       ws[:, LANES * h:LANES * (h + 1)], v_aug)
    m_ref[...] = jnp.broadcast_to(m_t[n - 1:n, :], m_ref.shape)


def _mlstm(cvm, p3, sm3, i_bias, f_bias, n_ctx):
    nb, t, _ = cvm.shape
    nt = t // TILE
    assert _SM_OFF["i_f"] == ML_GATE and _SM_OFF["f_f"] == ML_GATE + ML_ND
    row = lambda v, off: jnp.pad(v.reshape(1, -1), ((0, 0), (off, LANES - off - ML_ND)))
    sel = np.zeros((LANES, ML_ND * LANES), np.float32)
    for r in range(ML_ND):
        sel[ML_GATE + r, r * LANES:(r + 1) * LANES] = 1.0
    const = lambda b, i: (0, 0)
    specs = lambda order: [
        pl.BlockSpec((1, TILE, 2 * ML_QK_WIDTH), lambda b, i: (b, order(i), 0)),
        pl.BlockSpec((1, TILE, ML_V_WIDTH), lambda b, i: (b, order(i), _P_OFF["m_v"] // ML_V_WIDTH)),
        pl.BlockSpec((1, TILE, LANES), lambda b, i: (b, order(i), 0)),
        pl.BlockSpec((1, LANES), const),
        pl.BlockSpec((1, LANES), const),
        pl.BlockSpec((LANES, ML_ND * LANES), const)]
    args = (cvm, p3, sm3, row(i_bias, ML_GATE), row(f_bias, ML_GATE + ML_ND), jnp.asarray(sel, BF16))
    cn_block = (1, 1, ML_HEADS, ML_QK_DIM, ML_AUG)
    m_block = (1, 1, SUBLANES, LANES)
    scratch = [pltpu.VMEM(cn_block[2:], F32), pltpu.VMEM(m_block[2:], F32)]
    bwd = _tile_order(nt, True, n_ctx // TILE)
    st_idx = lambda b, i: (b, bwd(i)) + (0,) * 3
    cn_b, m_b = pl.pallas_call(
        _ml_states_kernel,
        out_shape=(jax.ShapeDtypeStruct((nb, nt) + cn_block[2:], BF16),
                   jax.ShapeDtypeStruct((nb, nt) + m_block[2:], F32)),
        grid=(nb, nt),
        in_specs=specs(bwd),
        out_specs=(pl.BlockSpec(cn_block, st_idx), pl.BlockSpec(m_block, lambda b, i: (b, bwd(i), 0, 0))),
        scratch_shapes=scratch,
        compiler_params=_cparams(("parallel", "arbitrary")),
        name="mlstm_states",
    )(*args)
    fwd = _tile_order(nt, False)
    return pl.pallas_call(
        _ml_out_kernel,
        out_shape=jax.ShapeDtypeStruct((nb, t, ML_V_WIDTH), F32),
        grid=(nb, nt),
        in_specs=specs(fwd) + [pl.BlockSpec(cn_block, lambda b, i: (b, i, 0, 0, 0)),
                               pl.BlockSpec(m_block, lambda b, i: (b, i, 0, 0))],
        out_specs=pl.BlockSpec((1, TILE, ML_V_WIDTH), lambda b, i: (b, i, 0)),
        scratch_shapes=scratch,
        compiler_params=_cparams(("parallel", "arbitrary")),
        name="mlstm_out",
    )(*args, cn_b, m_b)


GLA_SUB = 256
GLA_NCH = GLA_SUB // GLA_CHUNK
GLA_COLS = BF16_SUBLANES


def _gla_layout(is_ctx, lat_rows):
    r = np.arange(GLA_SUB)
    if is_ctx:
        return r // GLA_CHUNK, r % GLA_CHUNK
    col = r % SUBLANES
    cpc = GLA_CHUNK // lat_rows
    return col // cpc, (col % cpc) * lat_rows + r // SUBLANES


def _gla_consts(is_ctx, lat_rows, rev):
    ch, pos = _gla_layout(is_ctx, lat_rows)
    same = ch[:, None] == ch[None, :]
    before = (pos[None, :] >= pos[:, None]) if rev else (pos[None, :] <= pos[:, None])
    tri = (same & before).astype(np.float32)
    cmask = np.stack([np.repeat((ch == j)[:, None], LANES, axis=1) for j in range(GLA_NCH)])
    return tri, cmask.astype(np.float32)


def _gla_row(is_ctx, lat_rows, j, p):
    ch, pos = _gla_layout(is_ctx, lat_rows)
    return int(np.nonzero((ch == j) & (pos == p))[0][0])


def _per_chunk_rows(b, is_ctx, lat_rows, p):
    rows = [b[_gla_row(is_ctx, lat_rows, j, p):_gla_row(is_ctx, lat_rows, j, p) + 1, :]
            for j in range(GLA_NCH)]
    w = b.shape[1]
    if is_ctx:
        full = jnp.concatenate([jnp.broadcast_to(r, (GLA_CHUNK, w)) for r in rows], axis=0)
    else:
        rep = SUBLANES // GLA_NCH
        pat = jnp.concatenate([jnp.broadcast_to(r, (rep, w)) for r in rows], axis=0)
        full = jnp.broadcast_to(pat[None], (GLA_SUB // SUBLANES, SUBLANES, w)).reshape(GLA_SUB, w)
    return rows, full


def _gla_sub(q, k, v, araw, aup, abias, tri_b, tri_f, cmask_ref, st_ref, *, rev, is_ctx, lat_rows):
    g = _log_sigmoid(_dot_hp(araw, aup) + abias) * (1.0 / GLA_TAU)
    b = _dot_3x(tri_b, g)
    _, ref = _per_chunk_rows(b, is_ctx, lat_rows, GLA_CHUNK // 2)
    lasts, last = _per_chunk_rows(b, is_ctx, lat_rows, 0 if rev else GLA_CHUNK - 1)
    qs = q * (GLA_K_DIM ** -0.5)
    qe = (qs * jnp.exp(b - ref)).astype(BF16)
    ke = (k * jnp.exp(ref - b)).astype(BF16)
    qb = (qs * jnp.exp(b)).astype(BF16)
    kl = (k * jnp.exp(last - b)).astype(BF16)
    visible = tri_f > 0.0
    order = range(GLA_NCH - 1, -1, -1) if rev else range(GLA_NCH)
    outs = []
    for h in range(GLA_HEADS):
        ks = slice(GLA_K_DIM * h, GLA_K_DIM * (h + 1))
        vh = v[:, GLA_V_DIM * h:GLA_V_DIM * (h + 1)]
        att = jnp.where(visible, _dot_nt(qe[:, ks], ke[:, ks]), 0.0).astype(BF16)
        o = _dot(att, vh)
        klm = jnp.concatenate([kl[:, ks] * cmask_ref[j] for j in range(GLA_NCH)], axis=1)
        upd = _dot_tn(vh, klm)
        s = st_ref[h]
        s_in = [None] * GLA_NCH
        for j in order:
            s_in[j] = s.astype(BF16)
            s = s * jnp.exp(lasts[j][:, ks]) + upd[:, GLA_K_DIM * j:GLA_K_DIM * (j + 1)]
        st_ref[h] = s
        qbm = jnp.concatenate([qb[:, ks] * cmask_ref[j] for j in range(GLA_NCH)], axis=1)
        outs.append(o + _dot_nt(qbm, jnp.concatenate(s_in, axis=1)))
    return jnp.concatenate(outs, axis=1)


def _gla_kernel(qc_ref, kc_ref, vc_ref, sc_ref, ql_ref, kl_ref, vl_ref, sl_ref,
                aup_ref, ab_ref, tcb_ref, tcf_ref, cmc_ref, tlb_ref, tlf_ref, cml_ref,
                *rest, rev, ctx_rows, lat_rows, n_cblk):
    acc_ref, o_ref, st_ref, ctxo_ref = rest if len(rest) == 4 else (None,) + rest
    i = pl.program_id(1)
    aup = aup_ref[...]
    abias = ab_ref[...]
    r0, r1 = ctx_rows, ctx_rows + lat_rows
    sub = functools.partial(_gla_sub, aup=aup, abias=abias, st_ref=st_ref, rev=rev, lat_rows=lat_rows)

    @pl.when(i == 0)
    def _ctx():
        st_ref[...] = jnp.zeros_like(st_ref)
        ctxo_ref[...] = sub(qc_ref[0].astype(F32), kc_ref[0].astype(F32), vc_ref[0], sc_ref[0],
                            tri_b=tcb_ref[...], tri_f=tcf_ref[...], cmask_ref=cmc_ref, is_ctx=True)

    @pl.when(i > 0)
    def _lat():
        cblk = (n_cblk - i) if rev else (i - 1)
        halves = range(GLA_COLS // SUBLANES)
        for half in (reversed(halves) if rev else halves):
            cs = slice(SUBLANES * half, SUBLANES * (half + 1))
            take = lambda r: r[0, r0:r1].astype(F32)[:, cs, :].reshape(GLA_SUB, r.shape[-1])
            o = sub(take(ql_ref), take(kl_ref), take(vl_ref).astype(BF16), take(sl_ref),
                    tri_b=tlb_ref[...], tri_f=tlf_ref[...], cmask_ref=cml_ref, is_ctx=False)
            _write(o_ref, (0, slice(r0, r1), cs, slice(None)),
                   o.reshape(lat_rows, SUBLANES, GLA_V_WIDTH), acc_ref)
        for r in range(ctx_rows):
            start = pl.multiple_of(r * GRID_W + cblk * GLA_COLS, GLA_COLS)
            _write(o_ref, (0, r, slice(None), slice(None)), ctxo_ref[pl.ds(start, GLA_COLS), :], acc_ref)


def _gla_scan(p3, sm3, a_up, a_bias, n_ctx, rev, acc):
    nb, t, ncol = p3.shape
    rows = t // GRID_W
    ctx_rows = n_ctx // GRID_W
    lat_rows = rows - ctx_rows
    n_cblk = GRID_W // GLA_COLS
    p4 = p3.reshape(nb, rows, GRID_W, ncol)
    sm4 = sm3.reshape(nb, rows, GRID_W, LANES)
    a_off = _SM_OFF["a_b"] if rev else _SM_OFF["a_f"]
    aup = jnp.pad(a_up, ((a_off, LANES - a_off - GLA_RANK), (0, 0)))
    cblk = lambda i: jnp.where(i == 0, n_cblk - 1 if rev else 0, (n_cblk - i) if rev else (i - 1))
    ctx = lambda blk: (lambda b, i: (b, 0, blk))
    lat = lambda blk: (lambda b, i: (b, 0, cblk(i), blk))
    const2 = lambda b, i: (0, 0)
    const3 = lambda b, i: (0, 0, 0)
    widths = (GLA_K_WIDTH, GLA_K_WIDTH, GLA_V_WIDTH)
    offs = (_P_OFF["g_q"], _P_OFF["g_k"], _P_OFF["g_v"])
    consts = []
    const_specs = []
    for is_ctx in (True, False):
        tri, cmask = _gla_consts(is_ctx, lat_rows, rev)
        consts += [jnp.asarray(tri, BF16), jnp.asarray(tri, F32), jnp.asarray(cmask, BF16)]
        const_specs += [pl.BlockSpec((GLA_SUB, GLA_SUB), const2), pl.BlockSpec((GLA_SUB, GLA_SUB), const2),
                        pl.BlockSpec((GLA_NCH, GLA_SUB, LANES), const3)]
    out_spec = pl.BlockSpec((1, rows, GLA_COLS, GLA_V_WIDTH), lat(0))
    in_specs = ([pl.BlockSpec((1, n_ctx, w), ctx(o // w)) for w, o in zip(widths, offs)]
                + [pl.BlockSpec((1, n_ctx, LANES), ctx(0))]
                + [pl.BlockSpec((1, rows, GLA_COLS, w), lat(o // w)) for w, o in zip(widths, offs)]
                + [pl.BlockSpec((1, rows, GLA_COLS, LANES), lat(0))]
                + [pl.BlockSpec((LANES, GLA_K_WIDTH), const2), pl.BlockSpec((1, GLA_K_WIDTH), const2)]
                + const_specs)
    args = [p3, p3, p3, sm3, p4, p4, p4, sm4, aup, a_bias.reshape(1, GLA_K_WIDTH)] + consts
    if acc is not None:
        in_specs.append(out_spec)
        args.append(acc.reshape(nb, rows, GRID_W, GLA_V_WIDTH))
    out = pl.pallas_call(
        functools.partial(_gla_kernel, rev=rev, ctx_rows=ctx_rows, lat_rows=lat_rows, n_cblk=n_cblk),
        out_shape=jax.ShapeDtypeStruct((nb, rows, GRID_W, GLA_V_WIDTH), F32),
        grid=(nb, n_cblk + 1),
        in_specs=in_specs,
        out_specs=out_spec,
        scratch_shapes=[pltpu.VMEM((GLA_HEADS, GLA_V_DIM, GLA_K_DIM), F32),
                        pltpu.VMEM((n_ctx, GLA_V_WIDTH), F32)],
        input_output_aliases={} if acc is None else {len(args) - 1: 0},
        compiler_params=_cparams(("parallel", "arbitrary")),
        name="gla_scan",
    )(*args)
    return out.reshape(nb, t, GLA_V_WIDTH)


def _group_rmsnorm(y, groups):
    width = y.shape[-1] // groups
    ones = jnp.ones((width, LANES), BF16)
    out = []
    for g in range(groups):
        yg = y[:, width * g:width * (g + 1)]
        ms = _dot((yg * yg).astype(BF16), ones) * (1.0 / width)
        out.append(yg * jnp.concatenate([lax.rsqrt(ms + EPS)] * (width // LANES), axis=1))
    return jnp.concatenate(out, axis=1)


def _post_kernel(x_ref, y_ref, h_ref, o_ref_in, z_ref, mo_ref, gg_ref, gs_ref, gm_ref, gl_ref,
                 nws_ref, nwm_ref, nwg_ref, wbs_ref, wbm_ref, wbg_ref, wout_ref, g1_ref,
                 nwf_ref, sc2_ref, sh2_ref, xo_ref, ho_ref):
    y_ssd = (_group_rmsnorm(y_ref[0] * _silu(z_ref[0]), 2) * nws_ref[...]).astype(BF16)
    y_ml = (_group_rmsnorm(h_ref[0], ML_HEADS) * nwm_ref[...] * _sigmoid(mo_ref[0])).astype(BF16)
    y_gla = (_group_rmsnorm(o_ref_in[0], GLA_HEADS) * nwg_ref[...] * _silu(gg_ref[0])).astype(BF16)
    merged = (_sigmoid(gs_ref[0]) * _dot(y_ssd, wbs_ref[...])
              + _sigmoid(gm_ref[0]) * _dot(y_ml, wbm_ref[...])
              + _sigmoid(gl_ref[0]) * _dot(y_gla, wbg_ref[...]))
    x_new = x_ref[0] + g1_ref[0] * _dot(merged.astype(BF16), wout_ref[...])
    xo_ref[0] = x_new
    ho_ref[0] = (_rms(x_new) * nwf_ref[...] * (1.0 + sc2_ref[0]) + sh2_ref[0]).astype(ho_ref.dtype)


def _post(x, scans, p3, norm_ws, w_bs, w_out, norm_ffn_w, mods, n_ctx, tile0):
    nb, t, d = x.shape
    nt = t // TILE - tile0
    ctx_tiles = n_ctx // TILE
    tok = lambda blk: (lambda b, i: (b, i + tile0, blk))
    out = lambda b, i: (b, i, 0)
    const = lambda b, i: (0, 0)
    tok_spec = lambda blk: pl.BlockSpec((1, TILE, d), tok(blk))
    w_spec = pl.BlockSpec((d, d), const, pipeline_mode=pl.Buffered(1))
    vec = pl.BlockSpec((1, d), const)
    mod = lambda k: pl.BlockSpec((1, 1, d), _mod_row(nb, k, ctx_tiles, tile0))
    names = ("s_z", "m_o", "g_g", "gate_ssd", "gate_ml", "gate_gla")
    in_specs = ([tok_spec(0)] * 4 + [tok_spec(_P_OFF[nm] // d) for nm in names]
                + [vec] * 3 + [w_spec] * 4 + [mod(2), vec, mod(4), mod(3)])
    return pl.pallas_call(
        _post_kernel,
        out_shape=(jax.ShapeDtypeStruct((nb, nt * TILE, d), F32),
                   jax.ShapeDtypeStruct((nb, nt * TILE, d), BF16)),
        grid=(nb, nt),
        in_specs=in_specs,
        out_specs=(pl.BlockSpec((1, TILE, d), out), pl.BlockSpec((1, TILE, d), out)),
        compiler_params=_cparams(("parallel", "parallel")),
        name="post",
    )(x, *scans, *([p3] * 6), *[w.reshape(1, d) for w in norm_ws], *w_bs, w_out, mods,
      norm_ffn_w.reshape(1, d), mods, mods)


def _ffn_in_kernel(a_ref, w_ref, o_ref):
    acc = _dot(a_ref[...], w_ref[...])
    half = acc.shape[1] // 2
    o_ref[...] = (_silu(acc[:, :half]) * acc[:, half:]).astype(o_ref.dtype)


def _ffn_in(h, w_gu, half):
    m, k = h.shape
    n_half = w_gu.shape[1] // 2
    tm = _row_tile(m)
    return pl.pallas_call(
        _ffn_in_kernel,
        out_shape=jax.ShapeDtypeStruct((m, n_half), BF16),
        grid=(n_half // half, m // tm),
        in_specs=[pl.BlockSpec((tm, k), lambda j, i: (i, 0)),
                  pl.BlockSpec((k, 2 * half), lambda j, i: (0, j))],
        out_specs=pl.BlockSpec((tm, half), lambda j, i: (i, j)),
        compiler_params=_cparams(("parallel", "parallel")),
        name="ffn_in",
    )(h, w_gu)


def _ffn_out_kernel(a_ref, w_ref, x_ref, g_ref, nw_ref, sc_ref, sh_ref, xo_ref, ho_ref):
    x_new = x_ref[0] + g_ref[0] * _dot(a_ref[0], w_ref[...])
    xo_ref[0] = x_new
    ho_ref[0] = (_rms(x_new) * nw_ref[...] * (1.0 + sc_ref[0]) + sh_ref[0]).astype(ho_ref.dtype)


def _ffn_out_last_kernel(a_ref, w_ref, x_ref, g_ref, nw_ref, o_ref):
    x_new = x_ref[0] + g_ref[0] * _dot(a_ref[0], w_ref[...])
    o_ref[0] = _rms(x_new) * nw_ref[...]


def _ffn_out(a, w, x, mods, n_ctx, tile0, next_norm_w, next_mods):
    nb, t, d = x.shape
    k = a.shape[-1]
    ctx_tiles = n_ctx // TILE
    tok = lambda b, i: (b, i, 0)
    mod = lambda k_, m=None: pl.BlockSpec((1, 1, d), _mod_row(nb, k_, ctx_tiles, tile0))
    in_specs = [pl.BlockSpec((1, TILE, k), tok),
                pl.BlockSpec((k, d), lambda b, i: (0, 0), pipeline_mode=pl.Buffered(1)),
                pl.BlockSpec((1, TILE, d), tok),
                mod(5),
                pl.BlockSpec((1, d), lambda b, i: (0, 0))]
    args = [a, w, x, mods, next_norm_w.reshape(1, d)]
    if next_mods is None:
        body = _ffn_out_last_kernel
        out_shape = jax.ShapeDtypeStruct((nb, t, d), F32)
        out_specs = pl.BlockSpec((1, TILE, d), tok)
    else:
        body = _ffn_out_kernel
        in_specs += [mod(1), mod(0)]
        args += [next_mods, next_mods]
        out_shape = (jax.ShapeDtypeStruct((nb, t, d), F32), jax.ShapeDtypeStruct((nb, t, d), BF16))
        out_specs = (pl.BlockSpec((1, TILE, d), tok), pl.BlockSpec((1, TILE, d), tok))
    return pl.pallas_call(
        body,
        out_shape=out_shape,
        grid=(nb, t // TILE),
        in_specs=in_specs,
        out_specs=out_specs,
        compiler_params=_cparams(("parallel", "parallel")),
        name="ffn_out",
    )(*args)


def _proj_weights(w_in):
    main = jnp.concatenate([w_in[:, _IN_OFF[nm]:_IN_OFF[nm] + _IN_W[nm]] for nm in _P_ORDER], axis=1)
    small = [w_in[:, _IN_OFF[nm]:_IN_OFF[nm] + _IN_W[nm]] for nm in _SMALL]
    small.append(jnp.zeros((w_in.shape[0], LANES - N_SMALL_USED), w_in.dtype))
    return main.astype(BF16), jnp.concatenate(small, axis=1).astype(BF16)


def _ffn_weight(w_ffn_in, half):
    d_ff = w_ffn_in.shape[1] // 2
    cols = []
    for j in range(d_ff // half):
        cols.append(w_ffn_in[:, j * half:(j + 1) * half])
        cols.append(w_ffn_in[:, d_ff + j * half:d_ff + (j + 1) * half])
    return jnp.concatenate(cols, axis=1).astype(BF16)


def kernel(x, c, ctx, c_ctx, w_mod, b_mod, norm_mix_w, norm_ffn_w, w_in, ssd_conv_w, ssd_conv_b, ssd_dt_bias, ssd_a_log, ssd_d, ssd_norm_w, ml_conv_w, ml_conv_b, ml_i_bias, ml_f_bias, ml_norm_w, gla_a_up, gla_a_bias, gla_norm_w, w_b_ssd, w_b_ml, w_b_gla, w_out, w_ffn_in, w_ffn_out, final_norm_w):
    nb, n_lat, d = x.shape
    n_ctx = ctx.shape[1]
    t = n_ctx + n_lat
    depth = w_in.shape[0]
    d_ff = w_ffn_out.shape[1]
    assert n_ctx == TILE == GLA_SUB and n_lat % TILE == 0 and n_lat // GRID_W == 32
    ffn_half = d_ff // 2

    xs = jnp.concatenate([ctx, x], axis=1)
    c16 = jnp.pad(jnp.concatenate([c, c_ctx[None]], axis=0), ((0, 2 * SUBLANES - nb - 1), (0, 0)))
    mods = [_modulation(c16, w_mod[l], b_mod[l]).reshape(2 * SUBLANES * 6, 1, d) for l in range(depth)]
    h = _norm_mod(xs, norm_mix_w[0], mods[0], 0, 1, n_ctx)
    ml_post = jnp.concatenate([jnp.ones((ML_QK_WIDTH,), F32),
                               jnp.full((ML_QK_WIDTH,), ML_QK_DIM ** -0.5, F32)])
    for l in range(depth):
        last = l == depth - 1
        w_main, w_small = _proj_weights(w_in[l])
        h2d = h.reshape(nb * t, d)
        p3 = _matmul(h2d, w_main, N_PROJ // 5, BF16).reshape(nb, t, N_PROJ)
        sm3 = _matmul(h2d, w_small, LANES, F32).reshape(nb, t, LANES)
        cvs = _conv(p3, _P_OFF["s_x"], ssd_conv_w[l], ssd_conv_b[l],
                    jnp.ones((ssd_conv_w.shape[-1],), F32), n_ctx)
        cvm = _conv(p3, _P_OFF["m_q"], ml_conv_w[l], ml_conv_b[l], ml_post, n_ctx)
        d_e = jnp.repeat(ssd_d[l], SSD_HEAD_DIM).reshape(1, SSD_WIDTH)
        y = _ssd(cvs, sm3, ssd_dt_bias[l], ssd_a_log[l], d_e, n_ctx)
        hm = _mlstm(cvm, p3, sm3, ml_i_bias[l], ml_f_bias[l], n_ctx)
        og = None
        for rev in (False, True):
            k = int(rev)
            og = _gla_scan(p3, sm3, gla_a_up[l, k], gla_a_bias[l, k], n_ctx, rev, og)
        tile0 = n_ctx // TILE if last else 0
        xs, h2 = _post(xs, (y, hm, og), p3, (ssd_norm_w[l], ml_norm_w[l], gla_norm_w[l]),
                       (w_b_ssd[l].astype(BF16), w_b_ml[l].astype(BF16), w_b_gla[l].astype(BF16)),
                       w_out[l].astype(BF16), norm_ffn_w[l], mods[l], n_ctx, tile0)
        nt = xs.shape[1]
        a = _ffn_in(h2.reshape(nb * nt, d), _ffn_weight(w_ffn_in[l], ffn_half), ffn_half)
        a = a.reshape(nb, nt, d_ff)
        if last:
            return _ffn_out(a, w_ffn_out[l].astype(BF16), xs, mods[l], n_ctx, tile0, final_norm_w, None)
        xs, h = _ffn_out(a, w_ffn_out[l].astype(BF16), xs, mods[l], n_ctx, tile0,
                         norm_mix_w[l + 1], mods[l + 1])
```

```python
import functools

import numpy as np
import jax
import jax.numpy as jnp
from jax import lax
from jax.experimental import pallas as pl
from jax.experimental.pallas import tpu as pltpu

F32 = jnp.float32
BF16 = jnp.bfloat16

EPS = 1e-6
LOG2E = 1.4426950408889634
GRID_W = 64
SSD_HEADS = 16
SSD_HEAD_DIM = 64
SSD_WIDTH = 1024
SSD_STATE = 64
SSD_BC = 128
ML_HEADS = 4
ML_QK_DIM = 128
ML_V_DIM = 256
ML_QK_WIDTH = 512
ML_V_WIDTH = 1024
GLA_HEADS = 4
GLA_K_DIM = 128
GLA_V_DIM = 256
GLA_K_WIDTH = 512
GLA_V_WIDTH = 1024
GLA_RANK = 16
GLA_TAU = 16.0
GLA_CHUNK = 64

LANES = 128
SUBLANES = 8
BF16_SUBLANES = 16
VMEM_LIMIT = 56 * 1024 * 1024

TILE = 256

_IN_NAMES = ("s_x", "s_z", "s_b", "s_c", "dt_f", "dt_b",
             "m_q", "m_k", "m_v", "m_o", "i_f", "i_b", "f_f", "f_b",
             "g_q", "g_k", "g_v", "g_g", "a_f", "a_b",
             "gate_ssd", "gate_ml", "gate_gla")
_IN_WIDTHS = (1024, 1024, 128, 128, 16, 16,
              512, 512, 1024, 1024, 4, 4, 4, 4,
              512, 512, 1024, 1024, 16, 16,
              1024, 1024, 1024)
_IN_OFF = dict(zip(_IN_NAMES, np.concatenate([[0], np.cumsum(_IN_WIDTHS)[:-1]]).tolist()))
_IN_W = dict(zip(_IN_NAMES, _IN_WIDTHS))

_P_ORDER = ("s_z", "m_o", "g_g", "gate_ssd", "gate_ml", "gate_gla", "m_v", "g_v",
            "g_q", "g_k", "s_x", "s_b", "s_c", "m_q", "m_k")
_P_OFF = {}
_o = 0
for _n in _P_ORDER:
    _P_OFF[_n] = _o
    _o += _IN_W[_n]
N_PROJ = _o
_SMALL = ("dt_f", "dt_b", "i_f", "i_b", "f_f", "f_b", "a_f", "a_b")
_SM_OFF = {}
_s = 0
for _n in _SMALL:
    _SM_OFF[_n] = _s
    _s += _IN_W[_n]
N_SMALL_USED = _s


def _cparams(sem):
    return pltpu.CompilerParams(dimension_semantics=sem, vmem_limit_bytes=VMEM_LIMIT)


def _sigmoid(x):
    return 0.5 * jnp.tanh(0.5 * x) + 0.5


def _silu(x):
    return x * _sigmoid(x)


def _softplus(x):
    return jnp.maximum(x, 0.0) + jnp.log1p(jnp.exp(-jnp.abs(x)))


def _log_sigmoid(x):
    return jnp.minimum(x, 0.0) - jnp.log(1.0 + jnp.exp(-jnp.abs(x)))


def _split(x, n):
    out = []
    r = x
    for _ in range(n):
        p = r.astype(BF16)
        out.append(p)
        r = r - p.astype(F32)
    return out


def _dot(a, b):
    return jnp.dot(a, b, preferred_element_type=F32)


def _dot_nt(a, b):
    return lax.dot_general(a, b, (((1,), (1,)), ((), ())), preferred_element_type=F32)


def _dot_tn(a, b):
    return lax.dot_general(a, b, (((0,), (0,)), ((), ())), preferred_element_type=F32)


def _dot_exact_lhs(t, x, pieces):
    return sum(_dot(t, p) for p in _split(x, pieces))


def _dot_hp(a, b):
    ah, am = _split(a, 2)
    bh, bm = _split(b, 2)
    return _dot(ah, bh) + _dot(ah, bm) + _dot(am, bh)


def _causal(n, rev):
    t = lax.broadcasted_iota(jnp.int32, (n, n), 0)
    s = lax.broadcasted_iota(jnp.int32, (n, n), 1)
    return (s >= t) if rev else (s <= t)


def _rms(x):
    return x * lax.rsqrt(jnp.mean(x * x, axis=-1, keepdims=True) + EPS)


def _mod_kernel(c_ref, w_ref, b_ref, o_ref):
    o_ref[...] = _dot_hp(_silu(c_ref[...]), w_ref[...]) + b_ref[...]


def _modulation(c16, w_mod, b_mod, layer):
    rows, d = c16.shape
    n = w_mod.shape[-1]
    tn = 1536
    return pl.pallas_call(
        _mod_kernel,
        out_shape=jax.ShapeDtypeStruct((rows, n), F32),
        grid=(n // tn,),
        in_specs=[pl.BlockSpec((rows, d), lambda j: (0, 0)),
                  pl.BlockSpec((None, d, tn), lambda j: (layer, 0, j)),
                  pl.BlockSpec((None, 1, tn), lambda j: (layer, 0, j))],
        out_specs=pl.BlockSpec((rows, tn), lambda j: (0, j)),
        compiler_params=_cparams(("arbitrary",)),
        name="modulation",
    )(c16, w_mod, b_mod.reshape(b_mod.shape[0], 1, n))


def _mod_row(nb, k, ctx_tiles, tile0=0):
    return lambda b, i: (jnp.where(i + tile0 < ctx_tiles, nb, b) * 6 + k, 0, 0)


def _norm_mod_kernel(x_ref, w_ref, sc_ref, sh_ref, o_ref):
    o_ref[0] = (_rms(x_ref[0]) * w_ref[...] * (1.0 + sc_ref[0]) + sh_ref[0]).astype(o_ref.dtype)


def _norm_mod(x, w, mods, k_shift, k_scale, n_ctx):
    nb, t, d = x.shape
    return pl.pallas_call(
        _norm_mod_kernel,
        out_shape=jax.ShapeDtypeStruct((nb, t, d), BF16),
        grid=(nb, t // TILE),
        in_specs=[pl.BlockSpec((1, TILE, d), lambda b, i: (b, i, 0)),
                  pl.BlockSpec((1, d), lambda b, i: (0, 0)),
                  pl.BlockSpec((1, 1, d), _mod_row(nb, k_scale, n_ctx // TILE)),
                  pl.BlockSpec((1, 1, d), _mod_row(nb, k_shift, n_ctx // TILE))],
        out_specs=pl.BlockSpec((1, TILE, d), lambda b, i: (b, i, 0)),
        compiler_params=_cparams(("parallel", "parallel")),
        name="norm_mod",
    )(x, w.reshape(1, d), mods, mods)


def _mm_kernel(a_ref, w_ref, o_ref):
    o_ref[...] = _dot(a_ref[...], w_ref[...]).astype(o_ref.dtype)


def _row_tile(m):
    return 512 if m % 512 == 0 else TILE


def _matmul(a, w, layer, tn, out_dtype):
    m, k = a.shape
    n = w.shape[-1]
    tm = _row_tile(m)
    return pl.pallas_call(
        _mm_kernel,
        out_shape=jax.ShapeDtypeStruct((m, n), out_dtype),
        grid=(n // tn, m // tm),
        in_specs=[pl.BlockSpec((tm, k), lambda j, i: (i, 0)),
                  pl.BlockSpec((None, k, tn), lambda j, i: (layer, 0, j))],
        out_specs=pl.BlockSpec((tm, tn), lambda j, i: (i, j)),
        compiler_params=_cparams(("parallel", "parallel")),
        name="matmul",
    )(a, w)


CONV_K = 5
CONV_ROWS = 256


def _conv_kernel(u_ref, w_ref, b_ref, s_ref, o_ref, pad_ref, *, n_ctx):
    t, c = u_ref.shape[1], u_ref.shape[2]
    half = CONV_K // 2
    zeros = jnp.zeros((SUBLANES, c), F32)
    w = w_ref[...]
    bias = b_ref[...]
    post = s_ref[...]
    for s0, n in ((0, n_ctx), (n_ctx, t - n_ctx)):
        pad_ref[0:SUBLANES, :] = zeros
        pad_ref[SUBLANES:SUBLANES + n, :] = u_ref[0, s0:s0 + n, :].astype(F32)
        pad_ref[SUBLANES + n:2 * SUBLANES + n, :] = zeros
        for r0 in range(0, n, CONV_ROWS):
            acc = bias
            for j in range(CONV_K):
                lo = SUBLANES - half + j + r0
                acc = acc + w[j:j + 1, :] * pad_ref[lo:lo + CONV_ROWS, :]
            o_ref[0, s0 + r0:s0 + r0 + CONV_ROWS, :] = (_silu(acc) * post).astype(o_ref.dtype)


def _conv(p3, col0, w, b, post_scale, n_ctx):
    nb, t, _ = p3.shape
    width = w.shape[1]
    cb = 256
    return pl.pallas_call(
        functools.partial(_conv_kernel, n_ctx=n_ctx),
        out_shape=jax.ShapeDtypeStruct((nb, t, width), BF16),
        grid=(nb, width // cb),
        in_specs=[pl.BlockSpec((1, t, cb), lambda b_, j: (b_, 0, col0 // cb + j)),
                  pl.BlockSpec((CONV_K, cb), lambda b_, j: (0, j)),
                  pl.BlockSpec((1, cb), lambda b_, j: (0, j)),
                  pl.BlockSpec((1, cb), lambda b_, j: (0, j))],
        out_specs=pl.BlockSpec((1, t, cb), lambda b_, j: (b_, 0, j)),
        scratch_shapes=[pltpu.VMEM((t + 2 * SUBLANES, cb), F32)],
        compiler_params=_cparams(("parallel", "parallel")),
        name="conv",
    )(p3, w, b.reshape(1, width), post_scale.reshape(1, width))


def _tile_order(n_tiles, rev, ctx_tiles=1):
    if rev:
        return lambda i: jnp.where(i < ctx_tiles, ctx_tiles - 1 - i, n_tiles - 1 + ctx_tiles - i)
    return lambda i: i


def _write(o_ref, idx, val, acc_ref):
    o_ref[idx] = val if acc_ref is None else acc_ref[idx] + val


SSD_PAIRS = SSD_HEADS // 2
SSD_B_OFF = SSD_HEADS
SSD_TILE = 256


def _ssd_gates(sm_ref, dtb_ref, alog_ref, n, both):
    lane = lax.broadcasted_iota(jnp.int32, (1, LANES), 1)
    dt = _softplus(sm_ref[0] + dtb_ref[...])
    la = dt * jnp.where(lane < 2 * SSD_HEADS, -jnp.exp(alog_ref[...]), 0.0)
    parts = _split(la, 3)
    upp = jnp.where(_causal(n, True), 1.0, 0.0).astype(BF16)
    cum = sum(_dot(upp, p) for p in parts)
    if both:
        low = jnp.where(_causal(n, False), 1.0, 0.0).astype(BF16)
        cum = jnp.where(lane < SSD_HEADS, sum(_dot(low, p) for p in parts), cum)
    return lane, dt, cum


def _expand(a, e):
    return sum(_dot(p, e) for p in _split(a, 2))


def _group_dup(v, g, lo):
    other = pltpu.roll(v, SSD_STATE, axis=1)
    return jnp.where(lo, v, other) if g == 0 else jnp.where(lo, other, v)


def _ssd_state_step(st_ref, j, bw, xp, elast, off):
    r = lax.broadcasted_iota(jnp.int32, (LANES, LANES), 0) < SSD_STATE
    c = lax.broadcasted_iota(jnp.int32, (LANES, LANES), 1) < SSD_HEAD_DIM
    dec = jnp.where(r, elast[:, off + 2 * j:off + 2 * j + 1], elast[:, off + 2 * j + 1:off + 2 * j + 2])
    st_ref[j] = jnp.where(r == c, dec * st_ref[j] + _dot_tn(bw.astype(BF16), xp), 0.0)


def _ssd_states_kernel(x_ref, bc_ref, sm_ref, dtb_ref, alog_ref, eb_ref, o_ref, st_ref):
    n = x_ref.shape[1]

    @pl.when(pl.program_id(1) == 0)
    def _init():
        st_ref[...] = jnp.zeros_like(st_ref)

    lane, dt, cum = _ssd_gates(sm_ref, dtb_ref, alog_ref, n, False)
    lo = lane < SSD_HEAD_DIM
    last = cum[0:1, :]
    wst = _expand(jnp.exp(last - cum) * dt, eb_ref[...])
    elast = jnp.exp(last)
    b128 = bc_ref[0, :, :SSD_BC].astype(F32)
    o_ref[0, 0] = st_ref[...].astype(o_ref.dtype)
    for j in range(SSD_PAIRS):
        sl = slice(LANES * j, LANES * (j + 1))
        bw = _group_dup(b128, j // (SSD_PAIRS // 2), lo) * wst[:, sl]
        _ssd_state_step(st_ref, j, bw, x_ref[0, :, sl], elast, SSD_B_OFF)


def _ssd_out_kernel(x_ref, bc_ref, sm_ref, dtb_ref, alog_ref, ef_ref, eb_ref, d_ref, sb_ref, o_ref, st_ref):
    n = x_ref.shape[1]

    @pl.when(pl.program_id(1) == 0)
    def _init():
        st_ref[...] = jnp.zeros_like(st_ref)

    zero_b = jnp.zeros((), BF16)
    lo = lax.broadcasted_iota(jnp.int32, (1, LANES), 1) < SSD_HEAD_DIM
    b128_b = bc_ref[0, :, :SSD_BC]
    c128_b = bc_ref[0, :, SSD_BC:]
    cb_all = [_dot_nt(jnp.where(lo if g == 0 else jnp.logical_not(lo), c128_b, zero_b), b128_b)
              for g in range(2)]
    pair_x = lambda j: x_ref[0, :, LANES * j:LANES * (j + 1)]
    rhs_all = [jnp.concatenate([jnp.where(lo, pair_x(j), zero_b), jnp.where(lo, zero_b, pair_x(j)),
                                st_ref[j].astype(BF16), sb_ref[0, 0, j]], axis=0) for j in range(SSD_PAIRS)]

    lane, dt, cum = _ssd_gates(sm_ref, dtb_ref, alog_ref, n, True)
    is_f = lane < SSD_HEADS
    ldt = jnp.log(dt)
    dsum = jnp.log(dt + pltpu.roll(dt, LANES - SSD_B_OFF, axis=1))
    rt = (jnp.where(lane < 2 * SSD_HEADS, cum - ldt, pltpu.roll(dsum, 2 * SSD_HEADS, axis=1)) * LOG2E).T
    cum2 = cum * LOG2E
    last = jnp.where(is_f, cum[n - 1:n, :], cum[0:1, :])
    elast = jnp.exp(last)
    ecum = _split(jnp.exp(cum), 2)
    ecum_f = sum(_dot(p, ef_ref[...]) for p in ecum)
    ecum_b = sum(_dot(p, eb_ref[...]) for p in ecum)
    wst = _expand(jnp.exp(last - cum) * dt, ef_ref[...])
    bc = bc_ref[0].astype(F32)
    b128, c128 = bc[:, :SSD_BC], bc[:, SSD_BC:]
    ti = lax.broadcasted_iota(jnp.int32, (n, n), 0)
    si = lax.broadcasted_iota(jnp.int32, (n, n), 1)
    below, above = si < ti, si > ti
    half = SSD_PAIRS // 2
    for g in range(2):
        cb = cb_all[g]
        cdup = _group_dup(c128, g, lo)
        for j in range(g * half, (g + 1) * half):
            ms = []
            for h in (2 * j, 2 * j + 1):
                e_f = cum2[:, h:h + 1] - rt[h:h + 1, :]
                e_b = cum2[:, SSD_B_OFF + h:SSD_B_OFF + h + 1] - rt[SSD_B_OFF + h:SSD_B_OFF + h + 1, :]
                e = jnp.where(below, e_f, jnp.where(above, e_b, rt[2 * SSD_HEADS + h:2 * SSD_HEADS + h + 1, :]))
                ms.append((cb * jnp.exp2(e)).astype(BF16))
            sl = slice(LANES * j, LANES * (j + 1))
            cs_f = (cdup * ecum_f[:, sl]).astype(BF16)
            cs_b = (cdup * ecum_b[:, sl]).astype(BF16)
            lhs = jnp.concatenate(ms + [cs_f, cs_b], axis=1)
            o_ref[0, :, sl] = _dot(lhs, rhs_all[j]) + d_ref[:, sl] * pair_x(j).astype(F32)
    for j in range(SSD_PAIRS):
        sl = slice(LANES * j, LANES * (j + 1))
        _ssd_state_step(st_ref, j, _group_dup(b128, j // half, lo) * wst[:, sl], x_ref[0, :, sl], elast, 0)


def _ssd(cvs, sm3, dt_bias, a_log, d_e, n_ctx):
    nb, t, _ = cvs.shape
    tile = SSD_TILE
    nt = t // tile
    row = lambda v: jnp.pad(v.reshape(1, -1), ((0, 0), (0, LANES - 2 * SSD_HEADS)))
    const = lambda b, i: (0, 0)
    specs = lambda order: [
        pl.BlockSpec((1, tile, SSD_WIDTH), lambda b, i: (b, order(i), 0)),
        pl.BlockSpec((1, tile, 2 * SSD_BC), lambda b, i: (b, order(i), SSD_WIDTH // (2 * SSD_BC))),
        pl.BlockSpec((1, tile, LANES), lambda b, i: (b, order(i), 0)),
        pl.BlockSpec((1, LANES), const),
        pl.BlockSpec((1, LANES), const)]
    st_block = (1, 1, SSD_PAIRS, LANES, LANES)
    args = (cvs, cvs, sm3, row(dt_bias), row(a_log))
    sel = np.zeros((2, LANES, SSD_WIDTH), np.float32)
    for h in range(SSD_HEADS):
        sel[0, h, h * SSD_HEAD_DIM:(h + 1) * SSD_HEAD_DIM] = 1.0
        sel[1, SSD_B_OFF + h, h * SSD_HEAD_DIM:(h + 1) * SSD_HEAD_DIM] = 1.0
    e_f, e_b = jnp.asarray(sel[0], BF16), jnp.asarray(sel[1], BF16)
    e_spec = pl.BlockSpec((LANES, SSD_WIDTH), const)
    bwd = _tile_order(nt, True, n_ctx // tile)
    states_b = pl.pallas_call(
        _ssd_states_kernel,
        out_shape=jax.ShapeDtypeStruct((nb, nt) + st_block[2:], BF16),
        grid=(nb, nt),
        in_specs=specs(bwd) + [e_spec],
        out_specs=pl.BlockSpec(st_block, lambda b, i: (b, bwd(i), 0, 0, 0)),
        scratch_shapes=[pltpu.VMEM(st_block[2:], F32)],
        compiler_params=_cparams(("parallel", "arbitrary")),
        name="ssd_states",
    )(*args, e_b)
    fwd = _tile_order(nt, False)
    return pl.pallas_call(
        _ssd_out_kernel,
        out_shape=jax.ShapeDtypeStruct((nb, t, SSD_WIDTH), F32),
        grid=(nb, nt),
        in_specs=specs(fwd) + [e_spec, e_spec, pl.BlockSpec((1, SSD_WIDTH), const),
                               pl.BlockSpec(st_block, lambda b, i: (b, i, 0, 0, 0))],
        out_specs=pl.BlockSpec((1, tile, SSD_WIDTH), lambda b, i: (b, i, 0)),
        scratch_shapes=[pltpu.VMEM(st_block[2:], F32)],
        compiler_params=_cparams(("parallel", "arbitrary")),
        name="ssd_out",
    )(*args, e_f, e_b, d_e, states_b)


ML_GATE = 32
ML_ND = 2 * ML_HEADS
ML_AUG = ML_V_DIM + LANES


def _ml_gates(sm_ref, ib_ref, fb_ref, n, both):
    lane = lax.broadcasted_iota(jnp.int32, (1, LANES), 1)
    valid = (lane >= ML_GATE) & (lane < ML_GATE + ML_ND)
    is_f = lane < ML_GATE + ML_HEADS
    sm = sm_ref[0]
    li = sm + ib_ref[...]
    lf = pltpu.roll(_log_sigmoid(sm + fb_ref[...]), LANES - ML_ND, axis=1)
    parts = _split(jnp.where(valid, lf, 0.0), 3)
    upp = jnp.where(_causal(n, True), 1.0, 0.0).astype(BF16)
    bcum = sum(_dot(upp, p) for p in parts)
    if both:
        low = jnp.where(_causal(n, False), 1.0, 0.0).astype(BF16)
        bcum = jnp.where(is_f, sum(_dot(low, p) for p in parts), bcum)
    return valid, is_f, bcum, jnp.where(valid, li - bcum, 0.0)


def _ml_state_step(cn_ref, h, cn, keep, k, ws_dense, v_aug):
    w3 = jnp.concatenate([ws_dense.astype(BF16)] * (ML_AUG // LANES), axis=1)
    cn_ref[h] = keep * cn + _dot_tn(k, w3 * v_aug)


def _ml_v_aug(v_ref, h, n):
    return jnp.concatenate([v_ref[0, :, ML_V_DIM * h:ML_V_DIM * (h + 1)], jnp.ones((n, LANES), BF16)], axis=1)


def _ml_states_kernel(qk_ref, v_ref, sm_ref, ib_ref, fb_ref, sel_ref, cn_out, m_out, cn_ref, m_ref):
    n = qk_ref.shape[1]

    @pl.when(pl.program_id(1) == 0)
    def _init():
        cn_ref[...] = jnp.zeros_like(cn_ref)
        m_ref[...] = jnp.zeros_like(m_ref)

    valid, is_f, bcum, a = _ml_gates(sm_ref, ib_ref, fb_ref, n, False)
    cn_out[0, 0] = cn_ref[...].astype(cn_out.dtype)
    m_out[0, 0] = m_ref[...]
    m_prev = m_ref[0:1, :]
    g_last = jnp.maximum(m_prev, jnp.max(a, axis=0, keepdims=True))
    ws = _expand(jnp.exp(a - g_last), sel_ref[:, ML_HEADS * LANES:])
    keep = jnp.exp(m_prev - g_last)
    for h in range(ML_HEADS):
        lane_b = ML_GATE + ML_HEADS + h
        k = qk_ref[0, :, ML_QK_WIDTH + ML_QK_DIM * h:ML_QK_WIDTH + ML_QK_DIM * (h + 1)]
        _ml_state_step(cn_ref, h, cn_ref[h], keep[:, lane_b:lane_b + 1], k,
                       ws[:, LANES * h:LANES * (h + 1)], _ml_v_aug(v_ref, h, n))
    m_ref[...] = jnp.broadcast_to(bcum[0:1, :] + g_last, m_ref.shape)


def _ml_out_kernel(qk_ref, v_ref, sm_ref, ib_ref, fb_ref, sel_ref, cnb_ref, mb_ref, o_ref, cn_ref, m_ref):
    n = qk_ref.shape[1]

    @pl.when(pl.program_id(1) == 0)
    def _init():
        cn_ref[...] = jnp.zeros_like(cn_ref)
        m_ref[...] = jnp.zeros_like(m_ref)

    head_q = lambda h: qk_ref[0, :, ML_QK_DIM * h:ML_QK_DIM * (h + 1)]
    head_k = lambda h: qk_ref[0, :, ML_QK_WIDTH + ML_QK_DIM * h:ML_QK_WIDTH + ML_QK_DIM * (h + 1)]
    qk_all = [_dot_nt(head_q(h), head_k(h)) for h in range(ML_HEADS)]
    inter_all = [_dot(head_q(h), jnp.concatenate([cn_ref[h].astype(BF16), cnb_ref[0, 0, h]], axis=1))
                 for h in range(ML_HEADS)]

    valid, is_f, bcum, a = _ml_gates(sm_ref, ib_ref, fb_ref, n, True)
    m_prev = jnp.where(is_f, m_ref[0:1, :], mb_ref[0, 0, 0:1, :])
    a_t = a.T
    pre = suf = a_t[ML_GATE:ML_GATE + ML_ND, :]
    pos = lax.broadcasted_iota(jnp.int32, (ML_ND, n), 1)
    k = 1
    while k < n:
        pre = jnp.maximum(pre, jnp.where(pos >= k, pltpu.roll(pre, k, axis=1), -jnp.inf))
        suf = jnp.maximum(suf, jnp.where(pos < n - k, pltpu.roll(suf, n - k, axis=1), -jnp.inf))
        k *= 2
    run = jnp.where(lax.broadcasted_iota(jnp.int32, (ML_ND, n), 0) < ML_HEADS, pre, suf)
    run = jnp.concatenate([jnp.zeros((ML_GATE, n), F32), run,
                           jnp.zeros((LANES - ML_GATE - ML_ND, n), F32)], axis=0).T
    g = jnp.maximum(m_prev, run)
    m_t = bcum + g
    floor = jnp.exp(-m_t)
    dense = lambda x, lane_: jnp.broadcast_to(x[:, lane_:lane_ + 1], (n, LANES))
    diag_t = jnp.exp(a - g).T
    g_last = g[n - 1:n, :]
    ws = _expand(jnp.exp(a - g_last), sel_ref[:, :ML_HEADS * LANES])
    keep = jnp.exp(m_prev - g_last)
    ti = lax.broadcasted_iota(jnp.int32, (n, n), 0)
    si = lax.broadcasted_iota(jnp.int32, (n, n), 1)
    not_above, above, on_diag = si <= ti, si > ti, si == ti
    wide = lambda x: jnp.concatenate([x] * (n // LANES), axis=1)
    for h in range(ML_HEADS):
        lanes = (ML_GATE + h, ML_GATE + ML_HEADS + h)
        g_d = [dense(g, ln) for ln in lanes]
        qk, inter = qk_all[h], inter_all[h]
        e = jnp.where(not_above, a_t[lanes[0]:lanes[0] + 1, :] - wide(g_d[0]),
                      a_t[lanes[1]:lanes[1] + 1, :] - wide(g_d[1]))
        p = qk * jnp.exp(e)
        p_f = jnp.where(not_above, p, 0.0).astype(BF16)
        p_b = jnp.where(above, p, jnp.where(on_diag, qk * diag_t[lanes[1]:lanes[1] + 1, :], 0.0)).astype(BF16)
        intra = _dot(jnp.concatenate([p_f, p_b], axis=0), _ml_v_aug(v_ref, h, n))
        out = None
        for d in range(2):
            w_inter = jnp.exp(m_prev[:, lanes[d]:lanes[d] + 1] - g_d[d])
            s = (intra[n * d:n * (d + 1)] + jnp.concatenate([w_inter] * (ML_AUG // LANES), axis=1)
                 * inter[:, ML_AUG * d:ML_AUG * (d + 1)])
            rn = 1.0 / jnp.maximum(jnp.abs(s[:, ML_V_DIM:]), dense(floor, lanes[d]))
            hid = s[:, :ML_V_DIM] * jnp.concatenate([rn] * (ML_V_DIM // LANES), axis=1)
            out = hid if out is None else out + hid
        o_ref[0, :, ML_V_DIM * h:ML_V_DIM * (h + 1)] = out
    for h in range(ML_HEADS):
        lane_f = ML_GATE + h
        _ml_state_step(cn_ref, h, cn_ref[h], keep[:, lane_f:lane_f + 1], head_k(h),
                       ws[:, LANES * h:LANES * (h + 1)], _ml_v_aug(v_ref, h, n))
    m_ref[...] = jnp.broadcast_to(m_t[n - 1:n, :], m_ref.shape)


def _mlstm(cvm, p3, sm3, i_bias, f_bias, n_ctx):
    nb, t, _ = cvm.shape
    nt = t // TILE
    assert _SM_OFF["i_f"] == ML_GATE and _SM_OFF["f_f"] == ML_GATE + ML_ND
    row = lambda v, off: jnp.pad(v.reshape(1, -1), ((0, 0), (off, LANES - off - ML_ND)))
    sel = np.zeros((LANES, ML_ND * LANES), np.float32)
    for r in range(ML_ND):
        sel[ML_GATE + r, r * LANES:(r + 1) * LANES] = 1.0
    const = lambda b, i: (0, 0)
    specs = lambda order: [
        pl.BlockSpec((1, TILE, 2 * ML_QK_WIDTH), lambda b, i: (b, order(i), 0)),
        pl.BlockSpec((1, TILE, ML_V_WIDTH), lambda b, i: (b, order(i), _P_OFF["m_v"] // ML_V_WIDTH)),
        pl.BlockSpec((1, TILE, LANES), lambda b, i: (b, order(i), 0)),
        pl.BlockSpec((1, LANES), const),
        pl.BlockSpec((1, LANES), const),
        pl.BlockSpec((LANES, ML_ND * LANES), const)]
    args = (cvm, p3, sm3, row(i_bias, ML_GATE), row(f_bias, ML_GATE + ML_ND), jnp.asarray(sel, BF16))
    cn_block = (1, 1, ML_HEADS, ML_QK_DIM, ML_AUG)
    m_block = (1, 1, SUBLANES, LANES)
    scratch = [pltpu.VMEM(cn_block[2:], F32), pltpu.VMEM(m_block[2:], F32)]
    bwd = _tile_order(nt, True, n_ctx // TILE)
    st_idx = lambda b, i: (b, bwd(i)) + (0,) * 3
    cn_b, m_b = pl.pallas_call(
        _ml_states_kernel,
        out_shape=(jax.ShapeDtypeStruct((nb, nt) + cn_block[2:], BF16),
                   jax.ShapeDtypeStruct((nb, nt) + m_block[2:], F32)),
        grid=(nb, nt),
        in_specs=specs(bwd),
        out_specs=(pl.BlockSpec(cn_block, st_idx), pl.BlockSpec(m_block, lambda b, i: (b, bwd(i), 0, 0))),
        scratch_shapes=scratch,
        compiler_params=_cparams(("parallel", "arbitrary")),
        name="mlstm_states",
    )(*args)
    fwd = _tile_order(nt, False)
    return pl.pallas_call(
        _ml_out_kernel,
        out_shape=jax.ShapeDtypeStruct((nb, t, ML_V_WIDTH), F32),
        grid=(nb, nt),
        in_specs=specs(fwd) + [pl.BlockSpec(cn_block, lambda b, i: (b, i, 0, 0, 0)),
                               pl.BlockSpec(m_block, lambda b, i: (b, i, 0, 0))],
        out_specs=pl.BlockSpec((1, TILE, ML_V_WIDTH), lambda b, i: (b, i, 0)),
        scratch_shapes=scratch,
        compiler_params=_cparams(("parallel", "arbitrary")),
        name="mlstm_out",
    )(*args, cn_b, m_b)


GLA_SUB = 256
GLA_NCH = GLA_SUB // GLA_CHUNK
GLA_COLS = BF16_SUBLANES


def _gla_layout(is_ctx, lat_rows):
    r = np.arange(GLA_SUB)
    if is_ctx:
        return r // GLA_CHUNK, r % GLA_CHUNK
    col = r % SUBLANES
    cpc = GLA_CHUNK // lat_rows
    return col // cpc, (col % cpc) * lat_rows + r // SUBLANES


def _gla_consts(is_ctx, lat_rows, rev):
    ch, pos = _gla_layout(is_ctx, lat_rows)
    same = ch[:, None] == ch[None, :]
    before = (pos[None, :] >= pos[:, None]) if rev else (pos[None, :] <= pos[:, None])
    tri = (same & before).astype(np.float32)
    cmask = np.stack([np.repeat((ch == j)[:, None], LANES, axis=1) for j in range(GLA_NCH)])
    return tri, cmask.astype(np.float32)


def _gla_row(is_ctx, lat_rows, j, p):
    ch, pos = _gla_layout(is_ctx, lat_rows)
    return int(np.nonzero((ch == j) & (pos == p))[0][0])


def _per_chunk_rows(b, is_ctx, lat_rows, p):
    rows = [b[_gla_row(is_ctx, lat_rows, j, p):_gla_row(is_ctx, lat_rows, j, p) + 1, :]
            for j in range(GLA_NCH)]
    w = b.shape[1]
    if is_ctx:
        full = jnp.concatenate([jnp.broadcast_to(r, (GLA_CHUNK, w)) for r in rows], axis=0)
    else:
        rep = SUBLANES // GLA_NCH
        pat = jnp.concatenate([jnp.broadcast_to(r, (rep, w)) for r in rows], axis=0)
        full = jnp.broadcast_to(pat[None], (GLA_SUB // SUBLANES, SUBLANES, w)).reshape(GLA_SUB, w)
    return rows, full


def _gla_sub(q, k, v, araw, aup, abias, tri_b, tri_f, cmask_ref, st_ref, *, rev, is_ctx, lat_rows,
             want_out=True):
    g = _log_sigmoid(_dot_hp(araw, aup) + abias) * (1.0 / GLA_TAU)
    b = _dot_exact_lhs(tri_b, g, 2)
    lasts, last = _per_chunk_rows(b, is_ctx, lat_rows, 0 if rev else GLA_CHUNK - 1)
    kl = (k * jnp.exp(last - b)).astype(BF16)
    if want_out:
        _, ref = _per_chunk_rows(b, is_ctx, lat_rows, GLA_CHUNK // 2)
        qs = q * (GLA_K_DIM ** -0.5)
        qe = (qs * jnp.exp(b - ref)).astype(BF16)
        ke = (k * jnp.exp(ref - b)).astype(BF16)
        qb = (qs * jnp.exp(b)).astype(BF16)
        visible = tri_f > 0.0
    order = range(GLA_NCH - 1, -1, -1) if rev else range(GLA_NCH)
    outs = []
    for h in range(GLA_HEADS):
        ks = slice(GLA_K_DIM * h, GLA_K_DIM * (h + 1))
        vh = v[:, GLA_V_DIM * h:GLA_V_DIM * (h + 1)]
        klm = jnp.concatenate([kl[:, ks] * cmask_ref[j] for j in range(GLA_NCH)], axis=1)
        upd = _dot_tn(vh, klm)
        s = st_ref[h]
        s_in = [None] * GLA_NCH
        for j in order:
            s_in[j] = s.astype(BF16)
            s = s * jnp.exp(lasts[j][:, ks]) + upd[:, GLA_K_DIM * j:GLA_K_DIM * (j + 1)]
        st_ref[h] = s
        if want_out:
            att = jnp.where(visible, _dot_nt(qe[:, ks], ke[:, ks]), 0.0).astype(BF16)
            qbm = jnp.concatenate([qb[:, ks] * cmask_ref[j] for j in range(GLA_NCH)], axis=1)
            outs.append(_dot(att, vh) + _dot_nt(qbm, jnp.concatenate(s_in, axis=1)))
    return jnp.concatenate(outs, axis=1) if want_out else None


def _gla_kernel(qc_ref, kc_ref, vc_ref, sc_ref, ql_ref, kl_ref, vl_ref, sl_ref,
                aup_ref, ab_ref, tcb_ref, tcf_ref, cmc_ref, tlb_ref, tlf_ref, cml_ref,
                *rest, rev, ctx_rows, lat_rows, n_cblk, ctx_out):
    acc_ref, o_ref, st_ref, ctxo_ref = rest if len(rest) == 4 else (None,) + rest
    i = pl.program_id(1)
    aup = aup_ref[...]
    abias = ab_ref[...]
    r0, r1 = ctx_rows, ctx_rows + lat_rows
    sub = functools.partial(_gla_sub, aup=aup, abias=abias, st_ref=st_ref, rev=rev, lat_rows=lat_rows)

    @pl.when(i == 0)
    def _ctx():
        st_ref[...] = jnp.zeros_like(st_ref)
        o = sub(qc_ref[0].astype(F32), kc_ref[0].astype(F32), vc_ref[0], sc_ref[0], tri_b=tcb_ref[...],
                tri_f=tcf_ref[...], cmask_ref=cmc_ref, is_ctx=True, want_out=ctx_out)
        if ctx_out:
            ctxo_ref[...] = o

    @pl.when(i > 0)
    def _lat():
        cblk = (n_cblk - i) if rev else (i - 1)
        halves = range(GLA_COLS // SUBLANES)
        for half in (reversed(halves) if rev else halves):
            cs = slice(SUBLANES * half, SUBLANES * (half + 1))
            take = lambda r: r[0, r0:r1].astype(F32)[:, cs, :].reshape(GLA_SUB, r.shape[-1])
            o = sub(take(ql_ref), take(kl_ref), take(vl_ref).astype(BF16), take(sl_ref),
                    tri_b=tlb_ref[...], tri_f=tlf_ref[...], cmask_ref=cml_ref, is_ctx=False)
            _write(o_ref, (0, slice(r0, r1), cs, slice(None)),
                   o.reshape(lat_rows, SUBLANES, GLA_V_WIDTH), acc_ref)
        for r in range(ctx_rows):
            if ctx_out:
                start = pl.multiple_of(r * GRID_W + cblk * GLA_COLS, GLA_COLS)
                _write(o_ref, (0, r, slice(None), slice(None)), ctxo_ref[pl.ds(start, GLA_COLS), :], acc_ref)
            else:
                o_ref[0, r, :, :] = jnp.zeros((GLA_COLS, GLA_V_WIDTH), F32)


def _gla_scan(p3, sm3, a_up, a_bias, n_ctx, rev, acc, ctx_out):
    nb, t, ncol = p3.shape
    rows = t // GRID_W
    ctx_rows = n_ctx // GRID_W
    lat_rows = rows - ctx_rows
    n_cblk = GRID_W // GLA_COLS
    p4 = p3.reshape(nb, rows, GRID_W, ncol)
    sm4 = sm3.reshape(nb, rows, GRID_W, LANES)
    a_off = _SM_OFF["a_b"] if rev else _SM_OFF["a_f"]
    aup = jnp.pad(a_up, ((a_off, LANES - a_off - GLA_RANK), (0, 0)))
    cblk = lambda i: jnp.where(i == 0, n_cblk - 1 if rev else 0, (n_cblk - i) if rev else (i - 1))
    ctx = lambda blk: (lambda b, i: (b, 0, blk))
    lat = lambda blk: (lambda b, i: (b, 0, cblk(i), blk))
    const2 = lambda b, i: (0, 0)
    const3 = lambda b, i: (0, 0, 0)
    widths = (GLA_K_WIDTH, GLA_K_WIDTH, GLA_V_WIDTH)
    offs = (_P_OFF["g_q"], _P_OFF["g_k"], _P_OFF["g_v"])
    consts = []
    const_specs = []
    for is_ctx in (True, False):
        tri, cmask = _gla_consts(is_ctx, lat_rows, rev)
        consts += [jnp.asarray(tri, BF16), jnp.asarray(tri, F32), jnp.asarray(cmask, BF16)]
        const_specs += [pl.BlockSpec((GLA_SUB, GLA_SUB), const2), pl.BlockSpec((GLA_SUB, GLA_SUB), const2),
                        pl.BlockSpec((GLA_NCH, GLA_SUB, LANES), const3)]
    out_spec = pl.BlockSpec((1, rows, GLA_COLS, GLA_V_WIDTH), lat(0))
    in_specs = ([pl.BlockSpec((1, n_ctx, w), ctx(o // w)) for w, o in zip(widths, offs)]
                + [pl.BlockSpec((1, n_ctx, LANES), ctx(0))]
                + [pl.BlockSpec((1, rows, GLA_COLS, w), lat(o // w)) for w, o in zip(widths, offs)]
                + [pl.BlockSpec((1, rows, GLA_COLS, LANES), lat(0))]
                + [pl.BlockSpec((LANES, GLA_K_WIDTH), const2), pl.BlockSpec((1, GLA_K_WIDTH), const2)]
                + const_specs)
    args = [p3, p3, p3, sm3, p4, p4, p4, sm4, aup, a_bias.reshape(1, GLA_K_WIDTH)] + consts
    if acc is not None:
        in_specs.append(out_spec)
        args.append(acc.reshape(nb, rows, GRID_W, GLA_V_WIDTH))
    out = pl.pallas_call(
        functools.partial(_gla_kernel, rev=rev, ctx_rows=ctx_rows, lat_rows=lat_rows, n_cblk=n_cblk,
                          ctx_out=ctx_out),
        out_shape=jax.ShapeDtypeStruct((nb, rows, GRID_W, GLA_V_WIDTH), F32),
        grid=(nb, n_cblk + 1),
        in_specs=in_specs,
        out_specs=out_spec,
        scratch_shapes=[pltpu.VMEM((GLA_HEADS, GLA_V_DIM, GLA_K_DIM), F32),
                        pltpu.VMEM((n_ctx, GLA_V_WIDTH), F32)],
        input_output_aliases={} if acc is None else {len(args) - 1: 0},
        compiler_params=_cparams(("parallel", "arbitrary")),
        name="gla_scan",
    )(*args)
    return out.reshape(nb, t, GLA_V_WIDTH)


def _group_rmsnorm(y, groups):
    width = y.shape[-1] // groups
    ones = jnp.ones((width, LANES), BF16)
    out = []
    for g in range(groups):
        yg = y[:, width * g:width * (g + 1)]
        ms = _dot((yg * yg).astype(BF16), ones) * (1.0 / width)
        out.append(yg * jnp.concatenate([lax.rsqrt(ms + EPS)] * (width // LANES), axis=1))
    return jnp.concatenate(out, axis=1)


def _post_kernel(x_ref, y_ref, h_ref, o_ref_in, z_ref, mo_ref, gg_ref, gs_ref, gm_ref, gl_ref,
                 nws_ref, nwm_ref, nwg_ref, wbs_ref, wbm_ref, wbg_ref, wout_ref, g1_ref,
                 nwf_ref, sc2_ref, sh2_ref, xo_ref, ho_ref):
    y_ssd = (_group_rmsnorm(y_ref[0] * _silu(z_ref[0]), 2) * nws_ref[...]).astype(BF16)
    y_ml = (_group_rmsnorm(h_ref[0], ML_HEADS) * nwm_ref[...] * _sigmoid(mo_ref[0])).astype(BF16)
    y_gla = (_group_rmsnorm(o_ref_in[0], GLA_HEADS) * nwg_ref[...] * _silu(gg_ref[0])).astype(BF16)
    merged = (_sigmoid(gs_ref[0]) * _dot(y_ssd, wbs_ref[...])
              + _sigmoid(gm_ref[0]) * _dot(y_ml, wbm_ref[...])
              + _sigmoid(gl_ref[0]) * _dot(y_gla, wbg_ref[...]))
    x_new = x_ref[0] + g1_ref[0] * _dot(merged.astype(BF16), wout_ref[...])
    xo_ref[0] = x_new
    ho_ref[0] = (_rms(x_new) * nwf_ref[...] * (1.0 + sc2_ref[0]) + sh2_ref[0]).astype(ho_ref.dtype)


def _post(x, scans, p3, norm_ws, w_bs, w_out, layer, norm_ffn_w, mods, n_ctx, tile0):
    nb, t, d = x.shape
    nt = t // TILE - tile0
    ctx_tiles = n_ctx // TILE
    tok = lambda blk: (lambda b, i: (b, i + tile0, blk))
    out = lambda b, i: (b, i, 0)
    const = lambda b, i: (0, 0)
    tok_spec = lambda blk: pl.BlockSpec((1, TILE, d), tok(blk))
    w_spec = pl.BlockSpec((None, d, d), lambda b, i: (layer, 0, 0), pipeline_mode=pl.Buffered(1))
    vec = pl.BlockSpec((1, d), const)
    mod = lambda k: pl.BlockSpec((1, 1, d), _mod_row(nb, k, ctx_tiles, tile0))
    names = ("s_z", "m_o", "g_g", "gate_ssd", "gate_ml", "gate_gla")
    in_specs = ([tok_spec(0)] * 4 + [tok_spec(_P_OFF[nm] // d) for nm in names]
                + [vec] * 3 + [w_spec] * 4 + [mod(2), vec, mod(4), mod(3)])
    return pl.pallas_call(
        _post_kernel,
        out_shape=(jax.ShapeDtypeStruct((nb, nt * TILE, d), F32),
                   jax.ShapeDtypeStruct((nb, nt * TILE, d), BF16)),
        grid=(nb, nt),
        in_specs=in_specs,
        out_specs=(pl.BlockSpec((1, TILE, d), out), pl.BlockSpec((1, TILE, d), out)),
        compiler_params=_cparams(("parallel", "parallel")),
        name="post",
    )(x, *scans, *([p3] * 6), *[w.reshape(1, d) for w in norm_ws], *w_bs, w_out, mods,
      norm_ffn_w.reshape(1, d), mods, mods)


def _ffn_in_kernel(a_ref, w_ref, o_ref):
    acc = _dot(a_ref[...], w_ref[...])
    half = acc.shape[1] // 2
    o_ref[...] = (_silu(acc[:, :half]) * acc[:, half:]).astype(o_ref.dtype)


def _ffn_in(h, w_gu, layer, half):
    m, k = h.shape
    n_half = w_gu.shape[-1] // 2
    tm = _row_tile(m)
    return pl.pallas_call(
        _ffn_in_kernel,
        out_shape=jax.ShapeDtypeStruct((m, n_half), BF16),
        grid=(n_half // half, m // tm),
        in_specs=[pl.BlockSpec((tm, k), lambda j, i: (i, 0)),
                  pl.BlockSpec((None, k, 2 * half), lambda j, i: (layer, 0, j))],
        out_specs=pl.BlockSpec((tm, half), lambda j, i: (i, j)),
        compiler_params=_cparams(("parallel", "parallel")),
        name="ffn_in",
    )(h, w_gu)


def _ffn_out_kernel(a_ref, w_ref, x_ref, g_ref, nw_ref, sc_ref, sh_ref, xo_ref, ho_ref):
    x_new = x_ref[0] + g_ref[0] * _dot(a_ref[0], w_ref[...])
    xo_ref[0] = x_new
    ho_ref[0] = (_rms(x_new) * nw_ref[...] * (1.0 + sc_ref[0]) + sh_ref[0]).astype(ho_ref.dtype)


def _ffn_out_last_kernel(a_ref, w_ref, x_ref, g_ref, nw_ref, o_ref):
    x_new = x_ref[0] + g_ref[0] * _dot(a_ref[0], w_ref[...])
    o_ref[0] = _rms(x_new) * nw_ref[...]


def _ffn_out(a, w, layer, x, mods, n_ctx, tile0, next_norm_w, next_mods):
    nb, t, d = x.shape
    k = a.shape[-1]
    ctx_tiles = n_ctx // TILE
    tok = lambda b, i: (b, i, 0)
    mod = lambda k_: pl.BlockSpec((1, 1, d), _mod_row(nb, k_, ctx_tiles, tile0))
    in_specs = [pl.BlockSpec((1, TILE, k), tok),
                pl.BlockSpec((None, k, d), lambda b, i: (layer, 0, 0), pipeline_mode=pl.Buffered(1)),
                pl.BlockSpec((1, TILE, d), tok),
                mod(5),
                pl.BlockSpec((1, d), lambda b, i: (0, 0))]
    args = [a, w, x, mods, next_norm_w.reshape(1, d)]
    if next_mods is None:
        body = _ffn_out_last_kernel
        out_shape = jax.ShapeDtypeStruct((nb, t, d), F32)
        out_specs = pl.BlockSpec((1, TILE, d), tok)
    else:
        body = _ffn_out_kernel
        in_specs += [mod(1), mod(0)]
        args += [next_mods, next_mods]
        out_shape = (jax.ShapeDtypeStruct((nb, t, d), F32), jax.ShapeDtypeStruct((nb, t, d), BF16))
        out_specs = (pl.BlockSpec((1, TILE, d), tok), pl.BlockSpec((1, TILE, d), tok))
    return pl.pallas_call(
        body,
        out_shape=out_shape,
        grid=(nb, t // TILE),
        in_specs=in_specs,
        out_specs=out_specs,
        compiler_params=_cparams(("parallel", "parallel")),
        name="ffn_out",
    )(*args)


def _proj_weights(w_in):
    cols = lambda names: [w_in[..., _IN_OFF[nm]:_IN_OFF[nm] + _IN_W[nm]] for nm in names]
    main = jnp.concatenate(cols(_P_ORDER), axis=-1)
    small = jnp.pad(jnp.concatenate(cols(_SMALL), axis=-1), ((0, 0), (0, 0), (0, LANES - N_SMALL_USED)))
    return main.astype(BF16), small.astype(BF16)


def _ffn_weight(w_ffn_in, half):
    d_ff = w_ffn_in.shape[-1] // 2
    cols = []
    for j in range(d_ff // half):
        cols.append(w_ffn_in[..., j * half:(j + 1) * half])
        cols.append(w_ffn_in[..., d_ff + j * half:d_ff + (j + 1) * half])
    return jnp.concatenate(cols, axis=-1).astype(BF16)


def kernel(x, c, ctx, c_ctx, w_mod, b_mod, norm_mix_w, norm_ffn_w, w_in, ssd_conv_w, ssd_conv_b, ssd_dt_bias, ssd_a_log, ssd_d, ssd_norm_w, ml_conv_w, ml_conv_b, ml_i_bias, ml_f_bias, ml_norm_w, gla_a_up, gla_a_bias, gla_norm_w, w_b_ssd, w_b_ml, w_b_gla, w_out, w_ffn_in, w_ffn_out, final_norm_w):
    nb, n_lat, d = x.shape
    n_ctx = ctx.shape[1]
    t = n_ctx + n_lat
    depth = w_in.shape[0]
    d_ff = w_ffn_out.shape[1]
    assert n_ctx == TILE == GLA_SUB and n_lat % TILE == 0 and n_lat // GRID_W == 32
    ffn_half = d_ff // 2

    xs = jnp.concatenate([ctx, x], axis=1)
    c16 = jnp.pad(jnp.concatenate([c, c_ctx[None]], axis=0), ((0, 2 * SUBLANES - nb - 1), (0, 0)))
    mods = [_modulation(c16, w_mod, b_mod, l).reshape(2 * SUBLANES * 6, 1, d) for l in range(depth)]
    h = _norm_mod(xs, norm_mix_w[0], mods[0], 0, 1, n_ctx)
    ml_post = jnp.concatenate([jnp.ones((ML_QK_WIDTH,), F32),
                               jnp.full((ML_QK_WIDTH,), ML_QK_DIM ** -0.5, F32)])
    w_main, w_small = _proj_weights(w_in)
    w_gu = _ffn_weight(w_ffn_in, ffn_half)
    w_bs = [w.astype(BF16) for w in (w_b_ssd, w_b_ml, w_b_gla)]
    w_out_b = w_out.astype(BF16)
    w_ffn_out_b = w_ffn_out.astype(BF16)
    for l in range(depth):
        last = l == depth - 1
        h2d = h.reshape(nb * t, d)
        p3 = _matmul(h2d, w_main, l, N_PROJ // 5, BF16).reshape(nb, t, N_PROJ)
        sm3 = _matmul(h2d, w_small, l, LANES, F32).reshape(nb, t, LANES)
        cvs = _conv(p3, _P_OFF["s_x"], ssd_conv_w[l], ssd_conv_b[l],
                    jnp.ones((ssd_conv_w.shape[-1],), F32), n_ctx)
        cvm = _conv(p3, _P_OFF["m_q"], ml_conv_w[l], ml_conv_b[l], ml_post, n_ctx)
        d_e = jnp.repeat(ssd_d[l], SSD_HEAD_DIM).reshape(1, SSD_WIDTH)
        y = _ssd(cvs, sm3, ssd_dt_bias[l], ssd_a_log[l], d_e, n_ctx)
        hm = _mlstm(cvm, p3, sm3, ml_i_bias[l], ml_f_bias[l], n_ctx)
        og = None
        for rev in (False, True):
            k = int(rev)
            og = _gla_scan(p3, sm3, gla_a_up[l, k], gla_a_bias[l, k], n_ctx, rev, og, not last)
        tile0 = n_ctx // TILE if last else 0
        xs, h2 = _post(xs, (y, hm, og), p3, (ssd_norm_w[l], ml_norm_w[l], gla_norm_w[l]),
                       w_bs, w_out_b, l, norm_ffn_w[l], mods[l], n_ctx, tile0)
        nt = xs.shape[1]
        a = _ffn_in(h2.reshape(nb * nt, d), w_gu, l, ffn_half)
        a = a.reshape(nb, nt, d_ff)
        if last:
            return _ffn_out(a, w_ffn_out_b, l, xs, mods[l], n_ctx, tile0, final_norm_w, None)
        xs, h = _ffn_out(a, w_ffn_out_b, l, xs, mods[l], n_ctx, tile0,
                         norm_mix_w[l + 1], mods[l + 1])
```

```python
import functools

import numpy as np
import jax
import jax.numpy as jnp
from jax import lax
from jax.experimental import pallas as pl
from jax.experimental.pallas import tpu as pltpu

F32 = jnp.float32
BF16 = jnp.bfloat16

EPS = 1e-6
LOG2E = 1.4426950408889634
GRID_W = 64
SSD_HEADS = 16
SSD_HEAD_DIM = 64
SSD_WIDTH = 1024
SSD_STATE = 64
SSD_BC = 128
ML_HEADS = 4
ML_QK_DIM = 128
ML_V_DIM = 256
ML_QK_WIDTH = 512
ML_V_WIDTH = 1024
GLA_HEADS = 4
GLA_K_DIM = 128
GLA_V_DIM = 256
GLA_K_WIDTH = 512
GLA_V_WIDTH = 1024
GLA_RANK = 16
GLA_TAU = 16.0
GLA_CHUNK = 64

LANES = 128
SUBLANES = 8
BF16_SUBLANES = 16
VMEM_LIMIT = 56 * 1024 * 1024

TILE = 256

_IN_NAMES = ("s_x", "s_z", "s_b", "s_c", "dt_f", "dt_b",
             "m_q", "m_k", "m_v", "m_o", "i_f", "i_b", "f_f", "f_b",
             "g_q", "g_k", "g_v", "g_g", "a_f", "a_b",
             "gate_ssd", "gate_ml", "gate_gla")
_IN_WIDTHS = (1024, 1024, 128, 128, 16, 16,
              512, 512, 1024, 1024, 4, 4, 4, 4,
              512, 512, 1024, 1024, 16, 16,
              1024, 1024, 1024)
_IN_OFF = dict(zip(_IN_NAMES, np.concatenate([[0], np.cumsum(_IN_WIDTHS)[:-1]]).tolist()))
_IN_W = dict(zip(_IN_NAMES, _IN_WIDTHS))

_P_ORDER = ("s_z", "m_o", "g_g", "gate_ssd", "gate_ml", "gate_gla", "m_v", "g_v",
            "g_q", "g_k", "s_x", "s_b", "s_c", "m_q", "m_k")
_P_OFF = {}
_o = 0
for _n in _P_ORDER:
    _P_OFF[_n] = _o
    _o += _IN_W[_n]
N_PROJ = _o
_SMALL = ("dt_f", "dt_b", "i_f", "i_b", "f_f", "f_b", "a_f", "a_b")
_SM_OFF = {}
_s = 0
for _n in _SMALL:
    _SM_OFF[_n] = _s
    _s += _IN_W[_n]
N_SMALL_USED = _s


def _cparams(sem):
    return pltpu.CompilerParams(dimension_semantics=sem, vmem_limit_bytes=VMEM_LIMIT)


def _sigmoid(x):
    return 0.5 * jnp.tanh(0.5 * x) + 0.5


def _silu(x):
    return x * _sigmoid(x)


def _softplus(x):
    return jnp.maximum(x, 0.0) + jnp.log1p(jnp.exp(-jnp.abs(x)))


def _log_sigmoid(x):
    return jnp.minimum(x, 0.0) - jnp.log(1.0 + jnp.exp(-jnp.abs(x)))


def _split(x, n):
    out = []
    r = x
    for _ in range(n):
        p = r.astype(BF16)
        out.append(p)
        r = r - p.astype(F32)
    return out


def _dot(a, b):
    return jnp.dot(a, b, preferred_element_type=F32)


def _dot_nt(a, b):
    return lax.dot_general(a, b, (((1,), (1,)), ((), ())), preferred_element_type=F32)


def _dot_tn(a, b):
    return lax.dot_general(a, b, (((0,), (0,)), ((), ())), preferred_element_type=F32)


def _dot_exact_lhs(t, x, pieces):
    return sum(_dot(t, p) for p in _split(x, pieces))


def _dot_hp(a, b):
    ah, am = _split(a, 2)
    bh, bm = _split(b, 2)
    return _dot(ah, bh) + _dot(ah, bm) + _dot(am, bh)


def _causal(n, rev):
    t = lax.broadcasted_iota(jnp.int32, (n, n), 0)
    s = lax.broadcasted_iota(jnp.int32, (n, n), 1)
    return (s >= t) if rev else (s <= t)


def _rms(x):
    return x * lax.rsqrt(jnp.mean(x * x, axis=-1, keepdims=True) + EPS)


def _mod_kernel(c_ref, w_ref, b_ref, o_ref):
    o_ref[...] = _dot_hp(_silu(c_ref[...]), w_ref[...]) + b_ref[...]


def _modulation(c16, w_mod, b_mod, layer):
    rows, d = c16.shape
    n = w_mod.shape[-1]
    tn = 1536
    return pl.pallas_call(
        _mod_kernel,
        out_shape=jax.ShapeDtypeStruct((rows, n), F32),
        grid=(n // tn,),
        in_specs=[pl.BlockSpec((rows, d), lambda j: (0, 0)),
                  pl.BlockSpec((None, d, tn), lambda j: (layer, 0, j)),
                  pl.BlockSpec((None, 1, tn), lambda j: (layer, 0, j))],
        out_specs=pl.BlockSpec((rows, tn), lambda j: (0, j)),
        compiler_params=_cparams(("arbitrary",)),
        name="modulation",
    )(c16, w_mod, b_mod.reshape(b_mod.shape[0], 1, n))


def _mod_row(nb, k, ctx_tiles, tile0=0):
    return lambda b, i: (jnp.where(i + tile0 < ctx_tiles, nb, b) * 6 + k, 0, 0)


def _norm_mod_kernel(x_ref, w_ref, sc_ref, sh_ref, o_ref):
    o_ref[0] = (_rms(x_ref[0]) * w_ref[...] * (1.0 + sc_ref[0]) + sh_ref[0]).astype(o_ref.dtype)


def _norm_mod(x, w, mods, k_shift, k_scale, n_ctx):
    nb, t, d = x.shape
    return pl.pallas_call(
        _norm_mod_kernel,
        out_shape=jax.ShapeDtypeStruct((nb, t, d), BF16),
        grid=(nb, t // TILE),
        in_specs=[pl.BlockSpec((1, TILE, d), lambda b, i: (b, i, 0)),
                  pl.BlockSpec((1, d), lambda b, i: (0, 0)),
                  pl.BlockSpec((1, 1, d), _mod_row(nb, k_scale, n_ctx // TILE)),
                  pl.BlockSpec((1, 1, d), _mod_row(nb, k_shift, n_ctx // TILE))],
        out_specs=pl.BlockSpec((1, TILE, d), lambda b, i: (b, i, 0)),
        compiler_params=_cparams(("parallel", "parallel")),
        name="norm_mod",
    )(x, w.reshape(1, d), mods, mods)


def _mm_kernel(a_ref, w_ref, o_ref):
    o_ref[...] = _dot(a_ref[...], w_ref[...]).astype(o_ref.dtype)


def _row_tile(m):
    return 512 if m % 512 == 0 else TILE


def _matmul(a, w, layer, tn, out_dtype):
    m, k = a.shape
    n = w.shape[-1]
    tm = _row_tile(m)
    return pl.pallas_call(
        _mm_kernel,
        out_shape=jax.ShapeDtypeStruct((m, n), out_dtype),
        grid=(n // tn, m // tm),
        in_specs=[pl.BlockSpec((tm, k), lambda j, i: (i, 0)),
                  pl.BlockSpec((None, k, tn), lambda j, i: (layer, 0, j))],
        out_specs=pl.BlockSpec((tm, tn), lambda j, i: (i, j)),
        compiler_params=_cparams(("parallel", "parallel")),
        name="matmul",
    )(a, w)


CONV_K = 5
CONV_ROWS = 256


def _conv_kernel(u_ref, w_ref, b_ref, s_ref, o_ref, pad_ref, *, n_ctx):
    t, c = u_ref.shape[1], u_ref.shape[2]
    half = CONV_K // 2
    zeros = jnp.zeros((SUBLANES, c), F32)
    w = w_ref[...]
    bias = b_ref[...]
    post = s_ref[...]
    for s0, n in ((0, n_ctx), (n_ctx, t - n_ctx)):
        pad_ref[0:SUBLANES, :] = zeros
        pad_ref[SUBLANES:SUBLANES + n, :] = u_ref[0, s0:s0 + n, :].astype(F32)
        pad_ref[SUBLANES + n:2 * SUBLANES + n, :] = zeros
        for r0 in range(0, n, CONV_ROWS):
            acc = bias
            for j in range(CONV_K):
                lo = SUBLANES - half + j + r0
                acc = acc + w[j:j + 1, :] * pad_ref[lo:lo + CONV_ROWS, :]
            o_ref[0, s0 + r0:s0 + r0 + CONV_ROWS, :] = (_silu(acc) * post).astype(o_ref.dtype)


def _conv(p3, col0, w, b, post_scale, n_ctx):
    nb, t, _ = p3.shape
    width = w.shape[1]
    cb = 256
    return pl.pallas_call(
        functools.partial(_conv_kernel, n_ctx=n_ctx),
        out_shape=jax.ShapeDtypeStruct((nb, t, width), BF16),
        grid=(nb, width // cb),
        in_specs=[pl.BlockSpec((1, t, cb), lambda b_, j: (b_, 0, col0 // cb + j)),
                  pl.BlockSpec((CONV_K, cb), lambda b_, j: (0, j)),
                  pl.BlockSpec((1, cb), lambda b_, j: (0, j)),
                  pl.BlockSpec((1, cb), lambda b_, j: (0, j))],
        out_specs=pl.BlockSpec((1, t, cb), lambda b_, j: (b_, 0, j)),
        scratch_shapes=[pltpu.VMEM((t + 2 * SUBLANES, cb), F32)],
        compiler_params=_cparams(("parallel", "parallel")),
        name="conv",
    )(p3, w, b.reshape(1, width), post_scale.reshape(1, width))


def _tile_order(n_tiles, rev, ctx_tiles=1):
    if rev:
        return lambda i: jnp.where(i < ctx_tiles, ctx_tiles - 1 - i, n_tiles - 1 + ctx_tiles - i)
    return lambda i: i


def _write(o_ref, idx, val, acc_ref):
    o_ref[idx] = val if acc_ref is None else acc_ref[idx] + val


SSD_PAIRS = SSD_HEADS // 2
SSD_B_OFF = SSD_HEADS
SSD_TILE = 256


def _ssd_gates(sm_ref, dtb_ref, alog_ref, n, both):
    lane = lax.broadcasted_iota(jnp.int32, (1, LANES), 1)
    dt = _softplus(sm_ref[0] + dtb_ref[...])
    la = dt * jnp.where(lane < 2 * SSD_HEADS, -jnp.exp(alog_ref[...]), 0.0)
    parts = _split(la, 3)
    upp = jnp.where(_causal(n, True), 1.0, 0.0).astype(BF16)
    cum = sum(_dot(upp, p) for p in parts)
    if both:
        low = jnp.where(_causal(n, False), 1.0, 0.0).astype(BF16)
        cum = jnp.where(lane < SSD_HEADS, sum(_dot(low, p) for p in parts), cum)
    return lane, dt, cum


def _expand(a, e):
    return sum(_dot(p, e) for p in _split(a, 2))


def _group_dup(v, g, lo):
    other = pltpu.roll(v, SSD_STATE, axis=1)
    return jnp.where(lo, v, other) if g == 0 else jnp.where(lo, other, v)


def _ssd_state_step(st_ref, j, bw, xp, elast, off):
    r = lax.broadcasted_iota(jnp.int32, (LANES, LANES), 0) < SSD_STATE
    c = lax.broadcasted_iota(jnp.int32, (LANES, LANES), 1) < SSD_HEAD_DIM
    dec = jnp.where(r, elast[:, off + 2 * j:off + 2 * j + 1], elast[:, off + 2 * j + 1:off + 2 * j + 2])
    st_ref[j] = jnp.where(r == c, dec * st_ref[j] + _dot_tn(bw.astype(BF16), xp), 0.0)


def _interleave(*bodies):
    live = list(bodies)
    while live:
        for body in list(live):
            if next(body, StopIteration) is StopIteration:
                live.remove(body)


def _ssd_states_body(x_ref, bc_ref, sm_ref, dtb_ref, alog_ref, eb_ref, o_ref, st_ref):
    n = x_ref.shape[1]

    @pl.when(pl.program_id(1) == 0)
    def _init():
        st_ref[...] = jnp.zeros_like(st_ref)

    o_ref[0, 0] = st_ref[...].astype(o_ref.dtype)
    yield
    lane, dt, cum = _ssd_gates(sm_ref, dtb_ref, alog_ref, n, False)
    lo = lane < SSD_HEAD_DIM
    last = cum[0:1, :]
    yield
    wst = _expand(jnp.exp(last - cum) * dt, eb_ref[...])
    elast = jnp.exp(last)
    b128 = bc_ref[0, :, :SSD_BC].astype(F32)
    for j in range(SSD_PAIRS):
        sl = slice(LANES * j, LANES * (j + 1))
        bw = _group_dup(b128, j // (SSD_PAIRS // 2), lo) * wst[:, sl]
        _ssd_state_step(st_ref, j, bw, x_ref[0, :, sl], elast, SSD_B_OFF)
        if j % 2:
            yield


def _ssd_out_body(x_ref, bc_ref, sm_ref, dtb_ref, alog_ref, ef_ref, eb_ref, d_ref, sb_ref, o_ref, st_ref):
    n = x_ref.shape[1]

    @pl.when(pl.program_id(1) == 0)
    def _init():
        st_ref[...] = jnp.zeros_like(st_ref)

    zero_b = jnp.zeros((), BF16)
    lo = lax.broadcasted_iota(jnp.int32, (1, LANES), 1) < SSD_HEAD_DIM
    b128_b = bc_ref[0, :, :SSD_BC]
    c128_b = bc_ref[0, :, SSD_BC:]
    cb_all = [_dot_nt(jnp.where(lo if g == 0 else jnp.logical_not(lo), c128_b, zero_b), b128_b)
              for g in range(2)]
    pair_x = lambda j: x_ref[0, :, LANES * j:LANES * (j + 1)]
    rhs_all = [jnp.concatenate([jnp.where(lo, pair_x(j), zero_b), jnp.where(lo, zero_b, pair_x(j)),
                                st_ref[j].astype(BF16), sb_ref[0, 0, j]], axis=0) for j in range(SSD_PAIRS)]
    yield
    lane, dt, cum = _ssd_gates(sm_ref, dtb_ref, alog_ref, n, True)
    is_f = lane < SSD_HEADS
    ldt = jnp.log(dt)
    dsum = jnp.log(dt + pltpu.roll(dt, LANES - SSD_B_OFF, axis=1))
    rt = (jnp.where(lane < 2 * SSD_HEADS, cum - ldt, pltpu.roll(dsum, 2 * SSD_HEADS, axis=1)) * LOG2E).T
    cum2 = cum * LOG2E
    yield
    last = jnp.where(is_f, cum[n - 1:n, :], cum[0:1, :])
    elast = jnp.exp(last)
    ecum = _split(jnp.exp(cum), 2)
    ecum_f = sum(_dot(p, ef_ref[...]) for p in ecum)
    ecum_b = sum(_dot(p, eb_ref[...]) for p in ecum)
    wst = _expand(jnp.exp(last - cum) * dt, ef_ref[...])
    bc = bc_ref[0].astype(F32)
    b128, c128 = bc[:, :SSD_BC], bc[:, SSD_BC:]
    ti = lax.broadcasted_iota(jnp.int32, (n, n), 0)
    si = lax.broadcasted_iota(jnp.int32, (n, n), 1)
    below, above = si < ti, si > ti
    half = SSD_PAIRS // 2
    yield
    for g in range(2):
        cb = cb_all[g]
        cdup = _group_dup(c128, g, lo)
        for j in range(g * half, (g + 1) * half):
            ms = []
            for h in (2 * j, 2 * j + 1):
                e_f = cum2[:, h:h + 1] - rt[h:h + 1, :]
                e_b = cum2[:, SSD_B_OFF + h:SSD_B_OFF + h + 1] - rt[SSD_B_OFF + h:SSD_B_OFF + h + 1, :]
                e = jnp.where(below, e_f, jnp.where(above, e_b, rt[2 * SSD_HEADS + h:2 * SSD_HEADS + h + 1, :]))
                ms.append((cb * jnp.exp2(e)).astype(BF16))
            sl = slice(LANES * j, LANES * (j + 1))
            cs_f = (cdup * ecum_f[:, sl]).astype(BF16)
            cs_b = (cdup * ecum_b[:, sl]).astype(BF16)
            lhs = jnp.concatenate(ms + [cs_f, cs_b], axis=1)
            o_ref[0, :, sl] = _dot(lhs, rhs_all[j]) + d_ref[:, sl] * pair_x(j).astype(F32)
            yield
    for j in range(SSD_PAIRS):
        sl = slice(LANES * j, LANES * (j + 1))
        _ssd_state_step(st_ref, j, _group_dup(b128, j // half, lo) * wst[:, sl], x_ref[0, :, sl], elast, 0)
        if j % 2:
            yield


def _run_together(name, grid, *parts):
    n_in = [len(p[1]) for p in parts]
    n_out = [len(p[3]) for p in parts]
    n_scr = [len(p[5]) for p in parts]

    def kern(*refs):
        ins, outs, scr = refs[:sum(n_in)], refs[sum(n_in):sum(n_in) + sum(n_out)], refs[sum(n_in) + sum(n_out):]
        bodies = []
        for k, p in enumerate(parts):
            take = lambda seq, counts: seq[sum(counts[:k]):sum(counts[:k + 1])]
            bodies.append(p[0](*take(ins, n_in), *take(outs, n_out), *take(scr, n_scr)))
        _interleave(*bodies)

    res = pl.pallas_call(
        kern,
        out_shape=tuple(s for p in parts for s in p[3]),
        grid=grid,
        in_specs=[s for p in parts for s in p[1]],
        out_specs=tuple(s for p in parts for s in p[4]),
        scratch_shapes=[s for p in parts for s in p[5]],
        compiler_params=_cparams(("parallel", "arbitrary")),
        name=name,
    )(*[a for p in parts for a in p[2]])
    return [list(res[sum(n_out[:k]):sum(n_out[:k + 1])]) for k in range(len(parts))]


def _ssd_parts(cvs, sm3, dt_bias, a_log, d_e, n_ctx):
    nb, t, _ = cvs.shape
    tile = SSD_TILE
    nt = t // tile
    row = lambda v: jnp.pad(v.reshape(1, -1), ((0, 0), (0, LANES - 2 * SSD_HEADS)))
    const = lambda b, i: (0, 0)
    specs = lambda order: [
        pl.BlockSpec((1, tile, SSD_WIDTH), lambda b, i: (b, order(i), 0)),
        pl.BlockSpec((1, tile, 2 * SSD_BC), lambda b, i: (b, order(i), SSD_WIDTH // (2 * SSD_BC))),
        pl.BlockSpec((1, tile, LANES), lambda b, i: (b, order(i), 0)),
        pl.BlockSpec((1, LANES), const),
        pl.BlockSpec((1, LANES), const)]
    st_block = (1, 1, SSD_PAIRS, LANES, LANES)
    args = (cvs, cvs, sm3, row(dt_bias), row(a_log))
    sel = np.zeros((2, LANES, SSD_WIDTH), np.float32)
    for h in range(SSD_HEADS):
        sel[0, h, h * SSD_HEAD_DIM:(h + 1) * SSD_HEAD_DIM] = 1.0
        sel[1, SSD_B_OFF + h, h * SSD_HEAD_DIM:(h + 1) * SSD_HEAD_DIM] = 1.0
    e_f, e_b = jnp.asarray(sel[0], BF16), jnp.asarray(sel[1], BF16)
    e_spec = pl.BlockSpec((LANES, SSD_WIDTH), const)
    bwd = _tile_order(nt, True, n_ctx // tile)
    fwd = _tile_order(nt, False)
    scratch = [pltpu.VMEM(st_block[2:], F32)]
    states = (_ssd_states_body, specs(bwd) + [e_spec], list(args) + [e_b],
              [jax.ShapeDtypeStruct((nb, nt) + st_block[2:], BF16)],
              [pl.BlockSpec(st_block, lambda b, i: (b, bwd(i), 0, 0, 0))], scratch)
    out = lambda states_b: (
        _ssd_out_body,
        specs(fwd) + [e_spec, e_spec, pl.BlockSpec((1, SSD_WIDTH), const),
                      pl.BlockSpec(st_block, lambda b, i: (b, i, 0, 0, 0))],
        list(args) + [e_f, e_b, d_e, states_b],
        [jax.ShapeDtypeStruct((nb, t, SSD_WIDTH), F32)],
        [pl.BlockSpec((1, tile, SSD_WIDTH), lambda b, i: (b, i, 0))], scratch)
    return states, out


ML_GATE = 32
ML_ND = 2 * ML_HEADS
ML_AUG = ML_V_DIM + LANES


def _ml_gates(sm_ref, ib_ref, fb_ref, n, both):
    lane = lax.broadcasted_iota(jnp.int32, (1, LANES), 1)
    valid = (lane >= ML_GATE) & (lane < ML_GATE + ML_ND)
    is_f = lane < ML_GATE + ML_HEADS
    sm = sm_ref[0]
    li = sm + ib_ref[...]
    lf = pltpu.roll(_log_sigmoid(sm + fb_ref[...]), LANES - ML_ND, axis=1)
    parts = _split(jnp.where(valid, lf, 0.0), 3)
    upp = jnp.where(_causal(n, True), 1.0, 0.0).astype(BF16)
    bcum = sum(_dot(upp, p) for p in parts)
    if both:
        low = jnp.where(_causal(n, False), 1.0, 0.0).astype(BF16)
        bcum = jnp.where(is_f, sum(_dot(low, p) for p in parts), bcum)
    return valid, is_f, bcum, jnp.where(valid, li - bcum, 0.0)


def _ml_state_step(cn_ref, h, cn, keep, k, ws_dense, v_aug):
    w3 = jnp.concatenate([ws_dense.astype(BF16)] * (ML_AUG // LANES), axis=1)
    cn_ref[h] = keep * cn + _dot_tn(k, w3 * v_aug)


def _ml_v_aug(v_ref, h, n):
    return jnp.concatenate([v_ref[0, :, ML_V_DIM * h:ML_V_DIM * (h + 1)], jnp.ones((n, LANES), BF16)], axis=1)


def _ml_states_body(qk_ref, v_ref, sm_ref, ib_ref, fb_ref, sel_ref, cn_out, m_out, cn_ref, m_ref):
    n = qk_ref.shape[1]

    @pl.when(pl.program_id(1) == 0)
    def _init():
        cn_ref[...] = jnp.zeros_like(cn_ref)
        m_ref[...] = jnp.zeros_like(m_ref)

    cn_out[0, 0] = cn_ref[...].astype(cn_out.dtype)
    m_out[0, 0] = m_ref[...]
    yield
    valid, is_f, bcum, a = _ml_gates(sm_ref, ib_ref, fb_ref, n, False)
    m_prev = m_ref[0:1, :]
    g_last = jnp.maximum(m_prev, jnp.max(a, axis=0, keepdims=True))
    yield
    ws = _expand(jnp.exp(a - g_last), sel_ref[:, ML_HEADS * LANES:])
    keep = jnp.exp(m_prev - g_last)
    for h in range(ML_HEADS):
        lane_b = ML_GATE + ML_HEADS + h
        k = qk_ref[0, :, ML_QK_WIDTH + ML_QK_DIM * h:ML_QK_WIDTH + ML_QK_DIM * (h + 1)]
        _ml_state_step(cn_ref, h, cn_ref[h], keep[:, lane_b:lane_b + 1], k,
                       ws[:, LANES * h:LANES * (h + 1)], _ml_v_aug(v_ref, h, n))
        yield
    m_ref[...] = jnp.broadcast_to(bcum[0:1, :] + g_last, m_ref.shape)


def _ml_out_body(qk_ref, v_ref, sm_ref, ib_ref, fb_ref, sel_ref, cnb_ref, mb_ref, o_ref, cn_ref, m_ref):
    n = qk_ref.shape[1]

    @pl.when(pl.program_id(1) == 0)
    def _init():
        cn_ref[...] = jnp.zeros_like(cn_ref)
        m_ref[...] = jnp.zeros_like(m_ref)

    head_q = lambda h: qk_ref[0, :, ML_QK_DIM * h:ML_QK_DIM * (h + 1)]
    head_k = lambda h: qk_ref[0, :, ML_QK_WIDTH + ML_QK_DIM * h:ML_QK_WIDTH + ML_QK_DIM * (h + 1)]
    qk_all = [_dot_nt(head_q(h), head_k(h)) for h in range(ML_HEADS)]
    inter_all = [_dot(head_q(h), jnp.concatenate([cn_ref[h].astype(BF16), cnb_ref[0, 0, h]], axis=1))
                 for h in range(ML_HEADS)]
    yield
    valid, is_f, bcum, a = _ml_gates(sm_ref, ib_ref, fb_ref, n, True)
    m_prev = jnp.where(is_f, m_ref[0:1, :], mb_ref[0, 0, 0:1, :])
    a_t = a.T
    pre = suf = a_t[ML_GATE:ML_GATE + ML_ND, :]
    pos = lax.broadcasted_iota(jnp.int32, (ML_ND, n), 1)
    k = 1
    while k < n:
        pre = jnp.maximum(pre, jnp.where(pos >= k, pltpu.roll(pre, k, axis=1), -jnp.inf))
        suf = jnp.maximum(suf, jnp.where(pos < n - k, pltpu.roll(suf, n - k, axis=1), -jnp.inf))
        k *= 2
    run = jnp.where(lax.broadcasted_iota(jnp.int32, (ML_ND, n), 0) < ML_HEADS, pre, suf)
    run = jnp.concatenate([jnp.zeros((ML_GATE, n), F32), run,
                           jnp.zeros((LANES - ML_GATE - ML_ND, n), F32)], axis=0).T
    g = jnp.maximum(m_prev, run)
    m_t = bcum + g
    yield
    floor = jnp.exp(-m_t)
    dense = lambda x, lane_: jnp.broadcast_to(x[:, lane_:lane_ + 1], (n, LANES))
    diag_t = jnp.exp(a - g).T
    g_last = g[n - 1:n, :]
    ws = _expand(jnp.exp(a - g_last), sel_ref[:, :ML_HEADS * LANES])
    keep = jnp.exp(m_prev - g_last)
    ti = lax.broadcasted_iota(jnp.int32, (n, n), 0)
    si = lax.broadcasted_iota(jnp.int32, (n, n), 1)
    not_above, above, on_diag = si <= ti, si > ti, si == ti
    wide = lambda x: jnp.concatenate([x] * (n // LANES), axis=1)
    for h in range(ML_HEADS):
        lanes = (ML_GATE + h, ML_GATE + ML_HEADS + h)
        g_d = [dense(g, ln) for ln in lanes]
        qk, inter = qk_all[h], inter_all[h]
        e = jnp.where(not_above, a_t[lanes[0]:lanes[0] + 1, :] - wide(g_d[0]),
                      a_t[lanes[1]:lanes[1] + 1, :] - wide(g_d[1]))
        p = qk * jnp.exp(e)
        p_f = jnp.where(not_above, p, 0.0).astype(BF16)
        p_b = jnp.where(above, p, jnp.where(on_diag, qk * diag_t[lanes[1]:lanes[1] + 1, :], 0.0)).astype(BF16)
        intra = _dot(jnp.concatenate([p_f, p_b], axis=0), _ml_v_aug(v_ref, h, n))
        out = None
        for d in range(2):
            w_inter = jnp.exp(m_prev[:, lanes[d]:lanes[d] + 1] - g_d[d])
            s = (intra[n * d:n * (d + 1)] + jnp.concatenate([w_inter] * (ML_AUG // LANES), axis=1)
                 * inter[:, ML_AUG * d:ML_AUG * (d + 1)])
            rn = 1.0 / jnp.maximum(jnp.abs(s[:, ML_V_DIM:]), dense(floor, lanes[d]))
            hid = s[:, :ML_V_DIM] * jnp.concatenate([rn] * (ML_V_DIM // LANES), axis=1)
            out = hid if out is None else out + hid
        o_ref[0, :, ML_V_DIM * h:ML_V_DIM * (h + 1)] = out
        yield
    for h in range(ML_HEADS):
        lane_f = ML_GATE + h
        _ml_state_step(cn_ref, h, cn_ref[h], keep[:, lane_f:lane_f + 1], head_k(h),
                       ws[:, LANES * h:LANES * (h + 1)], _ml_v_aug(v_ref, h, n))
        yield
    m_ref[...] = jnp.broadcast_to(m_t[n - 1:n, :], m_ref.shape)


def _mlstm_parts(cvm, p3, sm3, i_bias, f_bias, n_ctx):
    nb, t, _ = cvm.shape
    nt = t // TILE
    assert _SM_OFF["i_f"] == ML_GATE and _SM_OFF["f_f"] == ML_GATE + ML_ND
    row = lambda v, off: jnp.pad(v.reshape(1, -1), ((0, 0), (off, LANES - off - ML_ND)))
    sel = np.zeros((LANES, ML_ND * LANES), np.float32)
    for r in range(ML_ND):
        sel[ML_GATE + r, r * LANES:(r + 1) * LANES] = 1.0
    const = lambda b, i: (0, 0)
    specs = lambda order: [
        pl.BlockSpec((1, TILE, 2 * ML_QK_WIDTH), lambda b, i: (b, order(i), 0)),
        pl.BlockSpec((1, TILE, ML_V_WIDTH), lambda b, i: (b, order(i), _P_OFF["m_v"] // ML_V_WIDTH)),
        pl.BlockSpec((1, TILE, LANES), lambda b, i: (b, order(i), 0)),
        pl.BlockSpec((1, LANES), const),
        pl.BlockSpec((1, LANES), const),
        pl.BlockSpec((LANES, ML_ND * LANES), const)]
    args = (cvm, p3, sm3, row(i_bias, ML_GATE), row(f_bias, ML_GATE + ML_ND), jnp.asarray(sel, BF16))
    cn_block = (1, 1, ML_HEADS, ML_QK_DIM, ML_AUG)
    m_block = (1, 1, SUBLANES, LANES)
    scratch = [pltpu.VMEM(cn_block[2:], F32), pltpu.VMEM(m_block[2:], F32)]
    bwd = _tile_order(nt, True, n_ctx // TILE)
    fwd = _tile_order(nt, False)
    st_idx = lambda b, i: (b, bwd(i)) + (0,) * 3
    states = (_ml_states_body, specs(bwd), list(args),
              [jax.ShapeDtypeStruct((nb, nt) + cn_block[2:], BF16),
               jax.ShapeDtypeStruct((nb, nt) + m_block[2:], F32)],
              [pl.BlockSpec(cn_block, st_idx), pl.BlockSpec(m_block, lambda b, i: (b, bwd(i), 0, 0))],
              scratch)
    out = lambda cn_b, m_b: (
        _ml_out_body,
        specs(fwd) + [pl.BlockSpec(cn_block, lambda b, i: (b, i, 0, 0, 0)),
                      pl.BlockSpec(m_block, lambda b, i: (b, i, 0, 0))],
        list(args) + [cn_b, m_b],
        [jax.ShapeDtypeStruct((nb, t, ML_V_WIDTH), F32)],
        [pl.BlockSpec((1, TILE, ML_V_WIDTH), lambda b, i: (b, i, 0))], scratch)
    return states, out


GLA_SUB = 256
GLA_NCH = GLA_SUB // GLA_CHUNK
GLA_COLS = BF16_SUBLANES


def _gla_layout(is_ctx, lat_rows):
    r = np.arange(GLA_SUB)
    if is_ctx:
        return r // GLA_CHUNK, r % GLA_CHUNK
    col = r % SUBLANES
    cpc = GLA_CHUNK // lat_rows
    return col // cpc, (col % cpc) * lat_rows + r // SUBLANES


def _gla_consts(is_ctx, lat_rows, rev):
    ch, pos = _gla_layout(is_ctx, lat_rows)
    same = ch[:, None] == ch[None, :]
    before = (pos[None, :] >= pos[:, None]) if rev else (pos[None, :] <= pos[:, None])
    tri = (same & before).astype(np.float32)
    cmask = np.stack([np.repeat((ch == j)[:, None], LANES, axis=1) for j in range(GLA_NCH)])
    return tri, cmask.astype(np.float32)


def _gla_row(is_ctx, lat_rows, j, p):
    ch, pos = _gla_layout(is_ctx, lat_rows)
    return int(np.nonzero((ch == j) & (pos == p))[0][0])


def _per_chunk_rows(b, is_ctx, lat_rows, p):
    rows = [b[_gla_row(is_ctx, lat_rows, j, p):_gla_row(is_ctx, lat_rows, j, p) + 1, :]
            for j in range(GLA_NCH)]
    w = b.shape[1]
    if is_ctx:
        full = jnp.concatenate([jnp.broadcast_to(r, (GLA_CHUNK, w)) for r in rows], axis=0)
    else:
        rep = SUBLANES // GLA_NCH
        pat = jnp.concatenate([jnp.broadcast_to(r, (rep, w)) for r in rows], axis=0)
        full = jnp.broadcast_to(pat[None], (GLA_SUB // SUBLANES, SUBLANES, w)).reshape(GLA_SUB, w)
    return rows, full


def _gla_sub(q, k, v, araw, aup, abias, tri_b, tri_f, cmask_ref, st_ref, *, rev, is_ctx, lat_rows,
             want_out=True):
    g = _log_sigmoid(_dot_hp(araw, aup) + abias) * (1.0 / GLA_TAU)
    b = _dot_exact_lhs(tri_b, g, 2)
    lasts, last = _per_chunk_rows(b, is_ctx, lat_rows, 0 if rev else GLA_CHUNK - 1)
    kl = (k * jnp.exp(last - b)).astype(BF16)
    if want_out:
        _, ref = _per_chunk_rows(b, is_ctx, lat_rows, GLA_CHUNK // 2)
        qs = q * (GLA_K_DIM ** -0.5)
        qe = (qs * jnp.exp(b - ref)).astype(BF16)
        ke = (k * jnp.exp(ref - b)).astype(BF16)
        qb = (qs * jnp.exp(b)).astype(BF16)
        visible = tri_f > 0.0
    order = range(GLA_NCH - 1, -1, -1) if rev else range(GLA_NCH)
    outs = []
    for h in range(GLA_HEADS):
        ks = slice(GLA_K_DIM * h, GLA_K_DIM * (h + 1))
        vh = v[:, GLA_V_DIM * h:GLA_V_DIM * (h + 1)]
        klm = jnp.concatenate([kl[:, ks] * cmask_ref[j] for j in range(GLA_NCH)], axis=1)
        upd = _dot_tn(vh, klm)
        s = st_ref[h]
        s_in = [None] * GLA_NCH
        for j in order:
            s_in[j] = s.astype(BF16)
            s = s * jnp.exp(lasts[j][:, ks]) + upd[:, GLA_K_DIM * j:GLA_K_DIM * (j + 1)]
        st_ref[h] = s
        if want_out:
            att = jnp.where(visible, _dot_nt(qe[:, ks], ke[:, ks]), 0.0).astype(BF16)
            qbm = jnp.concatenate([qb[:, ks] * cmask_ref[j] for j in range(GLA_NCH)], axis=1)
            outs.append(_dot(att, vh) + _dot_nt(qbm, jnp.concatenate(s_in, axis=1)))
    return jnp.concatenate(outs, axis=1) if want_out else None


def _gla_kernel(qc_ref, kc_ref, vc_ref, sc_ref, ql_ref, kl_ref, vl_ref, sl_ref,
                aup_ref, ab_ref, tcb_ref, tcf_ref, cmc_ref, tlb_ref, tlf_ref, cml_ref,
                *rest, rev, ctx_rows, lat_rows, n_cblk, ctx_out):
    acc_ref, o_ref, st_ref, ctxo_ref = rest if len(rest) == 4 else (None,) + rest
    i = pl.program_id(1)
    aup = aup_ref[...]
    abias = ab_ref[...]
    r0, r1 = ctx_rows, ctx_rows + lat_rows
    sub = functools.partial(_gla_sub, aup=aup, abias=abias, st_ref=st_ref, rev=rev, lat_rows=lat_rows)

    @pl.when(i == 0)
    def _ctx():
        st_ref[...] = jnp.zeros_like(st_ref)
        o = sub(qc_ref[0].astype(F32), kc_ref[0].astype(F32), vc_ref[0], sc_ref[0], tri_b=tcb_ref[...],
                tri_f=tcf_ref[...], cmask_ref=cmc_ref, is_ctx=True, want_out=ctx_out)
        if ctx_out:
            ctxo_ref[...] = o

    @pl.when(i > 0)
    def _lat():
        cblk = (n_cblk - i) if rev else (i - 1)
        halves = range(GLA_COLS // SUBLANES)
        for half in (reversed(halves) if rev else halves):
            cs = slice(SUBLANES * half, SUBLANES * (half + 1))
            take = lambda r: r[0, r0:r1].astype(F32)[:, cs, :].reshape(GLA_SUB, r.shape[-1])
            o = sub(take(ql_ref), take(kl_ref), take(vl_ref).astype(BF16), take(sl_ref),
                    tri_b=tlb_ref[...], tri_f=tlf_ref[...], cmask_ref=cml_ref, is_ctx=False)
            _write(o_ref, (0, slice(r0, r1), cs, slice(None)),
                   o.reshape(lat_rows, SUBLANES, GLA_V_WIDTH), acc_ref)
        for r in range(ctx_rows):
            if ctx_out:
                start = pl.multiple_of(r * GRID_W + cblk * GLA_COLS, GLA_COLS)
                _write(o_ref, (0, r, slice(None), slice(None)), ctxo_ref[pl.ds(start, GLA_COLS), :], acc_ref)
            else:
                o_ref[0, r, :, :] = jnp.zeros((GLA_COLS, GLA_V_WIDTH), F32)


def _gla_scan(p3, sm3, a_up, a_bias, n_ctx, rev, acc, ctx_out):
    nb, t, ncol = p3.shape
    rows = t // GRID_W
    ctx_rows = n_ctx // GRID_W
    lat_rows = rows - ctx_rows
    n_cblk = GRID_W // GLA_COLS
    p4 = p3.reshape(nb, rows, GRID_W, ncol)
    sm4 = sm3.reshape(nb, rows, GRID_W, LANES)
    a_off = _SM_OFF["a_b"] if rev else _SM_OFF["a_f"]
    aup = jnp.pad(a_up, ((a_off, LANES - a_off - GLA_RANK), (0, 0)))
    cblk = lambda i: jnp.where(i == 0, n_cblk - 1 if rev else 0, (n_cblk - i) if rev else (i - 1))
    ctx = lambda blk: (lambda b, i: (b, 0, blk))
    lat = lambda blk: (lambda b, i: (b, 0, cblk(i), blk))
    const2 = lambda b, i: (0, 0)
    const3 = lambda b, i: (0, 0, 0)
    widths = (GLA_K_WIDTH, GLA_K_WIDTH, GLA_V_WIDTH)
    offs = (_P_OFF["g_q"], _P_OFF["g_k"], _P_OFF["g_v"])
    consts = []
    const_specs = []
    for is_ctx in (True, False):
        tri, cmask = _gla_consts(is_ctx, lat_rows, rev)
        consts += [jnp.asarray(tri, BF16), jnp.asarray(tri, F32), jnp.asarray(cmask, BF16)]
        const_specs += [pl.BlockSpec((GLA_SUB, GLA_SUB), const2), pl.BlockSpec((GLA_SUB, GLA_SUB), const2),
                        pl.BlockSpec((GLA_NCH, GLA_SUB, LANES), const3)]
    out_spec = pl.BlockSpec((1, rows, GLA_COLS, GLA_V_WIDTH), lat(0))
    in_specs = ([pl.BlockSpec((1, n_ctx, w), ctx(o // w)) for w, o in zip(widths, offs)]
                + [pl.BlockSpec((1, n_ctx, LANES), ctx(0))]
                + [pl.BlockSpec((1, rows, GLA_COLS, w), lat(o // w)) for w, o in zip(widths, offs)]
                + [pl.BlockSpec((1, rows, GLA_COLS, LANES), lat(0))]
                + [pl.BlockSpec((LANES, GLA_K_WIDTH), const2), pl.BlockSpec((1, GLA_K_WIDTH), const2)]
                + const_specs)
    args = [p3, p3, p3, sm3, p4, p4, p4, sm4, aup, a_bias.reshape(1, GLA_K_WIDTH)] + consts
    if acc is not None:
        in_specs.append(out_spec)
        args.append(acc.reshape(nb, rows, GRID_W, GLA_V_WIDTH))
    out = pl.pallas_call(
        functools.partial(_gla_kernel, rev=rev, ctx_rows=ctx_rows, lat_rows=lat_rows, n_cblk=n_cblk,
                          ctx_out=ctx_out),
        out_shape=jax.ShapeDtypeStruct((nb, rows, GRID_W, GLA_V_WIDTH), F32),
        grid=(nb, n_cblk + 1),
        in_specs=in_specs,
        out_specs=out_spec,
        scratch_shapes=[pltpu.VMEM((GLA_HEADS, GLA_V_DIM, GLA_K_DIM), F32),
                        pltpu.VMEM((n_ctx, GLA_V_WIDTH), F32)],
        input_output_aliases={} if acc is None else {len(args) - 1: 0},
        compiler_params=_cparams(("parallel", "arbitrary")),
        name="gla_scan",
    )(*args)
    return out.reshape(nb, t, GLA_V_WIDTH)


def _group_rmsnorm(y, groups):
    width = y.shape[-1] // groups
    ones = jnp.ones((width, LANES), BF16)
    out = []
    for g in range(groups):
        yg = y[:, width * g:width * (g + 1)]
        ms = _dot((yg * yg).astype(BF16), ones) * (1.0 / width)
        out.append(yg * jnp.concatenate([lax.rsqrt(ms + EPS)] * (width // LANES), axis=1))
    return jnp.concatenate(out, axis=1)


def _post_kernel(x_ref, y_ref, h_ref, o_ref_in, z_ref, mo_ref, gg_ref, gs_ref, gm_ref, gl_ref,
                 nws_ref, nwm_ref, nwg_ref, wbs_ref, wbm_ref, wbg_ref, wout_ref, g1_ref,
                 nwf_ref, sc2_ref, sh2_ref, xo_ref, ho_ref):
    y_ssd = (_group_rmsnorm(y_ref[0] * _silu(z_ref[0]), 2) * nws_ref[...]).astype(BF16)
    y_ml = (_group_rmsnorm(h_ref[0], ML_HEADS) * nwm_ref[...] * _sigmoid(mo_ref[0])).astype(BF16)
    y_gla = (_group_rmsnorm(o_ref_in[0], GLA_HEADS) * nwg_ref[...] * _silu(gg_ref[0])).astype(BF16)
    merged = (_sigmoid(gs_ref[0]) * _dot(y_ssd, wbs_ref[...])
              + _sigmoid(gm_ref[0]) * _dot(y_ml, wbm_ref[...])
              + _sigmoid(gl_ref[0]) * _dot(y_gla, wbg_ref[...]))
    x_new = x_ref[0] + g1_ref[0] * _dot(merged.astype(BF16), wout_ref[...])
    xo_ref[0] = x_new
    ho_ref[0] = (_rms(x_new) * nwf_ref[...] * (1.0 + sc2_ref[0]) + sh2_ref[0]).astype(ho_ref.dtype)


def _post(x, scans, p3, norm_ws, w_bs, w_out, layer, norm_ffn_w, mods, n_ctx, tile0):
    nb, t, d = x.shape
    nt = t // TILE - tile0
    ctx_tiles = n_ctx // TILE
    tok = lambda blk: (lambda b, i: (b, i + tile0, blk))
    out = lambda b, i: (b, i, 0)
    const = lambda b, i: (0, 0)
    tok_spec = lambda blk: pl.BlockSpec((1, TILE, d), tok(blk))
    w_spec = pl.BlockSpec((None, d, d), lambda b, i: (layer, 0, 0), pipeline_mode=pl.Buffered(1))
    vec = pl.BlockSpec((1, d), const)
    mod = lambda k: pl.BlockSpec((1, 1, d), _mod_row(nb, k, ctx_tiles, tile0))
    names = ("s_z", "m_o", "g_g", "gate_ssd", "gate_ml", "gate_gla")
    in_specs = ([tok_spec(0)] * 4 + [tok_spec(_P_OFF[nm] // d) for nm in names]
                + [vec] * 3 + [w_spec] * 4 + [mod(2), vec, mod(4), mod(3)])
    return pl.pallas_call(
        _post_kernel,
        out_shape=(jax.ShapeDtypeStruct((nb, nt * TILE, d), F32),
                   jax.ShapeDtypeStruct((nb, nt * TILE, d), BF16)),
        grid=(nb, nt),
        in_specs=in_specs,
        out_specs=(pl.BlockSpec((1, TILE, d), out), pl.BlockSpec((1, TILE, d), out)),
        compiler_params=_cparams(("parallel", "parallel")),
        name="post",
    )(x, *scans, *([p3] * 6), *[w.reshape(1, d) for w in norm_ws], *w_bs, w_out, mods,
      norm_ffn_w.reshape(1, d), mods, mods)


def _ffn_in_kernel(a_ref, w_ref, o_ref):
    acc = _dot(a_ref[...], w_ref[...])
    half = acc.shape[1] // 2
    o_ref[...] = (_silu(acc[:, :half]) * acc[:, half:]).astype(o_ref.dtype)


def _ffn_in(h, w_gu, layer, half):
    m, k = h.shape
    n_half = w_gu.shape[-1] // 2
    tm = _row_tile(m)
    return pl.pallas_call(
        _ffn_in_kernel,
        out_shape=jax.ShapeDtypeStruct((m, n_half), BF16),
        grid=(n_half // half, m // tm),
        in_specs=[pl.BlockSpec((tm, k), lambda j, i: (i, 0)),
                  pl.BlockSpec((None, k, 2 * half), lambda j, i: (layer, 0, j))],
        out_specs=pl.BlockSpec((tm, half), lambda j, i: (i, j)),
        compiler_params=_cparams(("parallel", "parallel")),
        name="ffn_in",
    )(h, w_gu)


def _ffn_out_kernel(a_ref, w_ref, x_ref, g_ref, nw_ref, sc_ref, sh_ref, xo_ref, ho_ref):
    x_new = x_ref[0] + g_ref[0] * _dot(a_ref[0], w_ref[...])
    xo_ref[0] = x_new
    ho_ref[0] = (_rms(x_new) * nw_ref[...] * (1.0 + sc_ref[0]) + sh_ref[0]).astype(ho_ref.dtype)


def _ffn_out_last_kernel(a_ref, w_ref, x_ref, g_ref, nw_ref, o_ref):
    x_new = x_ref[0] + g_ref[0] * _dot(a_ref[0], w_ref[...])
    o_ref[0] = _rms(x_new) * nw_ref[...]


def _ffn_out(a, w, layer, x, mods, n_ctx, tile0, next_norm_w, next_mods):
    nb, t, d = x.shape
    k = a.shape[-1]
    ctx_tiles = n_ctx // TILE
    tok = lambda b, i: (b, i, 0)
    mod = lambda k_: pl.BlockSpec((1, 1, d), _mod_row(nb, k_, ctx_tiles, tile0))
    in_specs = [pl.BlockSpec((1, TILE, k), tok),
                pl.BlockSpec((None, k, d), lambda b, i: (layer, 0, 0), pipeline_mode=pl.Buffered(1)),
                pl.BlockSpec((1, TILE, d), tok),
                mod(5),
                pl.BlockSpec((1, d), lambda b, i: (0, 0))]
    args = [a, w, x, mods, next_norm_w.reshape(1, d)]
    if next_mods is None:
        body = _ffn_out_last_kernel
        out_shape = jax.ShapeDtypeStruct((nb, t, d), F32)
        out_specs = pl.BlockSpec((1, TILE, d), tok)
    else:
        body = _ffn_out_kernel
        in_specs += [mod(1), mod(0)]
        args += [next_mods, next_mods]
        out_shape = (jax.ShapeDtypeStruct((nb, t, d), F32), jax.ShapeDtypeStruct((nb, t, d), BF16))
        out_specs = (pl.BlockSpec((1, TILE, d), tok), pl.BlockSpec((1, TILE, d), tok))
    return pl.pallas_call(
        body,
        out_shape=out_shape,
        grid=(nb, t // TILE),
        in_specs=in_specs,
        out_specs=out_specs,
        compiler_params=_cparams(("parallel", "parallel")),
        name="ffn_out",
    )(*args)


def _proj_weights(w_in):
    cols = lambda names: [w_in[..., _IN_OFF[nm]:_IN_OFF[nm] + _IN_W[nm]] for nm in names]
    main = jnp.concatenate(cols(_P_ORDER), axis=-1)
    small = jnp.pad(jnp.concatenate(cols(_SMALL), axis=-1), ((0, 0), (0, 0), (0, LANES - N_SMALL_USED)))
    return main.astype(BF16), small.astype(BF16)


def _ffn_weight(w_ffn_in, half):
    d_ff = w_ffn_in.shape[-1] // 2
    cols = []
    for j in range(d_ff // half):
        cols.append(w_ffn_in[..., j * half:(j + 1) * half])
        cols.append(w_ffn_in[..., d_ff + j * half:d_ff + (j + 1) * half])
    return jnp.concatenate(cols, axis=-1).astype(BF16)


def kernel(x, c, ctx, c_ctx, w_mod, b_mod, norm_mix_w, norm_ffn_w, w_in, ssd_conv_w, ssd_conv_b, ssd_dt_bias, ssd_a_log, ssd_d, ssd_norm_w, ml_conv_w, ml_conv_b, ml_i_bias, ml_f_bias, ml_norm_w, gla_a_up, gla_a_bias, gla_norm_w, w_b_ssd, w_b_ml, w_b_gla, w_out, w_ffn_in, w_ffn_out, final_norm_w):
    nb, n_lat, d = x.shape
    n_ctx = ctx.shape[1]
    t = n_ctx + n_lat
    depth = w_in.shape[0]
    d_ff = w_ffn_out.shape[1]
    assert n_ctx == TILE == GLA_SUB and n_lat % TILE == 0 and n_lat // GRID_W == 32
    ffn_half = d_ff // 2

    xs = jnp.concatenate([ctx, x], axis=1)
    c16 = jnp.pad(jnp.concatenate([c, c_ctx[None]], axis=0), ((0, 2 * SUBLANES - nb - 1), (0, 0)))
    mods = [_modulation(c16, w_mod, b_mod, l).reshape(2 * SUBLANES * 6, 1, d) for l in range(depth)]
    h = _norm_mod(xs, norm_mix_w[0], mods[0], 0, 1, n_ctx)
    ml_post = jnp.concatenate([jnp.ones((ML_QK_WIDTH,), F32),
                               jnp.full((ML_QK_WIDTH,), ML_QK_DIM ** -0.5, F32)])
    w_main, w_small = _proj_weights(w_in)
    w_gu = _ffn_weight(w_ffn_in, ffn_half)
    w_bs = [w.astype(BF16) for w in (w_b_ssd, w_b_ml, w_b_gla)]
    w_out_b = w_out.astype(BF16)
    w_ffn_out_b = w_ffn_out.astype(BF16)
    for l in range(depth):
        last = l == depth - 1
        h2d = h.reshape(nb * t, d)
        p3 = _matmul(h2d, w_main, l, N_PROJ // 5, BF16).reshape(nb, t, N_PROJ)
        sm3 = _matmul(h2d, w_small, l, LANES, F32).reshape(nb, t, LANES)
        cvs = _conv(p3, _P_OFF["s_x"], ssd_conv_w[l], ssd_conv_b[l],
                    jnp.ones((ssd_conv_w.shape[-1],), F32), n_ctx)
        cvm = _conv(p3, _P_OFF["m_q"], ml_conv_w[l], ml_conv_b[l], ml_post, n_ctx)
        d_e = jnp.repeat(ssd_d[l], SSD_HEAD_DIM).reshape(1, SSD_WIDTH)
        ssd_states, ssd_out = _ssd_parts(cvs, sm3, ssd_dt_bias[l], ssd_a_log[l], d_e, n_ctx)
        ml_states, ml_out = _mlstm_parts(cvm, p3, sm3, ml_i_bias[l], ml_f_bias[l], n_ctx)
        grid = (nb, t // TILE)
        ssd_st, ml_st = _run_together("bwd_states", grid, ssd_states, ml_states)
        (y,), (hm,) = _run_together("ssd_mlstm", grid, ssd_out(*ssd_st), ml_out(*ml_st))
        og = None
        for rev in (False, True):
            k = int(rev)
            og = _gla_scan(p3, sm3, gla_a_up[l, k], gla_a_bias[l, k], n_ctx, rev, og, not last)
        tile0 = n_ctx // TILE if last else 0
        xs, h2 = _post(xs, (y, hm, og), p3, (ssd_norm_w[l], ml_norm_w[l], gla_norm_w[l]),
                       w_bs, w_out_b, l, norm_ffn_w[l], mods[l], n_ctx, tile0)
        nt = xs.shape[1]
        a = _ffn_in(h2.reshape(nb * nt, d), w_gu, l, ffn_half)
        a = a.reshape(nb, nt, d_ff)
        if last:
            return _ffn_out(a, w_ffn_out_b, l, xs, mods[l], n_ctx, tile0, final_norm_w, None)
        xs, h = _ffn_out(a, w_ffn_out_b, l, xs, mods[l], n_ctx, tile0,
                         norm_mix_w[l + 1], mods[l + 1])
```

```python
import functools

import numpy as np
import jax
import jax.numpy as jnp
from jax import lax
from jax.experimental import pallas as pl
from jax.experimental.pallas import tpu as pltpu

F32 = jnp.float32
BF16 = jnp.bfloat16

EPS = 1e-6
LOG2E = 1.4426950408889634
GRID_W = 64
SSD_HEADS = 16
SSD_HEAD_DIM = 64
SSD_WIDTH = 1024
SSD_STATE = 64
SSD_BC = 128
ML_HEADS = 4
ML_QK_DIM = 128
ML_V_DIM = 256
ML_QK_WIDTH = 512
ML_V_WIDTH = 1024
GLA_HEADS = 4
GLA_K_DIM = 128
GLA_V_DIM = 256
GLA_K_WIDTH = 512
GLA_V_WIDTH = 1024
GLA_RANK = 16
GLA_TAU = 16.0
GLA_CHUNK = 64

LANES = 128
SUBLANES = 8
BF16_SUBLANES = 16
VMEM_LIMIT = 56 * 1024 * 1024

TILE = 256

_IN_NAMES = ("s_x", "s_z", "s_b", "s_c", "dt_f", "dt_b",
             "m_q", "m_k", "m_v", "m_o", "i_f", "i_b", "f_f", "f_b",
             "g_q", "g_k", "g_v", "g_g", "a_f", "a_b",
             "gate_ssd", "gate_ml", "gate_gla")
_IN_WIDTHS = (1024, 1024, 128, 128, 16, 16,
              512, 512, 1024, 1024, 4, 4, 4, 4,
              512, 512, 1024, 1024, 16, 16,
              1024, 1024, 1024)
_IN_OFF = dict(zip(_IN_NAMES, np.concatenate([[0], np.cumsum(_IN_WIDTHS)[:-1]]).tolist()))
_IN_W = dict(zip(_IN_NAMES, _IN_WIDTHS))

_P_ORDER = ("s_z", "m_o", "g_g", "gate_ssd", "gate_ml", "gate_gla", "m_v", "g_v",
            "g_q", "g_k", "s_x", "s_b", "s_c", "m_q", "m_k")
_P_OFF = {}
_o = 0
for _n in _P_ORDER:
    _P_OFF[_n] = _o
    _o += _IN_W[_n]
N_PROJ = _o
_SMALL = ("dt_f", "dt_b", "i_f", "i_b", "f_f", "f_b", "a_f", "a_b")
_SM_OFF = {}
_s = 0
for _n in _SMALL:
    _SM_OFF[_n] = _s
    _s += _IN_W[_n]
N_SMALL_USED = _s


def _cparams(sem):
    return pltpu.CompilerParams(dimension_semantics=sem, vmem_limit_bytes=VMEM_LIMIT)


def _sigmoid(x):
    return 0.5 * jnp.tanh(0.5 * x) + 0.5


def _silu(x):
    return x * _sigmoid(x)


def _softplus(x):
    return jnp.maximum(x, 0.0) + jnp.log1p(jnp.exp(-jnp.abs(x)))


def _log_sigmoid(x):
    return jnp.minimum(x, 0.0) - jnp.log(1.0 + jnp.exp(-jnp.abs(x)))


def _split(x, n):
    out = []
    r = x
    for _ in range(n):
        p = r.astype(BF16)
        out.append(p)
        r = r - p.astype(F32)
    return out


def _dot(a, b):
    return jnp.dot(a, b, preferred_element_type=F32)


def _dot_nt(a, b):
    return lax.dot_general(a, b, (((1,), (1,)), ((), ())), preferred_element_type=F32)


def _dot_tn(a, b):
    return lax.dot_general(a, b, (((0,), (0,)), ((), ())), preferred_element_type=F32)


def _dot_exact_lhs(t, x, pieces):
    return sum(_dot(t, p) for p in _split(x, pieces))


def _dot_hp(a, b):
    ah, am = _split(a, 2)
    bh, bm = _split(b, 2)
    return _dot(ah, bh) + _dot(ah, bm) + _dot(am, bh)


def _causal(n, rev):
    t = lax.broadcasted_iota(jnp.int32, (n, n), 0)
    s = lax.broadcasted_iota(jnp.int32, (n, n), 1)
    return (s >= t) if rev else (s <= t)


def _rms(x):
    return x * lax.rsqrt(jnp.mean(x * x, axis=-1, keepdims=True) + EPS)


def _mod_kernel(c_ref, w_ref, b_ref, o_ref):
    o_ref[...] = _dot_hp(_silu(c_ref[...]), w_ref[...]) + b_ref[...]


def _modulation(c16, w_mod, b_mod, layer):
    rows, d = c16.shape
    n = w_mod.shape[-1]
    tn = 1536
    return pl.pallas_call(
        _mod_kernel,
        out_shape=jax.ShapeDtypeStruct((rows, n), F32),
        grid=(n // tn,),
        in_specs=[pl.BlockSpec((rows, d), lambda j: (0, 0)),
                  pl.BlockSpec((None, d, tn), lambda j: (layer, 0, j)),
                  pl.BlockSpec((None, 1, tn), lambda j: (layer, 0, j))],
        out_specs=pl.BlockSpec((rows, tn), lambda j: (0, j)),
        compiler_params=_cparams(("arbitrary",)),
        name="modulation",
    )(c16, w_mod, b_mod.reshape(b_mod.shape[0], 1, n))


def _mod_row(nb, k, ctx_tiles, tile0=0):
    return lambda b, i: (jnp.where(i + tile0 < ctx_tiles, nb, b) * 6 + k, 0, 0)


def _norm_mod_kernel(x_ref, w_ref, sc_ref, sh_ref, o_ref):
    o_ref[0] = (_rms(x_ref[0]) * w_ref[...] * (1.0 + sc_ref[0]) + sh_ref[0]).astype(o_ref.dtype)


def _norm_mod(x, w, mods, k_shift, k_scale, n_ctx):
    nb, t, d = x.shape
    return pl.pallas_call(
        _norm_mod_kernel,
        out_shape=jax.ShapeDtypeStruct((nb, t, d), BF16),
        grid=(nb, t // TILE),
        in_specs=[pl.BlockSpec((1, TILE, d), lambda b, i: (b, i, 0)),
                  pl.BlockSpec((1, d), lambda b, i: (0, 0)),
                  pl.BlockSpec((1, 1, d), _mod_row(nb, k_scale, n_ctx // TILE)),
                  pl.BlockSpec((1, 1, d), _mod_row(nb, k_shift, n_ctx // TILE))],
        out_specs=pl.BlockSpec((1, TILE, d), lambda b, i: (b, i, 0)),
        compiler_params=_cparams(("parallel", "parallel")),
        name="norm_mod",
    )(x, w.reshape(1, d), mods, mods)


MM_STAGE = 256


def _mm_body(a_ref, w_ref, o_ref):
    a = a_ref[...]
    width = o_ref.shape[1]
    stage = min(MM_STAGE, width)
    for c0 in range(0, width, stage):
        o_ref[:, c0:c0 + stage] = _dot(a, w_ref[:, c0:c0 + stage]).astype(o_ref.dtype)
        yield


def _row_tile(m):
    return 512 if m % 512 == 0 else TILE


def _mm_part(a, w, layer, tn, out_dtype, tile0, n_tiles):
    m, k = a.shape
    tm = _row_tile(m)
    return (_mm_body,
            [pl.BlockSpec((tm, k), lambda j, i: (i, 0)),
             pl.BlockSpec((None, k, tn), lambda j, i: (layer, 0, j + tile0))],
            [a, w], [jax.ShapeDtypeStruct((m, n_tiles * tn), out_dtype)],
            [pl.BlockSpec((tm, tn), lambda j, i: (i, j))], [])


CONV_K = 5
CONV_ROWS = 128
CONV_X = SSD_WIDTH // LANES
CONV_BC = 2 * SSD_BC // LANES
CONV_QK = 2 * ML_QK_WIDTH // LANES
CV_QK = SSD_WIDTH
CV_BC = SSD_WIDTH + 2 * ML_QK_WIDTH
CV_WIDTH = CV_BC + 2 * SSD_BC


def _conv_body(u_ref, w_ref, b_ref, s_ref, o_ref, pad_ref, *, n_ctx):
    t, c = u_ref.shape[1], u_ref.shape[2]
    half = CONV_K // 2
    zeros = jnp.zeros((SUBLANES, c), F32)
    w = w_ref[...]
    bias = b_ref[...]
    post = s_ref[...]
    for s0, n in ((0, n_ctx), (n_ctx, t - n_ctx)):
        pad_ref[0:SUBLANES, :] = zeros
        pad_ref[SUBLANES:SUBLANES + n, :] = u_ref[0, s0:s0 + n, :].astype(F32)
        pad_ref[SUBLANES + n:2 * SUBLANES + n, :] = zeros
        for r0 in range(0, n, CONV_ROWS):
            acc = bias
            for j in range(CONV_K):
                lo = SUBLANES - half + j + r0
                acc = acc + w[j:j + 1, :] * pad_ref[lo:lo + CONV_ROWS, :]
            o_ref[0, s0 + r0:s0 + r0 + CONV_ROWS, :] = (_silu(acc) * post).astype(o_ref.dtype)
            yield


def _conv_part(u3, w, b, post_scale, n_ctx, grid):
    nb, t, width = u3.shape
    n_cb = width // LANES
    steps = grid[0] * grid[1]
    rep = steps // (nb * n_cb)
    assert steps == rep * nb * n_cb and n_cb == CONV_X + CONV_BC + CONV_QK
    blk = lambda j, i: (j * grid[1] + i) // rep
    chan = lambda j, i: blk(j, i) % n_cb
    out_chan = lambda c: jnp.where(c < CONV_X, c, jnp.where(c < CONV_X + CONV_BC, c + CONV_QK, c - CONV_BC))
    vec = lambda rows: pl.BlockSpec((rows, LANES), lambda j, i: (0, chan(j, i)))
    return (functools.partial(_conv_body, n_ctx=n_ctx),
            [pl.BlockSpec((1, t, LANES), lambda j, i: (blk(j, i) // n_cb, 0, chan(j, i))),
             vec(CONV_K), vec(1), vec(1)],
            [u3, w, b.reshape(1, width), post_scale.reshape(1, width)],
            [jax.ShapeDtypeStruct((nb, t, width), BF16)],
            [pl.BlockSpec((1, t, LANES), lambda j, i: (blk(j, i) // n_cb, 0, out_chan(chan(j, i))))],
            [pltpu.VMEM((t + 2 * SUBLANES, LANES), F32)])


def _tile_order(n_tiles, rev, ctx_tiles=1):
    if rev:
        return lambda i: jnp.where(i < ctx_tiles, ctx_tiles - 1 - i, n_tiles - 1 + ctx_tiles - i)
    return lambda i: i


def _write(o_ref, idx, val, acc_ref):
    o_ref[idx] = val if acc_ref is None else acc_ref[idx] + val


SSD_PAIRS = SSD_HEADS // 2
SSD_B_OFF = SSD_HEADS
SSD_TILE = 256


def _ssd_gates(sm_ref, dtb_ref, alog_ref, n, both):
    lane = lax.broadcasted_iota(jnp.int32, (1, LANES), 1)
    dt = _softplus(sm_ref[0] + dtb_ref[...])
    la = dt * jnp.where(lane < 2 * SSD_HEADS, -jnp.exp(alog_ref[...]), 0.0)
    parts = _split(la, 3)
    upp = jnp.where(_causal(n, True), 1.0, 0.0).astype(BF16)
    cum = sum(_dot(upp, p) for p in parts)
    if both:
        low = jnp.where(_causal(n, False), 1.0, 0.0).astype(BF16)
        cum = jnp.where(lane < SSD_HEADS, sum(_dot(low, p) for p in parts), cum)
    return lane, dt, cum


def _expand(a, e):
    return sum(_dot(p, e) for p in _split(a, 2))


def _group_dup(v, g, lo):
    other = pltpu.roll(v, SSD_STATE, axis=1)
    return jnp.where(lo, v, other) if g == 0 else jnp.where(lo, other, v)


def _ssd_state_step(st_ref, j, bw, xp, elast, off):
    r = lax.broadcasted_iota(jnp.int32, (LANES, LANES), 0) < SSD_STATE
    c = lax.broadcasted_iota(jnp.int32, (LANES, LANES), 1) < SSD_HEAD_DIM
    dec = jnp.where(r, elast[:, off + 2 * j:off + 2 * j + 1], elast[:, off + 2 * j + 1:off + 2 * j + 2])
    st_ref[j] = jnp.where(r == c, dec * st_ref[j] + _dot_tn(bw.astype(BF16), xp), 0.0)


def _interleave(*bodies):
    live = list(bodies)
    while live:
        for body in list(live):
            if next(body, StopIteration) is StopIteration:
                live.remove(body)


def _ssd_states_body(x_ref, bc_ref, sm_ref, dtb_ref, alog_ref, eb_ref, o_ref, st_ref):
    n = x_ref.shape[1]

    @pl.when(pl.program_id(1) == 0)
    def _init():
        st_ref[...] = jnp.zeros_like(st_ref)

    o_ref[0, 0] = st_ref[...].astype(o_ref.dtype)
    yield
    lane, dt, cum = _ssd_gates(sm_ref, dtb_ref, alog_ref, n, False)
    lo = lane < SSD_HEAD_DIM
    last = cum[0:1, :]
    yield
    wst = _expand(jnp.exp(last - cum) * dt, eb_ref[...])
    elast = jnp.exp(last)
    b128 = bc_ref[0, :, :SSD_BC].astype(F32)
    for j in range(SSD_PAIRS):
        sl = slice(LANES * j, LANES * (j + 1))
        bw = _group_dup(b128, j // (SSD_PAIRS // 2), lo) * wst[:, sl]
        _ssd_state_step(st_ref, j, bw, x_ref[0, :, sl], elast, SSD_B_OFF)
        if j % 2:
            yield


def _ssd_out_body(x_ref, bc_ref, sm_ref, dtb_ref, alog_ref, ef_ref, eb_ref, d_ref, sb_ref, o_ref, st_ref):
    n = x_ref.shape[1]

    @pl.when(pl.program_id(1) == 0)
    def _init():
        st_ref[...] = jnp.zeros_like(st_ref)

    zero_b = jnp.zeros((), BF16)
    lo = lax.broadcasted_iota(jnp.int32, (1, LANES), 1) < SSD_HEAD_DIM
    b128_b = bc_ref[0, :, :SSD_BC]
    c128_b = bc_ref[0, :, SSD_BC:]
    cb_all = [_dot_nt(jnp.where(lo if g == 0 else jnp.logical_not(lo), c128_b, zero_b), b128_b)
              for g in range(2)]
    pair_x = lambda j: x_ref[0, :, LANES * j:LANES * (j + 1)]
    rhs_all = [jnp.concatenate([jnp.where(lo, pair_x(j), zero_b), jnp.where(lo, zero_b, pair_x(j)),
                                st_ref[j].astype(BF16), sb_ref[0, 0, j]], axis=0) for j in range(SSD_PAIRS)]
    yield
    lane, dt, cum = _ssd_gates(sm_ref, dtb_ref, alog_ref, n, True)
    is_f = lane < SSD_HEADS
    ldt = jnp.log(dt)
    dsum = jnp.log(dt + pltpu.roll(dt, LANES - SSD_B_OFF, axis=1))
    rt = (jnp.where(lane < 2 * SSD_HEADS, cum - ldt, pltpu.roll(dsum, 2 * SSD_HEADS, axis=1)) * LOG2E).T
    cum2 = cum * LOG2E
    yield
    last = jnp.where(is_f, cum[n - 1:n, :], cum[0:1, :])
    elast = jnp.exp(last)
    ecum = _split(jnp.exp(cum), 2)
    ecum_f = sum(_dot(p, ef_ref[...]) for p in ecum)
    ecum_b = sum(_dot(p, eb_ref[...]) for p in ecum)
    wst = _expand(jnp.exp(last - cum) * dt, ef_ref[...])
    bc = bc_ref[0].astype(F32)
    b128, c128 = bc[:, :SSD_BC], bc[:, SSD_BC:]
    ti = lax.broadcasted_iota(jnp.int32, (n, n), 0)
    si = lax.broadcasted_iota(jnp.int32, (n, n), 1)
    below, above = si < ti, si > ti
    half = SSD_PAIRS // 2
    yield
    for g in range(2):
        cb = cb_all[g]
        cdup = _group_dup(c128, g, lo)
        for j in range(g * half, (g + 1) * half):
            ms = []
            for h in (2 * j, 2 * j + 1):
                e_f = cum2[:, h:h + 1] - rt[h:h + 1, :]
                e_b = cum2[:, SSD_B_OFF + h:SSD_B_OFF + h + 1] - rt[SSD_B_OFF + h:SSD_B_OFF + h + 1, :]
                e = jnp.where(below, e_f, jnp.where(above, e_b, rt[2 * SSD_HEADS + h:2 * SSD_HEADS + h + 1, :]))
                ms.append((cb * jnp.exp2(e)).astype(BF16))
            sl = slice(LANES * j, LANES * (j + 1))
            cs_f = (cdup * ecum_f[:, sl]).astype(BF16)
            cs_b = (cdup * ecum_b[:, sl]).astype(BF16)
            lhs = jnp.concatenate(ms + [cs_f, cs_b], axis=1)
            o_ref[0, :, sl] = _dot(lhs, rhs_all[j]) + d_ref[:, sl] * pair_x(j).astype(F32)
            yield
    for j in range(SSD_PAIRS):
        sl = slice(LANES * j, LANES * (j + 1))
        _ssd_state_step(st_ref, j, _group_dup(b128, j // half, lo) * wst[:, sl], x_ref[0, :, sl], elast, 0)
        if j % 2:
            yield


def _run_together(name, grid, *parts):
    n_in = [len(p[1]) for p in parts]
    n_out = [len(p[3]) for p in parts]
    n_scr = [len(p[5]) for p in parts]

    def kern(*refs):
        ins, outs, scr = refs[:sum(n_in)], refs[sum(n_in):sum(n_in) + sum(n_out)], refs[sum(n_in) + sum(n_out):]
        bodies = []
        for k, p in enumerate(parts):
            take = lambda seq, counts: seq[sum(counts[:k]):sum(counts[:k + 1])]
            bodies.append(p[0](*take(ins, n_in), *take(outs, n_out), *take(scr, n_scr)))
        _interleave(*bodies)

    res = pl.pallas_call(
        kern,
        out_shape=tuple(s for p in parts for s in p[3]),
        grid=grid,
        in_specs=[s for p in parts for s in p[1]],
        out_specs=tuple(s for p in parts for s in p[4]),
        scratch_shapes=[s for p in parts for s in p[5]],
        compiler_params=_cparams(("arbitrary", "arbitrary")),
        name=name,
    )(*[a for p in parts for a in p[2]])
    return [list(res[sum(n_out[:k]):sum(n_out[:k + 1])]) for k in range(len(parts))]


def _ssd_parts(cvs, sm3, dt_bias, a_log, d_e, n_ctx):
    nb, t, _ = cvs.shape
    tile = SSD_TILE
    nt = t // tile
    row = lambda v: jnp.pad(v.reshape(1, -1), ((0, 0), (0, LANES - 2 * SSD_HEADS)))
    const = lambda b, i: (0, 0)
    specs = lambda order: [
        pl.BlockSpec((1, tile, SSD_WIDTH), lambda b, i: (b, order(i), 0)),
        pl.BlockSpec((1, tile, 2 * SSD_BC), lambda b, i: (b, order(i), CV_BC // (2 * SSD_BC))),
        pl.BlockSpec((1, tile, LANES), lambda b, i: (b, order(i), 0)),
        pl.BlockSpec((1, LANES), const),
        pl.BlockSpec((1, LANES), const)]
    st_block = (1, 1, SSD_PAIRS, LANES, LANES)
    args = (cvs, cvs, sm3, row(dt_bias), row(a_log))
    sel = np.zeros((2, LANES, SSD_WIDTH), np.float32)
    for h in range(SSD_HEADS):
        sel[0, h, h * SSD_HEAD_DIM:(h + 1) * SSD_HEAD_DIM] = 1.0
        sel[1, SSD_B_OFF + h, h * SSD_HEAD_DIM:(h + 1) * SSD_HEAD_DIM] = 1.0
    e_f, e_b = jnp.asarray(sel[0], BF16), jnp.asarray(sel[1], BF16)
    e_spec = pl.BlockSpec((LANES, SSD_WIDTH), const)
    bwd = _tile_order(nt, True, n_ctx // tile)
    fwd = _tile_order(nt, False)
    scratch = [pltpu.VMEM(st_block[2:], F32)]
    states = (_ssd_states_body, specs(bwd) + [e_spec], list(args) + [e_b],
              [jax.ShapeDtypeStruct((nb, nt) + st_block[2:], BF16)],
              [pl.BlockSpec(st_block, lambda b, i: (b, bwd(i), 0, 0, 0))], scratch)
    out = lambda states_b: (
        _ssd_out_body,
        specs(fwd) + [e_spec, e_spec, pl.BlockSpec((1, SSD_WIDTH), const),
                      pl.BlockSpec(st_block, lambda b, i: (b, i, 0, 0, 0))],
        list(args) + [e_f, e_b, d_e, states_b],
        [jax.ShapeDtypeStruct((nb, t, SSD_WIDTH), F32)],
        [pl.BlockSpec((1, tile, SSD_WIDTH), lambda b, i: (b, i, 0))], scratch)
    return states, out


ML_GATE = 32
ML_ND = 2 * ML_HEADS
ML_AUG = ML_V_DIM + LANES


def _ml_gates(sm_ref, ib_ref, fb_ref, n, both):
    lane = lax.broadcasted_iota(jnp.int32, (1, LANES), 1)
    valid = (lane >= ML_GATE) & (lane < ML_GATE + ML_ND)
    is_f = lane < ML_GATE + ML_HEADS
    sm = sm_ref[0]
    li = sm + ib_ref[...]
    lf = pltpu.roll(_log_sigmoid(sm + fb_ref[...]), LANES - ML_ND, axis=1)
    parts = _split(jnp.where(valid, lf, 0.0), 3)
    upp = jnp.where(_causal(n, True), 1.0, 0.0).astype(BF16)
    bcum = sum(_dot(upp, p) for p in parts)
    if both:
        low = jnp.where(_causal(n, False), 1.0, 0.0).astype(BF16)
        bcum = jnp.where(is_f, sum(_dot(low, p) for p in parts), bcum)
    return valid, is_f, bcum, jnp.where(valid, li - bcum, 0.0)


def _ml_state_step(cn_ref, h, cn, keep, k, ws_dense, v_aug):
    w3 = jnp.concatenate([ws_dense.astype(BF16)] * (ML_AUG // LANES), axis=1)
    cn_ref[h] = keep * cn + _dot_tn(k, w3 * v_aug)


def _ml_v_aug(v_ref, h, n):
    return jnp.concatenate([v_ref[0, :, ML_V_DIM * h:ML_V_DIM * (h + 1)], jnp.ones((n, LANES), BF16)], axis=1)


def _ml_states_body(qk_ref, v_ref, sm_ref, ib_ref, fb_ref, sel_ref, cn_out, m_out, cn_ref, m_ref):
    n = qk_ref.shape[1]

    @pl.when(pl.program_id(1) == 0)
    def _init():
        cn_ref[...] = jnp.zeros_like(cn_ref)
        m_ref[...] = jnp.zeros_like(m_ref)

    cn_out[0, 0] = cn_ref[...].astype(cn_out.dtype)
    m_out[0, 0] = m_ref[...]
    yield
    valid, is_f, bcum, a = _ml_gates(sm_ref, ib_ref, fb_ref, n, False)
    m_prev = m_ref[0:1, :]
    g_last = jnp.maximum(m_prev, jnp.max(a, axis=0, keepdims=True))
    yield
    ws = _expand(jnp.exp(a - g_last), sel_ref[:, ML_HEADS * LANES:])
    keep = jnp.exp(m_prev - g_last)
    for h in range(ML_HEADS):
        lane_b = ML_GATE + ML_HEADS + h
        k = qk_ref[0, :, ML_QK_WIDTH + ML_QK_DIM * h:ML_QK_WIDTH + ML_QK_DIM * (h + 1)]
        _ml_state_step(cn_ref, h, cn_ref[h], keep[:, lane_b:lane_b + 1], k,
                       ws[:, LANES * h:LANES * (h + 1)], _ml_v_aug(v_ref, h, n))
        yield
    m_ref[...] = jnp.broadcast_to(bcum[0:1, :] + g_last, m_ref.shape)


def _ml_out_body(qk_ref, v_ref, sm_ref, ib_ref, fb_ref, sel_ref, cnb_ref, mb_ref, o_ref, cn_ref, m_ref):
    n = qk_ref.shape[1]

    @pl.when(pl.program_id(1) == 0)
    def _init():
        cn_ref[...] = jnp.zeros_like(cn_ref)
        m_ref[...] = jnp.zeros_like(m_ref)

    head_q = lambda h: qk_ref[0, :, ML_QK_DIM * h:ML_QK_DIM * (h + 1)]
    head_k = lambda h: qk_ref[0, :, ML_QK_WIDTH + ML_QK_DIM * h:ML_QK_WIDTH + ML_QK_DIM * (h + 1)]
    qk_all = [_dot_nt(head_q(h), head_k(h)) for h in range(ML_HEADS)]
    inter_all = [_dot(head_q(h), jnp.concatenate([cn_ref[h].astype(BF16), cnb_ref[0, 0, h]], axis=1))
                 for h in range(ML_HEADS)]
    yield
    valid, is_f, bcum, a = _ml_gates(sm_ref, ib_ref, fb_ref, n, True)
    m_prev = jnp.where(is_f, m_ref[0:1, :], mb_ref[0, 0, 0:1, :])
    a_t = a.T
    pre = suf = a_t[ML_GATE:ML_GATE + ML_ND, :]
    pos = lax.broadcasted_iota(jnp.int32, (ML_ND, n), 1)
    k = 1
    while k < n:
        pre = jnp.maximum(pre, jnp.where(pos >= k, pltpu.roll(pre, k, axis=1), -jnp.inf))
        suf = jnp.maximum(suf, jnp.where(pos < n - k, pltpu.roll(suf, n - k, axis=1), -jnp.inf))
        k *= 2
    run = jnp.where(lax.broadcasted_iota(jnp.int32, (ML_ND, n), 0) < ML_HEADS, pre, suf)
    run = jnp.concatenate([jnp.zeros((ML_GATE, n), F32), run,
                           jnp.zeros((LANES - ML_GATE - ML_ND, n), F32)], axis=0).T
    g = jnp.maximum(m_prev, run)
    m_t = bcum + g
    yield
    floor = jnp.exp(-m_t)
    dense = lambda x, lane_: jnp.broadcast_to(x[:, lane_:lane_ + 1], (n, LANES))
    diag_t = jnp.exp(a - g).T
    g_last = g[n - 1:n, :]
    ws = _expand(jnp.exp(a - g_last), sel_ref[:, :ML_HEADS * LANES])
    keep = jnp.exp(m_prev - g_last)
    ti = lax.broadcasted_iota(jnp.int32, (n, n), 0)
    si = lax.broadcasted_iota(jnp.int32, (n, n), 1)
    not_above, above, on_diag = si <= ti, si > ti, si == ti
    wide = lambda x: jnp.concatenate([x] * (n // LANES), axis=1)
    for h in range(ML_HEADS):
        lanes = (ML_GATE + h, ML_GATE + ML_HEADS + h)
        g_d = [dense(g, ln) for ln in lanes]
        qk, inter = qk_all[h], inter_all[h]
        e = jnp.where(not_above, a_t[lanes[0]:lanes[0] + 1, :] - wide(g_d[0]),
                      a_t[lanes[1]:lanes[1] + 1, :] - wide(g_d[1]))
        p = qk * jnp.exp(e)
        p_f = jnp.where(not_above, p, 0.0).astype(BF16)
        p_b = jnp.where(above, p, jnp.where(on_diag, qk * diag_t[lanes[1]:lanes[1] + 1, :], 0.0)).astype(BF16)
        intra = _dot(jnp.concatenate([p_f, p_b], axis=0), _ml_v_aug(v_ref, h, n))
        out = None
        for d in range(2):
            w_inter = jnp.exp(m_prev[:, lanes[d]:lanes[d] + 1] - g_d[d])
            s = (intra[n * d:n * (d + 1)] + jnp.concatenate([w_inter] * (ML_AUG // LANES), axis=1)
                 * inter[:, ML_AUG * d:ML_AUG * (d + 1)])
            rn = 1.0 / jnp.maximum(jnp.abs(s[:, ML_V_DIM:]), dense(floor, lanes[d]))
            hid = s[:, :ML_V_DIM] * jnp.concatenate([rn] * (ML_V_DIM // LANES), axis=1)
            out = hid if out is None else out + hid
        o_ref[0, :, ML_V_DIM * h:ML_V_DIM * (h + 1)] = out
        yield
    for h in range(ML_HEADS):
        lane_f = ML_GATE + h
        _ml_state_step(cn_ref, h, cn_ref[h], keep[:, lane_f:lane_f + 1], head_k(h),
                       ws[:, LANES * h:LANES * (h + 1)], _ml_v_aug(v_ref, h, n))
        yield
    m_ref[...] = jnp.broadcast_to(m_t[n - 1:n, :], m_ref.shape)


def _mlstm_parts(cvm, p3, sm3, i_bias, f_bias, n_ctx):
    nb, t, _ = cvm.shape
    nt = t // TILE
    assert _SM_OFF["i_f"] == ML_GATE and _SM_OFF["f_f"] == ML_GATE + ML_ND
    row = lambda v, off: jnp.pad(v.reshape(1, -1), ((0, 0), (off, LANES - off - ML_ND)))
    sel = np.zeros((LANES, ML_ND * LANES), np.float32)
    for r in range(ML_ND):
        sel[ML_GATE + r, r * LANES:(r + 1) * LANES] = 1.0
    const = lambda b, i: (0, 0)
    specs = lambda order: [
        pl.BlockSpec((1, TILE, 2 * ML_QK_WIDTH), lambda b, i: (b, order(i), CV_QK // (2 * ML_QK_WIDTH))),
        pl.BlockSpec((1, TILE, ML_V_WIDTH), lambda b, i: (b, order(i), _P_OFF["m_v"] // ML_V_WIDTH)),
        pl.BlockSpec((1, TILE, LANES), lambda b, i: (b, order(i), 0)),
        pl.BlockSpec((1, LANES), const),
        pl.BlockSpec((1, LANES), const),
        pl.BlockSpec((LANES, ML_ND * LANES), const)]
    args = (cvm, p3, sm3, row(i_bias, ML_GATE), row(f_bias, ML_GATE + ML_ND), jnp.asarray(sel, BF16))
    cn_block = (1, 1, ML_HEADS, ML_QK_DIM, ML_AUG)
    m_block = (1, 1, SUBLANES, LANES)
    scratch = [pltpu.VMEM(cn_block[2:], F32), pltpu.VMEM(m_block[2:], F32)]
    bwd = _tile_order(nt, True, n_ctx // TILE)
    fwd = _tile_order(nt, False)
    st_idx = lambda b, i: (b, bwd(i)) + (0,) * 3
    states = (_ml_states_body, specs(bwd), list(args),
              [jax.ShapeDtypeStruct((nb, nt) + cn_block[2:], BF16),
               jax.ShapeDtypeStruct((nb, nt) + m_block[2:], F32)],
              [pl.BlockSpec(cn_block, st_idx), pl.BlockSpec(m_block, lambda b, i: (b, bwd(i), 0, 0))],
              scratch)
    out = lambda cn_b, m_b: (
        _ml_out_body,
        specs(fwd) + [pl.BlockSpec(cn_block, lambda b, i: (b, i, 0, 0, 0)),
                      pl.BlockSpec(m_block, lambda b, i: (b, i, 0, 0))],
        list(args) + [cn_b, m_b],
        [jax.ShapeDtypeStruct((nb, t, ML_V_WIDTH), F32)],
        [pl.BlockSpec((1, TILE, ML_V_WIDTH), lambda b, i: (b, i, 0))], scratch)
    return states, out


GLA_SUB = 256
GLA_NCH = GLA_SUB // GLA_CHUNK
GLA_COLS = BF16_SUBLANES


def _gla_layout(is_ctx, lat_rows):
    r = np.arange(GLA_SUB)
    if is_ctx:
        return r // GLA_CHUNK, r % GLA_CHUNK
    col = r % SUBLANES
    cpc = GLA_CHUNK // lat_rows
    return col // cpc, (col % cpc) * lat_rows + r // SUBLANES


def _gla_consts(is_ctx, lat_rows, rev):
    ch, pos = _gla_layout(is_ctx, lat_rows)
    same = ch[:, None] == ch[None, :]
    before = (pos[None, :] >= pos[:, None]) if rev else (pos[None, :] <= pos[:, None])
    tri = (same & before).astype(np.float32)
    cmask = np.stack([np.repeat((ch == j)[:, None], LANES, axis=1) for j in range(GLA_NCH)])
    return tri, cmask.astype(np.float32)


def _gla_row(is_ctx, lat_rows, j, p):
    ch, pos = _gla_layout(is_ctx, lat_rows)
    return int(np.nonzero((ch == j) & (pos == p))[0][0])


def _per_chunk_rows(b, is_ctx, lat_rows, p):
    rows = [b[_gla_row(is_ctx, lat_rows, j, p):_gla_row(is_ctx, lat_rows, j, p) + 1, :]
            for j in range(GLA_NCH)]
    w = b.shape[1]
    if is_ctx:
        full = jnp.concatenate([jnp.broadcast_to(r, (GLA_CHUNK, w)) for r in rows], axis=0)
    else:
        rep = SUBLANES // GLA_NCH
        pat = jnp.concatenate([jnp.broadcast_to(r, (rep, w)) for r in rows], axis=0)
        full = jnp.broadcast_to(pat[None], (GLA_SUB // SUBLANES, SUBLANES, w)).reshape(GLA_SUB, w)
    return rows, full


def _gla_sub(q, k, v, araw, aup, abias, tri_b, tri_f, cmask_ref, st_ref, *, rev, is_ctx, lat_rows,
             want_out=True):
    g = _log_sigmoid(_dot_hp(araw, aup) + abias) * (1.0 / GLA_TAU)
    b = _dot_exact_lhs(tri_b, g, 2)
    lasts, last = _per_chunk_rows(b, is_ctx, lat_rows, 0 if rev else GLA_CHUNK - 1)
    kl = (k * jnp.exp(last - b)).astype(BF16)
    if want_out:
        _, ref = _per_chunk_rows(b, is_ctx, lat_rows, GLA_CHUNK // 2)
        qs = q * (GLA_K_DIM ** -0.5)
        qe = (qs * jnp.exp(b - ref)).astype(BF16)
        ke = (k * jnp.exp(ref - b)).astype(BF16)
        qb = (qs * jnp.exp(b)).astype(BF16)
        visible = tri_f > 0.0
    order = range(GLA_NCH - 1, -1, -1) if rev else range(GLA_NCH)
    outs = []
    for h in range(GLA_HEADS):
        ks = slice(GLA_K_DIM * h, GLA_K_DIM * (h + 1))
        vh = v[:, GLA_V_DIM * h:GLA_V_DIM * (h + 1)]
        klm = jnp.concatenate([kl[:, ks] * cmask_ref[j] for j in range(GLA_NCH)], axis=1)
        upd = _dot_tn(vh, klm)
        s = st_ref[h]
        s_in = [None] * GLA_NCH
        for j in order:
            s_in[j] = s.astype(BF16)
            s = s * jnp.exp(lasts[j][:, ks]) + upd[:, GLA_K_DIM * j:GLA_K_DIM * (j + 1)]
        st_ref[h] = s
        if want_out:
            att = jnp.where(visible, _dot_nt(qe[:, ks], ke[:, ks]), 0.0).astype(BF16)
            qbm = jnp.concatenate([qb[:, ks] * cmask_ref[j] for j in range(GLA_NCH)], axis=1)
            outs.append(_dot(att, vh) + _dot_nt(qbm, jnp.concatenate(s_in, axis=1)))
    return jnp.concatenate(outs, axis=1) if want_out else None


def _gla_kernel(qc_ref, kc_ref, vc_ref, sc_ref, ql_ref, kl_ref, vl_ref, sl_ref,
                aup_ref, ab_ref, tcb_ref, tcf_ref, cmc_ref, tlb_ref, tlf_ref, cml_ref,
                *rest, rev, ctx_rows, lat_rows, n_cblk, ctx_out):
    acc_ref, o_ref, st_ref, ctxo_ref = rest if len(rest) == 4 else (None,) + rest
    i = pl.program_id(1)
    aup = aup_ref[...]
    abias = ab_ref[...]
    r0, r1 = ctx_rows, ctx_rows + lat_rows
    sub = functools.partial(_gla_sub, aup=aup, abias=abias, st_ref=st_ref, rev=rev, lat_rows=lat_rows)

    @pl.when(i == 0)
    def _ctx():
        st_ref[...] = jnp.zeros_like(st_ref)
        o = sub(qc_ref[0].astype(F32), kc_ref[0].astype(F32), vc_ref[0], sc_ref[0], tri_b=tcb_ref[...],
                tri_f=tcf_ref[...], cmask_ref=cmc_ref, is_ctx=True, want_out=ctx_out)
        if ctx_out:
            ctxo_ref[...] = o

    @pl.when(i > 0)
    def _lat():
        cblk = (n_cblk - i) if rev else (i - 1)
        halves = range(GLA_COLS // SUBLANES)
        for half in (reversed(halves) if rev else halves):
            cs = slice(SUBLANES * half, SUBLANES * (half + 1))
            take = lambda r: r[0, r0:r1].astype(F32)[:, cs, :].reshape(GLA_SUB, r.shape[-1])
            o = sub(take(ql_ref), take(kl_ref), take(vl_ref).astype(BF16), take(sl_ref),
                    tri_b=tlb_ref[...], tri_f=tlf_ref[...], cmask_ref=cml_ref, is_ctx=False)
            _write(o_ref, (0, slice(r0, r1), cs, slice(None)),
                   o.reshape(lat_rows, SUBLANES, GLA_V_WIDTH), acc_ref)
        for r in range(ctx_rows):
            if ctx_out:
                start = pl.multiple_of(r * GRID_W + cblk * GLA_COLS, GLA_COLS)
                _write(o_ref, (0, r, slice(None), slice(None)), ctxo_ref[pl.ds(start, GLA_COLS), :], acc_ref)
            else:
                o_ref[0, r, :, :] = jnp.zeros((GLA_COLS, GLA_V_WIDTH), F32)


def _gla_scan(p3, sm3, a_up, a_bias, n_ctx, rev, acc, ctx_out):
    nb, t, ncol = p3.shape
    rows = t // GRID_W
    ctx_rows = n_ctx // GRID_W
    lat_rows = rows - ctx_rows
    n_cblk = GRID_W // GLA_COLS
    p4 = p3.reshape(nb, rows, GRID_W, ncol)
    sm4 = sm3.reshape(nb, rows, GRID_W, LANES)
    a_off = _SM_OFF["a_b"] if rev else _SM_OFF["a_f"]
    aup = jnp.pad(a_up, ((a_off, LANES - a_off - GLA_RANK), (0, 0)))
    cblk = lambda i: jnp.where(i == 0, n_cblk - 1 if rev else 0, (n_cblk - i) if rev else (i - 1))
    ctx = lambda blk: (lambda b, i: (b, 0, blk))
    lat = lambda blk: (lambda b, i: (b, 0, cblk(i), blk))
    const2 = lambda b, i: (0, 0)
    const3 = lambda b, i: (0, 0, 0)
    widths = (GLA_K_WIDTH, GLA_K_WIDTH, GLA_V_WIDTH)
    offs = (_P_OFF["g_q"], _P_OFF["g_k"], _P_OFF["g_v"])
    consts = []
    const_specs = []
    for is_ctx in (True, False):
        tri, cmask = _gla_consts(is_ctx, lat_rows, rev)
        consts += [jnp.asarray(tri, BF16), jnp.asarray(tri, F32), jnp.asarray(cmask, BF16)]
        const_specs += [pl.BlockSpec((GLA_SUB, GLA_SUB), const2), pl.BlockSpec((GLA_SUB, GLA_SUB), const2),
                        pl.BlockSpec((GLA_NCH, GLA_SUB, LANES), const3)]
    out_spec = pl.BlockSpec((1, rows, GLA_COLS, GLA_V_WIDTH), lat(0))
    in_specs = ([pl.BlockSpec((1, n_ctx, w), ctx(o // w)) for w, o in zip(widths, offs)]
                + [pl.BlockSpec((1, n_ctx, LANES), ctx(0))]
                + [pl.BlockSpec((1, rows, GLA_COLS, w), lat(o // w)) for w, o in zip(widths, offs)]
                + [pl.BlockSpec((1, rows, GLA_COLS, LANES), lat(0))]
                + [pl.BlockSpec((LANES, GLA_K_WIDTH), const2), pl.BlockSpec((1, GLA_K_WIDTH), const2)]
                + const_specs)
    args = [p3, p3, p3, sm3, p4, p4, p4, sm4, aup, a_bias.reshape(1, GLA_K_WIDTH)] + consts
    if acc is not None:
        in_specs.append(out_spec)
        args.append(acc.reshape(nb, rows, GRID_W, GLA_V_WIDTH))
    out = pl.pallas_call(
        functools.partial(_gla_kernel, rev=rev, ctx_rows=ctx_rows, lat_rows=lat_rows, n_cblk=n_cblk,
                          ctx_out=ctx_out),
        out_shape=jax.ShapeDtypeStruct((nb, rows, GRID_W, GLA_V_WIDTH), F32),
        grid=(nb, n_cblk + 1),
        in_specs=in_specs,
        out_specs=out_spec,
        scratch_shapes=[pltpu.VMEM((GLA_HEADS, GLA_V_DIM, GLA_K_DIM), F32),
                        pltpu.VMEM((n_ctx, GLA_V_WIDTH), F32)],
        input_output_aliases={} if acc is None else {len(args) - 1: 0},
        compiler_params=_cparams(("parallel", "arbitrary")),
        name="gla_scan",
    )(*args)
    return out.reshape(nb, t, GLA_V_WIDTH)


def _group_rmsnorm(y, groups):
    width = y.shape[-1] // groups
    ones = jnp.ones((width, LANES), BF16)
    out = []
    for g in range(groups):
        yg = y[:, width * g:width * (g + 1)]
        ms = _dot((yg * yg).astype(BF16), ones) * (1.0 / width)
        out.append(yg * jnp.concatenate([lax.rsqrt(ms + EPS)] * (width // LANES), axis=1))
    return jnp.concatenate(out, axis=1)


def _post_kernel(x_ref, y_ref, h_ref, o_ref_in, z_ref, mo_ref, gg_ref, gs_ref, gm_ref, gl_ref,
                 nws_ref, nwm_ref, nwg_ref, wbs_ref, wbm_ref, wbg_ref, wout_ref, g1_ref,
                 nwf_ref, sc2_ref, sh2_ref, xo_ref, ho_ref):
    y_ssd = (_group_rmsnorm(y_ref[0] * _silu(z_ref[0]), 2) * nws_ref[...]).astype(BF16)
    y_ml = (_group_rmsnorm(h_ref[0], ML_HEADS) * nwm_ref[...] * _sigmoid(mo_ref[0])).astype(BF16)
    y_gla = (_group_rmsnorm(o_ref_in[0], GLA_HEADS) * nwg_ref[...] * _silu(gg_ref[0])).astype(BF16)
    merged = (_sigmoid(gs_ref[0]) * _dot(y_ssd, wbs_ref[...])
              + _sigmoid(gm_ref[0]) * _dot(y_ml, wbm_ref[...])
              + _sigmoid(gl_ref[0]) * _dot(y_gla, wbg_ref[...]))
    x_new = x_ref[0] + g1_ref[0] * _dot(merged.astype(BF16), wout_ref[...])
    xo_ref[0] = x_new
    ho_ref[0] = (_rms(x_new) * nwf_ref[...] * (1.0 + sc2_ref[0]) + sh2_ref[0]).astype(ho_ref.dtype)


def _post(x, scans, p3, norm_ws, w_bs, w_out, layer, norm_ffn_w, mods, n_ctx, tile0):
    nb, t, d = x.shape
    nt = t // TILE - tile0
    ctx_tiles = n_ctx // TILE
    tok = lambda blk: (lambda b, i: (b, i + tile0, blk))
    out = lambda b, i: (b, i, 0)
    const = lambda b, i: (0, 0)
    tok_spec = lambda blk: pl.BlockSpec((1, TILE, d), tok(blk))
    w_spec = pl.BlockSpec((None, d, d), lambda b, i: (layer, 0, 0), pipeline_mode=pl.Buffered(1))
    vec = pl.BlockSpec((1, d), const)
    mod = lambda k: pl.BlockSpec((1, 1, d), _mod_row(nb, k, ctx_tiles, tile0))
    names = ("s_z", "m_o", "g_g", "gate_ssd", "gate_ml", "gate_gla")
    in_specs = ([tok_spec(0)] * 4 + [tok_spec(_P_OFF[nm] // d) for nm in names]
                + [vec] * 3 + [w_spec] * 4 + [mod(2), vec, mod(4), mod(3)])
    return pl.pallas_call(
        _post_kernel,
        out_shape=(jax.ShapeDtypeStruct((nb, nt * TILE, d), F32),
                   jax.ShapeDtypeStruct((nb, nt * TILE, d), BF16)),
        grid=(nb, nt),
        in_specs=in_specs,
        out_specs=(pl.BlockSpec((1, TILE, d), out), pl.BlockSpec((1, TILE, d), out)),
        compiler_params=_cparams(("parallel", "parallel")),
        name="post",
    )(x, *scans, *([p3] * 6), *[w.reshape(1, d) for w in norm_ws], *w_bs, w_out, mods,
      norm_ffn_w.reshape(1, d), mods, mods)


def _ffn_in_kernel(a_ref, w_ref, o_ref):
    acc = _dot(a_ref[...], w_ref[...])
    half = acc.shape[1] // 2
    o_ref[...] = (_silu(acc[:, :half]) * acc[:, half:]).astype(o_ref.dtype)


def _ffn_in(h, w_gu, layer, half):
    m, k = h.shape
    n_half = w_gu.shape[-1] // 2
    tm = _row_tile(m)
    return pl.pallas_call(
        _ffn_in_kernel,
        out_shape=jax.ShapeDtypeStruct((m, n_half), BF16),
        grid=(n_half // half, m // tm),
        in_specs=[pl.BlockSpec((tm, k), lambda j, i: (i, 0)),
                  pl.BlockSpec((None, k, 2 * half), lambda j, i: (layer, 0, j))],
        out_specs=pl.BlockSpec((tm, half), lambda j, i: (i, j)),
        compiler_params=_cparams(("parallel", "parallel")),
        name="ffn_in",
    )(h, w_gu)


def _ffn_out_kernel(a_ref, w_ref, x_ref, g_ref, nw_ref, sc_ref, sh_ref, xo_ref, ho_ref):
    x_new = x_ref[0] + g_ref[0] * _dot(a_ref[0], w_ref[...])
    xo_ref[0] = x_new
    ho_ref[0] = (_rms(x_new) * nw_ref[...] * (1.0 + sc_ref[0]) + sh_ref[0]).astype(ho_ref.dtype)


def _ffn_out_last_kernel(a_ref, w_ref, x_ref, g_ref, nw_ref, o_ref):
    x_new = x_ref[0] + g_ref[0] * _dot(a_ref[0], w_ref[...])
    o_ref[0] = _rms(x_new) * nw_ref[...]


def _ffn_out(a, w, layer, x, mods, n_ctx, tile0, next_norm_w, next_mods):
    nb, t, d = x.shape
    k = a.shape[-1]
    ctx_tiles = n_ctx // TILE
    tok = lambda b, i: (b, i, 0)
    mod = lambda k_: pl.BlockSpec((1, 1, d), _mod_row(nb, k_, ctx_tiles, tile0))
    in_specs = [pl.BlockSpec((1, TILE, k), tok),
                pl.BlockSpec((None, k, d), lambda b, i: (layer, 0, 0), pipeline_mode=pl.Buffered(1)),
                pl.BlockSpec((1, TILE, d), tok),
                mod(5),
                pl.BlockSpec((1, d), lambda b, i: (0, 0))]
    args = [a, w, x, mods, next_norm_w.reshape(1, d)]
    if next_mods is None:
        body = _ffn_out_last_kernel
        out_shape = jax.ShapeDtypeStruct((nb, t, d), F32)
        out_specs = pl.BlockSpec((1, TILE, d), tok)
    else:
        body = _ffn_out_kernel
        in_specs += [mod(1), mod(0)]
        args += [next_mods, next_mods]
        out_shape = (jax.ShapeDtypeStruct((nb, t, d), F32), jax.ShapeDtypeStruct((nb, t, d), BF16))
        out_specs = (pl.BlockSpec((1, TILE, d), tok), pl.BlockSpec((1, TILE, d), tok))
    return pl.pallas_call(
        body,
        out_shape=out_shape,
        grid=(nb, t // TILE),
        in_specs=in_specs,
        out_specs=out_specs,
        compiler_params=_cparams(("parallel", "parallel")),
        name="ffn_out",
    )(*args)


def _proj_weights(w_in):
    cols = lambda names: [w_in[..., _IN_OFF[nm]:_IN_OFF[nm] + _IN_W[nm]] for nm in names]
    main = jnp.concatenate(cols(_P_ORDER), axis=-1)
    small = jnp.pad(jnp.concatenate(cols(_SMALL), axis=-1), ((0, 0), (0, 0), (0, LANES - N_SMALL_USED)))
    return main.astype(BF16), small.astype(BF16)


def _ffn_weight(w_ffn_in, half):
    d_ff = w_ffn_in.shape[-1] // 2
    cols = []
    for j in range(d_ff // half):
        cols.append(w_ffn_in[..., j * half:(j + 1) * half])
        cols.append(w_ffn_in[..., d_ff + j * half:d_ff + (j + 1) * half])
    return jnp.concatenate(cols, axis=-1).astype(BF16)


def kernel(x, c, ctx, c_ctx, w_mod, b_mod, norm_mix_w, norm_ffn_w, w_in, ssd_conv_w, ssd_conv_b, ssd_dt_bias, ssd_a_log, ssd_d, ssd_norm_w, ml_conv_w, ml_conv_b, ml_i_bias, ml_f_bias, ml_norm_w, gla_a_up, gla_a_bias, gla_norm_w, w_b_ssd, w_b_ml, w_b_gla, w_out, w_ffn_in, w_ffn_out, final_norm_w):
    nb, n_lat, d = x.shape
    n_ctx = ctx.shape[1]
    t = n_ctx + n_lat
    depth = w_in.shape[0]
    d_ff = w_ffn_out.shape[1]
    assert n_ctx == TILE == GLA_SUB and n_lat % TILE == 0 and n_lat // GRID_W == 32
    ffn_half = d_ff // 2

    xs = jnp.concatenate([ctx, x], axis=1)
    c16 = jnp.pad(jnp.concatenate([c, c_ctx[None]], axis=0), ((0, 2 * SUBLANES - nb - 1), (0, 0)))
    mods = [_modulation(c16, w_mod, b_mod, l).reshape(2 * SUBLANES * 6, 1, d) for l in range(depth)]
    h = _norm_mod(xs, norm_mix_w[0], mods[0], 0, 1, n_ctx)
    conv_w = jnp.concatenate([ssd_conv_w, ml_conv_w], axis=-1)
    conv_b = jnp.concatenate([ssd_conv_b, ml_conv_b], axis=-1)
    conv_post = jnp.concatenate([jnp.ones((ssd_conv_w.shape[-1] + ML_QK_WIDTH,), F32),
                                 jnp.full((ML_QK_WIDTH,), ML_QK_DIM ** -0.5, F32)])
    w_main, w_small = _proj_weights(w_in)
    w_gu = _ffn_weight(w_ffn_in, ffn_half)
    w_bs = [w.astype(BF16) for w in (w_b_ssd, w_b_ml, w_b_gla)]
    w_out_b = w_out.astype(BF16)
    w_ffn_out_b = w_ffn_out.astype(BF16)
    for l in range(depth):
        last = l == depth - 1
        h2d = h.reshape(nb * t, d)
        tn = N_PROJ // 5
        conv_tile = _P_OFF["s_x"] // tn
        assert conv_tile * tn == _P_OFF["s_x"] and N_PROJ - _P_OFF["s_x"] == tn == CV_WIDTH
        m_tiles = nb * t // _row_tile(nb * t)
        (pc,), (sm,) = _run_together("proj_first", (1, m_tiles),
                                     _mm_part(h2d, w_main, l, tn, BF16, conv_tile, 1),
                                     _mm_part(h2d, w_small, l, LANES, F32, 0, 1))
        pc3, sm3 = pc.reshape(nb, t, tn), sm.reshape(nb, t, LANES)
        grid = (conv_tile, m_tiles)
        (p,), (cv,) = _run_together(
            "proj_conv", grid, _mm_part(h2d, w_main, l, tn, BF16, 0, conv_tile),
            _conv_part(pc3, conv_w[l], conv_b[l], conv_post, n_ctx, grid))
        p3 = p.reshape(nb, t, conv_tile * tn)
        d_e = jnp.repeat(ssd_d[l], SSD_HEAD_DIM).reshape(1, SSD_WIDTH)
        ssd_states, ssd_out = _ssd_parts(cv, sm3, ssd_dt_bias[l], ssd_a_log[l], d_e, n_ctx)
        ml_states, ml_out = _mlstm_parts(cv, p3, sm3, ml_i_bias[l], ml_f_bias[l], n_ctx)
        grid = (nb, t // TILE)
        ssd_st, ml_st = _run_together("bwd_states", grid, ssd_states, ml_states)
        (y,), (hm,) = _run_together("ssd_mlstm", grid, ssd_out(*ssd_st), ml_out(*ml_st))
        og = None
        for rev in (False, True):
            k = int(rev)
            og = _gla_scan(p3, sm3, gla_a_up[l, k], gla_a_bias[l, k], n_ctx, rev, og, not last)
        tile0 = n_ctx // TILE if last else 0
        xs, h2 = _post(xs, (y, hm, og), p3, (ssd_norm_w[l], ml_norm_w[l], gla_norm_w[l]),
                       w_bs, w_out_b, l, norm_ffn_w[l], mods[l], n_ctx, tile0)
        nt = xs.shape[1]
        a = _ffn_in(h2.reshape(nb * nt, d), w_gu, l, ffn_half)
        a = a.reshape(nb, nt, d_ff)
        if last:
            return _ffn_out(a, w_ffn_out_b, l, xs, mods[l], n_ctx, tile0, final_norm_w, None)
        xs, h = _ffn_out(a, w_ffn_out_b, l, xs, mods[l], n_ctx, tile0,
                         norm_mix_w[l + 1], mods[l + 1])
```

```python
import functools

import numpy as np
import jax
import jax.numpy as jnp
from jax import lax
from jax.experimental import pallas as pl
from jax.experimental.pallas import tpu as pltpu

F32 = jnp.float32
BF16 = jnp.bfloat16

EPS = 1e-6
LOG2E = 1.4426950408889634
GRID_W = 64
SSD_HEADS = 16
SSD_HEAD_DIM = 64
SSD_WIDTH = 1024
SSD_STATE = 64
SSD_BC = 128
ML_HEADS = 4
ML_QK_DIM = 128
ML_V_DIM = 256
ML_QK_WIDTH = 512
ML_V_WIDTH = 1024
GLA_HEADS = 4
GLA_K_DIM = 128
GLA_V_DIM = 256
GLA_K_WIDTH = 512
GLA_V_WIDTH = 1024
GLA_RANK = 16
GLA_TAU = 16.0
GLA_CHUNK = 64

LANES = 128
SUBLANES = 8
BF16_SUBLANES = 16
VMEM_LIMIT = 56 * 1024 * 1024

TILE = 256

_IN_NAMES = ("s_x", "s_z", "s_b", "s_c", "dt_f", "dt_b",
             "m_q", "m_k", "m_v", "m_o", "i_f", "i_b", "f_f", "f_b",
             "g_q", "g_k", "g_v", "g_g", "a_f", "a_b",
             "gate_ssd", "gate_ml", "gate_gla")
_IN_WIDTHS = (1024, 1024, 128, 128, 16, 16,
              512, 512, 1024, 1024, 4, 4, 4, 4,
              512, 512, 1024, 1024, 16, 16,
              1024, 1024, 1024)
_IN_OFF = dict(zip(_IN_NAMES, np.concatenate([[0], np.cumsum(_IN_WIDTHS)[:-1]]).tolist()))
_IN_W = dict(zip(_IN_NAMES, _IN_WIDTHS))

_P_ORDER = ("s_z", "m_o", "g_g", "gate_ssd", "gate_ml", "gate_gla", "m_v", "g_v",
            "g_q", "g_k", "s_x", "s_b", "s_c", "m_q", "m_k")
_P_OFF = {}
_o = 0
for _n in _P_ORDER:
    _P_OFF[_n] = _o
    _o += _IN_W[_n]
N_PROJ = _o
_SMALL = ("dt_f", "dt_b", "i_f", "i_b", "f_f", "f_b", "a_f", "a_b")
_SM_OFF = {}
_s = 0
for _n in _SMALL:
    _SM_OFF[_n] = _s
    _s += _IN_W[_n]
N_SMALL_USED = _s


def _cparams(sem):
    return pltpu.CompilerParams(dimension_semantics=sem, vmem_limit_bytes=VMEM_LIMIT)


def _sigmoid(x):
    return 0.5 * jnp.tanh(0.5 * x) + 0.5


def _silu(x):
    return x * _sigmoid(x)


def _softplus(x):
    return jnp.maximum(x, 0.0) + jnp.log1p(jnp.exp(-jnp.abs(x)))


def _log_sigmoid(x):
    return jnp.minimum(x, 0.0) - jnp.log(1.0 + jnp.exp(-jnp.abs(x)))


def _split(x, n):
    out = []
    r = x
    for _ in range(n):
        p = r.astype(BF16)
        out.append(p)
        r = r - p.astype(F32)
    return out


def _dot(a, b):
    return jnp.dot(a, b, preferred_element_type=F32)


def _dot_nt(a, b):
    return lax.dot_general(a, b, (((1,), (1,)), ((), ())), preferred_element_type=F32)


def _dot_tn(a, b):
    return lax.dot_general(a, b, (((0,), (0,)), ((), ())), preferred_element_type=F32)


def _dot_exact_lhs(t, x, pieces):
    return sum(_dot(t, p) for p in _split(x, pieces))


def _dot_hp(a, b):
    ah, am = _split(a, 2)
    bh, bm = _split(b, 2)
    return _dot(ah, bh) + _dot(ah, bm) + _dot(am, bh)


def _causal(n, rev):
    t = lax.broadcasted_iota(jnp.int32, (n, n), 0)
    s = lax.broadcasted_iota(jnp.int32, (n, n), 1)
    return (s >= t) if rev else (s <= t)


def _rms(x):
    return x * lax.rsqrt(jnp.mean(x * x, axis=-1, keepdims=True) + EPS)


def _mod_kernel(c_ref, w_ref, b_ref, o_ref):
    o_ref[...] = _dot_hp(_silu(c_ref[...]), w_ref[...]) + b_ref[...]


def _modulation(c16, w_mod, b_mod, layer):
    rows, d = c16.shape
    n = w_mod.shape[-1]
    tn = 1536
    return pl.pallas_call(
        _mod_kernel,
        out_shape=jax.ShapeDtypeStruct((rows, n), F32),
        grid=(n // tn,),
        in_specs=[pl.BlockSpec((rows, d), lambda j: (0, 0)),
                  pl.BlockSpec((None, d, tn), lambda j: (layer, 0, j)),
                  pl.BlockSpec((None, 1, tn), lambda j: (layer, 0, j))],
        out_specs=pl.BlockSpec((rows, tn), lambda j: (0, j)),
        compiler_params=_cparams(("arbitrary",)),
        name="modulation",
    )(c16, w_mod, b_mod.reshape(b_mod.shape[0], 1, n))


def _mod_row(nb, k, ctx_tiles, tile0=0):
    return lambda b, i: (jnp.where(i + tile0 < ctx_tiles, nb, b) * 6 + k, 0, 0)


def _norm_mod_kernel(x_ref, w_ref, sc_ref, sh_ref, o_ref):
    o_ref[0] = (_rms(x_ref[0]) * w_ref[...] * (1.0 + sc_ref[0]) + sh_ref[0]).astype(o_ref.dtype)


def _norm_mod(x, w, mods, k_shift, k_scale, n_ctx):
    nb, t, d = x.shape
    return pl.pallas_call(
        _norm_mod_kernel,
        out_shape=jax.ShapeDtypeStruct((nb, t, d), BF16),
        grid=(nb, t // TILE),
        in_specs=[pl.BlockSpec((1, TILE, d), lambda b, i: (b, i, 0)),
                  pl.BlockSpec((1, d), lambda b, i: (0, 0)),
                  pl.BlockSpec((1, 1, d), _mod_row(nb, k_scale, n_ctx // TILE)),
                  pl.BlockSpec((1, 1, d), _mod_row(nb, k_shift, n_ctx // TILE))],
        out_specs=pl.BlockSpec((1, TILE, d), lambda b, i: (b, i, 0)),
        compiler_params=_cparams(("parallel", "parallel")),
        name="norm_mod",
    )(x, w.reshape(1, d), mods, mods)


MM_STAGE = 256


def _mm_body(a_ref, w_ref, o_ref):
    a = a_ref[...]
    width = o_ref.shape[1]
    stage = min(MM_STAGE, width)
    for c0 in range(0, width, stage):
        o_ref[:, c0:c0 + stage] = _dot(a, w_ref[:, c0:c0 + stage]).astype(o_ref.dtype)
        yield


def _row_tile(m):
    return 512 if m % 512 == 0 else TILE


def _mm_part(a, w, layer, tn, out_dtype, tile0, n_tiles):
    m, k = a.shape
    tm = _row_tile(m)
    return (_mm_body,
            [pl.BlockSpec((tm, k), lambda j, i: (i, 0)),
             pl.BlockSpec((None, k, tn), lambda j, i: (layer, 0, j + tile0))],
            [a, w], [jax.ShapeDtypeStruct((m, n_tiles * tn), out_dtype)],
            [pl.BlockSpec((tm, tn), lambda j, i: (i, j))], [])


CONV_K = 5
CONV_ROWS = 128
CONV_X = SSD_WIDTH // LANES
CONV_BC = 2 * SSD_BC // LANES
CONV_QK = 2 * ML_QK_WIDTH // LANES
CV_QK = SSD_WIDTH
CV_BC = SSD_WIDTH + 2 * ML_QK_WIDTH
CV_WIDTH = CV_BC + 2 * SSD_BC


def _conv_body(u_ref, w_ref, b_ref, s_ref, o_ref, pad_ref, *, n_ctx):
    t, c = u_ref.shape[1], u_ref.shape[2]
    half = CONV_K // 2
    zeros = jnp.zeros((SUBLANES, c), F32)
    w = w_ref[...]
    bias = b_ref[...]
    post = s_ref[...]
    for s0, n in ((0, n_ctx), (n_ctx, t - n_ctx)):
        pad_ref[0:SUBLANES, :] = zeros
        pad_ref[SUBLANES:SUBLANES + n, :] = u_ref[0, s0:s0 + n, :].astype(F32)
        pad_ref[SUBLANES + n:2 * SUBLANES + n, :] = zeros
        for r0 in range(0, n, CONV_ROWS):
            acc = bias
            for j in range(CONV_K):
                lo = SUBLANES - half + j + r0
                acc = acc + w[j:j + 1, :] * pad_ref[lo:lo + CONV_ROWS, :]
            o_ref[0, s0 + r0:s0 + r0 + CONV_ROWS, :] = (_silu(acc) * post).astype(o_ref.dtype)
            yield


def _conv_part(u3, w, b, post_scale, n_ctx, grid):
    nb, t, width = u3.shape
    n_cb = width // LANES
    steps = grid[0] * grid[1]
    rep = steps // (nb * n_cb)
    assert steps == rep * nb * n_cb and n_cb == CONV_X + CONV_BC + CONV_QK
    blk = lambda j, i: (j * grid[1] + i) // rep
    chan = lambda j, i: blk(j, i) % n_cb
    out_chan = lambda c: jnp.where(c < CONV_X, c, jnp.where(c < CONV_X + CONV_BC, c + CONV_QK, c - CONV_BC))
    vec = lambda rows: pl.BlockSpec((rows, LANES), lambda j, i: (0, chan(j, i)))
    return (functools.partial(_conv_body, n_ctx=n_ctx),
            [pl.BlockSpec((1, t, LANES), lambda j, i: (blk(j, i) // n_cb, 0, chan(j, i))),
             vec(CONV_K), vec(1), vec(1)],
            [u3, w, b.reshape(1, width), post_scale.reshape(1, width)],
            [jax.ShapeDtypeStruct((nb, t, width), BF16)],
            [pl.BlockSpec((1, t, LANES), lambda j, i: (blk(j, i) // n_cb, 0, out_chan(chan(j, i))))],
            [pltpu.VMEM((t + 2 * SUBLANES, LANES), F32)])


def _tile_order(n_tiles, rev, ctx_tiles=1):
    if rev:
        return lambda i: jnp.where(i < ctx_tiles, ctx_tiles - 1 - i, n_tiles - 1 + ctx_tiles - i)
    return lambda i: i


SSD_PAIRS = SSD_HEADS // 2
SSD_B_OFF = SSD_HEADS
SSD_TILE = 256


def _ssd_gates(sm_ref, dtb_ref, alog_ref, n, both):
    lane = lax.broadcasted_iota(jnp.int32, (1, LANES), 1)
    dt = _softplus(sm_ref[0] + dtb_ref[...])
    la = dt * jnp.where(lane < 2 * SSD_HEADS, -jnp.exp(alog_ref[...]), 0.0)
    parts = _split(la, 3)
    upp = jnp.where(_causal(n, True), 1.0, 0.0).astype(BF16)
    cum = sum(_dot(upp, p) for p in parts)
    if both:
        low = jnp.where(_causal(n, False), 1.0, 0.0).astype(BF16)
        cum = jnp.where(lane < SSD_HEADS, sum(_dot(low, p) for p in parts), cum)
    return lane, dt, cum


def _expand(a, e):
    return sum(_dot(p, e) for p in _split(a, 2))


def _group_dup(v, g, lo):
    other = pltpu.roll(v, SSD_STATE, axis=1)
    return jnp.where(lo, v, other) if g == 0 else jnp.where(lo, other, v)


def _ssd_state_step(st_ref, j, bw, xp, elast, off):
    r = lax.broadcasted_iota(jnp.int32, (LANES, LANES), 0) < SSD_STATE
    c = lax.broadcasted_iota(jnp.int32, (LANES, LANES), 1) < SSD_HEAD_DIM
    dec = jnp.where(r, elast[:, off + 2 * j:off + 2 * j + 1], elast[:, off + 2 * j + 1:off + 2 * j + 2])
    st_ref[j] = jnp.where(r == c, dec * st_ref[j] + _dot_tn(bw.astype(BF16), xp), 0.0)


def _interleave(*bodies):
    live = list(bodies)
    while live:
        for body in list(live):
            if next(body, StopIteration) is StopIteration:
                live.remove(body)


def _ssd_states_body(x_ref, bc_ref, sm_ref, dtb_ref, alog_ref, eb_ref, o_ref, st_ref):
    n = x_ref.shape[1]

    @pl.when(pl.program_id(1) == 0)
    def _init():
        st_ref[...] = jnp.zeros_like(st_ref)

    o_ref[0, 0] = st_ref[...].astype(o_ref.dtype)
    yield
    lane, dt, cum = _ssd_gates(sm_ref, dtb_ref, alog_ref, n, False)
    lo = lane < SSD_HEAD_DIM
    last = cum[0:1, :]
    yield
    wst = _expand(jnp.exp(last - cum) * dt, eb_ref[...])
    elast = jnp.exp(last)
    b128 = bc_ref[0, :, :SSD_BC].astype(F32)
    for j in range(SSD_PAIRS):
        sl = slice(LANES * j, LANES * (j + 1))
        bw = _group_dup(b128, j // (SSD_PAIRS // 2), lo) * wst[:, sl]
        _ssd_state_step(st_ref, j, bw, x_ref[0, :, sl], elast, SSD_B_OFF)
        if j % 2:
            yield


def _ssd_out_body(x_ref, bc_ref, sm_ref, dtb_ref, alog_ref, ef_ref, eb_ref, d_ref, sb_ref, o_ref, st_ref):
    n = x_ref.shape[1]

    @pl.when(pl.program_id(1) == 0)
    def _init():
        st_ref[...] = jnp.zeros_like(st_ref)

    zero_b = jnp.zeros((), BF16)
    lo = lax.broadcasted_iota(jnp.int32, (1, LANES), 1) < SSD_HEAD_DIM
    b128_b = bc_ref[0, :, :SSD_BC]
    c128_b = bc_ref[0, :, SSD_BC:]
    cb_all = [_dot_nt(jnp.where(lo if g == 0 else jnp.logical_not(lo), c128_b, zero_b), b128_b)
              for g in range(2)]
    pair_x = lambda j: x_ref[0, :, LANES * j:LANES * (j + 1)]
    rhs_all = [jnp.concatenate([jnp.where(lo, pair_x(j), zero_b), jnp.where(lo, zero_b, pair_x(j)),
                                st_ref[j].astype(BF16), sb_ref[0, 0, j]], axis=0) for j in range(SSD_PAIRS)]
    yield
    lane, dt, cum = _ssd_gates(sm_ref, dtb_ref, alog_ref, n, True)
    is_f = lane < SSD_HEADS
    ldt = jnp.log(dt)
    dsum = jnp.log(dt + pltpu.roll(dt, LANES - SSD_B_OFF, axis=1))
    rt = (jnp.where(lane < 2 * SSD_HEADS, cum - ldt, pltpu.roll(dsum, 2 * SSD_HEADS, axis=1)) * LOG2E).T
    cum2 = cum * LOG2E
    yield
    last = jnp.where(is_f, cum[n - 1:n, :], cum[0:1, :])
    elast = jnp.exp(last)
    ecum = _split(jnp.exp(cum), 2)
    ecum_f = sum(_dot(p, ef_ref[...]) for p in ecum)
    ecum_b = sum(_dot(p, eb_ref[...]) for p in ecum)
    wst = _expand(jnp.exp(last - cum) * dt, ef_ref[...])
    bc = bc_ref[0].astype(F32)
    b128, c128 = bc[:, :SSD_BC], bc[:, SSD_BC:]
    ti = lax.broadcasted_iota(jnp.int32, (n, n), 0)
    si = lax.broadcasted_iota(jnp.int32, (n, n), 1)
    below, above = si < ti, si > ti
    half = SSD_PAIRS // 2
    yield
    for g in range(2):
        cb = cb_all[g]
        cdup = _group_dup(c128, g, lo)
        for j in range(g * half, (g + 1) * half):
            ms = []
            for h in (2 * j, 2 * j + 1):
                e_f = cum2[:, h:h + 1] - rt[h:h + 1, :]
                e_b = cum2[:, SSD_B_OFF + h:SSD_B_OFF + h + 1] - rt[SSD_B_OFF + h:SSD_B_OFF + h + 1, :]
                e = jnp.where(below, e_f, jnp.where(above, e_b, rt[2 * SSD_HEADS + h:2 * SSD_HEADS + h + 1, :]))
                ms.append((cb * jnp.exp2(e)).astype(BF16))
            sl = slice(LANES * j, LANES * (j + 1))
            cs_f = (cdup * ecum_f[:, sl]).astype(BF16)
            cs_b = (cdup * ecum_b[:, sl]).astype(BF16)
            lhs = jnp.concatenate(ms + [cs_f, cs_b], axis=1)
            o_ref[0, :, sl] = _dot(lhs, rhs_all[j]) + d_ref[:, sl] * pair_x(j).astype(F32)
            yield
    for j in range(SSD_PAIRS):
        sl = slice(LANES * j, LANES * (j + 1))
        _ssd_state_step(st_ref, j, _group_dup(b128, j // half, lo) * wst[:, sl], x_ref[0, :, sl], elast, 0)
        if j % 2:
            yield


def _run_together(name, grid, *parts, phases=None):
    n_in = [len(p[1]) for p in parts]
    n_out = [len(p[3]) for p in parts]
    n_scr = [len(p[5]) for p in parts]

    def kern(*refs):
        ins, outs, scr = refs[:sum(n_in)], refs[sum(n_in):sum(n_in) + sum(n_out)], refs[sum(n_in) + sum(n_out):]

        def run(key):
            bodies = []
            for k, p in enumerate(parts):
                take = lambda seq, counts: seq[sum(counts[:k]):sum(counts[:k + 1])]
                body = p[0][key] if isinstance(p[0], dict) else p[0]
                bodies.append(body(*take(ins, n_in), *take(outs, n_out), *take(scr, n_scr)))
            _interleave(*bodies)

        if phases is None:
            run(None)
        else:
            for pred, key in phases:
                pl.when(pred(pl.program_id(1)))(functools.partial(run, key))

    res = pl.pallas_call(
        kern,
        out_shape=tuple(s for p in parts for s in p[3]),
        grid=grid,
        in_specs=[s for p in parts for s in p[1]],
        out_specs=tuple(s for p in parts for s in p[4]),
        scratch_shapes=[s for p in parts for s in p[5]],
        compiler_params=_cparams(("arbitrary", "arbitrary")),
        name=name,
    )(*[a for p in parts for a in p[2]])
    return [list(res[sum(n_out[:k]):sum(n_out[:k + 1])]) for k in range(len(parts))]


def _ssd_parts(cvs, sm3, dt_bias, a_log, d_e, n_ctx):
    nb, t, _ = cvs.shape
    tile = SSD_TILE
    nt = t // tile
    row = lambda v: jnp.pad(v.reshape(1, -1), ((0, 0), (0, LANES - 2 * SSD_HEADS)))
    const = lambda b, i: (0, 0)
    specs = lambda order: [
        pl.BlockSpec((1, tile, SSD_WIDTH), lambda b, i: (b, order(i), 0)),
        pl.BlockSpec((1, tile, 2 * SSD_BC), lambda b, i: (b, order(i), CV_BC // (2 * SSD_BC))),
        pl.BlockSpec((1, tile, LANES), lambda b, i: (b, order(i), 0)),
        pl.BlockSpec((1, LANES), const),
        pl.BlockSpec((1, LANES), const)]
    st_block = (1, 1, SSD_PAIRS, LANES, LANES)
    args = (cvs, cvs, sm3, row(dt_bias), row(a_log))
    sel = np.zeros((2, LANES, SSD_WIDTH), np.float32)
    for h in range(SSD_HEADS):
        sel[0, h, h * SSD_HEAD_DIM:(h + 1) * SSD_HEAD_DIM] = 1.0
        sel[1, SSD_B_OFF + h, h * SSD_HEAD_DIM:(h + 1) * SSD_HEAD_DIM] = 1.0
    e_f, e_b = jnp.asarray(sel[0], BF16), jnp.asarray(sel[1], BF16)
    e_spec = pl.BlockSpec((LANES, SSD_WIDTH), const)
    bwd = _tile_order(nt, True, n_ctx // tile)
    fwd = _tile_order(nt, False)
    scratch = [pltpu.VMEM(st_block[2:], F32)]
    states = (_ssd_states_body, specs(bwd) + [e_spec], list(args) + [e_b],
              [jax.ShapeDtypeStruct((nb, nt) + st_block[2:], BF16)],
              [pl.BlockSpec(st_block, lambda b, i: (b, bwd(i), 0, 0, 0))], scratch)
    out = lambda states_b: (
        _ssd_out_body,
        specs(fwd) + [e_spec, e_spec, pl.BlockSpec((1, SSD_WIDTH), const),
                      pl.BlockSpec(st_block, lambda b, i: (b, i, 0, 0, 0))],
        list(args) + [e_f, e_b, d_e, states_b],
        [jax.ShapeDtypeStruct((nb, t, SSD_WIDTH), F32)],
        [pl.BlockSpec((1, tile, SSD_WIDTH), lambda b, i: (b, i, 0))], scratch)
    return states, out


ML_GATE = 32
ML_ND = 2 * ML_HEADS
ML_AUG = ML_V_DIM + LANES


def _ml_gates(sm_ref, ib_ref, fb_ref, n, both):
    lane = lax.broadcasted_iota(jnp.int32, (1, LANES), 1)
    valid = (lane >= ML_GATE) & (lane < ML_GATE + ML_ND)
    is_f = lane < ML_GATE + ML_HEADS
    sm = sm_ref[0]
    li = sm + ib_ref[...]
    lf = pltpu.roll(_log_sigmoid(sm + fb_ref[...]), LANES - ML_ND, axis=1)
    parts = _split(jnp.where(valid, lf, 0.0), 3)
    upp = jnp.where(_causal(n, True), 1.0, 0.0).astype(BF16)
    bcum = sum(_dot(upp, p) for p in parts)
    if both:
        low = jnp.where(_causal(n, False), 1.0, 0.0).astype(BF16)
        bcum = jnp.where(is_f, sum(_dot(low, p) for p in parts), bcum)
    return valid, is_f, bcum, jnp.where(valid, li - bcum, 0.0)


def _ml_state_step(cn_ref, h, cn, keep, k, ws_dense, v_aug):
    w3 = jnp.concatenate([ws_dense.astype(BF16)] * (ML_AUG // LANES), axis=1)
    cn_ref[h] = keep * cn + _dot_tn(k, w3 * v_aug)


def _ml_v_aug(v_ref, h, n):
    return jnp.concatenate([v_ref[0, :, ML_V_DIM * h:ML_V_DIM * (h + 1)], jnp.ones((n, LANES), BF16)], axis=1)


def _ml_states_body(qk_ref, v_ref, sm_ref, ib_ref, fb_ref, sel_ref, cn_out, m_out, cn_ref, m_ref):
    n = qk_ref.shape[1]

    @pl.when(pl.program_id(1) == 0)
    def _init():
        cn_ref[...] = jnp.zeros_like(cn_ref)
        m_ref[...] = jnp.zeros_like(m_ref)

    cn_out[0, 0] = cn_ref[...].astype(cn_out.dtype)
    m_out[0, 0] = m_ref[...]
    yield
    valid, is_f, bcum, a = _ml_gates(sm_ref, ib_ref, fb_ref, n, False)
    m_prev = m_ref[0:1, :]
    g_last = jnp.maximum(m_prev, jnp.max(a, axis=0, keepdims=True))
    yield
    ws = _expand(jnp.exp(a - g_last), sel_ref[:, ML_HEADS * LANES:])
    keep = jnp.exp(m_prev - g_last)
    for h in range(ML_HEADS):
        lane_b = ML_GATE + ML_HEADS + h
        k = qk_ref[0, :, ML_QK_WIDTH + ML_QK_DIM * h:ML_QK_WIDTH + ML_QK_DIM * (h + 1)]
        _ml_state_step(cn_ref, h, cn_ref[h], keep[:, lane_b:lane_b + 1], k,
                       ws[:, LANES * h:LANES * (h + 1)], _ml_v_aug(v_ref, h, n))
        yield
    m_ref[...] = jnp.broadcast_to(bcum[0:1, :] + g_last, m_ref.shape)


def _ml_out_body(qk_ref, v_ref, sm_ref, ib_ref, fb_ref, sel_ref, cnb_ref, mb_ref, o_ref, cn_ref, m_ref):
    n = qk_ref.shape[1]

    @pl.when(pl.program_id(1) == 0)
    def _init():
        cn_ref[...] = jnp.zeros_like(cn_ref)
        m_ref[...] = jnp.zeros_like(m_ref)

    head_q = lambda h: qk_ref[0, :, ML_QK_DIM * h:ML_QK_DIM * (h + 1)]
    head_k = lambda h: qk_ref[0, :, ML_QK_WIDTH + ML_QK_DIM * h:ML_QK_WIDTH + ML_QK_DIM * (h + 1)]
    qk_all = [_dot_nt(head_q(h), head_k(h)) for h in range(ML_HEADS)]
    inter_all = [_dot(head_q(h), jnp.concatenate([cn_ref[h].astype(BF16), cnb_ref[0, 0, h]], axis=1))
                 for h in range(ML_HEADS)]
    yield
    valid, is_f, bcum, a = _ml_gates(sm_ref, ib_ref, fb_ref, n, True)
    m_prev = jnp.where(is_f, m_ref[0:1, :], mb_ref[0, 0, 0:1, :])
    a_t = a.T
    pre = suf = a_t[ML_GATE:ML_GATE + ML_ND, :]
    pos = lax.broadcasted_iota(jnp.int32, (ML_ND, n), 1)
    k = 1
    while k < n:
        pre = jnp.maximum(pre, jnp.where(pos >= k, pltpu.roll(pre, k, axis=1), -jnp.inf))
        suf = jnp.maximum(suf, jnp.where(pos < n - k, pltpu.roll(suf, n - k, axis=1), -jnp.inf))
        k *= 2
    run = jnp.where(lax.broadcasted_iota(jnp.int32, (ML_ND, n), 0) < ML_HEADS, pre, suf)
    run = jnp.concatenate([jnp.zeros((ML_GATE, n), F32), run,
                           jnp.zeros((LANES - ML_GATE - ML_ND, n), F32)], axis=0).T
    g = jnp.maximum(m_prev, run)
    m_t = bcum + g
    yield
    floor = jnp.exp(-m_t)
    dense = lambda x, lane_: jnp.broadcast_to(x[:, lane_:lane_ + 1], (n, LANES))
    diag_t = jnp.exp(a - g).T
    g_last = g[n - 1:n, :]
    ws = _expand(jnp.exp(a - g_last), sel_ref[:, :ML_HEADS * LANES])
    keep = jnp.exp(m_prev - g_last)
    ti = lax.broadcasted_iota(jnp.int32, (n, n), 0)
    si = lax.broadcasted_iota(jnp.int32, (n, n), 1)
    not_above, above, on_diag = si <= ti, si > ti, si == ti
    wide = lambda x: jnp.concatenate([x] * (n // LANES), axis=1)
    for h in range(ML_HEADS):
        lanes = (ML_GATE + h, ML_GATE + ML_HEADS + h)
        g_d = [dense(g, ln) for ln in lanes]
        qk, inter = qk_all[h], inter_all[h]
        e = jnp.where(not_above, a_t[lanes[0]:lanes[0] + 1, :] - wide(g_d[0]),
                      a_t[lanes[1]:lanes[1] + 1, :] - wide(g_d[1]))
        p = qk * jnp.exp(e)
        p_f = jnp.where(not_above, p, 0.0).astype(BF16)
        p_b = jnp.where(above, p, jnp.where(on_diag, qk * diag_t[lanes[1]:lanes[1] + 1, :], 0.0)).astype(BF16)
        intra = _dot(jnp.concatenate([p_f, p_b], axis=0), _ml_v_aug(v_ref, h, n))
        out = None
        for d in range(2):
            w_inter = jnp.exp(m_prev[:, lanes[d]:lanes[d] + 1] - g_d[d])
            s = (intra[n * d:n * (d + 1)] + jnp.concatenate([w_inter] * (ML_AUG // LANES), axis=1)
                 * inter[:, ML_AUG * d:ML_AUG * (d + 1)])
            rn = 1.0 / jnp.maximum(jnp.abs(s[:, ML_V_DIM:]), dense(floor, lanes[d]))
            hid = s[:, :ML_V_DIM] * jnp.concatenate([rn] * (ML_V_DIM // LANES), axis=1)
            out = hid if out is None else out + hid
        o_ref[0, :, ML_V_DIM * h:ML_V_DIM * (h + 1)] = out
        yield
    for h in range(ML_HEADS):
        lane_f = ML_GATE + h
        _ml_state_step(cn_ref, h, cn_ref[h], keep[:, lane_f:lane_f + 1], head_k(h),
                       ws[:, LANES * h:LANES * (h + 1)], _ml_v_aug(v_ref, h, n))
        yield
    m_ref[...] = jnp.broadcast_to(m_t[n - 1:n, :], m_ref.shape)


def _mlstm_parts(cvm, p3, sm3, i_bias, f_bias, n_ctx):
    nb, t, _ = cvm.shape
    nt = t // TILE
    assert _SM_OFF["i_f"] == ML_GATE and _SM_OFF["f_f"] == ML_GATE + ML_ND
    row = lambda v, off: jnp.pad(v.reshape(1, -1), ((0, 0), (off, LANES - off - ML_ND)))
    sel = np.zeros((LANES, ML_ND * LANES), np.float32)
    for r in range(ML_ND):
        sel[ML_GATE + r, r * LANES:(r + 1) * LANES] = 1.0
    const = lambda b, i: (0, 0)
    specs = lambda order: [
        pl.BlockSpec((1, TILE, 2 * ML_QK_WIDTH), lambda b, i: (b, order(i), CV_QK // (2 * ML_QK_WIDTH))),
        pl.BlockSpec((1, TILE, ML_V_WIDTH), lambda b, i: (b, order(i), _P_OFF["m_v"] // ML_V_WIDTH)),
        pl.BlockSpec((1, TILE, LANES), lambda b, i: (b, order(i), 0)),
        pl.BlockSpec((1, LANES), const),
        pl.BlockSpec((1, LANES), const),
        pl.BlockSpec((LANES, ML_ND * LANES), const)]
    args = (cvm, p3, sm3, row(i_bias, ML_GATE), row(f_bias, ML_GATE + ML_ND), jnp.asarray(sel, BF16))
    cn_block = (1, 1, ML_HEADS, ML_QK_DIM, ML_AUG)
    m_block = (1, 1, SUBLANES, LANES)
    scratch = [pltpu.VMEM(cn_block[2:], F32), pltpu.VMEM(m_block[2:], F32)]
    bwd = _tile_order(nt, True, n_ctx // TILE)
    fwd = _tile_order(nt, False)
    st_idx = lambda b, i: (b, bwd(i)) + (0,) * 3
    states = (_ml_states_body, specs(bwd), list(args),
              [jax.ShapeDtypeStruct((nb, nt) + cn_block[2:], BF16),
               jax.ShapeDtypeStruct((nb, nt) + m_block[2:], F32)],
              [pl.BlockSpec(cn_block, st_idx), pl.BlockSpec(m_block, lambda b, i: (b, bwd(i), 0, 0))],
              scratch)
    out = lambda cn_b, m_b: (
        _ml_out_body,
        specs(fwd) + [pl.BlockSpec(cn_block, lambda b, i: (b, i, 0, 0, 0)),
                      pl.BlockSpec(m_block, lambda b, i: (b, i, 0, 0))],
        list(args) + [cn_b, m_b],
        [jax.ShapeDtypeStruct((nb, t, ML_V_WIDTH), F32)],
        [pl.BlockSpec((1, TILE, ML_V_WIDTH), lambda b, i: (b, i, 0))], scratch)
    return states, out


GLA_SUB = 256
GLA_NCH = GLA_SUB // GLA_CHUNK
GLA_COLS = BF16_SUBLANES


def _gla_layout(is_ctx, lat_rows):
    r = np.arange(GLA_SUB)
    if is_ctx:
        return r // GLA_CHUNK, r % GLA_CHUNK
    col = r % SUBLANES
    cpc = GLA_CHUNK // lat_rows
    return col // cpc, (col % cpc) * lat_rows + r // SUBLANES


def _gla_consts(is_ctx, lat_rows, rev):
    ch, pos = _gla_layout(is_ctx, lat_rows)
    same = ch[:, None] == ch[None, :]
    before = (pos[None, :] >= pos[:, None]) if rev else (pos[None, :] <= pos[:, None])
    tri = (same & before).astype(np.float32)
    cmask = np.stack([np.repeat((ch == j)[:, None], LANES, axis=1) for j in range(GLA_NCH)])
    return tri, cmask.astype(np.float32)


def _gla_row(is_ctx, lat_rows, j, p):
    ch, pos = _gla_layout(is_ctx, lat_rows)
    return int(np.nonzero((ch == j) & (pos == p))[0][0])


def _per_chunk_rows(b, is_ctx, lat_rows, p):
    rows = [b[_gla_row(is_ctx, lat_rows, j, p):_gla_row(is_ctx, lat_rows, j, p) + 1, :]
            for j in range(GLA_NCH)]
    w = b.shape[1]
    if is_ctx:
        full = jnp.concatenate([jnp.broadcast_to(r, (GLA_CHUNK, w)) for r in rows], axis=0)
    else:
        rep = SUBLANES // GLA_NCH
        pat = jnp.concatenate([jnp.broadcast_to(r, (rep, w)) for r in rows], axis=0)
        full = jnp.broadcast_to(pat[None], (GLA_SUB // SUBLANES, SUBLANES, w)).reshape(GLA_SUB, w)
    return rows, full


def _gla_sub(q, k, v, araw, aup, abias, tri_b, tri_f, cmask_ref, st_ref, store, *, rev, is_ctx, lat_rows):
    want_out = store is not None
    g = _log_sigmoid(_dot_hp(araw, aup) + abias) * (1.0 / GLA_TAU)
    b = _dot_exact_lhs(tri_b, g, 2)
    yield
    lasts, last = _per_chunk_rows(b, is_ctx, lat_rows, 0 if rev else GLA_CHUNK - 1)
    kl = (k * jnp.exp(last - b)).astype(BF16)
    if want_out:
        _, ref = _per_chunk_rows(b, is_ctx, lat_rows, GLA_CHUNK // 2)
        qs = q * (GLA_K_DIM ** -0.5)
        qe = (qs * jnp.exp(b - ref)).astype(BF16)
        ke = (k * jnp.exp(ref - b)).astype(BF16)
        qb = (qs * jnp.exp(b)).astype(BF16)
        visible = tri_f > 0.0
    yield
    order = range(GLA_NCH - 1, -1, -1) if rev else range(GLA_NCH)
    outs = []
    for h in range(GLA_HEADS):
        ks = slice(GLA_K_DIM * h, GLA_K_DIM * (h + 1))
        vh = v[:, GLA_V_DIM * h:GLA_V_DIM * (h + 1)]
        klm = jnp.concatenate([kl[:, ks] * cmask_ref[j] for j in range(GLA_NCH)], axis=1)
        upd = _dot_tn(vh, klm)
        s = st_ref[h]
        s_in = [None] * GLA_NCH
        for j in order:
            s_in[j] = s.astype(BF16)
            s = s * jnp.exp(lasts[j][:, ks]) + upd[:, GLA_K_DIM * j:GLA_K_DIM * (j + 1)]
        st_ref[h] = s
        if want_out:
            att = jnp.where(visible, _dot_nt(qe[:, ks], ke[:, ks]), 0.0).astype(BF16)
            qbm = jnp.concatenate([qb[:, ks] * cmask_ref[j] for j in range(GLA_NCH)], axis=1)
            outs.append(_dot(att, vh) + _dot_nt(qbm, jnp.concatenate(s_in, axis=1)))
        yield
    if want_out:
        store(jnp.concatenate(outs, axis=1))


def _gla_ctx_body(qc_ref, kc_ref, vc_ref, sc_ref, ql_ref, kl_ref, vl_ref, sl_ref,
                  aup_ref, ab_ref, tcb_ref, tcf_ref, cmc_ref, tlb_ref, tlf_ref, cml_ref,
                  o_ref, st_ref, ctxo_ref, *, rev, ctx_rows, lat_rows, n_cblk, ctx_out):
    st_ref[...] = jnp.zeros_like(st_ref)

    def store(o):
        ctxo_ref[...] = o

    yield from _gla_sub(qc_ref[0].astype(F32), kc_ref[0].astype(F32), vc_ref[0], sc_ref[0],
                        aup_ref[...], ab_ref[...], tcb_ref[...], tcf_ref[...], cmc_ref, st_ref,
                        store if ctx_out else None, rev=rev, is_ctx=True, lat_rows=lat_rows)


def _gla_lat_body(qc_ref, kc_ref, vc_ref, sc_ref, ql_ref, kl_ref, vl_ref, sl_ref,
                  aup_ref, ab_ref, tcb_ref, tcf_ref, cmc_ref, tlb_ref, tlf_ref, cml_ref,
                  o_ref, st_ref, ctxo_ref, *, rev, ctx_rows, lat_rows, n_cblk, ctx_out):
    i = pl.program_id(1)
    r0, r1 = ctx_rows, ctx_rows + lat_rows
    cblk = (n_cblk - i) if rev else (i - 1)
    halves = range(GLA_COLS // SUBLANES)
    for half in (reversed(halves) if rev else halves):
        cs = slice(SUBLANES * half, SUBLANES * (half + 1))
        take = lambda r: r[0, r0:r1].astype(F32)[:, cs, :].reshape(GLA_SUB, r.shape[-1])

        def store(o, cs=cs):
            o_ref[0, r0:r1, cs, :] = o.reshape(lat_rows, SUBLANES, GLA_V_WIDTH)

        yield from _gla_sub(take(ql_ref), take(kl_ref), take(vl_ref).astype(BF16), take(sl_ref),
                            aup_ref[...], ab_ref[...], tlb_ref[...], tlf_ref[...], cml_ref, st_ref,
                            store, rev=rev, is_ctx=False, lat_rows=lat_rows)
    for r in range(ctx_rows):
        if ctx_out:
            start = pl.multiple_of(r * GRID_W + cblk * GLA_COLS, GLA_COLS)
            o_ref[0, r, :, :] = ctxo_ref[pl.ds(start, GLA_COLS), :]
        else:
            o_ref[0, r, :, :] = jnp.zeros((GLA_COLS, GLA_V_WIDTH), F32)


GLA_PHASES = ((lambda i: i == 0, "ctx"), (lambda i: i > 0, "lat"))


def _gla_part(p3, sm3, a_up, a_bias, n_ctx, rev, ctx_out):
    nb, t, ncol = p3.shape
    rows = t // GRID_W
    ctx_rows = n_ctx // GRID_W
    lat_rows = rows - ctx_rows
    n_cblk = GRID_W // GLA_COLS
    p4 = p3.reshape(nb, rows, GRID_W, ncol)
    sm4 = sm3.reshape(nb, rows, GRID_W, LANES)
    a_off = _SM_OFF["a_b"] if rev else _SM_OFF["a_f"]
    aup = jnp.pad(a_up, ((a_off, LANES - a_off - GLA_RANK), (0, 0)))
    cblk = lambda i: jnp.where(i == 0, n_cblk - 1 if rev else 0, (n_cblk - i) if rev else (i - 1))
    ctx = lambda blk: (lambda b, i: (b, 0, blk))
    lat = lambda blk: (lambda b, i: (b, 0, cblk(i), blk))
    const2 = lambda b, i: (0, 0)
    const3 = lambda b, i: (0, 0, 0)
    widths = (GLA_K_WIDTH, GLA_K_WIDTH, GLA_V_WIDTH)
    offs = (_P_OFF["g_q"], _P_OFF["g_k"], _P_OFF["g_v"])
    consts = []
    const_specs = []
    for is_ctx in (True, False):
        tri, cmask = _gla_consts(is_ctx, lat_rows, rev)
        consts += [jnp.asarray(tri, BF16), jnp.asarray(tri, F32), jnp.asarray(cmask, BF16)]
        const_specs += [pl.BlockSpec((GLA_SUB, GLA_SUB), const2), pl.BlockSpec((GLA_SUB, GLA_SUB), const2),
                        pl.BlockSpec((GLA_NCH, GLA_SUB, LANES), const3)]
    out_spec = pl.BlockSpec((1, rows, GLA_COLS, GLA_V_WIDTH), lat(0))
    in_specs = ([pl.BlockSpec((1, n_ctx, w), ctx(o // w)) for w, o in zip(widths, offs)]
                + [pl.BlockSpec((1, n_ctx, LANES), ctx(0))]
                + [pl.BlockSpec((1, rows, GLA_COLS, w), lat(o // w)) for w, o in zip(widths, offs)]
                + [pl.BlockSpec((1, rows, GLA_COLS, LANES), lat(0))]
                + [pl.BlockSpec((LANES, GLA_K_WIDTH), const2), pl.BlockSpec((1, GLA_K_WIDTH), const2)]
                + const_specs)
    args = [p3, p3, p3, sm3, p4, p4, p4, sm4, aup, a_bias.reshape(1, GLA_K_WIDTH)] + consts
    static = dict(rev=rev, ctx_rows=ctx_rows, lat_rows=lat_rows, n_cblk=n_cblk, ctx_out=ctx_out)
    bodies = {"ctx": functools.partial(_gla_ctx_body, **static),
              "lat": functools.partial(_gla_lat_body, **static)}
    return (bodies, in_specs, args,
            [jax.ShapeDtypeStruct((nb, rows, GRID_W, GLA_V_WIDTH), F32)], [out_spec],
            [pltpu.VMEM((GLA_HEADS, GLA_V_DIM, GLA_K_DIM), F32), pltpu.VMEM((n_ctx, GLA_V_WIDTH), F32)])


def _group_rmsnorm(y, groups):
    width = y.shape[-1] // groups
    ones = jnp.ones((width, LANES), BF16)
    out = []
    for g in range(groups):
        yg = y[:, width * g:width * (g + 1)]
        ms = _dot((yg * yg).astype(BF16), ones) * (1.0 / width)
        out.append(yg * jnp.concatenate([lax.rsqrt(ms + EPS)] * (width // LANES), axis=1))
    return jnp.concatenate(out, axis=1)


def _post_kernel(x_ref, y_ref, h_ref, of_ref, ob_ref, z_ref, mo_ref, gg_ref, gs_ref, gm_ref, gl_ref,
                 nws_ref, nwm_ref, nwg_ref, wbs_ref, wbm_ref, wbg_ref, wout_ref, g1_ref,
                 nwf_ref, sc2_ref, sh2_ref, xo_ref, ho_ref):
    y_ssd = (_group_rmsnorm(y_ref[0] * _silu(z_ref[0]), 2) * nws_ref[...]).astype(BF16)
    y_ml = (_group_rmsnorm(h_ref[0], ML_HEADS) * nwm_ref[...] * _sigmoid(mo_ref[0])).astype(BF16)
    y_gla = (_group_rmsnorm(of_ref[0] + ob_ref[0], GLA_HEADS) * nwg_ref[...] * _silu(gg_ref[0])).astype(BF16)
    merged = (_sigmoid(gs_ref[0]) * _dot(y_ssd, wbs_ref[...])
              + _sigmoid(gm_ref[0]) * _dot(y_ml, wbm_ref[...])
              + _sigmoid(gl_ref[0]) * _dot(y_gla, wbg_ref[...]))
    x_new = x_ref[0] + g1_ref[0] * _dot(merged.astype(BF16), wout_ref[...])
    xo_ref[0] = x_new
    ho_ref[0] = (_rms(x_new) * nwf_ref[...] * (1.0 + sc2_ref[0]) + sh2_ref[0]).astype(ho_ref.dtype)


def _post(x, scans, p3, norm_ws, w_bs, w_out, layer, norm_ffn_w, mods, n_ctx, tile0):
    nb, t, d = x.shape
    nt = t // TILE - tile0
    ctx_tiles = n_ctx // TILE
    tok = lambda blk: (lambda b, i: (b, i + tile0, blk))
    out = lambda b, i: (b, i, 0)
    const = lambda b, i: (0, 0)
    tok_spec = lambda blk: pl.BlockSpec((1, TILE, d), tok(blk))
    w_spec = pl.BlockSpec((None, d, d), lambda b, i: (layer, 0, 0), pipeline_mode=pl.Buffered(1))
    vec = pl.BlockSpec((1, d), const)
    mod = lambda k: pl.BlockSpec((1, 1, d), _mod_row(nb, k, ctx_tiles, tile0))
    names = ("s_z", "m_o", "g_g", "gate_ssd", "gate_ml", "gate_gla")
    in_specs = ([tok_spec(0)] * (1 + len(scans)) + [tok_spec(_P_OFF[nm] // d) for nm in names]
                + [vec] * 3 + [w_spec] * 4 + [mod(2), vec, mod(4), mod(3)])
    return pl.pallas_call(
        _post_kernel,
        out_shape=(jax.ShapeDtypeStruct((nb, nt * TILE, d), F32),
                   jax.ShapeDtypeStruct((nb, nt * TILE, d), BF16)),
        grid=(nb, nt),
        in_specs=in_specs,
        out_specs=(pl.BlockSpec((1, TILE, d), out), pl.BlockSpec((1, TILE, d), out)),
        compiler_params=_cparams(("parallel", "parallel")),
        name="post",
    )(x, *scans, *([p3] * 6), *[w.reshape(1, d) for w in norm_ws], *w_bs, w_out, mods,
      norm_ffn_w.reshape(1, d), mods, mods)


def _ffn_in_kernel(a_ref, w_ref, o_ref):
    acc = _dot(a_ref[...], w_ref[...])
    half = acc.shape[1] // 2
    o_ref[...] = (_silu(acc[:, :half]) * acc[:, half:]).astype(o_ref.dtype)


def _ffn_in(h, w_gu, layer, half):
    m, k = h.shape
    n_half = w_gu.shape[-1] // 2
    tm = _row_tile(m)
    return pl.pallas_call(
        _ffn_in_kernel,
        out_shape=jax.ShapeDtypeStruct((m, n_half), BF16),
        grid=(n_half // half, m // tm),
        in_specs=[pl.BlockSpec((tm, k), lambda j, i: (i, 0)),
                  pl.BlockSpec((None, k, 2 * half), lambda j, i: (layer, 0, j))],
        out_specs=pl.BlockSpec((tm, half), lambda j, i: (i, j)),
        compiler_params=_cparams(("parallel", "parallel")),
        name="ffn_in",
    )(h, w_gu)


def _ffn_out_kernel(a_ref, w_ref, x_ref, g_ref, nw_ref, sc_ref, sh_ref, xo_ref, ho_ref):
    x_new = x_ref[0] + g_ref[0] * _dot(a_ref[0], w_ref[...])
    xo_ref[0] = x_new
    ho_ref[0] = (_rms(x_new) * nw_ref[...] * (1.0 + sc_ref[0]) + sh_ref[0]).astype(ho_ref.dtype)


def _ffn_out_last_kernel(a_ref, w_ref, x_ref, g_ref, nw_ref, o_ref):
    x_new = x_ref[0] + g_ref[0] * _dot(a_ref[0], w_ref[...])
    o_ref[0] = _rms(x_new) * nw_ref[...]


def _ffn_out(a, w, layer, x, mods, n_ctx, tile0, next_norm_w, next_mods):
    nb, t, d = x.shape
    k = a.shape[-1]
    ctx_tiles = n_ctx // TILE
    tok = lambda b, i: (b, i, 0)
    mod = lambda k_: pl.BlockSpec((1, 1, d), _mod_row(nb, k_, ctx_tiles, tile0))
    in_specs = [pl.BlockSpec((1, TILE, k), tok),
                pl.BlockSpec((None, k, d), lambda b, i: (layer, 0, 0), pipeline_mode=pl.Buffered(1)),
                pl.BlockSpec((1, TILE, d), tok),
                mod(5),
                pl.BlockSpec((1, d), lambda b, i: (0, 0))]
    args = [a, w, x, mods, next_norm_w.reshape(1, d)]
    if next_mods is None:
        body = _ffn_out_last_kernel
        out_shape = jax.ShapeDtypeStruct((nb, t, d), F32)
        out_specs = pl.BlockSpec((1, TILE, d), tok)
    else:
        body = _ffn_out_kernel
        in_specs += [mod(1), mod(0)]
        args += [next_mods, next_mods]
        out_shape = (jax.ShapeDtypeStruct((nb, t, d), F32), jax.ShapeDtypeStruct((nb, t, d), BF16))
        out_specs = (pl.BlockSpec((1, TILE, d), tok), pl.BlockSpec((1, TILE, d), tok))
    return pl.pallas_call(
        body,
        out_shape=out_shape,
        grid=(nb, t // TILE),
        in_specs=in_specs,
        out_specs=out_specs,
        compiler_params=_cparams(("parallel", "parallel")),
        name="ffn_out",
    )(*args)


def _proj_weights(w_in):
    cols = lambda names: [w_in[..., _IN_OFF[nm]:_IN_OFF[nm] + _IN_W[nm]] for nm in names]
    main = jnp.concatenate(cols(_P_ORDER), axis=-1)
    small = jnp.pad(jnp.concatenate(cols(_SMALL), axis=-1), ((0, 0), (0, 0), (0, LANES - N_SMALL_USED)))
    return main.astype(BF16), small.astype(BF16)


def _ffn_weight(w_ffn_in, half):
    d_ff = w_ffn_in.shape[-1] // 2
    cols = []
    for j in range(d_ff // half):
        cols.append(w_ffn_in[..., j * half:(j + 1) * half])
        cols.append(w_ffn_in[..., d_ff + j * half:d_ff + (j + 1) * half])
    return jnp.concatenate(cols, axis=-1).astype(BF16)


def kernel(x, c, ctx, c_ctx, w_mod, b_mod, norm_mix_w, norm_ffn_w, w_in, ssd_conv_w, ssd_conv_b, ssd_dt_bias, ssd_a_log, ssd_d, ssd_norm_w, ml_conv_w, ml_conv_b, ml_i_bias, ml_f_bias, ml_norm_w, gla_a_up, gla_a_bias, gla_norm_w, w_b_ssd, w_b_ml, w_b_gla, w_out, w_ffn_in, w_ffn_out, final_norm_w):
    nb, n_lat, d = x.shape
    n_ctx = ctx.shape[1]
    t = n_ctx + n_lat
    depth = w_in.shape[0]
    d_ff = w_ffn_out.shape[1]
    assert n_ctx == TILE == GLA_SUB and n_lat % TILE == 0 and n_lat // GRID_W == 32
    ffn_half = d_ff // 2

    xs = jnp.concatenate([ctx, x], axis=1)
    c16 = jnp.pad(jnp.concatenate([c, c_ctx[None]], axis=0), ((0, 2 * SUBLANES - nb - 1), (0, 0)))
    mods = [_modulation(c16, w_mod, b_mod, l).reshape(2 * SUBLANES * 6, 1, d) for l in range(depth)]
    h = _norm_mod(xs, norm_mix_w[0], mods[0], 0, 1, n_ctx)
    conv_w = jnp.concatenate([ssd_conv_w, ml_conv_w], axis=-1)
    conv_b = jnp.concatenate([ssd_conv_b, ml_conv_b], axis=-1)
    conv_post = jnp.concatenate([jnp.ones((ssd_conv_w.shape[-1] + ML_QK_WIDTH,), F32),
                                 jnp.full((ML_QK_WIDTH,), ML_QK_DIM ** -0.5, F32)])
    w_main, w_small = _proj_weights(w_in)
    w_gu = _ffn_weight(w_ffn_in, ffn_half)
    w_bs = [w.astype(BF16) for w in (w_b_ssd, w_b_ml, w_b_gla)]
    w_out_b = w_out.astype(BF16)
    w_ffn_out_b = w_ffn_out.astype(BF16)
    for l in range(depth):
        last = l == depth - 1
        h2d = h.reshape(nb * t, d)
        tn = N_PROJ // 5
        conv_tile = _P_OFF["s_x"] // tn
        assert conv_tile * tn == _P_OFF["s_x"] and N_PROJ - _P_OFF["s_x"] == tn == CV_WIDTH
        m_tiles = nb * t // _row_tile(nb * t)
        (pc,), (sm,) = _run_together("proj_first", (1, m_tiles),
                                     _mm_part(h2d, w_main, l, tn, BF16, conv_tile, 1),
                                     _mm_part(h2d, w_small, l, LANES, F32, 0, 1))
        pc3, sm3 = pc.reshape(nb, t, tn), sm.reshape(nb, t, LANES)
        grid = (conv_tile, m_tiles)
        (p,), (cv,) = _run_together(
            "proj_conv", grid, _mm_part(h2d, w_main, l, tn, BF16, 0, conv_tile),
            _conv_part(pc3, conv_w[l], conv_b[l], conv_post, n_ctx, grid))
        p3 = p.reshape(nb, t, conv_tile * tn)
        d_e = jnp.repeat(ssd_d[l], SSD_HEAD_DIM).reshape(1, SSD_WIDTH)
        ssd_states, ssd_out = _ssd_parts(cv, sm3, ssd_dt_bias[l], ssd_a_log[l], d_e, n_ctx)
        ml_states, ml_out = _mlstm_parts(cv, p3, sm3, ml_i_bias[l], ml_f_bias[l], n_ctx)
        grid = (nb, t // TILE)
        ssd_st, ml_st = _run_together("bwd_states", grid, ssd_states, ml_states)
        (y,), (hm,) = _run_together("ssd_mlstm", grid, ssd_out(*ssd_st), ml_out(*ml_st))
        gla = [_gla_part(p3, sm3, gla_a_up[l, k], gla_a_bias[l, k], n_ctx, bool(k), not last) for k in range(2)]
        (og_f,), (og_b,) = _run_together("gla", (nb, GRID_W // GLA_COLS + 1), *gla, phases=GLA_PHASES)
        og_f, og_b = og_f.reshape(nb, t, GLA_V_WIDTH), og_b.reshape(nb, t, GLA_V_WIDTH)
        tile0 = n_ctx // TILE if last else 0
        xs, h2 = _post(xs, (y, hm, og_f, og_b), p3, (ssd_norm_w[l], ml_norm_w[l], gla_norm_w[l]),
                       w_bs, w_out_b, l, norm_ffn_w[l], mods[l], n_ctx, tile0)
        nt = xs.shape[1]
        a = _ffn_in(h2.reshape(nb * nt, d), w_gu, l, ffn_half)
        a = a.reshape(nb, nt, d_ff)
        if last:
            return _ffn_out(a, w_ffn_out_b, l, xs, mods[l], n_ctx, tile0, final_norm_w, None)
        xs, h = _ffn_out(a, w_ffn_out_b, l, xs, mods[l], n_ctx, tile0,
                         norm_mix_w[l + 1], mods[l + 1])
```

```python
import functools

import numpy as np
import jax
import jax.numpy as jnp
from jax import lax
from jax.experimental import pallas as pl
from jax.experimental.pallas import tpu as pltpu

F32 = jnp.float32
BF16 = jnp.bfloat16

EPS = 1e-6
LOG2E = 1.4426950408889634
GRID_W = 64
SSD_HEADS = 16
SSD_HEAD_DIM = 64
SSD_WIDTH = 1024
SSD_STATE = 64
SSD_BC = 128
ML_HEADS = 4
ML_QK_DIM = 128
ML_V_DIM = 256
ML_QK_WIDTH = 512
ML_V_WIDTH = 1024
GLA_HEADS = 4
GLA_K_DIM = 128
GLA_V_DIM = 256
GLA_K_WIDTH = 512
GLA_V_WIDTH = 1024
GLA_RANK = 16
GLA_TAU = 16.0
GLA_CHUNK = 64

LANES = 128
SUBLANES = 8
BF16_SUBLANES = 16
VMEM_LIMIT = 56 * 1024 * 1024

TILE = 256

_IN_NAMES = ("s_x", "s_z", "s_b", "s_c", "dt_f", "dt_b",
             "m_q", "m_k", "m_v", "m_o", "i_f", "i_b", "f_f", "f_b",
             "g_q", "g_k", "g_v", "g_g", "a_f", "a_b",
             "gate_ssd", "gate_ml", "gate_gla")
_IN_WIDTHS = (1024, 1024, 128, 128, 16, 16,
              512, 512, 1024, 1024, 4, 4, 4, 4,
              512, 512, 1024, 1024, 16, 16,
              1024, 1024, 1024)
_IN_OFF = dict(zip(_IN_NAMES, np.concatenate([[0], np.cumsum(_IN_WIDTHS)[:-1]]).tolist()))
_IN_W = dict(zip(_IN_NAMES, _IN_WIDTHS))

_P_ORDER = ("s_z", "m_o", "g_g", "gate_ssd", "gate_ml", "gate_gla", "m_v", "g_v",
            "g_q", "g_k", "s_x", "s_b", "s_c", "m_q", "m_k")
_P_OFF = {}
_o = 0
for _n in _P_ORDER:
    _P_OFF[_n] = _o
    _o += _IN_W[_n]
N_PROJ = _o
_SMALL = ("dt_f", "dt_b", "i_f", "i_b", "f_f", "f_b", "a_f", "a_b")
_SM_OFF = {}
_s = 0
for _n in _SMALL:
    _SM_OFF[_n] = _s
    _s += _IN_W[_n]
N_SMALL_USED = _s


def _cparams(sem):
    return pltpu.CompilerParams(dimension_semantics=sem, vmem_limit_bytes=VMEM_LIMIT)


def _sigmoid(x):
    return 0.5 * jnp.tanh(0.5 * x) + 0.5


def _silu(x):
    return x * _sigmoid(x)


def _softplus(x):
    return jnp.maximum(x, 0.0) + jnp.log1p(jnp.exp(-jnp.abs(x)))


def _log_sigmoid(x):
    return jnp.minimum(x, 0.0) - jnp.log(1.0 + jnp.exp(-jnp.abs(x)))


def _split(x, n):
    out = []
    r = x
    for _ in range(n):
        p = r.astype(BF16)
        out.append(p)
        r = r - p.astype(F32)
    return out


def _dot(a, b):
    return jnp.dot(a, b, preferred_element_type=F32)


def _dot_nt(a, b):
    return lax.dot_general(a, b, (((1,), (1,)), ((), ())), preferred_element_type=F32)


def _dot_tn(a, b):
    return lax.dot_general(a, b, (((0,), (0,)), ((), ())), preferred_element_type=F32)


def _dot_exact_lhs(t, x, pieces):
    return sum(_dot(t, p) for p in _split(x, pieces))


def _dot_hp(a, b):
    ah, am = _split(a, 2)
    bh, bm = _split(b, 2)
    return _dot(ah, bh) + _dot(ah, bm) + _dot(am, bh)


def _causal(n, rev):
    t = lax.broadcasted_iota(jnp.int32, (n, n), 0)
    s = lax.broadcasted_iota(jnp.int32, (n, n), 1)
    return (s >= t) if rev else (s <= t)


def _rms(x):
    return x * lax.rsqrt(jnp.mean(x * x, axis=-1, keepdims=True) + EPS)


def _mod_kernel(c_ref, w_ref, b_ref, o_ref):
    o_ref[...] = _dot_hp(_silu(c_ref[...]), w_ref[...]) + b_ref[...]


def _modulation(c16, w_mod, b_mod, layer):
    rows, d = c16.shape
    n = w_mod.shape[-1]
    tn = 1536
    return pl.pallas_call(
        _mod_kernel,
        out_shape=jax.ShapeDtypeStruct((rows, n), F32),
        grid=(n // tn,),
        in_specs=[pl.BlockSpec((rows, d), lambda j: (0, 0)),
                  pl.BlockSpec((None, d, tn), lambda j: (layer, 0, j)),
                  pl.BlockSpec((None, 1, tn), lambda j: (layer, 0, j))],
        out_specs=pl.BlockSpec((rows, tn), lambda j: (0, j)),
        compiler_params=_cparams(("arbitrary",)),
        name="modulation",
    )(c16, w_mod, b_mod.reshape(b_mod.shape[0], 1, n))


def _mod_row(nb, k, ctx_tiles, tile0=0):
    return lambda b, i: (jnp.where(i + tile0 < ctx_tiles, nb, b) * 6 + k, 0, 0)


def _norm_mod_kernel(ctx_ref, x_ref, w_ref, sc_ref, sh_ref, xo_ref, ho_ref):
    x = jnp.where(pl.program_id(1) == 0, ctx_ref[0], x_ref[0])
    xo_ref[0] = x
    ho_ref[0] = (_rms(x) * w_ref[...] * (1.0 + sc_ref[0]) + sh_ref[0]).astype(ho_ref.dtype)


def _norm_mod(ctx, x, w, mods, k_shift, k_scale):
    nb, n_lat, d = x.shape
    n_ctx = ctx.shape[1]
    assert n_ctx == TILE
    t = n_ctx + n_lat
    tok = lambda b, i: (b, i, 0)
    return pl.pallas_call(
        _norm_mod_kernel,
        out_shape=(jax.ShapeDtypeStruct((nb, t, d), F32), jax.ShapeDtypeStruct((nb, t, d), BF16)),
        grid=(nb, t // TILE),
        in_specs=[pl.BlockSpec((1, TILE, d), lambda b, i: (b, 0, 0)),
                  pl.BlockSpec((1, TILE, d), lambda b, i: (b, jnp.maximum(i - 1, 0), 0)),
                  pl.BlockSpec((1, d), lambda b, i: (0, 0)),
                  pl.BlockSpec((1, 1, d), _mod_row(nb, k_scale, n_ctx // TILE)),
                  pl.BlockSpec((1, 1, d), _mod_row(nb, k_shift, n_ctx // TILE))],
        out_specs=(pl.BlockSpec((1, TILE, d), tok), pl.BlockSpec((1, TILE, d), tok)),
        compiler_params=_cparams(("parallel", "arbitrary")),
        name="norm_mod",
    )(ctx, x, w.reshape(1, d), mods, mods)


MM_STAGE = 256


def _mm_body(a_ref, w_ref, o_ref):
    a = a_ref[...]
    width = o_ref.shape[1]
    stage = min(MM_STAGE, width)
    for c0 in range(0, width, stage):
        o_ref[:, c0:c0 + stage] = _dot(a, w_ref[:, c0:c0 + stage]).astype(o_ref.dtype)
        yield


def _row_tile(m):
    return 512 if m % 512 == 0 else TILE


def _mm_part(a, w, layer, tn, out_dtype, tile0, n_tiles):
    m, k = a.shape
    tm = _row_tile(m)
    return (_mm_body,
            [pl.BlockSpec((tm, k), lambda j, i: (i, 0)),
             pl.BlockSpec((None, k, tn), lambda j, i: (layer, 0, j + tile0))],
            [a, w], [jax.ShapeDtypeStruct((m, n_tiles * tn), out_dtype)],
            [pl.BlockSpec((tm, tn), lambda j, i: (i, j))], [])


CONV_K = 5
CONV_ROWS = 128
CONV_X = SSD_WIDTH // LANES
CONV_BC = 2 * SSD_BC // LANES
CONV_QK = 2 * ML_QK_WIDTH // LANES
CV_QK = SSD_WIDTH
CV_BC = SSD_WIDTH + 2 * ML_QK_WIDTH
CV_WIDTH = CV_BC + 2 * SSD_BC


def _conv_body(u_ref, w_ref, b_ref, s_ref, o_ref, pad_ref, *, n_ctx):
    t, c = u_ref.shape[1], u_ref.shape[2]
    half = CONV_K // 2
    zeros = jnp.zeros((SUBLANES, c), F32)
    w = w_ref[...]
    bias = b_ref[...]
    post = s_ref[...]
    for s0, n in ((0, n_ctx), (n_ctx, t - n_ctx)):
        pad_ref[0:SUBLANES, :] = zeros
        pad_ref[SUBLANES:SUBLANES + n, :] = u_ref[0, s0:s0 + n, :].astype(F32)
        pad_ref[SUBLANES + n:2 * SUBLANES + n, :] = zeros
        for r0 in range(0, n, CONV_ROWS):
            acc = bias
            for j in range(CONV_K):
                lo = SUBLANES - half + j + r0
                acc = acc + w[j:j + 1, :] * pad_ref[lo:lo + CONV_ROWS, :]
            o_ref[0, s0 + r0:s0 + r0 + CONV_ROWS, :] = (_silu(acc) * post).astype(o_ref.dtype)
            yield


def _conv_part(u3, w, b, post_scale, n_ctx, grid):
    nb, t, width = u3.shape
    n_cb = width // LANES
    steps = grid[0] * grid[1]
    rep = steps // (nb * n_cb)
    assert steps == rep * nb * n_cb and n_cb == CONV_X + CONV_BC + CONV_QK
    blk = lambda j, i: (j * grid[1] + i) // rep
    chan = lambda j, i: blk(j, i) % n_cb
    out_chan = lambda c: jnp.where(c < CONV_X, c, jnp.where(c < CONV_X + CONV_BC, c + CONV_QK, c - CONV_BC))
    vec = lambda rows: pl.BlockSpec((rows, LANES), lambda j, i: (0, chan(j, i)))
    return (functools.partial(_conv_body, n_ctx=n_ctx),
            [pl.BlockSpec((1, t, LANES), lambda j, i: (blk(j, i) // n_cb, 0, chan(j, i))),
             vec(CONV_K), vec(1), vec(1)],
            [u3, w, b.reshape(1, width), post_scale.reshape(1, width)],
            [jax.ShapeDtypeStruct((nb, t, width), BF16)],
            [pl.BlockSpec((1, t, LANES), lambda j, i: (blk(j, i) // n_cb, 0, out_chan(chan(j, i))))],
            [pltpu.VMEM((t + 2 * SUBLANES, LANES), F32)])


def _tile_order(n_tiles, rev, ctx_tiles=1):
    if rev:
        return lambda i: jnp.where(i < ctx_tiles, ctx_tiles - 1 - i, n_tiles - 1 + ctx_tiles - i)
    return lambda i: i


SSD_PAIRS = SSD_HEADS // 2
SSD_B_OFF = SSD_HEADS
SSD_TILE = 256


def _ssd_gates(sm_ref, dtb_ref, alog_ref, n, both):
    lane = lax.broadcasted_iota(jnp.int32, (1, LANES), 1)
    dt = _softplus(sm_ref[0] + dtb_ref[...])
    la = dt * jnp.where(lane < 2 * SSD_HEADS, -jnp.exp(alog_ref[...]), 0.0)
    parts = _split(la, 3)
    upp = jnp.where(_causal(n, True), 1.0, 0.0).astype(BF16)
    cum = sum(_dot(upp, p) for p in parts)
    if both:
        low = jnp.where(_causal(n, False), 1.0, 0.0).astype(BF16)
        cum = jnp.where(lane < SSD_HEADS, sum(_dot(low, p) for p in parts), cum)
    return lane, dt, cum


def _expand(a, e):
    return sum(_dot(p, e) for p in _split(a, 2))


def _group_dup(v, g, lo):
    other = pltpu.roll(v, SSD_STATE, axis=1)
    return jnp.where(lo, v, other) if g == 0 else jnp.where(lo, other, v)


def _ssd_state_step(st_ref, j, bw, xp, elast, off):
    r = lax.broadcasted_iota(jnp.int32, (LANES, LANES), 0) < SSD_STATE
    c = lax.broadcasted_iota(jnp.int32, (LANES, LANES), 1) < SSD_HEAD_DIM
    dec = jnp.where(r, elast[:, off + 2 * j:off + 2 * j + 1], elast[:, off + 2 * j + 1:off + 2 * j + 2])
    st_ref[j] = jnp.where(r == c, dec * st_ref[j] + _dot_tn(bw.astype(BF16), xp), 0.0)


def _interleave(*bodies):
    live = list(bodies)
    while live:
        for body in list(live):
            if next(body, StopIteration) is StopIteration:
                live.remove(body)


def _ssd_states_body(x_ref, bc_ref, sm_ref, dtb_ref, alog_ref, eb_ref, o_ref, st_ref):
    n = x_ref.shape[1]

    @pl.when(pl.program_id(1) == 0)
    def _init():
        st_ref[...] = jnp.zeros_like(st_ref)

    o_ref[0, 0] = st_ref[...].astype(o_ref.dtype)
    yield
    lane, dt, cum = _ssd_gates(sm_ref, dtb_ref, alog_ref, n, False)
    lo = lane < SSD_HEAD_DIM
    last = cum[0:1, :]
    yield
    wst = _expand(jnp.exp(last - cum) * dt, eb_ref[...])
    elast = jnp.exp(last)
    b128 = bc_ref[0, :, :SSD_BC].astype(F32)
    for j in range(SSD_PAIRS):
        sl = slice(LANES * j, LANES * (j + 1))
        bw = _group_dup(b128, j // (SSD_PAIRS // 2), lo) * wst[:, sl]
        _ssd_state_step(st_ref, j, bw, x_ref[0, :, sl], elast, SSD_B_OFF)
        if j % 2:
            yield


def _ssd_out_body(x_ref, bc_ref, sm_ref, dtb_ref, alog_ref, ef_ref, eb_ref, d_ref, sb_ref, o_ref, st_ref):
    n = x_ref.shape[1]

    @pl.when(pl.program_id(1) == 0)
    def _init():
        st_ref[...] = jnp.zeros_like(st_ref)

    zero_b = jnp.zeros((), BF16)
    lo = lax.broadcasted_iota(jnp.int32, (1, LANES), 1) < SSD_HEAD_DIM
    b128_b = bc_ref[0, :, :SSD_BC]
    c128_b = bc_ref[0, :, SSD_BC:]
    cb_all = [_dot_nt(jnp.where(lo if g == 0 else jnp.logical_not(lo), c128_b, zero_b), b128_b)
              for g in range(2)]
    pair_x = lambda j: x_ref[0, :, LANES * j:LANES * (j + 1)]
    rhs_all = [jnp.concatenate([jnp.where(lo, pair_x(j), zero_b), jnp.where(lo, zero_b, pair_x(j)),
                                st_ref[j].astype(BF16), sb_ref[0, 0, j]], axis=0) for j in range(SSD_PAIRS)]
    yield
    lane, dt, cum = _ssd_gates(sm_ref, dtb_ref, alog_ref, n, True)
    is_f = lane < SSD_HEADS
    ldt = jnp.log(dt)
    dsum = jnp.log(dt + pltpu.roll(dt, LANES - SSD_B_OFF, axis=1))
    rt = (jnp.where(lane < 2 * SSD_HEADS, cum - ldt, pltpu.roll(dsum, 2 * SSD_HEADS, axis=1)) * LOG2E).T
    cum2 = cum * LOG2E
    yield
    last = jnp.where(is_f, cum[n - 1:n, :], cum[0:1, :])
    elast = jnp.exp(last)
    ecum = _split(jnp.exp(cum), 2)
    ecum_f = sum(_dot(p, ef_ref[...]) for p in ecum)
    ecum_b = sum(_dot(p, eb_ref[...]) for p in ecum)
    wst = _expand(jnp.exp(last - cum) * dt, ef_ref[...])
    bc = bc_ref[0].astype(F32)
    b128, c128 = bc[:, :SSD_BC], bc[:, SSD_BC:]
    ti = lax.broadcasted_iota(jnp.int32, (n, n), 0)
    si = lax.broadcasted_iota(jnp.int32, (n, n), 1)
    below, above = si < ti, si > ti
    half = SSD_PAIRS // 2
    yield
    for g in range(2):
        cb = cb_all[g]
        cdup = _group_dup(c128, g, lo)
        for j in range(g * half, (g + 1) * half):
            ms = []
            for h in (2 * j, 2 * j + 1):
                e_f = cum2[:, h:h + 1] - rt[h:h + 1, :]
                e_b = cum2[:, SSD_B_OFF + h:SSD_B_OFF + h + 1] - rt[SSD_B_OFF + h:SSD_B_OFF + h + 1, :]
                e = jnp.where(below, e_f, jnp.where(above, e_b, rt[2 * SSD_HEADS + h:2 * SSD_HEADS + h + 1, :]))
                ms.append((cb * jnp.exp2(e)).astype(BF16))
            sl = slice(LANES * j, LANES * (j + 1))
            cs_f = (cdup * ecum_f[:, sl]).astype(BF16)
            cs_b = (cdup * ecum_b[:, sl]).astype(BF16)
            lhs = jnp.concatenate(ms + [cs_f, cs_b], axis=1)
            o_ref[0, :, sl] = _dot(lhs, rhs_all[j]) + d_ref[:, sl] * pair_x(j).astype(F32)
            yield
    for j in range(SSD_PAIRS):
        sl = slice(LANES * j, LANES * (j + 1))
        _ssd_state_step(st_ref, j, _group_dup(b128, j // half, lo) * wst[:, sl], x_ref[0, :, sl], elast, 0)
        if j % 2:
            yield


def _run_together(name, grid, *parts, phases=None):
    n_in = [len(p[1]) for p in parts]
    n_out = [len(p[3]) for p in parts]
    n_scr = [len(p[5]) for p in parts]

    def kern(*refs):
        ins, outs, scr = refs[:sum(n_in)], refs[sum(n_in):sum(n_in) + sum(n_out)], refs[sum(n_in) + sum(n_out):]

        def run(key):
            bodies = []
            for k, p in enumerate(parts):
                take = lambda seq, counts: seq[sum(counts[:k]):sum(counts[:k + 1])]
                body = p[0][key] if isinstance(p[0], dict) else p[0]
                bodies.append(body(*take(ins, n_in), *take(outs, n_out), *take(scr, n_scr)))
            _interleave(*bodies)

        if phases is None:
            run(None)
        else:
            for pred, key in phases:
                pl.when(pred(pl.program_id(1)))(functools.partial(run, key))

    res = pl.pallas_call(
        kern,
        out_shape=tuple(s for p in parts for s in p[3]),
        grid=grid,
        in_specs=[s for p in parts for s in p[1]],
        out_specs=tuple(s for p in parts for s in p[4]),
        scratch_shapes=[s for p in parts for s in p[5]],
        compiler_params=_cparams(("arbitrary", "arbitrary")),
        name=name,
    )(*[a for p in parts for a in p[2]])
    return [list(res[sum(n_out[:k]):sum(n_out[:k + 1])]) for k in range(len(parts))]


def _ssd_parts(cvs, sm3, dt_bias, a_log, d_e, n_ctx):
    nb, t, _ = cvs.shape
    tile = SSD_TILE
    nt = t // tile
    row = lambda v: jnp.pad(v.reshape(1, -1), ((0, 0), (0, LANES - 2 * SSD_HEADS)))
    const = lambda b, i: (0, 0)
    specs = lambda order: [
        pl.BlockSpec((1, tile, SSD_WIDTH), lambda b, i: (b, order(i), 0)),
        pl.BlockSpec((1, tile, 2 * SSD_BC), lambda b, i: (b, order(i), CV_BC // (2 * SSD_BC))),
        pl.BlockSpec((1, tile, LANES), lambda b, i: (b, order(i), 0)),
        pl.BlockSpec((1, LANES), const),
        pl.BlockSpec((1, LANES), const)]
    st_block = (1, 1, SSD_PAIRS, LANES, LANES)
    args = (cvs, cvs, sm3, row(dt_bias), row(a_log))
    sel = np.zeros((2, LANES, SSD_WIDTH), np.float32)
    for h in range(SSD_HEADS):
        sel[0, h, h * SSD_HEAD_DIM:(h + 1) * SSD_HEAD_DIM] = 1.0
        sel[1, SSD_B_OFF + h, h * SSD_HEAD_DIM:(h + 1) * SSD_HEAD_DIM] = 1.0
    e_f, e_b = jnp.asarray(sel[0], BF16), jnp.asarray(sel[1], BF16)
    e_spec = pl.BlockSpec((LANES, SSD_WIDTH), const)
    bwd = _tile_order(nt, True, n_ctx // tile)
    fwd = _tile_order(nt, False)
    scratch = [pltpu.VMEM(st_block[2:], F32)]
    states = (_ssd_states_body, specs(bwd) + [e_spec], list(args) + [e_b],
              [jax.ShapeDtypeStruct((nb, nt) + st_block[2:], BF16)],
              [pl.BlockSpec(st_block, lambda b, i: (b, bwd(i), 0, 0, 0))], scratch)
    out = lambda states_b: (
        _ssd_out_body,
        specs(fwd) + [e_spec, e_spec, pl.BlockSpec((1, SSD_WIDTH), const),
                      pl.BlockSpec(st_block, lambda b, i: (b, i, 0, 0, 0))],
        list(args) + [e_f, e_b, d_e, states_b],
        [jax.ShapeDtypeStruct((nb, t, SSD_WIDTH), F32)],
        [pl.BlockSpec((1, tile, SSD_WIDTH), lambda b, i: (b, i, 0))], scratch)
    return states, out


ML_GATE = 32
ML_ND = 2 * ML_HEADS
ML_AUG = ML_V_DIM + LANES


def _ml_gates(sm_ref, ib_ref, fb_ref, n, both):
    lane = lax.broadcasted_iota(jnp.int32, (1, LANES), 1)
    valid = (lane >= ML_GATE) & (lane < ML_GATE + ML_ND)
    is_f = lane < ML_GATE + ML_HEADS
    sm = sm_ref[0]
    li = sm + ib_ref[...]
    lf = pltpu.roll(_log_sigmoid(sm + fb_ref[...]), LANES - ML_ND, axis=1)
    parts = _split(jnp.where(valid, lf, 0.0), 3)
    upp = jnp.where(_causal(n, True), 1.0, 0.0).astype(BF16)
    bcum = sum(_dot(upp, p) for p in parts)
    if both:
        low = jnp.where(_causal(n, False), 1.0, 0.0).astype(BF16)
        bcum = jnp.where(is_f, sum(_dot(low, p) for p in parts), bcum)
    return valid, is_f, bcum, jnp.where(valid, li - bcum, 0.0)


def _ml_state_step(cn_ref, h, cn, keep, k, ws_dense, v_aug):
    w3 = jnp.concatenate([ws_dense.astype(BF16)] * (ML_AUG // LANES), axis=1)
    cn_ref[h] = keep * cn + _dot_tn(k, w3 * v_aug)


def _ml_v_aug(v_ref, h, n):
    return jnp.concatenate([v_ref[0, :, ML_V_DIM * h:ML_V_DIM * (h + 1)], jnp.ones((n, LANES), BF16)], axis=1)


def _ml_states_body(qk_ref, v_ref, sm_ref, ib_ref, fb_ref, sel_ref, cn_out, m_out, cn_ref, m_ref):
    n = qk_ref.shape[1]

    @pl.when(pl.program_id(1) == 0)
    def _init():
        cn_ref[...] = jnp.zeros_like(cn_ref)
        m_ref[...] = jnp.zeros_like(m_ref)

    cn_out[0, 0] = cn_ref[...].astype(cn_out.dtype)
    m_out[0, 0] = m_ref[...]
    yield
    valid, is_f, bcum, a = _ml_gates(sm_ref, ib_ref, fb_ref, n, False)
    m_prev = m_ref[0:1, :]
    g_last = jnp.maximum(m_prev, jnp.max(a, axis=0, keepdims=True))
    yield
    ws = _expand(jnp.exp(a - g_last), sel_ref[:, ML_HEADS * LANES:])
    keep = jnp.exp(m_prev - g_last)
    for h in range(ML_HEADS):
        lane_b = ML_GATE + ML_HEADS + h
        k = qk_ref[0, :, ML_QK_WIDTH + ML_QK_DIM * h:ML_QK_WIDTH + ML_QK_DIM * (h + 1)]
        _ml_state_step(cn_ref, h, cn_ref[h], keep[:, lane_b:lane_b + 1], k,
                       ws[:, LANES * h:LANES * (h + 1)], _ml_v_aug(v_ref, h, n))
        yield
    m_ref[...] = jnp.broadcast_to(bcum[0:1, :] + g_last, m_ref.shape)


def _ml_out_body(qk_ref, v_ref, sm_ref, ib_ref, fb_ref, sel_ref, cnb_ref, mb_ref, o_ref, cn_ref, m_ref):
    n = qk_ref.shape[1]

    @pl.when(pl.program_id(1) == 0)
    def _init():
        cn_ref[...] = jnp.zeros_like(cn_ref)
        m_ref[...] = jnp.zeros_like(m_ref)

    head_q = lambda h: qk_ref[0, :, ML_QK_DIM * h:ML_QK_DIM * (h + 1)]
    head_k = lambda h: qk_ref[0, :, ML_QK_WIDTH + ML_QK_DIM * h:ML_QK_WIDTH + ML_QK_DIM * (h + 1)]
    qk_all = [_dot_nt(head_q(h), head_k(h)) for h in range(ML_HEADS)]
    inter_all = [_dot(head_q(h), jnp.concatenate([cn_ref[h].astype(BF16), cnb_ref[0, 0, h]], axis=1))
                 for h in range(ML_HEADS)]
    yield
    valid, is_f, bcum, a = _ml_gates(sm_ref, ib_ref, fb_ref, n, True)
    m_prev = jnp.where(is_f, m_ref[0:1, :], mb_ref[0, 0, 0:1, :])
    a_t = a.T
    pre = suf = a_t[ML_GATE:ML_GATE + ML_ND, :]
    pos = lax.broadcasted_iota(jnp.int32, (ML_ND, n), 1)
    k = 1
    while k < n:
        pre = jnp.maximum(pre, jnp.where(pos >= k, pltpu.roll(pre, k, axis=1), -jnp.inf))
        suf = jnp.maximum(suf, jnp.where(pos < n - k, pltpu.roll(suf, n - k, axis=1), -jnp.inf))
        k *= 2
    run = jnp.where(lax.broadcasted_iota(jnp.int32, (ML_ND, n), 0) < ML_HEADS, pre, suf)
    run = jnp.concatenate([jnp.zeros((ML_GATE, n), F32), run,
                           jnp.zeros((LANES - ML_GATE - ML_ND, n), F32)], axis=0).T
    g = jnp.maximum(m_prev, run)
    m_t = bcum + g
    yield
    floor = jnp.exp(-m_t)
    dense = lambda x, lane_: jnp.broadcast_to(x[:, lane_:lane_ + 1], (n, LANES))
    diag_t = jnp.exp(a - g).T
    g_last = g[n - 1:n, :]
    ws = _expand(jnp.exp(a - g_last), sel_ref[:, :ML_HEADS * LANES])
    keep = jnp.exp(m_prev - g_last)
    ti = lax.broadcasted_iota(jnp.int32, (n, n), 0)
    si = lax.broadcasted_iota(jnp.int32, (n, n), 1)
    not_above, above, on_diag = si <= ti, si > ti, si == ti
    wide = lambda x: jnp.concatenate([x] * (n // LANES), axis=1)
    for h in range(ML_HEADS):
        lanes = (ML_GATE + h, ML_GATE + ML_HEADS + h)
        g_d = [dense(g, ln) for ln in lanes]
        qk, inter = qk_all[h], inter_all[h]
        e = jnp.where(not_above, a_t[lanes[0]:lanes[0] + 1, :] - wide(g_d[0]),
                      a_t[lanes[1]:lanes[1] + 1, :] - wide(g_d[1]))
        p = qk * jnp.exp(e)
        p_f = jnp.where(not_above, p, 0.0).astype(BF16)
        p_b = jnp.where(above, p, jnp.where(on_diag, qk * diag_t[lanes[1]:lanes[1] + 1, :], 0.0)).astype(BF16)
        intra = _dot(jnp.concatenate([p_f, p_b], axis=0), _ml_v_aug(v_ref, h, n))
        out = None
        for d in range(2):
            w_inter = jnp.exp(m_prev[:, lanes[d]:lanes[d] + 1] - g_d[d])
            s = (intra[n * d:n * (d + 1)] + jnp.concatenate([w_inter] * (ML_AUG // LANES), axis=1)
                 * inter[:, ML_AUG * d:ML_AUG * (d + 1)])
            rn = 1.0 / jnp.maximum(jnp.abs(s[:, ML_V_DIM:]), dense(floor, lanes[d]))
            hid = s[:, :ML_V_DIM] * jnp.concatenate([rn] * (ML_V_DIM // LANES), axis=1)
            out = hid if out is None else out + hid
        o_ref[0, :, ML_V_DIM * h:ML_V_DIM * (h + 1)] = out
        yield
    for h in range(ML_HEADS):
        lane_f = ML_GATE + h
        _ml_state_step(cn_ref, h, cn_ref[h], keep[:, lane_f:lane_f + 1], head_k(h),
                       ws[:, LANES * h:LANES * (h + 1)], _ml_v_aug(v_ref, h, n))
        yield
    m_ref[...] = jnp.broadcast_to(m_t[n - 1:n, :], m_ref.shape)


def _mlstm_parts(cvm, p3, sm3, i_bias, f_bias, n_ctx):
    nb, t, _ = cvm.shape
    nt = t // TILE
    assert _SM_OFF["i_f"] == ML_GATE and _SM_OFF["f_f"] == ML_GATE + ML_ND
    row = lambda v, off: jnp.pad(v.reshape(1, -1), ((0, 0), (off, LANES - off - ML_ND)))
    sel = np.zeros((LANES, ML_ND * LANES), np.float32)
    for r in range(ML_ND):
        sel[ML_GATE + r, r * LANES:(r + 1) * LANES] = 1.0
    const = lambda b, i: (0, 0)
    specs = lambda order: [
        pl.BlockSpec((1, TILE, 2 * ML_QK_WIDTH), lambda b, i: (b, order(i), CV_QK // (2 * ML_QK_WIDTH))),
        pl.BlockSpec((1, TILE, ML_V_WIDTH), lambda b, i: (b, order(i), _P_OFF["m_v"] // ML_V_WIDTH)),
        pl.BlockSpec((1, TILE, LANES), lambda b, i: (b, order(i), 0)),
        pl.BlockSpec((1, LANES), const),
        pl.BlockSpec((1, LANES), const),
        pl.BlockSpec((LANES, ML_ND * LANES), const)]
    args = (cvm, p3, sm3, row(i_bias, ML_GATE), row(f_bias, ML_GATE + ML_ND), jnp.asarray(sel, BF16))
    cn_block = (1, 1, ML_HEADS, ML_QK_DIM, ML_AUG)
    m_block = (1, 1, SUBLANES, LANES)
    scratch = [pltpu.VMEM(cn_block[2:], F32), pltpu.VMEM(m_block[2:], F32)]
    bwd = _tile_order(nt, True, n_ctx // TILE)
    fwd = _tile_order(nt, False)
    st_idx = lambda b, i: (b, bwd(i)) + (0,) * 3
    states = (_ml_states_body, specs(bwd), list(args),
              [jax.ShapeDtypeStruct((nb, nt) + cn_block[2:], BF16),
               jax.ShapeDtypeStruct((nb, nt) + m_block[2:], F32)],
              [pl.BlockSpec(cn_block, st_idx), pl.BlockSpec(m_block, lambda b, i: (b, bwd(i), 0, 0))],
              scratch)
    out = lambda cn_b, m_b: (
        _ml_out_body,
        specs(fwd) + [pl.BlockSpec(cn_block, lambda b, i: (b, i, 0, 0, 0)),
                      pl.BlockSpec(m_block, lambda b, i: (b, i, 0, 0))],
        list(args) + [cn_b, m_b],
        [jax.ShapeDtypeStruct((nb, t, ML_V_WIDTH), F32)],
        [pl.BlockSpec((1, TILE, ML_V_WIDTH), lambda b, i: (b, i, 0))], scratch)
    return states, out


GLA_SUB = 256
GLA_NCH = GLA_SUB // GLA_CHUNK
GLA_COLS = BF16_SUBLANES


def _gla_layout(is_ctx, lat_rows):
    r = np.arange(GLA_SUB)
    if is_ctx:
        return r // GLA_CHUNK, r % GLA_CHUNK
    col = r % SUBLANES
    cpc = GLA_CHUNK // lat_rows
    return col // cpc, (col % cpc) * lat_rows + r // SUBLANES


def _gla_consts(is_ctx, lat_rows, rev):
    ch, pos = _gla_layout(is_ctx, lat_rows)
    same = ch[:, None] == ch[None, :]
    before = (pos[None, :] >= pos[:, None]) if rev else (pos[None, :] <= pos[:, None])
    tri = (same & before).astype(np.float32)
    cmask = np.stack([np.repeat((ch == j)[:, None], LANES, axis=1) for j in range(GLA_NCH)])
    return tri, cmask.astype(np.float32)


def _gla_row(is_ctx, lat_rows, j, p):
    ch, pos = _gla_layout(is_ctx, lat_rows)
    return int(np.nonzero((ch == j) & (pos == p))[0][0])


def _per_chunk_rows(b, is_ctx, lat_rows, p):
    rows = [b[_gla_row(is_ctx, lat_rows, j, p):_gla_row(is_ctx, lat_rows, j, p) + 1, :]
            for j in range(GLA_NCH)]
    w = b.shape[1]
    if is_ctx:
        full = jnp.concatenate([jnp.broadcast_to(r, (GLA_CHUNK, w)) for r in rows], axis=0)
    else:
        rep = SUBLANES // GLA_NCH
        pat = jnp.concatenate([jnp.broadcast_to(r, (rep, w)) for r in rows], axis=0)
        full = jnp.broadcast_to(pat[None], (GLA_SUB // SUBLANES, SUBLANES, w)).reshape(GLA_SUB, w)
    return rows, full


def _gla_sub(q, k, v, araw, aup, abias, tri_b, tri_f, cmask_ref, st_ref, store, *, rev, is_ctx, lat_rows):
    want_out = store is not None
    g = _log_sigmoid(_dot_hp(araw, aup) + abias) * (1.0 / GLA_TAU)
    b = _dot_exact_lhs(tri_b, g, 2)
    yield
    lasts, last = _per_chunk_rows(b, is_ctx, lat_rows, 0 if rev else GLA_CHUNK - 1)
    kl = (k * jnp.exp(last - b)).astype(BF16)
    if want_out:
        _, ref = _per_chunk_rows(b, is_ctx, lat_rows, GLA_CHUNK // 2)
        qs = q * (GLA_K_DIM ** -0.5)
        qe = (qs * jnp.exp(b - ref)).astype(BF16)
        ke = (k * jnp.exp(ref - b)).astype(BF16)
        qb = (qs * jnp.exp(b)).astype(BF16)
        visible = tri_f > 0.0
    yield
    order = range(GLA_NCH - 1, -1, -1) if rev else range(GLA_NCH)
    outs = []
    for h in range(GLA_HEADS):
        ks = slice(GLA_K_DIM * h, GLA_K_DIM * (h + 1))
        vh = v[:, GLA_V_DIM * h:GLA_V_DIM * (h + 1)]
        klm = jnp.concatenate([kl[:, ks] * cmask_ref[j] for j in range(GLA_NCH)], axis=1)
        upd = _dot_tn(vh, klm)
        s = st_ref[h]
        s_in = [None] * GLA_NCH
        for j in order:
            s_in[j] = s.astype(BF16)
            s = s * jnp.exp(lasts[j][:, ks]) + upd[:, GLA_K_DIM * j:GLA_K_DIM * (j + 1)]
        st_ref[h] = s
        if want_out:
            att = jnp.where(visible, _dot_nt(qe[:, ks], ke[:, ks]), 0.0).astype(BF16)
            qbm = jnp.concatenate([qb[:, ks] * cmask_ref[j] for j in range(GLA_NCH)], axis=1)
            outs.append(_dot(att, vh) + _dot_nt(qbm, jnp.concatenate(s_in, axis=1)))
        yield
    if want_out:
        store(jnp.concatenate(outs, axis=1))


def _gla_ctx_body(qc_ref, kc_ref, vc_ref, sc_ref, ql_ref, kl_ref, vl_ref, sl_ref,
                  aup_ref, ab_ref, tcb_ref, tcf_ref, cmc_ref, tlb_ref, tlf_ref, cml_ref,
                  o_ref, st_ref, ctxo_ref, *, rev, ctx_rows, lat_rows, n_sub, ctx_out):
    st_ref[...] = jnp.zeros_like(st_ref)

    def store(o):
        ctxo_ref[...] = o

    yield from _gla_sub(qc_ref[0].astype(F32), kc_ref[0].astype(F32), vc_ref[0], sc_ref[0],
                        aup_ref[...], ab_ref[...], tcb_ref[...], tcf_ref[...], cmc_ref, st_ref,
                        store if ctx_out else None, rev=rev, is_ctx=True, lat_rows=lat_rows)


def _gla_lat_body(qc_ref, kc_ref, vc_ref, sc_ref, ql_ref, kl_ref, vl_ref, sl_ref,
                  aup_ref, ab_ref, tcb_ref, tcf_ref, cmc_ref, tlb_ref, tlf_ref, cml_ref,
                  o_ref, st_ref, ctxo_ref, *, half, rev, ctx_rows, lat_rows, n_sub, ctx_out):
    i = pl.program_id(1)
    r0, r1 = ctx_rows, ctx_rows + lat_rows
    cblk = ((n_sub - i) if rev else (i - 1)) // (GLA_COLS // SUBLANES)
    cs = slice(SUBLANES * half, SUBLANES * (half + 1))
    take = lambda r: r[0, r0:r1].astype(F32)[:, cs, :].reshape(GLA_SUB, r.shape[-1])

    def store(o):
        o_ref[0, r0:r1, cs, :] = o.reshape(lat_rows, SUBLANES, GLA_V_WIDTH)

    yield from _gla_sub(take(ql_ref), take(kl_ref), take(vl_ref).astype(BF16), take(sl_ref),
                        aup_ref[...], ab_ref[...], tlb_ref[...], tlf_ref[...], cml_ref, st_ref,
                        store, rev=rev, is_ctx=False, lat_rows=lat_rows)
    for r in range(ctx_rows):
        if ctx_out:
            start = pl.multiple_of(r * GRID_W + cblk * GLA_COLS + SUBLANES * half, SUBLANES)
            o_ref[0, r, cs, :] = ctxo_ref[pl.ds(start, SUBLANES), :]
        else:
            o_ref[0, r, cs, :] = jnp.zeros((SUBLANES, GLA_V_WIDTH), F32)


SCAN_PHASES = ((lambda i: i == 0, "ctx"), (lambda i: i % 2 == 1, "odd"),
               (lambda i: (i > 0) & (i % 2 == 0), "even"))


def _gla_part(p3, sm3, a_up, a_bias, n_ctx, rev, ctx_out):
    nb, t, ncol = p3.shape
    rows = t // GRID_W
    ctx_rows = n_ctx // GRID_W
    lat_rows = rows - ctx_rows
    n_sub = GRID_W // SUBLANES
    per_blk = GLA_COLS // SUBLANES
    p4 = p3.reshape(nb, rows, GRID_W, ncol)
    sm4 = sm3.reshape(nb, rows, GRID_W, LANES)
    a_off = _SM_OFF["a_b"] if rev else _SM_OFF["a_f"]
    aup = jnp.pad(a_up, ((a_off, LANES - a_off - GLA_RANK), (0, 0)))
    sub = lambda i: (n_sub - jnp.maximum(i, 1)) if rev else (jnp.maximum(i, 1) - 1)
    cblk = lambda i: sub(i) // per_blk
    ctx = lambda blk: (lambda b, i: (b, 0, blk))
    lat = lambda blk: (lambda b, i: (b, 0, cblk(i), blk))
    const2 = lambda b, i: (0, 0)
    const3 = lambda b, i: (0, 0, 0)
    widths = (GLA_K_WIDTH, GLA_K_WIDTH, GLA_V_WIDTH)
    offs = (_P_OFF["g_q"], _P_OFF["g_k"], _P_OFF["g_v"])
    consts = []
    const_specs = []
    for is_ctx in (True, False):
        tri, cmask = _gla_consts(is_ctx, lat_rows, rev)
        consts += [jnp.asarray(tri, BF16), jnp.asarray(tri, F32), jnp.asarray(cmask, BF16)]
        const_specs += [pl.BlockSpec((GLA_SUB, GLA_SUB), const2), pl.BlockSpec((GLA_SUB, GLA_SUB), const2),
                        pl.BlockSpec((GLA_NCH, GLA_SUB, LANES), const3)]
    out_spec = pl.BlockSpec((1, rows, GLA_COLS, GLA_V_WIDTH), lat(0))
    in_specs = ([pl.BlockSpec((1, n_ctx, w), ctx(o // w)) for w, o in zip(widths, offs)]
                + [pl.BlockSpec((1, n_ctx, LANES), ctx(0))]
                + [pl.BlockSpec((1, rows, GLA_COLS, w), lat(o // w)) for w, o in zip(widths, offs)]
                + [pl.BlockSpec((1, rows, GLA_COLS, LANES), lat(0))]
                + [pl.BlockSpec((LANES, GLA_K_WIDTH), const2), pl.BlockSpec((1, GLA_K_WIDTH), const2)]
                + const_specs)
    args = [p3, p3, p3, sm3, p4, p4, p4, sm4, aup, a_bias.reshape(1, GLA_K_WIDTH)] + consts
    static = dict(rev=rev, ctx_rows=ctx_rows, lat_rows=lat_rows, n_sub=n_sub, ctx_out=ctx_out)
    bodies = {"ctx": functools.partial(_gla_ctx_body, **static),
              "odd": functools.partial(_gla_lat_body, half=int(rev), **static),
              "even": functools.partial(_gla_lat_body, half=1 - int(rev), **static)}
    return (bodies, in_specs, args,
            [jax.ShapeDtypeStruct((nb, rows, GRID_W, GLA_V_WIDTH), F32)], [out_spec],
            [pltpu.VMEM((GLA_HEADS, GLA_V_DIM, GLA_K_DIM), F32), pltpu.VMEM((n_ctx, GLA_V_WIDTH), F32)])


def _group_rmsnorm(y, groups):
    width = y.shape[-1] // groups
    ones = jnp.ones((width, LANES), BF16)
    out = []
    for g in range(groups):
        yg = y[:, width * g:width * (g + 1)]
        ms = _dot((yg * yg).astype(BF16), ones) * (1.0 / width)
        out.append(yg * jnp.concatenate([lax.rsqrt(ms + EPS)] * (width // LANES), axis=1))
    return jnp.concatenate(out, axis=1)


def _post_kernel(x_ref, y_ref, h_ref, of_ref, ob_ref, z_ref, mo_ref, gg_ref, gs_ref, gm_ref, gl_ref,
                 nws_ref, nwm_ref, nwg_ref, wbs_ref, wbm_ref, wbg_ref, wout_ref, g1_ref,
                 nwf_ref, sc2_ref, sh2_ref, xo_ref, ho_ref):
    y_ssd = (_group_rmsnorm(y_ref[0] * _silu(z_ref[0]), 2) * nws_ref[...]).astype(BF16)
    y_ml = (_group_rmsnorm(h_ref[0], ML_HEADS) * nwm_ref[...] * _sigmoid(mo_ref[0])).astype(BF16)
    y_gla = (_group_rmsnorm(of_ref[0] + ob_ref[0], GLA_HEADS) * nwg_ref[...] * _silu(gg_ref[0])).astype(BF16)
    merged = (_sigmoid(gs_ref[0]) * _dot(y_ssd, wbs_ref[...])
              + _sigmoid(gm_ref[0]) * _dot(y_ml, wbm_ref[...])
              + _sigmoid(gl_ref[0]) * _dot(y_gla, wbg_ref[...]))
    x_new = x_ref[0] + g1_ref[0] * _dot(merged.astype(BF16), wout_ref[...])
    xo_ref[0] = x_new
    ho_ref[0] = (_rms(x_new) * nwf_ref[...] * (1.0 + sc2_ref[0]) + sh2_ref[0]).astype(ho_ref.dtype)


def _post(x, scans, p3, norm_ws, w_bs, w_out, layer, norm_ffn_w, mods, n_ctx, tile0):
    nb, t, d = x.shape
    nt = t // TILE - tile0
    ctx_tiles = n_ctx // TILE
    tok = lambda blk: (lambda b, i: (b, i + tile0, blk))
    out = lambda b, i: (b, i, 0)
    const = lambda b, i: (0, 0)
    tok_spec = lambda blk: pl.BlockSpec((1, TILE, d), tok(blk))
    w_spec = pl.BlockSpec((None, d, d), lambda b, i: (layer, 0, 0), pipeline_mode=pl.Buffered(1))
    vec = pl.BlockSpec((1, d), const)
    mod = lambda k: pl.BlockSpec((1, 1, d), _mod_row(nb, k, ctx_tiles, tile0))
    names = ("s_z", "m_o", "g_g", "gate_ssd", "gate_ml", "gate_gla")
    in_specs = ([tok_spec(0)] * (1 + len(scans)) + [tok_spec(_P_OFF[nm] // d) for nm in names]
                + [vec] * 3 + [w_spec] * 4 + [mod(2), vec, mod(4), mod(3)])
    return pl.pallas_call(
        _post_kernel,
        out_shape=(jax.ShapeDtypeStruct((nb, nt * TILE, d), F32),
                   jax.ShapeDtypeStruct((nb, nt * TILE, d), BF16)),
        grid=(nb, nt),
        in_specs=in_specs,
        out_specs=(pl.BlockSpec((1, TILE, d), out), pl.BlockSpec((1, TILE, d), out)),
        compiler_params=_cparams(("parallel", "parallel")),
        name="post",
    )(x, *scans, *([p3] * 6), *[w.reshape(1, d) for w in norm_ws], *w_bs, w_out, mods,
      norm_ffn_w.reshape(1, d), mods, mods)


def _ffn_in_kernel(a_ref, w_ref, o_ref):
    acc = _dot(a_ref[...], w_ref[...])
    half = acc.shape[1] // 2
    o_ref[...] = (_silu(acc[:, :half]) * acc[:, half:]).astype(o_ref.dtype)


def _ffn_in(h, w_gu, layer, half):
    m, k = h.shape
    n_half = w_gu.shape[-1] // 2
    tm = _row_tile(m)
    return pl.pallas_call(
        _ffn_in_kernel,
        out_shape=jax.ShapeDtypeStruct((m, n_half), BF16),
        grid=(n_half // half, m // tm),
        in_specs=[pl.BlockSpec((tm, k), lambda j, i: (i, 0)),
                  pl.BlockSpec((None, k, 2 * half), lambda j, i: (layer, 0, j))],
        out_specs=pl.BlockSpec((tm, half), lambda j, i: (i, j)),
        compiler_params=_cparams(("parallel", "parallel")),
        name="ffn_in",
    )(h, w_gu)


def _ffn_out_kernel(a_ref, w_ref, x_ref, g_ref, nw_ref, sc_ref, sh_ref, xo_ref, ho_ref):
    x_new = x_ref[0] + g_ref[0] * _dot(a_ref[0], w_ref[...])
    xo_ref[0] = x_new
    ho_ref[0] = (_rms(x_new) * nw_ref[...] * (1.0 + sc_ref[0]) + sh_ref[0]).astype(ho_ref.dtype)


def _ffn_out_last_kernel(a_ref, w_ref, x_ref, g_ref, nw_ref, o_ref):
    x_new = x_ref[0] + g_ref[0] * _dot(a_ref[0], w_ref[...])
    o_ref[0] = _rms(x_new) * nw_ref[...]


def _ffn_out(a, w, layer, x, mods, n_ctx, tile0, next_norm_w, next_mods):
    nb, t, d = x.shape
    k = a.shape[-1]
    ctx_tiles = n_ctx // TILE
    tok = lambda b, i: (b, i, 0)
    mod = lambda k_: pl.BlockSpec((1, 1, d), _mod_row(nb, k_, ctx_tiles, tile0))
    in_specs = [pl.BlockSpec((1, TILE, k), tok),
                pl.BlockSpec((None, k, d), lambda b, i: (layer, 0, 0), pipeline_mode=pl.Buffered(1)),
                pl.BlockSpec((1, TILE, d), tok),
                mod(5),
                pl.BlockSpec((1, d), lambda b, i: (0, 0))]
    args = [a, w, x, mods, next_norm_w.reshape(1, d)]
    if next_mods is None:
        body = _ffn_out_last_kernel
        out_shape = jax.ShapeDtypeStruct((nb, t, d), F32)
        out_specs = pl.BlockSpec((1, TILE, d), tok)
    else:
        body = _ffn_out_kernel
        in_specs += [mod(1), mod(0)]
        args += [next_mods, next_mods]
        out_shape = (jax.ShapeDtypeStruct((nb, t, d), F32), jax.ShapeDtypeStruct((nb, t, d), BF16))
        out_specs = (pl.BlockSpec((1, TILE, d), tok), pl.BlockSpec((1, TILE, d), tok))
    return pl.pallas_call(
        body,
        out_shape=out_shape,
        grid=(nb, t // TILE),
        in_specs=in_specs,
        out_specs=out_specs,
        compiler_params=_cparams(("parallel", "parallel")),
        name="ffn_out",
    )(*args)


def _proj_weights(w_in):
    cols = lambda names: [w_in[..., _IN_OFF[nm]:_IN_OFF[nm] + _IN_W[nm]] for nm in names]
    main = jnp.concatenate(cols(_P_ORDER), axis=-1)
    small = jnp.pad(jnp.concatenate(cols(_SMALL), axis=-1), ((0, 0), (0, 0), (0, LANES - N_SMALL_USED)))
    return main.astype(BF16), small.astype(BF16)


def _ffn_weight(w_ffn_in, half):
    d_ff = w_ffn_in.shape[-1] // 2
    cols = []
    for j in range(d_ff // half):
        cols.append(w_ffn_in[..., j * half:(j + 1) * half])
        cols.append(w_ffn_in[..., d_ff + j * half:d_ff + (j + 1) * half])
    return jnp.concatenate(cols, axis=-1).astype(BF16)


def kernel(x, c, ctx, c_ctx, w_mod, b_mod, norm_mix_w, norm_ffn_w, w_in, ssd_conv_w, ssd_conv_b, ssd_dt_bias, ssd_a_log, ssd_d, ssd_norm_w, ml_conv_w, ml_conv_b, ml_i_bias, ml_f_bias, ml_norm_w, gla_a_up, gla_a_bias, gla_norm_w, w_b_ssd, w_b_ml, w_b_gla, w_out, w_ffn_in, w_ffn_out, final_norm_w):
    nb, n_lat, d = x.shape
    n_ctx = ctx.shape[1]
    t = n_ctx + n_lat
    depth = w_in.shape[0]
    d_ff = w_ffn_out.shape[1]
    assert n_ctx == TILE == GLA_SUB and n_lat % TILE == 0 and n_lat // GRID_W == 32
    ffn_half = d_ff // 2

    c16 = jnp.pad(jnp.concatenate([c, c_ctx[None]], axis=0), ((0, 2 * SUBLANES - nb - 1), (0, 0)))
    mods = [_modulation(c16, w_mod, b_mod, l).reshape(2 * SUBLANES * 6, 1, d) for l in range(depth)]
    xs, h = _norm_mod(ctx, x, norm_mix_w[0], mods[0], 0, 1)
    conv_w = jnp.concatenate([ssd_conv_w, ml_conv_w], axis=-1)
    conv_b = jnp.concatenate([ssd_conv_b, ml_conv_b], axis=-1)
    conv_post = jnp.concatenate([jnp.ones((ssd_conv_w.shape[-1] + ML_QK_WIDTH,), F32),
                                 jnp.full((ML_QK_WIDTH,), ML_QK_DIM ** -0.5, F32)])
    w_main, w_small = _proj_weights(w_in)
    w_gu = _ffn_weight(w_ffn_in, ffn_half)
    w_bs = [w.astype(BF16) for w in (w_b_ssd, w_b_ml, w_b_gla)]
    w_out_b = w_out.astype(BF16)
    w_ffn_out_b = w_ffn_out.astype(BF16)
    for l in range(depth):
        last = l == depth - 1
        h2d = h.reshape(nb * t, d)
        tn = N_PROJ // 5
        conv_tile = _P_OFF["s_x"] // tn
        assert conv_tile * tn == _P_OFF["s_x"] and N_PROJ - _P_OFF["s_x"] == tn == CV_WIDTH
        m_tiles = nb * t // _row_tile(nb * t)
        (pc,), (sm,) = _run_together("proj_first", (1, m_tiles),
                                     _mm_part(h2d, w_main, l, tn, BF16, conv_tile, 1),
                                     _mm_part(h2d, w_small, l, LANES, F32, 0, 1))
        pc3, sm3 = pc.reshape(nb, t, tn), sm.reshape(nb, t, LANES)
        grid = (conv_tile, m_tiles)
        (p,), (cv,) = _run_together(
            "proj_conv", grid, _mm_part(h2d, w_main, l, tn, BF16, 0, conv_tile),
            _conv_part(pc3, conv_w[l], conv_b[l], conv_post, n_ctx, grid))
        p3 = p.reshape(nb, t, conv_tile * tn)
        d_e = jnp.repeat(ssd_d[l], SSD_HEAD_DIM).reshape(1, SSD_WIDTH)
        ssd_states, ssd_out = _ssd_parts(cv, sm3, ssd_dt_bias[l], ssd_a_log[l], d_e, n_ctx)
        ml_states, ml_out = _mlstm_parts(cv, p3, sm3, ml_i_bias[l], ml_f_bias[l], n_ctx)
        gla = [_gla_part(p3, sm3, gla_a_up[l, k], gla_a_bias[l, k], n_ctx, bool(k), not last) for k in range(2)]
        grid = (nb, t // TILE)
        assert grid[1] == 1 + GRID_W // SUBLANES
        (og_f,), (og_b,), ssd_st, ml_st = _run_together("gla_states", grid, *gla, ssd_states, ml_states,
                                                        phases=SCAN_PHASES)
        og_f, og_b = og_f.reshape(nb, t, GLA_V_WIDTH), og_b.reshape(nb, t, GLA_V_WIDTH)
        (y,), (hm,) = _run_together("ssd_mlstm", grid, ssd_out(*ssd_st), ml_out(*ml_st))
        tile0 = n_ctx // TILE if last else 0
        xs, h2 = _post(xs, (y, hm, og_f, og_b), p3, (ssd_norm_w[l], ml_norm_w[l], gla_norm_w[l]),
                       w_bs, w_out_b, l, norm_ffn_w[l], mods[l], n_ctx, tile0)
        nt = xs.shape[1]
        a = _ffn_in(h2.reshape(nb * nt, d), w_gu, l, ffn_half)
        a = a.reshape(nb, nt, d_ff)
        if last:
            return _ffn_out(a, w_ffn_out_b, l, xs, mods[l], n_ctx, tile0, final_norm_w, None)
        xs, h = _ffn_out(a, w_ffn_out_b, l, xs, mods[l], n_ctx, tile0,
                         norm_mix_w[l + 1], mods[l + 1])
```

```python
import functools

import numpy as np
import jax
import jax.numpy as jnp
from jax import lax
from jax.experimental import pallas as pl
from jax.experimental.pallas import tpu as pltpu

F32 = jnp.float32
BF16 = jnp.bfloat16

EPS = 1e-6
LOG2E = 1.4426950408889634
GRID_W = 64
SSD_HEADS = 16
SSD_HEAD_DIM = 64
SSD_WIDTH = 1024
SSD_STATE = 64
SSD_BC = 128
ML_HEADS = 4
ML_QK_DIM = 128
ML_V_DIM = 256
ML_QK_WIDTH = 512
ML_V_WIDTH = 1024
GLA_HEADS = 4
GLA_K_DIM = 128
GLA_V_DIM = 256
GLA_K_WIDTH = 512
GLA_V_WIDTH = 1024
GLA_RANK = 16
GLA_TAU = 16.0
GLA_CHUNK = 64

LANES = 128
SUBLANES = 8
BF16_SUBLANES = 16
VMEM_LIMIT = 56 * 1024 * 1024

TILE = 256

_IN_NAMES = ("s_x", "s_z", "s_b", "s_c", "dt_f", "dt_b",
             "m_q", "m_k", "m_v", "m_o", "i_f", "i_b", "f_f", "f_b",
             "g_q", "g_k", "g_v", "g_g", "a_f", "a_b",
             "gate_ssd", "gate_ml", "gate_gla")
_IN_WIDTHS = (1024, 1024, 128, 128, 16, 16,
              512, 512, 1024, 1024, 4, 4, 4, 4,
              512, 512, 1024, 1024, 16, 16,
              1024, 1024, 1024)
_IN_OFF = dict(zip(_IN_NAMES, np.concatenate([[0], np.cumsum(_IN_WIDTHS)[:-1]]).tolist()))
_IN_W = dict(zip(_IN_NAMES, _IN_WIDTHS))

_P_ORDER = ("s_z", "m_o", "g_g", "gate_ssd", "gate_ml", "gate_gla", "m_v", "g_v",
            "g_q", "g_k", "s_x", "s_b", "s_c", "m_q", "m_k")
_P_OFF = {}
_o = 0
for _n in _P_ORDER:
    _P_OFF[_n] = _o
    _o += _IN_W[_n]
N_PROJ = _o
_SMALL = ("dt_f", "dt_b", "i_f", "i_b", "f_f", "f_b", "a_f", "a_b")
_SM_OFF = {}
_s = 0
for _n in _SMALL:
    _SM_OFF[_n] = _s
    _s += _IN_W[_n]
N_SMALL_USED = _s


def _cparams(sem):
    return pltpu.CompilerParams(dimension_semantics=sem, vmem_limit_bytes=VMEM_LIMIT)


def _sigmoid(x):
    return 0.5 * jnp.tanh(0.5 * x) + 0.5


def _silu(x):
    h = 0.5 * x
    return h + h * jnp.tanh(h)


def _softplus(x):
    return jnp.maximum(x, 0.0) + jnp.log1p(jnp.exp(-jnp.abs(x)))


def _log_sigmoid(x):
    return jnp.minimum(x, 0.0) - jnp.log(1.0 + jnp.exp(-jnp.abs(x)))


def _split(x, n):
    out = []
    r = x
    for _ in range(n):
        p = r.astype(BF16)
        out.append(p)
        r = r - p.astype(F32)
    return out


def _dot(a, b):
    return jnp.dot(a, b, preferred_element_type=F32)


def _dot_nt(a, b):
    return lax.dot_general(a, b, (((1,), (1,)), ((), ())), preferred_element_type=F32)


def _dot_tn(a, b):
    return lax.dot_general(a, b, (((0,), (0,)), ((), ())), preferred_element_type=F32)


def _dot_exact_lhs(t, x, pieces):
    return sum(_dot(t, p) for p in _split(x, pieces))


def _dot_hp(a, b):
    ah, am = _split(a, 2)
    bh, bm = _split(b, 2)
    return _dot(ah, bh) + _dot(ah, bm) + _dot(am, bh)


def _causal(n, rev):
    t = lax.broadcasted_iota(jnp.int32, (n, n), 0)
    s = lax.broadcasted_iota(jnp.int32, (n, n), 1)
    return (s >= t) if rev else (s <= t)


def _rms(x):
    return x * lax.rsqrt(jnp.mean(x * x, axis=-1, keepdims=True) + EPS)


def _mod_kernel(c_ref, w_ref, b_ref, o_ref):
    o_ref[...] = _dot_hp(_silu(c_ref[...]), w_ref[...]) + b_ref[...]


def _modulation(c16, w_mod, b_mod, layer):
    rows, d = c16.shape
    n = w_mod.shape[-1]
    tn = 1536
    return pl.pallas_call(
        _mod_kernel,
        out_shape=jax.ShapeDtypeStruct((rows, n), F32),
        grid=(n // tn,),
        in_specs=[pl.BlockSpec((rows, d), lambda j: (0, 0)),
                  pl.BlockSpec((None, d, tn), lambda j: (layer, 0, j)),
                  pl.BlockSpec((None, 1, tn), lambda j: (layer, 0, j))],
        out_specs=pl.BlockSpec((rows, tn), lambda j: (0, j)),
        compiler_params=_cparams(("arbitrary",)),
        name="modulation",
    )(c16, w_mod, b_mod.reshape(b_mod.shape[0], 1, n))


def _mod_row(nb, k, ctx_tiles, tile0=0):
    return lambda b, i: (jnp.where(i + tile0 < ctx_tiles, nb, b) * 6 + k, 0, 0)


def _norm_mod_kernel(ctx_ref, x_ref, w_ref, sc_ref, sh_ref, xo_ref, ho_ref):
    x = jnp.where(pl.program_id(1) == 0, ctx_ref[0], x_ref[0])
    xo_ref[0] = x
    ho_ref[0] = (_rms(x) * w_ref[...] * (1.0 + sc_ref[0]) + sh_ref[0]).astype(ho_ref.dtype)


def _norm_mod(ctx, x, w, mods, k_shift, k_scale):
    nb, n_lat, d = x.shape
    n_ctx = ctx.shape[1]
    assert n_ctx == TILE
    t = n_ctx + n_lat
    tok = lambda b, i: (b, i, 0)
    return pl.pallas_call(
        _norm_mod_kernel,
        out_shape=(jax.ShapeDtypeStruct((nb, t, d), F32), jax.ShapeDtypeStruct((nb, t, d), BF16)),
        grid=(nb, t // TILE),
        in_specs=[pl.BlockSpec((1, TILE, d), lambda b, i: (b, 0, 0)),
                  pl.BlockSpec((1, TILE, d), lambda b, i: (b, jnp.maximum(i - 1, 0), 0)),
                  pl.BlockSpec((1, d), lambda b, i: (0, 0)),
                  pl.BlockSpec((1, 1, d), _mod_row(nb, k_scale, n_ctx // TILE)),
                  pl.BlockSpec((1, 1, d), _mod_row(nb, k_shift, n_ctx // TILE))],
        out_specs=(pl.BlockSpec((1, TILE, d), tok), pl.BlockSpec((1, TILE, d), tok)),
        compiler_params=_cparams(("parallel", "arbitrary")),
        name="norm_mod",
    )(ctx, x, w.reshape(1, d), mods, mods)


MM_STAGE = 256


def _mm_body(a_ref, w_ref, o_ref):
    a = a_ref[...]
    width = o_ref.shape[1]
    stage = min(MM_STAGE, width)
    for c0 in range(0, width, stage):
        o_ref[:, c0:c0 + stage] = _dot(a, w_ref[:, c0:c0 + stage]).astype(o_ref.dtype)
        yield


def _row_tile(m):
    return 512 if m % 512 == 0 else TILE


def _mm_part(a, w, layer, tn, out_dtype, tile0, n_tiles):
    m, k = a.shape
    tm = _row_tile(m)
    return (_mm_body,
            [pl.BlockSpec((tm, k), lambda j, i: (i, 0)),
             pl.BlockSpec((None, k, tn), lambda j, i: (layer, 0, j + tile0))],
            [a, w], [jax.ShapeDtypeStruct((m, n_tiles * tn), out_dtype)],
            [pl.BlockSpec((tm, tn), lambda j, i: (i, j))], [])


CONV_K = 5
CONV_ROWS = 128
CONV_X = SSD_WIDTH // LANES
CONV_BC = 2 * SSD_BC // LANES
CONV_QK = 2 * ML_QK_WIDTH // LANES
CV_QK = SSD_WIDTH
CV_BC = SSD_WIDTH + 2 * ML_QK_WIDTH
CV_WIDTH = CV_BC + 2 * SSD_BC


def _conv_body(u_ref, w_ref, b_ref, s_ref, o_ref, pad_ref, *, n_ctx):
    t, c = u_ref.shape[1], u_ref.shape[2]
    half = CONV_K // 2
    zeros = jnp.zeros((SUBLANES, c), F32)
    w = w_ref[...]
    bias = b_ref[...]
    post = s_ref[...]
    for s0, n in ((0, n_ctx), (n_ctx, t - n_ctx)):
        pad_ref[0:SUBLANES, :] = zeros
        pad_ref[SUBLANES:SUBLANES + n, :] = u_ref[0, s0:s0 + n, :].astype(F32)
        pad_ref[SUBLANES + n:2 * SUBLANES + n, :] = zeros
        for r0 in range(0, n, CONV_ROWS):
            acc = bias
            for j in range(CONV_K):
                lo = SUBLANES - half + j + r0
                acc = acc + w[j:j + 1, :] * pad_ref[lo:lo + CONV_ROWS, :]
            o_ref[0, s0 + r0:s0 + r0 + CONV_ROWS, :] = (_silu(acc) * post).astype(o_ref.dtype)
            yield


def _conv_part(u3, w, b, post_scale, n_ctx, grid):
    nb, t, width = u3.shape
    n_cb = width // LANES
    steps = grid[0] * grid[1]
    rep = steps // (nb * n_cb)
    assert steps == rep * nb * n_cb and n_cb == CONV_X + CONV_BC + CONV_QK
    blk = lambda j, i: (j * grid[1] + i) // rep
    chan = lambda j, i: blk(j, i) % n_cb
    out_chan = lambda c: jnp.where(c < CONV_X, c, jnp.where(c < CONV_X + CONV_BC, c + CONV_QK, c - CONV_BC))
    vec = lambda rows: pl.BlockSpec((rows, LANES), lambda j, i: (0, chan(j, i)))
    return (functools.partial(_conv_body, n_ctx=n_ctx),
            [pl.BlockSpec((1, t, LANES), lambda j, i: (blk(j, i) // n_cb, 0, chan(j, i))),
             vec(CONV_K), vec(1), vec(1)],
            [u3, w, b.reshape(1, width), post_scale.reshape(1, width)],
            [jax.ShapeDtypeStruct((nb, t, width), BF16)],
            [pl.BlockSpec((1, t, LANES), lambda j, i: (blk(j, i) // n_cb, 0, out_chan(chan(j, i))))],
            [pltpu.VMEM((t + 2 * SUBLANES, LANES), F32)])


def _tile_order(n_tiles, rev, ctx_tiles=1):
    if rev:
        return lambda i: jnp.where(i < ctx_tiles, ctx_tiles - 1 - i, n_tiles - 1 + ctx_tiles - i)
    return lambda i: i


SSD_PAIRS = SSD_HEADS // 2
SSD_B_OFF = SSD_HEADS
SSD_TILE = 256


def _ssd_gates(sm_ref, dtb_ref, alog_ref, n, both):
    lane = lax.broadcasted_iota(jnp.int32, (1, LANES), 1)
    dt = _softplus(sm_ref[0] + dtb_ref[...])
    la = dt * jnp.where(lane < 2 * SSD_HEADS, -jnp.exp(alog_ref[...]), 0.0)
    parts = _split(la, 3)
    upp = jnp.where(_causal(n, True), 1.0, 0.0).astype(BF16)
    cum = sum(_dot(upp, p) for p in parts)
    if both:
        low = jnp.where(_causal(n, False), 1.0, 0.0).astype(BF16)
        cum = jnp.where(lane < SSD_HEADS, sum(_dot(low, p) for p in parts), cum)
    return lane, dt, cum


def _expand(a, e):
    return sum(_dot(p, e) for p in _split(a, 2))


def _group_dup(v, g, lo):
    other = pltpu.roll(v, SSD_STATE, axis=1)
    return jnp.where(lo, v, other) if g == 0 else jnp.where(lo, other, v)


def _ssd_state_step(st_ref, j, bw, xp, elast, off):
    r = lax.broadcasted_iota(jnp.int32, (LANES, LANES), 0) < SSD_STATE
    c = lax.broadcasted_iota(jnp.int32, (LANES, LANES), 1) < SSD_HEAD_DIM
    dec = jnp.where(r, elast[:, off + 2 * j:off + 2 * j + 1], elast[:, off + 2 * j + 1:off + 2 * j + 2])
    st_ref[j] = jnp.where(r == c, dec * st_ref[j] + _dot_tn(bw.astype(BF16), xp), 0.0)


def _interleave(*bodies):
    live = list(bodies)
    while live:
        for body in list(live):
            if next(body, StopIteration) is StopIteration:
                live.remove(body)


def _ssd_states_body(x_ref, bc_ref, sm_ref, dtb_ref, alog_ref, eb_ref, o_ref, st_ref):
    n = x_ref.shape[1]

    @pl.when(pl.program_id(1) == 0)
    def _init():
        st_ref[...] = jnp.zeros_like(st_ref)

    o_ref[0, 0] = st_ref[...].astype(o_ref.dtype)
    yield
    lane, dt, cum = _ssd_gates(sm_ref, dtb_ref, alog_ref, n, False)
    lo = lane < SSD_HEAD_DIM
    last = cum[0:1, :]
    yield
    wst = _expand(jnp.exp(last - cum) * dt, eb_ref[...])
    elast = jnp.exp(last)
    b128 = bc_ref[0, :, :SSD_BC].astype(F32)
    for j in range(SSD_PAIRS):
        sl = slice(LANES * j, LANES * (j + 1))
        bw = _group_dup(b128, j // (SSD_PAIRS // 2), lo) * wst[:, sl]
        _ssd_state_step(st_ref, j, bw, x_ref[0, :, sl], elast, SSD_B_OFF)
        if j % 2:
            yield


def _ssd_out_body(x_ref, bc_ref, sm_ref, dtb_ref, alog_ref, ef_ref, eb_ref, d_ref, sb_ref, o_ref, st_ref):
    n = x_ref.shape[1]

    @pl.when(pl.program_id(1) == 0)
    def _init():
        st_ref[...] = jnp.zeros_like(st_ref)

    zero_b = jnp.zeros((), BF16)
    lo = lax.broadcasted_iota(jnp.int32, (1, LANES), 1) < SSD_HEAD_DIM
    b128_b = bc_ref[0, :, :SSD_BC]
    c128_b = bc_ref[0, :, SSD_BC:]
    cb_all = [_dot_nt(jnp.where(lo if g == 0 else jnp.logical_not(lo), c128_b, zero_b), b128_b)
              for g in range(2)]
    pair_x = lambda j: x_ref[0, :, LANES * j:LANES * (j + 1)]
    rhs_all = [jnp.concatenate([jnp.where(lo, pair_x(j), zero_b), jnp.where(lo, zero_b, pair_x(j)),
                                st_ref[j].astype(BF16), sb_ref[0, 0, j]], axis=0) for j in range(SSD_PAIRS)]
    yield
    lane, dt, cum = _ssd_gates(sm_ref, dtb_ref, alog_ref, n, True)
    is_f = lane < SSD_HEADS
    ldt = jnp.log(dt)
    dsum = jnp.log(dt + pltpu.roll(dt, LANES - SSD_B_OFF, axis=1))
    rt = (jnp.where(lane < 2 * SSD_HEADS, cum - ldt, pltpu.roll(dsum, 2 * SSD_HEADS, axis=1)) * LOG2E).T
    cum2 = cum * LOG2E
    yield
    last = jnp.where(is_f, cum[n - 1:n, :], cum[0:1, :])
    elast = jnp.exp(last)
    ecum = _split(jnp.exp(cum), 2)
    ecum_f = sum(_dot(p, ef_ref[...]) for p in ecum)
    ecum_b = sum(_dot(p, eb_ref[...]) for p in ecum)
    wst = _expand(jnp.exp(last - cum) * dt, ef_ref[...])
    bc = bc_ref[0].astype(F32)
    b128, c128 = bc[:, :SSD_BC], bc[:, SSD_BC:]
    ti = lax.broadcasted_iota(jnp.int32, (n, n), 0)
    si = lax.broadcasted_iota(jnp.int32, (n, n), 1)
    below, above = si < ti, si > ti
    half = SSD_PAIRS // 2
    yield
    for g in range(2):
        cb = cb_all[g]
        cdup = _group_dup(c128, g, lo)
        for j in range(g * half, (g + 1) * half):
            ms = []
            for h in (2 * j, 2 * j + 1):
                e_f = cum2[:, h:h + 1] - rt[h:h + 1, :]
                e_b = cum2[:, SSD_B_OFF + h:SSD_B_OFF + h + 1] - rt[SSD_B_OFF + h:SSD_B_OFF + h + 1, :]
                e = jnp.where(below, e_f, jnp.where(above, e_b, rt[2 * SSD_HEADS + h:2 * SSD_HEADS + h + 1, :]))
                ms.append((cb * jnp.exp2(e)).astype(BF16))
            sl = slice(LANES * j, LANES * (j + 1))
            cs_f = (cdup * ecum_f[:, sl]).astype(BF16)
            cs_b = (cdup * ecum_b[:, sl]).astype(BF16)
            lhs = jnp.concatenate(ms + [cs_f, cs_b], axis=1)
            o_ref[0, :, sl] = _dot(lhs, rhs_all[j]) + d_ref[:, sl] * pair_x(j).astype(F32)
            yield
    for j in range(SSD_PAIRS):
        sl = slice(LANES * j, LANES * (j + 1))
        _ssd_state_step(st_ref, j, _group_dup(b128, j // half, lo) * wst[:, sl], x_ref[0, :, sl], elast, 0)
        if j % 2:
            yield


def _run_together(name, grid, *parts, phases=None):
    n_in = [len(p[1]) for p in parts]
    n_out = [len(p[3]) for p in parts]
    n_scr = [len(p[5]) for p in parts]

    def kern(*refs):
        ins, outs, scr = refs[:sum(n_in)], refs[sum(n_in):sum(n_in) + sum(n_out)], refs[sum(n_in) + sum(n_out):]

        def run(key):
            bodies = []
            for k, p in enumerate(parts):
                take = lambda seq, counts: seq[sum(counts[:k]):sum(counts[:k + 1])]
                body = p[0][key] if isinstance(p[0], dict) else p[0]
                bodies.append(body(*take(ins, n_in), *take(outs, n_out), *take(scr, n_scr)))
            _interleave(*bodies)

        if phases is None:
            run(None)
        else:
            for pred, key in phases:
                pl.when(pred(pl.program_id(1)))(functools.partial(run, key))

    res = pl.pallas_call(
        kern,
        out_shape=tuple(s for p in parts for s in p[3]),
        grid=grid,
        in_specs=[s for p in parts for s in p[1]],
        out_specs=tuple(s for p in parts for s in p[4]),
        scratch_shapes=[s for p in parts for s in p[5]],
        compiler_params=_cparams(("arbitrary", "arbitrary")),
        name=name,
    )(*[a for p in parts for a in p[2]])
    return [list(res[sum(n_out[:k]):sum(n_out[:k + 1])]) for k in range(len(parts))]


def _ssd_parts(cvs, sm3, dt_bias, a_log, d_e, n_ctx):
    nb, t, _ = cvs.shape
    tile = SSD_TILE
    nt = t // tile
    row = lambda v: jnp.pad(v.reshape(1, -1), ((0, 0), (0, LANES - 2 * SSD_HEADS)))
    const = lambda b, i: (0, 0)
    specs = lambda order: [
        pl.BlockSpec((1, tile, SSD_WIDTH), lambda b, i: (b, order(i), 0)),
        pl.BlockSpec((1, tile, 2 * SSD_BC), lambda b, i: (b, order(i), CV_BC // (2 * SSD_BC))),
        pl.BlockSpec((1, tile, LANES), lambda b, i: (b, order(i), 0)),
        pl.BlockSpec((1, LANES), const),
        pl.BlockSpec((1, LANES), const)]
    st_block = (1, 1, SSD_PAIRS, LANES, LANES)
    args = (cvs, cvs, sm3, row(dt_bias), row(a_log))
    sel = np.zeros((2, LANES, SSD_WIDTH), np.float32)
    for h in range(SSD_HEADS):
        sel[0, h, h * SSD_HEAD_DIM:(h + 1) * SSD_HEAD_DIM] = 1.0
        sel[1, SSD_B_OFF + h, h * SSD_HEAD_DIM:(h + 1) * SSD_HEAD_DIM] = 1.0
    e_f, e_b = jnp.asarray(sel[0], BF16), jnp.asarray(sel[1], BF16)
    e_spec = pl.BlockSpec((LANES, SSD_WIDTH), const)
    bwd = _tile_order(nt, True, n_ctx // tile)
    fwd = _tile_order(nt, False)
    scratch = [pltpu.VMEM(st_block[2:], F32)]
    states = (_ssd_states_body, specs(bwd) + [e_spec], list(args) + [e_b],
              [jax.ShapeDtypeStruct((nb, nt) + st_block[2:], BF16)],
              [pl.BlockSpec(st_block, lambda b, i: (b, bwd(i), 0, 0, 0))], scratch)
    out = lambda states_b: (
        _ssd_out_body,
        specs(fwd) + [e_spec, e_spec, pl.BlockSpec((1, SSD_WIDTH), const),
                      pl.BlockSpec(st_block, lambda b, i: (b, i, 0, 0, 0))],
        list(args) + [e_f, e_b, d_e, states_b],
        [jax.ShapeDtypeStruct((nb, t, SSD_WIDTH), F32)],
        [pl.BlockSpec((1, tile, SSD_WIDTH), lambda b, i: (b, i, 0))], scratch)
    return states, out


ML_GATE = 32
ML_ND = 2 * ML_HEADS
ML_AUG = ML_V_DIM + LANES


def _ml_gates(sm_ref, ib_ref, fb_ref, n, both):
    lane = lax.broadcasted_iota(jnp.int32, (1, LANES), 1)
    valid = (lane >= ML_GATE) & (lane < ML_GATE + ML_ND)
    is_f = lane < ML_GATE + ML_HEADS
    sm = sm_ref[0]
    li = sm + ib_ref[...]
    lf = pltpu.roll(_log_sigmoid(sm + fb_ref[...]), LANES - ML_ND, axis=1)
    parts = _split(jnp.where(valid, lf, 0.0), 3)
    upp = jnp.where(_causal(n, True), 1.0, 0.0).astype(BF16)
    bcum = sum(_dot(upp, p) for p in parts)
    if both:
        low = jnp.where(_causal(n, False), 1.0, 0.0).astype(BF16)
        bcum = jnp.where(is_f, sum(_dot(low, p) for p in parts), bcum)
    return valid, is_f, bcum, jnp.where(valid, li - bcum, 0.0)


def _ml_state_step(cn_ref, h, cn, keep, k, ws_dense, v_aug):
    w3 = jnp.concatenate([ws_dense.astype(BF16)] * (ML_AUG // LANES), axis=1)
    cn_ref[h] = keep * cn + _dot_tn(k, w3 * v_aug)


def _ml_v_aug(v_ref, h, n):
    return jnp.concatenate([v_ref[0, :, ML_V_DIM * h:ML_V_DIM * (h + 1)], jnp.ones((n, LANES), BF16)], axis=1)


def _ml_states_body(qk_ref, v_ref, sm_ref, ib_ref, fb_ref, sel_ref, cn_out, m_out, cn_ref, m_ref):
    n = qk_ref.shape[1]

    @pl.when(pl.program_id(1) == 0)
    def _init():
        cn_ref[...] = jnp.zeros_like(cn_ref)
        m_ref[...] = jnp.zeros_like(m_ref)

    cn_out[0, 0] = cn_ref[...].astype(cn_out.dtype)
    m_out[0, 0] = m_ref[...]
    yield
    valid, is_f, bcum, a = _ml_gates(sm_ref, ib_ref, fb_ref, n, False)
    m_prev = m_ref[0:1, :]
    g_last = jnp.maximum(m_prev, jnp.max(a, axis=0, keepdims=True))
    yield
    ws = _expand(jnp.exp(a - g_last), sel_ref[:, ML_HEADS * LANES:])
    keep = jnp.exp(m_prev - g_last)
    for h in range(ML_HEADS):
        lane_b = ML_GATE + ML_HEADS + h
        k = qk_ref[0, :, ML_QK_WIDTH + ML_QK_DIM * h:ML_QK_WIDTH + ML_QK_DIM * (h + 1)]
        _ml_state_step(cn_ref, h, cn_ref[h], keep[:, lane_b:lane_b + 1], k,
                       ws[:, LANES * h:LANES * (h + 1)], _ml_v_aug(v_ref, h, n))
        yield
    m_ref[...] = jnp.broadcast_to(bcum[0:1, :] + g_last, m_ref.shape)


def _ml_out_body(qk_ref, v_ref, sm_ref, ib_ref, fb_ref, sel_ref, cnb_ref, mb_ref, o_ref, cn_ref, m_ref):
    n = qk_ref.shape[1]

    @pl.when(pl.program_id(1) == 0)
    def _init():
        cn_ref[...] = jnp.zeros_like(cn_ref)
        m_ref[...] = jnp.zeros_like(m_ref)

    head_q = lambda h: qk_ref[0, :, ML_QK_DIM * h:ML_QK_DIM * (h + 1)]
    head_k = lambda h: qk_ref[0, :, ML_QK_WIDTH + ML_QK_DIM * h:ML_QK_WIDTH + ML_QK_DIM * (h + 1)]
    qk_all = [_dot_nt(head_q(h), head_k(h)) for h in range(ML_HEADS)]
    inter_all = [_dot(head_q(h), jnp.concatenate([cn_ref[h].astype(BF16), cnb_ref[0, 0, h]], axis=1))
                 for h in range(ML_HEADS)]
    yield
    valid, is_f, bcum, a = _ml_gates(sm_ref, ib_ref, fb_ref, n, True)
    m_prev = jnp.where(is_f, m_ref[0:1, :], mb_ref[0, 0, 0:1, :])
    a_t = a.T
    pre = suf = a_t[ML_GATE:ML_GATE + ML_ND, :]
    pos = lax.broadcasted_iota(jnp.int32, (ML_ND, n), 1)
    k = 1
    while k < n:
        pre = jnp.maximum(pre, jnp.where(pos >= k, pltpu.roll(pre, k, axis=1), -jnp.inf))
        suf = jnp.maximum(suf, jnp.where(pos < n - k, pltpu.roll(suf, n - k, axis=1), -jnp.inf))
        k *= 2
    run = jnp.where(lax.broadcasted_iota(jnp.int32, (ML_ND, n), 0) < ML_HEADS, pre, suf)
    run = jnp.concatenate([jnp.zeros((ML_GATE, n), F32), run,
                           jnp.zeros((LANES - ML_GATE - ML_ND, n), F32)], axis=0).T
    g = jnp.maximum(m_prev, run)
    m_t = bcum + g
    yield
    floor = jnp.exp(-m_t)
    dense = lambda x, lane_: jnp.broadcast_to(x[:, lane_:lane_ + 1], (n, LANES))
    diag_t = jnp.exp(a - g).T
    g_last = g[n - 1:n, :]
    ws = _expand(jnp.exp(a - g_last), sel_ref[:, :ML_HEADS * LANES])
    keep = jnp.exp(m_prev - g_last)
    ti = lax.broadcasted_iota(jnp.int32, (n, n), 0)
    si = lax.broadcasted_iota(jnp.int32, (n, n), 1)
    not_above, above, on_diag = si <= ti, si > ti, si == ti
    wide = lambda x: jnp.concatenate([x] * (n // LANES), axis=1)
    for h in range(ML_HEADS):
        lanes = (ML_GATE + h, ML_GATE + ML_HEADS + h)
        g_d = [dense(g, ln) for ln in lanes]
        qk, inter = qk_all[h], inter_all[h]
        e = jnp.where(not_above, a_t[lanes[0]:lanes[0] + 1, :] - wide(g_d[0]),
                      a_t[lanes[1]:lanes[1] + 1, :] - wide(g_d[1]))
        p = qk * jnp.exp(e)
        p_f = jnp.where(not_above, p, 0.0).astype(BF16)
        p_b = jnp.where(above, p, jnp.where(on_diag, qk * diag_t[lanes[1]:lanes[1] + 1, :], 0.0)).astype(BF16)
        intra = _dot(jnp.concatenate([p_f, p_b], axis=0), _ml_v_aug(v_ref, h, n))
        out = None
        for d in range(2):
            w_inter = jnp.exp(m_prev[:, lanes[d]:lanes[d] + 1] - g_d[d])
            s = (intra[n * d:n * (d + 1)] + jnp.concatenate([w_inter] * (ML_AUG // LANES), axis=1)
                 * inter[:, ML_AUG * d:ML_AUG * (d + 1)])
            rn = 1.0 / jnp.maximum(jnp.abs(s[:, ML_V_DIM:]), dense(floor, lanes[d]))
            hid = s[:, :ML_V_DIM] * jnp.concatenate([rn] * (ML_V_DIM // LANES), axis=1)
            out = hid if out is None else out + hid
        o_ref[0, :, ML_V_DIM * h:ML_V_DIM * (h + 1)] = out
        yield
    for h in range(ML_HEADS):
        lane_f = ML_GATE + h
        _ml_state_step(cn_ref, h, cn_ref[h], keep[:, lane_f:lane_f + 1], head_k(h),
                       ws[:, LANES * h:LANES * (h + 1)], _ml_v_aug(v_ref, h, n))
        yield
    m_ref[...] = jnp.broadcast_to(m_t[n - 1:n, :], m_ref.shape)


def _mlstm_parts(cvm, p3, sm3, i_bias, f_bias, n_ctx):
    nb, t, _ = cvm.shape
    nt = t // TILE
    assert _SM_OFF["i_f"] == ML_GATE and _SM_OFF["f_f"] == ML_GATE + ML_ND
    row = lambda v, off: jnp.pad(v.reshape(1, -1), ((0, 0), (off, LANES - off - ML_ND)))
    sel = np.zeros((LANES, ML_ND * LANES), np.float32)
    for r in range(ML_ND):
        sel[ML_GATE + r, r * LANES:(r + 1) * LANES] = 1.0
    const = lambda b, i: (0, 0)
    specs = lambda order: [
        pl.BlockSpec((1, TILE, 2 * ML_QK_WIDTH), lambda b, i: (b, order(i), CV_QK // (2 * ML_QK_WIDTH))),
        pl.BlockSpec((1, TILE, ML_V_WIDTH), lambda b, i: (b, order(i), _P_OFF["m_v"] // ML_V_WIDTH)),
        pl.BlockSpec((1, TILE, LANES), lambda b, i: (b, order(i), 0)),
        pl.BlockSpec((1, LANES), const),
        pl.BlockSpec((1, LANES), const),
        pl.BlockSpec((LANES, ML_ND * LANES), const)]
    args = (cvm, p3, sm3, row(i_bias, ML_GATE), row(f_bias, ML_GATE + ML_ND), jnp.asarray(sel, BF16))
    cn_block = (1, 1, ML_HEADS, ML_QK_DIM, ML_AUG)
    m_block = (1, 1, SUBLANES, LANES)
    scratch = [pltpu.VMEM(cn_block[2:], F32), pltpu.VMEM(m_block[2:], F32)]
    bwd = _tile_order(nt, True, n_ctx // TILE)
    fwd = _tile_order(nt, False)
    st_idx = lambda b, i: (b, bwd(i)) + (0,) * 3
    states = (_ml_states_body, specs(bwd), list(args),
              [jax.ShapeDtypeStruct((nb, nt) + cn_block[2:], BF16),
               jax.ShapeDtypeStruct((nb, nt) + m_block[2:], F32)],
              [pl.BlockSpec(cn_block, st_idx), pl.BlockSpec(m_block, lambda b, i: (b, bwd(i), 0, 0))],
              scratch)
    out = lambda cn_b, m_b: (
        _ml_out_body,
        specs(fwd) + [pl.BlockSpec(cn_block, lambda b, i: (b, i, 0, 0, 0)),
                      pl.BlockSpec(m_block, lambda b, i: (b, i, 0, 0))],
        list(args) + [cn_b, m_b],
        [jax.ShapeDtypeStruct((nb, t, ML_V_WIDTH), F32)],
        [pl.BlockSpec((1, TILE, ML_V_WIDTH), lambda b, i: (b, i, 0))], scratch)
    return states, out


GLA_SUB = 256
GLA_NCH = GLA_SUB // GLA_CHUNK
GLA_COLS = BF16_SUBLANES


def _gla_layout(is_ctx, lat_rows):
    r = np.arange(GLA_SUB)
    if is_ctx:
        return r // GLA_CHUNK, r % GLA_CHUNK
    col = r % SUBLANES
    cpc = GLA_CHUNK // lat_rows
    return col // cpc, (col % cpc) * lat_rows + r // SUBLANES


def _gla_consts(is_ctx, lat_rows, rev):
    ch, pos = _gla_layout(is_ctx, lat_rows)
    same = ch[:, None] == ch[None, :]
    before = (pos[None, :] >= pos[:, None]) if rev else (pos[None, :] <= pos[:, None])
    tri = (same & before).astype(np.float32)
    cmask = np.stack([np.repeat((ch == j)[:, None], LANES, axis=1) for j in range(GLA_NCH)])
    return tri, cmask.astype(np.float32)


def _gla_row(is_ctx, lat_rows, j, p):
    ch, pos = _gla_layout(is_ctx, lat_rows)
    return int(np.nonzero((ch == j) & (pos == p))[0][0])


def _per_chunk_rows(b, is_ctx, lat_rows, p):
    rows = [b[_gla_row(is_ctx, lat_rows, j, p):_gla_row(is_ctx, lat_rows, j, p) + 1, :]
            for j in range(GLA_NCH)]
    w = b.shape[1]
    if is_ctx:
        full = jnp.concatenate([jnp.broadcast_to(r, (GLA_CHUNK, w)) for r in rows], axis=0)
    else:
        rep = SUBLANES // GLA_NCH
        pat = jnp.concatenate([jnp.broadcast_to(r, (rep, w)) for r in rows], axis=0)
        full = jnp.broadcast_to(pat[None], (GLA_SUB // SUBLANES, SUBLANES, w)).reshape(GLA_SUB, w)
    return rows, full


def _gla_sub(q, k, v, araw, aup, abias, tri_b, tri_f, cmask_ref, st_ref, store, *, rev, is_ctx, lat_rows):
    want_out = store is not None
    g = _log_sigmoid(_dot_hp(araw, aup) + abias) * (1.0 / GLA_TAU)
    b = _dot_exact_lhs(tri_b, g, 2)
    yield
    lasts, last = _per_chunk_rows(b, is_ctx, lat_rows, 0 if rev else GLA_CHUNK - 1)
    kl = (k * jnp.exp(last - b)).astype(BF16)
    if want_out:
        _, ref = _per_chunk_rows(b, is_ctx, lat_rows, GLA_CHUNK // 2)
        qs = q * (GLA_K_DIM ** -0.5)
        qe = (qs * jnp.exp(b - ref)).astype(BF16)
        ke = (k * jnp.exp(ref - b)).astype(BF16)
        qb = (qs * jnp.exp(b)).astype(BF16)
        visible = tri_f > 0.0
    yield
    order = range(GLA_NCH - 1, -1, -1) if rev else range(GLA_NCH)
    outs = []
    for h in range(GLA_HEADS):
        ks = slice(GLA_K_DIM * h, GLA_K_DIM * (h + 1))
        vh = v[:, GLA_V_DIM * h:GLA_V_DIM * (h + 1)]
        klm = jnp.concatenate([kl[:, ks] * cmask_ref[j] for j in range(GLA_NCH)], axis=1)
        upd = _dot_tn(vh, klm)
        s = st_ref[h]
        s_in = [None] * GLA_NCH
        for j in order:
            s_in[j] = s.astype(BF16)
            s = s * jnp.exp(lasts[j][:, ks]) + upd[:, GLA_K_DIM * j:GLA_K_DIM * (j + 1)]
        st_ref[h] = s
        if want_out:
            att = jnp.where(visible, _dot_nt(qe[:, ks], ke[:, ks]), 0.0).astype(BF16)
            qbm = jnp.concatenate([qb[:, ks] * cmask_ref[j] for j in range(GLA_NCH)], axis=1)
            outs.append(_dot(att, vh) + _dot_nt(qbm, jnp.concatenate(s_in, axis=1)))
        yield
    if want_out:
        store(jnp.concatenate(outs, axis=1))


def _gla_ctx_body(qc_ref, kc_ref, vc_ref, sc_ref, ql_ref, kl_ref, vl_ref, sl_ref,
                  aup_ref, ab_ref, tcb_ref, tcf_ref, cmc_ref, tlb_ref, tlf_ref, cml_ref,
                  o_ref, st_ref, ctxo_ref, *, rev, ctx_rows, lat_rows, n_cblk, ctx_out):
    st_ref[...] = jnp.zeros_like(st_ref)

    def store(o):
        ctxo_ref[...] = o

    yield from _gla_sub(qc_ref[0].astype(F32), kc_ref[0].astype(F32), vc_ref[0], sc_ref[0],
                        aup_ref[...], ab_ref[...], tcb_ref[...], tcf_ref[...], cmc_ref, st_ref,
                        store if ctx_out else None, rev=rev, is_ctx=True, lat_rows=lat_rows)


def _gla_lat_body(qc_ref, kc_ref, vc_ref, sc_ref, ql_ref, kl_ref, vl_ref, sl_ref,
                  aup_ref, ab_ref, tcb_ref, tcf_ref, cmc_ref, tlb_ref, tlf_ref, cml_ref,
                  o_ref, st_ref, ctxo_ref, *, rev, ctx_rows, lat_rows, n_cblk, ctx_out):
    i = pl.program_id(1)
    r0, r1 = ctx_rows, ctx_rows + lat_rows
    cblk = (n_cblk - i) if rev else (i - 1)
    halves = range(GLA_COLS // SUBLANES)
    for half in (reversed(halves) if rev else halves):
        cs = slice(SUBLANES * half, SUBLANES * (half + 1))
        take = lambda r: r[0, r0:r1].astype(F32)[:, cs, :].reshape(GLA_SUB, r.shape[-1])

        def store(o, cs=cs):
            o_ref[0, r0:r1, cs, :] = o.reshape(lat_rows, SUBLANES, GLA_V_WIDTH)

        yield from _gla_sub(take(ql_ref), take(kl_ref), take(vl_ref).astype(BF16), take(sl_ref),
                            aup_ref[...], ab_ref[...], tlb_ref[...], tlf_ref[...], cml_ref, st_ref,
                            store, rev=rev, is_ctx=False, lat_rows=lat_rows)
    for r in range(ctx_rows):
        if ctx_out:
            start = pl.multiple_of(r * GRID_W + cblk * GLA_COLS, GLA_COLS)
            o_ref[0, r, :, :] = ctxo_ref[pl.ds(start, GLA_COLS), :]
        else:
            o_ref[0, r, :, :] = jnp.zeros((GLA_COLS, GLA_V_WIDTH), F32)


GLA_PHASES = ((lambda i: i == 0, "ctx"), (lambda i: i > 0, "lat"))


def _gla_part(p3, sm3, a_up, a_bias, n_ctx, rev, ctx_out):
    nb, t, ncol = p3.shape
    rows = t // GRID_W
    ctx_rows = n_ctx // GRID_W
    lat_rows = rows - ctx_rows
    n_cblk = GRID_W // GLA_COLS
    p4 = p3.reshape(nb, rows, GRID_W, ncol)
    sm4 = sm3.reshape(nb, rows, GRID_W, LANES)
    a_off = _SM_OFF["a_b"] if rev else _SM_OFF["a_f"]
    aup = jnp.pad(a_up, ((a_off, LANES - a_off - GLA_RANK), (0, 0)))
    cblk = lambda i: jnp.where(i == 0, n_cblk - 1 if rev else 0, (n_cblk - i) if rev else (i - 1))
    ctx = lambda blk: (lambda b, i: (b, 0, blk))
    lat = lambda blk: (lambda b, i: (b, 0, cblk(i), blk))
    const2 = lambda b, i: (0, 0)
    const3 = lambda b, i: (0, 0, 0)
    widths = (GLA_K_WIDTH, GLA_K_WIDTH, GLA_V_WIDTH)
    offs = (_P_OFF["g_q"], _P_OFF["g_k"], _P_OFF["g_v"])
    consts = []
    const_specs = []
    for is_ctx in (True, False):
        tri, cmask = _gla_consts(is_ctx, lat_rows, rev)
        consts += [jnp.asarray(tri, BF16), jnp.asarray(tri, F32), jnp.asarray(cmask, BF16)]
        const_specs += [pl.BlockSpec((GLA_SUB, GLA_SUB), const2), pl.BlockSpec((GLA_SUB, GLA_SUB), const2),
                        pl.BlockSpec((GLA_NCH, GLA_SUB, LANES), const3)]
    out_spec = pl.BlockSpec((1, rows, GLA_COLS, GLA_V_WIDTH), lat(0))
    in_specs = ([pl.BlockSpec((1, n_ctx, w), ctx(o // w)) for w, o in zip(widths, offs)]
                + [pl.BlockSpec((1, n_ctx, LANES), ctx(0))]
                + [pl.BlockSpec((1, rows, GLA_COLS, w), lat(o // w)) for w, o in zip(widths, offs)]
                + [pl.BlockSpec((1, rows, GLA_COLS, LANES), lat(0))]
                + [pl.BlockSpec((LANES, GLA_K_WIDTH), const2), pl.BlockSpec((1, GLA_K_WIDTH), const2)]
                + const_specs)
    args = [p3, p3, p3, sm3, p4, p4, p4, sm4, aup, a_bias.reshape(1, GLA_K_WIDTH)] + consts
    static = dict(rev=rev, ctx_rows=ctx_rows, lat_rows=lat_rows, n_cblk=n_cblk, ctx_out=ctx_out)
    bodies = {"ctx": functools.partial(_gla_ctx_body, **static),
              "lat": functools.partial(_gla_lat_body, **static)}
    return (bodies, in_specs, args,
            [jax.ShapeDtypeStruct((nb, rows, GRID_W, GLA_V_WIDTH), F32)], [out_spec],
            [pltpu.VMEM((GLA_HEADS, GLA_V_DIM, GLA_K_DIM), F32), pltpu.VMEM((n_ctx, GLA_V_WIDTH), F32)])


def _group_rmsnorm(y, groups):
    width = y.shape[-1] // groups
    ones = jnp.ones((width, LANES), BF16)
    out = []
    for g in range(groups):
        yg = y[:, width * g:width * (g + 1)]
        ms = _dot((yg * yg).astype(BF16), ones) * (1.0 / width)
        out.append(yg * jnp.concatenate([lax.rsqrt(ms + EPS)] * (width // LANES), axis=1))
    return jnp.concatenate(out, axis=1)


def _post_kernel(x_ref, y_ref, h_ref, of_ref, ob_ref, z_ref, mo_ref, gg_ref, gs_ref, gm_ref, gl_ref,
                 nws_ref, nwm_ref, nwg_ref, wbs_ref, wbm_ref, wbg_ref, wout_ref, g1_ref,
                 nwf_ref, sc2_ref, sh2_ref, xo_ref, ho_ref):
    y_ssd = (_group_rmsnorm(y_ref[0] * _silu(z_ref[0]), 2) * nws_ref[...]).astype(BF16)
    y_ml = (_group_rmsnorm(h_ref[0], ML_HEADS) * nwm_ref[...] * _sigmoid(mo_ref[0])).astype(BF16)
    y_gla = (_group_rmsnorm(of_ref[0] + ob_ref[0], GLA_HEADS) * nwg_ref[...] * _silu(gg_ref[0])).astype(BF16)
    merged = (_sigmoid(gs_ref[0]) * _dot(y_ssd, wbs_ref[...])
              + _sigmoid(gm_ref[0]) * _dot(y_ml, wbm_ref[...])
              + _sigmoid(gl_ref[0]) * _dot(y_gla, wbg_ref[...]))
    x_new = x_ref[0] + g1_ref[0] * _dot(merged.astype(BF16), wout_ref[...])
    xo_ref[0] = x_new
    ho_ref[0] = (_rms(x_new) * nwf_ref[...] * (1.0 + sc2_ref[0]) + sh2_ref[0]).astype(ho_ref.dtype)


def _post(x, scans, p3, norm_ws, w_bs, w_out, layer, norm_ffn_w, mods, n_ctx, tile0):
    nb, t, d = x.shape
    nt = t // TILE - tile0
    ctx_tiles = n_ctx // TILE
    tok = lambda blk: (lambda b, i: (b, i + tile0, blk))
    out = lambda b, i: (b, i, 0)
    const = lambda b, i: (0, 0)
    tok_spec = lambda blk: pl.BlockSpec((1, TILE, d), tok(blk))
    w_spec = pl.BlockSpec((None, d, d), lambda b, i: (layer, 0, 0), pipeline_mode=pl.Buffered(1))
    vec = pl.BlockSpec((1, d), const)
    mod = lambda k: pl.BlockSpec((1, 1, d), _mod_row(nb, k, ctx_tiles, tile0))
    names = ("s_z", "m_o", "g_g", "gate_ssd", "gate_ml", "gate_gla")
    in_specs = ([tok_spec(0)] * (1 + len(scans)) + [tok_spec(_P_OFF[nm] // d) for nm in names]
                + [vec] * 3 + [w_spec] * 4 + [mod(2), vec, mod(4), mod(3)])
    return pl.pallas_call(
        _post_kernel,
        out_shape=(jax.ShapeDtypeStruct((nb, nt * TILE, d), F32),
                   jax.ShapeDtypeStruct((nb, nt * TILE, d), BF16)),
        grid=(nb, nt),
        in_specs=in_specs,
        out_specs=(pl.BlockSpec((1, TILE, d), out), pl.BlockSpec((1, TILE, d), out)),
        compiler_params=_cparams(("parallel", "parallel")),
        name="post",
    )(x, *scans, *([p3] * 6), *[w.reshape(1, d) for w in norm_ws], *w_bs, w_out, mods,
      norm_ffn_w.reshape(1, d), mods, mods)


def _ffn_in_kernel(a_ref, w_ref, o_ref):
    acc = _dot(a_ref[...], w_ref[...])
    half = acc.shape[1] // 2
    o_ref[...] = (_silu(acc[:, :half]) * acc[:, half:]).astype(o_ref.dtype)


def _ffn_in(h, w_gu, layer, half):
    m, k = h.shape
    n_half = w_gu.shape[-1] // 2
    tm = _row_tile(m)
    return pl.pallas_call(
        _ffn_in_kernel,
        out_shape=jax.ShapeDtypeStruct((m, n_half), BF16),
        grid=(n_half // half, m // tm),
        in_specs=[pl.BlockSpec((tm, k), lambda j, i: (i, 0)),
                  pl.BlockSpec((None, k, 2 * half), lambda j, i: (layer, 0, j))],
        out_specs=pl.BlockSpec((tm, half), lambda j, i: (i, j)),
        compiler_params=_cparams(("parallel", "parallel")),
        name="ffn_in",
    )(h, w_gu)


def _ffn_out_kernel(a_ref, w_ref, x_ref, g_ref, nw_ref, sc_ref, sh_ref, xo_ref, ho_ref):
    x_new = x_ref[0] + g_ref[0] * _dot(a_ref[0], w_ref[...])
    xo_ref[0] = x_new
    ho_ref[0] = (_rms(x_new) * nw_ref[...] * (1.0 + sc_ref[0]) + sh_ref[0]).astype(ho_ref.dtype)


def _ffn_out_last_kernel(a_ref, w_ref, x_ref, g_ref, nw_ref, o_ref):
    x_new = x_ref[0] + g_ref[0] * _dot(a_ref[0], w_ref[...])
    o_ref[0] = _rms(x_new) * nw_ref[...]


def _ffn_out(a, w, layer, x, mods, n_ctx, tile0, next_norm_w, next_mods):
    nb, t, d = x.shape
    k = a.shape[-1]
    ctx_tiles = n_ctx // TILE
    tok = lambda b, i: (b, i, 0)
    mod = lambda k_: pl.BlockSpec((1, 1, d), _mod_row(nb, k_, ctx_tiles, tile0))
    in_specs = [pl.BlockSpec((1, TILE, k), tok),
                pl.BlockSpec((None, k, d), lambda b, i: (layer, 0, 0), pipeline_mode=pl.Buffered(1)),
                pl.BlockSpec((1, TILE, d), tok),
                mod(5),
                pl.BlockSpec((1, d), lambda b, i: (0, 0))]
    args = [a, w, x, mods, next_norm_w.reshape(1, d)]
    if next_mods is None:
        body = _ffn_out_last_kernel
        out_shape = jax.ShapeDtypeStruct((nb, t, d), F32)
        out_specs = pl.BlockSpec((1, TILE, d), tok)
    else:
        body = _ffn_out_kernel
        in_specs += [mod(1), mod(0)]
        args += [next_mods, next_mods]
        out_shape = (jax.ShapeDtypeStruct((nb, t, d), F32), jax.ShapeDtypeStruct((nb, t, d), BF16))
        out_specs = (pl.BlockSpec((1, TILE, d), tok), pl.BlockSpec((1, TILE, d), tok))
    return pl.pallas_call(
        body,
        out_shape=out_shape,
        grid=(nb, t // TILE),
        in_specs=in_specs,
        out_specs=out_specs,
        compiler_params=_cparams(("parallel", "parallel")),
        name="ffn_out",
    )(*args)


def _proj_weights(w_in):
    cols = lambda names: [w_in[..., _IN_OFF[nm]:_IN_OFF[nm] + _IN_W[nm]] for nm in names]
    main = jnp.concatenate(cols(_P_ORDER), axis=-1)
    small = jnp.pad(jnp.concatenate(cols(_SMALL), axis=-1), ((0, 0), (0, 0), (0, LANES - N_SMALL_USED)))
    return main.astype(BF16), small.astype(BF16)


def _ffn_weight(w_ffn_in, half):
    d_ff = w_ffn_in.shape[-1] // 2
    cols = []
    for j in range(d_ff // half):
        cols.append(w_ffn_in[..., j * half:(j + 1) * half])
        cols.append(w_ffn_in[..., d_ff + j * half:d_ff + (j + 1) * half])
    return jnp.concatenate(cols, axis=-1).astype(BF16)


def kernel(x, c, ctx, c_ctx, w_mod, b_mod, norm_mix_w, norm_ffn_w, w_in, ssd_conv_w, ssd_conv_b, ssd_dt_bias, ssd_a_log, ssd_d, ssd_norm_w, ml_conv_w, ml_conv_b, ml_i_bias, ml_f_bias, ml_norm_w, gla_a_up, gla_a_bias, gla_norm_w, w_b_ssd, w_b_ml, w_b_gla, w_out, w_ffn_in, w_ffn_out, final_norm_w):
    nb, n_lat, d = x.shape
    n_ctx = ctx.shape[1]
    t = n_ctx + n_lat
    depth = w_in.shape[0]
    d_ff = w_ffn_out.shape[1]
    assert n_ctx == TILE == GLA_SUB and n_lat % TILE == 0 and n_lat // GRID_W == 32
    ffn_half = d_ff // 2

    c16 = jnp.pad(jnp.concatenate([c, c_ctx[None]], axis=0), ((0, 2 * SUBLANES - nb - 1), (0, 0)))
    mods = [_modulation(c16, w_mod, b_mod, l).reshape(2 * SUBLANES * 6, 1, d) for l in range(depth)]
    xs, h = _norm_mod(ctx, x, norm_mix_w[0], mods[0], 0, 1)
    conv_w = jnp.concatenate([ssd_conv_w, ml_conv_w], axis=-1)
    conv_b = jnp.concatenate([ssd_conv_b, ml_conv_b], axis=-1)
    conv_post = jnp.concatenate([jnp.ones((ssd_conv_w.shape[-1] + ML_QK_WIDTH,), F32),
                                 jnp.full((ML_QK_WIDTH,), ML_QK_DIM ** -0.5, F32)])
    w_main, w_small = _proj_weights(w_in)
    w_gu = _ffn_weight(w_ffn_in, ffn_half)
    w_bs = [w.astype(BF16) for w in (w_b_ssd, w_b_ml, w_b_gla)]
    w_out_b = w_out.astype(BF16)
    w_ffn_out_b = w_ffn_out.astype(BF16)
    for l in range(depth):
        last = l == depth - 1
        h2d = h.reshape(nb * t, d)
        tn = N_PROJ // 5
        conv_tile = _P_OFF["s_x"] // tn
        assert conv_tile * tn == _P_OFF["s_x"] and N_PROJ - _P_OFF["s_x"] == tn == CV_WIDTH
        m_tiles = nb * t // _row_tile(nb * t)
        (pc,), (sm,) = _run_together("proj_first", (1, m_tiles),
                                     _mm_part(h2d, w_main, l, tn, BF16, conv_tile, 1),
                                     _mm_part(h2d, w_small, l, LANES, F32, 0, 1))
        pc3, sm3 = pc.reshape(nb, t, tn), sm.reshape(nb, t, LANES)
        grid = (conv_tile, m_tiles)
        (p,), (cv,) = _run_together(
            "proj_conv", grid, _mm_part(h2d, w_main, l, tn, BF16, 0, conv_tile),
            _conv_part(pc3, conv_w[l], conv_b[l], conv_post, n_ctx, grid))
        p3 = p.reshape(nb, t, conv_tile * tn)
        d_e = jnp.repeat(ssd_d[l], SSD_HEAD_DIM).reshape(1, SSD_WIDTH)
        ssd_states, ssd_out = _ssd_parts(cv, sm3, ssd_dt_bias[l], ssd_a_log[l], d_e, n_ctx)
        ml_states, ml_out = _mlstm_parts(cv, p3, sm3, ml_i_bias[l], ml_f_bias[l], n_ctx)
        grid = (nb, t // TILE)
        ssd_st, ml_st = _run_together("bwd_states", grid, ssd_states, ml_states)
        (y,), (hm,) = _run_together("ssd_mlstm", grid, ssd_out(*ssd_st), ml_out(*ml_st))
        gla = [_gla_part(p3, sm3, gla_a_up[l, k], gla_a_bias[l, k], n_ctx, bool(k), not last) for k in range(2)]
        (og_f,), (og_b,) = _run_together("gla", (nb, GRID_W // GLA_COLS + 1), *gla, phases=GLA_PHASES)
        og_f, og_b = og_f.reshape(nb, t, GLA_V_WIDTH), og_b.reshape(nb, t, GLA_V_WIDTH)
        tile0 = n_ctx // TILE if last else 0
        xs, h2 = _post(xs, (y, hm, og_f, og_b), p3, (ssd_norm_w[l], ml_norm_w[l], gla_norm_w[l]),
                       w_bs, w_out_b, l, norm_ffn_w[l], mods[l], n_ctx, tile0)
        nt = xs.shape[1]
        a = _ffn_in(h2.reshape(nb * nt, d), w_gu, l, ffn_half)
        a = a.reshape(nb, nt, d_ff)
        if last:
            return _ffn_out(a, w_ffn_out_b, l, xs, mods[l], n_ctx, tile0, final_norm_w, None)
        xs, h = _ffn_out(a, w_ffn_out_b, l, xs, mods[l], n_ctx, tile0,
                         norm_mix_w[l + 1], mods[l + 1])
```

```python
import functools

import numpy as np
import jax
import jax.numpy as jnp
from jax import lax
from jax.experimental import pallas as pl
from jax.experimental.pallas import tpu as pltpu

F32 = jnp.float32
BF16 = jnp.bfloat16

EPS = 1e-6
LOG2E = 1.4426950408889634
GRID_W = 64
SSD_HEADS = 16
SSD_HEAD_DIM = 64
SSD_WIDTH = 1024
SSD_STATE = 64
SSD_BC = 128
ML_HEADS = 4
ML_QK_DIM = 128
ML_V_DIM = 256
ML_QK_WIDTH = 512
ML_V_WIDTH = 1024
GLA_HEADS = 4
GLA_K_DIM = 128
GLA_V_DIM = 256
GLA_K_WIDTH = 512
GLA_V_WIDTH = 1024
GLA_RANK = 16
GLA_TAU = 16.0
GLA_CHUNK = 64

LANES = 128
SUBLANES = 8
BF16_SUBLANES = 16
VMEM_LIMIT = 56 * 1024 * 1024

TILE = 256

_IN_NAMES = ("s_x", "s_z", "s_b", "s_c", "dt_f", "dt_b",
             "m_q", "m_k", "m_v", "m_o", "i_f", "i_b", "f_f", "f_b",
             "g_q", "g_k", "g_v", "g_g", "a_f", "a_b",
             "gate_ssd", "gate_ml", "gate_gla")
_IN_WIDTHS = (1024, 1024, 128, 128, 16, 16,
              512, 512, 1024, 1024, 4, 4, 4, 4,
              512, 512, 1024, 1024, 16, 16,
              1024, 1024, 1024)
_IN_OFF = dict(zip(_IN_NAMES, np.concatenate([[0], np.cumsum(_IN_WIDTHS)[:-1]]).tolist()))
_IN_W = dict(zip(_IN_NAMES, _IN_WIDTHS))

_P_ORDER = ("s_z", "m_o", "g_g", "gate_ssd", "gate_ml", "gate_gla", "m_v", "g_v",
            "g_q", "g_k", "s_x", "s_b", "s_c", "m_q", "m_k")
_P_OFF = {}
_o = 0
for _n in _P_ORDER:
    _P_OFF[_n] = _o
    _o += _IN_W[_n]
N_PROJ = _o
_SMALL = ("dt_f", "dt_b", "i_f", "i_b", "f_f", "f_b", "a_f", "a_b")
_SM_OFF = {}
_s = 0
for _n in _SMALL:
    _SM_OFF[_n] = _s
    _s += _IN_W[_n]
N_SMALL_USED = _s


def _cparams(sem):
    return pltpu.CompilerParams(dimension_semantics=sem, vmem_limit_bytes=VMEM_LIMIT)


def _sigmoid(x):
    return 0.5 * jnp.tanh(0.5 * x) + 0.5


def _silu(x):
    h = 0.5 * x
    return h + h * jnp.tanh(h)


def _softplus(x):
    return jnp.maximum(x, 0.0) + jnp.log1p(jnp.exp(-jnp.abs(x)))


def _log_sigmoid(x):
    return jnp.minimum(x, 0.0) - jnp.log(1.0 + jnp.exp(-jnp.abs(x)))


def _split(x, n):
    out = []
    r = x
    for _ in range(n):
        p = r.astype(BF16)
        out.append(p)
        r = r - p.astype(F32)
    return out


def _dot(a, b):
    return jnp.dot(a, b, preferred_element_type=F32)


def _dot_nt(a, b):
    return lax.dot_general(a, b, (((1,), (1,)), ((), ())), preferred_element_type=F32)


def _dot_tn(a, b):
    return lax.dot_general(a, b, (((0,), (0,)), ((), ())), preferred_element_type=F32)


def _dot_exact_lhs(t, x, pieces):
    return sum(_dot(t, p) for p in _split(x, pieces))


def _dot_hp(a, b):
    ah, am = _split(a, 2)
    bh, bm = _split(b, 2)
    return _dot(ah, bh) + _dot(ah, bm) + _dot(am, bh)


def _causal(n, rev):
    t = lax.broadcasted_iota(jnp.int32, (n, n), 0)
    s = lax.broadcasted_iota(jnp.int32, (n, n), 1)
    return (s >= t) if rev else (s <= t)


def _rms(x):
    return x * lax.rsqrt(jnp.mean(x * x, axis=-1, keepdims=True) + EPS)


def _mod_kernel(c_ref, w_ref, b_ref, o_ref):
    o_ref[...] = _dot_hp(_silu(c_ref[...]), w_ref[...]) + b_ref[...]


def _modulation(c16, w_mod, b_mod, layer):
    rows, d = c16.shape
    n = w_mod.shape[-1]
    tn = 1536
    return pl.pallas_call(
        _mod_kernel,
        out_shape=jax.ShapeDtypeStruct((rows, n), F32),
        grid=(n // tn,),
        in_specs=[pl.BlockSpec((rows, d), lambda j: (0, 0)),
                  pl.BlockSpec((None, d, tn), lambda j: (layer, 0, j)),
                  pl.BlockSpec((None, 1, tn), lambda j: (layer, 0, j))],
        out_specs=pl.BlockSpec((rows, tn), lambda j: (0, j)),
        compiler_params=_cparams(("arbitrary",)),
        name="modulation",
    )(c16, w_mod, b_mod.reshape(b_mod.shape[0], 1, n))


def _mod_row(nb, k, ctx_tiles, tile0=0):
    return lambda b, i: (jnp.where(i + tile0 < ctx_tiles, nb, b) * 6 + k, 0, 0)


def _norm_mod_kernel(ctx_ref, x_ref, w_ref, sc_ref, sh_ref, xo_ref, ho_ref):
    x = jnp.where(pl.program_id(1) == 0, ctx_ref[0], x_ref[0])
    xo_ref[0] = x
    ho_ref[0] = (_rms(x) * w_ref[...] * (1.0 + sc_ref[0]) + sh_ref[0]).astype(ho_ref.dtype)


def _norm_mod(ctx, x, w, mods, k_shift, k_scale):
    nb, n_lat, d = x.shape
    n_ctx = ctx.shape[1]
    assert n_ctx == TILE
    t = n_ctx + n_lat
    tok = lambda b, i: (b, i, 0)
    return pl.pallas_call(
        _norm_mod_kernel,
        out_shape=(jax.ShapeDtypeStruct((nb, t, d), F32), jax.ShapeDtypeStruct((nb, t, d), BF16)),
        grid=(nb, t // TILE),
        in_specs=[pl.BlockSpec((1, TILE, d), lambda b, i: (b, 0, 0)),
                  pl.BlockSpec((1, TILE, d), lambda b, i: (b, jnp.maximum(i - 1, 0), 0)),
                  pl.BlockSpec((1, d), lambda b, i: (0, 0)),
                  pl.BlockSpec((1, 1, d), _mod_row(nb, k_scale, n_ctx // TILE)),
                  pl.BlockSpec((1, 1, d), _mod_row(nb, k_shift, n_ctx // TILE))],
        out_specs=(pl.BlockSpec((1, TILE, d), tok), pl.BlockSpec((1, TILE, d), tok)),
        compiler_params=_cparams(("parallel", "arbitrary")),
        name="norm_mod",
    )(ctx, x, w.reshape(1, d), mods, mods)


MM_STAGE = 256


def _mm_body(a_ref, w_ref, o_ref):
    a = a_ref[...]
    width = o_ref.shape[1]
    stage = min(MM_STAGE, width)
    for c0 in range(0, width, stage):
        o_ref[:, c0:c0 + stage] = _dot(a, w_ref[:, c0:c0 + stage]).astype(o_ref.dtype)
        yield


def _row_tile(m):
    return 512 if m % 512 == 0 else TILE


def _mm_part(a, w, layer, tn, out_dtype, tile0, n_tiles):
    m, k = a.shape
    tm = _row_tile(m)
    return (_mm_body,
            [pl.BlockSpec((tm, k), lambda j, i: (i, 0)),
             pl.BlockSpec((None, k, tn), lambda j, i: (layer, 0, j + tile0))],
            [a, w], [jax.ShapeDtypeStruct((m, n_tiles * tn), out_dtype)],
            [pl.BlockSpec((tm, tn), lambda j, i: (i, j))], [])


CONV_K = 5
CONV_ROWS = 128
CONV_X = SSD_WIDTH // LANES
CONV_BC = 2 * SSD_BC // LANES
CONV_QK = 2 * ML_QK_WIDTH // LANES
CV_QK = SSD_WIDTH
CV_BC = SSD_WIDTH + 2 * ML_QK_WIDTH
CV_WIDTH = CV_BC + 2 * SSD_BC


def _conv_body(u_ref, w_ref, b_ref, s_ref, o_ref, pad_ref, *, n_ctx):
    t, c = u_ref.shape[1], u_ref.shape[2]
    half = CONV_K // 2
    zeros = jnp.zeros((SUBLANES, c), F32)
    w = w_ref[...]
    bias = b_ref[...]
    post = s_ref[...]
    for s0, n in ((0, n_ctx), (n_ctx, t - n_ctx)):
        pad_ref[0:SUBLANES, :] = zeros
        pad_ref[SUBLANES:SUBLANES + n, :] = u_ref[0, s0:s0 + n, :].astype(F32)
        pad_ref[SUBLANES + n:2 * SUBLANES + n, :] = zeros
        for r0 in range(0, n, CONV_ROWS):
            acc = bias
            for j in range(CONV_K):
                lo = SUBLANES - half + j + r0
                acc = acc + w[j:j + 1, :] * pad_ref[lo:lo + CONV_ROWS, :]
            o_ref[0, s0 + r0:s0 + r0 + CONV_ROWS, :] = (_silu(acc) * post).astype(o_ref.dtype)
            yield


def _conv_part(u3, w, b, post_scale, n_ctx, grid):
    nb, t, width = u3.shape
    n_cb = width // LANES
    steps = grid[0] * grid[1]
    rep = steps // (nb * n_cb)
    assert steps == rep * nb * n_cb and n_cb == CONV_X + CONV_BC + CONV_QK
    blk = lambda j, i: (j * grid[1] + i) // rep
    chan = lambda j, i: blk(j, i) % n_cb
    out_chan = lambda c: jnp.where(c < CONV_X, c, jnp.where(c < CONV_X + CONV_BC, c + CONV_QK, c - CONV_BC))
    vec = lambda rows: pl.BlockSpec((rows, LANES), lambda j, i: (0, chan(j, i)))
    return (functools.partial(_conv_body, n_ctx=n_ctx),
            [pl.BlockSpec((1, t, LANES), lambda j, i: (blk(j, i) // n_cb, 0, chan(j, i))),
             vec(CONV_K), vec(1), vec(1)],
            [u3, w, b.reshape(1, width), post_scale.reshape(1, width)],
            [jax.ShapeDtypeStruct((nb, t, width), BF16)],
            [pl.BlockSpec((1, t, LANES), lambda j, i: (blk(j, i) // n_cb, 0, out_chan(chan(j, i))))],
            [pltpu.VMEM((t + 2 * SUBLANES, LANES), F32)])


def _tile_order(n_tiles, rev, ctx_tiles=1):
    if rev:
        return lambda i: jnp.where(i < ctx_tiles, ctx_tiles - 1 - i, n_tiles - 1 + ctx_tiles - i)
    return lambda i: i


SSD_PAIRS = SSD_HEADS // 2
SSD_B_OFF = SSD_HEADS
SSD_TILE = 256


def _ssd_gates(sm_ref, dtb_ref, alog_ref, n, both):
    lane = lax.broadcasted_iota(jnp.int32, (1, LANES), 1)
    dt = _softplus(sm_ref[0] + dtb_ref[...])
    la = dt * jnp.where(lane < 2 * SSD_HEADS, -jnp.exp(alog_ref[...]), 0.0)
    parts = _split(la, 2)
    upp = jnp.where(_causal(n, True), 1.0, 0.0).astype(BF16)
    cum = sum(_dot(upp, p) for p in parts)
    if both:
        low = jnp.where(_causal(n, False), 1.0, 0.0).astype(BF16)
        cum = jnp.where(lane < SSD_HEADS, sum(_dot(low, p) for p in parts), cum)
    return lane, dt, cum


def _expand(a, e):
    return _dot(a.astype(BF16), e)


def _group_dup(v, g, lo):
    other = pltpu.roll(v, SSD_STATE, axis=1)
    return jnp.where(lo, v, other) if g == 0 else jnp.where(lo, other, v)


def _ssd_state_step(st_ref, j, bw, xp, elast, off):
    r = lax.broadcasted_iota(jnp.int32, (LANES, LANES), 0) < SSD_STATE
    c = lax.broadcasted_iota(jnp.int32, (LANES, LANES), 1) < SSD_HEAD_DIM
    dec = jnp.where(r, elast[:, off + 2 * j:off + 2 * j + 1], elast[:, off + 2 * j + 1:off + 2 * j + 2])
    st_ref[j] = jnp.where(r == c, dec * st_ref[j] + _dot_tn(bw.astype(BF16), xp), 0.0)


def _interleave(*bodies):
    live = list(bodies)
    while live:
        for body in list(live):
            if next(body, StopIteration) is StopIteration:
                live.remove(body)


def _ssd_states_body(x_ref, bc_ref, sm_ref, dtb_ref, alog_ref, eb_ref, o_ref, st_ref):
    n = x_ref.shape[1]

    @pl.when(pl.program_id(1) == 0)
    def _init():
        st_ref[...] = jnp.zeros_like(st_ref)

    o_ref[0, 0] = st_ref[...].astype(o_ref.dtype)
    yield
    lane, dt, cum = _ssd_gates(sm_ref, dtb_ref, alog_ref, n, False)
    lo = lane < SSD_HEAD_DIM
    last = cum[0:1, :]
    yield
    wst = _expand(jnp.exp(last - cum) * dt, eb_ref[...])
    elast = jnp.exp(last)
    b128 = bc_ref[0, :, :SSD_BC].astype(F32)
    for j in range(SSD_PAIRS):
        sl = slice(LANES * j, LANES * (j + 1))
        bw = _group_dup(b128, j // (SSD_PAIRS // 2), lo) * wst[:, sl]
        _ssd_state_step(st_ref, j, bw, x_ref[0, :, sl], elast, SSD_B_OFF)
        if j % 2:
            yield


def _ssd_out_body(x_ref, bc_ref, sm_ref, dtb_ref, alog_ref, ef_ref, eb_ref, d_ref, sb_ref, o_ref, st_ref):
    n = x_ref.shape[1]

    @pl.when(pl.program_id(1) == 0)
    def _init():
        st_ref[...] = jnp.zeros_like(st_ref)

    zero_b = jnp.zeros((), BF16)
    lo = lax.broadcasted_iota(jnp.int32, (1, LANES), 1) < SSD_HEAD_DIM
    b128_b = bc_ref[0, :, :SSD_BC]
    c128_b = bc_ref[0, :, SSD_BC:]
    cb_all = [_dot_nt(jnp.where(lo if g == 0 else jnp.logical_not(lo), c128_b, zero_b), b128_b)
              for g in range(2)]
    pair_x = lambda j: x_ref[0, :, LANES * j:LANES * (j + 1)]
    rhs_all = [jnp.concatenate([jnp.where(lo, pair_x(j), zero_b), jnp.where(lo, zero_b, pair_x(j)),
                                st_ref[j].astype(BF16), sb_ref[0, 0, j]], axis=0) for j in range(SSD_PAIRS)]
    yield
    lane, dt, cum = _ssd_gates(sm_ref, dtb_ref, alog_ref, n, True)
    is_f = lane < SSD_HEADS
    ldt = jnp.log(dt)
    dsum = jnp.log(dt + pltpu.roll(dt, LANES - SSD_B_OFF, axis=1))
    rt = (jnp.where(lane < 2 * SSD_HEADS, cum - ldt, pltpu.roll(dsum, 2 * SSD_HEADS, axis=1)) * LOG2E).T
    cum2 = cum * LOG2E
    yield
    last = jnp.where(is_f, cum[n - 1:n, :], cum[0:1, :])
    elast = jnp.exp(last)
    ecum = jnp.exp(cum)
    ecum_f = _expand(ecum, ef_ref[...])
    ecum_b = _expand(ecum, eb_ref[...])
    wst = _expand(jnp.exp(last - cum) * dt, ef_ref[...])
    bc = bc_ref[0].astype(F32)
    b128, c128 = bc[:, :SSD_BC], bc[:, SSD_BC:]
    ti = lax.broadcasted_iota(jnp.int32, (n, n), 0)
    si = lax.broadcasted_iota(jnp.int32, (n, n), 1)
    below, above = si < ti, si > ti
    half = SSD_PAIRS // 2
    yield
    for g in range(2):
        cb = cb_all[g]
        cdup = _group_dup(c128, g, lo)
        for j in range(g * half, (g + 1) * half):
            ms = []
            for h in (2 * j, 2 * j + 1):
                e_f = cum2[:, h:h + 1] - rt[h:h + 1, :]
                e_b = cum2[:, SSD_B_OFF + h:SSD_B_OFF + h + 1] - rt[SSD_B_OFF + h:SSD_B_OFF + h + 1, :]
                e = jnp.where(below, e_f, jnp.where(above, e_b, rt[2 * SSD_HEADS + h:2 * SSD_HEADS + h + 1, :]))
                ms.append((cb * jnp.exp2(e)).astype(BF16))
            sl = slice(LANES * j, LANES * (j + 1))
            cs_f = (cdup * ecum_f[:, sl]).astype(BF16)
            cs_b = (cdup * ecum_b[:, sl]).astype(BF16)
            lhs = jnp.concatenate(ms + [cs_f, cs_b], axis=1)
            o_ref[0, :, sl] = _dot(lhs, rhs_all[j]) + d_ref[:, sl] * pair_x(j).astype(F32)
            yield
    for j in range(SSD_PAIRS):
        sl = slice(LANES * j, LANES * (j + 1))
        _ssd_state_step(st_ref, j, _group_dup(b128, j // half, lo) * wst[:, sl], x_ref[0, :, sl], elast, 0)
        if j % 2:
            yield


def _run_together(name, grid, *parts, phases=None):
    n_in = [len(p[1]) for p in parts]
    n_out = [len(p[3]) for p in parts]
    n_scr = [len(p[5]) for p in parts]

    def kern(*refs):
        ins, outs, scr = refs[:sum(n_in)], refs[sum(n_in):sum(n_in) + sum(n_out)], refs[sum(n_in) + sum(n_out):]

        def run(key):
            bodies = []
            for k, p in enumerate(parts):
                take = lambda seq, counts: seq[sum(counts[:k]):sum(counts[:k + 1])]
                body = p[0][key] if isinstance(p[0], dict) else p[0]
                bodies.append(body(*take(ins, n_in), *take(outs, n_out), *take(scr, n_scr)))
            _interleave(*bodies)

        if phases is None:
            run(None)
        else:
            for pred, key in phases:
                pl.when(pred(pl.program_id(1)))(functools.partial(run, key))

    res = pl.pallas_call(
        kern,
        out_shape=tuple(s for p in parts for s in p[3]),
        grid=grid,
        in_specs=[s for p in parts for s in p[1]],
        out_specs=tuple(s for p in parts for s in p[4]),
        scratch_shapes=[s for p in parts for s in p[5]],
        compiler_params=_cparams(("arbitrary", "arbitrary")),
        name=name,
    )(*[a for p in parts for a in p[2]])
    return [list(res[sum(n_out[:k]):sum(n_out[:k + 1])]) for k in range(len(parts))]


def _ssd_parts(cvs, sm3, dt_bias, a_log, d_e, n_ctx):
    nb, t, _ = cvs.shape
    tile = SSD_TILE
    nt = t // tile
    row = lambda v: jnp.pad(v.reshape(1, -1), ((0, 0), (0, LANES - 2 * SSD_HEADS)))
    const = lambda b, i: (0, 0)
    specs = lambda order: [
        pl.BlockSpec((1, tile, SSD_WIDTH), lambda b, i: (b, order(i), 0)),
        pl.BlockSpec((1, tile, 2 * SSD_BC), lambda b, i: (b, order(i), CV_BC // (2 * SSD_BC))),
        pl.BlockSpec((1, tile, LANES), lambda b, i: (b, order(i), 0)),
        pl.BlockSpec((1, LANES), const),
        pl.BlockSpec((1, LANES), const)]
    st_block = (1, 1, SSD_PAIRS, LANES, LANES)
    args = (cvs, cvs, sm3, row(dt_bias), row(a_log))
    sel = np.zeros((2, LANES, SSD_WIDTH), np.float32)
    for h in range(SSD_HEADS):
        sel[0, h, h * SSD_HEAD_DIM:(h + 1) * SSD_HEAD_DIM] = 1.0
        sel[1, SSD_B_OFF + h, h * SSD_HEAD_DIM:(h + 1) * SSD_HEAD_DIM] = 1.0
    e_f, e_b = jnp.asarray(sel[0], BF16), jnp.asarray(sel[1], BF16)
    e_spec = pl.BlockSpec((LANES, SSD_WIDTH), const)
    bwd = _tile_order(nt, True, n_ctx // tile)
    fwd = _tile_order(nt, False)
    scratch = [pltpu.VMEM(st_block[2:], F32)]
    states = (_ssd_states_body, specs(bwd) + [e_spec], list(args) + [e_b],
              [jax.ShapeDtypeStruct((nb, nt) + st_block[2:], BF16)],
              [pl.BlockSpec(st_block, lambda b, i: (b, bwd(i), 0, 0, 0))], scratch)
    out = lambda states_b: (
        _ssd_out_body,
        specs(fwd) + [e_spec, e_spec, pl.BlockSpec((1, SSD_WIDTH), const),
                      pl.BlockSpec(st_block, lambda b, i: (b, i, 0, 0, 0))],
        list(args) + [e_f, e_b, d_e, states_b],
        [jax.ShapeDtypeStruct((nb, t, SSD_WIDTH), F32)],
        [pl.BlockSpec((1, tile, SSD_WIDTH), lambda b, i: (b, i, 0))], scratch)
    return states, out


ML_GATE = 32
ML_ND = 2 * ML_HEADS
ML_AUG = ML_V_DIM + LANES


def _ml_gates(sm_ref, ib_ref, fb_ref, n, both):
    lane = lax.broadcasted_iota(jnp.int32, (1, LANES), 1)
    valid = (lane >= ML_GATE) & (lane < ML_GATE + ML_ND)
    is_f = lane < ML_GATE + ML_HEADS
    sm = sm_ref[0]
    li = sm + ib_ref[...]
    lf = pltpu.roll(_log_sigmoid(sm + fb_ref[...]), LANES - ML_ND, axis=1)
    parts = _split(jnp.where(valid, lf, 0.0), 2)
    upp = jnp.where(_causal(n, True), 1.0, 0.0).astype(BF16)
    bcum = sum(_dot(upp, p) for p in parts)
    if both:
        low = jnp.where(_causal(n, False), 1.0, 0.0).astype(BF16)
        bcum = jnp.where(is_f, sum(_dot(low, p) for p in parts), bcum)
    return valid, is_f, bcum, jnp.where(valid, li - bcum, 0.0)


def _ml_state_step(cn_ref, h, cn, keep, k, ws_dense, v_aug):
    w3 = jnp.concatenate([ws_dense.astype(BF16)] * (ML_AUG // LANES), axis=1)
    cn_ref[h] = keep * cn + _dot_tn(k, w3 * v_aug)


def _ml_v_aug(v_ref, h, n):
    return jnp.concatenate([v_ref[0, :, ML_V_DIM * h:ML_V_DIM * (h + 1)], jnp.ones((n, LANES), BF16)], axis=1)


def _ml_states_body(qk_ref, v_ref, sm_ref, ib_ref, fb_ref, sel_ref, cn_out, m_out, cn_ref, m_ref):
    n = qk_ref.shape[1]

    @pl.when(pl.program_id(1) == 0)
    def _init():
        cn_ref[...] = jnp.zeros_like(cn_ref)
        m_ref[...] = jnp.zeros_like(m_ref)

    cn_out[0, 0] = cn_ref[...].astype(cn_out.dtype)
    m_out[0, 0] = m_ref[...]
    yield
    valid, is_f, bcum, a = _ml_gates(sm_ref, ib_ref, fb_ref, n, False)
    m_prev = m_ref[0:1, :]
    g_last = jnp.maximum(m_prev, jnp.max(a, axis=0, keepdims=True))
    yield
    ws = _expand(jnp.exp(a - g_last), sel_ref[:, ML_HEADS * LANES:])
    keep = jnp.exp(m_prev - g_last)
    for h in range(ML_HEADS):
        lane_b = ML_GATE + ML_HEADS + h
        k = qk_ref[0, :, ML_QK_WIDTH + ML_QK_DIM * h:ML_QK_WIDTH + ML_QK_DIM * (h + 1)]
        _ml_state_step(cn_ref, h, cn_ref[h], keep[:, lane_b:lane_b + 1], k,
                       ws[:, LANES * h:LANES * (h + 1)], _ml_v_aug(v_ref, h, n))
        yield
    m_ref[...] = jnp.broadcast_to(bcum[0:1, :] + g_last, m_ref.shape)


def _ml_out_body(qk_ref, v_ref, sm_ref, ib_ref, fb_ref, sel_ref, cnb_ref, mb_ref, o_ref, cn_ref, m_ref):
    n = qk_ref.shape[1]

    @pl.when(pl.program_id(1) == 0)
    def _init():
        cn_ref[...] = jnp.zeros_like(cn_ref)
        m_ref[...] = jnp.zeros_like(m_ref)

    head_q = lambda h: qk_ref[0, :, ML_QK_DIM * h:ML_QK_DIM * (h + 1)]
    head_k = lambda h: qk_ref[0, :, ML_QK_WIDTH + ML_QK_DIM * h:ML_QK_WIDTH + ML_QK_DIM * (h + 1)]
    qk_all = [_dot_nt(head_q(h), head_k(h)) for h in range(ML_HEADS)]
    inter_all = [_dot(head_q(h), jnp.concatenate([cn_ref[h].astype(BF16), cnb_ref[0, 0, h]], axis=1))
                 for h in range(ML_HEADS)]
    yield
    valid, is_f, bcum, a = _ml_gates(sm_ref, ib_ref, fb_ref, n, True)
    m_prev = jnp.where(is_f, m_ref[0:1, :], mb_ref[0, 0, 0:1, :])
    a_t = a.T
    pre = suf = a_t[ML_GATE:ML_GATE + ML_ND, :]
    pos = lax.broadcasted_iota(jnp.int32, (ML_ND, n), 1)
    k = 1
    while k < n:
        pre = jnp.maximum(pre, jnp.where(pos >= k, pltpu.roll(pre, k, axis=1), -jnp.inf))
        suf = jnp.maximum(suf, jnp.where(pos < n - k, pltpu.roll(suf, n - k, axis=1), -jnp.inf))
        k *= 2
    run = jnp.where(lax.broadcasted_iota(jnp.int32, (ML_ND, n), 0) < ML_HEADS, pre, suf)
    run = jnp.concatenate([jnp.zeros((ML_GATE, n), F32), run,
                           jnp.zeros((LANES - ML_GATE - ML_ND, n), F32)], axis=0).T
    g = jnp.maximum(m_prev, run)
    m_t = bcum + g
    yield
    floor = jnp.exp(-m_t)
    dense = lambda x, lane_: jnp.broadcast_to(x[:, lane_:lane_ + 1], (n, LANES))
    diag_t = jnp.exp(a - g).T
    g_last = g[n - 1:n, :]
    ws = _expand(jnp.exp(a - g_last), sel_ref[:, :ML_HEADS * LANES])
    keep = jnp.exp(m_prev - g_last)
    ti = lax.broadcasted_iota(jnp.int32, (n, n), 0)
    si = lax.broadcasted_iota(jnp.int32, (n, n), 1)
    not_above, above, on_diag = si <= ti, si > ti, si == ti
    wide = lambda x: jnp.concatenate([x] * (n // LANES), axis=1)
    for h in range(ML_HEADS):
        lanes = (ML_GATE + h, ML_GATE + ML_HEADS + h)
        g_d = [dense(g, ln) for ln in lanes]
        qk, inter = qk_all[h], inter_all[h]
        e = jnp.where(not_above, a_t[lanes[0]:lanes[0] + 1, :] - wide(g_d[0]),
                      a_t[lanes[1]:lanes[1] + 1, :] - wide(g_d[1]))
        p = qk * jnp.exp(e)
        p_f = jnp.where(not_above, p, 0.0).astype(BF16)
        p_b = jnp.where(above, p, jnp.where(on_diag, qk * diag_t[lanes[1]:lanes[1] + 1, :], 0.0)).astype(BF16)
        intra = _dot(jnp.concatenate([p_f, p_b], axis=0), _ml_v_aug(v_ref, h, n))
        out = None
        for d in range(2):
            w_inter = jnp.exp(m_prev[:, lanes[d]:lanes[d] + 1] - g_d[d])
            s = (intra[n * d:n * (d + 1)] + jnp.concatenate([w_inter] * (ML_AUG // LANES), axis=1)
                 * inter[:, ML_AUG * d:ML_AUG * (d + 1)])
            rn = 1.0 / jnp.maximum(jnp.abs(s[:, ML_V_DIM:]), dense(floor, lanes[d]))
            hid = s[:, :ML_V_DIM] * jnp.concatenate([rn] * (ML_V_DIM // LANES), axis=1)
            out = hid if out is None else out + hid
        o_ref[0, :, ML_V_DIM * h:ML_V_DIM * (h + 1)] = out
        yield
    for h in range(ML_HEADS):
        lane_f = ML_GATE + h
        _ml_state_step(cn_ref, h, cn_ref[h], keep[:, lane_f:lane_f + 1], head_k(h),
                       ws[:, LANES * h:LANES * (h + 1)], _ml_v_aug(v_ref, h, n))
        yield
    m_ref[...] = jnp.broadcast_to(m_t[n - 1:n, :], m_ref.shape)


def _mlstm_parts(cvm, p3, sm3, i_bias, f_bias, n_ctx):
    nb, t, _ = cvm.shape
    nt = t // TILE
    assert _SM_OFF["i_f"] == ML_GATE and _SM_OFF["f_f"] == ML_GATE + ML_ND
    row = lambda v, off: jnp.pad(v.reshape(1, -1), ((0, 0), (off, LANES - off - ML_ND)))
    sel = np.zeros((LANES, ML_ND * LANES), np.float32)
    for r in range(ML_ND):
        sel[ML_GATE + r, r * LANES:(r + 1) * LANES] = 1.0
    const = lambda b, i: (0, 0)
    specs = lambda order: [
        pl.BlockSpec((1, TILE, 2 * ML_QK_WIDTH), lambda b, i: (b, order(i), CV_QK // (2 * ML_QK_WIDTH))),
        pl.BlockSpec((1, TILE, ML_V_WIDTH), lambda b, i: (b, order(i), _P_OFF["m_v"] // ML_V_WIDTH)),
        pl.BlockSpec((1, TILE, LANES), lambda b, i: (b, order(i), 0)),
        pl.BlockSpec((1, LANES), const),
        pl.BlockSpec((1, LANES), const),
        pl.BlockSpec((LANES, ML_ND * LANES), const)]
    args = (cvm, p3, sm3, row(i_bias, ML_GATE), row(f_bias, ML_GATE + ML_ND), jnp.asarray(sel, BF16))
    cn_block = (1, 1, ML_HEADS, ML_QK_DIM, ML_AUG)
    m_block = (1, 1, SUBLANES, LANES)
    scratch = [pltpu.VMEM(cn_block[2:], F32), pltpu.VMEM(m_block[2:], F32)]
    bwd = _tile_order(nt, True, n_ctx // TILE)
    fwd = _tile_order(nt, False)
    st_idx = lambda b, i: (b, bwd(i)) + (0,) * 3
    states = (_ml_states_body, specs(bwd), list(args),
              [jax.ShapeDtypeStruct((nb, nt) + cn_block[2:], BF16),
               jax.ShapeDtypeStruct((nb, nt) + m_block[2:], F32)],
              [pl.BlockSpec(cn_block, st_idx), pl.BlockSpec(m_block, lambda b, i: (b, bwd(i), 0, 0))],
              scratch)
    out = lambda cn_b, m_b: (
        _ml_out_body,
        specs(fwd) + [pl.BlockSpec(cn_block, lambda b, i: (b, i, 0, 0, 0)),
                      pl.BlockSpec(m_block, lambda b, i: (b, i, 0, 0))],
        list(args) + [cn_b, m_b],
        [jax.ShapeDtypeStruct((nb, t, ML_V_WIDTH), F32)],
        [pl.BlockSpec((1, TILE, ML_V_WIDTH), lambda b, i: (b, i, 0))], scratch)
    return states, out


GLA_SUB = 256
GLA_NCH = GLA_SUB // GLA_CHUNK
GLA_COLS = BF16_SUBLANES


def _gla_layout(is_ctx, lat_rows):
    r = np.arange(GLA_SUB)
    if is_ctx:
        return r // GLA_CHUNK, r % GLA_CHUNK
    col = r % SUBLANES
    cpc = GLA_CHUNK // lat_rows
    return col // cpc, (col % cpc) * lat_rows + r // SUBLANES


def _gla_consts(is_ctx, lat_rows, rev):
    ch, pos = _gla_layout(is_ctx, lat_rows)
    same = ch[:, None] == ch[None, :]
    before = (pos[None, :] >= pos[:, None]) if rev else (pos[None, :] <= pos[:, None])
    tri = (same & before).astype(np.float32)
    cmask = np.stack([np.repeat((ch == j)[:, None], LANES, axis=1) for j in range(GLA_NCH)])
    return tri, cmask.astype(np.float32)


def _gla_row(is_ctx, lat_rows, j, p):
    ch, pos = _gla_layout(is_ctx, lat_rows)
    return int(np.nonzero((ch == j) & (pos == p))[0][0])


def _per_chunk_rows(b, is_ctx, lat_rows, p):
    rows = [b[_gla_row(is_ctx, lat_rows, j, p):_gla_row(is_ctx, lat_rows, j, p) + 1, :]
            for j in range(GLA_NCH)]
    w = b.shape[1]
    if is_ctx:
        full = jnp.concatenate([jnp.broadcast_to(r, (GLA_CHUNK, w)) for r in rows], axis=0)
    else:
        rep = SUBLANES // GLA_NCH
        pat = jnp.concatenate([jnp.broadcast_to(r, (rep, w)) for r in rows], axis=0)
        full = jnp.broadcast_to(pat[None], (GLA_SUB // SUBLANES, SUBLANES, w)).reshape(GLA_SUB, w)
    return rows, full


def _gla_sub(q, k, v, araw, aup, abias, tri_b, tri_f, cmask_ref, st_ref, store, *, rev, is_ctx, lat_rows):
    want_out = store is not None
    g = _log_sigmoid(_dot_hp(araw, aup) + abias) * (1.0 / GLA_TAU)
    b = _dot_exact_lhs(tri_b, g, 2)
    yield
    lasts, last = _per_chunk_rows(b, is_ctx, lat_rows, 0 if rev else GLA_CHUNK - 1)
    kl = (k * jnp.exp(last - b)).astype(BF16)
    if want_out:
        _, ref = _per_chunk_rows(b, is_ctx, lat_rows, GLA_CHUNK // 2)
        qs = q * (GLA_K_DIM ** -0.5)
        qe = (qs * jnp.exp(b - ref)).astype(BF16)
        ke = (k * jnp.exp(ref - b)).astype(BF16)
        qb = (qs * jnp.exp(b)).astype(BF16)
        visible = tri_f > 0.0
    yield
    order = range(GLA_NCH - 1, -1, -1) if rev else range(GLA_NCH)
    outs = []
    for h in range(GLA_HEADS):
        ks = slice(GLA_K_DIM * h, GLA_K_DIM * (h + 1))
        vh = v[:, GLA_V_DIM * h:GLA_V_DIM * (h + 1)]
        klm = jnp.concatenate([kl[:, ks] * cmask_ref[j] for j in range(GLA_NCH)], axis=1)
        upd = _dot_tn(vh, klm)
        s = st_ref[h]
        s_in = [None] * GLA_NCH
        for j in order:
            s_in[j] = s.astype(BF16)
            s = s * jnp.exp(lasts[j][:, ks]) + upd[:, GLA_K_DIM * j:GLA_K_DIM * (j + 1)]
        st_ref[h] = s
        if want_out:
            att = jnp.where(visible, _dot_nt(qe[:, ks], ke[:, ks]), 0.0).astype(BF16)
            qbm = jnp.concatenate([qb[:, ks] * cmask_ref[j] for j in range(GLA_NCH)], axis=1)
            outs.append(_dot(att, vh) + _dot_nt(qbm, jnp.concatenate(s_in, axis=1)))
        yield
    if want_out:
        store(jnp.concatenate(outs, axis=1))


def _gla_ctx_body(qc_ref, kc_ref, vc_ref, sc_ref, ql_ref, kl_ref, vl_ref, sl_ref,
                  aup_ref, ab_ref, tcb_ref, tcf_ref, cmc_ref, tlb_ref, tlf_ref, cml_ref,
                  o_ref, st_ref, ctxo_ref, *, rev, ctx_rows, lat_rows, n_cblk, ctx_out):
    st_ref[...] = jnp.zeros_like(st_ref)

    def store(o):
        ctxo_ref[...] = o

    yield from _gla_sub(qc_ref[0].astype(F32), kc_ref[0].astype(F32), vc_ref[0], sc_ref[0],
                        aup_ref[...], ab_ref[...], tcb_ref[...], tcf_ref[...], cmc_ref, st_ref,
                        store if ctx_out else None, rev=rev, is_ctx=True, lat_rows=lat_rows)


def _gla_lat_body(qc_ref, kc_ref, vc_ref, sc_ref, ql_ref, kl_ref, vl_ref, sl_ref,
                  aup_ref, ab_ref, tcb_ref, tcf_ref, cmc_ref, tlb_ref, tlf_ref, cml_ref,
                  o_ref, st_ref, ctxo_ref, *, rev, ctx_rows, lat_rows, n_cblk, ctx_out):
    i = pl.program_id(1)
    r0, r1 = ctx_rows, ctx_rows + lat_rows
    cblk = (n_cblk - i) if rev else (i - 1)
    halves = range(GLA_COLS // SUBLANES)
    for half in (reversed(halves) if rev else halves):
        cs = slice(SUBLANES * half, SUBLANES * (half + 1))
        take = lambda r: r[0, r0:r1].astype(F32)[:, cs, :].reshape(GLA_SUB, r.shape[-1])

        def store(o, cs=cs):
            o_ref[0, r0:r1, cs, :] = o.reshape(lat_rows, SUBLANES, GLA_V_WIDTH)

        yield from _gla_sub(take(ql_ref), take(kl_ref), take(vl_ref).astype(BF16), take(sl_ref),
                            aup_ref[...], ab_ref[...], tlb_ref[...], tlf_ref[...], cml_ref, st_ref,
                            store, rev=rev, is_ctx=False, lat_rows=lat_rows)
    for r in range(ctx_rows):
        if ctx_out:
            start = pl.multiple_of(r * GRID_W + cblk * GLA_COLS, GLA_COLS)
            o_ref[0, r, :, :] = ctxo_ref[pl.ds(start, GLA_COLS), :]
        else:
            o_ref[0, r, :, :] = jnp.zeros((GLA_COLS, GLA_V_WIDTH), F32)


GLA_PHASES = ((lambda i: i == 0, "ctx"), (lambda i: i > 0, "lat"))


def _gla_part(p3, sm3, a_up, a_bias, n_ctx, rev, ctx_out):
    nb, t, ncol = p3.shape
    rows = t // GRID_W
    ctx_rows = n_ctx // GRID_W
    lat_rows = rows - ctx_rows
    n_cblk = GRID_W // GLA_COLS
    p4 = p3.reshape(nb, rows, GRID_W, ncol)
    sm4 = sm3.reshape(nb, rows, GRID_W, LANES)
    a_off = _SM_OFF["a_b"] if rev else _SM_OFF["a_f"]
    aup = jnp.pad(a_up, ((a_off, LANES - a_off - GLA_RANK), (0, 0)))
    cblk = lambda i: jnp.where(i == 0, n_cblk - 1 if rev else 0, (n_cblk - i) if rev else (i - 1))
    ctx = lambda blk: (lambda b, i: (b, 0, blk))
    lat = lambda blk: (lambda b, i: (b, 0, cblk(i), blk))
    const2 = lambda b, i: (0, 0)
    const3 = lambda b, i: (0, 0, 0)
    widths = (GLA_K_WIDTH, GLA_K_WIDTH, GLA_V_WIDTH)
    offs = (_P_OFF["g_q"], _P_OFF["g_k"], _P_OFF["g_v"])
    consts = []
    const_specs = []
    for is_ctx in (True, False):
        tri, cmask = _gla_consts(is_ctx, lat_rows, rev)
        consts += [jnp.asarray(tri, BF16), jnp.asarray(tri, F32), jnp.asarray(cmask, BF16)]
        const_specs += [pl.BlockSpec((GLA_SUB, GLA_SUB), const2), pl.BlockSpec((GLA_SUB, GLA_SUB), const2),
                        pl.BlockSpec((GLA_NCH, GLA_SUB, LANES), const3)]
    out_spec = pl.BlockSpec((1, rows, GLA_COLS, GLA_V_WIDTH), lat(0))
    in_specs = ([pl.BlockSpec((1, n_ctx, w), ctx(o // w)) for w, o in zip(widths, offs)]
                + [pl.BlockSpec((1, n_ctx, LANES), ctx(0))]
                + [pl.BlockSpec((1, rows, GLA_COLS, w), lat(o // w)) for w, o in zip(widths, offs)]
                + [pl.BlockSpec((1, rows, GLA_COLS, LANES), lat(0))]
                + [pl.BlockSpec((LANES, GLA_K_WIDTH), const2), pl.BlockSpec((1, GLA_K_WIDTH), const2)]
                + const_specs)
    args = [p3, p3, p3, sm3, p4, p4, p4, sm4, aup, a_bias.reshape(1, GLA_K_WIDTH)] + consts
    static = dict(rev=rev, ctx_rows=ctx_rows, lat_rows=lat_rows, n_cblk=n_cblk, ctx_out=ctx_out)
    bodies = {"ctx": functools.partial(_gla_ctx_body, **static),
              "lat": functools.partial(_gla_lat_body, **static)}
    return (bodies, in_specs, args,
            [jax.ShapeDtypeStruct((nb, rows, GRID_W, GLA_V_WIDTH), F32)], [out_spec],
            [pltpu.VMEM((GLA_HEADS, GLA_V_DIM, GLA_K_DIM), F32), pltpu.VMEM((n_ctx, GLA_V_WIDTH), F32)])


def _group_rmsnorm(y, groups):
    width = y.shape[-1] // groups
    ones = jnp.ones((width, LANES), BF16)
    out = []
    for g in range(groups):
        yg = y[:, width * g:width * (g + 1)]
        ms = _dot((yg * yg).astype(BF16), ones) * (1.0 / width)
        out.append(yg * jnp.concatenate([lax.rsqrt(ms + EPS)] * (width // LANES), axis=1))
    return jnp.concatenate(out, axis=1)


def _post_kernel(x_ref, y_ref, h_ref, of_ref, ob_ref, z_ref, mo_ref, gg_ref, gs_ref, gm_ref, gl_ref,
                 nws_ref, nwm_ref, nwg_ref, wbs_ref, wbm_ref, wbg_ref, wout_ref, g1_ref,
                 nwf_ref, sc2_ref, sh2_ref, xo_ref, ho_ref):
    y_ssd = (_group_rmsnorm(y_ref[0] * _silu(z_ref[0]), 2) * nws_ref[...]).astype(BF16)
    y_ml = (_group_rmsnorm(h_ref[0], ML_HEADS) * nwm_ref[...] * _sigmoid(mo_ref[0])).astype(BF16)
    y_gla = (_group_rmsnorm(of_ref[0] + ob_ref[0], GLA_HEADS) * nwg_ref[...] * _silu(gg_ref[0])).astype(BF16)
    merged = (_sigmoid(gs_ref[0]) * _dot(y_ssd, wbs_ref[...])
              + _sigmoid(gm_ref[0]) * _dot(y_ml, wbm_ref[...])
              + _sigmoid(gl_ref[0]) * _dot(y_gla, wbg_ref[...]))
    x_new = x_ref[0] + g1_ref[0] * _dot(merged.astype(BF16), wout_ref[...])
    xo_ref[0] = x_new
    ho_ref[0] = (_rms(x_new) * nwf_ref[...] * (1.0 + sc2_ref[0]) + sh2_ref[0]).astype(ho_ref.dtype)


def _post(x, scans, p3, norm_ws, w_bs, w_out, layer, norm_ffn_w, mods, n_ctx, tile0):
    nb, t, d = x.shape
    nt = t // TILE - tile0
    ctx_tiles = n_ctx // TILE
    tok = lambda blk: (lambda b, i: (b, i + tile0, blk))
    out = lambda b, i: (b, i, 0)
    const = lambda b, i: (0, 0)
    tok_spec = lambda blk: pl.BlockSpec((1, TILE, d), tok(blk))
    w_spec = pl.BlockSpec((None, d, d), lambda b, i: (layer, 0, 0), pipeline_mode=pl.Buffered(1))
    vec = pl.BlockSpec((1, d), const)
    mod = lambda k: pl.BlockSpec((1, 1, d), _mod_row(nb, k, ctx_tiles, tile0))
    names = ("s_z", "m_o", "g_g", "gate_ssd", "gate_ml", "gate_gla")
    in_specs = ([tok_spec(0)] * (1 + len(scans)) + [tok_spec(_P_OFF[nm] // d) for nm in names]
                + [vec] * 3 + [w_spec] * 4 + [mod(2), vec, mod(4), mod(3)])
    return pl.pallas_call(
        _post_kernel,
        out_shape=(jax.ShapeDtypeStruct((nb, nt * TILE, d), F32),
                   jax.ShapeDtypeStruct((nb, nt * TILE, d), BF16)),
        grid=(nb, nt),
        in_specs=in_specs,
        out_specs=(pl.BlockSpec((1, TILE, d), out), pl.BlockSpec((1, TILE, d), out)),
        compiler_params=_cparams(("parallel", "parallel")),
        name="post",
    )(x, *scans, *([p3] * 6), *[w.reshape(1, d) for w in norm_ws], *w_bs, w_out, mods,
      norm_ffn_w.reshape(1, d), mods, mods)


def _ffn_in_kernel(a_ref, w_ref, o_ref):
    acc = _dot(a_ref[...], w_ref[...])
    half = acc.shape[1] // 2
    o_ref[...] = (_silu(acc[:, :half]) * acc[:, half:]).astype(o_ref.dtype)


def _ffn_in(h, w_gu, layer, half):
    m, k = h.shape
    n_half = w_gu.shape[-1] // 2
    tm = _row_tile(m)
    return pl.pallas_call(
        _ffn_in_kernel,
        out_shape=jax.ShapeDtypeStruct((m, n_half), BF16),
        grid=(n_half // half, m // tm),
        in_specs=[pl.BlockSpec((tm, k), lambda j, i: (i, 0)),
                  pl.BlockSpec((None, k, 2 * half), lambda j, i: (layer, 0, j))],
        out_specs=pl.BlockSpec((tm, half), lambda j, i: (i, j)),
        compiler_params=_cparams(("parallel", "parallel")),
        name="ffn_in",
    )(h, w_gu)


def _ffn_out_kernel(a_ref, w_ref, x_ref, g_ref, nw_ref, sc_ref, sh_ref, xo_ref, ho_ref):
    x_new = x_ref[0] + g_ref[0] * _dot(a_ref[0], w_ref[...])
    xo_ref[0] = x_new
    ho_ref[0] = (_rms(x_new) * nw_ref[...] * (1.0 + sc_ref[0]) + sh_ref[0]).astype(ho_ref.dtype)


def _ffn_out_last_kernel(a_ref, w_ref, x_ref, g_ref, nw_ref, o_ref):
    x_new = x_ref[0] + g_ref[0] * _dot(a_ref[0], w_ref[...])
    o_ref[0] = _rms(x_new) * nw_ref[...]


def _ffn_out(a, w, layer, x, mods, n_ctx, tile0, next_norm_w, next_mods):
    nb, t, d = x.shape
    k = a.shape[-1]
    ctx_tiles = n_ctx // TILE
    tok = lambda b, i: (b, i, 0)
    mod = lambda k_: pl.BlockSpec((1, 1, d), _mod_row(nb, k_, ctx_tiles, tile0))
    in_specs = [pl.BlockSpec((1, TILE, k), tok),
                pl.BlockSpec((None, k, d), lambda b, i: (layer, 0, 0), pipeline_mode=pl.Buffered(1)),
                pl.BlockSpec((1, TILE, d), tok),
                mod(5),
                pl.BlockSpec((1, d), lambda b, i: (0, 0))]
    args = [a, w, x, mods, next_norm_w.reshape(1, d)]
    if next_mods is None:
        body = _ffn_out_last_kernel
        out_shape = jax.ShapeDtypeStruct((nb, t, d), F32)
        out_specs = pl.BlockSpec((1, TILE, d), tok)
    else:
        body = _ffn_out_kernel
        in_specs += [mod(1), mod(0)]
        args += [next_mods, next_mods]
        out_shape = (jax.ShapeDtypeStruct((nb, t, d), F32), jax.ShapeDtypeStruct((nb, t, d), BF16))
        out_specs = (pl.BlockSpec((1, TILE, d), tok), pl.BlockSpec((1, TILE, d), tok))
    return pl.pallas_call(
        body,
        out_shape=out_shape,
        grid=(nb, t // TILE),
        in_specs=in_specs,
        out_specs=out_specs,
        compiler_params=_cparams(("parallel", "parallel")),
        name="ffn_out",
    )(*args)


def _proj_weights(w_in):
    cols = lambda names: [w_in[..., _IN_OFF[nm]:_IN_OFF[nm] + _IN_W[nm]] for nm in names]
    main = jnp.concatenate(cols(_P_ORDER), axis=-1)
    small = jnp.pad(jnp.concatenate(cols(_SMALL), axis=-1), ((0, 0), (0, 0), (0, LANES - N_SMALL_USED)))
    return main.astype(BF16), small.astype(BF16)


def _ffn_weight(w_ffn_in, half):
    d_ff = w_ffn_in.shape[-1] // 2
    cols = []
    for j in range(d_ff // half):
        cols.append(w_ffn_in[..., j * half:(j + 1) * half])
        cols.append(w_ffn_in[..., d_ff + j * half:d_ff + (j + 1) * half])
    return jnp.concatenate(cols, axis=-1).astype(BF16)


def kernel(x, c, ctx, c_ctx, w_mod, b_mod, norm_mix_w, norm_ffn_w, w_in, ssd_conv_w, ssd_conv_b, ssd_dt_bias, ssd_a_log, ssd_d, ssd_norm_w, ml_conv_w, ml_conv_b, ml_i_bias, ml_f_bias, ml_norm_w, gla_a_up, gla_a_bias, gla_norm_w, w_b_ssd, w_b_ml, w_b_gla, w_out, w_ffn_in, w_ffn_out, final_norm_w):
    nb, n_lat, d = x.shape
    n_ctx = ctx.shape[1]
    t = n_ctx + n_lat
    depth = w_in.shape[0]
    d_ff = w_ffn_out.shape[1]
    assert n_ctx == TILE == GLA_SUB and n_lat % TILE == 0 and n_lat // GRID_W == 32
    ffn_half = d_ff // 2

    c16 = jnp.pad(jnp.concatenate([c, c_ctx[None]], axis=0), ((0, 2 * SUBLANES - nb - 1), (0, 0)))
    mods = [_modulation(c16, w_mod, b_mod, l).reshape(2 * SUBLANES * 6, 1, d) for l in range(depth)]
    xs, h = _norm_mod(ctx, x, norm_mix_w[0], mods[0], 0, 1)
    conv_w = jnp.concatenate([ssd_conv_w, ml_conv_w], axis=-1)
    conv_b = jnp.concatenate([ssd_conv_b, ml_conv_b], axis=-1)
    conv_post = jnp.concatenate([jnp.ones((ssd_conv_w.shape[-1] + ML_QK_WIDTH,), F32),
                                 jnp.full((ML_QK_WIDTH,), ML_QK_DIM ** -0.5, F32)])
    w_main, w_small = _proj_weights(w_in)
    w_gu = _ffn_weight(w_ffn_in, ffn_half)
    w_bs = [w.astype(BF16) for w in (w_b_ssd, w_b_ml, w_b_gla)]
    w_out_b = w_out.astype(BF16)
    w_ffn_out_b = w_ffn_out.astype(BF16)
    for l in range(depth):
        last = l == depth - 1
        h2d = h.reshape(nb * t, d)
        tn = N_PROJ // 5
        conv_tile = _P_OFF["s_x"] // tn
        assert conv_tile * tn == _P_OFF["s_x"] and N_PROJ - _P_OFF["s_x"] == tn == CV_WIDTH
        m_tiles = nb * t // _row_tile(nb * t)
        (pc,), (sm,) = _run_together("proj_first", (1, m_tiles),
                                     _mm_part(h2d, w_main, l, tn, BF16, conv_tile, 1),
                                     _mm_part(h2d, w_small, l, LANES, F32, 0, 1))
        pc3, sm3 = pc.reshape(nb, t, tn), sm.reshape(nb, t, LANES)
        grid = (conv_tile, m_tiles)
        (p,), (cv,) = _run_together(
            "proj_conv", grid, _mm_part(h2d, w_main, l, tn, BF16, 0, conv_tile),
            _conv_part(pc3, conv_w[l], conv_b[l], conv_post, n_ctx, grid))
        p3 = p.reshape(nb, t, conv_tile * tn)
        d_e = jnp.repeat(ssd_d[l], SSD_HEAD_DIM).reshape(1, SSD_WIDTH)
        ssd_states, ssd_out = _ssd_parts(cv, sm3, ssd_dt_bias[l], ssd_a_log[l], d_e, n_ctx)
        ml_states, ml_out = _mlstm_parts(cv, p3, sm3, ml_i_bias[l], ml_f_bias[l], n_ctx)
        grid = (nb, t // TILE)
        ssd_st, ml_st = _run_together("bwd_states", grid, ssd_states, ml_states)
        (y,), (hm,) = _run_together("ssd_mlstm", grid, ssd_out(*ssd_st), ml_out(*ml_st))
        gla = [_gla_part(p3, sm3, gla_a_up[l, k], gla_a_bias[l, k], n_ctx, bool(k), not last) for k in range(2)]
        (og_f,), (og_b,) = _run_together("gla", (nb, GRID_W // GLA_COLS + 1), *gla, phases=GLA_PHASES)
        og_f, og_b = og_f.reshape(nb, t, GLA_V_WIDTH), og_b.reshape(nb, t, GLA_V_WIDTH)
        tile0 = n_ctx // TILE if last else 0
        xs, h2 = _post(xs, (y, hm, og_f, og_b), p3, (ssd_norm_w[l], ml_norm_w[l], gla_norm_w[l]),
                       w_bs, w_out_b, l, norm_ffn_w[l], mods[l], n_ctx, tile0)
        nt = xs.shape[1]
        a = _ffn_in(h2.reshape(nb * nt, d), w_gu, l, ffn_half)
        a = a.reshape(nb, nt, d_ff)
        if last:
            return _ffn_out(a, w_ffn_out_b, l, xs, mods[l], n_ctx, tile0, final_norm_w, None)
        xs, h = _ffn_out(a, w_ffn_out_b, l, xs, mods[l], n_ctx, tile0,
                         norm_mix_w[l + 1], mods[l + 1])
```

```python
import functools

import numpy as np
import jax
import jax.numpy as jnp
from jax import lax
from jax.experimental import pallas as pl
from jax.experimental.pallas import tpu as pltpu

F32 = jnp.float32
BF16 = jnp.bfloat16

EPS = 1e-6
LOG2E = 1.4426950408889634
GRID_W = 64
SSD_HEADS = 16
SSD_HEAD_DIM = 64
SSD_WIDTH = 1024
SSD_STATE = 64
SSD_BC = 128
ML_HEADS = 4
ML_QK_DIM = 128
ML_V_DIM = 256
ML_QK_WIDTH = 512
ML_V_WIDTH = 1024
GLA_HEADS = 4
GLA_K_DIM = 128
GLA_V_DIM = 256
GLA_K_WIDTH = 512
GLA_V_WIDTH = 1024
GLA_RANK = 16
GLA_TAU = 16.0
GLA_CHUNK = 64

LANES = 128
SUBLANES = 8
BF16_SUBLANES = 16
VMEM_LIMIT = 56 * 1024 * 1024

TILE = 256

_IN_NAMES = ("s_x", "s_z", "s_b", "s_c", "dt_f", "dt_b",
             "m_q", "m_k", "m_v", "m_o", "i_f", "i_b", "f_f", "f_b",
             "g_q", "g_k", "g_v", "g_g", "a_f", "a_b",
             "gate_ssd", "gate_ml", "gate_gla")
_IN_WIDTHS = (1024, 1024, 128, 128, 16, 16,
              512, 512, 1024, 1024, 4, 4, 4, 4,
              512, 512, 1024, 1024, 16, 16,
              1024, 1024, 1024)
_IN_OFF = dict(zip(_IN_NAMES, np.concatenate([[0], np.cumsum(_IN_WIDTHS)[:-1]]).tolist()))
_IN_W = dict(zip(_IN_NAMES, _IN_WIDTHS))

_P_ORDER = ("s_z", "m_o", "g_g", "gate_ssd", "gate_ml", "gate_gla", "m_v", "g_v",
            "g_q", "g_k", "s_x", "s_b", "s_c", "m_q", "m_k")
_P_OFF = {}
_o = 0
for _n in _P_ORDER:
    _P_OFF[_n] = _o
    _o += _IN_W[_n]
N_PROJ = _o
_SMALL = ("dt_f", "dt_b", "i_f", "i_b", "f_f", "f_b", "a_f", "a_b")
_SM_OFF = {}
_s = 0
for _n in _SMALL:
    _SM_OFF[_n] = _s
    _s += _IN_W[_n]
N_SMALL_USED = _s


def _cparams(sem):
    return pltpu.CompilerParams(dimension_semantics=sem, vmem_limit_bytes=VMEM_LIMIT)


def _sigmoid(x):
    return 0.5 * jnp.tanh(0.5 * x) + 0.5


def _silu(x):
    h = 0.5 * x
    return h + h * jnp.tanh(h)


def _softplus(x):
    return jnp.maximum(x, 0.0) + jnp.log1p(jnp.exp(-jnp.abs(x)))


def _log_sigmoid(x):
    return jnp.minimum(x, 0.0) - jnp.log(1.0 + jnp.exp(-jnp.abs(x)))


def _split(x, n):
    out = []
    r = x
    for _ in range(n):
        p = r.astype(BF16)
        out.append(p)
        r = r - p.astype(F32)
    return out


def _dot(a, b):
    return jnp.dot(a, b, preferred_element_type=F32)


def _dot_nt(a, b):
    return lax.dot_general(a, b, (((1,), (1,)), ((), ())), preferred_element_type=F32)


def _dot_tn(a, b):
    return lax.dot_general(a, b, (((0,), (0,)), ((), ())), preferred_element_type=F32)


def _dot_exact_lhs(t, x, pieces):
    return sum(_dot(t, p) for p in _split(x, pieces))


def _dot_hp(a, b):
    ah, am = _split(a, 2)
    bh, bm = _split(b, 2)
    return _dot(ah, bh) + _dot(ah, bm) + _dot(am, bh)


def _causal(n, rev):
    t = lax.broadcasted_iota(jnp.int32, (n, n), 0)
    s = lax.broadcasted_iota(jnp.int32, (n, n), 1)
    return (s >= t) if rev else (s <= t)


def _rms(x):
    return x * lax.rsqrt(jnp.mean(x * x, axis=-1, keepdims=True) + EPS)


def _mod_kernel(c_ref, w_ref, b_ref, o_ref):
    o_ref[...] = _dot_hp(_silu(c_ref[...]), w_ref[...]) + b_ref[...]


def _modulation(c16, w_mod, b_mod, layer):
    rows, d = c16.shape
    n = w_mod.shape[-1]
    tn = 1536
    return pl.pallas_call(
        _mod_kernel,
        out_shape=jax.ShapeDtypeStruct((rows, n), F32),
        grid=(n // tn,),
        in_specs=[pl.BlockSpec((rows, d), lambda j: (0, 0)),
                  pl.BlockSpec((None, d, tn), lambda j: (layer, 0, j)),
                  pl.BlockSpec((None, 1, tn), lambda j: (layer, 0, j))],
        out_specs=pl.BlockSpec((rows, tn), lambda j: (0, j)),
        compiler_params=_cparams(("arbitrary",)),
        name="modulation",
    )(c16, w_mod, b_mod.reshape(b_mod.shape[0], 1, n))


def _mod_row(nb, k, ctx_tiles, tile0=0):
    return lambda b, i: (jnp.where(i + tile0 < ctx_tiles, nb, b) * 6 + k, 0, 0)


def _norm_mod_kernel(ctx_ref, x_ref, w_ref, sc_ref, sh_ref, xo_ref, ho_ref):
    x = jnp.where(pl.program_id(1) == 0, ctx_ref[0], x_ref[0])
    xo_ref[0] = x
    ho_ref[0] = (_rms(x) * w_ref[...] * (1.0 + sc_ref[0]) + sh_ref[0]).astype(ho_ref.dtype)


def _norm_mod(ctx, x, w, mods, k_shift, k_scale):
    nb, n_lat, d = x.shape
    n_ctx = ctx.shape[1]
    assert n_ctx == TILE
    t = n_ctx + n_lat
    tok = lambda b, i: (b, i, 0)
    return pl.pallas_call(
        _norm_mod_kernel,
        out_shape=(jax.ShapeDtypeStruct((nb, t, d), F32), jax.ShapeDtypeStruct((nb, t, d), BF16)),
        grid=(nb, t // TILE),
        in_specs=[pl.BlockSpec((1, TILE, d), lambda b, i: (b, 0, 0)),
                  pl.BlockSpec((1, TILE, d), lambda b, i: (b, jnp.maximum(i - 1, 0), 0)),
                  pl.BlockSpec((1, d), lambda b, i: (0, 0)),
                  pl.BlockSpec((1, 1, d), _mod_row(nb, k_scale, n_ctx // TILE)),
                  pl.BlockSpec((1, 1, d), _mod_row(nb, k_shift, n_ctx // TILE))],
        out_specs=(pl.BlockSpec((1, TILE, d), tok), pl.BlockSpec((1, TILE, d), tok)),
        compiler_params=_cparams(("parallel", "arbitrary")),
        name="norm_mod",
    )(ctx, x, w.reshape(1, d), mods, mods)


MM_STAGE = 256


def _mm_body(a_ref, w_ref, o_ref):
    a = a_ref[...]
    width = o_ref.shape[1]
    stage = min(MM_STAGE, width)
    for c0 in range(0, width, stage):
        o_ref[:, c0:c0 + stage] = _dot(a, w_ref[:, c0:c0 + stage]).astype(o_ref.dtype)
        yield


def _row_tile(m):
    return 512 if m % 512 == 0 else TILE


def _mm_part(a, w, layer, tn, out_dtype, tile0, n_tiles):
    m, k = a.shape
    tm = _row_tile(m)
    return (_mm_body,
            [pl.BlockSpec((tm, k), lambda j, i: (i, 0)),
             pl.BlockSpec((None, k, tn), lambda j, i: (layer, 0, j + tile0))],
            [a, w], [jax.ShapeDtypeStruct((m, n_tiles * tn), out_dtype)],
            [pl.BlockSpec((tm, tn), lambda j, i: (i, j))], [])


CONV_K = 5
CONV_ROWS = 128
CONV_X = SSD_WIDTH // LANES
CONV_BC = 2 * SSD_BC // LANES
CONV_QK = 2 * ML_QK_WIDTH // LANES
CV_QK = SSD_WIDTH
CV_BC = SSD_WIDTH + 2 * ML_QK_WIDTH
CV_WIDTH = CV_BC + 2 * SSD_BC


def _conv_body(u_ref, w_ref, b_ref, s_ref, o_ref, pad_ref, *, n_ctx):
    t, c = u_ref.shape[1], u_ref.shape[2]
    half = CONV_K // 2
    zeros = jnp.zeros((SUBLANES, c), F32)
    w = w_ref[...]
    bias = b_ref[...]
    post = s_ref[...]
    for s0, n in ((0, n_ctx), (n_ctx, t - n_ctx)):
        pad_ref[0:SUBLANES, :] = zeros
        pad_ref[SUBLANES:SUBLANES + n, :] = u_ref[0, s0:s0 + n, :].astype(F32)
        pad_ref[SUBLANES + n:2 * SUBLANES + n, :] = zeros
        for r0 in range(0, n, CONV_ROWS):
            acc = bias
            for j in range(CONV_K):
                lo = SUBLANES - half + j + r0
                acc = acc + w[j:j + 1, :] * pad_ref[lo:lo + CONV_ROWS, :]
            o_ref[0, s0 + r0:s0 + r0 + CONV_ROWS, :] = (_silu(acc) * post).astype(o_ref.dtype)
            if (r0 // CONV_ROWS) % 4 == 3:
                yield


def _conv_part(u3, w, b, post_scale, n_ctx, grid):
    nb, t, width = u3.shape
    n_cb = width // LANES
    steps = grid[0] * grid[1]
    rep = steps // (nb * n_cb)
    assert steps == rep * nb * n_cb and n_cb == CONV_X + CONV_BC + CONV_QK
    blk = lambda j, i: (j * grid[1] + i) // rep
    chan = lambda j, i: blk(j, i) % n_cb
    out_chan = lambda c: jnp.where(c < CONV_X, c, jnp.where(c < CONV_X + CONV_BC, c + CONV_QK, c - CONV_BC))
    vec = lambda rows: pl.BlockSpec((rows, LANES), lambda j, i: (0, chan(j, i)))
    return (functools.partial(_conv_body, n_ctx=n_ctx),
            [pl.BlockSpec((1, t, LANES), lambda j, i: (blk(j, i) // n_cb, 0, chan(j, i))),
             vec(CONV_K), vec(1), vec(1)],
            [u3, w, b.reshape(1, width), post_scale.reshape(1, width)],
            [jax.ShapeDtypeStruct((nb, t, width), BF16)],
            [pl.BlockSpec((1, t, LANES), lambda j, i: (blk(j, i) // n_cb, 0, out_chan(chan(j, i))))],
            [pltpu.VMEM((t + 2 * SUBLANES, LANES), F32)])


def _tile_order(n_tiles, rev, ctx_tiles=1):
    if rev:
        return lambda i: jnp.where(i < ctx_tiles, ctx_tiles - 1 - i, n_tiles - 1 + ctx_tiles - i)
    return lambda i: i


SSD_PAIRS = SSD_HEADS // 2
SSD_B_OFF = SSD_HEADS
SSD_TILE = 256


def _ssd_gates(sm_ref, dtb_ref, alog_ref, n, both):
    lane = lax.broadcasted_iota(jnp.int32, (1, LANES), 1)
    dt = _softplus(sm_ref[0] + dtb_ref[...])
    la = dt * jnp.where(lane < 2 * SSD_HEADS, -jnp.exp(alog_ref[...]), 0.0)
    parts = _split(la, 2)
    upp = jnp.where(_causal(n, True), 1.0, 0.0).astype(BF16)
    cum = sum(_dot(upp, p) for p in parts)
    if both:
        low = jnp.where(_causal(n, False), 1.0, 0.0).astype(BF16)
        cum = jnp.where(lane < SSD_HEADS, sum(_dot(low, p) for p in parts), cum)
    return lane, dt, cum


def _expand(a, e):
    return _dot(a.astype(BF16), e)


def _group_dup(v, g, lo):
    other = pltpu.roll(v, SSD_STATE, axis=1)
    return jnp.where(lo, v, other) if g == 0 else jnp.where(lo, other, v)


def _ssd_state_step(st_ref, j, bw, xp, elast, off):
    r = lax.broadcasted_iota(jnp.int32, (LANES, LANES), 0) < SSD_STATE
    c = lax.broadcasted_iota(jnp.int32, (LANES, LANES), 1) < SSD_HEAD_DIM
    dec = jnp.where(r, elast[:, off + 2 * j:off + 2 * j + 1], elast[:, off + 2 * j + 1:off + 2 * j + 2])
    st_ref[j] = jnp.where(r == c, dec * st_ref[j] + _dot_tn(bw.astype(BF16), xp), 0.0)


def _interleave(*bodies):
    live = list(bodies)
    while live:
        for body in list(live):
            if next(body, StopIteration) is StopIteration:
                live.remove(body)


def _ssd_states_body(x_ref, bc_ref, sm_ref, dtb_ref, alog_ref, eb_ref, o_ref, st_ref):
    n = x_ref.shape[1]

    @pl.when(pl.program_id(1) == 0)
    def _init():
        st_ref[...] = jnp.zeros_like(st_ref)

    o_ref[0, 0] = st_ref[...].astype(o_ref.dtype)
    yield
    lane, dt, cum = _ssd_gates(sm_ref, dtb_ref, alog_ref, n, False)
    lo = lane < SSD_HEAD_DIM
    last = cum[0:1, :]
    yield
    wst = _expand(jnp.exp(last - cum) * dt, eb_ref[...])
    elast = jnp.exp(last)
    b128 = bc_ref[0, :, :SSD_BC].astype(F32)
    for j in range(SSD_PAIRS):
        sl = slice(LANES * j, LANES * (j + 1))
        bw = _group_dup(b128, j // (SSD_PAIRS // 2), lo) * wst[:, sl]
        _ssd_state_step(st_ref, j, bw, x_ref[0, :, sl], elast, SSD_B_OFF)
        if j % 2:
            yield


def _ssd_out_body(x_ref, bc_ref, sm_ref, dtb_ref, alog_ref, ef_ref, eb_ref, d_ref, sb_ref, o_ref, st_ref):
    n = x_ref.shape[1]

    @pl.when(pl.program_id(1) == 0)
    def _init():
        st_ref[...] = jnp.zeros_like(st_ref)

    zero_b = jnp.zeros((), BF16)
    lo = lax.broadcasted_iota(jnp.int32, (1, LANES), 1) < SSD_HEAD_DIM
    b128_b = bc_ref[0, :, :SSD_BC]
    c128_b = bc_ref[0, :, SSD_BC:]
    cb_all = [_dot_nt(jnp.where(lo if g == 0 else jnp.logical_not(lo), c128_b, zero_b), b128_b)
              for g in range(2)]
    pair_x = lambda j: x_ref[0, :, LANES * j:LANES * (j + 1)]
    rhs_all = [jnp.concatenate([jnp.where(lo, pair_x(j), zero_b), jnp.where(lo, zero_b, pair_x(j)),
                                st_ref[j].astype(BF16), sb_ref[0, 0, j]], axis=0) for j in range(SSD_PAIRS)]
    yield
    lane, dt, cum = _ssd_gates(sm_ref, dtb_ref, alog_ref, n, True)
    is_f = lane < SSD_HEADS
    ldt = jnp.log(dt)
    dsum = jnp.log(dt + pltpu.roll(dt, LANES - SSD_B_OFF, axis=1))
    rt = (jnp.where(lane < 2 * SSD_HEADS, cum - ldt, pltpu.roll(dsum, 2 * SSD_HEADS, axis=1)) * LOG2E).T
    cum2 = cum * LOG2E
    yield
    last = jnp.where(is_f, cum[n - 1:n, :], cum[0:1, :])
    elast = jnp.exp(last)
    ecum = jnp.exp(cum)
    ecum_f = _expand(ecum, ef_ref[...])
    ecum_b = _expand(ecum, eb_ref[...])
    wst = _expand(jnp.exp(last - cum) * dt, ef_ref[...])
    bc = bc_ref[0].astype(F32)
    b128, c128 = bc[:, :SSD_BC], bc[:, SSD_BC:]
    ti = lax.broadcasted_iota(jnp.int32, (n, n), 0)
    si = lax.broadcasted_iota(jnp.int32, (n, n), 1)
    below, above = si < ti, si > ti
    half = SSD_PAIRS // 2
    yield
    for g in range(2):
        cb = cb_all[g]
        cdup = _group_dup(c128, g, lo)
        for j in range(g * half, (g + 1) * half):
            ms = []
            for h in (2 * j, 2 * j + 1):
                e_f = cum2[:, h:h + 1] - rt[h:h + 1, :]
                e_b = cum2[:, SSD_B_OFF + h:SSD_B_OFF + h + 1] - rt[SSD_B_OFF + h:SSD_B_OFF + h + 1, :]
                e = jnp.where(below, e_f, jnp.where(above, e_b, rt[2 * SSD_HEADS + h:2 * SSD_HEADS + h + 1, :]))
                ms.append((cb * jnp.exp2(e)).astype(BF16))
            sl = slice(LANES * j, LANES * (j + 1))
            cs_f = (cdup * ecum_f[:, sl]).astype(BF16)
            cs_b = (cdup * ecum_b[:, sl]).astype(BF16)
            lhs = jnp.concatenate(ms + [cs_f, cs_b], axis=1)
            o_ref[0, :, sl] = _dot(lhs, rhs_all[j]) + d_ref[:, sl] * pair_x(j).astype(F32)
            yield
    for j in range(SSD_PAIRS):
        sl = slice(LANES * j, LANES * (j + 1))
        _ssd_state_step(st_ref, j, _group_dup(b128, j // half, lo) * wst[:, sl], x_ref[0, :, sl], elast, 0)
        if j % 2:
            yield


def _run_together(name, grid, *parts, phases=None):
    n_in = [len(p[1]) for p in parts]
    n_out = [len(p[3]) for p in parts]
    n_scr = [len(p[5]) for p in parts]

    def kern(*refs):
        ins, outs, scr = refs[:sum(n_in)], refs[sum(n_in):sum(n_in) + sum(n_out)], refs[sum(n_in) + sum(n_out):]

        def run(key):
            bodies = []
            for k, p in enumerate(parts):
                take = lambda seq, counts: seq[sum(counts[:k]):sum(counts[:k + 1])]
                body = p[0][key] if isinstance(p[0], dict) else p[0]
                bodies.append(body(*take(ins, n_in), *take(outs, n_out), *take(scr, n_scr)))
            _interleave(*bodies)

        if phases is None:
            run(None)
        else:
            for pred, key in phases:
                pl.when(pred(pl.program_id(1)))(functools.partial(run, key))

    res = pl.pallas_call(
        kern,
        out_shape=tuple(s for p in parts for s in p[3]),
        grid=grid,
        in_specs=[s for p in parts for s in p[1]],
        out_specs=tuple(s for p in parts for s in p[4]),
        scratch_shapes=[s for p in parts for s in p[5]],
        compiler_params=_cparams(("arbitrary", "arbitrary")),
        name=name,
    )(*[a for p in parts for a in p[2]])
    return [list(res[sum(n_out[:k]):sum(n_out[:k + 1])]) for k in range(len(parts))]


def _ssd_parts(cvs, sm3, dt_bias, a_log, d_e, n_ctx):
    nb, t, _ = cvs.shape
    tile = SSD_TILE
    nt = t // tile
    row = lambda v: jnp.pad(v.reshape(1, -1), ((0, 0), (0, LANES - 2 * SSD_HEADS)))
    const = lambda b, i: (0, 0)
    specs = lambda order: [
        pl.BlockSpec((1, tile, SSD_WIDTH), lambda b, i: (b, order(i), 0)),
        pl.BlockSpec((1, tile, 2 * SSD_BC), lambda b, i: (b, order(i), CV_BC // (2 * SSD_BC))),
        pl.BlockSpec((1, tile, LANES), lambda b, i: (b, order(i), 0)),
        pl.BlockSpec((1, LANES), const),
        pl.BlockSpec((1, LANES), const)]
    st_block = (1, 1, SSD_PAIRS, LANES, LANES)
    args = (cvs, cvs, sm3, row(dt_bias), row(a_log))
    sel = np.zeros((2, LANES, SSD_WIDTH), np.float32)
    for h in range(SSD_HEADS):
        sel[0, h, h * SSD_HEAD_DIM:(h + 1) * SSD_HEAD_DIM] = 1.0
        sel[1, SSD_B_OFF + h, h * SSD_HEAD_DIM:(h + 1) * SSD_HEAD_DIM] = 1.0
    e_f, e_b = jnp.asarray(sel[0], BF16), jnp.asarray(sel[1], BF16)
    e_spec = pl.BlockSpec((LANES, SSD_WIDTH), const)
    bwd = _tile_order(nt, True, n_ctx // tile)
    fwd = _tile_order(nt, False)
    scratch = [pltpu.VMEM(st_block[2:], F32)]
    states = (_ssd_states_body, specs(bwd) + [e_spec], list(args) + [e_b],
              [jax.ShapeDtypeStruct((nb, nt) + st_block[2:], BF16)],
              [pl.BlockSpec(st_block, lambda b, i: (b, bwd(i), 0, 0, 0))], scratch)
    out = lambda states_b: (
        _ssd_out_body,
        specs(fwd) + [e_spec, e_spec, pl.BlockSpec((1, SSD_WIDTH), const),
                      pl.BlockSpec(st_block, lambda b, i: (b, i, 0, 0, 0))],
        list(args) + [e_f, e_b, d_e, states_b],
        [jax.ShapeDtypeStruct((nb, t, SSD_WIDTH), F32)],
        [pl.BlockSpec((1, tile, SSD_WIDTH), lambda b, i: (b, i, 0))], scratch)
    return states, out


ML_GATE = 32
ML_ND = 2 * ML_HEADS
ML_AUG = ML_V_DIM + LANES


def _ml_gates(sm_ref, ib_ref, fb_ref, n, both):
    lane = lax.broadcasted_iota(jnp.int32, (1, LANES), 1)
    valid = (lane >= ML_GATE) & (lane < ML_GATE + ML_ND)
    is_f = lane < ML_GATE + ML_HEADS
    sm = sm_ref[0]
    li = sm + ib_ref[...]
    lf = pltpu.roll(_log_sigmoid(sm + fb_ref[...]), LANES - ML_ND, axis=1)
    parts = _split(jnp.where(valid, lf, 0.0), 2)
    upp = jnp.where(_causal(n, True), 1.0, 0.0).astype(BF16)
    bcum = sum(_dot(upp, p) for p in parts)
    if both:
        low = jnp.where(_causal(n, False), 1.0, 0.0).astype(BF16)
        bcum = jnp.where(is_f, sum(_dot(low, p) for p in parts), bcum)
    return valid, is_f, bcum, jnp.where(valid, li - bcum, 0.0)


def _ml_state_step(cn_ref, h, cn, keep, k, ws_dense, v_aug):
    w3 = jnp.concatenate([ws_dense.astype(BF16)] * (ML_AUG // LANES), axis=1)
    cn_ref[h] = keep * cn + _dot_tn(k, w3 * v_aug)


def _ml_v_aug(v_ref, h, n):
    return jnp.concatenate([v_ref[0, :, ML_V_DIM * h:ML_V_DIM * (h + 1)], jnp.ones((n, LANES), BF16)], axis=1)


def _ml_states_body(qk_ref, v_ref, sm_ref, ib_ref, fb_ref, sel_ref, cn_out, m_out, cn_ref, m_ref):
    n = qk_ref.shape[1]

    @pl.when(pl.program_id(1) == 0)
    def _init():
        cn_ref[...] = jnp.zeros_like(cn_ref)
        m_ref[...] = jnp.zeros_like(m_ref)

    cn_out[0, 0] = cn_ref[...].astype(cn_out.dtype)
    m_out[0, 0] = m_ref[...]
    yield
    valid, is_f, bcum, a = _ml_gates(sm_ref, ib_ref, fb_ref, n, False)
    m_prev = m_ref[0:1, :]
    g_last = jnp.maximum(m_prev, jnp.max(a, axis=0, keepdims=True))
    yield
    ws = _expand(jnp.exp(a - g_last), sel_ref[:, ML_HEADS * LANES:])
    keep = jnp.exp(m_prev - g_last)
    for h in range(ML_HEADS):
        lane_b = ML_GATE + ML_HEADS + h
        k = qk_ref[0, :, ML_QK_WIDTH + ML_QK_DIM * h:ML_QK_WIDTH + ML_QK_DIM * (h + 1)]
        _ml_state_step(cn_ref, h, cn_ref[h], keep[:, lane_b:lane_b + 1], k,
                       ws[:, LANES * h:LANES * (h + 1)], _ml_v_aug(v_ref, h, n))
        yield
    m_ref[...] = jnp.broadcast_to(bcum[0:1, :] + g_last, m_ref.shape)


def _ml_out_body(qk_ref, v_ref, sm_ref, ib_ref, fb_ref, sel_ref, cnb_ref, mb_ref, o_ref, cn_ref, m_ref):
    n = qk_ref.shape[1]

    @pl.when(pl.program_id(1) == 0)
    def _init():
        cn_ref[...] = jnp.zeros_like(cn_ref)
        m_ref[...] = jnp.zeros_like(m_ref)

    head_q = lambda h: qk_ref[0, :, ML_QK_DIM * h:ML_QK_DIM * (h + 1)]
    head_k = lambda h: qk_ref[0, :, ML_QK_WIDTH + ML_QK_DIM * h:ML_QK_WIDTH + ML_QK_DIM * (h + 1)]
    qk_all = [_dot_nt(head_q(h), head_k(h)) for h in range(ML_HEADS)]
    inter_all = [_dot(head_q(h), jnp.concatenate([cn_ref[h].astype(BF16), cnb_ref[0, 0, h]], axis=1))
                 for h in range(ML_HEADS)]
    yield
    valid, is_f, bcum, a = _ml_gates(sm_ref, ib_ref, fb_ref, n, True)
    m_prev = jnp.where(is_f, m_ref[0:1, :], mb_ref[0, 0, 0:1, :])
    a_t = a.T
    pre = suf = a_t[ML_GATE:ML_GATE + ML_ND, :]
    pos = lax.broadcasted_iota(jnp.int32, (ML_ND, n), 1)
    k = 1
    while k < n:
        pre = jnp.maximum(pre, jnp.where(pos >= k, pltpu.roll(pre, k, axis=1), -jnp.inf))
        suf = jnp.maximum(suf, jnp.where(pos < n - k, pltpu.roll(suf, n - k, axis=1), -jnp.inf))
        k *= 2
    run = jnp.where(lax.broadcasted_iota(jnp.int32, (ML_ND, n), 0) < ML_HEADS, pre, suf)
    run = jnp.concatenate([jnp.zeros((ML_GATE, n), F32), run,
                           jnp.zeros((LANES - ML_GATE - ML_ND, n), F32)], axis=0).T
    g = jnp.maximum(m_prev, run)
    m_t = bcum + g
    yield
    floor = jnp.exp(-m_t)
    dense = lambda x, lane_: jnp.broadcast_to(x[:, lane_:lane_ + 1], (n, LANES))
    diag_t = jnp.exp(a - g).T
    g_last = g[n - 1:n, :]
    ws = _expand(jnp.exp(a - g_last), sel_ref[:, :ML_HEADS * LANES])
    keep = jnp.exp(m_prev - g_last)
    ti = lax.broadcasted_iota(jnp.int32, (n, n), 0)
    si = lax.broadcasted_iota(jnp.int32, (n, n), 1)
    not_above, above, on_diag = si <= ti, si > ti, si == ti
    wide = lambda x: jnp.concatenate([x] * (n // LANES), axis=1)
    for h in range(ML_HEADS):
        lanes = (ML_GATE + h, ML_GATE + ML_HEADS + h)
        g_d = [dense(g, ln) for ln in lanes]
        qk, inter = qk_all[h], inter_all[h]
        e = jnp.where(not_above, a_t[lanes[0]:lanes[0] + 1, :] - wide(g_d[0]),
                      a_t[lanes[1]:lanes[1] + 1, :] - wide(g_d[1]))
        p = qk * jnp.exp(e)
        p_f = jnp.where(not_above, p, 0.0).astype(BF16)
        p_b = jnp.where(above, p, jnp.where(on_diag, qk * diag_t[lanes[1]:lanes[1] + 1, :], 0.0)).astype(BF16)
        intra = _dot(jnp.concatenate([p_f, p_b], axis=0), _ml_v_aug(v_ref, h, n))
        out = None
        for d in range(2):
            w_inter = jnp.exp(m_prev[:, lanes[d]:lanes[d] + 1] - g_d[d])
            s = (intra[n * d:n * (d + 1)] + jnp.concatenate([w_inter] * (ML_AUG // LANES), axis=1)
                 * inter[:, ML_AUG * d:ML_AUG * (d + 1)])
            rn = 1.0 / jnp.maximum(jnp.abs(s[:, ML_V_DIM:]), dense(floor, lanes[d]))
            hid = s[:, :ML_V_DIM] * jnp.concatenate([rn] * (ML_V_DIM // LANES), axis=1)
            out = hid if out is None else out + hid
        o_ref[0, :, ML_V_DIM * h:ML_V_DIM * (h + 1)] = out
        yield
    for h in range(ML_HEADS):
        lane_f = ML_GATE + h
        _ml_state_step(cn_ref, h, cn_ref[h], keep[:, lane_f:lane_f + 1], head_k(h),
                       ws[:, LANES * h:LANES * (h + 1)], _ml_v_aug(v_ref, h, n))
        yield
    m_ref[...] = jnp.broadcast_to(m_t[n - 1:n, :], m_ref.shape)


def _mlstm_parts(cvm, p3, sm3, i_bias, f_bias, n_ctx):
    nb, t, _ = cvm.shape
    nt = t // TILE
    assert _SM_OFF["i_f"] == ML_GATE and _SM_OFF["f_f"] == ML_GATE + ML_ND
    row = lambda v, off: jnp.pad(v.reshape(1, -1), ((0, 0), (off, LANES - off - ML_ND)))
    sel = np.zeros((LANES, ML_ND * LANES), np.float32)
    for r in range(ML_ND):
        sel[ML_GATE + r, r * LANES:(r + 1) * LANES] = 1.0
    const = lambda b, i: (0, 0)
    specs = lambda order: [
        pl.BlockSpec((1, TILE, 2 * ML_QK_WIDTH), lambda b, i: (b, order(i), CV_QK // (2 * ML_QK_WIDTH))),
        pl.BlockSpec((1, TILE, ML_V_WIDTH), lambda b, i: (b, order(i), _P_OFF["m_v"] // ML_V_WIDTH)),
        pl.BlockSpec((1, TILE, LANES), lambda b, i: (b, order(i), 0)),
        pl.BlockSpec((1, LANES), const),
        pl.BlockSpec((1, LANES), const),
        pl.BlockSpec((LANES, ML_ND * LANES), const)]
    args = (cvm, p3, sm3, row(i_bias, ML_GATE), row(f_bias, ML_GATE + ML_ND), jnp.asarray(sel, BF16))
    cn_block = (1, 1, ML_HEADS, ML_QK_DIM, ML_AUG)
    m_block = (1, 1, SUBLANES, LANES)
    scratch = [pltpu.VMEM(cn_block[2:], F32), pltpu.VMEM(m_block[2:], F32)]
    bwd = _tile_order(nt, True, n_ctx // TILE)
    fwd = _tile_order(nt, False)
    st_idx = lambda b, i: (b, bwd(i)) + (0,) * 3
    states = (_ml_states_body, specs(bwd), list(args),
              [jax.ShapeDtypeStruct((nb, nt) + cn_block[2:], BF16),
               jax.ShapeDtypeStruct((nb, nt) + m_block[2:], F32)],
              [pl.BlockSpec(cn_block, st_idx), pl.BlockSpec(m_block, lambda b, i: (b, bwd(i), 0, 0))],
              scratch)
    out = lambda cn_b, m_b: (
        _ml_out_body,
        specs(fwd) + [pl.BlockSpec(cn_block, lambda b, i: (b, i, 0, 0, 0)),
                      pl.BlockSpec(m_block, lambda b, i: (b, i, 0, 0))],
        list(args) + [cn_b, m_b],
        [jax.ShapeDtypeStruct((nb, t, ML_V_WIDTH), F32)],
        [pl.BlockSpec((1, TILE, ML_V_WIDTH), lambda b, i: (b, i, 0))], scratch)
    return states, out


GLA_SUB = 256
GLA_NCH = GLA_SUB // GLA_CHUNK
GLA_COLS = BF16_SUBLANES


def _gla_layout(is_ctx, lat_rows):
    r = np.arange(GLA_SUB)
    if is_ctx:
        return r // GLA_CHUNK, r % GLA_CHUNK
    col = r % SUBLANES
    cpc = GLA_CHUNK // lat_rows
    return col // cpc, (col % cpc) * lat_rows + r // SUBLANES


def _gla_consts(is_ctx, lat_rows, rev):
    ch, pos = _gla_layout(is_ctx, lat_rows)
    same = ch[:, None] == ch[None, :]
    before = (pos[None, :] >= pos[:, None]) if rev else (pos[None, :] <= pos[:, None])
    tri = (same & before).astype(np.float32)
    cmask = np.stack([np.repeat((ch == j)[:, None], LANES, axis=1) for j in range(GLA_NCH)])
    return tri, cmask.astype(np.float32)


def _gla_row(is_ctx, lat_rows, j, p):
    ch, pos = _gla_layout(is_ctx, lat_rows)
    return int(np.nonzero((ch == j) & (pos == p))[0][0])


def _per_chunk_rows(b, is_ctx, lat_rows, p):
    rows = [b[_gla_row(is_ctx, lat_rows, j, p):_gla_row(is_ctx, lat_rows, j, p) + 1, :]
            for j in range(GLA_NCH)]
    w = b.shape[1]
    if is_ctx:
        full = jnp.concatenate([jnp.broadcast_to(r, (GLA_CHUNK, w)) for r in rows], axis=0)
    else:
        rep = SUBLANES // GLA_NCH
        pat = jnp.concatenate([jnp.broadcast_to(r, (rep, w)) for r in rows], axis=0)
        full = jnp.broadcast_to(pat[None], (GLA_SUB // SUBLANES, SUBLANES, w)).reshape(GLA_SUB, w)
    return rows, full


def _gla_sub(q, k, v, araw, aup, abias, tri_b, tri_f, cmask_ref, st_ref, store, *, rev, is_ctx, lat_rows):
    want_out = store is not None
    g = _log_sigmoid(_dot_hp(araw, aup) + abias) * (1.0 / GLA_TAU)
    b = _dot_exact_lhs(tri_b, g, 2)
    yield
    lasts, last = _per_chunk_rows(b, is_ctx, lat_rows, 0 if rev else GLA_CHUNK - 1)
    kl = (k * jnp.exp(last - b)).astype(BF16)
    if want_out:
        _, ref = _per_chunk_rows(b, is_ctx, lat_rows, GLA_CHUNK // 2)
        qs = q * (GLA_K_DIM ** -0.5)
        qe = (qs * jnp.exp(b - ref)).astype(BF16)
        ke = (k * jnp.exp(ref - b)).astype(BF16)
        qb = (qs * jnp.exp(b)).astype(BF16)
        visible = tri_f > 0.0
    yield
    order = range(GLA_NCH - 1, -1, -1) if rev else range(GLA_NCH)
    outs = []
    for h in range(GLA_HEADS):
        ks = slice(GLA_K_DIM * h, GLA_K_DIM * (h + 1))
        vh = v[:, GLA_V_DIM * h:GLA_V_DIM * (h + 1)]
        klm = jnp.concatenate([kl[:, ks] * cmask_ref[j] for j in range(GLA_NCH)], axis=1)
        upd = _dot_tn(vh, klm)
        s = st_ref[h]
        s_in = [None] * GLA_NCH
        for j in order:
            s_in[j] = s.astype(BF16)
            s = s * jnp.exp(lasts[j][:, ks]) + upd[:, GLA_K_DIM * j:GLA_K_DIM * (j + 1)]
        st_ref[h] = s
        if want_out:
            att = jnp.where(visible, _dot_nt(qe[:, ks], ke[:, ks]), 0.0).astype(BF16)
            qbm = jnp.concatenate([qb[:, ks] * cmask_ref[j] for j in range(GLA_NCH)], axis=1)
            outs.append(_dot(att, vh) + _dot_nt(qbm, jnp.concatenate(s_in, axis=1)))
        yield
    if want_out:
        store(jnp.concatenate(outs, axis=1))


def _gla_ctx_body(qc_ref, kc_ref, vc_ref, sc_ref, ql_ref, kl_ref, vl_ref, sl_ref,
                  aup_ref, ab_ref, tcb_ref, tcf_ref, cmc_ref, tlb_ref, tlf_ref, cml_ref,
                  o_ref, st_ref, ctxo_ref, *, rev, ctx_rows, lat_rows, n_cblk, ctx_out):
    st_ref[...] = jnp.zeros_like(st_ref)

    def store(o):
        ctxo_ref[...] = o

    yield from _gla_sub(qc_ref[0].astype(F32), kc_ref[0].astype(F32), vc_ref[0], sc_ref[0],
                        aup_ref[...], ab_ref[...], tcb_ref[...], tcf_ref[...], cmc_ref, st_ref,
                        store if ctx_out else None, rev=rev, is_ctx=True, lat_rows=lat_rows)


def _gla_lat_body(qc_ref, kc_ref, vc_ref, sc_ref, ql_ref, kl_ref, vl_ref, sl_ref,
                  aup_ref, ab_ref, tcb_ref, tcf_ref, cmc_ref, tlb_ref, tlf_ref, cml_ref,
                  o_ref, st_ref, ctxo_ref, *, rev, ctx_rows, lat_rows, n_cblk, ctx_out):
    i = pl.program_id(1)
    r0, r1 = ctx_rows, ctx_rows + lat_rows
    cblk = (n_cblk - i) if rev else (i - 1)
    halves = range(GLA_COLS // SUBLANES)
    for half in (reversed(halves) if rev else halves):
        cs = slice(SUBLANES * half, SUBLANES * (half + 1))
        take = lambda r: r[0, r0:r1].astype(F32)[:, cs, :].reshape(GLA_SUB, r.shape[-1])

        def store(o, cs=cs):
            o_ref[0, r0:r1, cs, :] = o.reshape(lat_rows, SUBLANES, GLA_V_WIDTH)

        yield from _gla_sub(take(ql_ref), take(kl_ref), take(vl_ref).astype(BF16), take(sl_ref),
                            aup_ref[...], ab_ref[...], tlb_ref[...], tlf_ref[...], cml_ref, st_ref,
                            store, rev=rev, is_ctx=False, lat_rows=lat_rows)
    for r in range(ctx_rows):
        if ctx_out:
            start = pl.multiple_of(r * GRID_W + cblk * GLA_COLS, GLA_COLS)
            o_ref[0, r, :, :] = ctxo_ref[pl.ds(start, GLA_COLS), :]
        else:
            o_ref[0, r, :, :] = jnp.zeros((GLA_COLS, GLA_V_WIDTH), F32)


GLA_PHASES = ((lambda i: i == 0, "ctx"), (lambda i: i > 0, "lat"))


def _gla_part(p3, sm3, a_up, a_bias, n_ctx, rev, ctx_out):
    nb, t, ncol = p3.shape
    rows = t // GRID_W
    ctx_rows = n_ctx // GRID_W
    lat_rows = rows - ctx_rows
    n_cblk = GRID_W // GLA_COLS
    p4 = p3.reshape(nb, rows, GRID_W, ncol)
    sm4 = sm3.reshape(nb, rows, GRID_W, LANES)
    a_off = _SM_OFF["a_b"] if rev else _SM_OFF["a_f"]
    aup = jnp.pad(a_up, ((a_off, LANES - a_off - GLA_RANK), (0, 0)))
    cblk = lambda i: jnp.where(i == 0, n_cblk - 1 if rev else 0, (n_cblk - i) if rev else (i - 1))
    ctx = lambda blk: (lambda b, i: (b, 0, blk))
    lat = lambda blk: (lambda b, i: (b, 0, cblk(i), blk))
    const2 = lambda b, i: (0, 0)
    const3 = lambda b, i: (0, 0, 0)
    widths = (GLA_K_WIDTH, GLA_K_WIDTH, GLA_V_WIDTH)
    offs = (_P_OFF["g_q"], _P_OFF["g_k"], _P_OFF["g_v"])
    consts = []
    const_specs = []
    for is_ctx in (True, False):
        tri, cmask = _gla_consts(is_ctx, lat_rows, rev)
        consts += [jnp.asarray(tri, BF16), jnp.asarray(tri, F32), jnp.asarray(cmask, BF16)]
        const_specs += [pl.BlockSpec((GLA_SUB, GLA_SUB), const2), pl.BlockSpec((GLA_SUB, GLA_SUB), const2),
                        pl.BlockSpec((GLA_NCH, GLA_SUB, LANES), const3)]
    out_spec = pl.BlockSpec((1, rows, GLA_COLS, GLA_V_WIDTH), lat(0))
    in_specs = ([pl.BlockSpec((1, n_ctx, w), ctx(o // w)) for w, o in zip(widths, offs)]
                + [pl.BlockSpec((1, n_ctx, LANES), ctx(0))]
                + [pl.BlockSpec((1, rows, GLA_COLS, w), lat(o // w)) for w, o in zip(widths, offs)]
                + [pl.BlockSpec((1, rows, GLA_COLS, LANES), lat(0))]
                + [pl.BlockSpec((LANES, GLA_K_WIDTH), const2), pl.BlockSpec((1, GLA_K_WIDTH), const2)]
                + const_specs)
    args = [p3, p3, p3, sm3, p4, p4, p4, sm4, aup, a_bias.reshape(1, GLA_K_WIDTH)] + consts
    static = dict(rev=rev, ctx_rows=ctx_rows, lat_rows=lat_rows, n_cblk=n_cblk, ctx_out=ctx_out)
    bodies = {"ctx": functools.partial(_gla_ctx_body, **static),
              "lat": functools.partial(_gla_lat_body, **static)}
    return (bodies, in_specs, args,
            [jax.ShapeDtypeStruct((nb, rows, GRID_W, GLA_V_WIDTH), F32)], [out_spec],
            [pltpu.VMEM((GLA_HEADS, GLA_V_DIM, GLA_K_DIM), F32), pltpu.VMEM((n_ctx, GLA_V_WIDTH), F32)])


def _group_rmsnorm(y, groups):
    width = y.shape[-1] // groups
    ones = jnp.ones((width, LANES), BF16)
    out = []
    for g in range(groups):
        yg = y[:, width * g:width * (g + 1)]
        ms = _dot((yg * yg).astype(BF16), ones) * (1.0 / width)
        out.append(yg * jnp.concatenate([lax.rsqrt(ms + EPS)] * (width // LANES), axis=1))
    return jnp.concatenate(out, axis=1)


def _post_kernel(x_ref, y_ref, h_ref, of_ref, ob_ref, z_ref, mo_ref, gg_ref, gs_ref, gm_ref, gl_ref,
                 nws_ref, nwm_ref, nwg_ref, wbs_ref, wbm_ref, wbg_ref, wout_ref, g1_ref,
                 nwf_ref, sc2_ref, sh2_ref, xo_ref, ho_ref):
    y_ssd = (_group_rmsnorm(y_ref[0] * _silu(z_ref[0]), 2) * nws_ref[...]).astype(BF16)
    y_ml = (_group_rmsnorm(h_ref[0], ML_HEADS) * nwm_ref[...]).astype(BF16) * _sigmoid(mo_ref[0])
    y_gla = (_group_rmsnorm(of_ref[0] + ob_ref[0], GLA_HEADS) * nwg_ref[...]).astype(BF16) * _silu(gg_ref[0])
    merged = (_sigmoid(gs_ref[0]) * _dot(y_ssd, wbs_ref[...])
              + _sigmoid(gm_ref[0]) * _dot(y_ml, wbm_ref[...])
              + _sigmoid(gl_ref[0]) * _dot(y_gla, wbg_ref[...]))
    x_new = x_ref[0] + g1_ref[0] * _dot(merged.astype(BF16), wout_ref[...])
    xo_ref[0] = x_new
    ho_ref[0] = (_rms(x_new) * nwf_ref[...] * (1.0 + sc2_ref[0]) + sh2_ref[0]).astype(ho_ref.dtype)


def _post(x, scans, p3, norm_ws, w_bs, w_out, layer, norm_ffn_w, mods, n_ctx, tile0):
    nb, t, d = x.shape
    nt = t // TILE - tile0
    ctx_tiles = n_ctx // TILE
    tok = lambda blk: (lambda b, i: (b, i + tile0, blk))
    out = lambda b, i: (b, i, 0)
    const = lambda b, i: (0, 0)
    tok_spec = lambda blk: pl.BlockSpec((1, TILE, d), tok(blk))
    w_spec = pl.BlockSpec((None, d, d), lambda b, i: (layer, 0, 0), pipeline_mode=pl.Buffered(1))
    vec = pl.BlockSpec((1, d), const)
    mod = lambda k: pl.BlockSpec((1, 1, d), _mod_row(nb, k, ctx_tiles, tile0))
    names = ("s_z", "m_o", "g_g", "gate_ssd", "gate_ml", "gate_gla")
    in_specs = ([tok_spec(0)] * (1 + len(scans)) + [tok_spec(_P_OFF[nm] // d) for nm in names]
                + [vec] * 3 + [w_spec] * 4 + [mod(2), vec, mod(4), mod(3)])
    return pl.pallas_call(
        _post_kernel,
        out_shape=(jax.ShapeDtypeStruct((nb, nt * TILE, d), F32),
                   jax.ShapeDtypeStruct((nb, nt * TILE, d), BF16)),
        grid=(nb, nt),
        in_specs=in_specs,
        out_specs=(pl.BlockSpec((1, TILE, d), out), pl.BlockSpec((1, TILE, d), out)),
        compiler_params=_cparams(("parallel", "parallel")),
        name="post",
    )(x, *scans, *([p3] * 6), *[w.reshape(1, d) for w in norm_ws], *w_bs, w_out, mods,
      norm_ffn_w.reshape(1, d), mods, mods)


def _ffn_in_kernel(a_ref, w_ref, o_ref):
    acc = _dot(a_ref[...], w_ref[...])
    half = acc.shape[1] // 2
    o_ref[...] = (_silu(acc[:, :half]) * acc[:, half:]).astype(o_ref.dtype)


def _ffn_in(h, w_gu, layer, half):
    m, k = h.shape
    n_half = w_gu.shape[-1] // 2
    tm = _row_tile(m)
    return pl.pallas_call(
        _ffn_in_kernel,
        out_shape=jax.ShapeDtypeStruct((m, n_half), BF16),
        grid=(n_half // half, m // tm),
        in_specs=[pl.BlockSpec((tm, k), lambda j, i: (i, 0)),
                  pl.BlockSpec((None, k, 2 * half), lambda j, i: (layer, 0, j))],
        out_specs=pl.BlockSpec((tm, half), lambda j, i: (i, j)),
        compiler_params=_cparams(("parallel", "parallel")),
        name="ffn_in",
    )(h, w_gu)


def _ffn_out_kernel(a_ref, w_ref, x_ref, g_ref, nw_ref, sc_ref, sh_ref, xo_ref, ho_ref):
    x_new = x_ref[0] + g_ref[0] * _dot(a_ref[0], w_ref[...])
    xo_ref[0] = x_new
    ho_ref[0] = (_rms(x_new) * nw_ref[...] * (1.0 + sc_ref[0]) + sh_ref[0]).astype(ho_ref.dtype)


def _ffn_out_last_kernel(a_ref, w_ref, x_ref, g_ref, nw_ref, o_ref):
    x_new = x_ref[0] + g_ref[0] * _dot(a_ref[0], w_ref[...])
    o_ref[0] = _rms(x_new) * nw_ref[...]


def _ffn_out(a, w, layer, x, mods, n_ctx, tile0, next_norm_w, next_mods):
    nb, t, d = x.shape
    k = a.shape[-1]
    ctx_tiles = n_ctx // TILE
    tok = lambda b, i: (b, i, 0)
    mod = lambda k_: pl.BlockSpec((1, 1, d), _mod_row(nb, k_, ctx_tiles, tile0))
    in_specs = [pl.BlockSpec((1, TILE, k), tok),
                pl.BlockSpec((None, k, d), lambda b, i: (layer, 0, 0), pipeline_mode=pl.Buffered(1)),
                pl.BlockSpec((1, TILE, d), tok),
                mod(5),
                pl.BlockSpec((1, d), lambda b, i: (0, 0))]
    args = [a, w, x, mods, next_norm_w.reshape(1, d)]
    if next_mods is None:
        body = _ffn_out_last_kernel
        out_shape = jax.ShapeDtypeStruct((nb, t, d), F32)
        out_specs = pl.BlockSpec((1, TILE, d), tok)
    else:
        body = _ffn_out_kernel
        in_specs += [mod(1), mod(0)]
        args += [next_mods, next_mods]
        out_shape = (jax.ShapeDtypeStruct((nb, t, d), F32), jax.ShapeDtypeStruct((nb, t, d), BF16))
        out_specs = (pl.BlockSpec((1, TILE, d), tok), pl.BlockSpec((1, TILE, d), tok))
    return pl.pallas_call(
        body,
        out_shape=out_shape,
        grid=(nb, t // TILE),
        in_specs=in_specs,
        out_specs=out_specs,
        compiler_params=_cparams(("parallel", "parallel")),
        name="ffn_out",
    )(*args)


def _proj_weights(w_in):
    cols = lambda names: [w_in[..., _IN_OFF[nm]:_IN_OFF[nm] + _IN_W[nm]] for nm in names]
    main = jnp.concatenate(cols(_P_ORDER), axis=-1)
    small = jnp.pad(jnp.concatenate(cols(_SMALL), axis=-1), ((0, 0), (0, 0), (0, LANES - N_SMALL_USED)))
    return main.astype(BF16), small.astype(BF16)


def _ffn_weight(w_ffn_in, half):
    d_ff = w_ffn_in.shape[-1] // 2
    cols = []
    for j in range(d_ff // half):
        cols.append(w_ffn_in[..., j * half:(j + 1) * half])
        cols.append(w_ffn_in[..., d_ff + j * half:d_ff + (j + 1) * half])
    return jnp.concatenate(cols, axis=-1).astype(BF16)


def kernel(x, c, ctx, c_ctx, w_mod, b_mod, norm_mix_w, norm_ffn_w, w_in, ssd_conv_w, ssd_conv_b, ssd_dt_bias, ssd_a_log, ssd_d, ssd_norm_w, ml_conv_w, ml_conv_b, ml_i_bias, ml_f_bias, ml_norm_w, gla_a_up, gla_a_bias, gla_norm_w, w_b_ssd, w_b_ml, w_b_gla, w_out, w_ffn_in, w_ffn_out, final_norm_w):
    nb, n_lat, d = x.shape
    n_ctx = ctx.shape[1]
    t = n_ctx + n_lat
    depth = w_in.shape[0]
    d_ff = w_ffn_out.shape[1]
    assert n_ctx == TILE == GLA_SUB and n_lat % TILE == 0 and n_lat // GRID_W == 32
    ffn_half = d_ff // 2

    c16 = jnp.pad(jnp.concatenate([c, c_ctx[None]], axis=0), ((0, 2 * SUBLANES - nb - 1), (0, 0)))
    mods = [_modulation(c16, w_mod, b_mod, l).reshape(2 * SUBLANES * 6, 1, d) for l in range(depth)]
    xs, h = _norm_mod(ctx, x, norm_mix_w[0], mods[0], 0, 1)
    conv_w = jnp.concatenate([ssd_conv_w, ml_conv_w], axis=-1)
    conv_b = jnp.concatenate([ssd_conv_b, ml_conv_b], axis=-1)
    conv_post = jnp.concatenate([jnp.ones((ssd_conv_w.shape[-1] + ML_QK_WIDTH,), F32),
                                 jnp.full((ML_QK_WIDTH,), ML_QK_DIM ** -0.5, F32)])
    w_main, w_small = _proj_weights(w_in)
    w_gu = _ffn_weight(w_ffn_in, ffn_half)
    w_bs = [w.astype(BF16) for w in (w_b_ssd, w_b_ml, w_b_gla)]
    w_out_b = w_out.astype(BF16)
    w_ffn_out_b = w_ffn_out.astype(BF16)
    for l in range(depth):
        last = l == depth - 1
        h2d = h.reshape(nb * t, d)
        tn = N_PROJ // 5
        conv_tile = _P_OFF["s_x"] // tn
        assert conv_tile * tn == _P_OFF["s_x"] and N_PROJ - _P_OFF["s_x"] == tn == CV_WIDTH
        m_tiles = nb * t // _row_tile(nb * t)
        (pc,), (sm,) = _run_together("proj_first", (1, m_tiles),
                                     _mm_part(h2d, w_main, l, tn, BF16, conv_tile, 1),
                                     _mm_part(h2d, w_small, l, LANES, F32, 0, 1))
        pc3, sm3 = pc.reshape(nb, t, tn), sm.reshape(nb, t, LANES)
        grid = (conv_tile, m_tiles)
        (p,), (cv,) = _run_together(
            "proj_conv", grid, _mm_part(h2d, w_main, l, tn, BF16, 0, conv_tile),
            _conv_part(pc3, conv_w[l], conv_b[l], conv_post, n_ctx, grid))
        p3 = p.reshape(nb, t, conv_tile * tn)
        d_e = jnp.repeat(ssd_d[l], SSD_HEAD_DIM).reshape(1, SSD_WIDTH)
        ssd_states, ssd_out = _ssd_parts(cv, sm3, ssd_dt_bias[l], ssd_a_log[l], d_e, n_ctx)
        ml_states, ml_out = _mlstm_parts(cv, p3, sm3, ml_i_bias[l], ml_f_bias[l], n_ctx)
        grid = (nb, t // TILE)
        ssd_st, ml_st = _run_together("bwd_states", grid, ssd_states, ml_states)
        (y,), (hm,) = _run_together("ssd_mlstm", grid, ssd_out(*ssd_st), ml_out(*ml_st))
        gla = [_gla_part(p3, sm3, gla_a_up[l, k], gla_a_bias[l, k], n_ctx, bool(k), not last) for k in range(2)]
        (og_f,), (og_b,) = _run_together("gla", (nb, GRID_W // GLA_COLS + 1), *gla, phases=GLA_PHASES)
        og_f, og_b = og_f.reshape(nb, t, GLA_V_WIDTH), og_b.reshape(nb, t, GLA_V_WIDTH)
        tile0 = n_ctx // TILE if last else 0
        xs, h2 = _post(xs, (y, hm, og_f, og_b), p3, (ssd_norm_w[l], ml_norm_w[l], gla_norm_w[l]),
                       w_bs, w_out_b, l, norm_ffn_w[l], mods[l], n_ctx, tile0)
        nt = xs.shape[1]
        a = _ffn_in(h2.reshape(nb * nt, d), w_gu, l, ffn_half)
        a = a.reshape(nb, nt, d_ff)
        if last:
            return _ffn_out(a, w_ffn_out_b, l, xs, mods[l], n_ctx, tile0, final_norm_w, None)
        xs, h = _ffn_out(a, w_ffn_out_b, l, xs, mods[l], n_ctx, tile0,
                         norm_mix_w[l + 1], mods[l + 1])
```

```python
import functools

import numpy as np
import jax
import jax.numpy as jnp
from jax import lax
from jax.experimental import pallas as pl
from jax.experimental.pallas import tpu as pltpu

F32 = jnp.float32
BF16 = jnp.bfloat16

EPS = 1e-6
LOG2E = 1.4426950408889634
GRID_W = 64
SSD_HEADS = 16
SSD_HEAD_DIM = 64
SSD_WIDTH = 1024
SSD_STATE = 64
SSD_BC = 128
ML_HEADS = 4
ML_QK_DIM = 128
ML_V_DIM = 256
ML_QK_WIDTH = 512
ML_V_WIDTH = 1024
GLA_HEADS = 4
GLA_K_DIM = 128
GLA_V_DIM = 256
GLA_K_WIDTH = 512
GLA_V_WIDTH = 1024
GLA_RANK = 16
GLA_TAU = 16.0
GLA_CHUNK = 64

LANES = 128
SUBLANES = 8
BF16_SUBLANES = 16
VMEM_LIMIT = 56 * 1024 * 1024

TILE = 256

_IN_NAMES = ("s_x", "s_z", "s_b", "s_c", "dt_f", "dt_b",
             "m_q", "m_k", "m_v", "m_o", "i_f", "i_b", "f_f", "f_b",
             "g_q", "g_k", "g_v", "g_g", "a_f", "a_b",
             "gate_ssd", "gate_ml", "gate_gla")
_IN_WIDTHS = (1024, 1024, 128, 128, 16, 16,
              512, 512, 1024, 1024, 4, 4, 4, 4,
              512, 512, 1024, 1024, 16, 16,
              1024, 1024, 1024)
_IN_OFF = dict(zip(_IN_NAMES, np.concatenate([[0], np.cumsum(_IN_WIDTHS)[:-1]]).tolist()))
_IN_W = dict(zip(_IN_NAMES, _IN_WIDTHS))

_P_ORDER = ("s_z", "m_o", "g_g", "gate_ssd", "gate_ml", "gate_gla", "m_v", "g_v",
            "g_q", "g_k", "s_x", "s_b", "s_c", "m_q", "m_k")
_P_OFF = {}
_o = 0
for _n in _P_ORDER:
    _P_OFF[_n] = _o
    _o += _IN_W[_n]
N_PROJ = _o
_SMALL = ("dt_f", "dt_b", "i_f", "i_b", "f_f", "f_b", "a_f", "a_b")
_SM_OFF = {}
_s = 0
for _n in _SMALL:
    _SM_OFF[_n] = _s
    _s += _IN_W[_n]
N_SMALL_USED = _s


def _cparams(sem):
    return pltpu.CompilerParams(dimension_semantics=sem, vmem_limit_bytes=VMEM_LIMIT)


def _sigmoid(x):
    return 0.5 * jnp.tanh(0.5 * x) + 0.5


def _silu(x):
    h = 0.5 * x
    return h + h * jnp.tanh(h)


def _softplus(x):
    return jnp.maximum(x, 0.0) + jnp.log1p(jnp.exp(-jnp.abs(x)))


def _log_sigmoid(x):
    return jnp.minimum(x, 0.0) - jnp.log(1.0 + jnp.exp(-jnp.abs(x)))


def _split(x, n):
    out = []
    r = x
    for _ in range(n):
        p = r.astype(BF16)
        out.append(p)
        r = r - p.astype(F32)
    return out


def _dot(a, b):
    return jnp.dot(a, b, preferred_element_type=F32)


def _dot_nt(a, b):
    return lax.dot_general(a, b, (((1,), (1,)), ((), ())), preferred_element_type=F32)


def _dot_tn(a, b):
    return lax.dot_general(a, b, (((0,), (0,)), ((), ())), preferred_element_type=F32)


def _dot_exact_lhs(t, x, pieces):
    return sum(_dot(t, p) for p in _split(x, pieces))


def _dot_hp(a, b):
    ah, am = _split(a, 2)
    bh, bm = _split(b, 2)
    return _dot(ah, bh) + _dot(ah, bm) + _dot(am, bh)


def _causal(n, rev):
    t = lax.broadcasted_iota(jnp.int32, (n, n), 0)
    s = lax.broadcasted_iota(jnp.int32, (n, n), 1)
    return (s >= t) if rev else (s <= t)


def _rms(x):
    return x * lax.rsqrt(jnp.mean(x * x, axis=-1, keepdims=True) + EPS)


def _mod_kernel(c_ref, w_ref, b_ref, o_ref):
    o_ref[...] = _dot_hp(_silu(c_ref[...]), w_ref[...]) + b_ref[...]


def _modulation(c16, w_mod, b_mod, layer):
    rows, d = c16.shape
    n = w_mod.shape[-1]
    tn = 1536
    return pl.pallas_call(
        _mod_kernel,
        out_shape=jax.ShapeDtypeStruct((rows, n), F32),
        grid=(n // tn,),
        in_specs=[pl.BlockSpec((rows, d), lambda j: (0, 0)),
                  pl.BlockSpec((None, d, tn), lambda j: (layer, 0, j)),
                  pl.BlockSpec((None, 1, tn), lambda j: (layer, 0, j))],
        out_specs=pl.BlockSpec((rows, tn), lambda j: (0, j)),
        compiler_params=_cparams(("arbitrary",)),
        name="modulation",
    )(c16, w_mod, b_mod.reshape(b_mod.shape[0], 1, n))


def _mod_row(nb, k, ctx_tiles, tile0=0):
    return lambda b, i: (jnp.where(i + tile0 < ctx_tiles, nb, b) * 6 + k, 0, 0)


def _norm_mod_kernel(ctx_ref, x_ref, w_ref, sc_ref, sh_ref, xo_ref, ho_ref):
    x = jnp.where(pl.program_id(1) == 0, ctx_ref[0], x_ref[0])
    xo_ref[0] = x
    ho_ref[0] = (_rms(x) * w_ref[...] * (1.0 + sc_ref[0]) + sh_ref[0]).astype(ho_ref.dtype)


def _norm_mod(ctx, x, w, mods, k_shift, k_scale):
    nb, n_lat, d = x.shape
    n_ctx = ctx.shape[1]
    assert n_ctx == TILE
    t = n_ctx + n_lat
    tok = lambda b, i: (b, i, 0)
    return pl.pallas_call(
        _norm_mod_kernel,
        out_shape=(jax.ShapeDtypeStruct((nb, t, d), F32), jax.ShapeDtypeStruct((nb, t, d), BF16)),
        grid=(nb, t // TILE),
        in_specs=[pl.BlockSpec((1, TILE, d), lambda b, i: (b, 0, 0)),
                  pl.BlockSpec((1, TILE, d), lambda b, i: (b, jnp.maximum(i - 1, 0), 0)),
                  pl.BlockSpec((1, d), lambda b, i: (0, 0)),
                  pl.BlockSpec((1, 1, d), _mod_row(nb, k_scale, n_ctx // TILE)),
                  pl.BlockSpec((1, 1, d), _mod_row(nb, k_shift, n_ctx // TILE))],
        out_specs=(pl.BlockSpec((1, TILE, d), tok), pl.BlockSpec((1, TILE, d), tok)),
        compiler_params=_cparams(("parallel", "arbitrary")),
        name="norm_mod",
    )(ctx, x, w.reshape(1, d), mods, mods)


MM_STAGE = 256


def _mm_body(a_ref, w_ref, o_ref):
    a = a_ref[...]
    width = o_ref.shape[1]
    stage = min(MM_STAGE, width)
    for c0 in range(0, width, stage):
        o_ref[:, c0:c0 + stage] = _dot(a, w_ref[:, c0:c0 + stage]).astype(o_ref.dtype)
        yield


def _row_tile(m):
    return 512 if m % 512 == 0 else TILE


def _mm_part(a, w, layer, tn, out_dtype, tile0, n_tiles):
    m, k = a.shape
    tm = _row_tile(m)
    return (_mm_body,
            [pl.BlockSpec((tm, k), lambda j, i: (i, 0)),
             pl.BlockSpec((None, k, tn), lambda j, i: (layer, 0, j + tile0))],
            [a, w], [jax.ShapeDtypeStruct((m, n_tiles * tn), out_dtype)],
            [pl.BlockSpec((tm, tn), lambda j, i: (i, j))], [])


CONV_K = 5
CONV_ROWS = 128
CONV_X = SSD_WIDTH // LANES
CONV_BC = 2 * SSD_BC // LANES
CONV_QK = 2 * ML_QK_WIDTH // LANES
CV_QK = SSD_WIDTH
CV_BC = SSD_WIDTH + 2 * ML_QK_WIDTH
CV_WIDTH = CV_BC + 2 * SSD_BC


def _conv_body(u_ref, w_ref, b_ref, s_ref, o_ref, pad_ref, *, n_ctx):
    t, c = u_ref.shape[1], u_ref.shape[2]
    half = CONV_K // 2
    zeros = jnp.zeros((SUBLANES, c), F32)
    w = w_ref[...]
    bias = b_ref[...]
    post = s_ref[...]
    for s0, n in ((0, n_ctx), (n_ctx, t - n_ctx)):
        pad_ref[0:SUBLANES, :] = zeros
        pad_ref[SUBLANES:SUBLANES + n, :] = u_ref[0, s0:s0 + n, :].astype(F32)
        pad_ref[SUBLANES + n:2 * SUBLANES + n, :] = zeros
        for r0 in range(0, n, CONV_ROWS):
            acc = bias
            for j in range(CONV_K):
                lo = SUBLANES - half + j + r0
                acc = acc + w[j:j + 1, :] * pad_ref[lo:lo + CONV_ROWS, :]
            o_ref[0, s0 + r0:s0 + r0 + CONV_ROWS, :] = (_silu(acc) * post).astype(o_ref.dtype)
            if (r0 // CONV_ROWS) % 4 == 3:
                yield


def _conv_part(u3, w, b, post_scale, n_ctx, grid):
    nb, t, width = u3.shape
    n_cb = width // LANES
    steps = grid[0] * grid[1]
    rep = steps // (nb * n_cb)
    assert steps == rep * nb * n_cb and n_cb == CONV_X + CONV_BC + CONV_QK
    blk = lambda j, i: (j * grid[1] + i) // rep
    chan = lambda j, i: blk(j, i) % n_cb
    out_chan = lambda c: jnp.where(c < CONV_X, c, jnp.where(c < CONV_X + CONV_BC, c + CONV_QK, c - CONV_BC))
    vec = lambda rows: pl.BlockSpec((rows, LANES), lambda j, i: (0, chan(j, i)))
    return (functools.partial(_conv_body, n_ctx=n_ctx),
            [pl.BlockSpec((1, t, LANES), lambda j, i: (blk(j, i) // n_cb, 0, chan(j, i))),
             vec(CONV_K), vec(1), vec(1)],
            [u3, w, b.reshape(1, width), post_scale.reshape(1, width)],
            [jax.ShapeDtypeStruct((nb, t, width), BF16)],
            [pl.BlockSpec((1, t, LANES), lambda j, i: (blk(j, i) // n_cb, 0, out_chan(chan(j, i))))],
            [pltpu.VMEM((t + 2 * SUBLANES, LANES), F32)])


def _tile_order(n_tiles, rev, ctx_tiles=1):
    if rev:
        return lambda i: jnp.where(i < ctx_tiles, ctx_tiles - 1 - i, n_tiles - 1 + ctx_tiles - i)
    return lambda i: i


SSD_PAIRS = SSD_HEADS // 2
SSD_B_OFF = SSD_HEADS
SSD_TILE = 256


def _ssd_gates(sm_ref, dtb_ref, alog_ref, n, both):
    lane = lax.broadcasted_iota(jnp.int32, (1, LANES), 1)
    dt = _softplus(sm_ref[0] + dtb_ref[...])
    la = dt * jnp.where(lane < 2 * SSD_HEADS, -jnp.exp(alog_ref[...]), 0.0)
    parts = _split(la, 2)
    upp = jnp.where(_causal(n, True), 1.0, 0.0).astype(BF16)
    cum = sum(_dot(upp, p) for p in parts)
    if both:
        low = jnp.where(_causal(n, False), 1.0, 0.0).astype(BF16)
        cum = jnp.where(lane < SSD_HEADS, sum(_dot(low, p) for p in parts), cum)
    return lane, dt, cum


def _expand(a, e):
    return _dot(a.astype(BF16), e)


def _group_dup(v, g, lo):
    other = pltpu.roll(v, SSD_STATE, axis=1)
    return jnp.where(lo, v, other) if g == 0 else jnp.where(lo, other, v)


def _ssd_state_step(st_ref, j, bw, xp, elast, off):
    r = lax.broadcasted_iota(jnp.int32, (LANES, LANES), 0) < SSD_STATE
    c = lax.broadcasted_iota(jnp.int32, (LANES, LANES), 1) < SSD_HEAD_DIM
    dec = jnp.where(r, elast[:, off + 2 * j:off + 2 * j + 1], elast[:, off + 2 * j + 1:off + 2 * j + 2])
    st_ref[j] = jnp.where(r == c, dec * st_ref[j] + _dot_tn(bw.astype(BF16), xp), 0.0)


def _interleave(*bodies):
    live = list(bodies)
    while live:
        for body in list(live):
            if next(body, StopIteration) is StopIteration:
                live.remove(body)


def _ssd_states_body(x_ref, bc_ref, sm_ref, dtb_ref, alog_ref, eb_ref, o_ref, st_ref):
    n = x_ref.shape[1]

    @pl.when(pl.program_id(1) == 0)
    def _init():
        st_ref[...] = jnp.zeros_like(st_ref)

    o_ref[0, 0] = st_ref[...].astype(o_ref.dtype)
    yield
    lane, dt, cum = _ssd_gates(sm_ref, dtb_ref, alog_ref, n, False)
    lo = lane < SSD_HEAD_DIM
    last = cum[0:1, :]
    yield
    wst = _expand(jnp.exp(last - cum) * dt, eb_ref[...])
    elast = jnp.exp(last)
    b128 = bc_ref[0, :, :SSD_BC].astype(F32)
    for j in range(SSD_PAIRS):
        sl = slice(LANES * j, LANES * (j + 1))
        bw = _group_dup(b128, j // (SSD_PAIRS // 2), lo) * wst[:, sl]
        _ssd_state_step(st_ref, j, bw, x_ref[0, :, sl], elast, SSD_B_OFF)
        if j % 2:
            yield


def _ssd_out_body(x_ref, bc_ref, sm_ref, dtb_ref, alog_ref, ef_ref, eb_ref, d_ref, sb_ref, o_ref, st_ref):
    n = x_ref.shape[1]

    @pl.when(pl.program_id(1) == 0)
    def _init():
        st_ref[...] = jnp.zeros_like(st_ref)

    zero_b = jnp.zeros((), BF16)
    lo = lax.broadcasted_iota(jnp.int32, (1, LANES), 1) < SSD_HEAD_DIM
    b128_b = bc_ref[0, :, :SSD_BC]
    c128_b = bc_ref[0, :, SSD_BC:]
    cb_all = [_dot_nt(jnp.where(lo if g == 0 else jnp.logical_not(lo), c128_b, zero_b), b128_b)
              for g in range(2)]
    pair_x = lambda j: x_ref[0, :, LANES * j:LANES * (j + 1)]
    rhs_all = [jnp.concatenate([jnp.where(lo, pair_x(j), zero_b), jnp.where(lo, zero_b, pair_x(j)),
                                st_ref[j].astype(BF16), sb_ref[0, 0, j]], axis=0) for j in range(SSD_PAIRS)]
    yield
    lane, dt, cum = _ssd_gates(sm_ref, dtb_ref, alog_ref, n, True)
    is_f = lane < SSD_HEADS
    ldt = jnp.log(dt)
    dsum = jnp.log(dt + pltpu.roll(dt, LANES - SSD_B_OFF, axis=1))
    rt = (jnp.where(lane < 2 * SSD_HEADS, cum - ldt, pltpu.roll(dsum, 2 * SSD_HEADS, axis=1)) * LOG2E).T
    cum2 = cum * LOG2E
    yield
    last = jnp.where(is_f, cum[n - 1:n, :], cum[0:1, :])
    elast = jnp.exp(last)
    ecum = jnp.exp(cum)
    ecum_f = _expand(ecum, ef_ref[...])
    ecum_b = _expand(ecum, eb_ref[...])
    wst = _expand(jnp.exp(last - cum) * dt, ef_ref[...])
    bc = bc_ref[0].astype(F32)
    b128, c128 = bc[:, :SSD_BC], bc[:, SSD_BC:]
    ti = lax.broadcasted_iota(jnp.int32, (n, n), 0)
    si = lax.broadcasted_iota(jnp.int32, (n, n), 1)
    below, above = si < ti, si > ti
    half = SSD_PAIRS // 2
    yield
    for g in range(2):
        cb = cb_all[g]
        cdup = _group_dup(c128, g, lo)
        for j in range(g * half, (g + 1) * half):
            ms = []
            for h in (2 * j, 2 * j + 1):
                e_f = cum2[:, h:h + 1] - rt[h:h + 1, :]
                e_b = cum2[:, SSD_B_OFF + h:SSD_B_OFF + h + 1] - rt[SSD_B_OFF + h:SSD_B_OFF + h + 1, :]
                e = jnp.where(below, e_f, jnp.where(above, e_b, rt[2 * SSD_HEADS + h:2 * SSD_HEADS + h + 1, :]))
                ms.append((cb * jnp.exp2(e)).astype(BF16))
            sl = slice(LANES * j, LANES * (j + 1))
            cs_f = (cdup * ecum_f[:, sl]).astype(BF16)
            cs_b = (cdup * ecum_b[:, sl]).astype(BF16)
            lhs = jnp.concatenate(ms + [cs_f, cs_b], axis=1)
            o_ref[0, :, sl] = _dot(lhs, rhs_all[j]) + d_ref[:, sl] * pair_x(j).astype(F32)
            yield
    for j in range(SSD_PAIRS):
        sl = slice(LANES * j, LANES * (j + 1))
        _ssd_state_step(st_ref, j, _group_dup(b128, j // half, lo) * wst[:, sl], x_ref[0, :, sl], elast, 0)
        if j % 2:
            yield


def _run_together(name, grid, *parts, phases=None):
    n_in = [len(p[1]) for p in parts]
    n_out = [len(p[3]) for p in parts]
    n_scr = [len(p[5]) for p in parts]

    def kern(*refs):
        ins, outs, scr = refs[:sum(n_in)], refs[sum(n_in):sum(n_in) + sum(n_out)], refs[sum(n_in) + sum(n_out):]

        def run(key):
            bodies = []
            for k, p in enumerate(parts):
                take = lambda seq, counts: seq[sum(counts[:k]):sum(counts[:k + 1])]
                body = p[0][key] if isinstance(p[0], dict) else p[0]
                bodies.append(body(*take(ins, n_in), *take(outs, n_out), *take(scr, n_scr)))
            _interleave(*bodies)

        if phases is None:
            run(None)
        else:
            for pred, key in phases:
                pl.when(pred(pl.program_id(1)))(functools.partial(run, key))

    res = pl.pallas_call(
        kern,
        out_shape=tuple(s for p in parts for s in p[3]),
        grid=grid,
        in_specs=[s for p in parts for s in p[1]],
        out_specs=tuple(s for p in parts for s in p[4]),
        scratch_shapes=[s for p in parts for s in p[5]],
        compiler_params=_cparams(("arbitrary", "arbitrary")),
        name=name,
    )(*[a for p in parts for a in p[2]])
    return [list(res[sum(n_out[:k]):sum(n_out[:k + 1])]) for k in range(len(parts))]


def _ssd_parts(cvs, sm3, dt_bias, a_log, d_e, n_ctx):
    nb, t, _ = cvs.shape
    tile = SSD_TILE
    nt = t // tile
    row = lambda v: jnp.pad(v.reshape(1, -1), ((0, 0), (0, LANES - 2 * SSD_HEADS)))
    const = lambda b, i: (0, 0)
    specs = lambda order: [
        pl.BlockSpec((1, tile, SSD_WIDTH), lambda b, i: (b, order(i), 0)),
        pl.BlockSpec((1, tile, 2 * SSD_BC), lambda b, i: (b, order(i), CV_BC // (2 * SSD_BC))),
        pl.BlockSpec((1, tile, LANES), lambda b, i: (b, order(i), 0)),
        pl.BlockSpec((1, LANES), const),
        pl.BlockSpec((1, LANES), const)]
    st_block = (1, 1, SSD_PAIRS, LANES, LANES)
    args = (cvs, cvs, sm3, row(dt_bias), row(a_log))
    sel = np.zeros((2, LANES, SSD_WIDTH), np.float32)
    for h in range(SSD_HEADS):
        sel[0, h, h * SSD_HEAD_DIM:(h + 1) * SSD_HEAD_DIM] = 1.0
        sel[1, SSD_B_OFF + h, h * SSD_HEAD_DIM:(h + 1) * SSD_HEAD_DIM] = 1.0
    e_f, e_b = jnp.asarray(sel[0], BF16), jnp.asarray(sel[1], BF16)
    e_spec = pl.BlockSpec((LANES, SSD_WIDTH), const)
    bwd = _tile_order(nt, True, n_ctx // tile)
    fwd = _tile_order(nt, False)
    scratch = [pltpu.VMEM(st_block[2:], F32)]
    states = (_ssd_states_body, specs(bwd) + [e_spec], list(args) + [e_b],
              [jax.ShapeDtypeStruct((nb, nt) + st_block[2:], BF16)],
              [pl.BlockSpec(st_block, lambda b, i: (b, bwd(i), 0, 0, 0))], scratch)
    out = lambda states_b: (
        _ssd_out_body,
        specs(fwd) + [e_spec, e_spec, pl.BlockSpec((1, SSD_WIDTH), const),
                      pl.BlockSpec(st_block, lambda b, i: (b, i, 0, 0, 0))],
        list(args) + [e_f, e_b, d_e, states_b],
        [jax.ShapeDtypeStruct((nb, t, SSD_WIDTH), F32)],
        [pl.BlockSpec((1, tile, SSD_WIDTH), lambda b, i: (b, i, 0))], scratch)
    return states, out


ML_GATE = 32
ML_ND = 2 * ML_HEADS
ML_AUG = ML_V_DIM + LANES


def _ml_gates(sm_ref, ib_ref, fb_ref, n, both):
    lane = lax.broadcasted_iota(jnp.int32, (1, LANES), 1)
    valid = (lane >= ML_GATE) & (lane < ML_GATE + ML_ND)
    is_f = lane < ML_GATE + ML_HEADS
    sm = sm_ref[0]
    li = sm + ib_ref[...]
    lf = pltpu.roll(_log_sigmoid(sm + fb_ref[...]), LANES - ML_ND, axis=1)
    parts = _split(jnp.where(valid, lf, 0.0), 2)
    upp = jnp.where(_causal(n, True), 1.0, 0.0).astype(BF16)
    bcum = sum(_dot(upp, p) for p in parts)
    if both:
        low = jnp.where(_causal(n, False), 1.0, 0.0).astype(BF16)
        bcum = jnp.where(is_f, sum(_dot(low, p) for p in parts), bcum)
    return valid, is_f, bcum, jnp.where(valid, li - bcum, 0.0)


def _ml_state_step(cn_ref, h, cn, keep, k, ws_dense, v_aug):
    w3 = jnp.concatenate([ws_dense.astype(BF16)] * (ML_AUG // LANES), axis=1)
    cn_ref[h] = keep * cn + _dot_tn(k, w3 * v_aug)


def _ml_v_aug(v_ref, h, n):
    return jnp.concatenate([v_ref[0, :, ML_V_DIM * h:ML_V_DIM * (h + 1)], jnp.ones((n, LANES), BF16)], axis=1)


def _ml_states_body(qk_ref, v_ref, sm_ref, ib_ref, fb_ref, sel_ref, cn_out, m_out, cn_ref, m_ref):
    n = qk_ref.shape[1]

    @pl.when(pl.program_id(1) == 0)
    def _init():
        cn_ref[...] = jnp.zeros_like(cn_ref)
        m_ref[...] = jnp.zeros_like(m_ref)

    cn_out[0, 0] = cn_ref[...].astype(cn_out.dtype)
    m_out[0, 0] = m_ref[...]
    yield
    valid, is_f, bcum, a = _ml_gates(sm_ref, ib_ref, fb_ref, n, False)
    m_prev = m_ref[0:1, :]
    g_last = jnp.maximum(m_prev, jnp.max(a, axis=0, keepdims=True))
    yield
    ws = _expand(jnp.exp(a - g_last), sel_ref[:, ML_HEADS * LANES:])
    keep = jnp.exp(m_prev - g_last)
    for h in range(ML_HEADS):
        lane_b = ML_GATE + ML_HEADS + h
        k = qk_ref[0, :, ML_QK_WIDTH + ML_QK_DIM * h:ML_QK_WIDTH + ML_QK_DIM * (h + 1)]
        _ml_state_step(cn_ref, h, cn_ref[h], keep[:, lane_b:lane_b + 1], k,
                       ws[:, LANES * h:LANES * (h + 1)], _ml_v_aug(v_ref, h, n))
        yield
    m_ref[...] = jnp.broadcast_to(bcum[0:1, :] + g_last, m_ref.shape)


def _ml_out_body(qk_ref, v_ref, sm_ref, ib_ref, fb_ref, sel_ref, cnb_ref, mb_ref, o_ref, cn_ref, m_ref):
    n = qk_ref.shape[1]

    @pl.when(pl.program_id(1) == 0)
    def _init():
        cn_ref[...] = jnp.zeros_like(cn_ref)
        m_ref[...] = jnp.zeros_like(m_ref)

    head_q = lambda h: qk_ref[0, :, ML_QK_DIM * h:ML_QK_DIM * (h + 1)]
    head_k = lambda h: qk_ref[0, :, ML_QK_WIDTH + ML_QK_DIM * h:ML_QK_WIDTH + ML_QK_DIM * (h + 1)]
    qk_all = [_dot_nt(head_q(h), head_k(h)) for h in range(ML_HEADS)]
    inter_all = [_dot(head_q(h), jnp.concatenate([cn_ref[h].astype(BF16), cnb_ref[0, 0, h]], axis=1))
                 for h in range(ML_HEADS)]
    yield
    valid, is_f, bcum, a = _ml_gates(sm_ref, ib_ref, fb_ref, n, True)
    m_prev = jnp.where(is_f, m_ref[0:1, :], mb_ref[0, 0, 0:1, :])
    a_t = a.T
    pre = suf = a_t[ML_GATE:ML_GATE + ML_ND, :]
    pos = lax.broadcasted_iota(jnp.int32, (ML_ND, n), 1)
    k = 1
    while k < n:
        pre = jnp.maximum(pre, jnp.where(pos >= k, pltpu.roll(pre, k, axis=1), -jnp.inf))
        suf = jnp.maximum(suf, jnp.where(pos < n - k, pltpu.roll(suf, n - k, axis=1), -jnp.inf))
        k *= 2
    run = jnp.where(lax.broadcasted_iota(jnp.int32, (ML_ND, n), 0) < ML_HEADS, pre, suf)
    run = jnp.concatenate([jnp.zeros((ML_GATE, n), F32), run,
                           jnp.zeros((LANES - ML_GATE - ML_ND, n), F32)], axis=0).T
    g = jnp.maximum(m_prev, run)
    m_t = bcum + g
    yield
    floor = jnp.exp(-m_t)
    dense = lambda x, lane_: jnp.broadcast_to(x[:, lane_:lane_ + 1], (n, LANES))
    diag_t = jnp.exp(a - g).T
    g_last = g[n - 1:n, :]
    ws = _expand(jnp.exp(a - g_last), sel_ref[:, :ML_HEADS * LANES])
    keep = jnp.exp(m_prev - g_last)
    ti = lax.broadcasted_iota(jnp.int32, (n, n), 0)
    si = lax.broadcasted_iota(jnp.int32, (n, n), 1)
    not_above, above, on_diag = si <= ti, si > ti, si == ti
    wide = lambda x: jnp.concatenate([x] * (n // LANES), axis=1)
    for h in range(ML_HEADS):
        lanes = (ML_GATE + h, ML_GATE + ML_HEADS + h)
        g_d = [dense(g, ln) for ln in lanes]
        qk, inter = qk_all[h], inter_all[h]
        e = jnp.where(not_above, a_t[lanes[0]:lanes[0] + 1, :] - wide(g_d[0]),
                      a_t[lanes[1]:lanes[1] + 1, :] - wide(g_d[1]))
        p = qk * jnp.exp(e)
        p_f = jnp.where(not_above, p, 0.0).astype(BF16)
        p_b = jnp.where(above, p, jnp.where(on_diag, qk * diag_t[lanes[1]:lanes[1] + 1, :], 0.0)).astype(BF16)
        intra = _dot(jnp.concatenate([p_f, p_b], axis=0), _ml_v_aug(v_ref, h, n))
        out = None
        for d in range(2):
            w_inter = jnp.exp(m_prev[:, lanes[d]:lanes[d] + 1] - g_d[d])
            s = (intra[n * d:n * (d + 1)] + jnp.concatenate([w_inter] * (ML_AUG // LANES), axis=1)
                 * inter[:, ML_AUG * d:ML_AUG * (d + 1)])
            rn = 1.0 / jnp.maximum(jnp.abs(s[:, ML_V_DIM:]), dense(floor, lanes[d]))
            hid = s[:, :ML_V_DIM] * jnp.concatenate([rn] * (ML_V_DIM // LANES), axis=1)
            out = hid if out is None else out + hid
        o_ref[0, :, ML_V_DIM * h:ML_V_DIM * (h + 1)] = out
        yield
    for h in range(ML_HEADS):
        lane_f = ML_GATE + h
        _ml_state_step(cn_ref, h, cn_ref[h], keep[:, lane_f:lane_f + 1], head_k(h),
                       ws[:, LANES * h:LANES * (h + 1)], _ml_v_aug(v_ref, h, n))
        yield
    m_ref[...] = jnp.broadcast_to(m_t[n - 1:n, :], m_ref.shape)


def _mlstm_parts(cvm, p3, sm3, i_bias, f_bias, n_ctx):
    nb, t, _ = cvm.shape
    nt = t // TILE
    assert _SM_OFF["i_f"] == ML_GATE and _SM_OFF["f_f"] == ML_GATE + ML_ND
    row = lambda v, off: jnp.pad(v.reshape(1, -1), ((0, 0), (off, LANES - off - ML_ND)))
    sel = np.zeros((LANES, ML_ND * LANES), np.float32)
    for r in range(ML_ND):
        sel[ML_GATE + r, r * LANES:(r + 1) * LANES] = 1.0
    const = lambda b, i: (0, 0)
    specs = lambda order: [
        pl.BlockSpec((1, TILE, 2 * ML_QK_WIDTH), lambda b, i: (b, order(i), CV_QK // (2 * ML_QK_WIDTH))),
        pl.BlockSpec((1, TILE, ML_V_WIDTH), lambda b, i: (b, order(i), _P_OFF["m_v"] // ML_V_WIDTH)),
        pl.BlockSpec((1, TILE, LANES), lambda b, i: (b, order(i), 0)),
        pl.BlockSpec((1, LANES), const),
        pl.BlockSpec((1, LANES), const),
        pl.BlockSpec((LANES, ML_ND * LANES), const)]
    args = (cvm, p3, sm3, row(i_bias, ML_GATE), row(f_bias, ML_GATE + ML_ND), jnp.asarray(sel, BF16))
    cn_block = (1, 1, ML_HEADS, ML_QK_DIM, ML_AUG)
    m_block = (1, 1, SUBLANES, LANES)
    scratch = [pltpu.VMEM(cn_block[2:], F32), pltpu.VMEM(m_block[2:], F32)]
    bwd = _tile_order(nt, True, n_ctx // TILE)
    fwd = _tile_order(nt, False)
    st_idx = lambda b, i: (b, bwd(i)) + (0,) * 3
    states = (_ml_states_body, specs(bwd), list(args),
              [jax.ShapeDtypeStruct((nb, nt) + cn_block[2:], BF16),
               jax.ShapeDtypeStruct((nb, nt) + m_block[2:], F32)],
              [pl.BlockSpec(cn_block, st_idx), pl.BlockSpec(m_block, lambda b, i: (b, bwd(i), 0, 0))],
              scratch)
    out = lambda cn_b, m_b: (
        _ml_out_body,
        specs(fwd) + [pl.BlockSpec(cn_block, lambda b, i: (b, i, 0, 0, 0)),
                      pl.BlockSpec(m_block, lambda b, i: (b, i, 0, 0))],
        list(args) + [cn_b, m_b],
        [jax.ShapeDtypeStruct((nb, t, ML_V_WIDTH), F32)],
        [pl.BlockSpec((1, TILE, ML_V_WIDTH), lambda b, i: (b, i, 0))], scratch)
    return states, out


GLA_SUB = 256
GLA_NCH = GLA_SUB // GLA_CHUNK
GLA_COLS = BF16_SUBLANES


def _gla_layout(is_ctx, lat_rows):
    r = np.arange(GLA_SUB)
    if is_ctx:
        return r // GLA_CHUNK, r % GLA_CHUNK
    col = r % SUBLANES
    cpc = GLA_CHUNK // lat_rows
    return col // cpc, (col % cpc) * lat_rows + r // SUBLANES


def _gla_consts(is_ctx, lat_rows, rev):
    ch, pos = _gla_layout(is_ctx, lat_rows)
    same = ch[:, None] == ch[None, :]
    before = (pos[None, :] >= pos[:, None]) if rev else (pos[None, :] <= pos[:, None])
    tri = (same & before).astype(np.float32)
    cmask = np.stack([np.repeat((ch == j)[:, None], LANES, axis=1) for j in range(GLA_NCH)])
    return tri, cmask.astype(np.float32)


def _gla_row(is_ctx, lat_rows, j, p):
    ch, pos = _gla_layout(is_ctx, lat_rows)
    return int(np.nonzero((ch == j) & (pos == p))[0][0])


def _per_chunk_rows(b, is_ctx, lat_rows, p):
    rows = [b[_gla_row(is_ctx, lat_rows, j, p):_gla_row(is_ctx, lat_rows, j, p) + 1, :]
            for j in range(GLA_NCH)]
    w = b.shape[1]
    if is_ctx:
        full = jnp.concatenate([jnp.broadcast_to(r, (GLA_CHUNK, w)) for r in rows], axis=0)
    else:
        rep = SUBLANES // GLA_NCH
        pat = jnp.concatenate([jnp.broadcast_to(r, (rep, w)) for r in rows], axis=0)
        full = jnp.broadcast_to(pat[None], (GLA_SUB // SUBLANES, SUBLANES, w)).reshape(GLA_SUB, w)
    return rows, full


def _gla_sub(q, k, v, araw, aup, abias, tri_b, tri_f, cmask_ref, st_ref, store, *, rev, is_ctx, lat_rows):
    want_out = store is not None
    g = _log_sigmoid(_dot_hp(araw, aup) + abias) * (1.0 / GLA_TAU)
    b = _dot_exact_lhs(tri_b, g, 2)
    yield
    lasts, last = _per_chunk_rows(b, is_ctx, lat_rows, 0 if rev else GLA_CHUNK - 1)
    kl = (k * jnp.exp(last - b)).astype(BF16)
    if want_out:
        _, ref = _per_chunk_rows(b, is_ctx, lat_rows, GLA_CHUNK // 2)
        qs = q * (GLA_K_DIM ** -0.5)
        qe = (qs * jnp.exp(b - ref)).astype(BF16)
        ke = (k * jnp.exp(ref - b)).astype(BF16)
        qb = (qs * jnp.exp(b)).astype(BF16)
        visible = tri_f > 0.0
    yield
    order = range(GLA_NCH - 1, -1, -1) if rev else range(GLA_NCH)
    outs = []
    for h in range(GLA_HEADS):
        ks = slice(GLA_K_DIM * h, GLA_K_DIM * (h + 1))
        vh = v[:, GLA_V_DIM * h:GLA_V_DIM * (h + 1)]
        klm = jnp.concatenate([kl[:, ks] * cmask_ref[j] for j in range(GLA_NCH)], axis=1)
        upd = _dot_tn(vh, klm)
        s = st_ref[h]
        s_in = [None] * GLA_NCH
        for j in order:
            s_in[j] = s.astype(BF16)
            s = s * jnp.exp(lasts[j][:, ks]) + upd[:, GLA_K_DIM * j:GLA_K_DIM * (j + 1)]
        st_ref[h] = s
        if want_out:
            att = jnp.where(visible, _dot_nt(qe[:, ks], ke[:, ks]), 0.0).astype(BF16)
            qbm = jnp.concatenate([qb[:, ks] * cmask_ref[j] for j in range(GLA_NCH)], axis=1)
            outs.append(_dot(att, vh) + _dot_nt(qbm, jnp.concatenate(s_in, axis=1)))
        yield
    if want_out:
        store(jnp.concatenate(outs, axis=1))


def _gla_ctx_body(qc_ref, kc_ref, vc_ref, sc_ref, ql_ref, kl_ref, vl_ref, sl_ref,
                  aup_ref, ab_ref, tcb_ref, tcf_ref, cmc_ref, tlb_ref, tlf_ref, cml_ref,
                  o_ref, st_ref, ctxo_ref, *, rev, ctx_rows, lat_rows, n_cblk, ctx_out):
    st_ref[...] = jnp.zeros_like(st_ref)

    def store(o):
        ctxo_ref[...] = o

    yield from _gla_sub(qc_ref[0].astype(F32), kc_ref[0].astype(F32), vc_ref[0], sc_ref[0],
                        aup_ref[...], ab_ref[...], tcb_ref[...], tcf_ref[...], cmc_ref, st_ref,
                        store if ctx_out else None, rev=rev, is_ctx=True, lat_rows=lat_rows)


def _gla_lat_body(qc_ref, kc_ref, vc_ref, sc_ref, ql_ref, kl_ref, vl_ref, sl_ref,
                  aup_ref, ab_ref, tcb_ref, tcf_ref, cmc_ref, tlb_ref, tlf_ref, cml_ref,
                  o_ref, st_ref, ctxo_ref, *, rev, ctx_rows, lat_rows, n_cblk, ctx_out):
    i = pl.program_id(1)
    r0, r1 = ctx_rows, ctx_rows + lat_rows
    cblk = (n_cblk - i) if rev else (i - 1)
    halves = range(GLA_COLS // SUBLANES)
    for half in (reversed(halves) if rev else halves):
        cs = slice(SUBLANES * half, SUBLANES * (half + 1))
        take = lambda r: r[0, r0:r1].astype(F32)[:, cs, :].reshape(GLA_SUB, r.shape[-1])

        def store(o, cs=cs):
            o_ref[0, r0:r1, cs, :] = o.reshape(lat_rows, SUBLANES, GLA_V_WIDTH)

        yield from _gla_sub(take(ql_ref), take(kl_ref), take(vl_ref).astype(BF16), take(sl_ref),
                            aup_ref[...], ab_ref[...], tlb_ref[...], tlf_ref[...], cml_ref, st_ref,
                            store, rev=rev, is_ctx=False, lat_rows=lat_rows)
    for r in range(ctx_rows):
        if ctx_out:
            start = pl.multiple_of(r * GRID_W + cblk * GLA_COLS, GLA_COLS)
            o_ref[0, r, :, :] = ctxo_ref[pl.ds(start, GLA_COLS), :]
        else:
            o_ref[0, r, :, :] = jnp.zeros((GLA_COLS, GLA_V_WIDTH), F32)


GLA_PHASES = ((lambda i: i == 0, "ctx"), (lambda i: i > 0, "lat"))


def _gla_part(p3, sm3, a_up, a_bias, n_ctx, rev, ctx_out):
    nb, t, ncol = p3.shape
    rows = t // GRID_W
    ctx_rows = n_ctx // GRID_W
    lat_rows = rows - ctx_rows
    n_cblk = GRID_W // GLA_COLS
    p4 = p3.reshape(nb, rows, GRID_W, ncol)
    sm4 = sm3.reshape(nb, rows, GRID_W, LANES)
    a_off = _SM_OFF["a_b"] if rev else _SM_OFF["a_f"]
    aup = jnp.pad(a_up, ((a_off, LANES - a_off - GLA_RANK), (0, 0)))
    cblk = lambda i: jnp.where(i == 0, n_cblk - 1 if rev else 0, (n_cblk - i) if rev else (i - 1))
    ctx = lambda blk: (lambda b, i: (b, 0, blk))
    lat = lambda blk: (lambda b, i: (b, 0, cblk(i), blk))
    const2 = lambda b, i: (0, 0)
    const3 = lambda b, i: (0, 0, 0)
    widths = (GLA_K_WIDTH, GLA_K_WIDTH, GLA_V_WIDTH)
    offs = (_P_OFF["g_q"], _P_OFF["g_k"], _P_OFF["g_v"])
    consts = []
    const_specs = []
    for is_ctx in (True, False):
        tri, cmask = _gla_consts(is_ctx, lat_rows, rev)
        consts += [jnp.asarray(tri, BF16), jnp.asarray(tri, F32), jnp.asarray(cmask, BF16)]
        const_specs += [pl.BlockSpec((GLA_SUB, GLA_SUB), const2), pl.BlockSpec((GLA_SUB, GLA_SUB), const2),
                        pl.BlockSpec((GLA_NCH, GLA_SUB, LANES), const3)]
    out_spec = pl.BlockSpec((1, rows, GLA_COLS, GLA_V_WIDTH), lat(0))
    in_specs = ([pl.BlockSpec((1, n_ctx, w), ctx(o // w)) for w, o in zip(widths, offs)]
                + [pl.BlockSpec((1, n_ctx, LANES), ctx(0))]
                + [pl.BlockSpec((1, rows, GLA_COLS, w), lat(o // w)) for w, o in zip(widths, offs)]
                + [pl.BlockSpec((1, rows, GLA_COLS, LANES), lat(0))]
                + [pl.BlockSpec((LANES, GLA_K_WIDTH), const2), pl.BlockSpec((1, GLA_K_WIDTH), const2)]
                + const_specs)
    args = [p3, p3, p3, sm3, p4, p4, p4, sm4, aup, a_bias.reshape(1, GLA_K_WIDTH)] + consts
    static = dict(rev=rev, ctx_rows=ctx_rows, lat_rows=lat_rows, n_cblk=n_cblk, ctx_out=ctx_out)
    bodies = {"ctx": functools.partial(_gla_ctx_body, **static),
              "lat": functools.partial(_gla_lat_body, **static)}
    return (bodies, in_specs, args,
            [jax.ShapeDtypeStruct((nb, rows, GRID_W, GLA_V_WIDTH), F32)], [out_spec],
            [pltpu.VMEM((GLA_HEADS, GLA_V_DIM, GLA_K_DIM), F32), pltpu.VMEM((n_ctx, GLA_V_WIDTH), F32)])


def _group_rmsnorm(y, groups):
    width = y.shape[-1] // groups
    ones = jnp.ones((width, LANES), BF16)
    out = []
    for g in range(groups):
        yg = y[:, width * g:width * (g + 1)]
        ms = _dot((yg * yg).astype(BF16), ones) * (1.0 / width)
        out.append(yg * jnp.concatenate([lax.rsqrt(ms + EPS)] * (width // LANES), axis=1))
    return jnp.concatenate(out, axis=1)


def _cast_weights_once(pairs):
    @pl.when((pl.program_id(0) == 0) & (pl.program_id(1) == 0))
    def _cast():
        for w_ref, s_ref in pairs:
            s_ref[...] = w_ref[...].astype(s_ref.dtype)


def _post_kernel(x_ref, y_ref, h_ref, of_ref, ob_ref, z_ref, mo_ref, gg_ref, gs_ref, gm_ref, gl_ref,
                 nws_ref, nwm_ref, nwg_ref, wbs32_ref, wbm32_ref, wbg32_ref, wout32_ref, g1_ref,
                 nwf_ref, sc2_ref, sh2_ref, xo_ref, ho_ref, wbs_ref, wbm_ref, wbg_ref, wout_ref):
    _cast_weights_once(((wbs32_ref, wbs_ref), (wbm32_ref, wbm_ref), (wbg32_ref, wbg_ref), (wout32_ref, wout_ref)))
    y_ssd = (_group_rmsnorm(y_ref[0] * _silu(z_ref[0]), 2) * nws_ref[...]).astype(BF16)
    y_ml = (_group_rmsnorm(h_ref[0], ML_HEADS) * nwm_ref[...]).astype(BF16) * _sigmoid(mo_ref[0])
    y_gla = (_group_rmsnorm(of_ref[0] + ob_ref[0], GLA_HEADS) * nwg_ref[...]).astype(BF16) * _silu(gg_ref[0])
    merged = (_sigmoid(gs_ref[0]) * _dot(y_ssd, wbs_ref[...])
              + _sigmoid(gm_ref[0]) * _dot(y_ml, wbm_ref[...])
              + _sigmoid(gl_ref[0]) * _dot(y_gla, wbg_ref[...]))
    x_new = x_ref[0] + g1_ref[0] * _dot(merged.astype(BF16), wout_ref[...])
    xo_ref[0] = x_new
    ho_ref[0] = (_rms(x_new) * nwf_ref[...] * (1.0 + sc2_ref[0]) + sh2_ref[0]).astype(ho_ref.dtype)


def _post(x, scans, p3, norm_ws, w_bs, w_out, layer, norm_ffn_w, mods, n_ctx, tile0):
    nb, t, d = x.shape
    nt = t // TILE - tile0
    ctx_tiles = n_ctx // TILE
    tok = lambda blk: (lambda b, i: (b, i + tile0, blk))
    out = lambda b, i: (b, i, 0)
    const = lambda b, i: (0, 0)
    tok_spec = lambda blk: pl.BlockSpec((1, TILE, d), tok(blk))
    w_spec = pl.BlockSpec((None, d, d), lambda b, i: (layer, 0, 0), pipeline_mode=pl.Buffered(1))
    vec = pl.BlockSpec((1, d), const)
    mod = lambda k: pl.BlockSpec((1, 1, d), _mod_row(nb, k, ctx_tiles, tile0))
    names = ("s_z", "m_o", "g_g", "gate_ssd", "gate_ml", "gate_gla")
    in_specs = ([tok_spec(0)] * (1 + len(scans)) + [tok_spec(_P_OFF[nm] // d) for nm in names]
                + [vec] * 3 + [w_spec] * 4 + [mod(2), vec, mod(4), mod(3)])
    return pl.pallas_call(
        _post_kernel,
        out_shape=(jax.ShapeDtypeStruct((nb, nt * TILE, d), F32),
                   jax.ShapeDtypeStruct((nb, nt * TILE, d), BF16)),
        grid=(nb, nt),
        in_specs=in_specs,
        out_specs=(pl.BlockSpec((1, TILE, d), out), pl.BlockSpec((1, TILE, d), out)),
        scratch_shapes=[pltpu.VMEM((d, d), BF16)] * 4,
        compiler_params=_cparams(("arbitrary", "arbitrary")),
        name="post",
    )(x, *scans, *([p3] * 6), *[w.reshape(1, d) for w in norm_ws], *w_bs, w_out, mods,
      norm_ffn_w.reshape(1, d), mods, mods)


def _ffn_in_kernel(a_ref, w_ref, o_ref):
    acc = _dot(a_ref[...], w_ref[...])
    half = acc.shape[1] // 2
    o_ref[...] = (_silu(acc[:, :half]) * acc[:, half:]).astype(o_ref.dtype)


def _ffn_in(h, w_gu, layer, half):
    m, k = h.shape
    n_half = w_gu.shape[-1] // 2
    tm = _row_tile(m)
    return pl.pallas_call(
        _ffn_in_kernel,
        out_shape=jax.ShapeDtypeStruct((m, n_half), BF16),
        grid=(n_half // half, m // tm),
        in_specs=[pl.BlockSpec((tm, k), lambda j, i: (i, 0)),
                  pl.BlockSpec((None, k, 2 * half), lambda j, i: (layer, 0, j))],
        out_specs=pl.BlockSpec((tm, half), lambda j, i: (i, j)),
        compiler_params=_cparams(("parallel", "parallel")),
        name="ffn_in",
    )(h, w_gu)


def _ffn_out_kernel(a_ref, w32_ref, x_ref, g_ref, nw_ref, sc_ref, sh_ref, xo_ref, ho_ref, w_ref):
    _cast_weights_once(((w32_ref, w_ref),))
    x_new = x_ref[0] + g_ref[0] * _dot(a_ref[0], w_ref[...])
    xo_ref[0] = x_new
    ho_ref[0] = (_rms(x_new) * nw_ref[...] * (1.0 + sc_ref[0]) + sh_ref[0]).astype(ho_ref.dtype)


def _ffn_out_last_kernel(a_ref, w32_ref, x_ref, g_ref, nw_ref, o_ref, w_ref):
    _cast_weights_once(((w32_ref, w_ref),))
    x_new = x_ref[0] + g_ref[0] * _dot(a_ref[0], w_ref[...])
    o_ref[0] = _rms(x_new) * nw_ref[...]


def _ffn_out(a, w, layer, x, mods, n_ctx, tile0, next_norm_w, next_mods):
    nb, t, d = x.shape
    k = a.shape[-1]
    ctx_tiles = n_ctx // TILE
    tok = lambda b, i: (b, i, 0)
    mod = lambda k_: pl.BlockSpec((1, 1, d), _mod_row(nb, k_, ctx_tiles, tile0))
    in_specs = [pl.BlockSpec((1, TILE, k), tok),
                pl.BlockSpec((None, k, d), lambda b, i: (layer, 0, 0), pipeline_mode=pl.Buffered(1)),
                pl.BlockSpec((1, TILE, d), tok),
                mod(5),
                pl.BlockSpec((1, d), lambda b, i: (0, 0))]
    args = [a, w, x, mods, next_norm_w.reshape(1, d)]
    if next_mods is None:
        body = _ffn_out_last_kernel
        out_shape = jax.ShapeDtypeStruct((nb, t, d), F32)
        out_specs = pl.BlockSpec((1, TILE, d), tok)
    else:
        body = _ffn_out_kernel
        in_specs += [mod(1), mod(0)]
        args += [next_mods, next_mods]
        out_shape = (jax.ShapeDtypeStruct((nb, t, d), F32), jax.ShapeDtypeStruct((nb, t, d), BF16))
        out_specs = (pl.BlockSpec((1, TILE, d), tok), pl.BlockSpec((1, TILE, d), tok))
    return pl.pallas_call(
        body,
        out_shape=out_shape,
        grid=(nb, t // TILE),
        in_specs=in_specs,
        out_specs=out_specs,
        scratch_shapes=[pltpu.VMEM((k, d), BF16)],
        compiler_params=_cparams(("arbitrary", "arbitrary")),
        name="ffn_out",
    )(*args)


def _proj_weights(w_in):
    cols = lambda names: [w_in[..., _IN_OFF[nm]:_IN_OFF[nm] + _IN_W[nm]] for nm in names]
    main = jnp.concatenate(cols(_P_ORDER), axis=-1)
    small = jnp.pad(jnp.concatenate(cols(_SMALL), axis=-1), ((0, 0), (0, 0), (0, LANES - N_SMALL_USED)))
    return main.astype(BF16), small.astype(BF16)


def _ffn_weight(w_ffn_in, half):
    d_ff = w_ffn_in.shape[-1] // 2
    cols = []
    for j in range(d_ff // half):
        cols.append(w_ffn_in[..., j * half:(j + 1) * half])
        cols.append(w_ffn_in[..., d_ff + j * half:d_ff + (j + 1) * half])
    return jnp.concatenate(cols, axis=-1).astype(BF16)


def kernel(x, c, ctx, c_ctx, w_mod, b_mod, norm_mix_w, norm_ffn_w, w_in, ssd_conv_w, ssd_conv_b, ssd_dt_bias, ssd_a_log, ssd_d, ssd_norm_w, ml_conv_w, ml_conv_b, ml_i_bias, ml_f_bias, ml_norm_w, gla_a_up, gla_a_bias, gla_norm_w, w_b_ssd, w_b_ml, w_b_gla, w_out, w_ffn_in, w_ffn_out, final_norm_w):
    nb, n_lat, d = x.shape
    n_ctx = ctx.shape[1]
    t = n_ctx + n_lat
    depth = w_in.shape[0]
    d_ff = w_ffn_out.shape[1]
    assert n_ctx == TILE == GLA_SUB and n_lat % TILE == 0 and n_lat // GRID_W == 32
    ffn_half = d_ff // 2

    c16 = jnp.pad(jnp.concatenate([c, c_ctx[None]], axis=0), ((0, 2 * SUBLANES - nb - 1), (0, 0)))
    mods = [_modulation(c16, w_mod, b_mod, l).reshape(2 * SUBLANES * 6, 1, d) for l in range(depth)]
    xs, h = _norm_mod(ctx, x, norm_mix_w[0], mods[0], 0, 1)
    conv_w = jnp.concatenate([ssd_conv_w, ml_conv_w], axis=-1)
    conv_b = jnp.concatenate([ssd_conv_b, ml_conv_b], axis=-1)
    conv_post = jnp.concatenate([jnp.ones((ssd_conv_w.shape[-1] + ML_QK_WIDTH,), F32),
                                 jnp.full((ML_QK_WIDTH,), ML_QK_DIM ** -0.5, F32)])
    w_main, w_small = _proj_weights(w_in)
    w_gu = _ffn_weight(w_ffn_in, ffn_half)
    w_bs = (w_b_ssd, w_b_ml, w_b_gla)
    for l in range(depth):
        last = l == depth - 1
        h2d = h.reshape(nb * t, d)
        tn = N_PROJ // 5
        conv_tile = _P_OFF["s_x"] // tn
        assert conv_tile * tn == _P_OFF["s_x"] and N_PROJ - _P_OFF["s_x"] == tn == CV_WIDTH
        m_tiles = nb * t // _row_tile(nb * t)
        (pc,), (sm,) = _run_together("proj_first", (1, m_tiles),
                                     _mm_part(h2d, w_main, l, tn, BF16, conv_tile, 1),
                                     _mm_part(h2d, w_small, l, LANES, F32, 0, 1))
        pc3, sm3 = pc.reshape(nb, t, tn), sm.reshape(nb, t, LANES)
        grid = (conv_tile, m_tiles)
        (p,), (cv,) = _run_together(
            "proj_conv", grid, _mm_part(h2d, w_main, l, tn, BF16, 0, conv_tile),
            _conv_part(pc3, conv_w[l], conv_b[l], conv_post, n_ctx, grid))
        p3 = p.reshape(nb, t, conv_tile * tn)
        d_e = jnp.repeat(ssd_d[l], SSD_HEAD_DIM).reshape(1, SSD_WIDTH)
        ssd_states, ssd_out = _ssd_parts(cv, sm3, ssd_dt_bias[l], ssd_a_log[l], d_e, n_ctx)
        ml_states, ml_out = _mlstm_parts(cv, p3, sm3, ml_i_bias[l], ml_f_bias[l], n_ctx)
        grid = (nb, t // TILE)
        ssd_st, ml_st = _run_together("bwd_states", grid, ssd_states, ml_states)
        (y,), (hm,) = _run_together("ssd_mlstm", grid, ssd_out(*ssd_st), ml_out(*ml_st))
        gla = [_gla_part(p3, sm3, gla_a_up[l, k], gla_a_bias[l, k], n_ctx, bool(k), not last) for k in range(2)]
        (og_f,), (og_b,) = _run_together("gla", (nb, GRID_W // GLA_COLS + 1), *gla, phases=GLA_PHASES)
        og_f, og_b = og_f.reshape(nb, t, GLA_V_WIDTH), og_b.reshape(nb, t, GLA_V_WIDTH)
        tile0 = n_ctx // TILE if last else 0
        xs, h2 = _post(xs, (y, hm, og_f, og_b), p3, (ssd_norm_w[l], ml_norm_w[l], gla_norm_w[l]),
                       w_bs, w_out, l, norm_ffn_w[l], mods[l], n_ctx, tile0)
        nt = xs.shape[1]
        a = _ffn_in(h2.reshape(nb * nt, d), w_gu, l, ffn_half)
        a = a.reshape(nb, nt, d_ff)
        if last:
            return _ffn_out(a, w_ffn_out, l, xs, mods[l], n_ctx, tile0, final_norm_w, None)
        xs, h = _ffn_out(a, w_ffn_out, l, xs, mods[l], n_ctx, tile0,
                         norm_mix_w[l + 1], mods[l + 1])
```

```python
import functools

import numpy as np
import jax
import jax.numpy as jnp
from jax import lax
from jax.experimental import pallas as pl
from jax.experimental.pallas import tpu as pltpu

F32 = jnp.float32
BF16 = jnp.bfloat16

EPS = 1e-6
LOG2E = 1.4426950408889634
GRID_W = 64
SSD_HEADS = 16
SSD_HEAD_DIM = 64
SSD_WIDTH = 1024
SSD_STATE = 64
SSD_BC = 128
ML_HEADS = 4
ML_QK_DIM = 128
ML_V_DIM = 256
ML_QK_WIDTH = 512
ML_V_WIDTH = 1024
GLA_HEADS = 4
GLA_K_DIM = 128
GLA_V_DIM = 256
GLA_K_WIDTH = 512
GLA_V_WIDTH = 1024
GLA_RANK = 16
GLA_TAU = 16.0
GLA_CHUNK = 64

LANES = 128
SUBLANES = 8
BF16_SUBLANES = 16
VMEM_LIMIT = 56 * 1024 * 1024

TILE = 256

_IN_NAMES = ("s_x", "s_z", "s_b", "s_c", "dt_f", "dt_b",
             "m_q", "m_k", "m_v", "m_o", "i_f", "i_b", "f_f", "f_b",
             "g_q", "g_k", "g_v", "g_g", "a_f", "a_b",
             "gate_ssd", "gate_ml", "gate_gla")
_IN_WIDTHS = (1024, 1024, 128, 128, 16, 16,
              512, 512, 1024, 1024, 4, 4, 4, 4,
              512, 512, 1024, 1024, 16, 16,
              1024, 1024, 1024)
_IN_OFF = dict(zip(_IN_NAMES, np.concatenate([[0], np.cumsum(_IN_WIDTHS)[:-1]]).tolist()))
_IN_W = dict(zip(_IN_NAMES, _IN_WIDTHS))

_P_ORDER = ("s_z", "m_o", "g_g", "gate_ssd", "gate_ml", "gate_gla", "m_v", "g_v",
            "g_q", "g_k", "s_x", "s_b", "s_c", "m_q", "m_k")
_P_OFF = {}
_o = 0
for _n in _P_ORDER:
    _P_OFF[_n] = _o
    _o += _IN_W[_n]
N_PROJ = _o
_SMALL = ("dt_f", "dt_b", "i_f", "i_b", "f_f", "f_b", "a_f", "a_b")
_SM_OFF = {}
_s = 0
for _n in _SMALL:
    _SM_OFF[_n] = _s
    _s += _IN_W[_n]
N_SMALL_USED = _s


def _cparams(sem):
    return pltpu.CompilerParams(dimension_semantics=sem, vmem_limit_bytes=VMEM_LIMIT)


def _sigmoid(x):
    return 0.5 * jnp.tanh(0.5 * x) + 0.5


def _silu(x):
    h = 0.5 * x
    return h + h * jnp.tanh(h)


def _softplus(x):
    return jnp.maximum(x, 0.0) + jnp.log1p(jnp.exp(-jnp.abs(x)))


def _log_sigmoid(x):
    return jnp.minimum(x, 0.0) - jnp.log(1.0 + jnp.exp(-jnp.abs(x)))


def _split(x, n):
    out = []
    r = x
    for _ in range(n):
        p = r.astype(BF16)
        out.append(p)
        r = r - p.astype(F32)
    return out


def _dot(a, b):
    return jnp.dot(a, b, preferred_element_type=F32)


def _dot_nt(a, b):
    return lax.dot_general(a, b, (((1,), (1,)), ((), ())), preferred_element_type=F32)


def _dot_tn(a, b):
    return lax.dot_general(a, b, (((0,), (0,)), ((), ())), preferred_element_type=F32)


def _dot_exact_lhs(t, x, pieces):
    return sum(_dot(t, p) for p in _split(x, pieces))


def _dot_hp(a, b):
    ah, am = _split(a, 2)
    bh, bm = _split(b, 2)
    return _dot(ah, bh) + _dot(ah, bm) + _dot(am, bh)


def _causal(n, rev):
    t = lax.broadcasted_iota(jnp.int32, (n, n), 0)
    s = lax.broadcasted_iota(jnp.int32, (n, n), 1)
    return (s >= t) if rev else (s <= t)


def _rms(x):
    return x * lax.rsqrt(jnp.mean(x * x, axis=-1, keepdims=True) + EPS)


def _mod_kernel(c_ref, w_ref, b_ref, o_ref):
    o_ref[...] = _dot_hp(_silu(c_ref[...]), w_ref[...]) + b_ref[...]


def _modulation(c16, w_mod, b_mod, layer):
    rows, d = c16.shape
    n = w_mod.shape[-1]
    tn = 1536
    return pl.pallas_call(
        _mod_kernel,
        out_shape=jax.ShapeDtypeStruct((rows, n), F32),
        grid=(n // tn,),
        in_specs=[pl.BlockSpec((rows, d), lambda j: (0, 0)),
                  pl.BlockSpec((None, d, tn), lambda j: (layer, 0, j)),
                  pl.BlockSpec((None, 1, tn), lambda j: (layer, 0, j))],
        out_specs=pl.BlockSpec((rows, tn), lambda j: (0, j)),
        compiler_params=_cparams(("arbitrary",)),
        name="modulation",
    )(c16, w_mod, b_mod.reshape(b_mod.shape[0], 1, n))


def _mod_row(nb, k, ctx_tiles, tile0=0):
    return lambda b, i: (jnp.where(i + tile0 < ctx_tiles, nb, b) * 6 + k, 0, 0)


def _norm_mod_kernel(ctx_ref, x_ref, w_ref, sc_ref, sh_ref, xo_ref, ho_ref):
    x = jnp.where(pl.program_id(1) == 0, ctx_ref[0], x_ref[0])
    xo_ref[0] = x
    ho_ref[0] = (_rms(x) * w_ref[...] * (1.0 + sc_ref[0]) + sh_ref[0]).astype(ho_ref.dtype)


def _norm_mod(ctx, x, w, mods, k_shift, k_scale):
    nb, n_lat, d = x.shape
    n_ctx = ctx.shape[1]
    assert n_ctx == TILE
    t = n_ctx + n_lat
    tok = lambda b, i: (b, i, 0)
    return pl.pallas_call(
        _norm_mod_kernel,
        out_shape=(jax.ShapeDtypeStruct((nb, t, d), F32), jax.ShapeDtypeStruct((nb, t, d), BF16)),
        grid=(nb, t // TILE),
        in_specs=[pl.BlockSpec((1, TILE, d), lambda b, i: (b, 0, 0)),
                  pl.BlockSpec((1, TILE, d), lambda b, i: (b, jnp.maximum(i - 1, 0), 0)),
                  pl.BlockSpec((1, d), lambda b, i: (0, 0)),
                  pl.BlockSpec((1, 1, d), _mod_row(nb, k_scale, n_ctx // TILE)),
                  pl.BlockSpec((1, 1, d), _mod_row(nb, k_shift, n_ctx // TILE))],
        out_specs=(pl.BlockSpec((1, TILE, d), tok), pl.BlockSpec((1, TILE, d), tok)),
        compiler_params=_cparams(("parallel", "arbitrary")),
        name="norm_mod",
    )(ctx, x, w.reshape(1, d), mods, mods)


MM_STAGE = 256


def _mm_body(a_ref, w_ref, o_ref):
    a = a_ref[...]
    width = o_ref.shape[1]
    stage = min(MM_STAGE, width)
    for c0 in range(0, width, stage):
        o_ref[:, c0:c0 + stage] = _dot(a, w_ref[:, c0:c0 + stage]).astype(o_ref.dtype)
        yield


def _row_tile(m):
    return 512 if m % 512 == 0 else TILE


def _mm_part(a, w, layer, tn, out_dtype, tile0, n_tiles):
    m, k = a.shape
    tm = _row_tile(m)
    return (_mm_body,
            [pl.BlockSpec((tm, k), lambda j, i: (i, 0)),
             pl.BlockSpec((None, k, tn), lambda j, i: (layer, 0, j + tile0))],
            [a, w], [jax.ShapeDtypeStruct((m, n_tiles * tn), out_dtype)],
            [pl.BlockSpec((tm, tn), lambda j, i: (i, j))], [])


CONV_K = 5
CONV_ROWS = 128
CONV_X = SSD_WIDTH // LANES
CONV_BC = 2 * SSD_BC // LANES
CONV_QK = 2 * ML_QK_WIDTH // LANES
CV_QK = SSD_WIDTH
CV_BC = SSD_WIDTH + 2 * ML_QK_WIDTH
CV_WIDTH = CV_BC + 2 * SSD_BC


def _conv_body(u_ref, w_ref, b_ref, s_ref, o_ref, pad_ref, *, n_ctx):
    t, c = u_ref.shape[1], u_ref.shape[2]
    half = CONV_K // 2
    zeros = jnp.zeros((SUBLANES, c), F32)
    w = w_ref[...]
    bias = b_ref[...]
    post = s_ref[...]
    for s0, n in ((0, n_ctx), (n_ctx, t - n_ctx)):
        pad_ref[0:SUBLANES, :] = zeros
        pad_ref[SUBLANES:SUBLANES + n, :] = u_ref[0, s0:s0 + n, :].astype(F32)
        pad_ref[SUBLANES + n:2 * SUBLANES + n, :] = zeros
        for r0 in range(0, n, CONV_ROWS):
            acc = bias
            for j in range(CONV_K):
                lo = SUBLANES - half + j + r0
                acc = acc + w[j:j + 1, :] * pad_ref[lo:lo + CONV_ROWS, :]
            o_ref[0, s0 + r0:s0 + r0 + CONV_ROWS, :] = (_silu(acc) * post).astype(o_ref.dtype)
            if (r0 // CONV_ROWS) % 4 == 3:
                yield


def _conv_part(u3, w, b, post_scale, n_ctx, grid):
    nb, t, width = u3.shape
    n_cb = width // LANES
    steps = grid[0] * grid[1]
    rep = steps // (nb * n_cb)
    assert steps == rep * nb * n_cb and n_cb == CONV_X + CONV_BC + CONV_QK
    blk = lambda j, i: (j * grid[1] + i) // rep
    chan = lambda j, i: blk(j, i) % n_cb
    out_chan = lambda c: jnp.where(c < CONV_X, c, jnp.where(c < CONV_X + CONV_BC, c + CONV_QK, c - CONV_BC))
    vec = lambda rows: pl.BlockSpec((rows, LANES), lambda j, i: (0, chan(j, i)))
    return (functools.partial(_conv_body, n_ctx=n_ctx),
            [pl.BlockSpec((1, t, LANES), lambda j, i: (blk(j, i) // n_cb, 0, chan(j, i))),
             vec(CONV_K), vec(1), vec(1)],
            [u3, w, b.reshape(1, width), post_scale.reshape(1, width)],
            [jax.ShapeDtypeStruct((nb, t, width), BF16)],
            [pl.BlockSpec((1, t, LANES), lambda j, i: (blk(j, i) // n_cb, 0, out_chan(chan(j, i))))],
            [pltpu.VMEM((t + 2 * SUBLANES, LANES), F32)])


def _tile_order(n_tiles, rev, ctx_tiles=1):
    if rev:
        return lambda i: jnp.where(i < ctx_tiles, ctx_tiles - 1 - i, n_tiles - 1 + ctx_tiles - i)
    return lambda i: i


SSD_PAIRS = SSD_HEADS // 2
SSD_B_OFF = SSD_HEADS
SSD_TILE = 256


def _ssd_gates(sm_ref, dtb_ref, alog_ref, n, both):
    lane = lax.broadcasted_iota(jnp.int32, (1, LANES), 1)
    dt = _softplus(sm_ref[0] + dtb_ref[...])
    la = dt * jnp.where(lane < 2 * SSD_HEADS, -jnp.exp(alog_ref[...]), 0.0)
    parts = _split(la, 2)
    upp = jnp.where(_causal(n, True), 1.0, 0.0).astype(BF16)
    cum = sum(_dot(upp, p) for p in parts)
    if both:
        low = jnp.where(_causal(n, False), 1.0, 0.0).astype(BF16)
        cum = jnp.where(lane < SSD_HEADS, sum(_dot(low, p) for p in parts), cum)
    return lane, dt, cum


def _expand(a, e):
    return _dot(a.astype(BF16), e)


def _group_dup(v, g, lo):
    other = pltpu.roll(v, SSD_STATE, axis=1)
    return jnp.where(lo, v, other) if g == 0 else jnp.where(lo, other, v)


def _ssd_state_step(st_ref, j, bw, xp, elast, off):
    r = lax.broadcasted_iota(jnp.int32, (LANES, LANES), 0) < SSD_STATE
    c = lax.broadcasted_iota(jnp.int32, (LANES, LANES), 1) < SSD_HEAD_DIM
    dec = jnp.where(r, elast[:, off + 2 * j:off + 2 * j + 1], elast[:, off + 2 * j + 1:off + 2 * j + 2])
    st_ref[j] = jnp.where(r == c, dec * st_ref[j] + _dot_tn(bw.astype(BF16), xp), 0.0)


def _interleave(*bodies):
    live = list(bodies)
    while live:
        for body in list(live):
            if next(body, StopIteration) is StopIteration:
                live.remove(body)


def _ssd_states_body(x_ref, bc_ref, sm_ref, dtb_ref, alog_ref, eb_ref, o_ref, st_ref):
    n = x_ref.shape[1]

    @pl.when(pl.program_id(1) == 0)
    def _init():
        st_ref[...] = jnp.zeros_like(st_ref)

    o_ref[0, 0] = st_ref[...].astype(o_ref.dtype)
    yield
    lane, dt, cum = _ssd_gates(sm_ref, dtb_ref, alog_ref, n, False)
    lo = lane < SSD_HEAD_DIM
    last = cum[0:1, :]
    yield
    wst = _expand(jnp.exp(last - cum) * dt, eb_ref[...])
    elast = jnp.exp(last)
    b128 = bc_ref[0, :, :SSD_BC].astype(F32)
    for j in range(SSD_PAIRS):
        sl = slice(LANES * j, LANES * (j + 1))
        bw = _group_dup(b128, j // (SSD_PAIRS // 2), lo) * wst[:, sl]
        _ssd_state_step(st_ref, j, bw, x_ref[0, :, sl], elast, SSD_B_OFF)
        if j % 2:
            yield


def _ssd_out_body(x_ref, bc_ref, sm_ref, dtb_ref, alog_ref, ef_ref, eb_ref, d_ref, sb_ref, o_ref, st_ref):
    n = x_ref.shape[1]

    @pl.when(pl.program_id(1) == 0)
    def _init():
        st_ref[...] = jnp.zeros_like(st_ref)

    zero_b = jnp.zeros((), BF16)
    lo = lax.broadcasted_iota(jnp.int32, (1, LANES), 1) < SSD_HEAD_DIM
    b128_b = bc_ref[0, :, :SSD_BC]
    c128_b = bc_ref[0, :, SSD_BC:]
    cb_all = [_dot_nt(jnp.where(lo if g == 0 else jnp.logical_not(lo), c128_b, zero_b), b128_b)
              for g in range(2)]
    pair_x = lambda j: x_ref[0, :, LANES * j:LANES * (j + 1)]
    rhs_all = [jnp.concatenate([jnp.where(lo, pair_x(j), zero_b), jnp.where(lo, zero_b, pair_x(j)),
                                st_ref[j].astype(BF16), sb_ref[0, 0, j]], axis=0) for j in range(SSD_PAIRS)]
    yield
    lane, dt, cum = _ssd_gates(sm_ref, dtb_ref, alog_ref, n, True)
    is_f = lane < SSD_HEADS
    ldt = jnp.log(dt)
    dsum = jnp.log(dt + pltpu.roll(dt, LANES - SSD_B_OFF, axis=1))
    rt = (jnp.where(lane < 2 * SSD_HEADS, cum - ldt, pltpu.roll(dsum, 2 * SSD_HEADS, axis=1)) * LOG2E).T
    cum2 = cum * LOG2E
    yield
    last = jnp.where(is_f, cum[n - 1:n, :], cum[0:1, :])
    elast = jnp.exp(last)
    ecum = jnp.exp(cum)
    ecum_f = _expand(ecum, ef_ref[...])
    ecum_b = _expand(ecum, eb_ref[...])
    wst = _expand(jnp.exp(last - cum) * dt, ef_ref[...])
    bc = bc_ref[0].astype(F32)
    b128, c128 = bc[:, :SSD_BC], bc[:, SSD_BC:]
    ti = lax.broadcasted_iota(jnp.int32, (n, n), 0)
    si = lax.broadcasted_iota(jnp.int32, (n, n), 1)
    below, above = si < ti, si > ti
    half = SSD_PAIRS // 2
    yield
    for g in range(2):
        cb = cb_all[g]
        cdup = _group_dup(c128, g, lo)
        for j in range(g * half, (g + 1) * half):
            ms = []
            for h in (2 * j, 2 * j + 1):
                e_f = cum2[:, h:h + 1] - rt[h:h + 1, :]
                e_b = cum2[:, SSD_B_OFF + h:SSD_B_OFF + h + 1] - rt[SSD_B_OFF + h:SSD_B_OFF + h + 1, :]
                e = jnp.where(below, e_f, jnp.where(above, e_b, rt[2 * SSD_HEADS + h:2 * SSD_HEADS + h + 1, :]))
                ms.append((cb * jnp.exp2(e)).astype(BF16))
            sl = slice(LANES * j, LANES * (j + 1))
            cs_f = (cdup * ecum_f[:, sl]).astype(BF16)
            cs_b = (cdup * ecum_b[:, sl]).astype(BF16)
            lhs = jnp.concatenate(ms + [cs_f, cs_b], axis=1)
            o_ref[0, :, sl] = _dot(lhs, rhs_all[j]) + d_ref[:, sl] * pair_x(j).astype(F32)
            yield
    for j in range(SSD_PAIRS):
        sl = slice(LANES * j, LANES * (j + 1))
        _ssd_state_step(st_ref, j, _group_dup(b128, j // half, lo) * wst[:, sl], x_ref[0, :, sl], elast, 0)
        if j % 2:
            yield


def _run_together(name, grid, *parts, phases=None):
    n_in = [len(p[1]) for p in parts]
    n_out = [len(p[3]) for p in parts]
    n_scr = [len(p[5]) for p in parts]

    def kern(*refs):
        ins, outs, scr = refs[:sum(n_in)], refs[sum(n_in):sum(n_in) + sum(n_out)], refs[sum(n_in) + sum(n_out):]

        def run(key):
            bodies = []
            for k, p in enumerate(parts):
                take = lambda seq, counts: seq[sum(counts[:k]):sum(counts[:k + 1])]
                body = p[0][key] if isinstance(p[0], dict) else p[0]
                bodies.append(body(*take(ins, n_in), *take(outs, n_out), *take(scr, n_scr)))
            _interleave(*bodies)

        if phases is None:
            run(None)
        else:
            for pred, key in phases:
                pl.when(pred(pl.program_id(1)))(functools.partial(run, key))

    res = pl.pallas_call(
        kern,
        out_shape=tuple(s for p in parts for s in p[3]),
        grid=grid,
        in_specs=[s for p in parts for s in p[1]],
        out_specs=tuple(s for p in parts for s in p[4]),
        scratch_shapes=[s for p in parts for s in p[5]],
        compiler_params=_cparams(("arbitrary", "arbitrary")),
        name=name,
    )(*[a for p in parts for a in p[2]])
    return [list(res[sum(n_out[:k]):sum(n_out[:k + 1])]) for k in range(len(parts))]


def _ssd_parts(cvs, sm3, dt_bias, a_log, d_e, n_ctx):
    nb, t, _ = cvs.shape
    tile = SSD_TILE
    nt = t // tile
    row = lambda v: jnp.pad(v.reshape(1, -1), ((0, 0), (0, LANES - 2 * SSD_HEADS)))
    const = lambda b, i: (0, 0)
    specs = lambda order: [
        pl.BlockSpec((1, tile, SSD_WIDTH), lambda b, i: (b, order(i), 0)),
        pl.BlockSpec((1, tile, 2 * SSD_BC), lambda b, i: (b, order(i), CV_BC // (2 * SSD_BC))),
        pl.BlockSpec((1, tile, LANES), lambda b, i: (b, order(i), 0)),
        pl.BlockSpec((1, LANES), const),
        pl.BlockSpec((1, LANES), const)]
    st_block = (1, 1, SSD_PAIRS, LANES, LANES)
    args = (cvs, cvs, sm3, row(dt_bias), row(a_log))
    sel = np.zeros((2, LANES, SSD_WIDTH), np.float32)
    for h in range(SSD_HEADS):
        sel[0, h, h * SSD_HEAD_DIM:(h + 1) * SSD_HEAD_DIM] = 1.0
        sel[1, SSD_B_OFF + h, h * SSD_HEAD_DIM:(h + 1) * SSD_HEAD_DIM] = 1.0
    e_f, e_b = jnp.asarray(sel[0], BF16), jnp.asarray(sel[1], BF16)
    e_spec = pl.BlockSpec((LANES, SSD_WIDTH), const)
    bwd = _tile_order(nt, True, n_ctx // tile)
    fwd = _tile_order(nt, False)
    scratch = [pltpu.VMEM(st_block[2:], F32)]
    states = (_ssd_states_body, specs(bwd) + [e_spec], list(args) + [e_b],
              [jax.ShapeDtypeStruct((nb, nt) + st_block[2:], BF16)],
              [pl.BlockSpec(st_block, lambda b, i: (b, bwd(i), 0, 0, 0))], scratch)
    out = lambda states_b: (
        _ssd_out_body,
        specs(fwd) + [e_spec, e_spec, pl.BlockSpec((1, SSD_WIDTH), const),
                      pl.BlockSpec(st_block, lambda b, i: (b, i, 0, 0, 0))],
        list(args) + [e_f, e_b, d_e, states_b],
        [jax.ShapeDtypeStruct((nb, t, SSD_WIDTH), F32)],
        [pl.BlockSpec((1, tile, SSD_WIDTH), lambda b, i: (b, i, 0))], scratch)
    return states, out


ML_GATE = 32
ML_ND = 2 * ML_HEADS
ML_AUG = ML_V_DIM + LANES


def _ml_gates(sm_ref, ib_ref, fb_ref, n, both):
    lane = lax.broadcasted_iota(jnp.int32, (1, LANES), 1)
    valid = (lane >= ML_GATE) & (lane < ML_GATE + ML_ND)
    is_f = lane < ML_GATE + ML_HEADS
    sm = sm_ref[0]
    li = sm + ib_ref[...]
    lf = pltpu.roll(_log_sigmoid(sm + fb_ref[...]), LANES - ML_ND, axis=1)
    parts = _split(jnp.where(valid, lf, 0.0), 2)
    upp = jnp.where(_causal(n, True), 1.0, 0.0).astype(BF16)
    bcum = sum(_dot(upp, p) for p in parts)
    if both:
        low = jnp.where(_causal(n, False), 1.0, 0.0).astype(BF16)
        bcum = jnp.where(is_f, sum(_dot(low, p) for p in parts), bcum)
    return valid, is_f, bcum, jnp.where(valid, li - bcum, 0.0)


def _ml_state_step(cn_ref, h, cn, keep, k, ws_dense, v_aug):
    w3 = jnp.concatenate([ws_dense.astype(BF16)] * (ML_AUG // LANES), axis=1)
    cn_ref[h] = keep * cn + _dot_tn(k, w3 * v_aug)


def _ml_v_aug(v_ref, h, n):
    return jnp.concatenate([v_ref[0, :, ML_V_DIM * h:ML_V_DIM * (h + 1)], jnp.ones((n, LANES), BF16)], axis=1)


def _ml_states_body(qk_ref, v_ref, sm_ref, ib_ref, fb_ref, sel_ref, cn_out, m_out, cn_ref, m_ref):
    n = qk_ref.shape[1]

    @pl.when(pl.program_id(1) == 0)
    def _init():
        cn_ref[...] = jnp.zeros_like(cn_ref)
        m_ref[...] = jnp.zeros_like(m_ref)

    cn_out[0, 0] = cn_ref[...].astype(cn_out.dtype)
    m_out[0, 0] = m_ref[...]
    yield
    valid, is_f, bcum, a = _ml_gates(sm_ref, ib_ref, fb_ref, n, False)
    m_prev = m_ref[0:1, :]
    g_last = jnp.maximum(m_prev, jnp.max(a, axis=0, keepdims=True))
    yield
    ws = _expand(jnp.exp(a - g_last), sel_ref[:, ML_HEADS * LANES:])
    keep = jnp.exp(m_prev - g_last)
    for h in range(ML_HEADS):
        lane_b = ML_GATE + ML_HEADS + h
        k = qk_ref[0, :, ML_QK_WIDTH + ML_QK_DIM * h:ML_QK_WIDTH + ML_QK_DIM * (h + 1)]
        _ml_state_step(cn_ref, h, cn_ref[h], keep[:, lane_b:lane_b + 1], k,
                       ws[:, LANES * h:LANES * (h + 1)], _ml_v_aug(v_ref, h, n))
        yield
    m_ref[...] = jnp.broadcast_to(bcum[0:1, :] + g_last, m_ref.shape)


def _ml_out_body(qk_ref, v_ref, sm_ref, ib_ref, fb_ref, sel_ref, cnb_ref, mb_ref, o_ref, cn_ref, m_ref):
    n = qk_ref.shape[1]

    @pl.when(pl.program_id(1) == 0)
    def _init():
        cn_ref[...] = jnp.zeros_like(cn_ref)
        m_ref[...] = jnp.zeros_like(m_ref)

    head_q = lambda h: qk_ref[0, :, ML_QK_DIM * h:ML_QK_DIM * (h + 1)]
    head_k = lambda h: qk_ref[0, :, ML_QK_WIDTH + ML_QK_DIM * h:ML_QK_WIDTH + ML_QK_DIM * (h + 1)]
    qk_all = [_dot_nt(head_q(h), head_k(h)) for h in range(ML_HEADS)]
    inter_all = [_dot(head_q(h), jnp.concatenate([cn_ref[h].astype(BF16), cnb_ref[0, 0, h]], axis=1))
                 for h in range(ML_HEADS)]
    yield
    valid, is_f, bcum, a = _ml_gates(sm_ref, ib_ref, fb_ref, n, True)
    m_prev = jnp.where(is_f, m_ref[0:1, :], mb_ref[0, 0, 0:1, :])
    a_t = a.T
    pre = suf = a_t[ML_GATE:ML_GATE + ML_ND, :]
    pos = lax.broadcasted_iota(jnp.int32, (ML_ND, n), 1)
    k = 1
    while k < n:
        pre = jnp.maximum(pre, jnp.where(pos >= k, pltpu.roll(pre, k, axis=1), -jnp.inf))
        suf = jnp.maximum(suf, jnp.where(pos < n - k, pltpu.roll(suf, n - k, axis=1), -jnp.inf))
        k *= 2
    run = jnp.where(lax.broadcasted_iota(jnp.int32, (ML_ND, n), 0) < ML_HEADS, pre, suf)
    run = jnp.concatenate([jnp.zeros((ML_GATE, n), F32), run,
                           jnp.zeros((LANES - ML_GATE - ML_ND, n), F32)], axis=0).T
    g = jnp.maximum(m_prev, run)
    m_t = bcum + g
    yield
    floor = jnp.exp(-m_t)
    dense = lambda x, lane_: jnp.broadcast_to(x[:, lane_:lane_ + 1], (n, LANES))
    diag_t = jnp.exp(a - g).T
    g_last = g[n - 1:n, :]
    ws = _expand(jnp.exp(a - g_last), sel_ref[:, :ML_HEADS * LANES])
    keep = jnp.exp(m_prev - g_last)
    ti = lax.broadcasted_iota(jnp.int32, (n, n), 0)
    si = lax.broadcasted_iota(jnp.int32, (n, n), 1)
    not_above, above, on_diag = si <= ti, si > ti, si == ti
    wide = lambda x: jnp.concatenate([x] * (n // LANES), axis=1)
    for h in range(ML_HEADS):
        lanes = (ML_GATE + h, ML_GATE + ML_HEADS + h)
        g_d = [dense(g, ln) for ln in lanes]
        qk, inter = qk_all[h], inter_all[h]
        e = jnp.where(not_above, a_t[lanes[0]:lanes[0] + 1, :] - wide(g_d[0]),
                      a_t[lanes[1]:lanes[1] + 1, :] - wide(g_d[1]))
        p = qk * jnp.exp(e)
        p_f = jnp.where(not_above, p, 0.0).astype(BF16)
        p_b = jnp.where(above, p, jnp.where(on_diag, qk * diag_t[lanes[1]:lanes[1] + 1, :], 0.0)).astype(BF16)
        intra = _dot(jnp.concatenate([p_f, p_b], axis=0), _ml_v_aug(v_ref, h, n))
        out = None
        for d in range(2):
            w_inter = jnp.exp(m_prev[:, lanes[d]:lanes[d] + 1] - g_d[d])
            s = (intra[n * d:n * (d + 1)] + jnp.concatenate([w_inter] * (ML_AUG // LANES), axis=1)
                 * inter[:, ML_AUG * d:ML_AUG * (d + 1)])
            rn = 1.0 / jnp.maximum(jnp.abs(s[:, ML_V_DIM:]), dense(floor, lanes[d]))
            hid = s[:, :ML_V_DIM] * jnp.concatenate([rn] * (ML_V_DIM // LANES), axis=1)
            out = hid if out is None else out + hid
        o_ref[0, :, ML_V_DIM * h:ML_V_DIM * (h + 1)] = out
        yield
    for h in range(ML_HEADS):
        lane_f = ML_GATE + h
        _ml_state_step(cn_ref, h, cn_ref[h], keep[:, lane_f:lane_f + 1], head_k(h),
                       ws[:, LANES * h:LANES * (h + 1)], _ml_v_aug(v_ref, h, n))
        yield
    m_ref[...] = jnp.broadcast_to(m_t[n - 1:n, :], m_ref.shape)


def _mlstm_parts(cvm, p3, sm3, i_bias, f_bias, n_ctx):
    nb, t, _ = cvm.shape
    nt = t // TILE
    assert _SM_OFF["i_f"] == ML_GATE and _SM_OFF["f_f"] == ML_GATE + ML_ND
    row = lambda v, off: jnp.pad(v.reshape(1, -1), ((0, 0), (off, LANES - off - ML_ND)))
    sel = np.zeros((LANES, ML_ND * LANES), np.float32)
    for r in range(ML_ND):
        sel[ML_GATE + r, r * LANES:(r + 1) * LANES] = 1.0
    const = lambda b, i: (0, 0)
    specs = lambda order: [
        pl.BlockSpec((1, TILE, 2 * ML_QK_WIDTH), lambda b, i: (b, order(i), CV_QK // (2 * ML_QK_WIDTH))),
        pl.BlockSpec((1, TILE, ML_V_WIDTH), lambda b, i: (b, order(i), _P_OFF["m_v"] // ML_V_WIDTH)),
        pl.BlockSpec((1, TILE, LANES), lambda b, i: (b, order(i), 0)),
        pl.BlockSpec((1, LANES), const),
        pl.BlockSpec((1, LANES), const),
        pl.BlockSpec((LANES, ML_ND * LANES), const)]
    args = (cvm, p3, sm3, row(i_bias, ML_GATE), row(f_bias, ML_GATE + ML_ND), jnp.asarray(sel, BF16))
    cn_block = (1, 1, ML_HEADS, ML_QK_DIM, ML_AUG)
    m_block = (1, 1, SUBLANES, LANES)
    scratch = [pltpu.VMEM(cn_block[2:], F32), pltpu.VMEM(m_block[2:], F32)]
    bwd = _tile_order(nt, True, n_ctx // TILE)
    fwd = _tile_order(nt, False)
    st_idx = lambda b, i: (b, bwd(i)) + (0,) * 3
    states = (_ml_states_body, specs(bwd), list(args),
              [jax.ShapeDtypeStruct((nb, nt) + cn_block[2:], BF16),
               jax.ShapeDtypeStruct((nb, nt) + m_block[2:], F32)],
              [pl.BlockSpec(cn_block, st_idx), pl.BlockSpec(m_block, lambda b, i: (b, bwd(i), 0, 0))],
              scratch)
    out = lambda cn_b, m_b: (
        _ml_out_body,
        specs(fwd) + [pl.BlockSpec(cn_block, lambda b, i: (b, i, 0, 0, 0)),
                      pl.BlockSpec(m_block, lambda b, i: (b, i, 0, 0))],
        list(args) + [cn_b, m_b],
        [jax.ShapeDtypeStruct((nb, t, ML_V_WIDTH), F32)],
        [pl.BlockSpec((1, TILE, ML_V_WIDTH), lambda b, i: (b, i, 0))], scratch)
    return states, out


GLA_SUB = 256
GLA_NCH = GLA_SUB // GLA_CHUNK
GLA_COLS = BF16_SUBLANES


def _gla_layout(is_ctx, lat_rows):
    r = np.arange(GLA_SUB)
    if is_ctx:
        return r // GLA_CHUNK, r % GLA_CHUNK
    col = r % SUBLANES
    cpc = GLA_CHUNK // lat_rows
    return col // cpc, (col % cpc) * lat_rows + r // SUBLANES


def _gla_consts(is_ctx, lat_rows, rev):
    ch, pos = _gla_layout(is_ctx, lat_rows)
    same = ch[:, None] == ch[None, :]
    before = (pos[None, :] >= pos[:, None]) if rev else (pos[None, :] <= pos[:, None])
    tri = (same & before).astype(np.float32)
    cmask = np.stack([np.repeat((ch == j)[:, None], LANES, axis=1) for j in range(GLA_NCH)])
    return tri, cmask.astype(np.float32)


def _gla_row(is_ctx, lat_rows, j, p):
    ch, pos = _gla_layout(is_ctx, lat_rows)
    return int(np.nonzero((ch == j) & (pos == p))[0][0])


def _per_chunk_rows(b, is_ctx, lat_rows, p):
    rows = [b[_gla_row(is_ctx, lat_rows, j, p):_gla_row(is_ctx, lat_rows, j, p) + 1, :]
            for j in range(GLA_NCH)]
    w = b.shape[1]
    if is_ctx:
        full = jnp.concatenate([jnp.broadcast_to(r, (GLA_CHUNK, w)) for r in rows], axis=0)
    else:
        rep = SUBLANES // GLA_NCH
        pat = jnp.concatenate([jnp.broadcast_to(r, (rep, w)) for r in rows], axis=0)
        full = jnp.broadcast_to(pat[None], (GLA_SUB // SUBLANES, SUBLANES, w)).reshape(GLA_SUB, w)
    return rows, full


def _gla_sub(q, k, v, araw, aup, abias, tri_b, tri_f, cmask_ref, st_ref, store, *, rev, is_ctx, lat_rows):
    want_out = store is not None
    x = _dot(araw.astype(BF16), aup) + abias
    g = _log_sigmoid(x) * (1.0 / GLA_TAU)
    b = _dot_exact_lhs(tri_b, g, 2)
    yield
    lasts, last = _per_chunk_rows(b, is_ctx, lat_rows, 0 if rev else GLA_CHUNK - 1)
    kl = (k * jnp.exp(last - b)).astype(BF16)
    if want_out:
        _, ref = _per_chunk_rows(b, is_ctx, lat_rows, GLA_CHUNK // 2)
        qs = q * (GLA_K_DIM ** -0.5)
        qe = (qs * jnp.exp(b - ref)).astype(BF16)
        ke = (k * jnp.exp(ref - b)).astype(BF16)
        qb = (qs * jnp.exp(b)).astype(BF16)
        visible = tri_f > 0.0
    yield
    order = range(GLA_NCH - 1, -1, -1) if rev else range(GLA_NCH)
    outs = []
    for h in range(GLA_HEADS):
        ks = slice(GLA_K_DIM * h, GLA_K_DIM * (h + 1))
        vh = v[:, GLA_V_DIM * h:GLA_V_DIM * (h + 1)]
        klm = jnp.concatenate([kl[:, ks] * cmask_ref[j] for j in range(GLA_NCH)], axis=1)
        upd = _dot_tn(vh, klm)
        s = st_ref[h]
        s_in = [None] * GLA_NCH
        for j in order:
            s_in[j] = s.astype(BF16)
            s = s * jnp.exp(lasts[j][:, ks]) + upd[:, GLA_K_DIM * j:GLA_K_DIM * (j + 1)]
        st_ref[h] = s
        if want_out:
            att = jnp.where(visible, _dot_nt(qe[:, ks], ke[:, ks]), 0.0).astype(BF16)
            qbm = jnp.concatenate([qb[:, ks] * cmask_ref[j] for j in range(GLA_NCH)], axis=1)
            outs.append(_dot(att, vh) + _dot_nt(qbm, jnp.concatenate(s_in, axis=1)))
        yield
    if want_out:
        store(jnp.concatenate(outs, axis=1))


def _gla_ctx_body(qc_ref, kc_ref, vc_ref, sc_ref, ql_ref, kl_ref, vl_ref, sl_ref,
                  aup_ref, ab_ref, tcb_ref, tcf_ref, cmc_ref, tlb_ref, tlf_ref, cml_ref,
                  o_ref, st_ref, ctxo_ref, *, rev, ctx_rows, lat_rows, n_cblk, ctx_out):
    st_ref[...] = jnp.zeros_like(st_ref)

    def store(o):
        ctxo_ref[...] = o

    yield from _gla_sub(qc_ref[0].astype(F32), kc_ref[0].astype(F32), vc_ref[0], sc_ref[0],
                        aup_ref[...], ab_ref[...], tcb_ref[...], tcf_ref[...], cmc_ref, st_ref,
                        store if ctx_out else None, rev=rev, is_ctx=True, lat_rows=lat_rows)


def _gla_lat_body(qc_ref, kc_ref, vc_ref, sc_ref, ql_ref, kl_ref, vl_ref, sl_ref,
                  aup_ref, ab_ref, tcb_ref, tcf_ref, cmc_ref, tlb_ref, tlf_ref, cml_ref,
                  o_ref, st_ref, ctxo_ref, *, rev, ctx_rows, lat_rows, n_cblk, ctx_out):
    i = pl.program_id(1)
    r0, r1 = ctx_rows, ctx_rows + lat_rows
    cblk = (n_cblk - i) if rev else (i - 1)
    halves = range(GLA_COLS // SUBLANES)
    for half in (reversed(halves) if rev else halves):
        cs = slice(SUBLANES * half, SUBLANES * (half + 1))
        take = lambda r: r[0, r0:r1].astype(F32)[:, cs, :].reshape(GLA_SUB, r.shape[-1])

        def store(o, cs=cs):
            o_ref[0, r0:r1, cs, :] = o.reshape(lat_rows, SUBLANES, GLA_V_WIDTH)

        yield from _gla_sub(take(ql_ref), take(kl_ref), take(vl_ref).astype(BF16), take(sl_ref),
                            aup_ref[...], ab_ref[...], tlb_ref[...], tlf_ref[...], cml_ref, st_ref,
                            store, rev=rev, is_ctx=False, lat_rows=lat_rows)
    for r in range(ctx_rows):
        if ctx_out:
            start = pl.multiple_of(r * GRID_W + cblk * GLA_COLS, GLA_COLS)
            o_ref[0, r, :, :] = ctxo_ref[pl.ds(start, GLA_COLS), :]
        else:
            o_ref[0, r, :, :] = jnp.zeros((GLA_COLS, GLA_V_WIDTH), F32)


GLA_PHASES = ((lambda i: i == 0, "ctx"), (lambda i: i > 0, "lat"))


def _gla_part(p3, sm3, a_up, a_bias, n_ctx, rev, ctx_out):
    nb, t, ncol = p3.shape
    rows = t // GRID_W
    ctx_rows = n_ctx // GRID_W
    lat_rows = rows - ctx_rows
    n_cblk = GRID_W // GLA_COLS
    p4 = p3.reshape(nb, rows, GRID_W, ncol)
    sm4 = sm3.reshape(nb, rows, GRID_W, LANES)
    a_off = _SM_OFF["a_b"] if rev else _SM_OFF["a_f"]
    aup = jnp.pad(a_up, ((a_off, LANES - a_off - GLA_RANK), (0, 0))).astype(BF16)
    cblk = lambda i: jnp.where(i == 0, n_cblk - 1 if rev else 0, (n_cblk - i) if rev else (i - 1))
    ctx = lambda blk: (lambda b, i: (b, 0, blk))
    lat = lambda blk: (lambda b, i: (b, 0, cblk(i), blk))
    const2 = lambda b, i: (0, 0)
    const3 = lambda b, i: (0, 0, 0)
    widths = (GLA_K_WIDTH, GLA_K_WIDTH, GLA_V_WIDTH)
    offs = (_P_OFF["g_q"], _P_OFF["g_k"], _P_OFF["g_v"])
    consts = []
    const_specs = []
    for is_ctx in (True, False):
        tri, cmask = _gla_consts(is_ctx, lat_rows, rev)
        consts += [jnp.asarray(tri, BF16), jnp.asarray(tri, F32), jnp.asarray(cmask, BF16)]
        const_specs += [pl.BlockSpec((GLA_SUB, GLA_SUB), const2), pl.BlockSpec((GLA_SUB, GLA_SUB), const2),
                        pl.BlockSpec((GLA_NCH, GLA_SUB, LANES), const3)]
    out_spec = pl.BlockSpec((1, rows, GLA_COLS, GLA_V_WIDTH), lat(0))
    in_specs = ([pl.BlockSpec((1, n_ctx, w), ctx(o // w)) for w, o in zip(widths, offs)]
                + [pl.BlockSpec((1, n_ctx, LANES), ctx(0))]
                + [pl.BlockSpec((1, rows, GLA_COLS, w), lat(o // w)) for w, o in zip(widths, offs)]
                + [pl.BlockSpec((1, rows, GLA_COLS, LANES), lat(0))]
                + [pl.BlockSpec((LANES, GLA_K_WIDTH), const2), pl.BlockSpec((1, GLA_K_WIDTH), const2)]
                + const_specs)
    args = [p3, p3, p3, sm3, p4, p4, p4, sm4, aup, a_bias.reshape(1, GLA_K_WIDTH)] + consts
    static = dict(rev=rev, ctx_rows=ctx_rows, lat_rows=lat_rows, n_cblk=n_cblk, ctx_out=ctx_out)
    bodies = {"ctx": functools.partial(_gla_ctx_body, **static),
              "lat": functools.partial(_gla_lat_body, **static)}
    return (bodies, in_specs, args,
            [jax.ShapeDtypeStruct((nb, rows, GRID_W, GLA_V_WIDTH), F32)], [out_spec],
            [pltpu.VMEM((GLA_HEADS, GLA_V_DIM, GLA_K_DIM), F32), pltpu.VMEM((n_ctx, GLA_V_WIDTH), F32)])


def _group_rmsnorm(y, groups):
    width = y.shape[-1] // groups
    ones = jnp.ones((width, LANES), BF16)
    out = []
    for g in range(groups):
        yg = y[:, width * g:width * (g + 1)]
        ms = _dot((yg * yg).astype(BF16), ones) * (1.0 / width)
        out.append(yg * jnp.concatenate([lax.rsqrt(ms + EPS)] * (width // LANES), axis=1))
    return jnp.concatenate(out, axis=1)


def _cast_weights_once(pairs):
    @pl.when((pl.program_id(0) == 0) & (pl.program_id(1) == 0))
    def _cast():
        for w_ref, s_ref in pairs:
            s_ref[...] = w_ref[...].astype(s_ref.dtype)


def _post_kernel(x_ref, y_ref, h_ref, of_ref, ob_ref, z_ref, mo_ref, gg_ref, gs_ref, gm_ref, gl_ref,
                 nws_ref, nwm_ref, nwg_ref, wbs32_ref, wbm32_ref, wbg32_ref, wout32_ref, g1_ref,
                 nwf_ref, sc2_ref, sh2_ref, xo_ref, ho_ref, wbs_ref, wbm_ref, wbg_ref, wout_ref):
    _cast_weights_once(((wbs32_ref, wbs_ref), (wbm32_ref, wbm_ref), (wbg32_ref, wbg_ref), (wout32_ref, wout_ref)))
    y_ssd = (_group_rmsnorm(y_ref[0] * _silu(z_ref[0]), 2) * nws_ref[...]).astype(BF16)
    y_ml = (_group_rmsnorm(h_ref[0], ML_HEADS) * nwm_ref[...]).astype(BF16) * _sigmoid(mo_ref[0])
    y_gla = (_group_rmsnorm(of_ref[0] + ob_ref[0], GLA_HEADS) * nwg_ref[...]).astype(BF16) * _silu(gg_ref[0])
    merged = (_sigmoid(gs_ref[0]) * _dot(y_ssd, wbs_ref[...])
              + _sigmoid(gm_ref[0]) * _dot(y_ml, wbm_ref[...])
              + _sigmoid(gl_ref[0]) * _dot(y_gla, wbg_ref[...]))
    x_new = x_ref[0] + g1_ref[0] * _dot(merged.astype(BF16), wout_ref[...])
    xo_ref[0] = x_new
    ho_ref[0] = (_rms(x_new) * nwf_ref[...] * (1.0 + sc2_ref[0]) + sh2_ref[0]).astype(ho_ref.dtype)


def _post(x, scans, p3, norm_ws, w_bs, w_out, layer, norm_ffn_w, mods, n_ctx, tile0):
    nb, t, d = x.shape
    nt = t // TILE - tile0
    ctx_tiles = n_ctx // TILE
    tok = lambda blk: (lambda b, i: (b, i + tile0, blk))
    out = lambda b, i: (b, i, 0)
    const = lambda b, i: (0, 0)
    tok_spec = lambda blk: pl.BlockSpec((1, TILE, d), tok(blk))
    w_spec = pl.BlockSpec((None, d, d), lambda b, i: (layer, 0, 0), pipeline_mode=pl.Buffered(1))
    vec = pl.BlockSpec((1, d), const)
    mod = lambda k: pl.BlockSpec((1, 1, d), _mod_row(nb, k, ctx_tiles, tile0))
    names = ("s_z", "m_o", "g_g", "gate_ssd", "gate_ml", "gate_gla")
    in_specs = ([tok_spec(0)] * (1 + len(scans)) + [tok_spec(_P_OFF[nm] // d) for nm in names]
                + [vec] * 3 + [w_spec] * 4 + [mod(2), vec, mod(4), mod(3)])
    return pl.pallas_call(
        _post_kernel,
        out_shape=(jax.ShapeDtypeStruct((nb, nt * TILE, d), F32),
                   jax.ShapeDtypeStruct((nb, nt * TILE, d), BF16)),
        grid=(nb, nt),
        in_specs=in_specs,
        out_specs=(pl.BlockSpec((1, TILE, d), out), pl.BlockSpec((1, TILE, d), out)),
        scratch_shapes=[pltpu.VMEM((d, d), BF16)] * 4,
        compiler_params=_cparams(("arbitrary", "arbitrary")),
        name="post",
    )(x, *scans, *([p3] * 6), *[w.reshape(1, d) for w in norm_ws], *w_bs, w_out, mods,
      norm_ffn_w.reshape(1, d), mods, mods)


def _ffn_in_kernel(a_ref, wg_ref, wu_ref, o_ref, w_ref):
    half = o_ref.shape[1]

    @pl.when(pl.program_id(1) == 0)
    def _cast():
        w_ref[:, :half] = wg_ref[...].astype(w_ref.dtype)
        w_ref[:, half:] = wu_ref[...].astype(w_ref.dtype)

    acc = _dot(a_ref[...], w_ref[...])
    o_ref[...] = (_silu(acc[:, :half]) * acc[:, half:]).astype(o_ref.dtype)


def _ffn_in(h, w_ffn_in, layer, half):
    m, k = h.shape
    n_half = w_ffn_in.shape[-1] // 2
    tm = _row_tile(m)
    return pl.pallas_call(
        _ffn_in_kernel,
        out_shape=jax.ShapeDtypeStruct((m, n_half), BF16),
        grid=(n_half // half, m // tm),
        in_specs=[pl.BlockSpec((tm, k), lambda j, i: (i, 0)),
                  pl.BlockSpec((None, k, half), lambda j, i: (layer, 0, j)),
                  pl.BlockSpec((None, k, half), lambda j, i: (layer, 0, j + n_half // half))],
        out_specs=pl.BlockSpec((tm, half), lambda j, i: (i, j)),
        scratch_shapes=[pltpu.VMEM((k, 2 * half), BF16)],
        compiler_params=_cparams(("arbitrary", "arbitrary")),
        name="ffn_in",
    )(h, w_ffn_in, w_ffn_in)


def _ffn_out_kernel(a_ref, w32_ref, x_ref, g_ref, nw_ref, sc_ref, sh_ref, xo_ref, ho_ref, w_ref):
    _cast_weights_once(((w32_ref, w_ref),))
    x_new = x_ref[0] + g_ref[0] * _dot(a_ref[0], w_ref[...])
    xo_ref[0] = x_new
    ho_ref[0] = (_rms(x_new) * nw_ref[...] * (1.0 + sc_ref[0]) + sh_ref[0]).astype(ho_ref.dtype)


def _ffn_out_last_kernel(a_ref, w32_ref, x_ref, g_ref, nw_ref, o_ref, w_ref):
    _cast_weights_once(((w32_ref, w_ref),))
    x_new = x_ref[0] + g_ref[0] * _dot(a_ref[0], w_ref[...])
    o_ref[0] = _rms(x_new) * nw_ref[...]


def _ffn_out(a, w, layer, x, mods, n_ctx, tile0, next_norm_w, next_mods):
    nb, t, d = x.shape
    k = a.shape[-1]
    ctx_tiles = n_ctx // TILE
    tok = lambda b, i: (b, i, 0)
    mod = lambda k_: pl.BlockSpec((1, 1, d), _mod_row(nb, k_, ctx_tiles, tile0))
    in_specs = [pl.BlockSpec((1, TILE, k), tok),
                pl.BlockSpec((None, k, d), lambda b, i: (layer, 0, 0), pipeline_mode=pl.Buffered(1)),
                pl.BlockSpec((1, TILE, d), tok),
                mod(5),
                pl.BlockSpec((1, d), lambda b, i: (0, 0))]
    args = [a, w, x, mods, next_norm_w.reshape(1, d)]
    if next_mods is None:
        body = _ffn_out_last_kernel
        out_shape = jax.ShapeDtypeStruct((nb, t, d), F32)
        out_specs = pl.BlockSpec((1, TILE, d), tok)
    else:
        body = _ffn_out_kernel
        in_specs += [mod(1), mod(0)]
        args += [next_mods, next_mods]
        out_shape = (jax.ShapeDtypeStruct((nb, t, d), F32), jax.ShapeDtypeStruct((nb, t, d), BF16))
        out_specs = (pl.BlockSpec((1, TILE, d), tok), pl.BlockSpec((1, TILE, d), tok))
    return pl.pallas_call(
        body,
        out_shape=out_shape,
        grid=(nb, t // TILE),
        in_specs=in_specs,
        out_specs=out_specs,
        scratch_shapes=[pltpu.VMEM((k, d), BF16)],
        compiler_params=_cparams(("arbitrary", "arbitrary")),
        name="ffn_out",
    )(*args)


def _proj_weights(w_in):
    cols = lambda names: [w_in[..., _IN_OFF[nm]:_IN_OFF[nm] + _IN_W[nm]] for nm in names]
    main = jnp.concatenate(cols(_P_ORDER), axis=-1)
    small = jnp.pad(jnp.concatenate(cols(_SMALL), axis=-1), ((0, 0), (0, 0), (0, LANES - N_SMALL_USED)))
    return main.astype(BF16), small.astype(BF16)


def kernel(x, c, ctx, c_ctx, w_mod, b_mod, norm_mix_w, norm_ffn_w, w_in, ssd_conv_w, ssd_conv_b, ssd_dt_bias, ssd_a_log, ssd_d, ssd_norm_w, ml_conv_w, ml_conv_b, ml_i_bias, ml_f_bias, ml_norm_w, gla_a_up, gla_a_bias, gla_norm_w, w_b_ssd, w_b_ml, w_b_gla, w_out, w_ffn_in, w_ffn_out, final_norm_w):
    nb, n_lat, d = x.shape
    n_ctx = ctx.shape[1]
    t = n_ctx + n_lat
    depth = w_in.shape[0]
    d_ff = w_ffn_out.shape[1]
    assert n_ctx == TILE == GLA_SUB and n_lat % TILE == 0 and n_lat // GRID_W == 32
    ffn_half = d_ff // 2

    c16 = jnp.pad(jnp.concatenate([c, c_ctx[None]], axis=0), ((0, 2 * SUBLANES - nb - 1), (0, 0)))
    mods = [_modulation(c16, w_mod, b_mod, l).reshape(2 * SUBLANES * 6, 1, d) for l in range(depth)]
    xs, h = _norm_mod(ctx, x, norm_mix_w[0], mods[0], 0, 1)
    conv_w = jnp.concatenate([ssd_conv_w, ml_conv_w], axis=-1)
    conv_b = jnp.concatenate([ssd_conv_b, ml_conv_b], axis=-1)
    conv_post = jnp.concatenate([jnp.ones((ssd_conv_w.shape[-1] + ML_QK_WIDTH,), F32),
                                 jnp.full((ML_QK_WIDTH,), ML_QK_DIM ** -0.5, F32)])
    w_main, w_small = _proj_weights(w_in)
    w_bs = (w_b_ssd, w_b_ml, w_b_gla)
    for l in range(depth):
        last = l == depth - 1
        h2d = h.reshape(nb * t, d)
        tn = N_PROJ // 5
        conv_tile = _P_OFF["s_x"] // tn
        assert conv_tile * tn == _P_OFF["s_x"] and N_PROJ - _P_OFF["s_x"] == tn == CV_WIDTH
        m_tiles = nb * t // _row_tile(nb * t)
        (pc,), (sm,) = _run_together("proj_first", (1, m_tiles),
                                     _mm_part(h2d, w_main, l, tn, BF16, conv_tile, 1),
                                     _mm_part(h2d, w_small, l, LANES, F32, 0, 1))
        pc3, sm3 = pc.reshape(nb, t, tn), sm.reshape(nb, t, LANES)
        grid = (conv_tile, m_tiles)
        (p,), (cv,) = _run_together(
            "proj_conv", grid, _mm_part(h2d, w_main, l, tn, BF16, 0, conv_tile),
            _conv_part(pc3, conv_w[l], conv_b[l], conv_post, n_ctx, grid))
        p3 = p.reshape(nb, t, conv_tile * tn)
        d_e = jnp.repeat(ssd_d[l], SSD_HEAD_DIM).reshape(1, SSD_WIDTH)
        ssd_states, ssd_out = _ssd_parts(cv, sm3, ssd_dt_bias[l], ssd_a_log[l], d_e, n_ctx)
        ml_states, ml_out = _mlstm_parts(cv, p3, sm3, ml_i_bias[l], ml_f_bias[l], n_ctx)
        grid = (nb, t // TILE)
        ssd_st, ml_st = _run_together("bwd_states", grid, ssd_states, ml_states)
        (y,), (hm,) = _run_together("ssd_mlstm", grid, ssd_out(*ssd_st), ml_out(*ml_st))
        gla = [_gla_part(p3, sm3, gla_a_up[l, k], gla_a_bias[l, k], n_ctx, bool(k), not last) for k in range(2)]
        (og_f,), (og_b,) = _run_together("gla", (nb, GRID_W // GLA_COLS + 1), *gla, phases=GLA_PHASES)
        og_f, og_b = og_f.reshape(nb, t, GLA_V_WIDTH), og_b.reshape(nb, t, GLA_V_WIDTH)
        tile0 = n_ctx // TILE if last else 0
        xs, h2 = _post(xs, (y, hm, og_f, og_b), p3, (ssd_norm_w[l], ml_norm_w[l], gla_norm_w[l]),
                       w_bs, w_out, l, norm_ffn_w[l], mods[l], n_ctx, tile0)
        nt = xs.shape[1]
        a = _ffn_in(h2.reshape(nb * nt, d), w_ffn_in, l, ffn_half)
        a = a.reshape(nb, nt, d_ff)
        if last:
            return _ffn_out(a, w_ffn_out, l, xs, mods[l], n_ctx, tile0, final_norm_w, None)
        xs, h = _ffn_out(a, w_ffn_out, l, xs, mods[l], n_ctx, tile0,
                         norm_mix_w[l + 1], mods[l + 1])
```

```python
import functools

import numpy as np
import jax
import jax.numpy as jnp
from jax import lax
from jax.experimental import pallas as pl
from jax.experimental.pallas import tpu as pltpu

F32 = jnp.float32
BF16 = jnp.bfloat16

EPS = 1e-6
LOG2E = 1.4426950408889634
GRID_W = 64
SSD_HEADS = 16
SSD_HEAD_DIM = 64
SSD_WIDTH = 1024
SSD_STATE = 64
SSD_BC = 128
ML_HEADS = 4
ML_QK_DIM = 128
ML_V_DIM = 256
ML_QK_WIDTH = 512
ML_V_WIDTH = 1024
GLA_HEADS = 4
GLA_K_DIM = 128
GLA_V_DIM = 256
GLA_K_WIDTH = 512
GLA_V_WIDTH = 1024
GLA_RANK = 16
GLA_TAU = 16.0
GLA_CHUNK = 64

LANES = 128
SUBLANES = 8
BF16_SUBLANES = 16
VMEM_LIMIT = 56 * 1024 * 1024

TILE = 256

_IN_NAMES = ("s_x", "s_z", "s_b", "s_c", "dt_f", "dt_b",
             "m_q", "m_k", "m_v", "m_o", "i_f", "i_b", "f_f", "f_b",
             "g_q", "g_k", "g_v", "g_g", "a_f", "a_b",
             "gate_ssd", "gate_ml", "gate_gla")
_IN_WIDTHS = (1024, 1024, 128, 128, 16, 16,
              512, 512, 1024, 1024, 4, 4, 4, 4,
              512, 512, 1024, 1024, 16, 16,
              1024, 1024, 1024)
_IN_OFF = dict(zip(_IN_NAMES, np.concatenate([[0], np.cumsum(_IN_WIDTHS)[:-1]]).tolist()))
_IN_W = dict(zip(_IN_NAMES, _IN_WIDTHS))

_P_ORDER = ("s_z", "m_o", "g_g", "gate_ssd", "gate_ml", "gate_gla", "m_v", "g_v",
            "g_q", "g_k", "s_x", "s_b", "s_c", "m_q", "m_k")
_P_OFF = {}
_o = 0
for _n in _P_ORDER:
    _P_OFF[_n] = _o
    _o += _IN_W[_n]
N_PROJ = _o
_SMALL = ("dt_f", "dt_b", "i_f", "i_b", "f_f", "f_b", "a_f", "a_b")
_SM_OFF = {}
_s = 0
for _n in _SMALL:
    _SM_OFF[_n] = _s
    _s += _IN_W[_n]
N_SMALL_USED = _s


def _cparams(sem):
    return pltpu.CompilerParams(dimension_semantics=sem, vmem_limit_bytes=VMEM_LIMIT)


def _sigmoid(x):
    return 0.5 * jnp.tanh(0.5 * x) + 0.5


def _silu(x):
    h = 0.5 * x
    return h + h * jnp.tanh(h)


def _softplus(x):
    return jnp.maximum(x, 0.0) + jnp.log1p(jnp.exp(-jnp.abs(x)))


def _log_sigmoid(x):
    return jnp.minimum(x, 0.0) - jnp.log(1.0 + jnp.exp(-jnp.abs(x)))


def _split(x, n):
    out = []
    r = x
    for _ in range(n):
        p = r.astype(BF16)
        out.append(p)
        r = r - p.astype(F32)
    return out


def _dot(a, b):
    return jnp.dot(a, b, preferred_element_type=F32)


def _dot_nt(a, b):
    return lax.dot_general(a, b, (((1,), (1,)), ((), ())), preferred_element_type=F32)


def _dot_tn(a, b):
    return lax.dot_general(a, b, (((0,), (0,)), ((), ())), preferred_element_type=F32)


def _dot_exact_lhs(t, x, pieces):
    return sum(_dot(t, p) for p in _split(x, pieces))


def _dot_hp(a, b):
    ah, am = _split(a, 2)
    bh, bm = _split(b, 2)
    return _dot(ah, bh) + _dot(ah, bm) + _dot(am, bh)


def _causal(n, rev):
    t = lax.broadcasted_iota(jnp.int32, (n, n), 0)
    s = lax.broadcasted_iota(jnp.int32, (n, n), 1)
    return (s >= t) if rev else (s <= t)


def _rms(x):
    return x * lax.rsqrt(jnp.mean(x * x, axis=-1, keepdims=True) + EPS)


def _mod_kernel(c_ref, w_ref, b_ref, o_ref):
    o_ref[...] = _dot_hp(_silu(c_ref[...]), w_ref[...]) + b_ref[...]


def _modulation(c16, w_mod, b_mod, layer):
    rows, d = c16.shape
    n = w_mod.shape[-1]
    tn = 1536
    return pl.pallas_call(
        _mod_kernel,
        out_shape=jax.ShapeDtypeStruct((rows, n), F32),
        grid=(n // tn,),
        in_specs=[pl.BlockSpec((rows, d), lambda j: (0, 0)),
                  pl.BlockSpec((None, d, tn), lambda j: (layer, 0, j)),
                  pl.BlockSpec((None, 1, tn), lambda j: (layer, 0, j))],
        out_specs=pl.BlockSpec((rows, tn), lambda j: (0, j)),
        compiler_params=_cparams(("arbitrary",)),
        name="modulation",
    )(c16, w_mod, b_mod.reshape(b_mod.shape[0], 1, n))


def _mod_row(nb, k, ctx_tiles, tile0=0):
    return lambda b, i: (jnp.where(i + tile0 < ctx_tiles, nb, b) * 6 + k, 0, 0)


def _norm_mod_kernel(ctx_ref, x_ref, w_ref, sc_ref, sh_ref, xo_ref, ho_ref):
    x = jnp.where(pl.program_id(1) == 0, ctx_ref[0], x_ref[0])
    xo_ref[0] = x
    ho_ref[0] = (_rms(x) * w_ref[...] * (1.0 + sc_ref[0]) + sh_ref[0]).astype(ho_ref.dtype)


def _norm_mod(ctx, x, w, mods, k_shift, k_scale):
    nb, n_lat, d = x.shape
    n_ctx = ctx.shape[1]
    assert n_ctx == TILE
    t = n_ctx + n_lat
    tok = lambda b, i: (b, i, 0)
    return pl.pallas_call(
        _norm_mod_kernel,
        out_shape=(jax.ShapeDtypeStruct((nb, t, d), F32), jax.ShapeDtypeStruct((nb, t, d), BF16)),
        grid=(nb, t // TILE),
        in_specs=[pl.BlockSpec((1, TILE, d), lambda b, i: (b, 0, 0)),
                  pl.BlockSpec((1, TILE, d), lambda b, i: (b, jnp.maximum(i - 1, 0), 0)),
                  pl.BlockSpec((1, d), lambda b, i: (0, 0)),
                  pl.BlockSpec((1, 1, d), _mod_row(nb, k_scale, n_ctx // TILE)),
                  pl.BlockSpec((1, 1, d), _mod_row(nb, k_shift, n_ctx // TILE))],
        out_specs=(pl.BlockSpec((1, TILE, d), tok), pl.BlockSpec((1, TILE, d), tok)),
        compiler_params=_cparams(("parallel", "arbitrary")),
        name="norm_mod",
    )(ctx, x, w.reshape(1, d), mods, mods)


MM_STAGE = 256


def _mm_body(a_ref, w_ref, o_ref):
    a = a_ref[...]
    width = o_ref.shape[1]
    stage = min(MM_STAGE, width)
    for c0 in range(0, width, stage):
        o_ref[:, c0:c0 + stage] = _dot(a, w_ref[:, c0:c0 + stage]).astype(o_ref.dtype)
        yield


def _row_tile(m):
    return 512 if m % 512 == 0 else TILE


def _mm_part(a, w, layer, tn, out_dtype, tile0, n_tiles):
    m, k = a.shape
    tm = _row_tile(m)
    return (_mm_body,
            [pl.BlockSpec((tm, k), lambda j, i: (i, 0)),
             pl.BlockSpec((None, k, tn), lambda j, i: (layer, 0, j + tile0))],
            [a, w], [jax.ShapeDtypeStruct((m, n_tiles * tn), out_dtype)],
            [pl.BlockSpec((tm, tn), lambda j, i: (i, j))], [])


CONV_K = 5
CONV_ROWS = 128
CONV_X = SSD_WIDTH // LANES
CONV_BC = 2 * SSD_BC // LANES
CONV_QK = 2 * ML_QK_WIDTH // LANES
CV_QK = SSD_WIDTH
CV_BC = SSD_WIDTH + 2 * ML_QK_WIDTH
CV_WIDTH = CV_BC + 2 * SSD_BC


def _conv_body(u_ref, w_ref, b_ref, s_ref, o_ref, pad_ref, *, n_ctx):
    t, c = u_ref.shape[1], u_ref.shape[2]
    half = CONV_K // 2
    zeros = jnp.zeros((SUBLANES, c), F32)
    w = w_ref[...]
    bias = b_ref[...]
    post = s_ref[...]
    for s0, n in ((0, n_ctx), (n_ctx, t - n_ctx)):
        pad_ref[0:SUBLANES, :] = zeros
        pad_ref[SUBLANES:SUBLANES + n, :] = u_ref[0, s0:s0 + n, :].astype(F32)
        pad_ref[SUBLANES + n:2 * SUBLANES + n, :] = zeros
        for r0 in range(0, n, CONV_ROWS):
            acc = bias
            for j in range(CONV_K):
                lo = SUBLANES - half + j + r0
                acc = acc + w[j:j + 1, :] * pad_ref[lo:lo + CONV_ROWS, :]
            o_ref[0, s0 + r0:s0 + r0 + CONV_ROWS, :] = (_silu(acc) * post).astype(o_ref.dtype)
            if (r0 // CONV_ROWS) % 4 == 3:
                yield


def _conv_part(u3, w, b, post_scale, n_ctx, grid):
    nb, t, width = u3.shape
    n_cb = width // LANES
    steps = grid[0] * grid[1]
    rep = steps // (nb * n_cb)
    assert steps == rep * nb * n_cb and n_cb == CONV_X + CONV_BC + CONV_QK
    blk = lambda j, i: (j * grid[1] + i) // rep
    chan = lambda j, i: blk(j, i) % n_cb
    out_chan = lambda c: jnp.where(c < CONV_X, c, jnp.where(c < CONV_X + CONV_BC, c + CONV_QK, c - CONV_BC))
    vec = lambda rows: pl.BlockSpec((rows, LANES), lambda j, i: (0, chan(j, i)))
    return (functools.partial(_conv_body, n_ctx=n_ctx),
            [pl.BlockSpec((1, t, LANES), lambda j, i: (blk(j, i) // n_cb, 0, chan(j, i))),
             vec(CONV_K), vec(1), vec(1)],
            [u3, w, b.reshape(1, width), post_scale.reshape(1, width)],
            [jax.ShapeDtypeStruct((nb, t, width), BF16)],
            [pl.BlockSpec((1, t, LANES), lambda j, i: (blk(j, i) // n_cb, 0, out_chan(chan(j, i))))],
            [pltpu.VMEM((t + 2 * SUBLANES, LANES), F32)])


def _tile_order(n_tiles, rev, ctx_tiles=1):
    if rev:
        return lambda i: jnp.where(i < ctx_tiles, ctx_tiles - 1 - i, n_tiles - 1 + ctx_tiles - i)
    return lambda i: i


SSD_PAIRS = SSD_HEADS // 2
SSD_B_OFF = SSD_HEADS
SSD_TILE = 256


def _ssd_gates(sm_ref, dtb_ref, alog_ref, n, both):
    lane = lax.broadcasted_iota(jnp.int32, (1, LANES), 1)
    dt = _softplus(sm_ref[0] + dtb_ref[...])
    la = dt * jnp.where(lane < 2 * SSD_HEADS, -jnp.exp(alog_ref[...]), 0.0)
    parts = _split(la, 2)
    upp = jnp.where(_causal(n, True), 1.0, 0.0).astype(BF16)
    cum = sum(_dot(upp, p) for p in parts)
    if both:
        low = jnp.where(_causal(n, False), 1.0, 0.0).astype(BF16)
        cum = jnp.where(lane < SSD_HEADS, sum(_dot(low, p) for p in parts), cum)
    return lane, dt, cum


def _expand(a, e):
    return _dot(a.astype(BF16), e)


def _group_dup(v, g, lo):
    other = pltpu.roll(v, SSD_STATE, axis=1)
    return jnp.where(lo, v, other) if g == 0 else jnp.where(lo, other, v)


def _ssd_state_step(st_ref, j, bw, xp, elast, off):
    r = lax.broadcasted_iota(jnp.int32, (LANES, LANES), 0) < SSD_STATE
    c = lax.broadcasted_iota(jnp.int32, (LANES, LANES), 1) < SSD_HEAD_DIM
    dec = jnp.where(r, elast[:, off + 2 * j:off + 2 * j + 1], elast[:, off + 2 * j + 1:off + 2 * j + 2])
    st_ref[j] = jnp.where(r == c, dec * st_ref[j] + _dot_tn(bw.astype(BF16), xp), 0.0)


def _interleave(*bodies):
    live = list(bodies)
    while live:
        for body in list(live):
            if next(body, StopIteration) is StopIteration:
                live.remove(body)


def _ssd_states_body(x_ref, bc_ref, sm_ref, dtb_ref, alog_ref, eb_ref, o_ref, st_ref):
    n = x_ref.shape[1]

    @pl.when(pl.program_id(1) == 0)
    def _init():
        st_ref[...] = jnp.zeros_like(st_ref)

    o_ref[0, 0] = st_ref[...].astype(o_ref.dtype)
    yield
    lane, dt, cum = _ssd_gates(sm_ref, dtb_ref, alog_ref, n, False)
    lo = lane < SSD_HEAD_DIM
    last = cum[0:1, :]
    yield
    wst = _expand(jnp.exp(last - cum) * dt, eb_ref[...])
    elast = jnp.exp(last)
    b128 = bc_ref[0, :, :SSD_BC].astype(F32)
    for j in range(SSD_PAIRS):
        sl = slice(LANES * j, LANES * (j + 1))
        bw = _group_dup(b128, j // (SSD_PAIRS // 2), lo) * wst[:, sl]
        _ssd_state_step(st_ref, j, bw, x_ref[0, :, sl], elast, SSD_B_OFF)
        if j % 2:
            yield


def _ssd_out_body(x_ref, bc_ref, sm_ref, dtb_ref, alog_ref, ef_ref, eb_ref, d_ref, sb_ref, o_ref, st_ref):
    n = x_ref.shape[1]

    @pl.when(pl.program_id(1) == 0)
    def _init():
        st_ref[...] = jnp.zeros_like(st_ref)

    zero_b = jnp.zeros((), BF16)
    lo = lax.broadcasted_iota(jnp.int32, (1, LANES), 1) < SSD_HEAD_DIM
    b128_b = bc_ref[0, :, :SSD_BC]
    c128_b = bc_ref[0, :, SSD_BC:]
    cb_all = [_dot_nt(jnp.where(lo if g == 0 else jnp.logical_not(lo), c128_b, zero_b), b128_b)
              for g in range(2)]
    pair_x = lambda j: x_ref[0, :, LANES * j:LANES * (j + 1)]
    rhs_all = [jnp.concatenate([jnp.where(lo, pair_x(j), zero_b), jnp.where(lo, zero_b, pair_x(j)),
                                st_ref[j].astype(BF16), sb_ref[0, 0, j]], axis=0) for j in range(SSD_PAIRS)]
    yield
    lane, dt, cum = _ssd_gates(sm_ref, dtb_ref, alog_ref, n, True)
    is_f = lane < SSD_HEADS
    ldt = jnp.log(dt)
    dsum = jnp.log(dt + pltpu.roll(dt, LANES - SSD_B_OFF, axis=1))
    rt = (jnp.where(lane < 2 * SSD_HEADS, cum - ldt, pltpu.roll(dsum, 2 * SSD_HEADS, axis=1)) * LOG2E).T
    cum2 = cum * LOG2E
    yield
    last = jnp.where(is_f, cum[n - 1:n, :], cum[0:1, :])
    elast = jnp.exp(last)
    ecum = jnp.exp(cum)
    ecum_f = _expand(ecum, ef_ref[...])
    ecum_b = _expand(ecum, eb_ref[...])
    wst = _expand(jnp.exp(last - cum) * dt, ef_ref[...])
    bc = bc_ref[0].astype(F32)
    b128, c128 = bc[:, :SSD_BC], bc[:, SSD_BC:]
    ti = lax.broadcasted_iota(jnp.int32, (n, n), 0)
    si = lax.broadcasted_iota(jnp.int32, (n, n), 1)
    below, above = si < ti, si > ti
    half = SSD_PAIRS // 2
    yield
    for g in range(2):
        cb = cb_all[g]
        cdup = _group_dup(c128, g, lo)
        for j in range(g * half, (g + 1) * half):
            ms = []
            for h in (2 * j, 2 * j + 1):
                e_f = cum2[:, h:h + 1] - rt[h:h + 1, :]
                e_b = cum2[:, SSD_B_OFF + h:SSD_B_OFF + h + 1] - rt[SSD_B_OFF + h:SSD_B_OFF + h + 1, :]
                e = jnp.where(below, e_f, jnp.where(above, e_b, rt[2 * SSD_HEADS + h:2 * SSD_HEADS + h + 1, :]))
                ms.append((cb * jnp.exp2(e)).astype(BF16))
            sl = slice(LANES * j, LANES * (j + 1))
            cs_f = (cdup * ecum_f[:, sl]).astype(BF16)
            cs_b = (cdup * ecum_b[:, sl]).astype(BF16)
            lhs = jnp.concatenate(ms + [cs_f, cs_b], axis=1)
            o_ref[0, :, sl] = _dot(lhs, rhs_all[j]) + d_ref[:, sl] * pair_x(j).astype(F32)
            yield
    for j in range(SSD_PAIRS):
        sl = slice(LANES * j, LANES * (j + 1))
        _ssd_state_step(st_ref, j, _group_dup(b128, j // half, lo) * wst[:, sl], x_ref[0, :, sl], elast, 0)
        if j % 2:
            yield


def _run_together(name, grid, *parts, phases=None):
    n_in = [len(p[1]) for p in parts]
    n_out = [len(p[3]) for p in parts]
    n_scr = [len(p[5]) for p in parts]

    def kern(*refs):
        ins, outs, scr = refs[:sum(n_in)], refs[sum(n_in):sum(n_in) + sum(n_out)], refs[sum(n_in) + sum(n_out):]

        def run(key):
            bodies = []
            for k, p in enumerate(parts):
                take = lambda seq, counts: seq[sum(counts[:k]):sum(counts[:k + 1])]
                body = p[0][key] if isinstance(p[0], dict) else p[0]
                bodies.append(body(*take(ins, n_in), *take(outs, n_out), *take(scr, n_scr)))
            _interleave(*bodies)

        if phases is None:
            run(None)
        else:
            for pred, key in phases:
                pl.when(pred(pl.program_id(1)))(functools.partial(run, key))

    res = pl.pallas_call(
        kern,
        out_shape=tuple(s for p in parts for s in p[3]),
        grid=grid,
        in_specs=[s for p in parts for s in p[1]],
        out_specs=tuple(s for p in parts for s in p[4]),
        scratch_shapes=[s for p in parts for s in p[5]],
        compiler_params=_cparams(("arbitrary", "arbitrary")),
        name=name,
    )(*[a for p in parts for a in p[2]])
    return [list(res[sum(n_out[:k]):sum(n_out[:k + 1])]) for k in range(len(parts))]


def _ssd_parts(cvs, sm3, dt_bias, a_log, d_e, n_ctx):
    nb, t, _ = cvs.shape
    tile = SSD_TILE
    nt = t // tile
    row = lambda v: jnp.pad(v.reshape(1, -1), ((0, 0), (0, LANES - 2 * SSD_HEADS)))
    const = lambda b, i: (0, 0)
    specs = lambda order: [
        pl.BlockSpec((1, tile, SSD_WIDTH), lambda b, i: (b, order(i), 0)),
        pl.BlockSpec((1, tile, 2 * SSD_BC), lambda b, i: (b, order(i), CV_BC // (2 * SSD_BC))),
        pl.BlockSpec((1, tile, LANES), lambda b, i: (b, order(i), 0)),
        pl.BlockSpec((1, LANES), const),
        pl.BlockSpec((1, LANES), const)]
    st_block = (1, 1, SSD_PAIRS, LANES, LANES)
    args = (cvs, cvs, sm3, row(dt_bias), row(a_log))
    sel = np.zeros((2, LANES, SSD_WIDTH), np.float32)
    for h in range(SSD_HEADS):
        sel[0, h, h * SSD_HEAD_DIM:(h + 1) * SSD_HEAD_DIM] = 1.0
        sel[1, SSD_B_OFF + h, h * SSD_HEAD_DIM:(h + 1) * SSD_HEAD_DIM] = 1.0
    e_f, e_b = jnp.asarray(sel[0], BF16), jnp.asarray(sel[1], BF16)
    e_spec = pl.BlockSpec((LANES, SSD_WIDTH), const)
    bwd = _tile_order(nt, True, n_ctx // tile)
    fwd = _tile_order(nt, False)
    scratch = [pltpu.VMEM(st_block[2:], F32)]
    b_spec = pl.BlockSpec((1, tile, SSD_BC), lambda b, i: (b, bwd(i), CV_BC // SSD_BC))
    states = (_ssd_states_body, [specs(bwd)[0], b_spec] + specs(bwd)[2:] + [e_spec], list(args) + [e_b],
              [jax.ShapeDtypeStruct((nb, nt) + st_block[2:], BF16)],
              [pl.BlockSpec(st_block, lambda b, i: (b, bwd(i), 0, 0, 0))], scratch)
    out = lambda states_b: (
        _ssd_out_body,
        specs(fwd) + [e_spec, e_spec, pl.BlockSpec((1, SSD_WIDTH), const),
                      pl.BlockSpec(st_block, lambda b, i: (b, i, 0, 0, 0))],
        list(args) + [e_f, e_b, d_e, states_b],
        [jax.ShapeDtypeStruct((nb, t, SSD_WIDTH), F32)],
        [pl.BlockSpec((1, tile, SSD_WIDTH), lambda b, i: (b, i, 0))], scratch)
    return states, out


ML_GATE = 32
ML_ND = 2 * ML_HEADS
ML_AUG = ML_V_DIM + LANES


def _ml_gates(sm_ref, ib_ref, fb_ref, n, both):
    lane = lax.broadcasted_iota(jnp.int32, (1, LANES), 1)
    valid = (lane >= ML_GATE) & (lane < ML_GATE + ML_ND)
    is_f = lane < ML_GATE + ML_HEADS
    sm = sm_ref[0]
    li = sm + ib_ref[...]
    lf = pltpu.roll(_log_sigmoid(sm + fb_ref[...]), LANES - ML_ND, axis=1)
    parts = _split(jnp.where(valid, lf, 0.0), 2)
    upp = jnp.where(_causal(n, True), 1.0, 0.0).astype(BF16)
    bcum = sum(_dot(upp, p) for p in parts)
    if both:
        low = jnp.where(_causal(n, False), 1.0, 0.0).astype(BF16)
        bcum = jnp.where(is_f, sum(_dot(low, p) for p in parts), bcum)
    return valid, is_f, bcum, jnp.where(valid, li - bcum, 0.0)


def _ml_state_step(cn_ref, h, cn, keep, k, ws_dense, v_aug):
    w3 = jnp.concatenate([ws_dense.astype(BF16)] * (ML_AUG // LANES), axis=1)
    cn_ref[h] = keep * cn + _dot_tn(k, w3 * v_aug)


def _ml_v_aug(v_ref, h, n):
    return jnp.concatenate([v_ref[0, :, ML_V_DIM * h:ML_V_DIM * (h + 1)], jnp.ones((n, LANES), BF16)], axis=1)


def _ml_states_body(k_ref, v_ref, sm_ref, ib_ref, fb_ref, sel_ref, cn_out, m_out, cn_ref, m_ref):
    n = k_ref.shape[1]

    @pl.when(pl.program_id(1) == 0)
    def _init():
        cn_ref[...] = jnp.zeros_like(cn_ref)
        m_ref[...] = jnp.zeros_like(m_ref)

    cn_out[0, 0] = cn_ref[...].astype(cn_out.dtype)
    m_out[0, 0] = m_ref[...]
    yield
    valid, is_f, bcum, a = _ml_gates(sm_ref, ib_ref, fb_ref, n, False)
    m_prev = m_ref[0:1, :]
    g_last = jnp.maximum(m_prev, jnp.max(a, axis=0, keepdims=True))
    yield
    ws = _expand(jnp.exp(a - g_last), sel_ref[:, ML_HEADS * LANES:])
    keep = jnp.exp(m_prev - g_last)
    for h in range(ML_HEADS):
        lane_b = ML_GATE + ML_HEADS + h
        k = k_ref[0, :, ML_QK_DIM * h:ML_QK_DIM * (h + 1)]
        _ml_state_step(cn_ref, h, cn_ref[h], keep[:, lane_b:lane_b + 1], k,
                       ws[:, LANES * h:LANES * (h + 1)], _ml_v_aug(v_ref, h, n))
        yield
    m_ref[...] = jnp.broadcast_to(bcum[0:1, :] + g_last, m_ref.shape)


def _ml_out_body(qk_ref, v_ref, sm_ref, ib_ref, fb_ref, sel_ref, cnb_ref, mb_ref, o_ref, cn_ref, m_ref):
    n = qk_ref.shape[1]

    @pl.when(pl.program_id(1) == 0)
    def _init():
        cn_ref[...] = jnp.zeros_like(cn_ref)
        m_ref[...] = jnp.zeros_like(m_ref)

    head_q = lambda h: qk_ref[0, :, ML_QK_DIM * h:ML_QK_DIM * (h + 1)]
    head_k = lambda h: qk_ref[0, :, ML_QK_WIDTH + ML_QK_DIM * h:ML_QK_WIDTH + ML_QK_DIM * (h + 1)]
    qk_all = [_dot_nt(head_q(h), head_k(h)) for h in range(ML_HEADS)]
    inter_all = [_dot(head_q(h), jnp.concatenate([cn_ref[h].astype(BF16), cnb_ref[0, 0, h]], axis=1))
                 for h in range(ML_HEADS)]
    yield
    valid, is_f, bcum, a = _ml_gates(sm_ref, ib_ref, fb_ref, n, True)
    m_prev = jnp.where(is_f, m_ref[0:1, :], mb_ref[0, 0, 0:1, :])
    a_t = a.T
    pre = suf = a_t[ML_GATE:ML_GATE + ML_ND, :]
    pos = lax.broadcasted_iota(jnp.int32, (ML_ND, n), 1)
    k = 1
    while k < n:
        pre = jnp.maximum(pre, jnp.where(pos >= k, pltpu.roll(pre, k, axis=1), -jnp.inf))
        suf = jnp.maximum(suf, jnp.where(pos < n - k, pltpu.roll(suf, n - k, axis=1), -jnp.inf))
        k *= 2
    run = jnp.where(lax.broadcasted_iota(jnp.int32, (ML_ND, n), 0) < ML_HEADS, pre, suf)
    run = jnp.concatenate([jnp.zeros((ML_GATE, n), F32), run,
                           jnp.zeros((LANES - ML_GATE - ML_ND, n), F32)], axis=0).T
    g = jnp.maximum(m_prev, run)
    m_t = bcum + g
    yield
    floor = jnp.exp(-m_t)
    dense = lambda x, lane_: jnp.broadcast_to(x[:, lane_:lane_ + 1], (n, LANES))
    diag_t = jnp.exp(a - g).T
    g_last = g[n - 1:n, :]
    ws = _expand(jnp.exp(a - g_last), sel_ref[:, :ML_HEADS * LANES])
    keep = jnp.exp(m_prev - g_last)
    ti = lax.broadcasted_iota(jnp.int32, (n, n), 0)
    si = lax.broadcasted_iota(jnp.int32, (n, n), 1)
    not_above, above, on_diag = si <= ti, si > ti, si == ti
    wide = lambda x: jnp.concatenate([x] * (n // LANES), axis=1)
    for h in range(ML_HEADS):
        lanes = (ML_GATE + h, ML_GATE + ML_HEADS + h)
        g_d = [dense(g, ln) for ln in lanes]
        qk, inter = qk_all[h], inter_all[h]
        e = jnp.where(not_above, a_t[lanes[0]:lanes[0] + 1, :] - wide(g_d[0]),
                      a_t[lanes[1]:lanes[1] + 1, :] - wide(g_d[1]))
        p = qk * jnp.exp(e)
        p_f = jnp.where(not_above, p, 0.0).astype(BF16)
        p_b = jnp.where(above, p, jnp.where(on_diag, qk * diag_t[lanes[1]:lanes[1] + 1, :], 0.0)).astype(BF16)
        intra = _dot(jnp.concatenate([p_f, p_b], axis=0), _ml_v_aug(v_ref, h, n))
        out = None
        for d in range(2):
            w_inter = jnp.exp(m_prev[:, lanes[d]:lanes[d] + 1] - g_d[d])
            s = (intra[n * d:n * (d + 1)] + jnp.concatenate([w_inter] * (ML_AUG // LANES), axis=1)
                 * inter[:, ML_AUG * d:ML_AUG * (d + 1)])
            rn = 1.0 / jnp.maximum(jnp.abs(s[:, ML_V_DIM:]), dense(floor, lanes[d]))
            hid = s[:, :ML_V_DIM] * jnp.concatenate([rn] * (ML_V_DIM // LANES), axis=1)
            out = hid if out is None else out + hid
        o_ref[0, :, ML_V_DIM * h:ML_V_DIM * (h + 1)] = out
        yield
    for h in range(ML_HEADS):
        lane_f = ML_GATE + h
        _ml_state_step(cn_ref, h, cn_ref[h], keep[:, lane_f:lane_f + 1], head_k(h),
                       ws[:, LANES * h:LANES * (h + 1)], _ml_v_aug(v_ref, h, n))
        yield
    m_ref[...] = jnp.broadcast_to(m_t[n - 1:n, :], m_ref.shape)


def _mlstm_parts(cvm, p3, sm3, i_bias, f_bias, n_ctx):
    nb, t, _ = cvm.shape
    nt = t // TILE
    assert _SM_OFF["i_f"] == ML_GATE and _SM_OFF["f_f"] == ML_GATE + ML_ND
    row = lambda v, off: jnp.pad(v.reshape(1, -1), ((0, 0), (off, LANES - off - ML_ND)))
    sel = np.zeros((LANES, ML_ND * LANES), np.float32)
    for r in range(ML_ND):
        sel[ML_GATE + r, r * LANES:(r + 1) * LANES] = 1.0
    const = lambda b, i: (0, 0)
    specs = lambda order: [
        pl.BlockSpec((1, TILE, 2 * ML_QK_WIDTH), lambda b, i: (b, order(i), CV_QK // (2 * ML_QK_WIDTH))),
        pl.BlockSpec((1, TILE, ML_V_WIDTH), lambda b, i: (b, order(i), _P_OFF["m_v"] // ML_V_WIDTH)),
        pl.BlockSpec((1, TILE, LANES), lambda b, i: (b, order(i), 0)),
        pl.BlockSpec((1, LANES), const),
        pl.BlockSpec((1, LANES), const),
        pl.BlockSpec((LANES, ML_ND * LANES), const)]
    args = (cvm, p3, sm3, row(i_bias, ML_GATE), row(f_bias, ML_GATE + ML_ND), jnp.asarray(sel, BF16))
    cn_block = (1, 1, ML_HEADS, ML_QK_DIM, ML_AUG)
    m_block = (1, 1, SUBLANES, LANES)
    scratch = [pltpu.VMEM(cn_block[2:], F32), pltpu.VMEM(m_block[2:], F32)]
    bwd = _tile_order(nt, True, n_ctx // TILE)
    fwd = _tile_order(nt, False)
    st_idx = lambda b, i: (b, bwd(i)) + (0,) * 3
    k_spec = pl.BlockSpec((1, TILE, ML_QK_WIDTH), lambda b, i: (b, bwd(i), (CV_QK + ML_QK_WIDTH) // ML_QK_WIDTH))
    states = (_ml_states_body, [k_spec] + specs(bwd)[1:], list(args),
              [jax.ShapeDtypeStruct((nb, nt) + cn_block[2:], BF16),
               jax.ShapeDtypeStruct((nb, nt) + m_block[2:], F32)],
              [pl.BlockSpec(cn_block, st_idx), pl.BlockSpec(m_block, lambda b, i: (b, bwd(i), 0, 0))],
              scratch)
    out = lambda cn_b, m_b: (
        _ml_out_body,
        specs(fwd) + [pl.BlockSpec(cn_block, lambda b, i: (b, i, 0, 0, 0)),
                      pl.BlockSpec(m_block, lambda b, i: (b, i, 0, 0))],
        list(args) + [cn_b, m_b],
        [jax.ShapeDtypeStruct((nb, t, ML_V_WIDTH), F32)],
        [pl.BlockSpec((1, TILE, ML_V_WIDTH), lambda b, i: (b, i, 0))], scratch)
    return states, out


GLA_SUB = 256
GLA_NCH = GLA_SUB // GLA_CHUNK
GLA_COLS = BF16_SUBLANES


def _gla_layout(is_ctx, lat_rows):
    r = np.arange(GLA_SUB)
    if is_ctx:
        return r // GLA_CHUNK, r % GLA_CHUNK
    col = r % SUBLANES
    cpc = GLA_CHUNK // lat_rows
    return col // cpc, (col % cpc) * lat_rows + r // SUBLANES


def _gla_consts(is_ctx, lat_rows, rev):
    ch, pos = _gla_layout(is_ctx, lat_rows)
    same = ch[:, None] == ch[None, :]
    before = (pos[None, :] >= pos[:, None]) if rev else (pos[None, :] <= pos[:, None])
    tri = (same & before).astype(np.float32)
    cmask = np.stack([np.repeat((ch == j)[:, None], LANES, axis=1) for j in range(GLA_NCH)])
    return tri, cmask.astype(np.float32)


def _gla_row(is_ctx, lat_rows, j, p):
    ch, pos = _gla_layout(is_ctx, lat_rows)
    return int(np.nonzero((ch == j) & (pos == p))[0][0])


def _per_chunk_rows(b, is_ctx, lat_rows, p):
    rows = [b[_gla_row(is_ctx, lat_rows, j, p):_gla_row(is_ctx, lat_rows, j, p) + 1, :]
            for j in range(GLA_NCH)]
    w = b.shape[1]
    if is_ctx:
        full = jnp.concatenate([jnp.broadcast_to(r, (GLA_CHUNK, w)) for r in rows], axis=0)
    else:
        rep = SUBLANES // GLA_NCH
        pat = jnp.concatenate([jnp.broadcast_to(r, (rep, w)) for r in rows], axis=0)
        full = jnp.broadcast_to(pat[None], (GLA_SUB // SUBLANES, SUBLANES, w)).reshape(GLA_SUB, w)
    return rows, full


def _gla_sub(q, k, v, araw, aup, abias, tri_b, tri_f, cmask_ref, st_ref, store, *, rev, is_ctx, lat_rows):
    want_out = store is not None
    x = _dot(araw.astype(BF16), aup) + abias
    g = _log_sigmoid(x) * (1.0 / GLA_TAU)
    b = _dot_exact_lhs(tri_b, g, 2)
    yield
    lasts, last = _per_chunk_rows(b, is_ctx, lat_rows, 0 if rev else GLA_CHUNK - 1)
    kl = (k * jnp.exp(last - b)).astype(BF16)
    if want_out:
        _, ref = _per_chunk_rows(b, is_ctx, lat_rows, GLA_CHUNK // 2)
        qs = q * (GLA_K_DIM ** -0.5)
        qe = (qs * jnp.exp(b - ref)).astype(BF16)
        ke = (k * jnp.exp(ref - b)).astype(BF16)
        qb = (qs * jnp.exp(b)).astype(BF16)
        visible = tri_f > 0.0
    yield
    order = range(GLA_NCH - 1, -1, -1) if rev else range(GLA_NCH)
    outs = []
    for h in range(GLA_HEADS):
        ks = slice(GLA_K_DIM * h, GLA_K_DIM * (h + 1))
        vh = v[:, GLA_V_DIM * h:GLA_V_DIM * (h + 1)]
        klm = jnp.concatenate([kl[:, ks] * cmask_ref[j] for j in range(GLA_NCH)], axis=1)
        upd = _dot_tn(vh, klm)
        s = st_ref[h]
        s_in = [None] * GLA_NCH
        for j in order:
            s_in[j] = s.astype(BF16)
            s = s * jnp.exp(lasts[j][:, ks]) + upd[:, GLA_K_DIM * j:GLA_K_DIM * (j + 1)]
        st_ref[h] = s
        if want_out:
            att = jnp.where(visible, _dot_nt(qe[:, ks], ke[:, ks]), 0.0).astype(BF16)
            qbm = jnp.concatenate([qb[:, ks] * cmask_ref[j] for j in range(GLA_NCH)], axis=1)
            outs.append(_dot(att, vh) + _dot_nt(qbm, jnp.concatenate(s_in, axis=1)))
        yield
    if want_out:
        store(jnp.concatenate(outs, axis=1))


def _gla_ctx_body(qc_ref, kc_ref, vc_ref, sc_ref, ql_ref, kl_ref, vl_ref, sl_ref,
                  aup_ref, ab_ref, tcb_ref, tcf_ref, cmc_ref, tlb_ref, tlf_ref, cml_ref,
                  o_ref, st_ref, ctxo_ref, *, rev, ctx_rows, lat_rows, n_cblk, ctx_out):
    st_ref[...] = jnp.zeros_like(st_ref)

    def store(o):
        ctxo_ref[...] = o

    yield from _gla_sub(qc_ref[0].astype(F32), kc_ref[0].astype(F32), vc_ref[0], sc_ref[0],
                        aup_ref[...], ab_ref[...], tcb_ref[...], tcf_ref[...], cmc_ref, st_ref,
                        store if ctx_out else None, rev=rev, is_ctx=True, lat_rows=lat_rows)


def _gla_lat_body(qc_ref, kc_ref, vc_ref, sc_ref, ql_ref, kl_ref, vl_ref, sl_ref,
                  aup_ref, ab_ref, tcb_ref, tcf_ref, cmc_ref, tlb_ref, tlf_ref, cml_ref,
                  o_ref, st_ref, ctxo_ref, *, rev, ctx_rows, lat_rows, n_cblk, ctx_out):
    i = pl.program_id(1)
    r0, r1 = ctx_rows, ctx_rows + lat_rows
    cblk = (n_cblk - i) if rev else (i - 1)
    halves = range(GLA_COLS // SUBLANES)
    for half in (reversed(halves) if rev else halves):
        cs = slice(SUBLANES * half, SUBLANES * (half + 1))
        take = lambda r: r[0, r0:r1].astype(F32)[:, cs, :].reshape(GLA_SUB, r.shape[-1])

        def store(o, cs=cs):
            o_ref[0, r0:r1, cs, :] = o.reshape(lat_rows, SUBLANES, GLA_V_WIDTH)

        yield from _gla_sub(take(ql_ref), take(kl_ref), take(vl_ref).astype(BF16), take(sl_ref),
                            aup_ref[...], ab_ref[...], tlb_ref[...], tlf_ref[...], cml_ref, st_ref,
                            store, rev=rev, is_ctx=False, lat_rows=lat_rows)
    for r in range(ctx_rows):
        if ctx_out:
            start = pl.multiple_of(r * GRID_W + cblk * GLA_COLS, GLA_COLS)
            o_ref[0, r, :, :] = ctxo_ref[pl.ds(start, GLA_COLS), :]
        else:
            o_ref[0, r, :, :] = jnp.zeros((GLA_COLS, GLA_V_WIDTH), F32)


GLA_PHASES = ((lambda i: i == 0, "ctx"), (lambda i: i > 0, "lat"))


def _gla_part(p3, sm3, a_up, a_bias, n_ctx, rev, ctx_out):
    nb, t, ncol = p3.shape
    rows = t // GRID_W
    ctx_rows = n_ctx // GRID_W
    lat_rows = rows - ctx_rows
    n_cblk = GRID_W // GLA_COLS
    p4 = p3.reshape(nb, rows, GRID_W, ncol)
    sm4 = sm3.reshape(nb, rows, GRID_W, LANES)
    a_off = _SM_OFF["a_b"] if rev else _SM_OFF["a_f"]
    aup = jnp.pad(a_up, ((a_off, LANES - a_off - GLA_RANK), (0, 0))).astype(BF16)
    cblk = lambda i: jnp.where(i == 0, n_cblk - 1 if rev else 0, (n_cblk - i) if rev else (i - 1))
    ctx = lambda blk: (lambda b, i: (b, 0, blk))
    lat = lambda blk: (lambda b, i: (b, 0, cblk(i), blk))
    const2 = lambda b, i: (0, 0)
    const3 = lambda b, i: (0, 0, 0)
    widths = (GLA_K_WIDTH, GLA_K_WIDTH, GLA_V_WIDTH)
    offs = (_P_OFF["g_q"], _P_OFF["g_k"], _P_OFF["g_v"])
    consts = []
    const_specs = []
    for is_ctx in (True, False):
        tri, cmask = _gla_consts(is_ctx, lat_rows, rev)
        consts += [jnp.asarray(tri, BF16), jnp.asarray(tri, F32), jnp.asarray(cmask, BF16)]
        const_specs += [pl.BlockSpec((GLA_SUB, GLA_SUB), const2), pl.BlockSpec((GLA_SUB, GLA_SUB), const2),
                        pl.BlockSpec((GLA_NCH, GLA_SUB, LANES), const3)]
    out_spec = pl.BlockSpec((1, rows, GLA_COLS, GLA_V_WIDTH), lat(0))
    in_specs = ([pl.BlockSpec((1, n_ctx, w), ctx(o // w)) for w, o in zip(widths, offs)]
                + [pl.BlockSpec((1, n_ctx, LANES), ctx(0))]
                + [pl.BlockSpec((1, rows, GLA_COLS, w), lat(o // w)) for w, o in zip(widths, offs)]
                + [pl.BlockSpec((1, rows, GLA_COLS, LANES), lat(0))]
                + [pl.BlockSpec((LANES, GLA_K_WIDTH), const2), pl.BlockSpec((1, GLA_K_WIDTH), const2)]
                + const_specs)
    args = [p3, p3, p3, sm3, p4, p4, p4, sm4, aup, a_bias.reshape(1, GLA_K_WIDTH)] + consts
    static = dict(rev=rev, ctx_rows=ctx_rows, lat_rows=lat_rows, n_cblk=n_cblk, ctx_out=ctx_out)
    bodies = {"ctx": functools.partial(_gla_ctx_body, **static),
              "lat": functools.partial(_gla_lat_body, **static)}
    return (bodies, in_specs, args,
            [jax.ShapeDtypeStruct((nb, rows, GRID_W, GLA_V_WIDTH), F32)], [out_spec],
            [pltpu.VMEM((GLA_HEADS, GLA_V_DIM, GLA_K_DIM), F32), pltpu.VMEM((n_ctx, GLA_V_WIDTH), F32)])


def _group_rmsnorm(y, groups):
    width = y.shape[-1] // groups
    ones = jnp.ones((width, LANES), BF16)
    out = []
    for g in range(groups):
        yg = y[:, width * g:width * (g + 1)]
        ms = _dot((yg * yg).astype(BF16), ones) * (1.0 / width)
        out.append(yg * jnp.concatenate([lax.rsqrt(ms + EPS)] * (width // LANES), axis=1))
    return jnp.concatenate(out, axis=1)


def _cast_weights_once(pairs):
    @pl.when((pl.program_id(0) == 0) & (pl.program_id(1) == 0))
    def _cast():
        for w_ref, s_ref in pairs:
            s_ref[...] = w_ref[...].astype(s_ref.dtype)


def _post_kernel(x_ref, y_ref, h_ref, of_ref, ob_ref, z_ref, mo_ref, gg_ref, gs_ref, gm_ref, gl_ref,
                 nws_ref, nwm_ref, nwg_ref, wbs32_ref, wbm32_ref, wbg32_ref, wout32_ref, g1_ref,
                 nwf_ref, sc2_ref, sh2_ref, xo_ref, ho_ref, wbs_ref, wbm_ref, wbg_ref, wout_ref):
    _cast_weights_once(((wbs32_ref, wbs_ref), (wbm32_ref, wbm_ref), (wbg32_ref, wbg_ref), (wout32_ref, wout_ref)))
    y_ssd = (_group_rmsnorm(y_ref[0] * _silu(z_ref[0]), 2) * nws_ref[...]).astype(BF16)
    y_ml = (_group_rmsnorm(h_ref[0], ML_HEADS) * nwm_ref[...]).astype(BF16) * _sigmoid(mo_ref[0])
    y_gla = (_group_rmsnorm(of_ref[0] + ob_ref[0], GLA_HEADS) * nwg_ref[...]).astype(BF16) * _silu(gg_ref[0])
    merged = (_sigmoid(gs_ref[0]) * _dot(y_ssd, wbs_ref[...])
              + _sigmoid(gm_ref[0]) * _dot(y_ml, wbm_ref[...])
              + _sigmoid(gl_ref[0]) * _dot(y_gla, wbg_ref[...]))
    x_new = x_ref[0] + g1_ref[0] * _dot(merged.astype(BF16), wout_ref[...])
    xo_ref[0] = x_new
    ho_ref[0] = (_rms(x_new) * nwf_ref[...] * (1.0 + sc2_ref[0]) + sh2_ref[0]).astype(ho_ref.dtype)


def _post(x, scans, p3, norm_ws, w_bs, w_out, layer, norm_ffn_w, mods, n_ctx, tile0):
    nb, t, d = x.shape
    nt = t // TILE - tile0
    ctx_tiles = n_ctx // TILE
    tok = lambda blk: (lambda b, i: (b, i + tile0, blk))
    out = lambda b, i: (b, i, 0)
    const = lambda b, i: (0, 0)
    tok_spec = lambda blk: pl.BlockSpec((1, TILE, d), tok(blk))
    w_spec = pl.BlockSpec((None, d, d), lambda b, i: (layer, 0, 0), pipeline_mode=pl.Buffered(1))
    vec = pl.BlockSpec((1, d), const)
    mod = lambda k: pl.BlockSpec((1, 1, d), _mod_row(nb, k, ctx_tiles, tile0))
    names = ("s_z", "m_o", "g_g", "gate_ssd", "gate_ml", "gate_gla")
    in_specs = ([tok_spec(0)] * (1 + len(scans)) + [tok_spec(_P_OFF[nm] // d) for nm in names]
                + [vec] * 3 + [w_spec] * 4 + [mod(2), vec, mod(4), mod(3)])
    return pl.pallas_call(
        _post_kernel,
        out_shape=(jax.ShapeDtypeStruct((nb, nt * TILE, d), F32),
                   jax.ShapeDtypeStruct((nb, nt * TILE, d), BF16)),
        grid=(nb, nt),
        in_specs=in_specs,
        out_specs=(pl.BlockSpec((1, TILE, d), out), pl.BlockSpec((1, TILE, d), out)),
        scratch_shapes=[pltpu.VMEM((d, d), BF16)] * 4,
        compiler_params=_cparams(("arbitrary", "arbitrary")),
        name="post",
    )(x, *scans, *([p3] * 6), *[w.reshape(1, d) for w in norm_ws], *w_bs, w_out, mods,
      norm_ffn_w.reshape(1, d), mods, mods)


def _ffn_in_kernel(a_ref, wg_ref, wu_ref, o_ref, w_ref):
    half = o_ref.shape[1]

    @pl.when(pl.program_id(1) == 0)
    def _cast():
        w_ref[:, :half] = wg_ref[...].astype(w_ref.dtype)
        w_ref[:, half:] = wu_ref[...].astype(w_ref.dtype)

    acc = _dot(a_ref[...], w_ref[...])
    o_ref[...] = (_silu(acc[:, :half]) * acc[:, half:]).astype(o_ref.dtype)


def _ffn_in(h, w_ffn_in, layer, half):
    m, k = h.shape
    n_half = w_ffn_in.shape[-1] // 2
    tm = _row_tile(m)
    return pl.pallas_call(
        _ffn_in_kernel,
        out_shape=jax.ShapeDtypeStruct((m, n_half), BF16),
        grid=(n_half // half, m // tm),
        in_specs=[pl.BlockSpec((tm, k), lambda j, i: (i, 0)),
                  pl.BlockSpec((None, k, half), lambda j, i: (layer, 0, j)),
                  pl.BlockSpec((None, k, half), lambda j, i: (layer, 0, j + n_half // half))],
        out_specs=pl.BlockSpec((tm, half), lambda j, i: (i, j)),
        scratch_shapes=[pltpu.VMEM((k, 2 * half), BF16)],
        compiler_params=_cparams(("arbitrary", "arbitrary")),
        name="ffn_in",
    )(h, w_ffn_in, w_ffn_in)


def _ffn_out_kernel(a_ref, w32_ref, x_ref, g_ref, nw_ref, sc_ref, sh_ref, xo_ref, ho_ref, w_ref):
    _cast_weights_once(((w32_ref, w_ref),))
    x_new = x_ref[0] + g_ref[0] * _dot(a_ref[0], w_ref[...])
    xo_ref[0] = x_new
    ho_ref[0] = (_rms(x_new) * nw_ref[...] * (1.0 + sc_ref[0]) + sh_ref[0]).astype(ho_ref.dtype)


def _ffn_out_last_kernel(a_ref, w32_ref, x_ref, g_ref, nw_ref, o_ref, w_ref):
    _cast_weights_once(((w32_ref, w_ref),))
    x_new = x_ref[0] + g_ref[0] * _dot(a_ref[0], w_ref[...])
    o_ref[0] = _rms(x_new) * nw_ref[...]


def _ffn_out(a, w, layer, x, mods, n_ctx, tile0, next_norm_w, next_mods):
    nb, t, d = x.shape
    k = a.shape[-1]
    ctx_tiles = n_ctx // TILE
    tok = lambda b, i: (b, i, 0)
    mod = lambda k_: pl.BlockSpec((1, 1, d), _mod_row(nb, k_, ctx_tiles, tile0))
    in_specs = [pl.BlockSpec((1, TILE, k), tok),
                pl.BlockSpec((None, k, d), lambda b, i: (layer, 0, 0), pipeline_mode=pl.Buffered(1)),
                pl.BlockSpec((1, TILE, d), tok),
                mod(5),
                pl.BlockSpec((1, d), lambda b, i: (0, 0))]
    args = [a, w, x, mods, next_norm_w.reshape(1, d)]
    if next_mods is None:
        body = _ffn_out_last_kernel
        out_shape = jax.ShapeDtypeStruct((nb, t, d), F32)
        out_specs = pl.BlockSpec((1, TILE, d), tok)
    else:
        body = _ffn_out_kernel
        in_specs += [mod(1), mod(0)]
        args += [next_mods, next_mods]
        out_shape = (jax.ShapeDtypeStruct((nb, t, d), F32), jax.ShapeDtypeStruct((nb, t, d), BF16))
        out_specs = (pl.BlockSpec((1, TILE, d), tok), pl.BlockSpec((1, TILE, d), tok))
    return pl.pallas_call(
        body,
        out_shape=out_shape,
        grid=(nb, t // TILE),
        in_specs=in_specs,
        out_specs=out_specs,
        scratch_shapes=[pltpu.VMEM((k, d), BF16)],
        compiler_params=_cparams(("arbitrary", "arbitrary")),
        name="ffn_out",
    )(*args)


def _proj_weights(w_in):
    cols = lambda names: [w_in[..., _IN_OFF[nm]:_IN_OFF[nm] + _IN_W[nm]] for nm in names]
    main = jnp.concatenate(cols(_P_ORDER), axis=-1)
    small = jnp.pad(jnp.concatenate(cols(_SMALL), axis=-1), ((0, 0), (0, 0), (0, LANES - N_SMALL_USED)))
    return main.astype(BF16), small.astype(BF16)


def kernel(x, c, ctx, c_ctx, w_mod, b_mod, norm_mix_w, norm_ffn_w, w_in, ssd_conv_w, ssd_conv_b, ssd_dt_bias, ssd_a_log, ssd_d, ssd_norm_w, ml_conv_w, ml_conv_b, ml_i_bias, ml_f_bias, ml_norm_w, gla_a_up, gla_a_bias, gla_norm_w, w_b_ssd, w_b_ml, w_b_gla, w_out, w_ffn_in, w_ffn_out, final_norm_w):
    nb, n_lat, d = x.shape
    n_ctx = ctx.shape[1]
    t = n_ctx + n_lat
    depth = w_in.shape[0]
    d_ff = w_ffn_out.shape[1]
    assert n_ctx == TILE == GLA_SUB and n_lat % TILE == 0 and n_lat // GRID_W == 32
    ffn_half = d_ff // 2

    c16 = jnp.pad(jnp.concatenate([c, c_ctx[None]], axis=0), ((0, 2 * SUBLANES - nb - 1), (0, 0)))
    mods = [_modulation(c16, w_mod, b_mod, l).reshape(2 * SUBLANES * 6, 1, d) for l in range(depth)]
    xs, h = _norm_mod(ctx, x, norm_mix_w[0], mods[0], 0, 1)
    conv_w = jnp.concatenate([ssd_conv_w, ml_conv_w], axis=-1)
    conv_b = jnp.concatenate([ssd_conv_b, ml_conv_b], axis=-1)
    conv_post = jnp.concatenate([jnp.ones((ssd_conv_w.shape[-1] + ML_QK_WIDTH,), F32),
                                 jnp.full((ML_QK_WIDTH,), ML_QK_DIM ** -0.5, F32)])
    w_main, w_small = _proj_weights(w_in)
    w_bs = (w_b_ssd, w_b_ml, w_b_gla)
    for l in range(depth):
        last = l == depth - 1
        h2d = h.reshape(nb * t, d)
        tn = N_PROJ // 5
        conv_tile = _P_OFF["s_x"] // tn
        assert conv_tile * tn == _P_OFF["s_x"] and N_PROJ - _P_OFF["s_x"] == tn == CV_WIDTH
        m_tiles = nb * t // _row_tile(nb * t)
        (pc,), (sm,) = _run_together("proj_first", (1, m_tiles),
                                     _mm_part(h2d, w_main, l, tn, BF16, conv_tile, 1),
                                     _mm_part(h2d, w_small, l, LANES, F32, 0, 1))
        pc3, sm3 = pc.reshape(nb, t, tn), sm.reshape(nb, t, LANES)
        grid = (conv_tile, m_tiles)
        (p,), (cv,) = _run_together(
            "proj_conv", grid, _mm_part(h2d, w_main, l, tn, BF16, 0, conv_tile),
            _conv_part(pc3, conv_w[l], conv_b[l], conv_post, n_ctx, grid))
        p3 = p.reshape(nb, t, conv_tile * tn)
        d_e = jnp.repeat(ssd_d[l], SSD_HEAD_DIM).reshape(1, SSD_WIDTH)
        ssd_states, ssd_out = _ssd_parts(cv, sm3, ssd_dt_bias[l], ssd_a_log[l], d_e, n_ctx)
        ml_states, ml_out = _mlstm_parts(cv, p3, sm3, ml_i_bias[l], ml_f_bias[l], n_ctx)
        grid = (nb, t // TILE)
        ssd_st, ml_st = _run_together("bwd_states", grid, ssd_states, ml_states)
        (y,), (hm,) = _run_together("ssd_mlstm", grid, ssd_out(*ssd_st), ml_out(*ml_st))
        gla = [_gla_part(p3, sm3, gla_a_up[l, k], gla_a_bias[l, k], n_ctx, bool(k), not last) for k in range(2)]
        (og_f,), (og_b,) = _run_together("gla", (nb, GRID_W // GLA_COLS + 1), *gla, phases=GLA_PHASES)
        og_f, og_b = og_f.reshape(nb, t, GLA_V_WIDTH), og_b.reshape(nb, t, GLA_V_WIDTH)
        tile0 = n_ctx // TILE if last else 0
        xs, h2 = _post(xs, (y, hm, og_f, og_b), p3, (ssd_norm_w[l], ml_norm_w[l], gla_norm_w[l]),
                       w_bs, w_out, l, norm_ffn_w[l], mods[l], n_ctx, tile0)
        nt = xs.shape[1]
        a = _ffn_in(h2.reshape(nb * nt, d), w_ffn_in, l, ffn_half)
        a = a.reshape(nb, nt, d_ff)
        if last:
            return _ffn_out(a, w_ffn_out, l, xs, mods[l], n_ctx, tile0, final_norm_w, None)
        xs, h = _ffn_out(a, w_ffn_out, l, xs, mods[l], n_ctx, tile0,
                         norm_mix_w[l + 1], mods[l + 1])
```

```python
import functools

import numpy as np
import jax
import jax.numpy as jnp
from jax import lax
from jax.experimental import pallas as pl
from jax.experimental.pallas import tpu as pltpu

F32 = jnp.float32
BF16 = jnp.bfloat16

EPS = 1e-6
LOG2E = 1.4426950408889634
GRID_W = 64
SSD_HEADS = 16
SSD_HEAD_DIM = 64
SSD_WIDTH = 1024
SSD_STATE = 64
SSD_BC = 128
ML_HEADS = 4
ML_QK_DIM = 128
ML_V_DIM = 256
ML_QK_WIDTH = 512
ML_V_WIDTH = 1024
GLA_HEADS = 4
GLA_K_DIM = 128
GLA_V_DIM = 256
GLA_K_WIDTH = 512
GLA_V_WIDTH = 1024
GLA_RANK = 16
GLA_TAU = 16.0
GLA_CHUNK = 64

LANES = 128
SUBLANES = 8
BF16_SUBLANES = 16
VMEM_LIMIT = 56 * 1024 * 1024

TILE = 256

_IN_NAMES = ("s_x", "s_z", "s_b", "s_c", "dt_f", "dt_b",
             "m_q", "m_k", "m_v", "m_o", "i_f", "i_b", "f_f", "f_b",
             "g_q", "g_k", "g_v", "g_g", "a_f", "a_b",
             "gate_ssd", "gate_ml", "gate_gla")
_IN_WIDTHS = (1024, 1024, 128, 128, 16, 16,
              512, 512, 1024, 1024, 4, 4, 4, 4,
              512, 512, 1024, 1024, 16, 16,
              1024, 1024, 1024)
_IN_OFF = dict(zip(_IN_NAMES, np.concatenate([[0], np.cumsum(_IN_WIDTHS)[:-1]]).tolist()))
_IN_W = dict(zip(_IN_NAMES, _IN_WIDTHS))

_P_ORDER = ("s_z", "m_o", "g_g", "gate_ssd", "gate_ml", "gate_gla", "m_v", "g_v",
            "g_q", "g_k", "s_x", "s_b", "s_c", "m_q", "m_k")
_P_OFF = {}
_o = 0
for _n in _P_ORDER:
    _P_OFF[_n] = _o
    _o += _IN_W[_n]
N_PROJ = _o
_SMALL = ("dt_f", "dt_b", "i_f", "i_b", "f_f", "f_b", "a_f", "a_b")
_SM_OFF = {}
_s = 0
for _n in _SMALL:
    _SM_OFF[_n] = _s
    _s += _IN_W[_n]
N_SMALL_USED = _s


def _cparams(sem, fusible_inputs=None):
    return pltpu.CompilerParams(dimension_semantics=sem, vmem_limit_bytes=VMEM_LIMIT,
                                allow_input_fusion=fusible_inputs)


def _sigmoid(x):
    return 0.5 * jnp.tanh(0.5 * x) + 0.5


def _silu(x):
    h = 0.5 * x
    return h + h * jnp.tanh(h)


def _softplus(x):
    return jnp.maximum(x, 0.0) + jnp.log1p(jnp.exp(-jnp.abs(x)))


def _log_sigmoid(x):
    return jnp.minimum(x, 0.0) - jnp.log(1.0 + jnp.exp(-jnp.abs(x)))


def _split(x, n):
    out = []
    r = x
    for _ in range(n):
        p = r.astype(BF16)
        out.append(p)
        r = r - p.astype(F32)
    return out


def _dot(a, b):
    return jnp.dot(a, b, preferred_element_type=F32)


def _dot_nt(a, b):
    return lax.dot_general(a, b, (((1,), (1,)), ((), ())), preferred_element_type=F32)


def _dot_tn(a, b):
    return lax.dot_general(a, b, (((0,), (0,)), ((), ())), preferred_element_type=F32)


def _dot_exact_lhs(t, x, pieces):
    return sum(_dot(t, p) for p in _split(x, pieces))


def _dot_hp(a, b):
    ah, am = _split(a, 2)
    bh, bm = _split(b, 2)
    return _dot(ah, bh) + _dot(ah, bm) + _dot(am, bh)


def _causal(n, rev):
    t = lax.broadcasted_iota(jnp.int32, (n, n), 0)
    s = lax.broadcasted_iota(jnp.int32, (n, n), 1)
    return (s >= t) if rev else (s <= t)


def _rms(x):
    return x * lax.rsqrt(jnp.mean(x * x, axis=-1, keepdims=True) + EPS)


def _mod_kernel(c_ref, w_ref, b_ref, o_ref):
    o_ref[...] = _dot_hp(_silu(c_ref[...]), w_ref[...]) + b_ref[...]


def _modulation(c16, w_mod, b_mod, layer):
    rows, d = c16.shape
    n = w_mod.shape[-1]
    tn = 1536
    return pl.pallas_call(
        _mod_kernel,
        out_shape=jax.ShapeDtypeStruct((rows, n), F32),
        grid=(n // tn,),
        in_specs=[pl.BlockSpec((rows, d), lambda j: (0, 0)),
                  pl.BlockSpec((None, d, tn), lambda j: (layer, 0, j)),
                  pl.BlockSpec((None, 1, tn), lambda j: (layer, 0, j))],
        out_specs=pl.BlockSpec((rows, tn), lambda j: (0, j)),
        compiler_params=_cparams(("arbitrary",)),
        name="modulation",
    )(c16, w_mod, b_mod.reshape(b_mod.shape[0], 1, n))


def _mod_row(nb, k, ctx_tiles, tile0=0):
    return lambda b, i: (jnp.where(i + tile0 < ctx_tiles, nb, b) * 6 + k, 0, 0)


def _norm_mod_kernel(ctx_ref, x_ref, w_ref, sc_ref, sh_ref, xo_ref, ho_ref):
    x = jnp.where(pl.program_id(1) == 0, ctx_ref[0], x_ref[0])
    xo_ref[0] = x
    ho_ref[0] = (_rms(x) * w_ref[...] * (1.0 + sc_ref[0]) + sh_ref[0]).astype(ho_ref.dtype)


def _norm_mod(ctx, x, w, mods, k_shift, k_scale):
    nb, n_lat, d = x.shape
    n_ctx = ctx.shape[1]
    assert n_ctx == TILE
    t = n_ctx + n_lat
    tok = lambda b, i: (b, i, 0)
    return pl.pallas_call(
        _norm_mod_kernel,
        out_shape=(jax.ShapeDtypeStruct((nb, t, d), F32), jax.ShapeDtypeStruct((nb, t, d), BF16)),
        grid=(nb, t // TILE),
        in_specs=[pl.BlockSpec((1, TILE, d), lambda b, i: (b, 0, 0)),
                  pl.BlockSpec((1, TILE, d), lambda b, i: (b, jnp.maximum(i - 1, 0), 0)),
                  pl.BlockSpec((1, d), lambda b, i: (0, 0)),
                  pl.BlockSpec((1, 1, d), _mod_row(nb, k_scale, n_ctx // TILE)),
                  pl.BlockSpec((1, 1, d), _mod_row(nb, k_shift, n_ctx // TILE))],
        out_specs=(pl.BlockSpec((1, TILE, d), tok), pl.BlockSpec((1, TILE, d), tok)),
        compiler_params=_cparams(("parallel", "arbitrary")),
        name="norm_mod",
    )(ctx, x, w.reshape(1, d), mods, mods)


MM_STAGE = 256


def _mm_body(a_ref, w_ref, o_ref):
    a = a_ref[...]
    width = o_ref.shape[1]
    stage = min(MM_STAGE, width)
    for c0 in range(0, width, stage):
        o_ref[:, c0:c0 + stage] = _dot(a, w_ref[:, c0:c0 + stage]).astype(o_ref.dtype)
        yield


def _row_tile(m):
    return 512 if m % 512 == 0 else TILE


def _mm_part(a, w, layer, tn, out_dtype, tile0, n_tiles):
    m, k = a.shape
    tm = _row_tile(m)
    return (_mm_body,
            [pl.BlockSpec((tm, k), lambda j, i: (i, 0)),
             pl.BlockSpec((None, k, tn), lambda j, i: (layer, 0, j + tile0))],
            [a, w], [jax.ShapeDtypeStruct((m, n_tiles * tn), out_dtype)],
            [pl.BlockSpec((tm, tn), lambda j, i: (i, j))], [])


CONV_K = 5
CONV_ROWS = 128
CONV_X = SSD_WIDTH // LANES
CONV_BC = 2 * SSD_BC // LANES
CONV_QK = 2 * ML_QK_WIDTH // LANES
CV_QK = SSD_WIDTH
CV_BC = SSD_WIDTH + 2 * ML_QK_WIDTH
CV_WIDTH = CV_BC + 2 * SSD_BC


def _conv_body(u_ref, w_ref, b_ref, s_ref, o_ref, pad_ref, *, n_ctx):
    t, c = u_ref.shape[1], u_ref.shape[2]
    half = CONV_K // 2
    zeros = jnp.zeros((SUBLANES, c), F32)
    w = w_ref[...]
    bias = b_ref[...]
    post = s_ref[...]
    for s0, n in ((0, n_ctx), (n_ctx, t - n_ctx)):
        pad_ref[0:SUBLANES, :] = zeros
        pad_ref[SUBLANES:SUBLANES + n, :] = u_ref[0, s0:s0 + n, :].astype(F32)
        pad_ref[SUBLANES + n:2 * SUBLANES + n, :] = zeros
        for r0 in range(0, n, CONV_ROWS):
            acc = bias
            for j in range(CONV_K):
                lo = SUBLANES - half + j + r0
                acc = acc + w[j:j + 1, :] * pad_ref[lo:lo + CONV_ROWS, :]
            o_ref[0, s0 + r0:s0 + r0 + CONV_ROWS, :] = (_silu(acc) * post).astype(o_ref.dtype)
            if (r0 // CONV_ROWS) % 4 == 3:
                yield


def _conv_part(u3, w, b, post_scale, n_ctx, grid):
    nb, t, width = u3.shape
    n_cb = width // LANES
    steps = grid[0] * grid[1]
    rep = steps // (nb * n_cb)
    assert steps == rep * nb * n_cb and n_cb == CONV_X + CONV_BC + CONV_QK
    blk = lambda j, i: (j * grid[1] + i) // rep
    chan = lambda j, i: blk(j, i) % n_cb
    out_chan = lambda c: jnp.where(c < CONV_X, c, jnp.where(c < CONV_X + CONV_BC, c + CONV_QK, c - CONV_BC))
    vec = lambda rows: pl.BlockSpec((rows, LANES), lambda j, i: (0, chan(j, i)))
    return (functools.partial(_conv_body, n_ctx=n_ctx),
            [pl.BlockSpec((1, t, LANES), lambda j, i: (blk(j, i) // n_cb, 0, chan(j, i))),
             vec(CONV_K), vec(1), vec(1)],
            [u3, w, b.reshape(1, width), post_scale.reshape(1, width)],
            [jax.ShapeDtypeStruct((nb, t, width), BF16)],
            [pl.BlockSpec((1, t, LANES), lambda j, i: (blk(j, i) // n_cb, 0, out_chan(chan(j, i))))],
            [pltpu.VMEM((t + 2 * SUBLANES, LANES), F32)])


def _tile_order(n_tiles, rev, ctx_tiles=1):
    if rev:
        return lambda i: jnp.where(i < ctx_tiles, ctx_tiles - 1 - i, n_tiles - 1 + ctx_tiles - i)
    return lambda i: i


SSD_PAIRS = SSD_HEADS // 2
SSD_B_OFF = SSD_HEADS
SSD_TILE = 256


def _ssd_gates(sm_ref, dtb_ref, alog_ref, n, both):
    lane = lax.broadcasted_iota(jnp.int32, (1, LANES), 1)
    dt = _softplus(sm_ref[0] + dtb_ref[...])
    la = dt * jnp.where(lane < 2 * SSD_HEADS, -jnp.exp(alog_ref[...]), 0.0)
    parts = _split(la, 2)
    upp = jnp.where(_causal(n, True), 1.0, 0.0).astype(BF16)
    cum = sum(_dot(upp, p) for p in parts)
    if both:
        low = jnp.where(_causal(n, False), 1.0, 0.0).astype(BF16)
        cum = jnp.where(lane < SSD_HEADS, sum(_dot(low, p) for p in parts), cum)
    return lane, dt, cum


def _expand(a, e):
    return _dot(a.astype(BF16), e)


def _group_dup(v, g, lo):
    other = pltpu.roll(v, SSD_STATE, axis=1)
    return jnp.where(lo, v, other) if g == 0 else jnp.where(lo, other, v)


def _ssd_state_step(st_ref, j, bw, xp, elast, off):
    r = lax.broadcasted_iota(jnp.int32, (LANES, LANES), 0) < SSD_STATE
    c = lax.broadcasted_iota(jnp.int32, (LANES, LANES), 1) < SSD_HEAD_DIM
    dec = jnp.where(r, elast[:, off + 2 * j:off + 2 * j + 1], elast[:, off + 2 * j + 1:off + 2 * j + 2])
    st_ref[j] = jnp.where(r == c, dec * st_ref[j] + _dot_tn(bw.astype(BF16), xp), 0.0)


def _interleave(*bodies):
    live = list(bodies)
    while live:
        for body in list(live):
            if next(body, StopIteration) is StopIteration:
                live.remove(body)


def _ssd_states_body(x_ref, bc_ref, sm_ref, dtb_ref, alog_ref, eb_ref, o_ref, st_ref):
    n = x_ref.shape[1]

    @pl.when(pl.program_id(1) == 0)
    def _init():
        st_ref[...] = jnp.zeros_like(st_ref)

    o_ref[0, 0] = st_ref[...].astype(o_ref.dtype)
    yield
    lane, dt, cum = _ssd_gates(sm_ref, dtb_ref, alog_ref, n, False)
    lo = lane < SSD_HEAD_DIM
    last = cum[0:1, :]
    yield
    wst = _expand(jnp.exp(last - cum) * dt, eb_ref[...])
    elast = jnp.exp(last)
    b128 = bc_ref[0, :, :SSD_BC].astype(F32)
    for j in range(SSD_PAIRS):
        sl = slice(LANES * j, LANES * (j + 1))
        bw = _group_dup(b128, j // (SSD_PAIRS // 2), lo) * wst[:, sl]
        _ssd_state_step(st_ref, j, bw, x_ref[0, :, sl], elast, SSD_B_OFF)
        if j % 2:
            yield


def _ssd_out_body(x_ref, bc_ref, sm_ref, dtb_ref, alog_ref, ef_ref, eb_ref, d_ref, sb_ref, o_ref, st_ref):
    n = x_ref.shape[1]

    @pl.when(pl.program_id(1) == 0)
    def _init():
        st_ref[...] = jnp.zeros_like(st_ref)

    zero_b = jnp.zeros((), BF16)
    lo = lax.broadcasted_iota(jnp.int32, (1, LANES), 1) < SSD_HEAD_DIM
    b128_b = bc_ref[0, :, :SSD_BC]
    c128_b = bc_ref[0, :, SSD_BC:]
    cb_all = [_dot_nt(jnp.where(lo if g == 0 else jnp.logical_not(lo), c128_b, zero_b), b128_b)
              for g in range(2)]
    pair_x = lambda j: x_ref[0, :, LANES * j:LANES * (j + 1)]
    rhs_all = [jnp.concatenate([jnp.where(lo, pair_x(j), zero_b), jnp.where(lo, zero_b, pair_x(j)),
                                st_ref[j].astype(BF16), sb_ref[0, 0, j]], axis=0) for j in range(SSD_PAIRS)]
    yield
    lane, dt, cum = _ssd_gates(sm_ref, dtb_ref, alog_ref, n, True)
    is_f = lane < SSD_HEADS
    ldt = jnp.log(dt)
    dsum = jnp.log(dt + pltpu.roll(dt, LANES - SSD_B_OFF, axis=1))
    rt = (jnp.where(lane < 2 * SSD_HEADS, cum - ldt, pltpu.roll(dsum, 2 * SSD_HEADS, axis=1)) * LOG2E).T
    cum2 = cum * LOG2E
    yield
    last = jnp.where(is_f, cum[n - 1:n, :], cum[0:1, :])
    elast = jnp.exp(last)
    ecum = jnp.exp(cum)
    ecum_f = _expand(ecum, ef_ref[...])
    ecum_b = _expand(ecum, eb_ref[...])
    wst = _expand(jnp.exp(last - cum) * dt, ef_ref[...])
    bc = bc_ref[0].astype(F32)
    b128, c128 = bc[:, :SSD_BC], bc[:, SSD_BC:]
    ti = lax.broadcasted_iota(jnp.int32, (n, n), 0)
    si = lax.broadcasted_iota(jnp.int32, (n, n), 1)
    below, above = si < ti, si > ti
    half = SSD_PAIRS // 2
    yield
    for g in range(2):
        cb = cb_all[g]
        cdup = _group_dup(c128, g, lo)
        for j in range(g * half, (g + 1) * half):
            ms = []
            for h in (2 * j, 2 * j + 1):
                e_f = cum2[:, h:h + 1] - rt[h:h + 1, :]
                e_b = cum2[:, SSD_B_OFF + h:SSD_B_OFF + h + 1] - rt[SSD_B_OFF + h:SSD_B_OFF + h + 1, :]
                e = jnp.where(below, e_f, jnp.where(above, e_b, rt[2 * SSD_HEADS + h:2 * SSD_HEADS + h + 1, :]))
                ms.append((cb * jnp.exp2(e)).astype(BF16))
            sl = slice(LANES * j, LANES * (j + 1))
            cs_f = (cdup * ecum_f[:, sl]).astype(BF16)
            cs_b = (cdup * ecum_b[:, sl]).astype(BF16)
            lhs = jnp.concatenate(ms + [cs_f, cs_b], axis=1)
            o_ref[0, :, sl] = _dot(lhs, rhs_all[j]) + d_ref[:, sl] * pair_x(j).astype(F32)
            yield
    for j in range(SSD_PAIRS):
        sl = slice(LANES * j, LANES * (j + 1))
        _ssd_state_step(st_ref, j, _group_dup(b128, j // half, lo) * wst[:, sl], x_ref[0, :, sl], elast, 0)
        if j % 2:
            yield


def _run_together(name, grid, *parts, phases=None, fuse=None):
    n_in = [len(p[1]) for p in parts]
    n_out = [len(p[3]) for p in parts]
    n_scr = [len(p[5]) for p in parts]

    def kern(*refs):
        ins, outs, scr = refs[:sum(n_in)], refs[sum(n_in):sum(n_in) + sum(n_out)], refs[sum(n_in) + sum(n_out):]

        def run(key):
            bodies = []
            for k, p in enumerate(parts):
                take = lambda seq, counts: seq[sum(counts[:k]):sum(counts[:k + 1])]
                body = p[0][key] if isinstance(p[0], dict) else p[0]
                bodies.append(body(*take(ins, n_in), *take(outs, n_out), *take(scr, n_scr)))
            _interleave(*bodies)

        if phases is None:
            run(None)
        else:
            for pred, key in phases:
                pl.when(pred(pl.program_id(1)))(functools.partial(run, key))

    res = pl.pallas_call(
        kern,
        out_shape=tuple(s for p in parts for s in p[3]),
        grid=grid,
        in_specs=[s for p in parts for s in p[1]],
        out_specs=tuple(s for p in parts for s in p[4]),
        scratch_shapes=[s for p in parts for s in p[5]],
        compiler_params=_cparams(("arbitrary", "arbitrary"),
                                 None if fuse is None else [i in fuse for i in range(sum(n_in))]),
        name=name,
    )(*[a for p in parts for a in p[2]])
    return [list(res[sum(n_out[:k]):sum(n_out[:k + 1])]) for k in range(len(parts))]


def _ssd_parts(cvs, sm3, dt_bias, a_log, d_e, n_ctx):
    nb, t, _ = cvs.shape
    tile = SSD_TILE
    nt = t // tile
    row = lambda v: jnp.pad(v.reshape(1, -1), ((0, 0), (0, LANES - 2 * SSD_HEADS)))
    const = lambda b, i: (0, 0)
    specs = lambda order: [
        pl.BlockSpec((1, tile, SSD_WIDTH), lambda b, i: (b, order(i), 0)),
        pl.BlockSpec((1, tile, 2 * SSD_BC), lambda b, i: (b, order(i), CV_BC // (2 * SSD_BC))),
        pl.BlockSpec((1, tile, LANES), lambda b, i: (b, order(i), 0)),
        pl.BlockSpec((1, LANES), const),
        pl.BlockSpec((1, LANES), const)]
    st_block = (1, 1, SSD_PAIRS, LANES, LANES)
    args = (cvs, cvs, sm3, row(dt_bias), row(a_log))
    sel = np.zeros((2, LANES, SSD_WIDTH), np.float32)
    for h in range(SSD_HEADS):
        sel[0, h, h * SSD_HEAD_DIM:(h + 1) * SSD_HEAD_DIM] = 1.0
        sel[1, SSD_B_OFF + h, h * SSD_HEAD_DIM:(h + 1) * SSD_HEAD_DIM] = 1.0
    e_f, e_b = jnp.asarray(sel[0], BF16), jnp.asarray(sel[1], BF16)
    e_spec = pl.BlockSpec((LANES, SSD_WIDTH), const)
    bwd = _tile_order(nt, True, n_ctx // tile)
    fwd = _tile_order(nt, False)
    scratch = [pltpu.VMEM(st_block[2:], F32)]
    b_spec = pl.BlockSpec((1, tile, SSD_BC), lambda b, i: (b, bwd(i), CV_BC // SSD_BC))
    states = (_ssd_states_body, [specs(bwd)[0], b_spec] + specs(bwd)[2:] + [e_spec], list(args) + [e_b],
              [jax.ShapeDtypeStruct((nb, nt) + st_block[2:], BF16)],
              [pl.BlockSpec(st_block, lambda b, i: (b, bwd(i), 0, 0, 0))], scratch)
    out = lambda states_b: (
        _ssd_out_body,
        specs(fwd) + [e_spec, e_spec, pl.BlockSpec((1, SSD_WIDTH), const),
                      pl.BlockSpec(st_block, lambda b, i: (b, i, 0, 0, 0))],
        list(args) + [e_f, e_b, d_e, states_b],
        [jax.ShapeDtypeStruct((nb, t, SSD_WIDTH), F32)],
        [pl.BlockSpec((1, tile, SSD_WIDTH), lambda b, i: (b, i, 0))], scratch)
    return states, out


ML_GATE = 32
ML_ND = 2 * ML_HEADS
ML_AUG = ML_V_DIM + LANES


def _ml_gates(sm_ref, ib_ref, fb_ref, n, both):
    lane = lax.broadcasted_iota(jnp.int32, (1, LANES), 1)
    valid = (lane >= ML_GATE) & (lane < ML_GATE + ML_ND)
    is_f = lane < ML_GATE + ML_HEADS
    sm = sm_ref[0]
    li = sm + ib_ref[...]
    lf = pltpu.roll(_log_sigmoid(sm + fb_ref[...]), LANES - ML_ND, axis=1)
    parts = _split(jnp.where(valid, lf, 0.0), 2)
    upp = jnp.where(_causal(n, True), 1.0, 0.0).astype(BF16)
    bcum = sum(_dot(upp, p) for p in parts)
    if both:
        low = jnp.where(_causal(n, False), 1.0, 0.0).astype(BF16)
        bcum = jnp.where(is_f, sum(_dot(low, p) for p in parts), bcum)
    return valid, is_f, bcum, jnp.where(valid, li - bcum, 0.0)


def _ml_state_step(cn_ref, h, cn, keep, k, ws_dense, v_aug):
    w3 = jnp.concatenate([ws_dense.astype(BF16)] * (ML_AUG // LANES), axis=1)
    cn_ref[h] = keep * cn + _dot_tn(k, w3 * v_aug)


def _ml_v_aug(v_ref, h, n):
    return jnp.concatenate([v_ref[0, :, ML_V_DIM * h:ML_V_DIM * (h + 1)], jnp.ones((n, LANES), BF16)], axis=1)


def _ml_states_body(k_ref, v_ref, sm_ref, ib_ref, fb_ref, sel_ref, cn_out, m_out, cn_ref, m_ref):
    n = k_ref.shape[1]

    @pl.when(pl.program_id(1) == 0)
    def _init():
        cn_ref[...] = jnp.zeros_like(cn_ref)
        m_ref[...] = jnp.zeros_like(m_ref)

    cn_out[0, 0] = cn_ref[...].astype(cn_out.dtype)
    m_out[0, 0] = m_ref[...]
    yield
    valid, is_f, bcum, a = _ml_gates(sm_ref, ib_ref, fb_ref, n, False)
    m_prev = m_ref[0:1, :]
    g_last = jnp.maximum(m_prev, jnp.max(a, axis=0, keepdims=True))
    yield
    ws = _expand(jnp.exp(a - g_last), sel_ref[:, ML_HEADS * LANES:])
    keep = jnp.exp(m_prev - g_last)
    for h in range(ML_HEADS):
        lane_b = ML_GATE + ML_HEADS + h
        k = k_ref[0, :, ML_QK_DIM * h:ML_QK_DIM * (h + 1)]
        _ml_state_step(cn_ref, h, cn_ref[h], keep[:, lane_b:lane_b + 1], k,
                       ws[:, LANES * h:LANES * (h + 1)], _ml_v_aug(v_ref, h, n))
        yield
    m_ref[...] = jnp.broadcast_to(bcum[0:1, :] + g_last, m_ref.shape)


def _ml_out_body(qk_ref, v_ref, sm_ref, ib_ref, fb_ref, sel_ref, cnb_ref, mb_ref, o_ref, cn_ref, m_ref):
    n = qk_ref.shape[1]

    @pl.when(pl.program_id(1) == 0)
    def _init():
        cn_ref[...] = jnp.zeros_like(cn_ref)
        m_ref[...] = jnp.zeros_like(m_ref)

    head_q = lambda h: qk_ref[0, :, ML_QK_DIM * h:ML_QK_DIM * (h + 1)]
    head_k = lambda h: qk_ref[0, :, ML_QK_WIDTH + ML_QK_DIM * h:ML_QK_WIDTH + ML_QK_DIM * (h + 1)]
    qk_all = [_dot_nt(head_q(h), head_k(h)) for h in range(ML_HEADS)]
    inter_all = [_dot(head_q(h), jnp.concatenate([cn_ref[h].astype(BF16), cnb_ref[0, 0, h]], axis=1))
                 for h in range(ML_HEADS)]
    yield
    valid, is_f, bcum, a = _ml_gates(sm_ref, ib_ref, fb_ref, n, True)
    m_prev = jnp.where(is_f, m_ref[0:1, :], mb_ref[0, 0, 0:1, :])
    a_t = a.T
    pre = suf = a_t[ML_GATE:ML_GATE + ML_ND, :]
    pos = lax.broadcasted_iota(jnp.int32, (ML_ND, n), 1)
    k = 1
    while k < n:
        pre = jnp.maximum(pre, jnp.where(pos >= k, pltpu.roll(pre, k, axis=1), -jnp.inf))
        suf = jnp.maximum(suf, jnp.where(pos < n - k, pltpu.roll(suf, n - k, axis=1), -jnp.inf))
        k *= 2
    run = jnp.where(lax.broadcasted_iota(jnp.int32, (ML_ND, n), 0) < ML_HEADS, pre, suf)
    run = jnp.concatenate([jnp.zeros((ML_GATE, n), F32), run,
                           jnp.zeros((LANES - ML_GATE - ML_ND, n), F32)], axis=0).T
    g = jnp.maximum(m_prev, run)
    m_t = bcum + g
    yield
    floor = jnp.exp(-m_t)
    dense = lambda x, lane_: jnp.broadcast_to(x[:, lane_:lane_ + 1], (n, LANES))
    diag_t = jnp.exp(a - g).T
    g_last = g[n - 1:n, :]
    ws = _expand(jnp.exp(a - g_last), sel_ref[:, :ML_HEADS * LANES])
    keep = jnp.exp(m_prev - g_last)
    ti = lax.broadcasted_iota(jnp.int32, (n, n), 0)
    si = lax.broadcasted_iota(jnp.int32, (n, n), 1)
    not_above, above, on_diag = si <= ti, si > ti, si == ti
    wide = lambda x: jnp.concatenate([x] * (n // LANES), axis=1)
    for h in range(ML_HEADS):
        lanes = (ML_GATE + h, ML_GATE + ML_HEADS + h)
        g_d = [dense(g, ln) for ln in lanes]
        qk, inter = qk_all[h], inter_all[h]
        e = jnp.where(not_above, a_t[lanes[0]:lanes[0] + 1, :] - wide(g_d[0]),
                      a_t[lanes[1]:lanes[1] + 1, :] - wide(g_d[1]))
        p = qk * jnp.exp(e)
        p_f = jnp.where(not_above, p, 0.0).astype(BF16)
        p_b = jnp.where(above, p, jnp.where(on_diag, qk * diag_t[lanes[1]:lanes[1] + 1, :], 0.0)).astype(BF16)
        intra = _dot(jnp.concatenate([p_f, p_b], axis=0), _ml_v_aug(v_ref, h, n))
        out = None
        for d in range(2):
            w_inter = jnp.exp(m_prev[:, lanes[d]:lanes[d] + 1] - g_d[d])
            s = (intra[n * d:n * (d + 1)] + jnp.concatenate([w_inter] * (ML_AUG // LANES), axis=1)
                 * inter[:, ML_AUG * d:ML_AUG * (d + 1)])
            rn = 1.0 / jnp.maximum(jnp.abs(s[:, ML_V_DIM:]), dense(floor, lanes[d]))
            hid = s[:, :ML_V_DIM] * jnp.concatenate([rn] * (ML_V_DIM // LANES), axis=1)
            out = hid if out is None else out + hid
        o_ref[0, :, ML_V_DIM * h:ML_V_DIM * (h + 1)] = out
        yield
    for h in range(ML_HEADS):
        lane_f = ML_GATE + h
        _ml_state_step(cn_ref, h, cn_ref[h], keep[:, lane_f:lane_f + 1], head_k(h),
                       ws[:, LANES * h:LANES * (h + 1)], _ml_v_aug(v_ref, h, n))
        yield
    m_ref[...] = jnp.broadcast_to(m_t[n - 1:n, :], m_ref.shape)


def _mlstm_parts(cvm, p3, sm3, i_bias, f_bias, n_ctx):
    nb, t, _ = cvm.shape
    nt = t // TILE
    assert _SM_OFF["i_f"] == ML_GATE and _SM_OFF["f_f"] == ML_GATE + ML_ND
    row = lambda v, off: jnp.pad(v.reshape(1, -1), ((0, 0), (off, LANES - off - ML_ND)))
    sel = np.zeros((LANES, ML_ND * LANES), np.float32)
    for r in range(ML_ND):
        sel[ML_GATE + r, r * LANES:(r + 1) * LANES] = 1.0
    const = lambda b, i: (0, 0)
    specs = lambda order: [
        pl.BlockSpec((1, TILE, 2 * ML_QK_WIDTH), lambda b, i: (b, order(i), CV_QK // (2 * ML_QK_WIDTH))),
        pl.BlockSpec((1, TILE, ML_V_WIDTH), lambda b, i: (b, order(i), _P_OFF["m_v"] // ML_V_WIDTH)),
        pl.BlockSpec((1, TILE, LANES), lambda b, i: (b, order(i), 0)),
        pl.BlockSpec((1, LANES), const),
        pl.BlockSpec((1, LANES), const),
        pl.BlockSpec((LANES, ML_ND * LANES), const)]
    args = (cvm, p3, sm3, row(i_bias, ML_GATE), row(f_bias, ML_GATE + ML_ND), jnp.asarray(sel, BF16))
    cn_block = (1, 1, ML_HEADS, ML_QK_DIM, ML_AUG)
    m_block = (1, 1, SUBLANES, LANES)
    scratch = [pltpu.VMEM(cn_block[2:], F32), pltpu.VMEM(m_block[2:], F32)]
    bwd = _tile_order(nt, True, n_ctx // TILE)
    fwd = _tile_order(nt, False)
    st_idx = lambda b, i: (b, bwd(i)) + (0,) * 3
    k_spec = pl.BlockSpec((1, TILE, ML_QK_WIDTH), lambda b, i: (b, bwd(i), (CV_QK + ML_QK_WIDTH) // ML_QK_WIDTH))
    states = (_ml_states_body, [k_spec] + specs(bwd)[1:], list(args),
              [jax.ShapeDtypeStruct((nb, nt) + cn_block[2:], BF16),
               jax.ShapeDtypeStruct((nb, nt) + m_block[2:], F32)],
              [pl.BlockSpec(cn_block, st_idx), pl.BlockSpec(m_block, lambda b, i: (b, bwd(i), 0, 0))],
              scratch)
    out = lambda cn_b, m_b: (
        _ml_out_body,
        specs(fwd) + [pl.BlockSpec(cn_block, lambda b, i: (b, i, 0, 0, 0)),
                      pl.BlockSpec(m_block, lambda b, i: (b, i, 0, 0))],
        list(args) + [cn_b, m_b],
        [jax.ShapeDtypeStruct((nb, t, ML_V_WIDTH), F32)],
        [pl.BlockSpec((1, TILE, ML_V_WIDTH), lambda b, i: (b, i, 0))], scratch)
    return states, out


GLA_SUB = 256
GLA_NCH = GLA_SUB // GLA_CHUNK
GLA_COLS = BF16_SUBLANES


def _gla_layout(is_ctx, lat_rows):
    r = np.arange(GLA_SUB)
    if is_ctx:
        return r // GLA_CHUNK, r % GLA_CHUNK
    col = r % SUBLANES
    cpc = GLA_CHUNK // lat_rows
    return col // cpc, (col % cpc) * lat_rows + r // SUBLANES


def _gla_consts(is_ctx, lat_rows, rev):
    ch, pos = _gla_layout(is_ctx, lat_rows)
    same = ch[:, None] == ch[None, :]
    before = (pos[None, :] >= pos[:, None]) if rev else (pos[None, :] <= pos[:, None])
    tri = (same & before).astype(np.float32)
    cmask = np.stack([np.repeat((ch == j)[:, None], LANES, axis=1) for j in range(GLA_NCH)])
    return tri, cmask.astype(np.float32)


def _gla_row(is_ctx, lat_rows, j, p):
    ch, pos = _gla_layout(is_ctx, lat_rows)
    return int(np.nonzero((ch == j) & (pos == p))[0][0])


def _per_chunk_rows(b, is_ctx, lat_rows, p):
    rows = [b[_gla_row(is_ctx, lat_rows, j, p):_gla_row(is_ctx, lat_rows, j, p) + 1, :]
            for j in range(GLA_NCH)]
    w = b.shape[1]
    if is_ctx:
        full = jnp.concatenate([jnp.broadcast_to(r, (GLA_CHUNK, w)) for r in rows], axis=0)
    else:
        rep = SUBLANES // GLA_NCH
        pat = jnp.concatenate([jnp.broadcast_to(r, (rep, w)) for r in rows], axis=0)
        full = jnp.broadcast_to(pat[None], (GLA_SUB // SUBLANES, SUBLANES, w)).reshape(GLA_SUB, w)
    return rows, full


def _gla_sub(q, k, v, araw, aup, abias, tri_b, tri_f, cmask_ref, st_ref, store, *, rev, is_ctx, lat_rows):
    want_out = store is not None
    x = _dot(araw.astype(BF16), aup) + abias
    g = _log_sigmoid(x) * (1.0 / GLA_TAU)
    b = _dot_exact_lhs(tri_b, g, 2)
    yield
    lasts, last = _per_chunk_rows(b, is_ctx, lat_rows, 0 if rev else GLA_CHUNK - 1)
    kl = (k * jnp.exp(last - b)).astype(BF16)
    if want_out:
        _, ref = _per_chunk_rows(b, is_ctx, lat_rows, GLA_CHUNK // 2)
        qs = q * (GLA_K_DIM ** -0.5)
        qe = (qs * jnp.exp(b - ref)).astype(BF16)
        ke = (k * jnp.exp(ref - b)).astype(BF16)
        qb = (qs * jnp.exp(b)).astype(BF16)
        visible = tri_f > 0.0
    yield
    order = range(GLA_NCH - 1, -1, -1) if rev else range(GLA_NCH)
    outs = []
    for h in range(GLA_HEADS):
        ks = slice(GLA_K_DIM * h, GLA_K_DIM * (h + 1))
        vh = v[:, GLA_V_DIM * h:GLA_V_DIM * (h + 1)]
        klm = jnp.concatenate([kl[:, ks] * cmask_ref[j] for j in range(GLA_NCH)], axis=1)
        upd = _dot_tn(vh, klm)
        s = st_ref[h]
        s_in = [None] * GLA_NCH
        for j in order:
            s_in[j] = s.astype(BF16)
            s = s * jnp.exp(lasts[j][:, ks]) + upd[:, GLA_K_DIM * j:GLA_K_DIM * (j + 1)]
        st_ref[h] = s
        if want_out:
            att = jnp.where(visible, _dot_nt(qe[:, ks], ke[:, ks]), 0.0).astype(BF16)
            qbm = jnp.concatenate([qb[:, ks] * cmask_ref[j] for j in range(GLA_NCH)], axis=1)
            outs.append(_dot(att, vh) + _dot_nt(qbm, jnp.concatenate(s_in, axis=1)))
        yield
    if want_out:
        store(jnp.concatenate(outs, axis=1))


def _gla_ctx_body(qc_ref, kc_ref, vc_ref, sc_ref, ql_ref, kl_ref, vl_ref, sl_ref,
                  aup_ref, ab_ref, tcb_ref, tcf_ref, cmc_ref, tlb_ref, tlf_ref, cml_ref,
                  o_ref, st_ref, ctxo_ref, *, rev, ctx_rows, lat_rows, n_cblk, ctx_out):
    st_ref[...] = jnp.zeros_like(st_ref)

    def store(o):
        ctxo_ref[...] = o

    yield from _gla_sub(qc_ref[0].astype(F32), kc_ref[0].astype(F32), vc_ref[0], sc_ref[0],
                        aup_ref[...], ab_ref[...], tcb_ref[...], tcf_ref[...], cmc_ref, st_ref,
                        store if ctx_out else None, rev=rev, is_ctx=True, lat_rows=lat_rows)


def _gla_lat_body(qc_ref, kc_ref, vc_ref, sc_ref, ql_ref, kl_ref, vl_ref, sl_ref,
                  aup_ref, ab_ref, tcb_ref, tcf_ref, cmc_ref, tlb_ref, tlf_ref, cml_ref,
                  o_ref, st_ref, ctxo_ref, *, rev, ctx_rows, lat_rows, n_cblk, ctx_out):
    i = pl.program_id(1)
    r0, r1 = ctx_rows, ctx_rows + lat_rows
    cblk = (n_cblk - i) if rev else (i - 1)
    halves = range(GLA_COLS // SUBLANES)
    for half in (reversed(halves) if rev else halves):
        cs = slice(SUBLANES * half, SUBLANES * (half + 1))
        take = lambda r: r[0, r0:r1].astype(F32)[:, cs, :].reshape(GLA_SUB, r.shape[-1])

        def store(o, cs=cs):
            o_ref[0, r0:r1, cs, :] = o.reshape(lat_rows, SUBLANES, GLA_V_WIDTH)

        yield from _gla_sub(take(ql_ref), take(kl_ref), take(vl_ref).astype(BF16), take(sl_ref),
                            aup_ref[...], ab_ref[...], tlb_ref[...], tlf_ref[...], cml_ref, st_ref,
                            store, rev=rev, is_ctx=False, lat_rows=lat_rows)
    for r in range(ctx_rows):
        if ctx_out:
            start = pl.multiple_of(r * GRID_W + cblk * GLA_COLS, GLA_COLS)
            o_ref[0, r, :, :] = ctxo_ref[pl.ds(start, GLA_COLS), :]
        else:
            o_ref[0, r, :, :] = jnp.zeros((GLA_COLS, GLA_V_WIDTH), F32)


GLA_PHASES = ((lambda i: i == 0, "ctx"), (lambda i: i > 0, "lat"))


def _gla_part(p3, sm3, a_up, a_bias, n_ctx, rev, ctx_out):
    nb, t, ncol = p3.shape
    rows = t // GRID_W
    ctx_rows = n_ctx // GRID_W
    lat_rows = rows - ctx_rows
    n_cblk = GRID_W // GLA_COLS
    p4 = p3.reshape(nb, rows, GRID_W, ncol)
    sm4 = sm3.reshape(nb, rows, GRID_W, LANES)
    a_off = _SM_OFF["a_b"] if rev else _SM_OFF["a_f"]
    aup = jnp.pad(a_up, ((a_off, LANES - a_off - GLA_RANK), (0, 0))).astype(BF16)
    cblk = lambda i: jnp.where(i == 0, n_cblk - 1 if rev else 0, (n_cblk - i) if rev else (i - 1))
    ctx = lambda blk: (lambda b, i: (b, 0, blk))
    lat = lambda blk: (lambda b, i: (b, 0, cblk(i), blk))
    const2 = lambda b, i: (0, 0)
    const3 = lambda b, i: (0, 0, 0)
    widths = (GLA_K_WIDTH, GLA_K_WIDTH, GLA_V_WIDTH)
    offs = (_P_OFF["g_q"], _P_OFF["g_k"], _P_OFF["g_v"])
    consts = []
    const_specs = []
    for is_ctx in (True, False):
        tri, cmask = _gla_consts(is_ctx, lat_rows, rev)
        consts += [jnp.asarray(tri, BF16), jnp.asarray(tri, F32), jnp.asarray(cmask, BF16)]
        const_specs += [pl.BlockSpec((GLA_SUB, GLA_SUB), const2), pl.BlockSpec((GLA_SUB, GLA_SUB), const2),
                        pl.BlockSpec((GLA_NCH, GLA_SUB, LANES), const3)]
    out_spec = pl.BlockSpec((1, rows, GLA_COLS, GLA_V_WIDTH), lat(0))
    in_specs = ([pl.BlockSpec((1, n_ctx, w), ctx(o // w)) for w, o in zip(widths, offs)]
                + [pl.BlockSpec((1, n_ctx, LANES), ctx(0))]
                + [pl.BlockSpec((1, rows, GLA_COLS, w), lat(o // w)) for w, o in zip(widths, offs)]
                + [pl.BlockSpec((1, rows, GLA_COLS, LANES), lat(0))]
                + [pl.BlockSpec((LANES, GLA_K_WIDTH), const2), pl.BlockSpec((1, GLA_K_WIDTH), const2)]
                + const_specs)
    args = [p3, p3, p3, sm3, p4, p4, p4, sm4, aup, a_bias.reshape(1, GLA_K_WIDTH)] + consts
    static = dict(rev=rev, ctx_rows=ctx_rows, lat_rows=lat_rows, n_cblk=n_cblk, ctx_out=ctx_out)
    bodies = {"ctx": functools.partial(_gla_ctx_body, **static),
              "lat": functools.partial(_gla_lat_body, **static)}
    return (bodies, in_specs, args,
            [jax.ShapeDtypeStruct((nb, rows, GRID_W, GLA_V_WIDTH), F32)], [out_spec],
            [pltpu.VMEM((GLA_HEADS, GLA_V_DIM, GLA_K_DIM), F32), pltpu.VMEM((n_ctx, GLA_V_WIDTH), F32)])


def _group_rmsnorm(y, groups):
    width = y.shape[-1] // groups
    ones = jnp.ones((width, LANES), BF16)
    out = []
    for g in range(groups):
        yg = y[:, width * g:width * (g + 1)]
        ms = _dot((yg * yg).astype(BF16), ones) * (1.0 / width)
        out.append(yg * jnp.concatenate([lax.rsqrt(ms + EPS)] * (width // LANES), axis=1))
    return jnp.concatenate(out, axis=1)


def _cast_weights_once(pairs):
    @pl.when((pl.program_id(0) == 0) & (pl.program_id(1) == 0))
    def _cast():
        for w_ref, s_ref in pairs:
            s_ref[...] = w_ref[...].astype(s_ref.dtype)


def _post_kernel(x_ref, y_ref, h_ref, of_ref, ob_ref, z_ref, mo_ref, gg_ref, gs_ref, gm_ref, gl_ref,
                 nws_ref, nwm_ref, nwg_ref, wbs32_ref, wbm32_ref, wbg32_ref, wout32_ref, g1_ref,
                 nwf_ref, sc2_ref, sh2_ref, xo_ref, ho_ref, wbs_ref, wbm_ref, wbg_ref, wout_ref):
    _cast_weights_once(((wbs32_ref, wbs_ref), (wbm32_ref, wbm_ref), (wbg32_ref, wbg_ref), (wout32_ref, wout_ref)))
    y_ssd = (_group_rmsnorm(y_ref[0] * _silu(z_ref[0]), 2) * nws_ref[...]).astype(BF16)
    y_ml = (_group_rmsnorm(h_ref[0], ML_HEADS) * nwm_ref[...]).astype(BF16) * _sigmoid(mo_ref[0])
    y_gla = (_group_rmsnorm(of_ref[0] + ob_ref[0], GLA_HEADS) * nwg_ref[...]).astype(BF16) * _silu(gg_ref[0])
    merged = (_sigmoid(gs_ref[0]) * _dot(y_ssd, wbs_ref[...])
              + _sigmoid(gm_ref[0]) * _dot(y_ml, wbm_ref[...])
              + _sigmoid(gl_ref[0]) * _dot(y_gla, wbg_ref[...]))
    x_new = x_ref[0] + g1_ref[0] * _dot(merged.astype(BF16), wout_ref[...])
    xo_ref[0] = x_new
    ho_ref[0] = (_rms(x_new) * nwf_ref[...] * (1.0 + sc2_ref[0]) + sh2_ref[0]).astype(ho_ref.dtype)


def _post(x, scans, p3, norm_ws, w_bs, w_out, layer, norm_ffn_w, mods, n_ctx, tile0):
    nb, t, d = x.shape
    nt = t // TILE - tile0
    ctx_tiles = n_ctx // TILE
    tok = lambda blk: (lambda b, i: (b, i + tile0, blk))
    out = lambda b, i: (b, i, 0)
    const = lambda b, i: (0, 0)
    tok_spec = lambda blk: pl.BlockSpec((1, TILE, d), tok(blk))
    w_spec = pl.BlockSpec((None, d, d), lambda b, i: (layer, 0, 0), pipeline_mode=pl.Buffered(1))
    vec = pl.BlockSpec((1, d), const)
    mod = lambda k: pl.BlockSpec((1, 1, d), _mod_row(nb, k, ctx_tiles, tile0))
    names = ("s_z", "m_o", "g_g", "gate_ssd", "gate_ml", "gate_gla")
    in_specs = ([tok_spec(0)] * (1 + len(scans)) + [tok_spec(_P_OFF[nm] // d) for nm in names]
                + [vec] * 3 + [w_spec] * 4 + [mod(2), vec, mod(4), mod(3)])
    return pl.pallas_call(
        _post_kernel,
        out_shape=(jax.ShapeDtypeStruct((nb, nt * TILE, d), F32),
                   jax.ShapeDtypeStruct((nb, nt * TILE, d), BF16)),
        grid=(nb, nt),
        in_specs=in_specs,
        out_specs=(pl.BlockSpec((1, TILE, d), out), pl.BlockSpec((1, TILE, d), out)),
        scratch_shapes=[pltpu.VMEM((d, d), BF16)] * 4,
        compiler_params=_cparams(("arbitrary", "arbitrary")),
        name="post",
    )(x, *scans, *([p3] * 6), *[w.reshape(1, d) for w in norm_ws], *w_bs, w_out, mods,
      norm_ffn_w.reshape(1, d), mods, mods)


def _ffn_in_kernel(a_ref, wg_ref, wu_ref, o_ref, w_ref):
    half = o_ref.shape[1]

    @pl.when(pl.program_id(1) == 0)
    def _cast():
        w_ref[:, :half] = wg_ref[...].astype(w_ref.dtype)
        w_ref[:, half:] = wu_ref[...].astype(w_ref.dtype)

    acc = _dot(a_ref[...], w_ref[...])
    o_ref[...] = (_silu(acc[:, :half]) * acc[:, half:]).astype(o_ref.dtype)


def _ffn_in(h, w_ffn_in, layer, half):
    m, k = h.shape
    n_half = w_ffn_in.shape[-1] // 2
    tm = _row_tile(m)
    return pl.pallas_call(
        _ffn_in_kernel,
        out_shape=jax.ShapeDtypeStruct((m, n_half), BF16),
        grid=(n_half // half, m // tm),
        in_specs=[pl.BlockSpec((tm, k), lambda j, i: (i, 0)),
                  pl.BlockSpec((None, k, half), lambda j, i: (layer, 0, j)),
                  pl.BlockSpec((None, k, half), lambda j, i: (layer, 0, j + n_half // half))],
        out_specs=pl.BlockSpec((tm, half), lambda j, i: (i, j)),
        scratch_shapes=[pltpu.VMEM((k, 2 * half), BF16)],
        compiler_params=_cparams(("arbitrary", "arbitrary")),
        name="ffn_in",
    )(h, w_ffn_in, w_ffn_in)


def _ffn_out_kernel(a_ref, w32_ref, x_ref, g_ref, nw_ref, sc_ref, sh_ref, xo_ref, ho_ref, w_ref):
    _cast_weights_once(((w32_ref, w_ref),))
    x_new = x_ref[0] + g_ref[0] * _dot(a_ref[0], w_ref[...])
    xo_ref[0] = x_new
    ho_ref[0] = (_rms(x_new) * nw_ref[...] * (1.0 + sc_ref[0]) + sh_ref[0]).astype(ho_ref.dtype)


def _ffn_out_last_kernel(a_ref, w32_ref, x_ref, g_ref, nw_ref, o_ref, w_ref):
    _cast_weights_once(((w32_ref, w_ref),))
    x_new = x_ref[0] + g_ref[0] * _dot(a_ref[0], w_ref[...])
    o_ref[0] = _rms(x_new) * nw_ref[...]


def _ffn_out(a, w, layer, x, mods, n_ctx, tile0, next_norm_w, next_mods):
    nb, t, d = x.shape
    k = a.shape[-1]
    ctx_tiles = n_ctx // TILE
    tok = lambda b, i: (b, i, 0)
    mod = lambda k_: pl.BlockSpec((1, 1, d), _mod_row(nb, k_, ctx_tiles, tile0))
    in_specs = [pl.BlockSpec((1, TILE, k), tok),
                pl.BlockSpec((None, k, d), lambda b, i: (layer, 0, 0), pipeline_mode=pl.Buffered(1)),
                pl.BlockSpec((1, TILE, d), tok),
                mod(5),
                pl.BlockSpec((1, d), lambda b, i: (0, 0))]
    args = [a, w, x, mods, next_norm_w.reshape(1, d)]
    if next_mods is None:
        body = _ffn_out_last_kernel
        out_shape = jax.ShapeDtypeStruct((nb, t, d), F32)
        out_specs = pl.BlockSpec((1, TILE, d), tok)
    else:
        body = _ffn_out_kernel
        in_specs += [mod(1), mod(0)]
        args += [next_mods, next_mods]
        out_shape = (jax.ShapeDtypeStruct((nb, t, d), F32), jax.ShapeDtypeStruct((nb, t, d), BF16))
        out_specs = (pl.BlockSpec((1, TILE, d), tok), pl.BlockSpec((1, TILE, d), tok))
    return pl.pallas_call(
        body,
        out_shape=out_shape,
        grid=(nb, t // TILE),
        in_specs=in_specs,
        out_specs=out_specs,
        scratch_shapes=[pltpu.VMEM((k, d), BF16)],
        compiler_params=_cparams(("arbitrary", "arbitrary")),
        name="ffn_out",
    )(*args)


def _proj_weights(w_in):
    cols = lambda names: [w_in[..., _IN_OFF[nm]:_IN_OFF[nm] + _IN_W[nm]] for nm in names]
    main = jnp.concatenate(cols(_P_ORDER), axis=-1)
    small = jnp.pad(jnp.concatenate(cols(_SMALL), axis=-1), ((0, 0), (0, 0), (0, LANES - N_SMALL_USED)))
    return main.astype(BF16), small.astype(BF16)


def kernel(x, c, ctx, c_ctx, w_mod, b_mod, norm_mix_w, norm_ffn_w, w_in, ssd_conv_w, ssd_conv_b, ssd_dt_bias, ssd_a_log, ssd_d, ssd_norm_w, ml_conv_w, ml_conv_b, ml_i_bias, ml_f_bias, ml_norm_w, gla_a_up, gla_a_bias, gla_norm_w, w_b_ssd, w_b_ml, w_b_gla, w_out, w_ffn_in, w_ffn_out, final_norm_w):
    nb, n_lat, d = x.shape
    n_ctx = ctx.shape[1]
    t = n_ctx + n_lat
    depth = w_in.shape[0]
    d_ff = w_ffn_out.shape[1]
    assert n_ctx == TILE == GLA_SUB and n_lat % TILE == 0 and n_lat // GRID_W == 32
    ffn_half = d_ff // 2

    c16 = jnp.pad(jnp.concatenate([c, c_ctx[None]], axis=0), ((0, 2 * SUBLANES - nb - 1), (0, 0)))
    mods = [_modulation(c16, w_mod, b_mod, l).reshape(2 * SUBLANES * 6, 1, d) for l in range(depth)]
    xs, h = _norm_mod(ctx, x, norm_mix_w[0], mods[0], 0, 1)
    conv_w = jnp.concatenate([ssd_conv_w, ml_conv_w], axis=-1)
    conv_b = jnp.concatenate([ssd_conv_b, ml_conv_b], axis=-1)
    conv_post = jnp.concatenate([jnp.ones((ssd_conv_w.shape[-1] + ML_QK_WIDTH,), F32),
                                 jnp.full((ML_QK_WIDTH,), ML_QK_DIM ** -0.5, F32)])
    w_main, w_small = _proj_weights(w_in)
    w_bs = (w_b_ssd, w_b_ml, w_b_gla)
    for l in range(depth):
        last = l == depth - 1
        h2d = h.reshape(nb * t, d)
        tn = N_PROJ // 5
        conv_tile = _P_OFF["s_x"] // tn
        assert conv_tile * tn == _P_OFF["s_x"] and N_PROJ - _P_OFF["s_x"] == tn == CV_WIDTH
        m_tiles = nb * t // _row_tile(nb * t)
        (pc,), (sm,) = _run_together("proj_first", (1, m_tiles),
                                     _mm_part(h2d, w_main, l, tn, BF16, conv_tile, 1),
                                     _mm_part(h2d, w_small, l, LANES, F32, 0, 1), fuse=(1, 3))
        pc3, sm3 = pc.reshape(nb, t, tn), sm.reshape(nb, t, LANES)
        grid = (conv_tile, m_tiles)
        (p,), (cv,) = _run_together(
            "proj_conv", grid, _mm_part(h2d, w_main, l, tn, BF16, 0, conv_tile),
            _conv_part(pc3, conv_w[l], conv_b[l], conv_post, n_ctx, grid), fuse=(1,))
        p3 = p.reshape(nb, t, conv_tile * tn)
        d_e = jnp.repeat(ssd_d[l], SSD_HEAD_DIM).reshape(1, SSD_WIDTH)
        ssd_states, ssd_out = _ssd_parts(cv, sm3, ssd_dt_bias[l], ssd_a_log[l], d_e, n_ctx)
        ml_states, ml_out = _mlstm_parts(cv, p3, sm3, ml_i_bias[l], ml_f_bias[l], n_ctx)
        grid = (nb, t // TILE)
        ssd_st, ml_st = _run_together("bwd_states", grid, ssd_states, ml_states)
        (y,), (hm,) = _run_together("ssd_mlstm", grid, ssd_out(*ssd_st), ml_out(*ml_st))
        gla = [_gla_part(p3, sm3, gla_a_up[l, k], gla_a_bias[l, k], n_ctx, bool(k), not last) for k in range(2)]
        (og_f,), (og_b,) = _run_together("gla", (nb, GRID_W // GLA_COLS + 1), *gla, phases=GLA_PHASES)
        og_f, og_b = og_f.reshape(nb, t, GLA_V_WIDTH), og_b.reshape(nb, t, GLA_V_WIDTH)
        tile0 = n_ctx // TILE if last else 0
        xs, h2 = _post(xs, (y, hm, og_f, og_b), p3, (ssd_norm_w[l], ml_norm_w[l], gla_norm_w[l]),
                       w_bs, w_out, l, norm_ffn_w[l], mods[l], n_ctx, tile0)
        nt = xs.shape[1]
        a = _ffn_in(h2.reshape(nb * nt, d), w_ffn_in, l, ffn_half)
        a = a.reshape(nb, nt, d_ff)
        if last:
            return _ffn_out(a, w_ffn_out, l, xs, mods[l], n_ctx, tile0, final_norm_w, None)
        xs, h = _ffn_out(a, w_ffn_out, l, xs, mods[l], n_ctx, tile0,
                         norm_mix_w[l + 1], mods[l + 1])
```

```python
import functools

import numpy as np
import jax
import jax.numpy as jnp
from jax import lax
from jax.experimental import pallas as pl
from jax.experimental.pallas import tpu as pltpu

F32 = jnp.float32
BF16 = jnp.bfloat16

EPS = 1e-6
LOG2E = 1.4426950408889634
GRID_W = 64
SSD_HEADS = 16
SSD_HEAD_DIM = 64
SSD_WIDTH = 1024
SSD_STATE = 64
SSD_BC = 128
ML_HEADS = 4
ML_QK_DIM = 128
ML_V_DIM = 256
ML_QK_WIDTH = 512
ML_V_WIDTH = 1024
GLA_HEADS = 4
GLA_K_DIM = 128
GLA_V_DIM = 256
GLA_K_WIDTH = 512
GLA_V_WIDTH = 1024
GLA_RANK = 16
GLA_TAU = 16.0
GLA_CHUNK = 64

LANES = 128
SUBLANES = 8
BF16_SUBLANES = 16
VMEM_LIMIT = 56 * 1024 * 1024

TILE = 256

_IN_NAMES = ("s_x", "s_z", "s_b", "s_c", "dt_f", "dt_b",
             "m_q", "m_k", "m_v", "m_o", "i_f", "i_b", "f_f", "f_b",
             "g_q", "g_k", "g_v", "g_g", "a_f", "a_b",
             "gate_ssd", "gate_ml", "gate_gla")
_IN_WIDTHS = (1024, 1024, 128, 128, 16, 16,
              512, 512, 1024, 1024, 4, 4, 4, 4,
              512, 512, 1024, 1024, 16, 16,
              1024, 1024, 1024)
_IN_OFF = dict(zip(_IN_NAMES, np.concatenate([[0], np.cumsum(_IN_WIDTHS)[:-1]]).tolist()))
_IN_W = dict(zip(_IN_NAMES, _IN_WIDTHS))

_P_ORDER = ("s_z", "m_o", "g_g", "gate_ssd", "gate_ml", "gate_gla", "m_v", "g_v",
            "g_q", "g_k", "s_x", "s_b", "s_c", "m_q", "m_k")
_P_OFF = {}
_o = 0
for _n in _P_ORDER:
    _P_OFF[_n] = _o
    _o += _IN_W[_n]
N_PROJ = _o
_SMALL = ("dt_f", "dt_b", "i_f", "i_b", "f_f", "f_b", "a_f", "a_b")
_SM_OFF = {}
_s = 0
for _n in _SMALL:
    _SM_OFF[_n] = _s
    _s += _IN_W[_n]
N_SMALL_USED = _s


def _cparams(sem):
    return pltpu.CompilerParams(dimension_semantics=sem, vmem_limit_bytes=VMEM_LIMIT)


def _sigmoid(x):
    return 0.5 * jnp.tanh(0.5 * x) + 0.5


def _silu(x):
    h = 0.5 * x
    return h + h * jnp.tanh(h)


def _softplus(x):
    return jnp.maximum(x, 0.0) + jnp.log1p(jnp.exp(-jnp.abs(x)))


def _log_sigmoid(x):
    return jnp.minimum(x, 0.0) - jnp.log(1.0 + jnp.exp(-jnp.abs(x)))


def _split(x, n):
    out = []
    r = x
    for _ in range(n):
        p = r.astype(BF16)
        out.append(p)
        r = r - p.astype(F32)
    return out


def _dot(a, b):
    return jnp.dot(a, b, preferred_element_type=F32)


def _dot_nt(a, b):
    return lax.dot_general(a, b, (((1,), (1,)), ((), ())), preferred_element_type=F32)


def _dot_tn(a, b):
    return lax.dot_general(a, b, (((0,), (0,)), ((), ())), preferred_element_type=F32)


def _dot_exact_lhs(t, x, pieces):
    return sum(_dot(t, p) for p in _split(x, pieces))


def _dot_hp(a, b):
    ah, am = _split(a, 2)
    bh, bm = _split(b, 2)
    return _dot(ah, bh) + _dot(ah, bm) + _dot(am, bh)


def _causal(n, rev):
    t = lax.broadcasted_iota(jnp.int32, (n, n), 0)
    s = lax.broadcasted_iota(jnp.int32, (n, n), 1)
    return (s >= t) if rev else (s <= t)


def _rms(x):
    return x * lax.rsqrt(jnp.mean(x * x, axis=-1, keepdims=True) + EPS)


def _mod_kernel(c_ref, w_ref, b_ref, o_ref):
    o_ref[...] = _dot_hp(_silu(c_ref[...]), w_ref[...]) + b_ref[...]


def _modulation(c16, w_mod, b_mod, layer):
    rows, d = c16.shape
    n = w_mod.shape[-1]
    tn = 1536
    return pl.pallas_call(
        _mod_kernel,
        out_shape=jax.ShapeDtypeStruct((rows, n), F32),
        grid=(n // tn,),
        in_specs=[pl.BlockSpec((rows, d), lambda j: (0, 0)),
                  pl.BlockSpec((None, d, tn), lambda j: (layer, 0, j)),
                  pl.BlockSpec((None, 1, tn), lambda j: (layer, 0, j))],
        out_specs=pl.BlockSpec((rows, tn), lambda j: (0, j)),
        compiler_params=_cparams(("arbitrary",)),
        name="modulation",
    )(c16, w_mod, b_mod.reshape(b_mod.shape[0], 1, n))


def _mod_row(nb, k, ctx_tiles, tile0=0):
    return lambda b, i: (jnp.where(i + tile0 < ctx_tiles, nb, b) * 6 + k, 0, 0)


def _norm_mod_kernel(ctx_ref, x_ref, w_ref, sc_ref, sh_ref, xo_ref, ho_ref):
    x = jnp.where(pl.program_id(1) == 0, ctx_ref[0], x_ref[0])
    xo_ref[0] = x
    ho_ref[0] = (_rms(x) * w_ref[...] * (1.0 + sc_ref[0]) + sh_ref[0]).astype(ho_ref.dtype)


def _norm_mod(ctx, x, w, mods, k_shift, k_scale):
    nb, n_lat, d = x.shape
    n_ctx = ctx.shape[1]
    assert n_ctx == TILE
    t = n_ctx + n_lat
    tok = lambda b, i: (b, i, 0)
    return pl.pallas_call(
        _norm_mod_kernel,
        out_shape=(jax.ShapeDtypeStruct((nb, t, d), F32), jax.ShapeDtypeStruct((nb, t, d), BF16)),
        grid=(nb, t // TILE),
        in_specs=[pl.BlockSpec((1, TILE, d), lambda b, i: (b, 0, 0)),
                  pl.BlockSpec((1, TILE, d), lambda b, i: (b, jnp.maximum(i - 1, 0), 0)),
                  pl.BlockSpec((1, d), lambda b, i: (0, 0)),
                  pl.BlockSpec((1, 1, d), _mod_row(nb, k_scale, n_ctx // TILE)),
                  pl.BlockSpec((1, 1, d), _mod_row(nb, k_shift, n_ctx // TILE))],
        out_specs=(pl.BlockSpec((1, TILE, d), tok), pl.BlockSpec((1, TILE, d), tok)),
        compiler_params=_cparams(("parallel", "arbitrary")),
        name="norm_mod",
    )(ctx, x, w.reshape(1, d), mods, mods)


MM_STAGE = 256


def _mm_body(a_ref, w_ref, o_ref):
    a = a_ref[...]
    width = o_ref.shape[1]
    stage = min(MM_STAGE, width)
    for c0 in range(0, width, stage):
        o_ref[:, c0:c0 + stage] = _dot(a, w_ref[:, c0:c0 + stage]).astype(o_ref.dtype)
        yield


def _row_tile(m):
    return 512 if m % 512 == 0 else TILE


def _mm_part(a, w, layer, tn, out_dtype, tile0, n_tiles):
    m, k = a.shape
    tm = _row_tile(m)
    return (_mm_body,
            [pl.BlockSpec((tm, k), lambda j, i: (i, 0)),
             pl.BlockSpec((None, k, tn), lambda j, i: (layer, 0, j + tile0))],
            [a, w], [jax.ShapeDtypeStruct((m, n_tiles * tn), out_dtype)],
            [pl.BlockSpec((tm, tn), lambda j, i: (i, j))], [])


CONV_K = 5
CONV_ROWS = 128
CONV_X = SSD_WIDTH // LANES
CONV_BC = 2 * SSD_BC // LANES
CONV_QK = 2 * ML_QK_WIDTH // LANES
CV_QK = SSD_WIDTH
CV_BC = SSD_WIDTH + 2 * ML_QK_WIDTH
CV_WIDTH = CV_BC + 2 * SSD_BC


def _conv_body(u_ref, w_ref, b_ref, s_ref, o_ref, pad_ref, *, n_ctx):
    t, c = u_ref.shape[1], u_ref.shape[2]
    half = CONV_K // 2
    zeros = jnp.zeros((SUBLANES, c), F32)
    w = w_ref[...]
    bias = b_ref[...]
    post = s_ref[...]
    for s0, n in ((0, n_ctx), (n_ctx, t - n_ctx)):
        pad_ref[0:SUBLANES, :] = zeros
        pad_ref[SUBLANES:SUBLANES + n, :] = u_ref[0, s0:s0 + n, :].astype(F32)
        pad_ref[SUBLANES + n:2 * SUBLANES + n, :] = zeros
        for r0 in range(0, n, CONV_ROWS):
            acc = bias
            for j in range(CONV_K):
                lo = SUBLANES - half + j + r0
                acc = acc + w[j:j + 1, :] * pad_ref[lo:lo + CONV_ROWS, :]
            o_ref[0, s0 + r0:s0 + r0 + CONV_ROWS, :] = (_silu(acc) * post).astype(o_ref.dtype)
            if (r0 // CONV_ROWS) % 4 == 3:
                yield


def _conv_part(u3, w, b, post_scale, n_ctx, grid):
    nb, t, width = u3.shape
    n_cb = width // LANES
    steps = grid[0] * grid[1]
    rep = steps // (nb * n_cb)
    assert steps == rep * nb * n_cb and n_cb == CONV_X + CONV_BC + CONV_QK
    blk = lambda j, i: (j * grid[1] + i) // rep
    chan = lambda j, i: blk(j, i) % n_cb
    out_chan = lambda c: jnp.where(c < CONV_X, c, jnp.where(c < CONV_X + CONV_BC, c + CONV_QK, c - CONV_BC))
    vec = lambda rows: pl.BlockSpec((rows, LANES), lambda j, i: (0, chan(j, i)))
    return (functools.partial(_conv_body, n_ctx=n_ctx),
            [pl.BlockSpec((1, t, LANES), lambda j, i: (blk(j, i) // n_cb, 0, chan(j, i))),
             vec(CONV_K), vec(1), vec(1)],
            [u3, w, b.reshape(1, width), post_scale.reshape(1, width)],
            [jax.ShapeDtypeStruct((nb, t, width), BF16)],
            [pl.BlockSpec((1, t, LANES), lambda j, i: (blk(j, i) // n_cb, 0, out_chan(chan(j, i))))],
            [pltpu.VMEM((t + 2 * SUBLANES, LANES), F32)])


def _tile_order(n_tiles, rev, ctx_tiles=1):
    if rev:
        return lambda i: jnp.where(i < ctx_tiles, ctx_tiles - 1 - i, n_tiles - 1 + ctx_tiles - i)
    return lambda i: i


SSD_PAIRS = SSD_HEADS // 2
SSD_B_OFF = SSD_HEADS
SSD_TILE = 256


def _ssd_gates(sm_ref, dtb_ref, alog_ref, n, both):
    lane = lax.broadcasted_iota(jnp.int32, (1, LANES), 1)
    dt = _softplus(sm_ref[0] + dtb_ref[...])
    la = dt * jnp.where(lane < 2 * SSD_HEADS, -jnp.exp(alog_ref[...]), 0.0)
    parts = _split(la, 2)
    upp = jnp.where(_causal(n, True), 1.0, 0.0).astype(BF16)
    cum = sum(_dot(upp, p) for p in parts)
    if both:
        low = jnp.where(_causal(n, False), 1.0, 0.0).astype(BF16)
        cum = jnp.where(lane < SSD_HEADS, sum(_dot(low, p) for p in parts), cum)
    return lane, dt, cum


def _expand(a, e):
    return _dot(a.astype(BF16), e)


def _group_dup(v, g, lo):
    other = pltpu.roll(v, SSD_STATE, axis=1)
    return jnp.where(lo, v, other) if g == 0 else jnp.where(lo, other, v)


def _ssd_state_step(st_ref, j, bw, xp, elast, off):
    r = lax.broadcasted_iota(jnp.int32, (LANES, LANES), 0) < SSD_STATE
    c = lax.broadcasted_iota(jnp.int32, (LANES, LANES), 1) < SSD_HEAD_DIM
    dec = jnp.where(r, elast[:, off + 2 * j:off + 2 * j + 1], elast[:, off + 2 * j + 1:off + 2 * j + 2])
    st_ref[j] = jnp.where(r == c, dec * st_ref[j] + _dot_tn(bw.astype(BF16), xp), 0.0)


def _interleave(*bodies):
    live = list(bodies)
    while live:
        for body in list(live):
            if next(body, StopIteration) is StopIteration:
                live.remove(body)


def _ssd_states_body(x_ref, bc_ref, sm_ref, dtb_ref, alog_ref, eb_ref, o_ref, st_ref):
    n = x_ref.shape[1]

    @pl.when(pl.program_id(1) == 0)
    def _init():
        st_ref[...] = jnp.zeros_like(st_ref)

    o_ref[0, 0] = st_ref[...].astype(o_ref.dtype)
    yield
    lane, dt, cum = _ssd_gates(sm_ref, dtb_ref, alog_ref, n, False)
    lo = lane < SSD_HEAD_DIM
    last = cum[0:1, :]
    yield
    wst = _expand(jnp.exp(last - cum) * dt, eb_ref[...])
    elast = jnp.exp(last)
    b128 = bc_ref[0, :, :SSD_BC].astype(F32)
    for j in range(SSD_PAIRS):
        sl = slice(LANES * j, LANES * (j + 1))
        bw = _group_dup(b128, j // (SSD_PAIRS // 2), lo) * wst[:, sl]
        _ssd_state_step(st_ref, j, bw, x_ref[0, :, sl], elast, SSD_B_OFF)
        if j % 2:
            yield


def _ssd_out_body(x_ref, bc_ref, sm_ref, dtb_ref, alog_ref, ef_ref, eb_ref, d_ref, sb_ref, o_ref, st_ref):
    n = x_ref.shape[1]

    @pl.when(pl.program_id(1) == 0)
    def _init():
        st_ref[...] = jnp.zeros_like(st_ref)

    zero_b = jnp.zeros((), BF16)
    lo = lax.broadcasted_iota(jnp.int32, (1, LANES), 1) < SSD_HEAD_DIM
    b128_b = bc_ref[0, :, :SSD_BC]
    c128_b = bc_ref[0, :, SSD_BC:]
    cb_all = [_dot_nt(jnp.where(lo if g == 0 else jnp.logical_not(lo), c128_b, zero_b), b128_b)
              for g in range(2)]
    pair_x = lambda j: x_ref[0, :, LANES * j:LANES * (j + 1)]
    rhs_all = [jnp.concatenate([jnp.where(lo, pair_x(j), zero_b), jnp.where(lo, zero_b, pair_x(j)),
                                st_ref[j].astype(BF16), sb_ref[0, 0, j]], axis=0) for j in range(SSD_PAIRS)]
    yield
    lane, dt, cum = _ssd_gates(sm_ref, dtb_ref, alog_ref, n, True)
    is_f = lane < SSD_HEADS
    ldt = jnp.log(dt)
    dsum = jnp.log(dt + pltpu.roll(dt, LANES - SSD_B_OFF, axis=1))
    rt = (jnp.where(lane < 2 * SSD_HEADS, cum - ldt, pltpu.roll(dsum, 2 * SSD_HEADS, axis=1)) * LOG2E).T
    cum2 = cum * LOG2E
    yield
    last = jnp.where(is_f, cum[n - 1:n, :], cum[0:1, :])
    elast = jnp.exp(last)
    ecum = jnp.exp(cum)
    ecum_f = _expand(ecum, ef_ref[...])
    ecum_b = _expand(ecum, eb_ref[...])
    wst = _expand(jnp.exp(last - cum) * dt, ef_ref[...])
    bc = bc_ref[0].astype(F32)
    b128, c128 = bc[:, :SSD_BC], bc[:, SSD_BC:]
    ti = lax.broadcasted_iota(jnp.int32, (n, n), 0)
    si = lax.broadcasted_iota(jnp.int32, (n, n), 1)
    below, above = si < ti, si > ti
    half = SSD_PAIRS // 2
    yield
    for g in range(2):
        cb = cb_all[g]
        cdup = _group_dup(c128, g, lo)
        for j in range(g * half, (g + 1) * half):
            ms = []
            for h in (2 * j, 2 * j + 1):
                e_f = cum2[:, h:h + 1] - rt[h:h + 1, :]
                e_b = cum2[:, SSD_B_OFF + h:SSD_B_OFF + h + 1] - rt[SSD_B_OFF + h:SSD_B_OFF + h + 1, :]
                e = jnp.where(below, e_f, jnp.where(above, e_b, rt[2 * SSD_HEADS + h:2 * SSD_HEADS + h + 1, :]))
                ms.append((cb * jnp.exp2(e)).astype(BF16))
            sl = slice(LANES * j, LANES * (j + 1))
            cs_f = (cdup * ecum_f[:, sl]).astype(BF16)
            cs_b = (cdup * ecum_b[:, sl]).astype(BF16)
            lhs = jnp.concatenate(ms + [cs_f, cs_b], axis=1)
            o_ref[0, :, sl] = _dot(lhs, rhs_all[j]) + d_ref[:, sl] * pair_x(j).astype(F32)
            if j % 2:
                yield
    for j in range(SSD_PAIRS):
        sl = slice(LANES * j, LANES * (j + 1))
        _ssd_state_step(st_ref, j, _group_dup(b128, j // half, lo) * wst[:, sl], x_ref[0, :, sl], elast, 0)
        if j % 2:
            yield


def _run_together(name, grid, *parts, phases=None):
    n_in = [len(p[1]) for p in parts]
    n_out = [len(p[3]) for p in parts]
    n_scr = [len(p[5]) for p in parts]

    def kern(*refs):
        ins, outs, scr = refs[:sum(n_in)], refs[sum(n_in):sum(n_in) + sum(n_out)], refs[sum(n_in) + sum(n_out):]

        def run(key):
            bodies = []
            for k, p in enumerate(parts):
                take = lambda seq, counts: seq[sum(counts[:k]):sum(counts[:k + 1])]
                body = p[0][key] if isinstance(p[0], dict) else p[0]
                bodies.append(body(*take(ins, n_in), *take(outs, n_out), *take(scr, n_scr)))
            _interleave(*bodies)

        if phases is None:
            run(None)
        else:
            for pred, key in phases:
                pl.when(pred(pl.program_id(1)))(functools.partial(run, key))

    res = pl.pallas_call(
        kern,
        out_shape=tuple(s for p in parts for s in p[3]),
        grid=grid,
        in_specs=[s for p in parts for s in p[1]],
        out_specs=tuple(s for p in parts for s in p[4]),
        scratch_shapes=[s for p in parts for s in p[5]],
        compiler_params=_cparams(("arbitrary", "arbitrary")),
        name=name,
    )(*[a for p in parts for a in p[2]])
    return [list(res[sum(n_out[:k]):sum(n_out[:k + 1])]) for k in range(len(parts))]


def _ssd_parts(cvs, sm3, dt_bias, a_log, d_e, n_ctx):
    nb, t, _ = cvs.shape
    tile = SSD_TILE
    nt = t // tile
    row = lambda v: jnp.pad(v.reshape(1, -1), ((0, 0), (0, LANES - 2 * SSD_HEADS)))
    const = lambda b, i: (0, 0)
    specs = lambda order: [
        pl.BlockSpec((1, tile, SSD_WIDTH), lambda b, i: (b, order(i), 0)),
        pl.BlockSpec((1, tile, 2 * SSD_BC), lambda b, i: (b, order(i), CV_BC // (2 * SSD_BC))),
        pl.BlockSpec((1, tile, LANES), lambda b, i: (b, order(i), 0)),
        pl.BlockSpec((1, LANES), const),
        pl.BlockSpec((1, LANES), const)]
    st_block = (1, 1, SSD_PAIRS, LANES, LANES)
    args = (cvs, cvs, sm3, row(dt_bias), row(a_log))
    sel = np.zeros((2, LANES, SSD_WIDTH), np.float32)
    for h in range(SSD_HEADS):
        sel[0, h, h * SSD_HEAD_DIM:(h + 1) * SSD_HEAD_DIM] = 1.0
        sel[1, SSD_B_OFF + h, h * SSD_HEAD_DIM:(h + 1) * SSD_HEAD_DIM] = 1.0
    e_f, e_b = jnp.asarray(sel[0], BF16), jnp.asarray(sel[1], BF16)
    e_spec = pl.BlockSpec((LANES, SSD_WIDTH), const)
    bwd = _tile_order(nt, True, n_ctx // tile)
    fwd = _tile_order(nt, False)
    scratch = [pltpu.VMEM(st_block[2:], F32)]
    b_spec = pl.BlockSpec((1, tile, SSD_BC), lambda b, i: (b, bwd(i), CV_BC // SSD_BC))
    states = (_ssd_states_body, [specs(bwd)[0], b_spec] + specs(bwd)[2:] + [e_spec], list(args) + [e_b],
              [jax.ShapeDtypeStruct((nb, nt) + st_block[2:], BF16)],
              [pl.BlockSpec(st_block, lambda b, i: (b, bwd(i), 0, 0, 0))], scratch)
    out = lambda states_b: (
        _ssd_out_body,
        specs(fwd) + [e_spec, e_spec, pl.BlockSpec((1, SSD_WIDTH), const),
                      pl.BlockSpec(st_block, lambda b, i: (b, i, 0, 0, 0))],
        list(args) + [e_f, e_b, d_e, states_b],
        [jax.ShapeDtypeStruct((nb, t, SSD_WIDTH), F32)],
        [pl.BlockSpec((1, tile, SSD_WIDTH), lambda b, i: (b, i, 0))], scratch)
    return states, out


ML_GATE = 32
ML_ND = 2 * ML_HEADS
ML_AUG = ML_V_DIM + LANES


def _ml_gates(sm_ref, ib_ref, fb_ref, n, both):
    lane = lax.broadcasted_iota(jnp.int32, (1, LANES), 1)
    valid = (lane >= ML_GATE) & (lane < ML_GATE + ML_ND)
    is_f = lane < ML_GATE + ML_HEADS
    sm = sm_ref[0]
    li = sm + ib_ref[...]
    lf = pltpu.roll(_log_sigmoid(sm + fb_ref[...]), LANES - ML_ND, axis=1)
    parts = _split(jnp.where(valid, lf, 0.0), 2)
    upp = jnp.where(_causal(n, True), 1.0, 0.0).astype(BF16)
    bcum = sum(_dot(upp, p) for p in parts)
    if both:
        low = jnp.where(_causal(n, False), 1.0, 0.0).astype(BF16)
        bcum = jnp.where(is_f, sum(_dot(low, p) for p in parts), bcum)
    return valid, is_f, bcum, jnp.where(valid, li - bcum, 0.0)


def _ml_state_step(cn_ref, h, cn, keep, k, ws_dense, v_aug):
    w3 = jnp.concatenate([ws_dense.astype(BF16)] * (ML_AUG // LANES), axis=1)
    cn_ref[h] = keep * cn + _dot_tn(k, w3 * v_aug)


def _ml_v_aug(v_ref, h, n):
    return jnp.concatenate([v_ref[0, :, ML_V_DIM * h:ML_V_DIM * (h + 1)], jnp.ones((n, LANES), BF16)], axis=1)


def _ml_states_body(k_ref, v_ref, sm_ref, ib_ref, fb_ref, sel_ref, cn_out, m_out, cn_ref, m_ref):
    n = k_ref.shape[1]

    @pl.when(pl.program_id(1) == 0)
    def _init():
        cn_ref[...] = jnp.zeros_like(cn_ref)
        m_ref[...] = jnp.zeros_like(m_ref)

    cn_out[0, 0] = cn_ref[...].astype(cn_out.dtype)
    m_out[0, 0] = m_ref[...]
    yield
    valid, is_f, bcum, a = _ml_gates(sm_ref, ib_ref, fb_ref, n, False)
    m_prev = m_ref[0:1, :]
    g_last = jnp.maximum(m_prev, jnp.max(a, axis=0, keepdims=True))
    yield
    ws = _expand(jnp.exp(a - g_last), sel_ref[:, ML_HEADS * LANES:])
    keep = jnp.exp(m_prev - g_last)
    for h in range(ML_HEADS):
        lane_b = ML_GATE + ML_HEADS + h
        k = k_ref[0, :, ML_QK_DIM * h:ML_QK_DIM * (h + 1)]
        _ml_state_step(cn_ref, h, cn_ref[h], keep[:, lane_b:lane_b + 1], k,
                       ws[:, LANES * h:LANES * (h + 1)], _ml_v_aug(v_ref, h, n))
        yield
    m_ref[...] = jnp.broadcast_to(bcum[0:1, :] + g_last, m_ref.shape)


def _ml_out_body(qk_ref, v_ref, sm_ref, ib_ref, fb_ref, sel_ref, cnb_ref, mb_ref, o_ref, cn_ref, m_ref):
    n = qk_ref.shape[1]

    @pl.when(pl.program_id(1) == 0)
    def _init():
        cn_ref[...] = jnp.zeros_like(cn_ref)
        m_ref[...] = jnp.zeros_like(m_ref)

    head_q = lambda h: qk_ref[0, :, ML_QK_DIM * h:ML_QK_DIM * (h + 1)]
    head_k = lambda h: qk_ref[0, :, ML_QK_WIDTH + ML_QK_DIM * h:ML_QK_WIDTH + ML_QK_DIM * (h + 1)]
    qk_all = [_dot_nt(head_q(h), head_k(h)) for h in range(ML_HEADS)]
    inter_all = [_dot(head_q(h), jnp.concatenate([cn_ref[h].astype(BF16), cnb_ref[0, 0, h]], axis=1))
                 for h in range(ML_HEADS)]
    yield
    valid, is_f, bcum, a = _ml_gates(sm_ref, ib_ref, fb_ref, n, True)
    m_prev = jnp.where(is_f, m_ref[0:1, :], mb_ref[0, 0, 0:1, :])
    a_t = a.T
    pre = suf = a_t[ML_GATE:ML_GATE + ML_ND, :]
    pos = lax.broadcasted_iota(jnp.int32, (ML_ND, n), 1)
    k = 1
    while k < n:
        pre = jnp.maximum(pre, jnp.where(pos >= k, pltpu.roll(pre, k, axis=1), -jnp.inf))
        suf = jnp.maximum(suf, jnp.where(pos < n - k, pltpu.roll(suf, n - k, axis=1), -jnp.inf))
        k *= 2
    run = jnp.where(lax.broadcasted_iota(jnp.int32, (ML_ND, n), 0) < ML_HEADS, pre, suf)
    run = jnp.concatenate([jnp.zeros((ML_GATE, n), F32), run,
                           jnp.zeros((LANES - ML_GATE - ML_ND, n), F32)], axis=0).T
    g = jnp.maximum(m_prev, run)
    m_t = bcum + g
    yield
    floor = jnp.exp(-m_t)
    dense = lambda x, lane_: jnp.broadcast_to(x[:, lane_:lane_ + 1], (n, LANES))
    diag_t = jnp.exp(a - g).T
    g_last = g[n - 1:n, :]
    ws = _expand(jnp.exp(a - g_last), sel_ref[:, :ML_HEADS * LANES])
    keep = jnp.exp(m_prev - g_last)
    ti = lax.broadcasted_iota(jnp.int32, (n, n), 0)
    si = lax.broadcasted_iota(jnp.int32, (n, n), 1)
    not_above, above, on_diag = si <= ti, si > ti, si == ti
    wide = lambda x: jnp.concatenate([x] * (n // LANES), axis=1)
    for h in range(ML_HEADS):
        lanes = (ML_GATE + h, ML_GATE + ML_HEADS + h)
        g_d = [dense(g, ln) for ln in lanes]
        qk, inter = qk_all[h], inter_all[h]
        e = jnp.where(not_above, a_t[lanes[0]:lanes[0] + 1, :] - wide(g_d[0]),
                      a_t[lanes[1]:lanes[1] + 1, :] - wide(g_d[1]))
        p = qk * jnp.exp(e)
        p_f = jnp.where(not_above, p, 0.0).astype(BF16)
        p_b = jnp.where(above, p, jnp.where(on_diag, qk * diag_t[lanes[1]:lanes[1] + 1, :], 0.0)).astype(BF16)
        intra = _dot(jnp.concatenate([p_f, p_b], axis=0), _ml_v_aug(v_ref, h, n))
        out = None
        for d in range(2):
            w_inter = jnp.exp(m_prev[:, lanes[d]:lanes[d] + 1] - g_d[d])
            s = (intra[n * d:n * (d + 1)] + jnp.concatenate([w_inter] * (ML_AUG // LANES), axis=1)
                 * inter[:, ML_AUG * d:ML_AUG * (d + 1)])
            rn = 1.0 / jnp.maximum(jnp.abs(s[:, ML_V_DIM:]), dense(floor, lanes[d]))
            hid = s[:, :ML_V_DIM] * jnp.concatenate([rn] * (ML_V_DIM // LANES), axis=1)
            out = hid if out is None else out + hid
        o_ref[0, :, ML_V_DIM * h:ML_V_DIM * (h + 1)] = out
        yield
    for h in range(ML_HEADS):
        lane_f = ML_GATE + h
        _ml_state_step(cn_ref, h, cn_ref[h], keep[:, lane_f:lane_f + 1], head_k(h),
                       ws[:, LANES * h:LANES * (h + 1)], _ml_v_aug(v_ref, h, n))
        yield
    m_ref[...] = jnp.broadcast_to(m_t[n - 1:n, :], m_ref.shape)


def _mlstm_parts(cvm, p3, sm3, i_bias, f_bias, n_ctx):
    nb, t, _ = cvm.shape
    nt = t // TILE
    assert _SM_OFF["i_f"] == ML_GATE and _SM_OFF["f_f"] == ML_GATE + ML_ND
    row = lambda v, off: jnp.pad(v.reshape(1, -1), ((0, 0), (off, LANES - off - ML_ND)))
    sel = np.zeros((LANES, ML_ND * LANES), np.float32)
    for r in range(ML_ND):
        sel[ML_GATE + r, r * LANES:(r + 1) * LANES] = 1.0
    const = lambda b, i: (0, 0)
    specs = lambda order: [
        pl.BlockSpec((1, TILE, 2 * ML_QK_WIDTH), lambda b, i: (b, order(i), CV_QK // (2 * ML_QK_WIDTH))),
        pl.BlockSpec((1, TILE, ML_V_WIDTH), lambda b, i: (b, order(i), _P_OFF["m_v"] // ML_V_WIDTH)),
        pl.BlockSpec((1, TILE, LANES), lambda b, i: (b, order(i), 0)),
        pl.BlockSpec((1, LANES), const),
        pl.BlockSpec((1, LANES), const),
        pl.BlockSpec((LANES, ML_ND * LANES), const)]
    args = (cvm, p3, sm3, row(i_bias, ML_GATE), row(f_bias, ML_GATE + ML_ND), jnp.asarray(sel, BF16))
    cn_block = (1, 1, ML_HEADS, ML_QK_DIM, ML_AUG)
    m_block = (1, 1, SUBLANES, LANES)
    scratch = [pltpu.VMEM(cn_block[2:], F32), pltpu.VMEM(m_block[2:], F32)]
    bwd = _tile_order(nt, True, n_ctx // TILE)
    fwd = _tile_order(nt, False)
    st_idx = lambda b, i: (b, bwd(i)) + (0,) * 3
    k_spec = pl.BlockSpec((1, TILE, ML_QK_WIDTH), lambda b, i: (b, bwd(i), (CV_QK + ML_QK_WIDTH) // ML_QK_WIDTH))
    states = (_ml_states_body, [k_spec] + specs(bwd)[1:], list(args),
              [jax.ShapeDtypeStruct((nb, nt) + cn_block[2:], BF16),
               jax.ShapeDtypeStruct((nb, nt) + m_block[2:], F32)],
              [pl.BlockSpec(cn_block, st_idx), pl.BlockSpec(m_block, lambda b, i: (b, bwd(i), 0, 0))],
              scratch)
    out = lambda cn_b, m_b: (
        _ml_out_body,
        specs(fwd) + [pl.BlockSpec(cn_block, lambda b, i: (b, i, 0, 0, 0)),
                      pl.BlockSpec(m_block, lambda b, i: (b, i, 0, 0))],
        list(args) + [cn_b, m_b],
        [jax.ShapeDtypeStruct((nb, t, ML_V_WIDTH), F32)],
        [pl.BlockSpec((1, TILE, ML_V_WIDTH), lambda b, i: (b, i, 0))], scratch)
    return states, out


GLA_SUB = 256
GLA_NCH = GLA_SUB // GLA_CHUNK
GLA_COLS = BF16_SUBLANES


def _gla_layout(is_ctx, lat_rows):
    r = np.arange(GLA_SUB)
    if is_ctx:
        return r // GLA_CHUNK, r % GLA_CHUNK
    col = r % SUBLANES
    cpc = GLA_CHUNK // lat_rows
    return col // cpc, (col % cpc) * lat_rows + r // SUBLANES


def _gla_consts(is_ctx, lat_rows, rev):
    ch, pos = _gla_layout(is_ctx, lat_rows)
    same = ch[:, None] == ch[None, :]
    before = (pos[None, :] >= pos[:, None]) if rev else (pos[None, :] <= pos[:, None])
    tri = (same & before).astype(np.float32)
    cmask = np.stack([np.repeat((ch == j)[:, None], LANES, axis=1) for j in range(GLA_NCH)])
    return tri, cmask.astype(np.float32)


def _gla_row(is_ctx, lat_rows, j, p):
    ch, pos = _gla_layout(is_ctx, lat_rows)
    return int(np.nonzero((ch == j) & (pos == p))[0][0])


def _per_chunk_rows(b, is_ctx, lat_rows, p):
    rows = [b[_gla_row(is_ctx, lat_rows, j, p):_gla_row(is_ctx, lat_rows, j, p) + 1, :]
            for j in range(GLA_NCH)]
    w = b.shape[1]
    if is_ctx:
        full = jnp.concatenate([jnp.broadcast_to(r, (GLA_CHUNK, w)) for r in rows], axis=0)
    else:
        rep = SUBLANES // GLA_NCH
        pat = jnp.concatenate([jnp.broadcast_to(r, (rep, w)) for r in rows], axis=0)
        full = jnp.broadcast_to(pat[None], (GLA_SUB // SUBLANES, SUBLANES, w)).reshape(GLA_SUB, w)
    return rows, full


def _gla_sub(q, k, v, araw, aup, abias, tri_b, tri_f, cmask_ref, st_ref, store, *, rev, is_ctx, lat_rows):
    want_out = store is not None
    x = _dot(araw.astype(BF16), aup) + abias
    g = _log_sigmoid(x) * (1.0 / GLA_TAU)
    b = _dot_exact_lhs(tri_b, g, 2)
    yield
    lasts, last = _per_chunk_rows(b, is_ctx, lat_rows, 0 if rev else GLA_CHUNK - 1)
    kl = (k * jnp.exp(last - b)).astype(BF16)
    if want_out:
        _, ref = _per_chunk_rows(b, is_ctx, lat_rows, GLA_CHUNK // 2)
        qs = q * (GLA_K_DIM ** -0.5)
        qe = (qs * jnp.exp(b - ref)).astype(BF16)
        ke = (k * jnp.exp(ref - b)).astype(BF16)
        qb = (qs * jnp.exp(b)).astype(BF16)
        visible = tri_f > 0.0
    yield
    order = range(GLA_NCH - 1, -1, -1) if rev else range(GLA_NCH)
    outs = []
    for h in range(GLA_HEADS):
        ks = slice(GLA_K_DIM * h, GLA_K_DIM * (h + 1))
        vh = v[:, GLA_V_DIM * h:GLA_V_DIM * (h + 1)]
        klm = jnp.concatenate([kl[:, ks] * cmask_ref[j] for j in range(GLA_NCH)], axis=1)
        upd = _dot_tn(vh, klm)
        s = st_ref[h]
        s_in = [None] * GLA_NCH
        for j in order:
            s_in[j] = s.astype(BF16)
            s = s * jnp.exp(lasts[j][:, ks]) + upd[:, GLA_K_DIM * j:GLA_K_DIM * (j + 1)]
        st_ref[h] = s
        if want_out:
            att = jnp.where(visible, _dot_nt(qe[:, ks], ke[:, ks]), 0.0).astype(BF16)
            qbm = jnp.concatenate([qb[:, ks] * cmask_ref[j] for j in range(GLA_NCH)], axis=1)
            outs.append(_dot(att, vh) + _dot_nt(qbm, jnp.concatenate(s_in, axis=1)))
        yield
    if want_out:
        store(jnp.concatenate(outs, axis=1))


def _gla_ctx_body(qc_ref, kc_ref, vc_ref, sc_ref, ql_ref, kl_ref, vl_ref, sl_ref,
                  aup_ref, ab_ref, tcb_ref, tcf_ref, cmc_ref, tlb_ref, tlf_ref, cml_ref,
                  o_ref, st_ref, ctxo_ref, *, rev, ctx_rows, lat_rows, n_cblk, ctx_out):
    st_ref[...] = jnp.zeros_like(st_ref)

    def store(o):
        ctxo_ref[...] = o

    yield from _gla_sub(qc_ref[0].astype(F32), kc_ref[0].astype(F32), vc_ref[0], sc_ref[0],
                        aup_ref[...], ab_ref[...], tcb_ref[...], tcf_ref[...], cmc_ref, st_ref,
                        store if ctx_out else None, rev=rev, is_ctx=True, lat_rows=lat_rows)


def _gla_lat_body(qc_ref, kc_ref, vc_ref, sc_ref, ql_ref, kl_ref, vl_ref, sl_ref,
                  aup_ref, ab_ref, tcb_ref, tcf_ref, cmc_ref, tlb_ref, tlf_ref, cml_ref,
                  o_ref, st_ref, ctxo_ref, *, rev, ctx_rows, lat_rows, n_cblk, ctx_out):
    i = pl.program_id(1)
    r0, r1 = ctx_rows, ctx_rows + lat_rows
    cblk = (n_cblk - i) if rev else (i - 1)
    halves = range(GLA_COLS // SUBLANES)
    for half in (reversed(halves) if rev else halves):
        cs = slice(SUBLANES * half, SUBLANES * (half + 1))
        take = lambda r: r[0, r0:r1].astype(F32)[:, cs, :].reshape(GLA_SUB, r.shape[-1])

        def store(o, cs=cs):
            o_ref[0, r0:r1, cs, :] = o.reshape(lat_rows, SUBLANES, GLA_V_WIDTH)

        yield from _gla_sub(take(ql_ref), take(kl_ref), take(vl_ref).astype(BF16), take(sl_ref),
                            aup_ref[...], ab_ref[...], tlb_ref[...], tlf_ref[...], cml_ref, st_ref,
                            store, rev=rev, is_ctx=False, lat_rows=lat_rows)
    for r in range(ctx_rows):
        if ctx_out:
            start = pl.multiple_of(r * GRID_W + cblk * GLA_COLS, GLA_COLS)
            o_ref[0, r, :, :] = ctxo_ref[pl.ds(start, GLA_COLS), :]
        else:
            o_ref[0, r, :, :] = jnp.zeros((GLA_COLS, GLA_V_WIDTH), F32)


GLA_PHASES = ((lambda i: i == 0, "ctx"), (lambda i: i > 0, "lat"))


def _gla_part(p3, sm3, a_up, a_bias, n_ctx, rev, ctx_out):
    nb, t, ncol = p3.shape
    rows = t // GRID_W
    ctx_rows = n_ctx // GRID_W
    lat_rows = rows - ctx_rows
    n_cblk = GRID_W // GLA_COLS
    p4 = p3.reshape(nb, rows, GRID_W, ncol)
    sm4 = sm3.reshape(nb, rows, GRID_W, LANES)
    a_off = _SM_OFF["a_b"] if rev else _SM_OFF["a_f"]
    aup = jnp.pad(a_up, ((a_off, LANES - a_off - GLA_RANK), (0, 0))).astype(BF16)
    cblk = lambda i: jnp.where(i == 0, n_cblk - 1 if rev else 0, (n_cblk - i) if rev else (i - 1))
    ctx = lambda blk: (lambda b, i: (b, 0, blk))
    lat = lambda blk: (lambda b, i: (b, 0, cblk(i), blk))
    const2 = lambda b, i: (0, 0)
    const3 = lambda b, i: (0, 0, 0)
    widths = (GLA_K_WIDTH, GLA_K_WIDTH, GLA_V_WIDTH)
    offs = (_P_OFF["g_q"], _P_OFF["g_k"], _P_OFF["g_v"])
    consts = []
    const_specs = []
    for is_ctx in (True, False):
        tri, cmask = _gla_consts(is_ctx, lat_rows, rev)
        consts += [jnp.asarray(tri, BF16), jnp.asarray(tri, F32), jnp.asarray(cmask, BF16)]
        const_specs += [pl.BlockSpec((GLA_SUB, GLA_SUB), const2), pl.BlockSpec((GLA_SUB, GLA_SUB), const2),
                        pl.BlockSpec((GLA_NCH, GLA_SUB, LANES), const3)]
    out_spec = pl.BlockSpec((1, rows, GLA_COLS, GLA_V_WIDTH), lat(0))
    in_specs = ([pl.BlockSpec((1, n_ctx, w), ctx(o // w)) for w, o in zip(widths, offs)]
                + [pl.BlockSpec((1, n_ctx, LANES), ctx(0))]
                + [pl.BlockSpec((1, rows, GLA_COLS, w), lat(o // w)) for w, o in zip(widths, offs)]
                + [pl.BlockSpec((1, rows, GLA_COLS, LANES), lat(0))]
                + [pl.BlockSpec((LANES, GLA_K_WIDTH), const2), pl.BlockSpec((1, GLA_K_WIDTH), const2)]
                + const_specs)
    args = [p3, p3, p3, sm3, p4, p4, p4, sm4, aup, a_bias.reshape(1, GLA_K_WIDTH)] + consts
    static = dict(rev=rev, ctx_rows=ctx_rows, lat_rows=lat_rows, n_cblk=n_cblk, ctx_out=ctx_out)
    bodies = {"ctx": functools.partial(_gla_ctx_body, **static),
              "lat": functools.partial(_gla_lat_body, **static)}
    return (bodies, in_specs, args,
            [jax.ShapeDtypeStruct((nb, rows, GRID_W, GLA_V_WIDTH), F32)], [out_spec],
            [pltpu.VMEM((GLA_HEADS, GLA_V_DIM, GLA_K_DIM), F32), pltpu.VMEM((n_ctx, GLA_V_WIDTH), F32)])


def _group_rmsnorm(y, groups):
    width = y.shape[-1] // groups
    ones = jnp.ones((width, LANES), BF16)
    out = []
    for g in range(groups):
        yg = y[:, width * g:width * (g + 1)]
        ms = _dot((yg * yg).astype(BF16), ones) * (1.0 / width)
        out.append(yg * jnp.concatenate([lax.rsqrt(ms + EPS)] * (width // LANES), axis=1))
    return jnp.concatenate(out, axis=1)


def _cast_weights_once(pairs):
    @pl.when((pl.program_id(0) == 0) & (pl.program_id(1) == 0))
    def _cast():
        for w_ref, s_ref in pairs:
            s_ref[...] = w_ref[...].astype(s_ref.dtype)


def _post_kernel(x_ref, y_ref, h_ref, of_ref, ob_ref, z_ref, mo_ref, gg_ref, gs_ref, gm_ref, gl_ref,
                 nws_ref, nwm_ref, nwg_ref, wbs32_ref, wbm32_ref, wbg32_ref, wout32_ref, g1_ref,
                 nwf_ref, sc2_ref, sh2_ref, xo_ref, ho_ref, wbs_ref, wbm_ref, wbg_ref, wout_ref):
    _cast_weights_once(((wbs32_ref, wbs_ref), (wbm32_ref, wbm_ref), (wbg32_ref, wbg_ref), (wout32_ref, wout_ref)))
    y_ssd = (_group_rmsnorm(y_ref[0] * _silu(z_ref[0]), 2) * nws_ref[...]).astype(BF16)
    y_ml = (_group_rmsnorm(h_ref[0], ML_HEADS) * nwm_ref[...]).astype(BF16) * _sigmoid(mo_ref[0])
    y_gla = (_group_rmsnorm(of_ref[0] + ob_ref[0], GLA_HEADS) * nwg_ref[...]).astype(BF16) * _silu(gg_ref[0])
    merged = (_sigmoid(gs_ref[0]) * _dot(y_ssd, wbs_ref[...])
              + _sigmoid(gm_ref[0]) * _dot(y_ml, wbm_ref[...])
              + _sigmoid(gl_ref[0]) * _dot(y_gla, wbg_ref[...]))
    x_new = x_ref[0] + g1_ref[0] * _dot(merged.astype(BF16), wout_ref[...])
    xo_ref[0] = x_new
    ho_ref[0] = (_rms(x_new) * nwf_ref[...] * (1.0 + sc2_ref[0]) + sh2_ref[0]).astype(ho_ref.dtype)


def _post(x, scans, p3, norm_ws, w_bs, w_out, layer, norm_ffn_w, mods, n_ctx, tile0):
    nb, t, d = x.shape
    nt = t // TILE - tile0
    ctx_tiles = n_ctx // TILE
    tok = lambda blk: (lambda b, i: (b, i + tile0, blk))
    out = lambda b, i: (b, i, 0)
    const = lambda b, i: (0, 0)
    tok_spec = lambda blk: pl.BlockSpec((1, TILE, d), tok(blk))
    w_spec = pl.BlockSpec((None, d, d), lambda b, i: (layer, 0, 0), pipeline_mode=pl.Buffered(1))
    vec = pl.BlockSpec((1, d), const)
    mod = lambda k: pl.BlockSpec((1, 1, d), _mod_row(nb, k, ctx_tiles, tile0))
    names = ("s_z", "m_o", "g_g", "gate_ssd", "gate_ml", "gate_gla")
    in_specs = ([tok_spec(0)] * (1 + len(scans)) + [tok_spec(_P_OFF[nm] // d) for nm in names]
                + [vec] * 3 + [w_spec] * 4 + [mod(2), vec, mod(4), mod(3)])
    return pl.pallas_call(
        _post_kernel,
        out_shape=(jax.ShapeDtypeStruct((nb, nt * TILE, d), F32),
                   jax.ShapeDtypeStruct((nb, nt * TILE, d), BF16)),
        grid=(nb, nt),
        in_specs=in_specs,
        out_specs=(pl.BlockSpec((1, TILE, d), out), pl.BlockSpec((1, TILE, d), out)),
        scratch_shapes=[pltpu.VMEM((d, d), BF16)] * 4,
        compiler_params=_cparams(("arbitrary", "arbitrary")),
        name="post",
    )(x, *scans, *([p3] * 6), *[w.reshape(1, d) for w in norm_ws], *w_bs, w_out, mods,
      norm_ffn_w.reshape(1, d), mods, mods)


def _ffn_in_kernel(a_ref, wg_ref, wu_ref, o_ref, w_ref):
    half = o_ref.shape[1]

    @pl.when(pl.program_id(1) == 0)
    def _cast():
        w_ref[:, :half] = wg_ref[...].astype(w_ref.dtype)
        w_ref[:, half:] = wu_ref[...].astype(w_ref.dtype)

    acc = _dot(a_ref[...], w_ref[...])
    o_ref[...] = (_silu(acc[:, :half]) * acc[:, half:]).astype(o_ref.dtype)


def _ffn_in(h, w_ffn_in, layer, half):
    m, k = h.shape
    n_half = w_ffn_in.shape[-1] // 2
    tm = _row_tile(m)
    return pl.pallas_call(
        _ffn_in_kernel,
        out_shape=jax.ShapeDtypeStruct((m, n_half), BF16),
        grid=(n_half // half, m // tm),
        in_specs=[pl.BlockSpec((tm, k), lambda j, i: (i, 0)),
                  pl.BlockSpec((None, k, half), lambda j, i: (layer, 0, j)),
                  pl.BlockSpec((None, k, half), lambda j, i: (layer, 0, j + n_half // half))],
        out_specs=pl.BlockSpec((tm, half), lambda j, i: (i, j)),
        scratch_shapes=[pltpu.VMEM((k, 2 * half), BF16)],
        compiler_params=_cparams(("arbitrary", "arbitrary")),
        name="ffn_in",
    )(h, w_ffn_in, w_ffn_in)


def _ffn_out_kernel(a_ref, w32_ref, x_ref, g_ref, nw_ref, sc_ref, sh_ref, xo_ref, ho_ref, w_ref):
    _cast_weights_once(((w32_ref, w_ref),))
    x_new = x_ref[0] + g_ref[0] * _dot(a_ref[0], w_ref[...])
    xo_ref[0] = x_new
    ho_ref[0] = (_rms(x_new) * nw_ref[...] * (1.0 + sc_ref[0]) + sh_ref[0]).astype(ho_ref.dtype)


def _ffn_out_last_kernel(a_ref, w32_ref, x_ref, g_ref, nw_ref, o_ref, w_ref):
    _cast_weights_once(((w32_ref, w_ref),))
    x_new = x_ref[0] + g_ref[0] * _dot(a_ref[0], w_ref[...])
    o_ref[0] = _rms(x_new) * nw_ref[...]


def _ffn_out(a, w, layer, x, mods, n_ctx, tile0, next_norm_w, next_mods):
    nb, t, d = x.shape
    k = a.shape[-1]
    ctx_tiles = n_ctx // TILE
    tok = lambda b, i: (b, i, 0)
    mod = lambda k_: pl.BlockSpec((1, 1, d), _mod_row(nb, k_, ctx_tiles, tile0))
    in_specs = [pl.BlockSpec((1, TILE, k), tok),
                pl.BlockSpec((None, k, d), lambda b, i: (layer, 0, 0), pipeline_mode=pl.Buffered(1)),
                pl.BlockSpec((1, TILE, d), tok),
                mod(5),
                pl.BlockSpec((1, d), lambda b, i: (0, 0))]
    args = [a, w, x, mods, next_norm_w.reshape(1, d)]
    if next_mods is None:
        body = _ffn_out_last_kernel
        out_shape = jax.ShapeDtypeStruct((nb, t, d), F32)
        out_specs = pl.BlockSpec((1, TILE, d), tok)
    else:
        body = _ffn_out_kernel
        in_specs += [mod(1), mod(0)]
        args += [next_mods, next_mods]
        out_shape = (jax.ShapeDtypeStruct((nb, t, d), F32), jax.ShapeDtypeStruct((nb, t, d), BF16))
        out_specs = (pl.BlockSpec((1, TILE, d), tok), pl.BlockSpec((1, TILE, d), tok))
    return pl.pallas_call(
        body,
        out_shape=out_shape,
        grid=(nb, t // TILE),
        in_specs=in_specs,
        out_specs=out_specs,
        scratch_shapes=[pltpu.VMEM((k, d), BF16)],
        compiler_params=_cparams(("arbitrary", "arbitrary")),
        name="ffn_out",
    )(*args)


def _proj_weights(w_in):
    cols = lambda names: [w_in[..., _IN_OFF[nm]:_IN_OFF[nm] + _IN_W[nm]] for nm in names]
    main = jnp.concatenate(cols(_P_ORDER), axis=-1)
    small = jnp.pad(jnp.concatenate(cols(_SMALL), axis=-1), ((0, 0), (0, 0), (0, LANES - N_SMALL_USED)))
    return main.astype(BF16), small.astype(BF16)


def kernel(x, c, ctx, c_ctx, w_mod, b_mod, norm_mix_w, norm_ffn_w, w_in, ssd_conv_w, ssd_conv_b, ssd_dt_bias, ssd_a_log, ssd_d, ssd_norm_w, ml_conv_w, ml_conv_b, ml_i_bias, ml_f_bias, ml_norm_w, gla_a_up, gla_a_bias, gla_norm_w, w_b_ssd, w_b_ml, w_b_gla, w_out, w_ffn_in, w_ffn_out, final_norm_w):
    nb, n_lat, d = x.shape
    n_ctx = ctx.shape[1]
    t = n_ctx + n_lat
    depth = w_in.shape[0]
    d_ff = w_ffn_out.shape[1]
    assert n_ctx == TILE == GLA_SUB and n_lat % TILE == 0 and n_lat // GRID_W == 32
    ffn_half = d_ff // 2

    c16 = jnp.pad(jnp.concatenate([c, c_ctx[None]], axis=0), ((0, 2 * SUBLANES - nb - 1), (0, 0)))
    mods = [_modulation(c16, w_mod, b_mod, l).reshape(2 * SUBLANES * 6, 1, d) for l in range(depth)]
    xs, h = _norm_mod(ctx, x, norm_mix_w[0], mods[0], 0, 1)
    conv_w = jnp.concatenate([ssd_conv_w, ml_conv_w], axis=-1)
    conv_b = jnp.concatenate([ssd_conv_b, ml_conv_b], axis=-1)
    conv_post = jnp.concatenate([jnp.ones((ssd_conv_w.shape[-1] + ML_QK_WIDTH,), F32),
                                 jnp.full((ML_QK_WIDTH,), ML_QK_DIM ** -0.5, F32)])
    w_main, w_small = _proj_weights(w_in)
    w_bs = (w_b_ssd, w_b_ml, w_b_gla)
    for l in range(depth):
        last = l == depth - 1
        h2d = h.reshape(nb * t, d)
        tn = N_PROJ // 5
        conv_tile = _P_OFF["s_x"] // tn
        assert conv_tile * tn == _P_OFF["s_x"] and N_PROJ - _P_OFF["s_x"] == tn == CV_WIDTH
        m_tiles = nb * t // _row_tile(nb * t)
        (pc,), (sm,) = _run_together("proj_first", (1, m_tiles),
                                     _mm_part(h2d, w_main, l, tn, BF16, conv_tile, 1),
                                     _mm_part(h2d, w_small, l, LANES, F32, 0, 1))
        pc3, sm3 = pc.reshape(nb, t, tn), sm.reshape(nb, t, LANES)
        grid = (conv_tile, m_tiles)
        (p,), (cv,) = _run_together(
            "proj_conv", grid, _mm_part(h2d, w_main, l, tn, BF16, 0, conv_tile),
            _conv_part(pc3, conv_w[l], conv_b[l], conv_post, n_ctx, grid))
        p3 = p.reshape(nb, t, conv_tile * tn)
        d_e = jnp.repeat(ssd_d[l], SSD_HEAD_DIM).reshape(1, SSD_WIDTH)
        ssd_states, ssd_out = _ssd_parts(cv, sm3, ssd_dt_bias[l], ssd_a_log[l], d_e, n_ctx)
        ml_states, ml_out = _mlstm_parts(cv, p3, sm3, ml_i_bias[l], ml_f_bias[l], n_ctx)
        grid = (nb, t // TILE)
        ssd_st, ml_st = _run_together("bwd_states", grid, ssd_states, ml_states)
        (y,), (hm,) = _run_together("ssd_mlstm", grid, ssd_out(*ssd_st), ml_out(*ml_st))
        gla = [_gla_part(p3, sm3, gla_a_up[l, k], gla_a_bias[l, k], n_ctx, bool(k), not last) for k in range(2)]
        (og_f,), (og_b,) = _run_together("gla", (nb, GRID_W // GLA_COLS + 1), *gla, phases=GLA_PHASES)
        og_f, og_b = og_f.reshape(nb, t, GLA_V_WIDTH), og_b.reshape(nb, t, GLA_V_WIDTH)
        tile0 = n_ctx // TILE if last else 0
        xs, h2 = _post(xs, (y, hm, og_f, og_b), p3, (ssd_norm_w[l], ml_norm_w[l], gla_norm_w[l]),
                       w_bs, w_out, l, norm_ffn_w[l], mods[l], n_ctx, tile0)
        nt = xs.shape[1]
        a = _ffn_in(h2.reshape(nb * nt, d), w_ffn_in, l, ffn_half)
        a = a.reshape(nb, nt, d_ff)
        if last:
            return _ffn_out(a, w_ffn_out, l, xs, mods[l], n_ctx, tile0, final_norm_w, None)
        xs, h = _ffn_out(a, w_ffn_out, l, xs, mods[l], n_ctx, tile0,
                         norm_mix_w[l + 1], mods[l + 1])
```
